```python
import jax, jax.numpy as jnp
from jax import lax
import numpy as np

D_MODEL = 1024
BATCH = 2
SEQ = 8192
DEPTH = 1
DEC_BATCH = 32
DEC_SEQ = 1
PAST_LEN = 8192
PAGE_SIZE = 128

NSA_HEADS = 8
NSA_KV_HEADS = 2
NSA_GROUP = NSA_HEADS // NSA_KV_HEADS
HEAD_DIM = 64
NSA_WIDTH = NSA_HEADS * HEAD_DIM
KV_WIDTH = NSA_KV_HEADS * HEAD_DIM
CMP_BLOCK = 32
CMP_STRIDE = 16
CMP_RATIO = CMP_BLOCK // CMP_STRIDE
CMP_HIDDEN = 128
SEL_BLOCK = 64
SEL_RATIO = SEL_BLOCK // CMP_STRIDE
N_SEL = 16
WINDOW = 512
Q_BLOCK = 128
HG_HEADS = 4
HG_DIM = 128
HG_WIDTH = HG_HEADS * HG_DIM
HG_CHUNK = 64
MIX_WIDTH = NSA_WIDTH + HG_WIDTH
PLE_DIM = 256
EPS = 1e-6
NEG = -1e30
FORCE_BONUS = 1e6

IN_SPLITS = (('q_a', NSA_WIDTH), ('k_cmp', KV_WIDTH), ('v_cmp', KV_WIDTH), ('k_slc', KV_WIDTH), ('v_slc', KV_WIDTH),
             ('k_win', KV_WIDTH), ('v_win', KV_WIDTH), ('gate_a', 3 * NSA_HEADS), ('z_a', NSA_WIDTH),
             ('q_b', HG_WIDTH), ('f_b', HG_WIDTH), ('i_b', HG_WIDTH), ('z_b', HG_WIDTH))
IN_WIDTH = 2 * NSA_WIDTH + 6 * KV_WIDTH + 3 * NSA_HEADS + 4 * HG_WIDTH

kernel_name = 'hymba_nsa_hgrn2_decode_step'


def _rmsnorm(x, g):
    x32 = x.astype(jnp.float32)
    y = x32 * lax.rsqrt(jnp.mean(x32 * x32, axis=-1, keepdims=True) + EPS)
    return (y * g.astype(jnp.float32)).astype(x.dtype)


def _masked_softmax(s, mask):
    s = jnp.where(mask, s, NEG)
    m = jnp.max(s, axis=-1, keepdims=True)
    e = jnp.where(mask, jnp.exp(s - m), 0.0)
    return e / jnp.maximum(jnp.sum(e, axis=-1, keepdims=True), 1e-30)


def _alibi_slopes():
    h = jnp.arange(1, NSA_HEADS + 1, dtype=jnp.float32)
    return jnp.exp2(-8.0 * h / NSA_HEADS)


def _split(z):
    out, off = {}, 0
    for name, width in IN_SPLITS:
        out[name] = z[..., off:off + width]
        off += width
    return out


def _project(x, g_pre, w_in):
    B, L, _ = x.shape
    z = _split(_rmsnorm(x, g_pre) @ w_in)
    kv = {n: z[n].reshape(B, L, NSA_KV_HEADS, HEAD_DIM)
          for n in ('k_cmp', 'v_cmp', 'k_slc', 'v_slc', 'k_win', 'v_win')}
    gates = jax.nn.sigmoid(z['gate_a'].astype(jnp.float32)).reshape(B, L, NSA_HEADS, 3)
    return z, kv, gates


def _compress(x, pe, w1, b1, w2):
    B, L = x.shape[:2]
    nseg = L // CMP_STRIDE
    nc = nseg - CMP_RATIO + 1
    seg = x[:, :nseg * CMP_STRIDE].reshape(B, nseg, CMP_STRIDE, NSA_KV_HEADS, HEAD_DIM)
    blocks = jnp.concatenate([seg[:, j:j + nc] for j in range(CMP_RATIO)], axis=2)
    blocks = blocks + pe[None, None, :, None, :]
    flat = blocks.transpose(0, 1, 3, 2, 4).reshape(B, nc, NSA_KV_HEADS, CMP_BLOCK * HEAD_DIM)
    return jax.nn.silu(flat @ w1 + b1) @ w2


def _cmp_end(nc):
    return jnp.arange(nc) * CMP_STRIDE + CMP_BLOCK - 1


def _sel_blocks(x):
    B, L = x.shape[:2]
    ns = -(-L // SEL_BLOCK)
    x = jnp.pad(x, ((0, 0), (0, ns * SEL_BLOCK - L), (0, 0), (0, 0)))
    return x.reshape(B, ns, SEL_BLOCK, NSA_KV_HEADS, HEAD_DIM)


def _nsa_attend(q, gates, t, k_c, v_c, c_end, kb_s, vb_s, k_w, v_w, w_pos):
    f32 = jnp.float32
    B, Q = q.shape[:2]
    slopes = _alibi_slopes().reshape(NSA_KV_HEADS, NSA_GROUP)
    qg = q.astype(f32).reshape(B, Q, NSA_KV_HEADS, NSA_GROUP, HEAD_DIM) * (HEAD_DIM ** -0.5)
    d_c = t[:, None] - c_end[None, :]
    s_c = jnp.einsum('bqgrd,bngd->bgrqn', qg, k_c.astype(f32)) - slopes[:, :, None, None] * d_c.astype(f32)
    p_c = _masked_softmax(s_c, d_c >= 0)
    o_c = jnp.einsum('bgrqn,bngd->bqgrd', p_c, v_c.astype(f32))
    ns = kb_s.shape[1]
    imp = jnp.sum(p_c, axis=2)
    wts = np.convolve(np.ones(SEL_RATIO), np.ones(CMP_RATIO))
    imp = jnp.pad(imp, ((0, 0), (0, 0), (0, 0), (0, SEL_RATIO * ns + len(wts) - imp.shape[-1])))
    p_s = sum(float(wk) * imp[..., k:k + SEL_RATIO * ns:SEL_RATIO] for k, wk in enumerate(wts))
    blk = jnp.arange(ns)[None, :]
    jt = (t // SEL_BLOCK)[:, None]
    valid = blk * SEL_BLOCK <= t[:, None]
    forced = (blk == 0) | (blk == jt) | (blk == jt - 1)
    score = jnp.where(valid, p_s + jnp.where(forced, FORCE_BONUS, 0.0), NEG)
    top_v, top_i = lax.top_k(score, min(N_SEL, ns))
    gather = jax.vmap(jax.vmap(lambda blocks, idx: blocks[idx]))
    k_sel = gather(kb_s.transpose(0, 3, 1, 2, 4), top_i).astype(f32)
    v_sel = gather(vb_s.transpose(0, 3, 1, 2, 4), top_i).astype(f32)
    pos = top_i[..., None] * SEL_BLOCK + jnp.arange(SEL_BLOCK)
    d_s = t[None, None, :, None, None] - pos
    mask_s = (top_v > NEG * 0.5)[..., None] & (d_s >= 0)
    s_s = (jnp.einsum('bqgrd,bgqnkd->bgrqnk', qg, k_sel)
           - slopes[None, :, :, None, None, None] * d_s[:, :, None].astype(f32))
    p_sel = _masked_softmax(s_s.reshape(B, NSA_KV_HEADS, NSA_GROUP, Q, -1),
                            mask_s[:, :, None].reshape(B, NSA_KV_HEADS, 1, Q, -1))
    o_s = jnp.einsum('bgrqm,bgqmd->bqgrd', p_sel, v_sel.reshape(B, NSA_KV_HEADS, Q, -1, HEAD_DIM))
    d_w = t[:, None] - w_pos[None, :]
    mask_w = (d_w >= 0) & (d_w <= WINDOW) & (w_pos[None, :] >= 0)
    s_w = jnp.einsum('bqgrd,bkgd->bgrqk', qg, k_w.astype(f32)) - slopes[:, :, None, None] * d_w.astype(f32)
    p_w = _masked_softmax(s_w, mask_w)
    o_w = jnp.einsum('bgrqk,bkgd->bqgrd', p_w, v_w.astype(f32))
    g = gates.astype(f32).reshape(B, Q, NSA_KV_HEADS, NSA_GROUP, 3)
    o = g[..., 0:1] * o_c + g[..., 1:2] * o_s + g[..., 2:3] * o_w
    return o.reshape(B, Q, NSA_WIDTH).astype(q.dtype)


def _hgrn_inputs(z, lb):
    f32 = jnp.float32
    B, L = z['q_b'].shape[:2]
    fl = z['f_b'].astype(f32)
    f = lb + (1.0 - lb) * jax.nn.sigmoid(fl)
    k = (1.0 - lb) * jax.nn.sigmoid(-fl)
    q = jax.nn.silu(z['q_b'].astype(f32))
    v = z['i_b'].astype(f32)
    heads = lambda a: a.reshape(B, L, HG_HEADS, HG_DIM).transpose(0, 2, 1, 3)
    return heads(q), heads(k), heads(jnp.log(f)), heads(v)


def _hgrn_chunk(S, q, k, logf, v):
    C = q.shape[2]
    b = jnp.cumsum(logf, axis=2)
    o_inter = jnp.einsum('bhtk,bhkv->bhtv', q * jnp.exp(b), S)
    causal = jnp.tril(jnp.ones((C, C), dtype=bool))[:, :, None]
    diff = b[:, :, :, None, :] - b[:, :, None, :, :]
    dec = jnp.where(causal, jnp.exp(jnp.where(causal, diff, 0.0)), 0.0)
    a = jnp.einsum('bhtk,bhtsk,bhsk->bhts', q, dec, k)
    o = o_inter + jnp.einsum('bhts,bhsv->bhtv', a, v)
    b_last = b[:, :, -1]
    S_new = jnp.exp(b_last)[..., None] * S + jnp.einsum('bhsk,bhsv->bhkv', k * jnp.exp(b_last[:, :, None] - b), v)
    return S_new, o


def _hgrn_scan(q, k, logf, v, S0):
    B, H, L, _ = q.shape
    C = min(HG_CHUNK, L)
    nc = L // C
    ch = lambda a: a.reshape(B, H, nc, C, a.shape[-1]).transpose(2, 0, 1, 3, 4)
    S, o = lax.scan(lambda s, xs: _hgrn_chunk(s, *xs), S0, (ch(q), ch(k), ch(logf), ch(v)))
    return S, o.transpose(1, 2, 0, 3, 4).reshape(B, H, L, HG_DIM)


def _hgrn_out(o, norm_w, zb):
    B, _, L, _ = o.shape
    o = o.transpose(0, 2, 1, 3)
    o = o * lax.rsqrt(jnp.mean(o * o, axis=-1, keepdims=True) + EPS) * norm_w.astype(jnp.float32)
    return (o.reshape(B, L, HG_WIDTH) * jax.nn.silu(zb.astype(jnp.float32))).astype(zb.dtype)


def _finish(x, y_a, y_b, p, g_post, w_out, ple_proj, ple_gate):
    mix = jnp.concatenate([y_a, y_b], axis=-1) @ w_out
    h = x + _rmsnorm(mix, g_post)
    return h + (p @ ple_proj) * jax.nn.sigmoid(h @ ple_gate)


def _prompt_layer(x, p, w_in, g_pre, cmp_pe, cmp_w1, cmp_b1, cmp_w2, lb, hg_norm, w_out, g_post, ple_proj, ple_gate):
    B, L, _ = x.shape
    z, kv, gates = _project(x, g_pre, w_in)
    k_c = _compress(kv['k_cmp'], cmp_pe[0], cmp_w1[0], cmp_b1[0], cmp_w2[0])
    v_c = _compress(kv['v_cmp'], cmp_pe[1], cmp_w1[1], cmp_b1[1], cmp_w2[1])
    c_end = _cmp_end(k_c.shape[1])
    kb_s, vb_s = _sel_blocks(kv['k_slc']), _sel_blocks(kv['v_slc'])
    pad = ((0, 0), (WINDOW, 0), (0, 0), (0, 0))
    kp, vp = jnp.pad(kv['k_win'], pad), jnp.pad(kv['v_win'], pad)
    qb = min(Q_BLOCK, L)
    nq = L // qb
    q_blocks = z['q_a'].reshape(B, nq, qb, NSA_WIDTH).transpose(1, 0, 2, 3)
    g_blocks = gates.reshape(B, nq, qb, NSA_HEADS, 3).transpose(1, 0, 2, 3, 4)
    starts = jnp.arange(nq) * qb

    def one_block(args):
        qbk, gbk, s0 = args
        t = s0 + jnp.arange(qb)
        kw = lax.dynamic_slice_in_dim(kp, s0, WINDOW + qb, axis=1)
        vw = lax.dynamic_slice_in_dim(vp, s0, WINDOW + qb, axis=1)
        w_pos = s0 - WINDOW + jnp.arange(WINDOW + qb)
        return _nsa_attend(qbk, gbk, t, k_c, v_c, c_end, kb_s, vb_s, kw, vw, w_pos)

    o_a = lax.map(one_block, (q_blocks, g_blocks, starts)).transpose(1, 0, 2, 3).reshape(B, L, NSA_WIDTH)
    y_a = o_a * jax.nn.silu(z['z_a'])
    q, k, logf, v = _hgrn_inputs(z, lb)
    S, o_b = _hgrn_scan(q, k, logf, v, jnp.zeros((B, HG_HEADS, HG_DIM, HG_DIM), jnp.float32))
    y_b = _hgrn_out(o_b, hg_norm, z['z_b'])
    h = _finish(x, y_a, y_b, p, g_post, w_out, ple_proj, ple_gate)
    wl = min(WINDOW, L)
    new_cmp = jnp.stack([kv['k_cmp'], kv['v_cmp']], axis=2)
    new_slc = jnp.stack([kv['k_slc'], kv['v_slc']], axis=2)
    new_win = jnp.stack([kv['k_win'][:, L - wl:], kv['v_win'][:, L - wl:]], axis=2)
    return h, new_cmp, new_slc, new_win, S


def _sample_layer(x, p, cmp_cache, slc_cache, win_cache, S0, page_table, w_in, g_pre, cmp_pe, cmp_w1, cmp_b1, cmp_w2,
                  lb, hg_norm, w_out, g_post, ple_proj, ple_gate):
    B, T, _ = x.shape
    past = page_table.shape[1] * cmp_cache.shape[1]
    z, kv, gates = _project(x, g_pre, w_in)
    new_cmp = jnp.stack([kv['k_cmp'], kv['v_cmp']], axis=2)
    new_slc = jnp.stack([kv['k_slc'], kv['v_slc']], axis=2)
    new_win = jnp.stack([kv['k_win'], kv['v_win']], axis=2)
    tail = (2, NSA_KV_HEADS, HEAD_DIM)
    full_cmp = jnp.concatenate([cmp_cache[page_table].reshape((B, past) + tail), new_cmp], axis=1)
    full_slc = jnp.concatenate([slc_cache[page_table].reshape((B, past) + tail), new_slc], axis=1)
    win_all = jnp.concatenate([win_cache, new_win], axis=1)
    wl = win_cache.shape[1]
    t = past + jnp.arange(T)
    k_c = _compress(full_cmp[:, :, 0], cmp_pe[0], cmp_w1[0], cmp_b1[0], cmp_w2[0])
    v_c = _compress(full_cmp[:, :, 1], cmp_pe[1], cmp_w1[1], cmp_b1[1], cmp_w2[1])
    c_end = _cmp_end(k_c.shape[1])
    kb_s, vb_s = _sel_blocks(full_slc[:, :, 0]), _sel_blocks(full_slc[:, :, 1])
    w_pos = past - wl + jnp.arange(wl + T)
    o_a = _nsa_attend(z['q_a'], gates, t, k_c, v_c, c_end, kb_s, vb_s, win_all[:, :, 0], win_all[:, :, 1], w_pos)
    y_a = o_a * jax.nn.silu(z['z_a'])
    q, k, logf, v = _hgrn_inputs(z, lb)
    S, o_b = _hgrn_chunk(S0.astype(jnp.float32), q, k, logf, v)
    y_b = _hgrn_out(o_b, hg_norm, z['z_b'])
    h = _finish(x, y_a, y_b, p, g_post, w_out, ple_proj, ple_gate)
    return h, new_cmp, new_slc, win_all[:, T:], S


def setup_inputs(seed: int = 0) -> dict:
    key = jax.random.key(seed)
    ks = jax.random.split(key, 24)
    f32 = jnp.float32
    n_pages = PAST_LEN // PAGE_SIZE
    n_used = DEC_BATCH * n_pages
    n_pool = n_used + max(1, n_used // 4)
    win_len = min(WINDOW, PAST_LEN)

    def nrm(k, shape, scale=1.0):
        return jax.random.normal(k, shape, f32) * scale

    page_table = jax.random.permutation(ks[0], n_pool)[:n_used].reshape(DEC_BATCH, n_pages).astype(jnp.int32)
    kv_tail = (2, NSA_KV_HEADS, HEAD_DIM)
    return {
        'x_prompt': nrm(ks[1], (BATCH, SEQ, D_MODEL)),
        'x_sample': nrm(ks[2], (DEC_BATCH, DEC_SEQ, D_MODEL)),
        'cache_cmp_kv': nrm(ks[3], (DEPTH, n_pool, PAGE_SIZE) + kv_tail),
        'cache_slc_kv': nrm(ks[4], (DEPTH, n_pool, PAGE_SIZE) + kv_tail),
        'cache_win_kv': nrm(ks[5], (DEPTH, DEC_BATCH, win_len) + kv_tail),
        'state_hgrn': nrm(ks[6], (DEPTH, DEC_BATCH, HG_HEADS, HG_DIM, HG_DIM), 0.3),
        'page_table': page_table,
        'p_prompt': nrm(ks[7], (DEPTH, BATCH, SEQ, PLE_DIM)),
        'p_sample': nrm(ks[8], (DEPTH, DEC_BATCH, DEC_SEQ, PLE_DIM)),
        'w_in': nrm(ks[9], (DEPTH, D_MODEL, IN_WIDTH), D_MODEL ** -0.5),
        'g_pre': 1.0 + nrm(ks[10], (DEPTH, D_MODEL), 0.1),
        'cmp_pe': nrm(ks[11], (DEPTH, 2, CMP_BLOCK, HEAD_DIM), 0.5),
        'cmp_w1': nrm(ks[12], (DEPTH, 2, CMP_BLOCK * HEAD_DIM, CMP_HIDDEN), (CMP_BLOCK * HEAD_DIM) ** -0.5),
        'cmp_b1': nrm(ks[13], (DEPTH, 2, CMP_HIDDEN), 0.01),
        'cmp_w2': nrm(ks[14], (DEPTH, 2, CMP_HIDDEN, HEAD_DIM), CMP_HIDDEN ** -0.5),
        'hg_lower': nrm(ks[15], (DEPTH + 1, HG_WIDTH), 0.5),
        'hg_norm': 1.0 + nrm(ks[16], (DEPTH, HG_DIM), 0.1),
        'w_out': nrm(ks[17], (DEPTH, MIX_WIDTH, D_MODEL), MIX_WIDTH ** -0.5),
        'g_post': 1.0 + nrm(ks[18], (DEPTH, D_MODEL), 0.1),
        'ple_proj': nrm(ks[19], (DEPTH, PLE_DIM, D_MODEL), PLE_DIM ** -0.5),
        'ple_gate': nrm(ks[20], (DEPTH, D_MODEL, D_MODEL), D_MODEL ** -0.5),
    }


def reference(x_prompt, x_sample, cache_cmp_kv, cache_slc_kv, cache_win_kv, state_hgrn, page_table, p_prompt, p_sample,
              w_in, g_pre, cmp_pe, cmp_w1, cmp_b1, cmp_w2, hg_lower, hg_norm, w_out, g_post, ple_proj, ple_gate):
    lb_all = jnp.cumsum(jax.nn.softmax(hg_lower.astype(jnp.float32), axis=0), axis=0)
    hp, hs = x_prompt, x_sample
    st_p, st_s = [], []
    for i in range(DEPTH):
        shared = (w_in[i], g_pre[i], cmp_pe[i], cmp_w1[i], cmp_b1[i], cmp_w2[i], lb_all[i], hg_norm[i],
                  w_out[i], g_post[i], ple_proj[i], ple_gate[i])
        hp, a0, a1, a2, a3 = _prompt_layer(hp, p_prompt[i], *shared)
        hs, b0, b1, b2, b3 = _sample_layer(hs, p_sample[i], cache_cmp_kv[i], cache_slc_kv[i], cache_win_kv[i],
                                           state_hgrn[i], page_table, *shared)
        st_p.append((a0, a1, a2, a3))
        st_s.append((b0, b1, b2, b3))
    cmp_p = jnp.stack([s[0] for s in st_p])
    slc_p = jnp.stack([s[1] for s in st_p])
    win_p = jnp.stack([s[2] for s in st_p])
    hg_p = jnp.stack([s[3] for s in st_p])
    cmp_s = jnp.stack([s[0] for s in st_s])
    slc_s = jnp.stack([s[1] for s in st_s])
    win_s = jnp.stack([s[2] for s in st_s])
    hg_s = jnp.stack([s[3] for s in st_s])
    return (hp, hs, cmp_p, slc_p, win_p, hg_p, cmp_s, slc_s, win_s, hg_s)
```

```python
import functools

import numpy as np
import jax
import jax.numpy as jnp
from jax import lax
from jax.experimental import pallas as pl
from jax.experimental.pallas import tpu as pltpu

D_MODEL = 1024
PAGE_SIZE = 128
NSA_HEADS = 8
NSA_KV_HEADS = 2
NSA_GROUP = NSA_HEADS // NSA_KV_HEADS
HEAD_DIM = 64
NSA_WIDTH = NSA_HEADS * HEAD_DIM
KV_WIDTH = NSA_KV_HEADS * HEAD_DIM
CMP_BLOCK = 32
CMP_STRIDE = 16
CMP_RATIO = CMP_BLOCK // CMP_STRIDE
CMP_HIDDEN = 128
SEL_BLOCK = 64
SEL_RATIO = SEL_BLOCK // CMP_STRIDE
N_SEL = 16
WINDOW = 512
Q_BLOCK = 128
HG_HEADS = 4
HG_DIM = 128
HG_WIDTH = HG_HEADS * HG_DIM
PLE_DIM = 256
EPS = 1e-6
NEG = -1e30
FORCE_BONUS = 1e6
REMOVED = -3e38

COL_QA, COL_ZA, COL_QB, COL_FB, COL_IB, COL_ZB = 0, 512, 1024, 1536, 2048, 2560
COL_CMP, COL_SLC, COL_WIN, COL_GATE = 3072, 3328, 3584, 3840
PROJ_WIDTH = 4096
_ORIG = dict(q_a=0, k_cmp=512, gate_a=1280, z_a=1304, q_b=1816, f_b=2328, i_b=2840, z_b=3352, end=3864)

LANES = 128
VMEM_LIMIT = 56 * 1024 * 1024

SEL_CHUNK = 512
WIN_KEYS = WINDOW + Q_BLOCK
N_CMP_PAD = 512
N_SELBLK = 128

f32 = jnp.float32
bf16 = jnp.bfloat16


def _cparams(sem):
    return pltpu.CompilerParams(dimension_semantics=sem, vmem_limit_bytes=VMEM_LIMIT)


def _dot(a, b):
    return jnp.dot(a, b, preferred_element_type=f32)


def _dot_nt(a, b):
    return lax.dot_general(a, b, (((1,), (1,)), ((), ())), preferred_element_type=f32)


def _dot_tn(a, b):
    return lax.dot_general(a, b, (((0,), (0,)), ((), ())), preferred_element_type=f32)


def _sigmoid(x):
    return 1.0 / (1.0 + jnp.exp(-x))


def _silu(x):
    return x * _sigmoid(x)


def _proj_kernel(x_ref, g_ref, w_ref, z_ref, xn_ref):
    @pl.when(pl.program_id(1) == 0)
    def _():
        x = x_ref[...]
        y = x * lax.rsqrt(jnp.mean(x * x, axis=-1, keepdims=True) + EPS)
        xn_ref[...] = (y * g_ref[...]).astype(bf16)

    z_ref[...] = _dot(xn_ref[...], w_ref[...])


def _proj(x, g, w, tm):
    n = x.shape[0]
    tn = 1024
    return pl.pallas_call(
        _proj_kernel,
        grid=(n // tm, PROJ_WIDTH // tn),
        in_specs=[pl.BlockSpec((tm, D_MODEL), lambda i, j: (i, 0)),
                  pl.BlockSpec((1, D_MODEL), lambda i, j: (0, 0)),
                  pl.BlockSpec((D_MODEL, tn), lambda i, j: (0, j))],
        out_specs=pl.BlockSpec((tm, tn), lambda i, j: (i, j)),
        out_shape=jax.ShapeDtypeStruct((n, PROJ_WIDTH), f32),
        scratch_shapes=[pltpu.VMEM((tm, D_MODEL), bf16)],
        compiler_params=_cparams(("parallel", "arbitrary")),
        name="proj",
    )(x, g, w)


def _finish_kernel(x_ref, oa_ref, za_ref, yb_ref, p_ref, wo_ref, gp_ref, pp_ref, pg_ref, h_ref):
    ya = (oa_ref[...] * _silu(za_ref[...])).astype(bf16)
    yb = yb_ref[...].astype(bf16)
    mix = _dot(ya, wo_ref[0:NSA_WIDTH, :]) + _dot(yb, wo_ref[NSA_WIDTH:, :])
    nrm = mix * lax.rsqrt(jnp.mean(mix * mix, axis=-1, keepdims=True) + EPS) * gp_ref[...]
    h = x_ref[...] + nrm
    gate = _sigmoid(_dot(h.astype(bf16), pg_ref[...]))
    h_ref[...] = h + _dot(p_ref[...].astype(bf16), pp_ref[...]) * gate


def _finish(x, oa, z, yb, p, w_out, g_post, ple_proj, ple_gate, tm):
    n = x.shape[0]
    za_blk = COL_ZA // NSA_WIDTH
    row = lambda i: (i, 0)
    const = lambda i: (0, 0)
    return pl.pallas_call(
        _finish_kernel,
        grid=(n // tm,),
        in_specs=[pl.BlockSpec((tm, D_MODEL), row),
                  pl.BlockSpec((tm, NSA_WIDTH), row),
                  pl.BlockSpec((tm, NSA_WIDTH), lambda i: (i, za_blk)),
                  pl.BlockSpec((tm, HG_WIDTH), row),
                  pl.BlockSpec((tm, PLE_DIM), row),
                  pl.BlockSpec((D_MODEL, D_MODEL), const),
                  pl.BlockSpec((1, D_MODEL), const),
                  pl.BlockSpec((PLE_DIM, D_MODEL), const),
                  pl.BlockSpec((D_MODEL, D_MODEL), const)],
        out_specs=pl.BlockSpec((tm, D_MODEL), row),
        out_shape=jax.ShapeDtypeStruct((n, D_MODEL), f32),
        compiler_params=_cparams(("parallel",)),
        name="finish",
    )(x, oa, z, yb, p, w_out, g_post, ple_proj, ple_gate)


SEGS_PER_PAGE = PAGE_SIZE // CMP_STRIDE
SEG_WIDTH = CMP_STRIDE * 2 * KV_WIDTH
N_COMBO = 2 * NSA_KV_HEADS


def _compress_kernel(pt_ref, pages_ref, wa_ref, wb_ref, pea_ref, peb_ref, b1_ref, w2_ref, out_ref,
                     buf_ref, sem_ref, *, n_pages):
    b = pl.program_id(0)
    nb = pl.num_programs(0)
    slot = lax.rem(b, 2)

    def page_copy(seq, s, j):
        return pltpu.make_async_copy(pages_ref.at[pt_ref[seq, j]],
                                     buf_ref.at[s, pl.ds(j * SEGS_PER_PAGE, SEGS_PER_PAGE)],
                                     sem_ref.at[s])

    def start_all(seq, s):
        def body(j, c):
            page_copy(seq, s, j).start()
            return c
        lax.fori_loop(0, n_pages, body, 0)

    @pl.when(b == 0)
    def _():
        start_all(0, 0)

    @pl.when(b + 1 < nb)
    def _():
        start_all(b + 1, 1 - slot)

    def wait_body(j, c):
        page_copy(b, slot, j).wait()
        return c
    lax.fori_loop(0, n_pages, wait_body, 0)

    seg = buf_ref[slot]
    n_seg = seg.shape[0]
    pa = _dot((seg + pea_ref[...]).astype(bf16), wa_ref[...])
    pb = _dot((seg + peb_ref[...]).astype(bf16), wb_ref[...])
    pb = pltpu.roll(pb, n_seg - 1, axis=0)
    hid = _silu(pa + pb + b1_ref[...])
    for c in range(N_COMBO):
        hc = hid[:, c * CMP_HIDDEN:(c + 1) * CMP_HIDDEN].astype(bf16)
        out_ref[0, c] = _dot(hc, w2_ref[c // NSA_KV_HEADS])


def _compress(pages, page_table, wa, wb, pea, peb, b1, w2):
    nseq, n_pages = page_table.shape
    n_seg = n_pages * SEGS_PER_PAGE
    const2 = lambda b, pt: (0, 0)
    grid_spec = pltpu.PrefetchScalarGridSpec(
        num_scalar_prefetch=1,
        grid=(nseq,),
        in_specs=[pl.BlockSpec(memory_space=pl.ANY),
                  pl.BlockSpec((SEG_WIDTH, N_COMBO * CMP_HIDDEN), const2),
                  pl.BlockSpec((SEG_WIDTH, N_COMBO * CMP_HIDDEN), const2),
                  pl.BlockSpec((1, SEG_WIDTH), const2),
                  pl.BlockSpec((1, SEG_WIDTH), const2),
                  pl.BlockSpec((1, N_COMBO * CMP_HIDDEN), const2),
                  pl.BlockSpec((2, CMP_HIDDEN, HEAD_DIM), lambda b, pt: (0, 0, 0))],
        out_specs=pl.BlockSpec((1, N_COMBO, n_seg, HEAD_DIM), lambda b, pt: (b, 0, 0, 0)),
        scratch_shapes=[pltpu.VMEM((2, n_seg, SEG_WIDTH), f32),
                        pltpu.SemaphoreType.DMA((2,))],
    )
    return pl.pallas_call(
        functools.partial(_compress_kernel, n_pages=n_pages),
        grid_spec=grid_spec,
        out_shape=jax.ShapeDtypeStruct((nseq, N_COMBO, n_seg, HEAD_DIM), f32),
        compiler_params=_cparams(("arbitrary",)),
        name="compress",
    )(page_table, pages, wa, wb, pea, peb, b1, w2)


def _compress_weights(cmp_pe, cmp_w1, cmp_b1, cmp_w2):
    half = CMP_STRIDE * HEAD_DIM
    eye = jnp.eye(NSA_KV_HEADS, dtype=f32)

    def big(w_half):
        t = jnp.einsum('kjdh,ka,gb->jabdkgh', w_half, jnp.eye(2, dtype=f32), eye)
        return t.reshape(SEG_WIDTH, N_COMBO * CMP_HIDDEN)

    w1 = cmp_w1.reshape(2, 2, CMP_STRIDE, HEAD_DIM, CMP_HIDDEN)
    wa = big(w1[:, 0]).astype(bf16)
    wb = big(w1[:, 1]).astype(bf16)
    pe = cmp_pe.reshape(2, 2, CMP_STRIDE, HEAD_DIM)

    def pe_row(p):
        t = jnp.broadcast_to(p.transpose(1, 0, 2)[:, :, None, :],
                             (CMP_STRIDE, 2, NSA_KV_HEADS, HEAD_DIM))
        return t.reshape(1, SEG_WIDTH)

    pea, peb = pe_row(pe[:, 0]), pe_row(pe[:, 1])
    b1 = jnp.broadcast_to(cmp_b1[:, None, :], (2, NSA_KV_HEADS, CMP_HIDDEN)).reshape(1, N_COMBO * CMP_HIDDEN)
    return wa, wb, pea, peb, b1, cmp_w2.astype(bf16)


def _sel_map_matrix():
    wts = np.convolve(np.ones(SEL_RATIO), np.ones(CMP_RATIO))
    m = np.zeros((N_CMP_PAD, N_SELBLK), np.float32)
    for j in range(N_SELBLK):
        for k, wk in enumerate(wts):
            n = SEL_RATIO * j + k
            if n < N_CMP_PAD - 1:
                m[n, j] = wk
    return jnp.asarray(m, dtype=bf16)


def _expand_matrix(n_chunks):
    key = np.arange(n_chunks * SEL_CHUNK) // SEL_BLOCK
    e = (np.arange(N_SELBLK)[:, None] == key[None, :]).astype(np.float32)
    return jnp.asarray(e.reshape(N_SELBLK, n_chunks, SEL_CHUNK).transpose(1, 0, 2), dtype=bf16)


def _importance_to_blocks(imp, m):
    hi = imp.astype(bf16)
    lo = (imp - hi.astype(f32)).astype(bf16)
    return _dot(hi, m) + _dot(lo, m)


def _nsa_prompt_kernel(q_ref, gate_ref, kc_ref, vc_ref, kvs_ref, kvw_ref, m_ref, e_ref, o_ref,
                       acc_ref, mx_ref):
    g = pl.program_id(1)
    qi = pl.program_id(2)
    q0 = qi * Q_BLOCK
    rows = NSA_GROUP * Q_BLOCK

    q64 = q_ref[0, 0, 0] * jnp.asarray(HEAD_DIM ** -0.5, bf16)
    q128 = jnp.concatenate([q64, jnp.zeros_like(q64)], axis=1)
    slope = [jnp.where(g == 0, 2.0 ** -(r + 1), 2.0 ** -(NSA_GROUP + r + 1)).astype(f32)
             for r in range(NSA_GROUP)]

    qrow = lax.broadcasted_iota(jnp.int32, (Q_BLOCK, 1), 0)

    def softmax_rows(s, dist, mask):
        es, ls = [], []
        for r in range(NSA_GROUP):
            sr = jnp.where(mask, s[r * Q_BLOCK:(r + 1) * Q_BLOCK] - slope[r] * dist, NEG)
            m = jnp.max(sr, axis=1, keepdims=True)
            e = jnp.where(mask, jnp.exp(sr - m), 0.0)
            es.append(e)
            ls.append(1.0 / jnp.maximum(jnp.sum(e, axis=1, keepdims=True), 1e-30))
        return es, ls

    kc = kc_ref[0, 0].astype(bf16)
    vc = vc_ref[0, 0].astype(bf16)
    ncol = lax.broadcasted_iota(jnp.int32, (1, N_CMP_PAD), 1)
    d_c = (q0 + qrow) - (ncol * CMP_STRIDE + (CMP_BLOCK - 1))
    es, ls = softmax_rows(_dot_nt(q64, kc), d_c.astype(f32), d_c >= 0)
    ps_c = [e * l for e, l in zip(es, ls)]
    o_c = _dot(jnp.concatenate(ps_c, axis=0).astype(bf16), vc)
    imp = ps_c[0] + ps_c[1] + ps_c[2] + ps_c[3]
    p_sel = _importance_to_blocks(imp, m_ref[...])

    ps_t = p_sel.T
    blk = lax.broadcasted_iota(jnp.int32, (N_SELBLK, Q_BLOCK), 0)
    tq = q0 + lax.broadcasted_iota(jnp.int32, (N_SELBLK, Q_BLOCK), 1)
    jt = jnp.right_shift(tq, SEL_BLOCK.bit_length() - 1)
    valid = blk <= jt
    forced = (blk == 0) | (blk == jt) | (blk == jt - 1)
    score = jnp.where(valid, ps_t + jnp.where(forced, FORCE_BONUS, 0.0), NEG)
    sel_t = jnp.zeros((N_SELBLK, Q_BLOCK), f32)
    for _ in range(N_SEL):
        best = jnp.max(score, axis=0, keepdims=True)
        idx = jnp.min(jnp.where(score == best, blk, N_SELBLK), axis=0, keepdims=True)
        hit = blk == idx
        sel_t = jnp.where(hit & (best > NEG * 0.5), 1.0, sel_t)
        score = jnp.where(hit, REMOVED, score)
    sel = sel_t.T.astype(bf16)

    acc_ref[...] = jnp.zeros_like(acc_ref)
    mx_ref[...] = jnp.full_like(mx_ref, NEG)
    kcol = lax.broadcasted_iota(jnp.int32, (1, SEL_CHUNK), 1)
    d_base = (qrow - kcol).astype(f32)

    def sel_chunk(c, carry):
        k0 = pl.multiple_of(c * SEL_CHUNK, SEL_CHUNK)
        kv = kvs_ref[0, 0, pl.ds(k0, SEL_CHUNK), :]
        s = _dot_nt(q128, kv[:, 0:LANES])
        dist = d_base + (q0 - k0).astype(f32)
        allowed = (_dot(sel, e_ref[c]) > 0.5) & (dist >= 0.0)
        ps = []
        for r in range(NSA_GROUP):
            rs = slice(r * Q_BLOCK, (r + 1) * Q_BLOCK)
            sr = jnp.where(allowed, s[rs] - slope[r] * dist, NEG)
            m_old = mx_ref[rs]
            m_new = jnp.maximum(m_old, jnp.max(sr, axis=1, keepdims=True))
            acc_ref[rs] = acc_ref[rs] * jnp.exp(m_old - m_new)
            mx_ref[rs] = m_new
            ps.append(jnp.exp(sr - m_new).astype(bf16))
        acc_ref[...] += _dot(jnp.concatenate(ps, axis=0), kv[:, LANES:])
        return carry

    n_chunks = (q0 + Q_BLOCK + SEL_CHUNK - 1) // SEL_CHUNK
    lax.fori_loop(0, n_chunks, sel_chunk, 0)
    acc = acc_ref[...]
    o_s = acc[:, 0:HEAD_DIM] * (1.0 / jnp.maximum(acc[:, HEAD_DIM:HEAD_DIM + 1], 1e-30))

    w0 = pl.multiple_of(jnp.maximum(q0 - WINDOW, 0), Q_BLOCK)
    kvw = kvw_ref[0, 0, pl.ds(w0, WIN_KEYS), :]
    wcol = lax.broadcasted_iota(jnp.int32, (1, WIN_KEYS), 1)
    d_w = (q0 - w0) + qrow - wcol
    es, _ = softmax_rows(_dot_nt(q128, kvw[:, 0:LANES]), d_w.astype(f32), (d_w >= 0) & (d_w <= WINDOW))
    acc_w = _dot(jnp.concatenate(es, axis=0).astype(bf16), kvw[:, LANES:])
    o_w = acc_w[:, 0:HEAD_DIM] * (1.0 / jnp.maximum(acc_w[:, HEAD_DIM:HEAD_DIM + 1], 1e-30))

    gate = _sigmoid(gate_ref[...])

    def gate_col(r, j):
        lo = gate[:, r * 3 + j:r * 3 + j + 1]
        hi = gate[:, (NSA_GROUP + r) * 3 + j:(NSA_GROUP + r) * 3 + j + 1]
        return jnp.where(g == 0, lo, hi)

    for r in range(NSA_GROUP):
        rs = slice(r * Q_BLOCK, (r + 1) * Q_BLOCK)
        o_ref[0, 0, 0, rs, :] = gate_col(r, 0) * o_c[rs] + gate_col(r, 1) * o_s[rs] + gate_col(r, 2) * o_w[rs]


def _nsa_prompt(q_st, z, comp, kv_slc, kv_win, m_mat, e_mat):
    bsz, nq = q_st.shape[0], q_st.shape[1]
    seq = kv_slc.shape[2]
    rows = NSA_GROUP * Q_BLOCK
    gate_blk = COL_GATE // LANES
    return pl.pallas_call(
        _nsa_prompt_kernel,
        grid=(bsz, NSA_KV_HEADS, nq),
        in_specs=[pl.BlockSpec((1, 1, 1, rows, HEAD_DIM), lambda b, g, i: (b, i, g, 0, 0)),
                  pl.BlockSpec((Q_BLOCK, LANES), lambda b, g, i: (b * nq + i, gate_blk)),
                  pl.BlockSpec((1, 1, N_CMP_PAD, HEAD_DIM), lambda b, g, i: (b, g, 0, 0)),
                  pl.BlockSpec((1, 1, N_CMP_PAD, HEAD_DIM), lambda b, g, i: (b, NSA_KV_HEADS + g, 0, 0)),
                  pl.BlockSpec((1, 1, seq, 2 * LANES), lambda b, g, i: (b, g, 0, 0)),
                  pl.BlockSpec((1, 1, seq, 2 * LANES), lambda b, g, i: (b, g, 0, 0)),
                  pl.BlockSpec((N_CMP_PAD, N_SELBLK), lambda b, g, i: (0, 0)),
                  pl.BlockSpec(e_mat.shape, lambda b, g, i: (0, 0, 0))],
        out_specs=pl.BlockSpec((1, 1, 1, rows, HEAD_DIM), lambda b, g, i: (b, i, g, 0, 0)),
        out_shape=jax.ShapeDtypeStruct((bsz, nq, NSA_KV_HEADS, rows, HEAD_DIM), f32),
        scratch_shapes=[pltpu.VMEM((rows, LANES), f32), pltpu.VMEM((rows, 1), f32)],
        compiler_params=_cparams(("parallel", "parallel", "arbitrary")),
        name="nsa_prompt",
    )(q_st, z, comp, comp, kv_slc, kv_win, m_mat, e_mat)


def _attention_kv(piece, bsz, seq):
    t = piece.reshape(bsz, seq, 2, NSA_KV_HEADS, HEAD_DIM).transpose(2, 0, 3, 1, 4)
    zeros = jnp.zeros_like(t[0])
    ones = jnp.ones(t[0].shape[:-1] + (1,), f32)
    out = jnp.concatenate([t[0], zeros, t[1], ones, zeros[..., :HEAD_DIM - 1]], axis=-1)
    return out.astype(bf16)


HG_CHUNK = 128
HG_BAND = 8


def _lower_bound(lo):
    m = jnp.max(lo, axis=0, keepdims=True)
    e = jnp.exp(lo - m)
    return e[0:1] / jnp.sum(e, axis=0, keepdims=True)


def _hgrn_gates(q, fl, lb):
    f = lb + (1.0 - lb) * _sigmoid(fl)
    k = (1.0 - lb) * _sigmoid(-fl)
    return _silu(q), k, jnp.log(f)


def _hgrn_out(o, nw, zb):
    o = o * lax.rsqrt(jnp.mean(o * o, axis=-1, keepdims=True) + EPS) * nw
    return o * _silu(zb)


def _hgrn_prompt_kernel(q_ref, f_ref, i_ref, zb_ref, lo_ref, nw_ref, y_ref, s_ref, st_ref):
    c = pl.program_id(2)
    n = HG_CHUNK

    @pl.when(c == 0)
    def _():
        st_ref[...] = jnp.zeros_like(st_ref)

    lb = _lower_bound(lo_ref[...])
    qq, kk, logf = _hgrn_gates(q_ref[...], f_ref[...], lb)
    v = i_ref[...]
    row = lax.broadcasted_iota(jnp.int32, (n, n), 0)
    col = lax.broadcasted_iota(jnp.int32, (n, n), 1)

    b = logf
    sh = 1
    while sh < n:
        b = b + jnp.where(row >= sh, pltpu.roll(b, sh, axis=0), 0.0)
        sh *= 2

    a = jnp.zeros((n, n), f32)
    sub = jnp.bitwise_and(row, HG_BAND - 1)
    for d in range(HG_BAND):
        if d == 0:
            term = qq * kk
        else:
            ok = sub >= d
            decay = jnp.exp(jnp.where(ok, b - pltpu.roll(b, d, axis=0), 0.0))
            term = jnp.where(ok, qq * pltpu.roll(kk, d, axis=0) * decay, 0.0)
        a = a + jnp.where(col == row - d, jnp.sum(term, axis=1, keepdims=True), 0.0)

    size = 2 * HG_BAND
    while size <= n:
        half = size // 2
        qs, ks = [], []
        zero = jnp.zeros((half, HG_DIM), f32)
        for r0 in range(0, n, size):
            beta = b[r0 + half - 1:r0 + half]
            ks += [kk[r0:r0 + half] * jnp.exp(beta - b[r0:r0 + half]), zero]
            qs += [zero, qq[r0 + half:r0 + size] * jnp.exp(b[r0 + half:r0 + size] - beta)]
        part = _dot_nt(jnp.concatenate(qs, axis=0).astype(bf16), jnp.concatenate(ks, axis=0).astype(bf16))
        if size < n:
            sh = size.bit_length() - 1
            part = jnp.where(jnp.right_shift(row, sh) == jnp.right_shift(col, sh), part, 0.0)
        a = a + part
        size *= 2

    st = st_ref[...]
    o = _dot(a.astype(bf16), v.astype(bf16)) + _dot_nt((qq * jnp.exp(b)).astype(bf16), st.astype(bf16))
    b_last = b[n - 1:n]
    k_tail = kk * jnp.exp(b_last - b)
    st_new = st * jnp.exp(b_last) + _dot_tn(v.astype(bf16), k_tail.astype(bf16))
    st_ref[...] = st_new
    y_ref[...] = _hgrn_out(o, nw_ref[...], zb_ref[...])

    @pl.when(c == pl.num_programs(2) - 1)
    def _():
        s_ref[0, 0] = st_new.T


def _hgrn_prompt(z, hg_lower, hg_norm, bsz, seq):
    nch = seq // HG_CHUNK

    def zcol(base):
        return pl.BlockSpec((HG_CHUNK, HG_DIM), lambda b, h, c: (b * nch + c, base // HG_DIM + h))

    return pl.pallas_call(
        _hgrn_prompt_kernel,
        grid=(bsz, HG_HEADS, nch),
        in_specs=[zcol(COL_QB), zcol(COL_FB), zcol(COL_IB), zcol(COL_ZB),
                  pl.BlockSpec((hg_lower.shape[0], HG_DIM), lambda b, h, c: (0, h)),
                  pl.BlockSpec((1, HG_DIM), lambda b, h, c: (0, 0))],
        out_specs=[pl.BlockSpec((HG_CHUNK, HG_DIM), lambda b, h, c: (b * nch + c, h)),
                   pl.BlockSpec((1, 1, HG_DIM, HG_DIM), lambda b, h, c: (b, h, 0, 0))],
        out_shape=[jax.ShapeDtypeStruct((bsz * seq, HG_WIDTH), f32),
                   jax.ShapeDtypeStruct((bsz, HG_HEADS, HG_DIM, HG_DIM), f32)],
        scratch_shapes=[pltpu.VMEM((HG_DIM, HG_DIM), f32)],
        compiler_params=_cparams(("parallel", "parallel", "arbitrary")),
        name="hgrn_prompt",
    )(z, z, z, z, hg_lower, hg_norm)


def _hgrn_sample_kernel(q_ref, f_ref, i_ref, zb_ref, lo_ref, nw_ref, s_ref, y_ref, so_ref, o_scr):
    nb = q_ref.shape[0]
    lb = _lower_bound(lo_ref[...])
    qq, kk, logf = _hgrn_gates(q_ref[...], f_ref[...], lb)
    v = i_ref[...]
    pad = jnp.zeros((HG_DIM - nb, HG_DIM), f32)
    to_cols = lambda x: jnp.concatenate([x, pad], axis=0).T
    f_t, k_t, q_t = to_cols(jnp.exp(logf)), to_cols(kk), to_cols(qq)
    for b in range(nb):
        s_new = f_t[:, b:b + 1] * s_ref[0, b, 0] + k_t[:, b:b + 1] * v[b:b + 1, :]
        so_ref[0, b, 0] = s_new
        o_scr[b:b + 1, :] = jnp.sum(s_new * q_t[:, b:b + 1], axis=0, keepdims=True)
    y_ref[...] = _hgrn_out(o_scr[...], nw_ref[...], zb_ref[...])


def _hgrn_sample(z, state, hg_lower, hg_norm):
    nb = z.shape[0]

    def zcol(base):
        return pl.BlockSpec((nb, HG_DIM), lambda h: (0, base // HG_DIM + h))

    sblk = pl.BlockSpec((1, nb, 1, HG_DIM, HG_DIM), lambda h: (0, 0, h, 0, 0))
    return pl.pallas_call(
        _hgrn_sample_kernel,
        grid=(HG_HEADS,),
        in_specs=[zcol(COL_QB), zcol(COL_FB), zcol(COL_IB), zcol(COL_ZB),
                  pl.BlockSpec((hg_lower.shape[0], HG_DIM), lambda h: (0, h)),
                  pl.BlockSpec((1, HG_DIM), lambda h: (0, 0)),
                  sblk],
        out_specs=[pl.BlockSpec((nb, HG_DIM), lambda h: (0, h)), sblk],
        out_shape=[jax.ShapeDtypeStruct((nb, HG_WIDTH), f32),
                   jax.ShapeDtypeStruct(state.shape, f32)],
        scratch_shapes=[pltpu.VMEM((nb, HG_DIM), f32)],
        compiler_params=_cparams(("parallel",)),
        name="hgrn_sample",
    )(z, z, z, z, hg_lower, hg_norm, state)


N_PICK = N_SEL - 1


def _head_slopes():
    h = lax.broadcasted_iota(jnp.int32, (NSA_HEADS, 1), 0).astype(f32)
    return jnp.exp2(-(h + 1.0))


def _nsa_sample_cmp_kernel(q_ref, comp_ref, m_ref, oc_ref, ps_ref, *, t_pos):
    q = q_ref[0] * jnp.asarray(HEAD_DIM ** -0.5, bf16)
    slope = _head_slopes()
    head = lax.broadcasted_iota(jnp.int32, (NSA_HEADS, 1), 0)
    ncol = lax.broadcasted_iota(jnp.int32, (1, N_CMP_PAD), 1)
    d_c = t_pos - (ncol * CMP_STRIDE + (CMP_BLOCK - 1))
    mask = d_c >= 0
    o_c = jnp.zeros((NSA_HEADS, HEAD_DIM), f32)
    ps_ref[...] = jnp.zeros_like(ps_ref)
    for g in range(NSA_KV_HEADS):
        kc = comp_ref[0, g].astype(bf16)
        vc = comp_ref[0, NSA_KV_HEADS + g].astype(bf16)
        s = jnp.where(mask, _dot_nt(q, kc) - slope * d_c.astype(f32), NEG)
        m = jnp.max(s, axis=1, keepdims=True)
        e = jnp.where(mask, jnp.exp(s - m), 0.0)
        p = e * (1.0 / jnp.maximum(jnp.sum(e, axis=1, keepdims=True), 1e-30))
        mine = jnp.right_shift(head, NSA_GROUP.bit_length() - 1) == g
        o_c = jnp.where(mine, _dot(p.astype(bf16), vc), o_c)
        imp = jnp.sum(jnp.where(mine, p, 0.0), axis=0, keepdims=True)
        ps_ref[0, g:g + 1, :] = _importance_to_blocks(imp, m_ref[...])
    oc_ref[0] = o_c


def _nsa_sample_cmp(q8, comp, m_mat, t_pos):
    nb = q8.shape[0]
    return pl.pallas_call(
        functools.partial(_nsa_sample_cmp_kernel, t_pos=t_pos),
        grid=(nb,),
        in_specs=[pl.BlockSpec((1, NSA_HEADS, HEAD_DIM), lambda b: (b, 0, 0)),
                  pl.BlockSpec((1, N_COMBO, N_CMP_PAD, HEAD_DIM), lambda b: (b, 0, 0, 0)),
                  pl.BlockSpec((N_CMP_PAD, N_SELBLK), lambda b: (0, 0))],
        out_specs=[pl.BlockSpec((1, NSA_HEADS, HEAD_DIM), lambda b: (b, 0, 0)),
                   pl.BlockSpec((1, 8, N_SELBLK), lambda b: (b, 0, 0))],
        out_shape=[jax.ShapeDtypeStruct((nb, NSA_HEADS, HEAD_DIM), f32),
                   jax.ShapeDtypeStruct((nb, 8, N_SELBLK), f32)],
        compiler_params=_cparams(("parallel",)),
        name="nsa_sample_cmp",
    )(q8, comp, m_mat)


def _nsa_sample_topk_kernel(ps_ref, idx_ref, *, last_blk):
    ps = ps_ref[...]
    lane = lax.broadcasted_iota(jnp.int32, ps.shape, 1)
    score = ps + jnp.where((lane == 0) | (lane == last_blk), FORCE_BONUS, 0.0)
    out = jnp.zeros(ps.shape, jnp.int32)
    for r in range(N_PICK):
        best = jnp.max(score, axis=1, keepdims=True)
        idx = jnp.min(jnp.where(score == best, lane, N_SELBLK), axis=1, keepdims=True)
        out = jnp.where(lane == r, idx, out)
        score = jnp.where(lane == idx, REMOVED, score)
    idx_ref[...] = out


def _nsa_sample_topk(ps_rows, last_blk):
    return pl.pallas_call(
        functools.partial(_nsa_sample_topk_kernel, last_blk=last_blk),
        out_shape=jax.ShapeDtypeStruct(ps_rows.shape, jnp.int32),
        name="nsa_sample_topk",
    )(ps_rows)


def _nsa_sample_sel_kernel(pt_ref, idx_ref, slc_ref, q_ref, oc_ref, gate_ref, slc_new_ref, win_new_ref, wc_ref,
                           o_ref, wout_ref, kv_scr, sem_ref, *, t_pos):
    n_blk = NSA_KV_HEADS * N_PICK
    halves = PAGE_SIZE // SEL_BLOCK
    b = pl.program_id(0)
    nb = pl.num_programs(0)
    slot = lax.rem(b, 2)

    def blk_copy(seq, s, k):
        j = idx_ref[(seq * NSA_KV_HEADS + k // N_PICK) * N_SEL + k % N_PICK]
        src = pt_ref[seq, j // halves] * halves + j % halves
        return pltpu.make_async_copy(slc_ref.at[src], kv_scr.at[s, pl.ds(k * SEL_BLOCK, SEL_BLOCK)], sem_ref.at[s])

    def start_all(seq, s):
        for k in range(n_blk):
            blk_copy(seq, s, k).start()

    @pl.when(b == 0)
    def _():
        start_all(0, 0)

    @pl.when(b + 1 < nb)
    def _():
        start_all(b + 1, 1 - slot)

    for k in range(n_blk):
        blk_copy(b, slot, k).wait()

    q = q_ref[0] * jnp.asarray(HEAD_DIM ** -0.5, bf16)
    slope = _head_slopes()
    head = lax.broadcasted_iota(jnp.int32, (NSA_HEADS, 1), 0)
    n_keys = N_PICK * SEL_BLOCK
    lane = lax.broadcasted_iota(jnp.int32, (1, n_keys), 1)
    new_slc = slc_new_ref[0]
    new_win = win_new_ref[0]
    wc = wc_ref[0, 0]
    wl = wc.shape[0]
    wlane = lax.broadcasted_iota(jnp.int32, (1, wl), 1)
    d_w = (wl - wlane).astype(f32)
    qf = q.astype(f32)

    def attend(s, kv, new_row, kcols, vcols):
        k_new = new_row[:, kcols].astype(bf16).astype(f32)
        s_new = jnp.sum(qf * k_new, axis=1, keepdims=True)
        m = jnp.maximum(jnp.max(s, axis=1, keepdims=True), s_new)
        e, e_new = jnp.exp(s - m), jnp.exp(s_new - m)
        den = jnp.maximum(jnp.sum(e, axis=1, keepdims=True) + e_new, 1e-30)
        num = _dot(e.astype(bf16), kv[:, vcols].astype(bf16)) + \
            e_new.astype(bf16).astype(f32) * new_row[:, vcols].astype(bf16).astype(f32)
        return num * (1.0 / den)

    o_s = jnp.zeros((NSA_HEADS, HEAD_DIM), f32)
    o_w = jnp.zeros((NSA_HEADS, HEAD_DIM), f32)
    for g in range(NSA_KV_HEADS):
        mine = jnp.right_shift(head, NSA_GROUP.bit_length() - 1) == g
        kcols = slice(g * HEAD_DIM, (g + 1) * HEAD_DIM)
        vcols = slice(KV_WIDTH + g * HEAD_DIM, KV_WIDTH + (g + 1) * HEAD_DIM)
        pos = jnp.zeros((1, n_keys), jnp.int32)
        for n in range(N_PICK):
            j = idx_ref[(b * NSA_KV_HEADS + g) * N_SEL + n]
            in_blk = jnp.right_shift(lane, SEL_BLOCK.bit_length() - 1) == n
            pos = jnp.where(in_blk, j * SEL_BLOCK + jnp.bitwise_and(lane, SEL_BLOCK - 1), pos)
        kv = kv_scr[slot, g * n_keys:(g + 1) * n_keys, :]
        s = _dot_nt(q, kv[:, kcols].astype(bf16)) - slope * (t_pos - pos).astype(f32)
        o_s = jnp.where(mine, attend(s, kv, new_slc, kcols, vcols), o_s)
        s = _dot_nt(q, wc[:, kcols].astype(bf16)) - slope * d_w
        o_w = jnp.where(mine, attend(s, wc, new_win, kcols, vcols), o_w)

    gate = jnp.broadcast_to(_sigmoid(gate_ref[0]), (NSA_HEADS, LANES))
    glane = lax.broadcasted_iota(jnp.int32, (NSA_HEADS, LANES), 1)
    gcol = [jnp.sum(jnp.where(glane == 3 * head + j, gate, 0.0), axis=1, keepdims=True) for j in range(3)]
    o_ref[0] = gcol[0] * oc_ref[0] + gcol[1] * o_s + gcol[2] * o_w
    wrow = lax.broadcasted_iota(jnp.int32, (wl, 1), 0)
    wout_ref[0, 0] = jnp.where(wrow == wl - 1, new_win, pltpu.roll(wc, wl - 1, axis=0))


def _nsa_sample_sel(page_table, idx_flat, slc_halfpages, q8, o_c, gates, slc_new, win_new, cache_win, t_pos):
    nb = q8.shape[0]
    wl = cache_win.shape[2]
    n_blk = NSA_KV_HEADS * N_PICK
    per_seq = lambda b, pt, idx: (b, 0, 0)
    grid_spec = pltpu.PrefetchScalarGridSpec(
        num_scalar_prefetch=2,
        grid=(nb,),
        in_specs=[pl.BlockSpec(memory_space=pl.ANY),
                  pl.BlockSpec((1, NSA_HEADS, HEAD_DIM), per_seq),
                  pl.BlockSpec((1, NSA_HEADS, HEAD_DIM), per_seq),
                  pl.BlockSpec((1, 1, LANES), per_seq),
                  pl.BlockSpec((1, 1, 2 * KV_WIDTH), per_seq),
                  pl.BlockSpec((1, 1, 2 * KV_WIDTH), per_seq),
                  pl.BlockSpec((1, 1, wl, 2 * KV_WIDTH), lambda b, pt, idx: (0, b, 0, 0))],
        out_specs=[pl.BlockSpec((1, NSA_HEADS, HEAD_DIM), per_seq),
                   pl.BlockSpec((1, 1, wl, 2 * KV_WIDTH), lambda b, pt, idx: (0, b, 0, 0))],
        scratch_shapes=[pltpu.VMEM((2, n_blk * SEL_BLOCK, 2 * KV_WIDTH), f32),
                        pltpu.SemaphoreType.DMA((2,))],
    )
    return pl.pallas_call(
        functools.partial(_nsa_sample_sel_kernel, t_pos=t_pos),
        grid_spec=grid_spec,
        out_shape=[jax.ShapeDtypeStruct((nb, NSA_HEADS, HEAD_DIM), f32),
                   jax.ShapeDtypeStruct(cache_win.shape, f32)],
        compiler_params=_cparams(("arbitrary",)),
        name="nsa_sample_sel",
    )(page_table, idx_flat, slc_halfpages, q8, o_c, gates, slc_new, win_new, cache_win)


def kernel(x_prompt, x_sample, cache_cmp_kv, cache_slc_kv, cache_win_kv, state_hgrn, page_table, p_prompt, p_sample,
           w_in, g_pre, cmp_pe, cmp_w1, cmp_b1, cmp_w2, hg_lower, hg_norm, w_out, g_post, ple_proj, ple_gate):
    bsz, seq, _ = x_prompt.shape
    nb = x_sample.shape[0]
    n_pool = cache_cmp_kv.shape[1]
    n_pages = page_table.shape[1]
    past = n_pages * PAGE_SIZE
    nq = seq // Q_BLOCK
    kv_tail = (2, NSA_KV_HEADS, HEAD_DIM)

    o = _ORIG
    w = w_in[0]
    w_r = jnp.concatenate([w[:, o['q_a']:o['k_cmp']], w[:, o['z_a']:o['end']], w[:, o['k_cmp']:o['gate_a']],
                           w[:, o['gate_a']:o['z_a']],
                           jnp.zeros((D_MODEL, PROJ_WIDTH - o['end']), f32)], axis=1).astype(bf16)
    cw = _compress_weights(cmp_pe[0], cmp_w1[0], cmp_b1[0], cmp_w2[0])
    m_mat = _sel_map_matrix()
    w_out_b, ple_proj_b, ple_gate_b = w_out[0].astype(bf16), ple_proj[0].astype(bf16), ple_gate[0].astype(bf16)

    xp = x_prompt.reshape(bsz * seq, D_MODEL)
    zp = _proj(xp, g_pre, w_r, 1024)
    cmp_p = zp[:, COL_CMP:COL_CMP + 2 * KV_WIDTH]
    slc_p = zp[:, COL_SLC:COL_SLC + 2 * KV_WIDTH]
    win_p = zp[:, COL_WIN:COL_WIN + 2 * KV_WIDTH]
    pages_p = cmp_p.reshape(bsz * seq // PAGE_SIZE, SEGS_PER_PAGE, SEG_WIDTH)
    table_p = jnp.arange(bsz * seq // PAGE_SIZE, dtype=jnp.int32).reshape(bsz, seq // PAGE_SIZE)
    comp_p = _compress(pages_p, table_p, *cw)
    q_st = zp[:, COL_QA:COL_QA + NSA_WIDTH].astype(bf16)
    q_st = q_st.reshape(bsz, nq, Q_BLOCK, NSA_KV_HEADS, NSA_GROUP, HEAD_DIM).transpose(0, 1, 3, 4, 2, 5)
    q_st = q_st.reshape(bsz, nq, NSA_KV_HEADS, NSA_GROUP * Q_BLOCK, HEAD_DIM)
    o_st = _nsa_prompt(q_st, zp, comp_p, _attention_kv(slc_p, bsz, seq), _attention_kv(win_p, bsz, seq),
                       m_mat, _expand_matrix(seq // SEL_CHUNK))
    o_a = o_st.reshape(bsz, nq, NSA_KV_HEADS, NSA_GROUP, Q_BLOCK, HEAD_DIM).transpose(0, 1, 4, 2, 3, 5)
    o_a = o_a.reshape(bsz * seq, NSA_WIDTH)
    yb_p, hg_p = _hgrn_prompt(zp, hg_lower, hg_norm, bsz, seq)
    h_p = _finish(xp, o_a, zp, yb_p, p_prompt[0].reshape(bsz * seq, PLE_DIM),
                  w_out_b, g_post, ple_proj_b, ple_gate_b, 512)

    wl_p = min(WINDOW, seq)
    out_prompt = (h_p.reshape(bsz, seq, D_MODEL),
                  cmp_p.reshape((1, bsz, seq) + kv_tail),
                  slc_p.reshape((1, bsz, seq) + kv_tail),
                  win_p.reshape((bsz, seq) + kv_tail)[:, seq - wl_p:][None],
                  hg_p[None])

    xs = x_sample.reshape(nb, D_MODEL)
    zs = _proj(xs, g_pre, w_r, nb)
    comp_s = _compress(cache_cmp_kv[0].reshape(n_pool, SEGS_PER_PAGE, SEG_WIDTH), page_table, *cw)
    q8 = zs[:, COL_QA:COL_QA + NSA_WIDTH].astype(bf16).reshape(nb, NSA_HEADS, HEAD_DIM)
    oc_s, ps_s = _nsa_sample_cmp(q8, comp_s, m_mat, past)
    ps_rows = ps_s[:, :NSA_KV_HEADS, :].reshape(nb * NSA_KV_HEADS, N_SELBLK)
    idx = _nsa_sample_topk(ps_rows, past // SEL_BLOCK - 1)
    gates_s = zs[:, COL_GATE:COL_GATE + LANES].reshape(nb, 1, LANES)
    slc_half = cache_slc_kv[0].reshape(n_pool * (PAGE_SIZE // SEL_BLOCK), SEL_BLOCK, 2 * KV_WIDTH)
    wl_s = cache_win_kv.shape[2]
    oa_s, win_s = _nsa_sample_sel(page_table, idx[:, :N_SEL].reshape(-1), slc_half, q8, oc_s, gates_s,
                                  zs[:, COL_SLC:COL_SLC + 2 * KV_WIDTH].reshape(nb, 1, 2 * KV_WIDTH),
                                  zs[:, COL_WIN:COL_WIN + 2 * KV_WIDTH].reshape(nb, 1, 2 * KV_WIDTH),
                                  cache_win_kv.reshape(1, nb, wl_s, 2 * KV_WIDTH), past)
    yb_s, hg_s = _hgrn_sample(zs, state_hgrn, hg_lower, hg_norm)
    h_s = _finish(xs, oa_s.reshape(nb, NSA_WIDTH), zs, yb_s, p_sample[0].reshape(nb, PLE_DIM),
                  w_out_b, g_post, ple_proj_b, ple_gate_b, nb)

    out_sample = (h_s.reshape(nb, 1, D_MODEL),
                  zs[:, COL_CMP:COL_CMP + 2 * KV_WIDTH].reshape((1, nb, 1) + kv_tail),
                  zs[:, COL_SLC:COL_SLC + 2 * KV_WIDTH].reshape((1, nb, 1) + kv_tail),
                  win_s.reshape((1, nb, wl_s) + kv_tail),
                  hg_s)
    return (out_prompt[0], out_sample[0]) + out_prompt[1:] + out_sample[1:]
```

```python
import functools

import numpy as np
import jax
import jax.numpy as jnp
from jax import lax
from jax.experimental import pallas as pl
from jax.experimental.pallas import tpu as pltpu

D_MODEL = 1024
PAGE_SIZE = 128
NSA_HEADS = 8
NSA_KV_HEADS = 2
NSA_GROUP = NSA_HEADS // NSA_KV_HEADS
HEAD_DIM = 64
NSA_WIDTH = NSA_HEADS * HEAD_DIM
KV_WIDTH = NSA_KV_HEADS * HEAD_DIM
CMP_BLOCK = 32
CMP_STRIDE = 16
CMP_RATIO = CMP_BLOCK // CMP_STRIDE
CMP_HIDDEN = 128
SEL_BLOCK = 64
SEL_RATIO = SEL_BLOCK // CMP_STRIDE
N_SEL = 16
WINDOW = 512
Q_BLOCK = 128
HG_HEADS = 4
HG_DIM = 128
HG_WIDTH = HG_HEADS * HG_DIM
PLE_DIM = 256
EPS = 1e-6
NEG = -1e30
FORCE_BONUS = 1e6
REMOVED = -3e38

COL_QA, COL_ZA, COL_QB, COL_FB, COL_IB, COL_ZB = 0, 512, 1024, 1536, 2048, 2560
COL_CMP, COL_SLC, COL_WIN, COL_GATE = 3072, 3328, 3584, 3840
PROJ_WIDTH = 4096
_ORIG = dict(q_a=0, k_cmp=512, gate_a=1280, z_a=1304, q_b=1816, f_b=2328, i_b=2840, z_b=3352, end=3864)

LANES = 128
VMEM_LIMIT = 56 * 1024 * 1024

SEL_CHUNK = 512
WIN_KEYS = WINDOW + Q_BLOCK
N_CMP_PAD = 512
N_SELBLK = 128

f32 = jnp.float32
bf16 = jnp.bfloat16


def _cparams(sem):
    return pltpu.CompilerParams(dimension_semantics=sem, vmem_limit_bytes=VMEM_LIMIT)


def _dot(a, b):
    return jnp.dot(a, b, preferred_element_type=f32)


def _dot_nt(a, b):
    return lax.dot_general(a, b, (((1,), (1,)), ((), ())), preferred_element_type=f32)


def _dot_tn(a, b):
    return lax.dot_general(a, b, (((0,), (0,)), ((), ())), preferred_element_type=f32)


def _sigmoid(x):
    return 1.0 / (1.0 + jnp.exp(-x))


def _silu(x):
    return x * _sigmoid(x)


def _proj_kernel(x_ref, g_ref, w_ref, z_ref, xn_ref):
    @pl.when(pl.program_id(1) == 0)
    def _():
        x = x_ref[...]
        y = x * lax.rsqrt(jnp.mean(x * x, axis=-1, keepdims=True) + EPS)
        xn_ref[...] = (y * g_ref[...]).astype(bf16)

    z_ref[...] = _dot(xn_ref[...], w_ref[...])


def _proj(x, g, w, tm):
    n = x.shape[0]
    tn = 1024
    return pl.pallas_call(
        _proj_kernel,
        grid=(n // tm, PROJ_WIDTH // tn),
        in_specs=[pl.BlockSpec((tm, D_MODEL), lambda i, j: (i, 0)),
                  pl.BlockSpec((1, D_MODEL), lambda i, j: (0, 0)),
                  pl.BlockSpec((D_MODEL, tn), lambda i, j: (0, j))],
        out_specs=pl.BlockSpec((tm, tn), lambda i, j: (i, j)),
        out_shape=jax.ShapeDtypeStruct((n, PROJ_WIDTH), f32),
        scratch_shapes=[pltpu.VMEM((tm, D_MODEL), bf16)],
        compiler_params=_cparams(("parallel", "arbitrary")),
        name="proj",
    )(x, g, w)


def _finish_kernel(x_ref, oa_ref, za_ref, yb_ref, p_ref, wo_ref, gp_ref, pp_ref, pg_ref, h_ref):
    ya = (oa_ref[...] * _silu(za_ref[...])).astype(bf16)
    yb = yb_ref[...].astype(bf16)
    mix = _dot(ya, wo_ref[0:NSA_WIDTH, :]) + _dot(yb, wo_ref[NSA_WIDTH:, :])
    nrm = mix * lax.rsqrt(jnp.mean(mix * mix, axis=-1, keepdims=True) + EPS) * gp_ref[...]
    h = x_ref[...] + nrm
    gate = _sigmoid(_dot(h.astype(bf16), pg_ref[...]))
    h_ref[...] = h + _dot(p_ref[...].astype(bf16), pp_ref[...]) * gate


def _finish(x, oa, z, yb, p, w_out, g_post, ple_proj, ple_gate, tm):
    n = x.shape[0]
    za_blk = COL_ZA // NSA_WIDTH
    row = lambda i: (i, 0)
    const = lambda i: (0, 0)
    return pl.pallas_call(
        _finish_kernel,
        grid=(n // tm,),
        in_specs=[pl.BlockSpec((tm, D_MODEL), row),
                  pl.BlockSpec((tm, NSA_WIDTH), row),
                  pl.BlockSpec((tm, NSA_WIDTH), lambda i: (i, za_blk)),
                  pl.BlockSpec((tm, HG_WIDTH), row),
                  pl.BlockSpec((tm, PLE_DIM), row),
                  pl.BlockSpec((D_MODEL, D_MODEL), const),
                  pl.BlockSpec((1, D_MODEL), const),
                  pl.BlockSpec((PLE_DIM, D_MODEL), const),
                  pl.BlockSpec((D_MODEL, D_MODEL), const)],
        out_specs=pl.BlockSpec((tm, D_MODEL), row),
        out_shape=jax.ShapeDtypeStruct((n, D_MODEL), f32),
        compiler_params=_cparams(("parallel",)),
        name="finish",
    )(x, oa, z, yb, p, w_out, g_post, ple_proj, ple_gate)


SEGS_PER_PAGE = PAGE_SIZE // CMP_STRIDE
SEG_WIDTH = CMP_STRIDE * 2 * KV_WIDTH
N_COMBO = 2 * NSA_KV_HEADS


def _compress_kernel(pt_ref, pages_ref, wa_ref, wb_ref, pea_ref, peb_ref, b1_ref, w2_ref, out_ref,
                     buf_ref, sem_ref, *, n_pages):
    b = pl.program_id(0)
    nb = pl.num_programs(0)
    slot = lax.rem(b, 2)

    def page_copy(seq, s, j):
        return pltpu.make_async_copy(pages_ref.at[pt_ref[seq, j]],
                                     buf_ref.at[s, pl.ds(j * SEGS_PER_PAGE, SEGS_PER_PAGE)],
                                     sem_ref.at[s])

    def start_all(seq, s):
        def body(j, c):
            page_copy(seq, s, j).start()
            return c
        lax.fori_loop(0, n_pages, body, 0)

    @pl.when(b == 0)
    def _():
        start_all(0, 0)

    @pl.when(b + 1 < nb)
    def _():
        start_all(b + 1, 1 - slot)

    def wait_body(j, c):
        page_copy(b, slot, j).wait()
        return c
    lax.fori_loop(0, n_pages, wait_body, 0)

    seg = buf_ref[slot]
    n_seg = seg.shape[0]
    pa = _dot((seg + pea_ref[...]).astype(bf16), wa_ref[...])
    pb = _dot((seg + peb_ref[...]).astype(bf16), wb_ref[...])
    pb = pltpu.roll(pb, n_seg - 1, axis=0)
    hid = _silu(pa + pb + b1_ref[...])
    for c in range(N_COMBO):
        hc = hid[:, c * CMP_HIDDEN:(c + 1) * CMP_HIDDEN].astype(bf16)
        out_ref[0, c] = _dot(hc, w2_ref[c // NSA_KV_HEADS])


def _compress(pages, page_table, wa, wb, pea, peb, b1, w2):
    nseq, n_pages = page_table.shape
    n_seg = n_pages * SEGS_PER_PAGE
    const2 = lambda b, pt: (0, 0)
    grid_spec = pltpu.PrefetchScalarGridSpec(
        num_scalar_prefetch=1,
        grid=(nseq,),
        in_specs=[pl.BlockSpec(memory_space=pl.ANY),
                  pl.BlockSpec((SEG_WIDTH, N_COMBO * CMP_HIDDEN), const2),
                  pl.BlockSpec((SEG_WIDTH, N_COMBO * CMP_HIDDEN), const2),
                  pl.BlockSpec((1, SEG_WIDTH), const2),
                  pl.BlockSpec((1, SEG_WIDTH), const2),
                  pl.BlockSpec((1, N_COMBO * CMP_HIDDEN), const2),
                  pl.BlockSpec((2, CMP_HIDDEN, HEAD_DIM), lambda b, pt: (0, 0, 0))],
        out_specs=pl.BlockSpec((1, N_COMBO, n_seg, HEAD_DIM), lambda b, pt: (b, 0, 0, 0)),
        scratch_shapes=[pltpu.VMEM((2, n_seg, SEG_WIDTH), f32),
                        pltpu.SemaphoreType.DMA((2,))],
    )
    return pl.pallas_call(
        functools.partial(_compress_kernel, n_pages=n_pages),
        grid_spec=grid_spec,
        out_shape=jax.ShapeDtypeStruct((nseq, N_COMBO, n_seg, HEAD_DIM), f32),
        compiler_params=_cparams(("arbitrary",)),
        name="compress",
    )(page_table, pages, wa, wb, pea, peb, b1, w2)


def _compress_weights(cmp_pe, cmp_w1, cmp_b1, cmp_w2):
    half = CMP_STRIDE * HEAD_DIM
    eye = jnp.eye(NSA_KV_HEADS, dtype=f32)

    def big(w_half):
        t = jnp.einsum('kjdh,ka,gb->jabdkgh', w_half, jnp.eye(2, dtype=f32), eye)
        return t.reshape(SEG_WIDTH, N_COMBO * CMP_HIDDEN)

    w1 = cmp_w1.reshape(2, 2, CMP_STRIDE, HEAD_DIM, CMP_HIDDEN)
    wa = big(w1[:, 0]).astype(bf16)
    wb = big(w1[:, 1]).astype(bf16)
    pe = cmp_pe.reshape(2, 2, CMP_STRIDE, HEAD_DIM)

    def pe_row(p):
        t = jnp.broadcast_to(p.transpose(1, 0, 2)[:, :, None, :],
                             (CMP_STRIDE, 2, NSA_KV_HEADS, HEAD_DIM))
        return t.reshape(1, SEG_WIDTH)

    pea, peb = pe_row(pe[:, 0]), pe_row(pe[:, 1])
    b1 = jnp.broadcast_to(cmp_b1[:, None, :], (2, NSA_KV_HEADS, CMP_HIDDEN)).reshape(1, N_COMBO * CMP_HIDDEN)
    return wa, wb, pea, peb, b1, cmp_w2.astype(bf16)


def _sel_map_matrix():
    wts = np.convolve(np.ones(SEL_RATIO), np.ones(CMP_RATIO))
    m = np.zeros((N_CMP_PAD, N_SELBLK), np.float32)
    for j in range(N_SELBLK):
        for k, wk in enumerate(wts):
            n = SEL_RATIO * j + k
            if n < N_CMP_PAD - 1:
                m[n, j] = wk
    return jnp.asarray(m, dtype=bf16)


def _expand_matrix(n_chunks):
    key = np.arange(n_chunks * SEL_CHUNK) // SEL_BLOCK
    e = (np.arange(N_SELBLK)[:, None] == key[None, :]).astype(np.float32)
    return jnp.asarray(e.reshape(N_SELBLK, n_chunks, SEL_CHUNK).transpose(1, 0, 2), dtype=bf16)


def _importance_to_blocks(imp, m):
    hi = imp.astype(bf16)
    lo = (imp - hi.astype(f32)).astype(bf16)
    return _dot(hi, m) + _dot(lo, m)


def _nsa_prompt_kernel(q_ref, gate_ref, kc_ref, vc_ref, kvs_ref, kvw_ref, m_ref, e_ref, o_ref,
                       acc_ref, mx_ref, used_ref):
    g = pl.program_id(1)
    qi = pl.program_id(2)
    q0 = qi * Q_BLOCK
    rows = NSA_GROUP * Q_BLOCK

    q64 = q_ref[0, 0, 0] * jnp.asarray(HEAD_DIM ** -0.5, bf16)
    q128 = jnp.concatenate([q64, jnp.zeros_like(q64)], axis=1)
    slope = [jnp.where(g == 0, 2.0 ** -(r + 1), 2.0 ** -(NSA_GROUP + r + 1)).astype(f32)
             for r in range(NSA_GROUP)]

    qrow = lax.broadcasted_iota(jnp.int32, (Q_BLOCK, 1), 0)

    def softmax_rows(s, dist, mask):
        es, ls = [], []
        for r in range(NSA_GROUP):
            sr = jnp.where(mask, s[r * Q_BLOCK:(r + 1) * Q_BLOCK] - slope[r] * dist, NEG)
            m = jnp.max(sr, axis=1, keepdims=True)
            e = jnp.where(mask, jnp.exp(sr - m), 0.0)
            es.append(e)
            ls.append(1.0 / jnp.maximum(jnp.sum(e, axis=1, keepdims=True), 1e-30))
        return es, ls

    kc = kc_ref[0, 0].astype(bf16)
    vc = vc_ref[0, 0].astype(bf16)
    ncol = lax.broadcasted_iota(jnp.int32, (1, N_CMP_PAD), 1)
    d_c = (q0 + qrow) - (ncol * CMP_STRIDE + (CMP_BLOCK - 1))
    es, ls = softmax_rows(_dot_nt(q64, kc), d_c.astype(f32), d_c >= 0)
    ps_c = [e * l for e, l in zip(es, ls)]
    o_c = _dot(jnp.concatenate(ps_c, axis=0).astype(bf16), vc)
    imp = ps_c[0] + ps_c[1] + ps_c[2] + ps_c[3]
    p_sel = _importance_to_blocks(imp, m_ref[...])

    ps_t = p_sel.T
    blk = lax.broadcasted_iota(jnp.int32, (N_SELBLK, Q_BLOCK), 0)
    tq = q0 + lax.broadcasted_iota(jnp.int32, (N_SELBLK, Q_BLOCK), 1)
    jt = jnp.right_shift(tq, SEL_BLOCK.bit_length() - 1)
    valid = blk <= jt
    forced = (blk == 0) | (blk == jt) | (blk == jt - 1)
    score = jnp.where(valid, ps_t + jnp.where(forced, FORCE_BONUS, 0.0), NEG)
    sel_t = jnp.zeros((N_SELBLK, Q_BLOCK), f32)
    for _ in range(N_SEL):
        best = jnp.max(score, axis=0, keepdims=True)
        idx = jnp.min(jnp.where(score == best, blk, N_SELBLK), axis=0, keepdims=True)
        hit = blk == idx
        sel_t = jnp.where(hit & (best > NEG * 0.5), 1.0, sel_t)
        score = jnp.where(hit, REMOVED, score)
    sel = sel_t.T.astype(bf16)

    acc_ref[...] = jnp.zeros_like(acc_ref)
    mx_ref[...] = jnp.full_like(mx_ref, NEG)
    kcol = lax.broadcasted_iota(jnp.int32, (1, SEL_CHUNK), 1)
    d_base = (qrow - kcol).astype(f32)

    blk_per_chunk = SEL_CHUNK // SEL_BLOCK
    for c in range(e_ref.shape[0]):
        picked = jnp.max(sel_t[c * blk_per_chunk:(c + 1) * blk_per_chunk, :])
        used_ref[c] = (picked > 0.5).astype(jnp.int32)

    def sel_chunk(c, carry):
        @pl.when(used_ref[c] > 0)
        def _():
            sel_chunk_body(c)
        return carry

    def sel_chunk_body(c):
        k0 = pl.multiple_of(c * SEL_CHUNK, SEL_CHUNK)
        kv = kvs_ref[0, 0, pl.ds(k0, SEL_CHUNK), :]
        s = _dot_nt(q128, kv[:, 0:LANES])
        dist = d_base + (q0 - k0).astype(f32)
        allowed = (_dot(sel, e_ref[c]) > 0.5) & (dist >= 0.0)
        ps = []
        for r in range(NSA_GROUP):
            rs = slice(r * Q_BLOCK, (r + 1) * Q_BLOCK)
            sr = jnp.where(allowed, s[rs] - slope[r] * dist, NEG)
            m_old = mx_ref[rs]
            m_new = jnp.maximum(m_old, jnp.max(sr, axis=1, keepdims=True))
            acc_ref[rs] = acc_ref[rs] * jnp.exp(m_old - m_new)
            mx_ref[rs] = m_new
            ps.append(jnp.exp(sr - m_new).astype(bf16))
        acc_ref[...] += _dot(jnp.concatenate(ps, axis=0), kv[:, LANES:])

    n_chunks = (q0 + Q_BLOCK + SEL_CHUNK - 1) // SEL_CHUNK
    lax.fori_loop(0, n_chunks, sel_chunk, 0)
    acc = acc_ref[...]
    o_s = acc[:, 0:HEAD_DIM] * (1.0 / jnp.maximum(acc[:, HEAD_DIM:HEAD_DIM + 1], 1e-30))

    w0 = pl.multiple_of(jnp.maximum(q0 - WINDOW, 0), Q_BLOCK)
    kvw = kvw_ref[0, 0, pl.ds(w0, WIN_KEYS), :]
    wcol = lax.broadcasted_iota(jnp.int32, (1, WIN_KEYS), 1)
    d_w = (q0 - w0) + qrow - wcol
    es, _ = softmax_rows(_dot_nt(q128, kvw[:, 0:LANES]), d_w.astype(f32), (d_w >= 0) & (d_w <= WINDOW))
    acc_w = _dot(jnp.concatenate(es, axis=0).astype(bf16), kvw[:, LANES:])
    o_w = acc_w[:, 0:HEAD_DIM] * (1.0 / jnp.maximum(acc_w[:, HEAD_DIM:HEAD_DIM + 1], 1e-30))

    gate = _sigmoid(gate_ref[...])

    def gate_col(r, j):
        lo = gate[:, r * 3 + j:r * 3 + j + 1]
        hi = gate[:, (NSA_GROUP + r) * 3 + j:(NSA_GROUP + r) * 3 + j + 1]
        return jnp.where(g == 0, lo, hi)

    for r in range(NSA_GROUP):
        rs = slice(r * Q_BLOCK, (r + 1) * Q_BLOCK)
        o_ref[0, 0, 0, rs, :] = gate_col(r, 0) * o_c[rs] + gate_col(r, 1) * o_s[rs] + gate_col(r, 2) * o_w[rs]


def _nsa_prompt(q_st, z, comp, kv_slc, kv_win, m_mat, e_mat):
    bsz, nq = q_st.shape[0], q_st.shape[1]
    seq = kv_slc.shape[2]
    rows = NSA_GROUP * Q_BLOCK
    gate_blk = COL_GATE // LANES
    return pl.pallas_call(
        _nsa_prompt_kernel,
        grid=(bsz, NSA_KV_HEADS, nq),
        in_specs=[pl.BlockSpec((1, 1, 1, rows, HEAD_DIM), lambda b, g, i: (b, i, g, 0, 0)),
                  pl.BlockSpec((Q_BLOCK, LANES), lambda b, g, i: (b * nq + i, gate_blk)),
                  pl.BlockSpec((1, 1, N_CMP_PAD, HEAD_DIM), lambda b, g, i: (b, g, 0, 0)),
                  pl.BlockSpec((1, 1, N_CMP_PAD, HEAD_DIM), lambda b, g, i: (b, NSA_KV_HEADS + g, 0, 0)),
                  pl.BlockSpec((1, 1, seq, 2 * LANES), lambda b, g, i: (b, g, 0, 0)),
                  pl.BlockSpec((1, 1, seq, 2 * LANES), lambda b, g, i: (b, g, 0, 0)),
                  pl.BlockSpec((N_CMP_PAD, N_SELBLK), lambda b, g, i: (0, 0)),
                  pl.BlockSpec(e_mat.shape, lambda b, g, i: (0, 0, 0))],
        out_specs=pl.BlockSpec((1, 1, 1, rows, HEAD_DIM), lambda b, g, i: (b, i, g, 0, 0)),
        out_shape=jax.ShapeDtypeStruct((bsz, nq, NSA_KV_HEADS, rows, HEAD_DIM), f32),
        scratch_shapes=[pltpu.VMEM((rows, LANES), f32), pltpu.VMEM((rows, 1), f32),
                        pltpu.SMEM((e_mat.shape[0],), jnp.int32)],
        compiler_params=_cparams(("parallel", "parallel", "arbitrary")),
        name="nsa_prompt",
    )(q_st, z, comp, comp, kv_slc, kv_win, m_mat, e_mat)


def _attention_kv(piece, bsz, seq):
    t = piece.reshape(bsz, seq, 2, NSA_KV_HEADS, HEAD_DIM).transpose(2, 0, 3, 1, 4)
    zeros = jnp.zeros_like(t[0])
    ones = jnp.ones(t[0].shape[:-1] + (1,), f32)
    out = jnp.concatenate([t[0], zeros, t[1], ones, zeros[..., :HEAD_DIM - 1]], axis=-1)
    return out.astype(bf16)


HG_CHUNK = 128
HG_BAND = 8


def _lower_bound(lo):
    m = jnp.max(lo, axis=0, keepdims=True)
    e = jnp.exp(lo - m)
    return e[0:1] / jnp.sum(e, axis=0, keepdims=True)


def _hgrn_gates(q, fl, lb):
    f = lb + (1.0 - lb) * _sigmoid(fl)
    k = (1.0 - lb) * _sigmoid(-fl)
    return _silu(q), k, jnp.log(f)


def _hgrn_out(o, nw, zb):
    o = o * lax.rsqrt(jnp.mean(o * o, axis=-1, keepdims=True) + EPS) * nw
    return o * _silu(zb)


def _hgrn_prompt_kernel(q_ref, f_ref, i_ref, zb_ref, lo_ref, nw_ref, y_ref, s_ref, st_ref):
    c = pl.program_id(2)
    n = HG_CHUNK

    @pl.when(c == 0)
    def _():
        st_ref[...] = jnp.zeros_like(st_ref)

    lb = _lower_bound(lo_ref[...])
    qq, kk, logf = _hgrn_gates(q_ref[...], f_ref[...], lb)
    v = i_ref[...]
    row = lax.broadcasted_iota(jnp.int32, (n, n), 0)
    col = lax.broadcasted_iota(jnp.int32, (n, n), 1)

    b = logf
    sh = 1
    while sh < n:
        b = b + jnp.where(row >= sh, pltpu.roll(b, sh, axis=0), 0.0)
        sh *= 2

    a = jnp.zeros((n, n), f32)
    sub = jnp.bitwise_and(row, HG_BAND - 1)
    for d in range(HG_BAND):
        if d == 0:
            term = qq * kk
        else:
            ok = sub >= d
            decay = jnp.exp(jnp.where(ok, b - pltpu.roll(b, d, axis=0), 0.0))
            term = jnp.where(ok, qq * pltpu.roll(kk, d, axis=0) * decay, 0.0)
        a = a + jnp.where(col == row - d, jnp.sum(term, axis=1, keepdims=True), 0.0)

    size = 2 * HG_BAND
    while size <= n:
        half = size // 2
        qs, ks = [], []
        zero = jnp.zeros((half, HG_DIM), f32)
        for r0 in range(0, n, size):
            beta = b[r0 + half - 1:r0 + half]
            ks += [kk[r0:r0 + half] * jnp.exp(beta - b[r0:r0 + half]), zero]
            qs += [zero, qq[r0 + half:r0 + size] * jnp.exp(b[r0 + half:r0 + size] - beta)]
        part = _dot_nt(jnp.concatenate(qs, axis=0).astype(bf16), jnp.concatenate(ks, axis=0).astype(bf16))
        if size < n:
            sh = size.bit_length() - 1
            part = jnp.where(jnp.right_shift(row, sh) == jnp.right_shift(col, sh), part, 0.0)
        a = a + part
        size *= 2

    st = st_ref[...]
    o = _dot(a.astype(bf16), v.astype(bf16)) + _dot_nt((qq * jnp.exp(b)).astype(bf16), st.astype(bf16))
    b_last = b[n - 1:n]
    k_tail = kk * jnp.exp(b_last - b)
    st_new = st * jnp.exp(b_last) + _dot_tn(v.astype(bf16), k_tail.astype(bf16))
    st_ref[...] = st_new
    y_ref[...] = _hgrn_out(o, nw_ref[...], zb_ref[...])

    @pl.when(c == pl.num_programs(2) - 1)
    def _():
        s_ref[0, 0] = st_new.T


def _hgrn_prompt(z, hg_lower, hg_norm, bsz, seq):
    nch = seq // HG_CHUNK

    def zcol(base):
        return pl.BlockSpec((HG_CHUNK, HG_DIM), lambda b, h, c: (b * nch + c, base // HG_DIM + h))

    return pl.pallas_call(
        _hgrn_prompt_kernel,
        grid=(bsz, HG_HEADS, nch),
        in_specs=[zcol(COL_QB), zcol(COL_FB), zcol(COL_IB), zcol(COL_ZB),
                  pl.BlockSpec((hg_lower.shape[0], HG_DIM), lambda b, h, c: (0, h)),
                  pl.BlockSpec((1, HG_DIM), lambda b, h, c: (0, 0))],
        out_specs=[pl.BlockSpec((HG_CHUNK, HG_DIM), lambda b, h, c: (b * nch + c, h)),
                   pl.BlockSpec((1, 1, HG_DIM, HG_DIM), lambda b, h, c: (b, h, 0, 0))],
        out_shape=[jax.ShapeDtypeStruct((bsz * seq, HG_WIDTH), f32),
                   jax.ShapeDtypeStruct((bsz, HG_HEADS, HG_DIM, HG_DIM), f32)],
        scratch_shapes=[pltpu.VMEM((HG_DIM, HG_DIM), f32)],
        compiler_params=_cparams(("parallel", "parallel", "arbitrary")),
        name="hgrn_prompt",
    )(z, z, z, z, hg_lower, hg_norm)


def _hgrn_sample_kernel(q_ref, f_ref, i_ref, zb_ref, lo_ref, nw_ref, s_ref, y_ref, so_ref, o_scr):
    nb = q_ref.shape[0]
    lb = _lower_bound(lo_ref[...])
    qq, kk, logf = _hgrn_gates(q_ref[...], f_ref[...], lb)
    v = i_ref[...]
    pad = jnp.zeros((HG_DIM - nb, HG_DIM), f32)
    to_cols = lambda x: jnp.concatenate([x, pad], axis=0).T
    f_t, k_t, q_t = to_cols(jnp.exp(logf)), to_cols(kk), to_cols(qq)
    for b in range(nb):
        s_new = f_t[:, b:b + 1] * s_ref[0, b, 0] + k_t[:, b:b + 1] * v[b:b + 1, :]
        so_ref[0, b, 0] = s_new
        o_scr[b:b + 1, :] = jnp.sum(s_new * q_t[:, b:b + 1], axis=0, keepdims=True)
    y_ref[...] = _hgrn_out(o_scr[...], nw_ref[...], zb_ref[...])


def _hgrn_sample(z, state, hg_lower, hg_norm):
    nb = z.shape[0]

    def zcol(base):
        return pl.BlockSpec((nb, HG_DIM), lambda h: (0, base // HG_DIM + h))

    sblk = pl.BlockSpec((1, nb, 1, HG_DIM, HG_DIM), lambda h: (0, 0, h, 0, 0))
    return pl.pallas_call(
        _hgrn_sample_kernel,
        grid=(HG_HEADS,),
        in_specs=[zcol(COL_QB), zcol(COL_FB), zcol(COL_IB), zcol(COL_ZB),
                  pl.BlockSpec((hg_lower.shape[0], HG_DIM), lambda h: (0, h)),
                  pl.BlockSpec((1, HG_DIM), lambda h: (0, 0)),
                  sblk],
        out_specs=[pl.BlockSpec((nb, HG_DIM), lambda h: (0, h)), sblk],
        out_shape=[jax.ShapeDtypeStruct((nb, HG_WIDTH), f32),
                   jax.ShapeDtypeStruct(state.shape, f32)],
        scratch_shapes=[pltpu.VMEM((nb, HG_DIM), f32)],
        compiler_params=_cparams(("parallel",)),
        name="hgrn_sample",
    )(z, z, z, z, hg_lower, hg_norm, state)


N_PICK = N_SEL - 1


def _head_slopes():
    h = lax.broadcasted_iota(jnp.int32, (NSA_HEADS, 1), 0).astype(f32)
    return jnp.exp2(-(h + 1.0))


def _nsa_sample_cmp_kernel(q_ref, comp_ref, m_ref, oc_ref, ps_ref, *, t_pos):
    q = q_ref[0] * jnp.asarray(HEAD_DIM ** -0.5, bf16)
    slope = _head_slopes()
    head = lax.broadcasted_iota(jnp.int32, (NSA_HEADS, 1), 0)
    ncol = lax.broadcasted_iota(jnp.int32, (1, N_CMP_PAD), 1)
    d_c = t_pos - (ncol * CMP_STRIDE + (CMP_BLOCK - 1))
    mask = d_c >= 0
    o_c = jnp.zeros((NSA_HEADS, HEAD_DIM), f32)
    ps_ref[...] = jnp.zeros_like(ps_ref)
    for g in range(NSA_KV_HEADS):
        kc = comp_ref[0, g].astype(bf16)
        vc = comp_ref[0, NSA_KV_HEADS + g].astype(bf16)
        s = jnp.where(mask, _dot_nt(q, kc) - slope * d_c.astype(f32), NEG)
        m = jnp.max(s, axis=1, keepdims=True)
        e = jnp.where(mask, jnp.exp(s - m), 0.0)
        p = e * (1.0 / jnp.maximum(jnp.sum(e, axis=1, keepdims=True), 1e-30))
        mine = jnp.right_shift(head, NSA_GROUP.bit_length() - 1) == g
        o_c = jnp.where(mine, _dot(p.astype(bf16), vc), o_c)
        imp = jnp.sum(jnp.where(mine, p, 0.0), axis=0, keepdims=True)
        ps_ref[0, g:g + 1, :] = _importance_to_blocks(imp, m_ref[...])
    oc_ref[0] = o_c


def _nsa_sample_cmp(q8, comp, m_mat, t_pos):
    nb = q8.shape[0]
    return pl.pallas_call(
        functools.partial(_nsa_sample_cmp_kernel, t_pos=t_pos),
        grid=(nb,),
        in_specs=[pl.BlockSpec((1, NSA_HEADS, HEAD_DIM), lambda b: (b, 0, 0)),
                  pl.BlockSpec((1, N_COMBO, N_CMP_PAD, HEAD_DIM), lambda b: (b, 0, 0, 0)),
                  pl.BlockSpec((N_CMP_PAD, N_SELBLK), lambda b: (0, 0))],
        out_specs=[pl.BlockSpec((1, NSA_HEADS, HEAD_DIM), lambda b: (b, 0, 0)),
                   pl.BlockSpec((1, 8, N_SELBLK), lambda b: (b, 0, 0))],
        out_shape=[jax.ShapeDtypeStruct((nb, NSA_HEADS, HEAD_DIM), f32),
                   jax.ShapeDtypeStruct((nb, 8, N_SELBLK), f32)],
        compiler_params=_cparams(("parallel",)),
        name="nsa_sample_cmp",
    )(q8, comp, m_mat)


def _nsa_sample_topk_kernel(ps_ref, idx_ref, *, last_blk):
    ps = ps_ref[...]
    lane = lax.broadcasted_iota(jnp.int32, ps.shape, 1)
    score = ps + jnp.where((lane == 0) | (lane == last_blk), FORCE_BONUS, 0.0)
    out = jnp.zeros(ps.shape, jnp.int32)
    for r in range(N_PICK):
        best = jnp.max(score, axis=1, keepdims=True)
        idx = jnp.min(jnp.where(score == best, lane, N_SELBLK), axis=1, keepdims=True)
        out = jnp.where(lane == r, idx, out)
        score = jnp.where(lane == idx, REMOVED, score)
    idx_ref[...] = out


def _nsa_sample_topk(ps_rows, last_blk):
    return pl.pallas_call(
        functools.partial(_nsa_sample_topk_kernel, last_blk=last_blk),
        out_shape=jax.ShapeDtypeStruct(ps_rows.shape, jnp.int32),
        name="nsa_sample_topk",
    )(ps_rows)


def _nsa_sample_sel_kernel(pt_ref, idx_ref, slc_ref, q_ref, oc_ref, gate_ref, slc_new_ref, win_new_ref, wc_ref,
                           o_ref, wout_ref, kv_scr, sem_ref, *, t_pos):
    n_blk = NSA_KV_HEADS * N_PICK
    halves = PAGE_SIZE // SEL_BLOCK
    b = pl.program_id(0)
    nb = pl.num_programs(0)
    slot = lax.rem(b, 2)

    def picked(seq, k):
        return idx_ref[(seq * NSA_KV_HEADS + k // N_PICK) * N_SEL + k % N_PICK]

    def blk_copies(seq, s, k):
        page = pt_ref[seq, picked(seq, k) // halves]
        g = k // N_PICK
        return [pltpu.make_async_copy(slc_ref.at[page, kind * NSA_KV_HEADS + g], kv_scr.at[s, kind, k],
                                      sem_ref.at[s]) for kind in range(2)]

    def start_all(seq, s):
        for k in range(n_blk):
            for cp in blk_copies(seq, s, k):
                cp.start()

    @pl.when(b == 0)
    def _():
        start_all(0, 0)

    @pl.when(b + 1 < nb)
    def _():
        start_all(b + 1, 1 - slot)

    for k in range(n_blk):
        for cp in blk_copies(b, slot, k):
            cp.wait()

    q = q_ref[0] * jnp.asarray(HEAD_DIM ** -0.5, bf16)
    slope = _head_slopes()
    head = lax.broadcasted_iota(jnp.int32, (NSA_HEADS, 1), 0)
    lane = lax.broadcasted_iota(jnp.int32, (1, PAGE_SIZE), 1)
    new_slc = slc_new_ref[0]
    new_win = win_new_ref[0]
    wl = wc_ref.shape[3]
    wlane = lax.broadcasted_iota(jnp.int32, (1, wl), 1)
    d_w = (wl - wlane).astype(f32)
    qf = q.astype(f32)

    def attend(s, mask, v_t, new_row, g):
        k_new = new_row[:, g * HEAD_DIM:(g + 1) * HEAD_DIM].astype(bf16).astype(f32)
        v_new = new_row[:, KV_WIDTH + g * HEAD_DIM:KV_WIDTH + (g + 1) * HEAD_DIM].astype(bf16).astype(f32)
        s_new = jnp.sum(qf * k_new, axis=1, keepdims=True)
        m = jnp.maximum(jnp.max(s, axis=1, keepdims=True), s_new)
        e, e_new = jnp.exp(s - m), jnp.exp(s_new - m)
        if mask is not None:
            e = jnp.where(mask, e, 0.0)
        den = jnp.maximum(jnp.sum(e, axis=1, keepdims=True) + e_new, 1e-30)
        num = _dot_nt(e.astype(bf16), v_t.astype(bf16)) + e_new.astype(bf16).astype(f32) * v_new
        return num * (1.0 / den)

    o_s = jnp.zeros((NSA_HEADS, HEAD_DIM), f32)
    o_w = jnp.zeros((NSA_HEADS, HEAD_DIM), f32)
    for g in range(NSA_KV_HEADS):
        mine = jnp.right_shift(head, NSA_GROUP.bit_length() - 1) == g
        pos, member = [], []
        for n in range(N_PICK):
            j = picked(b, g * N_PICK + n)
            pos.append((j // halves) * PAGE_SIZE + lane)
            member.append(jnp.right_shift(lane, SEL_BLOCK.bit_length() - 1) == j % halves)
        pos = jnp.concatenate(pos, axis=1)
        member = jnp.concatenate(member, axis=1)
        k_t = jnp.concatenate([kv_scr[slot, 0, g * N_PICK + n] for n in range(N_PICK)], axis=1)
        v_t = jnp.concatenate([kv_scr[slot, 1, g * N_PICK + n] for n in range(N_PICK)], axis=1)
        s = jnp.where(member, _dot(q, k_t.astype(bf16)) - slope * (t_pos - pos).astype(f32), NEG)
        o_s = jnp.where(mine, attend(s, member, v_t, new_slc, g), o_s)
        s = _dot(q, wc_ref[0, g].astype(bf16)) - slope * d_w
        o_w = jnp.where(mine, attend(s, None, wc_ref[0, NSA_KV_HEADS + g], new_win, g), o_w)

    gate = jnp.broadcast_to(_sigmoid(gate_ref[0]), (NSA_HEADS, LANES))
    glane = lax.broadcasted_iota(jnp.int32, (NSA_HEADS, LANES), 1)
    gcol = [jnp.sum(jnp.where(glane == 3 * head + j, gate, 0.0), axis=1, keepdims=True) for j in range(3)]
    o_ref[0] = gcol[0] * oc_ref[0] + gcol[1] * o_s + gcol[2] * o_w
    eye = lax.broadcasted_iota(jnp.int32, (HEAD_DIM, HEAD_DIM), 0) == \
        lax.broadcasted_iota(jnp.int32, (HEAD_DIM, HEAD_DIM), 1)
    for c in range(N_COMBO):
        piece = jnp.broadcast_to(new_win[:, c * HEAD_DIM:(c + 1) * HEAD_DIM], (HEAD_DIM, HEAD_DIM))
        new_col = jnp.sum(jnp.where(eye, piece, 0.0), axis=1, keepdims=True)
        wout_ref[0, c] = jnp.where(wlane == wl - 1, new_col, pltpu.roll(wc_ref[0, c], wl - 1, axis=1))


def _nsa_sample_sel(page_table, idx_flat, slc_pages_t, q8, o_c, gates, slc_new, win_new, cache_win_t, t_pos):
    nb = q8.shape[0]
    wl = cache_win_t.shape[3]
    n_blk = NSA_KV_HEADS * N_PICK
    per_seq = lambda b, pt, idx: (b, 0, 0)
    win_blk = pl.BlockSpec((1, N_COMBO, HEAD_DIM, wl), lambda b, pt, idx: (b, 0, 0, 0))
    grid_spec = pltpu.PrefetchScalarGridSpec(
        num_scalar_prefetch=2,
        grid=(nb,),
        in_specs=[pl.BlockSpec(memory_space=pl.ANY),
                  pl.BlockSpec((1, NSA_HEADS, HEAD_DIM), per_seq),
                  pl.BlockSpec((1, NSA_HEADS, HEAD_DIM), per_seq),
                  pl.BlockSpec((1, 1, LANES), per_seq),
                  pl.BlockSpec((1, 1, 2 * KV_WIDTH), per_seq),
                  pl.BlockSpec((1, 1, 2 * KV_WIDTH), per_seq),
                  win_blk],
        out_specs=[pl.BlockSpec((1, NSA_HEADS, HEAD_DIM), per_seq), win_blk],
        scratch_shapes=[pltpu.VMEM((2, 2, n_blk, HEAD_DIM, PAGE_SIZE), f32),
                        pltpu.SemaphoreType.DMA((2,))],
    )
    return pl.pallas_call(
        functools.partial(_nsa_sample_sel_kernel, t_pos=t_pos),
        grid_spec=grid_spec,
        out_shape=[jax.ShapeDtypeStruct((nb, NSA_HEADS, HEAD_DIM), f32),
                   jax.ShapeDtypeStruct(cache_win_t.shape, f32)],
        compiler_params=_cparams(("arbitrary",)),
        name="nsa_sample_sel",
    )(page_table, idx_flat, slc_pages_t, q8, o_c, gates, slc_new, win_new, cache_win_t)


def kernel(x_prompt, x_sample, cache_cmp_kv, cache_slc_kv, cache_win_kv, state_hgrn, page_table, p_prompt, p_sample,
           w_in, g_pre, cmp_pe, cmp_w1, cmp_b1, cmp_w2, hg_lower, hg_norm, w_out, g_post, ple_proj, ple_gate):
    bsz, seq, _ = x_prompt.shape
    nb = x_sample.shape[0]
    n_pool = cache_cmp_kv.shape[1]
    n_pages = page_table.shape[1]
    past = n_pages * PAGE_SIZE
    nq = seq // Q_BLOCK
    kv_tail = (2, NSA_KV_HEADS, HEAD_DIM)

    o = _ORIG
    w = w_in[0]
    w_r = jnp.concatenate([w[:, o['q_a']:o['k_cmp']], w[:, o['z_a']:o['end']], w[:, o['k_cmp']:o['gate_a']],
                           w[:, o['gate_a']:o['z_a']],
                           jnp.zeros((D_MODEL, PROJ_WIDTH - o['end']), f32)], axis=1).astype(bf16)
    cw = _compress_weights(cmp_pe[0], cmp_w1[0], cmp_b1[0], cmp_w2[0])
    m_mat = _sel_map_matrix()
    w_out_b, ple_proj_b, ple_gate_b = w_out[0].astype(bf16), ple_proj[0].astype(bf16), ple_gate[0].astype(bf16)

    xp = x_prompt.reshape(bsz * seq, D_MODEL)
    zp = _proj(xp, g_pre, w_r, 1024)
    cmp_p = zp[:, COL_CMP:COL_CMP + 2 * KV_WIDTH]
    slc_p = zp[:, COL_SLC:COL_SLC + 2 * KV_WIDTH]
    win_p = zp[:, COL_WIN:COL_WIN + 2 * KV_WIDTH]
    pages_p = cmp_p.reshape(bsz * seq // PAGE_SIZE, SEGS_PER_PAGE, SEG_WIDTH)
    table_p = jnp.arange(bsz * seq // PAGE_SIZE, dtype=jnp.int32).reshape(bsz, seq // PAGE_SIZE)
    comp_p = _compress(pages_p, table_p, *cw)
    q_st = zp[:, COL_QA:COL_QA + NSA_WIDTH].astype(bf16)
    q_st = q_st.reshape(bsz, nq, Q_BLOCK, NSA_KV_HEADS, NSA_GROUP, HEAD_DIM).transpose(0, 1, 3, 4, 2, 5)
    q_st = q_st.reshape(bsz, nq, NSA_KV_HEADS, NSA_GROUP * Q_BLOCK, HEAD_DIM)
    o_st = _nsa_prompt(q_st, zp, comp_p, _attention_kv(slc_p, bsz, seq), _attention_kv(win_p, bsz, seq),
                       m_mat, _expand_matrix(seq // SEL_CHUNK))
    o_a = o_st.reshape(bsz, nq, NSA_KV_HEADS, NSA_GROUP, Q_BLOCK, HEAD_DIM).transpose(0, 1, 4, 2, 3, 5)
    o_a = o_a.reshape(bsz * seq, NSA_WIDTH)
    yb_p, hg_p = _hgrn_prompt(zp, hg_lower, hg_norm, bsz, seq)
    h_p = _finish(xp, o_a, zp, yb_p, p_prompt[0].reshape(bsz * seq, PLE_DIM),
                  w_out_b, g_post, ple_proj_b, ple_gate_b, 512)

    wl_p = min(WINDOW, seq)
    out_prompt = (h_p.reshape(bsz, seq, D_MODEL),
                  cmp_p.reshape((1, bsz, seq) + kv_tail),
                  slc_p.reshape((1, bsz, seq) + kv_tail),
                  win_p.reshape((bsz, seq) + kv_tail)[:, seq - wl_p:][None],
                  hg_p[None])

    xs = x_sample.reshape(nb, D_MODEL)
    zs = _proj(xs, g_pre, w_r, nb)
    comp_s = _compress(cache_cmp_kv[0].reshape(n_pool, SEGS_PER_PAGE, SEG_WIDTH), page_table, *cw)
    q8 = zs[:, COL_QA:COL_QA + NSA_WIDTH].astype(bf16).reshape(nb, NSA_HEADS, HEAD_DIM)
    oc_s, ps_s = _nsa_sample_cmp(q8, comp_s, m_mat, past)
    ps_rows = ps_s[:, :NSA_KV_HEADS, :].reshape(nb * NSA_KV_HEADS, N_SELBLK)
    idx = _nsa_sample_topk(ps_rows, past // SEL_BLOCK - 1)
    gates_s = zs[:, COL_GATE:COL_GATE + LANES].reshape(nb, 1, LANES)
    slc_t = cache_slc_kv.transpose(0, 1, 3, 4, 5, 2).reshape(n_pool, N_COMBO, HEAD_DIM, PAGE_SIZE)
    wl_s = cache_win_kv.shape[2]
    win_t = cache_win_kv.transpose(0, 1, 3, 4, 5, 2).reshape(nb, N_COMBO, HEAD_DIM, wl_s)
    oa_s, win_s = _nsa_sample_sel(page_table, idx[:, :N_SEL].reshape(-1), slc_t, q8, oc_s, gates_s,
                                  zs[:, COL_SLC:COL_SLC + 2 * KV_WIDTH].reshape(nb, 1, 2 * KV_WIDTH),
                                  zs[:, COL_WIN:COL_WIN + 2 * KV_WIDTH].reshape(nb, 1, 2 * KV_WIDTH),
                                  win_t, past)
    win_s = win_s.reshape((1, nb) + kv_tail + (wl_s,)).transpose(0, 1, 5, 2, 3, 4)
    yb_s, hg_s = _hgrn_sample(zs, state_hgrn, hg_lower, hg_norm)
    h_s = _finish(xs, oa_s.reshape(nb, NSA_WIDTH), zs, yb_s, p_sample[0].reshape(nb, PLE_DIM),
                  w_out_b, g_post, ple_proj_b, ple_gate_b, nb)

    out_sample = (h_s.reshape(nb, 1, D_MODEL),
                  zs[:, COL_CMP:COL_CMP + 2 * KV_WIDTH].reshape((1, nb, 1) + kv_tail),
                  zs[:, COL_SLC:COL_SLC + 2 * KV_WIDTH].reshape((1, nb, 1) + kv_tail),
                  win_s,
                  hg_s)
    return (out_prompt[0], out_sample[0]) + out_prompt[1:] + out_sample[1:]
```

```python
import functools

import numpy as np
import jax
import jax.numpy as jnp
from jax import lax
from jax.experimental import pallas as pl
from jax.experimental.pallas import tpu as pltpu

D_MODEL = 1024
PAGE_SIZE = 128
NSA_HEADS = 8
NSA_KV_HEADS = 2
NSA_GROUP = NSA_HEADS // NSA_KV_HEADS
HEAD_DIM = 64
NSA_WIDTH = NSA_HEADS * HEAD_DIM
KV_WIDTH = NSA_KV_HEADS * HEAD_DIM
CMP_BLOCK = 32
CMP_STRIDE = 16
CMP_RATIO = CMP_BLOCK // CMP_STRIDE
CMP_HIDDEN = 128
SEL_BLOCK = 64
SEL_RATIO = SEL_BLOCK // CMP_STRIDE
N_SEL = 16
WINDOW = 512
Q_BLOCK = 128
HG_HEADS = 4
HG_DIM = 128
HG_WIDTH = HG_HEADS * HG_DIM
PLE_DIM = 256
EPS = 1e-6
NEG = -1e30
FORCE_BONUS = 1e6
REMOVED = -3e38

COL_QA, COL_ZA, COL_QB, COL_FB, COL_IB, COL_ZB = 0, 512, 1024, 1536, 2048, 2560
COL_CMP, COL_SLC, COL_WIN, COL_GATE = 3072, 3328, 3584, 3840
PROJ_WIDTH = 4096
_ORIG = dict(q_a=0, k_cmp=512, gate_a=1280, z_a=1304, q_b=1816, f_b=2328, i_b=2840, z_b=3352, end=3864)

LANES = 128
VMEM_LIMIT = 56 * 1024 * 1024

SEL_CHUNK = 512
WIN_KEYS = WINDOW + Q_BLOCK
N_CMP_PAD = 512
N_SELBLK = 128

f32 = jnp.float32
bf16 = jnp.bfloat16


def _cparams(sem):
    return pltpu.CompilerParams(dimension_semantics=sem, vmem_limit_bytes=VMEM_LIMIT)


def _dot(a, b):
    return jnp.dot(a, b, preferred_element_type=f32)


def _dot_nt(a, b):
    return lax.dot_general(a, b, (((1,), (1,)), ((), ())), preferred_element_type=f32)


def _dot_tn(a, b):
    return lax.dot_general(a, b, (((0,), (0,)), ((), ())), preferred_element_type=f32)


def _sigmoid(x):
    return 1.0 / (1.0 + jnp.exp(-x))


def _silu(x):
    return x * _sigmoid(x)


def _proj_kernel(x_ref, g_ref, w_ref, z_ref, xn_ref):
    @pl.when(pl.program_id(1) == 0)
    def _():
        x = x_ref[...]
        y = x * lax.rsqrt(jnp.mean(x * x, axis=-1, keepdims=True) + EPS)
        xn_ref[...] = (y * g_ref[...]).astype(bf16)

    z_ref[...] = _dot(xn_ref[...], w_ref[...])


def _proj(x, g, w, tm):
    n = x.shape[0]
    tn = 1024
    return pl.pallas_call(
        _proj_kernel,
        grid=(n // tm, PROJ_WIDTH // tn),
        in_specs=[pl.BlockSpec((tm, D_MODEL), lambda i, j: (i, 0)),
                  pl.BlockSpec((1, D_MODEL), lambda i, j: (0, 0)),
                  pl.BlockSpec((D_MODEL, tn), lambda i, j: (0, j))],
        out_specs=pl.BlockSpec((tm, tn), lambda i, j: (i, j)),
        out_shape=jax.ShapeDtypeStruct((n, PROJ_WIDTH), f32),
        scratch_shapes=[pltpu.VMEM((tm, D_MODEL), bf16)],
        compiler_params=_cparams(("parallel", "arbitrary")),
        name="proj",
    )(x, g, w)


def _finish_kernel(x_ref, oa_ref, za_ref, yb_ref, p_ref, wo_ref, gp_ref, pp_ref, pg_ref, h_ref):
    ya = (oa_ref[...] * _silu(za_ref[...])).astype(bf16)
    yb = yb_ref[...].astype(bf16)
    mix = _dot(ya, wo_ref[0:NSA_WIDTH, :]) + _dot(yb, wo_ref[NSA_WIDTH:, :])
    nrm = mix * lax.rsqrt(jnp.mean(mix * mix, axis=-1, keepdims=True) + EPS) * gp_ref[...]
    h = x_ref[...] + nrm
    gate = _sigmoid(_dot(h.astype(bf16), pg_ref[...]))
    h_ref[...] = h + _dot(p_ref[...].astype(bf16), pp_ref[...]) * gate


def _finish(x, oa, z, yb, p, w_out, g_post, ple_proj, ple_gate, tm):
    n = x.shape[0]
    za_blk = COL_ZA // NSA_WIDTH
    row = lambda i: (i, 0)
    const = lambda i: (0, 0)
    return pl.pallas_call(
        _finish_kernel,
        grid=(n // tm,),
        in_specs=[pl.BlockSpec((tm, D_MODEL), row),
                  pl.BlockSpec((tm, NSA_WIDTH), row),
                  pl.BlockSpec((tm, NSA_WIDTH), lambda i: (i, za_blk)),
                  pl.BlockSpec((tm, HG_WIDTH), row),
                  pl.BlockSpec((tm, PLE_DIM), row),
                  pl.BlockSpec((D_MODEL, D_MODEL), const),
                  pl.BlockSpec((1, D_MODEL), const),
                  pl.BlockSpec((PLE_DIM, D_MODEL), const),
                  pl.BlockSpec((D_MODEL, D_MODEL), const)],
        out_specs=pl.BlockSpec((tm, D_MODEL), row),
        out_shape=jax.ShapeDtypeStruct((n, D_MODEL), f32),
        compiler_params=_cparams(("parallel",)),
        name="finish",
    )(x, oa, z, yb, p, w_out, g_post, ple_proj, ple_gate)


SEGS_PER_PAGE = PAGE_SIZE // CMP_STRIDE
ROW_WIDTH = 2 * KV_WIDTH
SEG_WIDTH = CMP_STRIDE * ROW_WIDTH
N_COMBO = 2 * NSA_KV_HEADS


def _compress_kernel(pt_ref, pages_ref, wa_ref, wb_ref, pea_ref, peb_ref, b1_ref, w2_ref, out_ref,
                     stage_ref, rows_ref, sem_ref, *, n_pages, feature_major):
    b = pl.program_id(0)
    nb = pl.num_programs(0)
    slot = lax.rem(b, 2)

    def page_copy(seq, s, j):
        return pltpu.make_async_copy(pages_ref.at[pt_ref[seq, j]], stage_ref.at[s, j], sem_ref.at[s])

    def start_all(seq, s):
        def body(j, c):
            page_copy(seq, s, j).start()
            return c
        lax.fori_loop(0, n_pages, body, 0)

    @pl.when(b == 0)
    def _():
        start_all(0, 0)

    @pl.when(b + 1 < nb)
    def _():
        start_all(b + 1, 1 - slot)

    def wait_body(j, c):
        page_copy(b, slot, j).wait()
        return c
    lax.fori_loop(0, n_pages, wait_body, 0)

    lane_halves = ROW_WIDTH // LANES

    def to_rows(j, c):
        tile = stage_ref[slot, j]
        rows = tile.T if feature_major else tile
        r0 = pl.multiple_of(j * PAGE_SIZE, PAGE_SIZE)
        for h in range(lane_halves):
            rows_ref[h, pl.ds(r0, PAGE_SIZE), :] = rows[:, h * LANES:(h + 1) * LANES]
        return c
    lax.fori_loop(0, n_pages, to_rows, 0)

    n_seg = n_pages * SEGS_PER_PAGE
    seg_a, seg_b = [], []
    for j in range(CMP_STRIDE):
        r = jnp.concatenate([rows_ref[h, pl.ds(j, n_seg, stride=CMP_STRIDE), :] for h in range(lane_halves)],
                            axis=1)
        seg_a.append((r + pea_ref[j]).astype(bf16))
        seg_b.append((r + peb_ref[j]).astype(bf16))
    pa = _dot(jnp.concatenate(seg_a, axis=1), wa_ref[...])
    pb = _dot(jnp.concatenate(seg_b, axis=1), wb_ref[...])
    pb = pltpu.roll(pb, n_seg - 1, axis=0)
    hid = _silu(pa + pb + b1_ref[...])
    for c in range(N_COMBO):
        hc = hid[:, c * CMP_HIDDEN:(c + 1) * CMP_HIDDEN].astype(bf16)
        out_ref[0, c] = _dot(hc, w2_ref[c // NSA_KV_HEADS])


def _compress(pages, page_table, wa, wb, pea, peb, b1, w2, feature_major):
    nseq, n_pages = page_table.shape
    n_seg = n_pages * SEGS_PER_PAGE
    const2 = lambda b, pt: (0, 0)
    const3 = lambda b, pt: (0, 0, 0)
    once = pl.Buffered(1)
    grid_spec = pltpu.PrefetchScalarGridSpec(
        num_scalar_prefetch=1,
        grid=(nseq,),
        in_specs=[pl.BlockSpec(memory_space=pl.ANY),
                  pl.BlockSpec((SEG_WIDTH, N_COMBO * CMP_HIDDEN), const2, pipeline_mode=once),
                  pl.BlockSpec((SEG_WIDTH, N_COMBO * CMP_HIDDEN), const2, pipeline_mode=once),
                  pl.BlockSpec((CMP_STRIDE, 1, ROW_WIDTH), const3),
                  pl.BlockSpec((CMP_STRIDE, 1, ROW_WIDTH), const3),
                  pl.BlockSpec((1, N_COMBO * CMP_HIDDEN), const2),
                  pl.BlockSpec((2, CMP_HIDDEN, HEAD_DIM), const3)],
        out_specs=pl.BlockSpec((1, N_COMBO, n_seg, HEAD_DIM), lambda b, pt: (b, 0, 0, 0)),
        scratch_shapes=[pltpu.VMEM((2, n_pages) + pages.shape[1:], f32),
                        pltpu.VMEM((ROW_WIDTH // LANES, n_pages * PAGE_SIZE, LANES), f32),
                        pltpu.SemaphoreType.DMA((2,))],
    )
    return pl.pallas_call(
        functools.partial(_compress_kernel, n_pages=n_pages, feature_major=feature_major),
        grid_spec=grid_spec,
        out_shape=jax.ShapeDtypeStruct((nseq, N_COMBO, n_seg, HEAD_DIM), f32),
        compiler_params=_cparams(("arbitrary",)),
        name="compress",
    )(page_table, pages, wa, wb, pea, peb, b1, w2)


def _compress_weights(cmp_pe, cmp_w1, cmp_b1, cmp_w2):
    half = CMP_STRIDE * HEAD_DIM
    eye = jnp.eye(NSA_KV_HEADS, dtype=f32)

    def big(w_half):
        t = jnp.einsum('kjdh,ka,gb->jabdkgh', w_half, jnp.eye(2, dtype=f32), eye)
        return t.reshape(SEG_WIDTH, N_COMBO * CMP_HIDDEN)

    w1 = cmp_w1.reshape(2, 2, CMP_STRIDE, HEAD_DIM, CMP_HIDDEN)
    wa = big(w1[:, 0]).astype(bf16)
    wb = big(w1[:, 1]).astype(bf16)
    pe = cmp_pe.reshape(2, 2, CMP_STRIDE, HEAD_DIM)

    def pe_row(p):
        t = jnp.broadcast_to(p.transpose(1, 0, 2)[:, :, None, :],
                             (CMP_STRIDE, 2, NSA_KV_HEADS, HEAD_DIM))
        return t.reshape(CMP_STRIDE, 1, ROW_WIDTH)

    pea, peb = pe_row(pe[:, 0]), pe_row(pe[:, 1])
    b1 = jnp.broadcast_to(cmp_b1[:, None, :], (2, NSA_KV_HEADS, CMP_HIDDEN)).reshape(1, N_COMBO * CMP_HIDDEN)
    return wa, wb, pea, peb, b1, cmp_w2.astype(bf16)


def _sel_map_matrix():
    wts = np.convolve(np.ones(SEL_RATIO), np.ones(CMP_RATIO))
    m = np.zeros((N_CMP_PAD, N_SELBLK), np.float32)
    for j in range(N_SELBLK):
        for k, wk in enumerate(wts):
            n = SEL_RATIO * j + k
            if n < N_CMP_PAD - 1:
                m[n, j] = wk
    return jnp.asarray(m, dtype=bf16)


def _expand_matrix(n_chunks):
    key = np.arange(n_chunks * SEL_CHUNK) // SEL_BLOCK
    e = (np.arange(N_SELBLK)[:, None] == key[None, :]).astype(np.float32)
    return jnp.asarray(e.reshape(N_SELBLK, n_chunks, SEL_CHUNK).transpose(1, 0, 2), dtype=bf16)


def _importance_to_blocks(imp, m):
    hi = imp.astype(bf16)
    lo = (imp - hi.astype(f32)).astype(bf16)
    return _dot(hi, m) + _dot(lo, m)


def _nsa_prompt_kernel(q_ref, gate_ref, kc_ref, vc_ref, kvs_ref, kvw_ref, m_ref, e_ref, o_ref,
                       acc_ref, mx_ref, used_ref):
    g = pl.program_id(1)
    qi = pl.program_id(2)
    q0 = qi * Q_BLOCK
    rows = NSA_GROUP * Q_BLOCK

    q64 = q_ref[0, 0, 0] * jnp.asarray(HEAD_DIM ** -0.5, bf16)
    q128 = jnp.concatenate([q64, jnp.zeros_like(q64)], axis=1)
    slope = [jnp.where(g == 0, 2.0 ** -(r + 1), 2.0 ** -(NSA_GROUP + r + 1)).astype(f32)
             for r in range(NSA_GROUP)]

    qrow = lax.broadcasted_iota(jnp.int32, (Q_BLOCK, 1), 0)

    def softmax_rows(s, dist, mask):
        es, ls = [], []
        for r in range(NSA_GROUP):
            sr = jnp.where(mask, s[r * Q_BLOCK:(r + 1) * Q_BLOCK] - slope[r] * dist, NEG)
            m = jnp.max(sr, axis=1, keepdims=True)
            e = jnp.where(mask, jnp.exp(sr - m), 0.0)
            es.append(e)
            ls.append(1.0 / jnp.maximum(jnp.sum(e, axis=1, keepdims=True), 1e-30))
        return es, ls

    kc = kc_ref[0, 0].astype(bf16)
    vc = vc_ref[0, 0].astype(bf16)
    ncol = lax.broadcasted_iota(jnp.int32, (1, N_CMP_PAD), 1)
    d_c = (q0 + qrow) - (ncol * CMP_STRIDE + (CMP_BLOCK - 1))
    es, ls = softmax_rows(_dot_nt(q64, kc), d_c.astype(f32), d_c >= 0)
    ps_c = [e * l for e, l in zip(es, ls)]
    o_c = _dot(jnp.concatenate(ps_c, axis=0).astype(bf16), vc)
    imp = ps_c[0] + ps_c[1] + ps_c[2] + ps_c[3]
    p_sel = _importance_to_blocks(imp, m_ref[...])

    ps_t = p_sel.T
    blk = lax.broadcasted_iota(jnp.int32, (N_SELBLK, Q_BLOCK), 0)
    tq = q0 + lax.broadcasted_iota(jnp.int32, (N_SELBLK, Q_BLOCK), 1)
    jt = jnp.right_shift(tq, SEL_BLOCK.bit_length() - 1)
    valid = blk <= jt
    forced = (blk == 0) | (blk == jt) | (blk == jt - 1)
    score = jnp.where(valid, ps_t + jnp.where(forced, FORCE_BONUS, 0.0), NEG)
    sel_t = jnp.zeros((N_SELBLK, Q_BLOCK), f32)
    for _ in range(N_SEL):
        best = jnp.max(score, axis=0, keepdims=True)
        idx = jnp.min(jnp.where(score == best, blk, N_SELBLK), axis=0, keepdims=True)
        hit = blk == idx
        sel_t = jnp.where(hit & (best > NEG * 0.5), 1.0, sel_t)
        score = jnp.where(hit, REMOVED, score)
    sel = sel_t.T.astype(bf16)

    acc_ref[...] = jnp.zeros_like(acc_ref)
    mx_ref[...] = jnp.full_like(mx_ref, NEG)
    kcol = lax.broadcasted_iota(jnp.int32, (1, SEL_CHUNK), 1)
    d_base = (qrow - kcol).astype(f32)

    blk_per_chunk = SEL_CHUNK // SEL_BLOCK
    for c in range(e_ref.shape[0]):
        picked = jnp.max(sel_t[c * blk_per_chunk:(c + 1) * blk_per_chunk, :])
        used_ref[c] = (picked > 0.5).astype(jnp.int32)

    def sel_chunk(c, carry):
        @pl.when(used_ref[c] > 0)
        def _():
            sel_chunk_body(c)
        return carry

    def sel_chunk_body(c):
        k0 = pl.multiple_of(c * SEL_CHUNK, SEL_CHUNK)
        kv = kvs_ref[0, 0, pl.ds(k0, SEL_CHUNK), :]
        s = _dot_nt(q128, kv[:, 0:LANES])
        dist = d_base + (q0 - k0).astype(f32)
        allowed = (_dot(sel, e_ref[c]) > 0.5) & (dist >= 0.0)
        ps = []
        for r in range(NSA_GROUP):
            rs = slice(r * Q_BLOCK, (r + 1) * Q_BLOCK)
            sr = jnp.where(allowed, s[rs] - slope[r] * dist, NEG)
            m_old = mx_ref[rs]
            m_new = jnp.maximum(m_old, jnp.max(sr, axis=1, keepdims=True))
            acc_ref[rs] = acc_ref[rs] * jnp.exp(m_old - m_new)
            mx_ref[rs] = m_new
            ps.append(jnp.exp(sr - m_new).astype(bf16))
        acc_ref[...] += _dot(jnp.concatenate(ps, axis=0), kv[:, LANES:])

    n_chunks = (q0 + Q_BLOCK + SEL_CHUNK - 1) // SEL_CHUNK
    lax.fori_loop(0, n_chunks, sel_chunk, 0)
    acc = acc_ref[...]
    o_s = acc[:, 0:HEAD_DIM] * (1.0 / jnp.maximum(acc[:, HEAD_DIM:HEAD_DIM + 1], 1e-30))

    w0 = pl.multiple_of(jnp.maximum(q0 - WINDOW, 0), Q_BLOCK)
    kvw = kvw_ref[0, 0, pl.ds(w0, WIN_KEYS), :]
    wcol = lax.broadcasted_iota(jnp.int32, (1, WIN_KEYS), 1)
    d_w = (q0 - w0) + qrow - wcol
    es, _ = softmax_rows(_dot_nt(q128, kvw[:, 0:LANES]), d_w.astype(f32), (d_w >= 0) & (d_w <= WINDOW))
    acc_w = _dot(jnp.concatenate(es, axis=0).astype(bf16), kvw[:, LANES:])
    o_w = acc_w[:, 0:HEAD_DIM] * (1.0 / jnp.maximum(acc_w[:, HEAD_DIM:HEAD_DIM + 1], 1e-30))

    gate = _sigmoid(gate_ref[...])

    def gate_col(r, j):
        lo = gate[:, r * 3 + j:r * 3 + j + 1]
        hi = gate[:, (NSA_GROUP + r) * 3 + j:(NSA_GROUP + r) * 3 + j + 1]
        return jnp.where(g == 0, lo, hi)

    for r in range(NSA_GROUP):
        rs = slice(r * Q_BLOCK, (r + 1) * Q_BLOCK)
        o_ref[0, 0, 0, rs, :] = gate_col(r, 0) * o_c[rs] + gate_col(r, 1) * o_s[rs] + gate_col(r, 2) * o_w[rs]


def _nsa_prompt(q_st, z, comp, kv_slc, kv_win, m_mat, e_mat):
    bsz, nq = q_st.shape[0], q_st.shape[1]
    seq = kv_slc.shape[2]
    rows = NSA_GROUP * Q_BLOCK
    gate_blk = COL_GATE // LANES
    return pl.pallas_call(
        _nsa_prompt_kernel,
        grid=(bsz, NSA_KV_HEADS, nq),
        in_specs=[pl.BlockSpec((1, 1, 1, rows, HEAD_DIM), lambda b, g, i: (b, i, g, 0, 0)),
                  pl.BlockSpec((Q_BLOCK, LANES), lambda b, g, i: (b * nq + i, gate_blk)),
                  pl.BlockSpec((1, 1, N_CMP_PAD, HEAD_DIM), lambda b, g, i: (b, g, 0, 0)),
                  pl.BlockSpec((1, 1, N_CMP_PAD, HEAD_DIM), lambda b, g, i: (b, NSA_KV_HEADS + g, 0, 0)),
                  pl.BlockSpec((1, 1, seq, 2 * LANES), lambda b, g, i: (b, g, 0, 0)),
                  pl.BlockSpec((1, 1, seq, 2 * LANES), lambda b, g, i: (b, g, 0, 0)),
                  pl.BlockSpec((N_CMP_PAD, N_SELBLK), lambda b, g, i: (0, 0)),
                  pl.BlockSpec(e_mat.shape, lambda b, g, i: (0, 0, 0))],
        out_specs=pl.BlockSpec((1, 1, 1, rows, HEAD_DIM), lambda b, g, i: (b, i, g, 0, 0)),
        out_shape=jax.ShapeDtypeStruct((bsz, nq, NSA_KV_HEADS, rows, HEAD_DIM), f32),
        scratch_shapes=[pltpu.VMEM((rows, LANES), f32), pltpu.VMEM((rows, 1), f32),
                        pltpu.SMEM((e_mat.shape[0],), jnp.int32)],
        compiler_params=_cparams(("parallel", "parallel", "arbitrary")),
        name="nsa_prompt",
    )(q_st, z, comp, comp, kv_slc, kv_win, m_mat, e_mat)


def _attention_kv(piece, bsz, seq):
    t = piece.reshape(bsz, seq, 2, NSA_KV_HEADS, HEAD_DIM).transpose(2, 0, 3, 1, 4)
    zeros = jnp.zeros_like(t[0])
    ones = jnp.ones(t[0].shape[:-1] + (1,), f32)
    out = jnp.concatenate([t[0], zeros, t[1], ones, zeros[..., :HEAD_DIM - 1]], axis=-1)
    return out.astype(bf16)


HG_CHUNK = 128
HG_BAND = 8


def _lower_bound(lo):
    m = jnp.max(lo, axis=0, keepdims=True)
    e = jnp.exp(lo - m)
    return e[0:1] / jnp.sum(e, axis=0, keepdims=True)


def _hgrn_gates(q, fl, lb):
    f = lb + (1.0 - lb) * _sigmoid(fl)
    k = (1.0 - lb) * _sigmoid(-fl)
    return _silu(q), k, jnp.log(f)


def _hgrn_out(o, nw, zb):
    o = o * lax.rsqrt(jnp.mean(o * o, axis=-1, keepdims=True) + EPS) * nw
    return o * _silu(zb)


def _hgrn_prompt_kernel(q_ref, f_ref, i_ref, zb_ref, lo_ref, nw_ref, y_ref, s_ref, st_ref):
    c = pl.program_id(2)
    n = HG_CHUNK

    @pl.when(c == 0)
    def _():
        st_ref[...] = jnp.zeros_like(st_ref)

    lb = _lower_bound(lo_ref[...])
    qq, kk, logf = _hgrn_gates(q_ref[...], f_ref[...], lb)
    v = i_ref[...]
    row = lax.broadcasted_iota(jnp.int32, (n, n), 0)
    col = lax.broadcasted_iota(jnp.int32, (n, n), 1)

    b = logf
    sh = 1
    while sh < n:
        b = b + jnp.where(row >= sh, pltpu.roll(b, sh, axis=0), 0.0)
        sh *= 2

    a = jnp.zeros((n, n), f32)
    sub = jnp.bitwise_and(row, HG_BAND - 1)
    for d in range(HG_BAND):
        if d == 0:
            term = qq * kk
        else:
            ok = sub >= d
            decay = jnp.exp(jnp.where(ok, b - pltpu.roll(b, d, axis=0), 0.0))
            term = jnp.where(ok, qq * pltpu.roll(kk, d, axis=0) * decay, 0.0)
        a = a + jnp.where(col == row - d, jnp.sum(term, axis=1, keepdims=True), 0.0)

    size = 2 * HG_BAND
    while size <= n:
        half = size // 2
        qs, ks = [], []
        zero = jnp.zeros((half, HG_DIM), f32)
        for r0 in range(0, n, size):
            beta = b[r0 + half - 1:r0 + half]
            ks += [kk[r0:r0 + half] * jnp.exp(beta - b[r0:r0 + half]), zero]
            qs += [zero, qq[r0 + half:r0 + size] * jnp.exp(b[r0 + half:r0 + size] - beta)]
        part = _dot_nt(jnp.concatenate(qs, axis=0).astype(bf16), jnp.concatenate(ks, axis=0).astype(bf16))
        if size < n:
            sh = size.bit_length() - 1
            part = jnp.where(jnp.right_shift(row, sh) == jnp.right_shift(col, sh), part, 0.0)
        a = a + part
        size *= 2

    st = st_ref[...]
    o = _dot(a.astype(bf16), v.astype(bf16)) + _dot_nt((qq * jnp.exp(b)).astype(bf16), st.astype(bf16))
    b_last = b[n - 1:n]
    k_tail = kk * jnp.exp(b_last - b)
    st_new = st * jnp.exp(b_last) + _dot_tn(v.astype(bf16), k_tail.astype(bf16))
    st_ref[...] = st_new
    y_ref[...] = _hgrn_out(o, nw_ref[...], zb_ref[...])

    @pl.when(c == pl.num_programs(2) - 1)
    def _():
        s_ref[0, 0] = st_new.T


def _hgrn_prompt(z, hg_lower, hg_norm, bsz, seq):
    nch = seq // HG_CHUNK

    def zcol(base):
        return pl.BlockSpec((HG_CHUNK, HG_DIM), lambda b, h, c: (b * nch + c, base // HG_DIM + h))

    return pl.pallas_call(
        _hgrn_prompt_kernel,
        grid=(bsz, HG_HEADS, nch),
        in_specs=[zcol(COL_QB), zcol(COL_FB), zcol(COL_IB), zcol(COL_ZB),
                  pl.BlockSpec((hg_lower.shape[0], HG_DIM), lambda b, h, c: (0, h)),
                  pl.BlockSpec((1, HG_DIM), lambda b, h, c: (0, 0))],
        out_specs=[pl.BlockSpec((HG_CHUNK, HG_DIM), lambda b, h, c: (b * nch + c, h)),
                   pl.BlockSpec((1, 1, HG_DIM, HG_DIM), lambda b, h, c: (b, h, 0, 0))],
        out_shape=[jax.ShapeDtypeStruct((bsz * seq, HG_WIDTH), f32),
                   jax.ShapeDtypeStruct((bsz, HG_HEADS, HG_DIM, HG_DIM), f32)],
        scratch_shapes=[pltpu.VMEM((HG_DIM, HG_DIM), f32)],
        compiler_params=_cparams(("parallel", "parallel", "arbitrary")),
        name="hgrn_prompt",
    )(z, z, z, z, hg_lower, hg_norm)


def _hgrn_sample_kernel(q_ref, f_ref, i_ref, zb_ref, lo_ref, nw_ref, s_ref, y_ref, so_ref, o_scr):
    nb = q_ref.shape[0]
    lb = _lower_bound(lo_ref[...])
    qq, kk, logf = _hgrn_gates(q_ref[...], f_ref[...], lb)
    v = i_ref[...]
    pad = jnp.zeros((HG_DIM - nb, HG_DIM), f32)
    to_cols = lambda x: jnp.concatenate([x, pad], axis=0).T
    f_t, k_t, q_t = to_cols(jnp.exp(logf)), to_cols(kk), to_cols(qq)
    for b in range(nb):
        s_new = f_t[:, b:b + 1] * s_ref[0, b, 0] + k_t[:, b:b + 1] * v[b:b + 1, :]
        so_ref[0, b, 0] = s_new
        o_scr[b:b + 1, :] = jnp.sum(s_new * q_t[:, b:b + 1], axis=0, keepdims=True)
    y_ref[...] = _hgrn_out(o_scr[...], nw_ref[...], zb_ref[...])


def _hgrn_sample(z, state, hg_lower, hg_norm):
    nb = z.shape[0]

    def zcol(base):
        return pl.BlockSpec((nb, HG_DIM), lambda h: (0, base // HG_DIM + h))

    sblk = pl.BlockSpec((1, nb, 1, HG_DIM, HG_DIM), lambda h: (0, 0, h, 0, 0))
    return pl.pallas_call(
        _hgrn_sample_kernel,
        grid=(HG_HEADS,),
        in_specs=[zcol(COL_QB), zcol(COL_FB), zcol(COL_IB), zcol(COL_ZB),
                  pl.BlockSpec((hg_lower.shape[0], HG_DIM), lambda h: (0, h)),
                  pl.BlockSpec((1, HG_DIM), lambda h: (0, 0)),
                  sblk],
        out_specs=[pl.BlockSpec((nb, HG_DIM), lambda h: (0, h)), sblk],
        out_shape=[jax.ShapeDtypeStruct((nb, HG_WIDTH), f32),
                   jax.ShapeDtypeStruct(state.shape, f32)],
        scratch_shapes=[pltpu.VMEM((nb, HG_DIM), f32)],
        compiler_params=_cparams(("parallel",)),
        name="hgrn_sample",
    )(z, z, z, z, hg_lower, hg_norm, state)


N_PICK = N_SEL - 1


def _head_slopes():
    h = lax.broadcasted_iota(jnp.int32, (NSA_HEADS, 1), 0).astype(f32)
    return jnp.exp2(-(h + 1.0))


def _nsa_sample_cmp_kernel(q_ref, comp_ref, m_ref, oc_ref, ps_ref, *, t_pos):
    q = q_ref[0] * jnp.asarray(HEAD_DIM ** -0.5, bf16)
    slope = _head_slopes()
    head = lax.broadcasted_iota(jnp.int32, (NSA_HEADS, 1), 0)
    ncol = lax.broadcasted_iota(jnp.int32, (1, N_CMP_PAD), 1)
    d_c = t_pos - (ncol * CMP_STRIDE + (CMP_BLOCK - 1))
    mask = d_c >= 0
    o_c = jnp.zeros((NSA_HEADS, HEAD_DIM), f32)
    ps_ref[...] = jnp.zeros_like(ps_ref)
    for g in range(NSA_KV_HEADS):
        kc = comp_ref[0, g].astype(bf16)
        vc = comp_ref[0, NSA_KV_HEADS + g].astype(bf16)
        s = jnp.where(mask, _dot_nt(q, kc) - slope * d_c.astype(f32), NEG)
        m = jnp.max(s, axis=1, keepdims=True)
        e = jnp.where(mask, jnp.exp(s - m), 0.0)
        p = e * (1.0 / jnp.maximum(jnp.sum(e, axis=1, keepdims=True), 1e-30))
        mine = jnp.right_shift(head, NSA_GROUP.bit_length() - 1) == g
        o_c = jnp.where(mine, _dot(p.astype(bf16), vc), o_c)
        imp = jnp.sum(jnp.where(mine, p, 0.0), axis=0, keepdims=True)
        ps_ref[0, g:g + 1, :] = _importance_to_blocks(imp, m_ref[...])
    oc_ref[0] = o_c


def _nsa_sample_cmp(q8, comp, m_mat, t_pos):
    nb = q8.shape[0]
    return pl.pallas_call(
        functools.partial(_nsa_sample_cmp_kernel, t_pos=t_pos),
        grid=(nb,),
        in_specs=[pl.BlockSpec((1, NSA_HEADS, HEAD_DIM), lambda b: (b, 0, 0)),
                  pl.BlockSpec((1, N_COMBO, N_CMP_PAD, HEAD_DIM), lambda b: (b, 0, 0, 0)),
                  pl.BlockSpec((N_CMP_PAD, N_SELBLK), lambda b: (0, 0))],
        out_specs=[pl.BlockSpec((1, NSA_HEADS, HEAD_DIM), lambda b: (b, 0, 0)),
                   pl.BlockSpec((1, 8, N_SELBLK), lambda b: (b, 0, 0))],
        out_shape=[jax.ShapeDtypeStruct((nb, NSA_HEADS, HEAD_DIM), f32),
                   jax.ShapeDtypeStruct((nb, 8, N_SELBLK), f32)],
        compiler_params=_cparams(("parallel",)),
        name="nsa_sample_cmp",
    )(q8, comp, m_mat)


def _nsa_sample_topk_kernel(ps_ref, idx_ref, *, last_blk):
    ps = ps_ref[...]
    lane = lax.broadcasted_iota(jnp.int32, ps.shape, 1)
    score = ps + jnp.where((lane == 0) | (lane == last_blk), FORCE_BONUS, 0.0)
    out = jnp.zeros(ps.shape, jnp.int32)
    for r in range(N_PICK):
        best = jnp.max(score, axis=1, keepdims=True)
        idx = jnp.min(jnp.where(score == best, lane, N_SELBLK), axis=1, keepdims=True)
        out = jnp.where(lane == r, idx, out)
        score = jnp.where(lane == idx, REMOVED, score)
    idx_ref[...] = out


def _nsa_sample_topk(ps_rows, last_blk):
    return pl.pallas_call(
        functools.partial(_nsa_sample_topk_kernel, last_blk=last_blk),
        out_shape=jax.ShapeDtypeStruct(ps_rows.shape, jnp.int32),
        name="nsa_sample_topk",
    )(ps_rows)


def _nsa_sample_sel_kernel(pt_ref, idx_ref, slc_ref, q_ref, oc_ref, gate_ref, slc_new_ref, win_new_ref, wc_ref,
                           o_ref, wout_ref, kv_scr, sem_ref, *, t_pos):
    n_blk = NSA_KV_HEADS * N_PICK
    halves = PAGE_SIZE // SEL_BLOCK
    b = pl.program_id(0)
    nb = pl.num_programs(0)
    slot = lax.rem(b, 2)

    def picked(seq, k):
        return idx_ref[(seq * NSA_KV_HEADS + k // N_PICK) * N_SEL + k % N_PICK]

    def blk_copies(seq, s, k):
        page = pt_ref[seq, picked(seq, k) // halves]
        g = k // N_PICK
        return [pltpu.make_async_copy(slc_ref.at[page, kind * NSA_KV_HEADS + g], kv_scr.at[s, kind, k],
                                      sem_ref.at[s]) for kind in range(2)]

    def start_all(seq, s):
        for k in range(n_blk):
            for cp in blk_copies(seq, s, k):
                cp.start()

    @pl.when(b == 0)
    def _():
        start_all(0, 0)

    @pl.when(b + 1 < nb)
    def _():
        start_all(b + 1, 1 - slot)

    for k in range(n_blk):
        for cp in blk_copies(b, slot, k):
            cp.wait()

    q = q_ref[0] * jnp.asarray(HEAD_DIM ** -0.5, bf16)
    slope = _head_slopes()
    head = lax.broadcasted_iota(jnp.int32, (NSA_HEADS, 1), 0)
    lane = lax.broadcasted_iota(jnp.int32, (1, PAGE_SIZE), 1)
    new_slc = slc_new_ref[0]
    new_win = win_new_ref[0]
    wl = wc_ref.shape[3]
    wlane = lax.broadcasted_iota(jnp.int32, (1, wl), 1)
    d_w = (wl - wlane).astype(f32)
    qf = q.astype(f32)

    def attend(s, mask, v_t, new_row, g):
        k_new = new_row[:, g * HEAD_DIM:(g + 1) * HEAD_DIM].astype(bf16).astype(f32)
        v_new = new_row[:, KV_WIDTH + g * HEAD_DIM:KV_WIDTH + (g + 1) * HEAD_DIM].astype(bf16).astype(f32)
        s_new = jnp.sum(qf * k_new, axis=1, keepdims=True)
        m = jnp.maximum(jnp.max(s, axis=1, keepdims=True), s_new)
        e, e_new = jnp.exp(s - m), jnp.exp(s_new - m)
        if mask is not None:
            e = jnp.where(mask, e, 0.0)
        den = jnp.maximum(jnp.sum(e, axis=1, keepdims=True) + e_new, 1e-30)
        num = _dot_nt(e.astype(bf16), v_t.astype(bf16)) + e_new.astype(bf16).astype(f32) * v_new
        return num * (1.0 / den)

    o_s = jnp.zeros((NSA_HEADS, HEAD_DIM), f32)
    o_w = jnp.zeros((NSA_HEADS, HEAD_DIM), f32)
    for g in range(NSA_KV_HEADS):
        mine = jnp.right_shift(head, NSA_GROUP.bit_length() - 1) == g
        pos, member = [], []
        for n in range(N_PICK):
            j = picked(b, g * N_PICK + n)
            pos.append((j // halves) * PAGE_SIZE + lane)
            member.append(jnp.right_shift(lane, SEL_BLOCK.bit_length() - 1) == j % halves)
        pos = jnp.concatenate(pos, axis=1)
        member = jnp.concatenate(member, axis=1)
        k_t = jnp.concatenate([kv_scr[slot, 0, g * N_PICK + n] for n in range(N_PICK)], axis=1)
        v_t = jnp.concatenate([kv_scr[slot, 1, g * N_PICK + n] for n in range(N_PICK)], axis=1)
        s = jnp.where(member, _dot(q, k_t.astype(bf16)) - slope * (t_pos - pos).astype(f32), NEG)
        o_s = jnp.where(mine, attend(s, member, v_t, new_slc, g), o_s)
        s = _dot(q, wc_ref[0, g].astype(bf16)) - slope * d_w
        o_w = jnp.where(mine, attend(s, None, wc_ref[0, NSA_KV_HEADS + g], new_win, g), o_w)

    gate = jnp.broadcast_to(_sigmoid(gate_ref[0]), (NSA_HEADS, LANES))
    glane = lax.broadcasted_iota(jnp.int32, (NSA_HEADS, LANES), 1)
    gcol = [jnp.sum(jnp.where(glane == 3 * head + j, gate, 0.0), axis=1, keepdims=True) for j in range(3)]
    o_ref[0] = gcol[0] * oc_ref[0] + gcol[1] * o_s + gcol[2] * o_w
    eye = lax.broadcasted_iota(jnp.int32, (HEAD_DIM, HEAD_DIM), 0) == \
        lax.broadcasted_iota(jnp.int32, (HEAD_DIM, HEAD_DIM), 1)
    for c in range(N_COMBO):
        piece = jnp.broadcast_to(new_win[:, c * HEAD_DIM:(c + 1) * HEAD_DIM], (HEAD_DIM, HEAD_DIM))
        new_col = jnp.sum(jnp.where(eye, piece, 0.0), axis=1, keepdims=True)
        wout_ref[0, c] = jnp.where(wlane == wl - 1, new_col, pltpu.roll(wc_ref[0, c], wl - 1, axis=1))


def _nsa_sample_sel(page_table, idx_flat, slc_pages_t, q8, o_c, gates, slc_new, win_new, cache_win_t, t_pos):
    nb = q8.shape[0]
    wl = cache_win_t.shape[3]
    n_blk = NSA_KV_HEADS * N_PICK
    per_seq = lambda b, pt, idx: (b, 0, 0)
    win_blk = pl.BlockSpec((1, N_COMBO, HEAD_DIM, wl), lambda b, pt, idx: (b, 0, 0, 0))
    grid_spec = pltpu.PrefetchScalarGridSpec(
        num_scalar_prefetch=2,
        grid=(nb,),
        in_specs=[pl.BlockSpec(memory_space=pl.ANY),
                  pl.BlockSpec((1, NSA_HEADS, HEAD_DIM), per_seq),
                  pl.BlockSpec((1, NSA_HEADS, HEAD_DIM), per_seq),
                  pl.BlockSpec((1, 1, LANES), per_seq),
                  pl.BlockSpec((1, 1, 2 * KV_WIDTH), per_seq),
                  pl.BlockSpec((1, 1, 2 * KV_WIDTH), per_seq),
                  win_blk],
        out_specs=[pl.BlockSpec((1, NSA_HEADS, HEAD_DIM), per_seq), win_blk],
        scratch_shapes=[pltpu.VMEM((2, 2, n_blk, HEAD_DIM, PAGE_SIZE), f32),
                        pltpu.SemaphoreType.DMA((2,))],
    )
    return pl.pallas_call(
        functools.partial(_nsa_sample_sel_kernel, t_pos=t_pos),
        grid_spec=grid_spec,
        out_shape=[jax.ShapeDtypeStruct((nb, NSA_HEADS, HEAD_DIM), f32),
                   jax.ShapeDtypeStruct(cache_win_t.shape, f32)],
        compiler_params=_cparams(("arbitrary",)),
        name="nsa_sample_sel",
    )(page_table, idx_flat, slc_pages_t, q8, o_c, gates, slc_new, win_new, cache_win_t)


def kernel(x_prompt, x_sample, cache_cmp_kv, cache_slc_kv, cache_win_kv, state_hgrn, page_table, p_prompt, p_sample,
           w_in, g_pre, cmp_pe, cmp_w1, cmp_b1, cmp_w2, hg_lower, hg_norm, w_out, g_post, ple_proj, ple_gate):
    bsz, seq, _ = x_prompt.shape
    nb = x_sample.shape[0]
    n_pool = cache_cmp_kv.shape[1]
    n_pages = page_table.shape[1]
    past = n_pages * PAGE_SIZE
    nq = seq // Q_BLOCK
    kv_tail = (2, NSA_KV_HEADS, HEAD_DIM)

    o = _ORIG
    w = w_in[0]
    w_r = jnp.concatenate([w[:, o['q_a']:o['k_cmp']], w[:, o['z_a']:o['end']], w[:, o['k_cmp']:o['gate_a']],
                           w[:, o['gate_a']:o['z_a']],
                           jnp.zeros((D_MODEL, PROJ_WIDTH - o['end']), f32)], axis=1).astype(bf16)
    cw = _compress_weights(cmp_pe[0], cmp_w1[0], cmp_b1[0], cmp_w2[0])
    m_mat = _sel_map_matrix()
    w_out_b, ple_proj_b, ple_gate_b = w_out[0].astype(bf16), ple_proj[0].astype(bf16), ple_gate[0].astype(bf16)

    xp = x_prompt.reshape(bsz * seq, D_MODEL)
    zp = _proj(xp, g_pre, w_r, 1024)
    cmp_p = zp[:, COL_CMP:COL_CMP + 2 * KV_WIDTH]
    slc_p = zp[:, COL_SLC:COL_SLC + 2 * KV_WIDTH]
    win_p = zp[:, COL_WIN:COL_WIN + 2 * KV_WIDTH]
    pages_p = cmp_p.reshape(bsz * seq // PAGE_SIZE, PAGE_SIZE, ROW_WIDTH)
    table_p = jnp.arange(bsz * seq // PAGE_SIZE, dtype=jnp.int32).reshape(bsz, seq // PAGE_SIZE)
    comp_p = _compress(pages_p, table_p, *cw, feature_major=False)
    q_st = zp[:, COL_QA:COL_QA + NSA_WIDTH].astype(bf16)
    q_st = q_st.reshape(bsz, nq, Q_BLOCK, NSA_KV_HEADS, NSA_GROUP, HEAD_DIM).transpose(0, 1, 3, 4, 2, 5)
    q_st = q_st.reshape(bsz, nq, NSA_KV_HEADS, NSA_GROUP * Q_BLOCK, HEAD_DIM)
    o_st = _nsa_prompt(q_st, zp, comp_p, _attention_kv(slc_p, bsz, seq), _attention_kv(win_p, bsz, seq),
                       m_mat, _expand_matrix(seq // SEL_CHUNK))
    o_a = o_st.reshape(bsz, nq, NSA_KV_HEADS, NSA_GROUP, Q_BLOCK, HEAD_DIM).transpose(0, 1, 4, 2, 3, 5)
    o_a = o_a.reshape(bsz * seq, NSA_WIDTH)
    yb_p, hg_p = _hgrn_prompt(zp, hg_lower, hg_norm, bsz, seq)
    h_p = _finish(xp, o_a, zp, yb_p, p_prompt[0].reshape(bsz * seq, PLE_DIM),
                  w_out_b, g_post, ple_proj_b, ple_gate_b, 512)

    wl_p = min(WINDOW, seq)
    out_prompt = (h_p.reshape(bsz, seq, D_MODEL),
                  cmp_p.reshape((1, bsz, seq) + kv_tail),
                  slc_p.reshape((1, bsz, seq) + kv_tail),
                  win_p.reshape((bsz, seq) + kv_tail)[:, seq - wl_p:][None],
                  hg_p[None])

    xs = x_sample.reshape(nb, D_MODEL)
    zs = _proj(xs, g_pre, w_r, nb)
    cmp_t = cache_cmp_kv.transpose(0, 1, 3, 4, 5, 2).reshape(n_pool, ROW_WIDTH, PAGE_SIZE)
    comp_s = _compress(cmp_t, page_table, *cw, feature_major=True)
    q8 = zs[:, COL_QA:COL_QA + NSA_WIDTH].astype(bf16).reshape(nb, NSA_HEADS, HEAD_DIM)
    oc_s, ps_s = _nsa_sample_cmp(q8, comp_s, m_mat, past)
    ps_rows = ps_s[:, :NSA_KV_HEADS, :].reshape(nb * NSA_KV_HEADS, N_SELBLK)
    idx = _nsa_sample_topk(ps_rows, past // SEL_BLOCK - 1)
    gates_s = zs[:, COL_GATE:COL_GATE + LANES].reshape(nb, 1, LANES)
    slc_t =cache_slc_kv.transpose(0, 1, 3, 4, 5, 2).reshape(n_pool, N_COMBO, HEAD_DIM, PAGE_SIZE)
    wl_s = cache_win_kv.shape[2]
    win_t = cache_win_kv.transpose(0, 1, 3, 4, 5, 2).reshape(nb, N_COMBO, HEAD_DIM, wl_s)
    oa_s, win_s = _nsa_sample_sel(page_table, idx[:, :N_SEL].reshape(-1), slc_t, q8, oc_s, gates_s,
                                  zs[:, COL_SLC:COL_SLC + 2 * KV_WIDTH].reshape(nb, 1, 2 * KV_WIDTH),
                                  zs[:, COL_WIN:COL_WIN + 2 * KV_WIDTH].reshape(nb, 1, 2 * KV_WIDTH),
                                  win_t, past)
    win_s = win_s.reshape((1, nb) + kv_tail + (wl_s,)).transpose(0, 1, 5, 2, 3, 4)
    yb_s, hg_s = _hgrn_sample(zs, state_hgrn, hg_lower, hg_norm)
    h_s = _finish(xs, oa_s.reshape(nb, NSA_WIDTH), zs, yb_s, p_sample[0].reshape(nb, PLE_DIM),
                  w_out_b, g_post, ple_proj_b, ple_gate_b, nb)

    out_sample = (h_s.reshape(nb, 1, D_MODEL),
                  zs[:, COL_CMP:COL_CMP + 2 * KV_WIDTH].reshape((1, nb, 1) + kv_tail),
                  zs[:, COL_SLC:COL_SLC + 2 * KV_WIDTH].reshape((1, nb, 1) + kv_tail),
                  win_s,
                  hg_s)
    return (out_prompt[0], out_sample[0]) + out_prompt[1:] + out_sample[1:]
```

```python
import functools

import numpy as np
import jax
import jax.numpy as jnp
from jax import lax
from jax.experimental import pallas as pl
from jax.experimental.pallas import tpu as pltpu

D_MODEL = 1024
PAGE_SIZE = 128
NSA_HEADS = 8
NSA_KV_HEADS = 2
NSA_GROUP = NSA_HEADS // NSA_KV_HEADS
HEAD_DIM = 64
NSA_WIDTH = NSA_HEADS * HEAD_DIM
KV_WIDTH = NSA_KV_HEADS * HEAD_DIM
CMP_BLOCK = 32
CMP_STRIDE = 16
CMP_RATIO = CMP_BLOCK // CMP_STRIDE
CMP_HIDDEN = 128
SEL_BLOCK = 64
SEL_RATIO = SEL_BLOCK // CMP_STRIDE
N_SEL = 16
WINDOW = 512
Q_BLOCK = 128
HG_HEADS = 4
HG_DIM = 128
HG_WIDTH = HG_HEADS * HG_DIM
PLE_DIM = 256
EPS = 1e-6
NEG = -1e30
FORCE_BONUS = 1e6
REMOVED = -3e38

COL_QA, COL_ZA, COL_QB, COL_FB, COL_IB, COL_ZB = 0, 512, 1024, 1536, 2048, 2560
COL_CMP, COL_SLC, COL_WIN, COL_GATE = 3072, 3328, 3584, 3840
PROJ_WIDTH = 4096
_ORIG = dict(q_a=0, k_cmp=512, gate_a=1280, z_a=1304, q_b=1816, f_b=2328, i_b=2840, z_b=3352, end=3864)

LANES = 128
VMEM_LIMIT = 56 * 1024 * 1024

SEL_CHUNK = 512
WIN_KEYS = WINDOW + Q_BLOCK
N_CMP_PAD = 512
N_SELBLK = 128

f32 = jnp.float32
bf16 = jnp.bfloat16


def _cparams(sem):
    return pltpu.CompilerParams(dimension_semantics=sem, vmem_limit_bytes=VMEM_LIMIT)


def _dot(a, b):
    return jnp.dot(a, b, preferred_element_type=f32)


def _dot_nt(a, b):
    return lax.dot_general(a, b, (((1,), (1,)), ((), ())), preferred_element_type=f32)


def _dot_tn(a, b):
    return lax.dot_general(a, b, (((0,), (0,)), ((), ())), preferred_element_type=f32)


def _sigmoid(x):
    return 1.0 / (1.0 + jnp.exp(-x))


def _silu(x):
    return x * _sigmoid(x)


def _proj_kernel(x_ref, g_ref, w_ref, z_ref, xn_ref):
    @pl.when(pl.program_id(1) == 0)
    def _():
        x = x_ref[...]
        y = x * lax.rsqrt(jnp.mean(x * x, axis=-1, keepdims=True) + EPS)
        xn_ref[...] = (y * g_ref[...]).astype(bf16)

    z_ref[...] = _dot(xn_ref[...], w_ref[...])


def _proj(x, g, w, tm):
    n = x.shape[0]
    tn = 1024
    return pl.pallas_call(
        _proj_kernel,
        grid=(n // tm, PROJ_WIDTH // tn),
        in_specs=[pl.BlockSpec((tm, D_MODEL), lambda i, j: (i, 0)),
                  pl.BlockSpec((1, D_MODEL), lambda i, j: (0, 0)),
                  pl.BlockSpec((D_MODEL, tn), lambda i, j: (0, j))],
        out_specs=pl.BlockSpec((tm, tn), lambda i, j: (i, j)),
        out_shape=jax.ShapeDtypeStruct((n, PROJ_WIDTH), f32),
        scratch_shapes=[pltpu.VMEM((tm, D_MODEL), bf16)],
        compiler_params=_cparams(("parallel", "arbitrary")),
        name="proj",
    )(x, g, w)


def _finish_kernel(x_ref, oa_ref, za_ref, yb_ref, p_ref, wo_ref, gp_ref, pp_ref, pg_ref, h_ref):
    ya = (oa_ref[...] * _silu(za_ref[...])).astype(bf16)
    yb = yb_ref[...].astype(bf16)
    mix = _dot(ya, wo_ref[0:NSA_WIDTH, :]) + _dot(yb, wo_ref[NSA_WIDTH:, :])
    nrm = mix * lax.rsqrt(jnp.mean(mix * mix, axis=-1, keepdims=True) + EPS) * gp_ref[...]
    h = x_ref[...] + nrm
    gate = _sigmoid(_dot(h.astype(bf16), pg_ref[...]))
    h_ref[...] = h + _dot(p_ref[...].astype(bf16), pp_ref[...]) * gate


def _finish(x, oa, z, yb, p, w_out, g_post, ple_proj, ple_gate, tm):
    n = x.shape[0]
    za_blk = COL_ZA // NSA_WIDTH
    row = lambda i: (i, 0)
    const = lambda i: (0, 0)
    return pl.pallas_call(
        _finish_kernel,
        grid=(n // tm,),
        in_specs=[pl.BlockSpec((tm, D_MODEL), row),
                  pl.BlockSpec((tm, NSA_WIDTH), row),
                  pl.BlockSpec((tm, NSA_WIDTH), lambda i: (i, za_blk)),
                  pl.BlockSpec((tm, HG_WIDTH), row),
                  pl.BlockSpec((tm, PLE_DIM), row),
                  pl.BlockSpec((D_MODEL, D_MODEL), const),
                  pl.BlockSpec((1, D_MODEL), const),
                  pl.BlockSpec((PLE_DIM, D_MODEL), const),
                  pl.BlockSpec((D_MODEL, D_MODEL), const)],
        out_specs=pl.BlockSpec((tm, D_MODEL), row),
        out_shape=jax.ShapeDtypeStruct((n, D_MODEL), f32),
        compiler_params=_cparams(("parallel",)),
        name="finish",
    )(x, oa, z, yb, p, w_out, g_post, ple_proj, ple_gate)


SEGS_PER_PAGE = PAGE_SIZE // CMP_STRIDE
ROW_WIDTH = 2 * KV_WIDTH
SEG_WIDTH = CMP_STRIDE * ROW_WIDTH
N_COMBO = 2 * NSA_KV_HEADS


def _compress_kernel(pt_ref, pages_ref, wa_ref, wb_ref, pea_ref, peb_ref, b1_ref, w2_ref, out_ref,
                     stage_ref, rows_ref, sem_ref, *, n_pages, feature_major):
    b = pl.program_id(0)
    nb = pl.num_programs(0)
    slot = lax.rem(b, 2)

    def page_copy(seq, s, j):
        return pltpu.make_async_copy(pages_ref.at[pt_ref[seq, j]], stage_ref.at[s, j], sem_ref.at[s])

    def start_all(seq, s):
        def body(j, c):
            page_copy(seq, s, j).start()
            return c
        lax.fori_loop(0, n_pages, body, 0)

    @pl.when(b == 0)
    def _():
        start_all(0, 0)

    @pl.when(b + 1 < nb)
    def _():
        start_all(b + 1, 1 - slot)

    def wait_body(j, c):
        page_copy(b, slot, j).wait()
        return c
    lax.fori_loop(0, n_pages, wait_body, 0)

    lane_halves = ROW_WIDTH // LANES

    def to_rows(j, c):
        tile = stage_ref[slot, j]
        rows = tile.T if feature_major else tile
        r0 = pl.multiple_of(j * PAGE_SIZE, PAGE_SIZE)
        for h in range(lane_halves):
            rows_ref[h, pl.ds(r0, PAGE_SIZE), :] = rows[:, h * LANES:(h + 1) * LANES]
        return c
    lax.fori_loop(0, n_pages, to_rows, 0, unroll=4)

    n_seg = n_pages * SEGS_PER_PAGE
    for kind in range(lane_halves):
        cols = slice(kind * LANES, (kind + 1) * LANES)
        seg_a, seg_b = [], []
        for j in range(CMP_STRIDE):
            r = rows_ref[kind, pl.ds(j, n_seg, stride=CMP_STRIDE), :]
            seg_a.append((r + pea_ref[j][:, cols]).astype(bf16))
            seg_b.append((r + peb_ref[j][:, cols]).astype(bf16))
        pa = _dot(jnp.concatenate(seg_a, axis=1), wa_ref[kind])
        pb = _dot(jnp.concatenate(seg_b, axis=1), wb_ref[kind])
        pb = pltpu.roll(pb, n_seg - 1, axis=0)
        hid = _silu(pa + pb + b1_ref[kind])
        for g in range(NSA_KV_HEADS):
            hg = hid[:, g * CMP_HIDDEN:(g + 1) * CMP_HIDDEN].astype(bf16)
            out_ref[0, kind * NSA_KV_HEADS + g] = _dot(hg, w2_ref[kind])


def _compress(pages, page_table, wa, wb, pea, peb, b1, w2, feature_major):
    nseq, n_pages = page_table.shape
    n_seg = n_pages * SEGS_PER_PAGE
    const2 = lambda b, pt: (0, 0)
    const3 = lambda b, pt: (0, 0, 0)
    once = pl.Buffered(1)
    grid_spec = pltpu.PrefetchScalarGridSpec(
        num_scalar_prefetch=1,
        grid=(nseq,),
        in_specs=[pl.BlockSpec(memory_space=pl.ANY),
                  pl.BlockSpec(wa.shape, const3, pipeline_mode=once),
                  pl.BlockSpec(wb.shape, const3, pipeline_mode=once),
                  pl.BlockSpec((CMP_STRIDE, 1, ROW_WIDTH), const3),
                  pl.BlockSpec((CMP_STRIDE, 1, ROW_WIDTH), const3),
                  pl.BlockSpec(b1.shape, const3),
                  pl.BlockSpec((2, CMP_HIDDEN, HEAD_DIM), const3)],
        out_specs=pl.BlockSpec((1, N_COMBO, n_seg, HEAD_DIM), lambda b, pt: (b, 0, 0, 0)),
        scratch_shapes=[pltpu.VMEM((2, n_pages) + pages.shape[1:], f32),
                        pltpu.VMEM((ROW_WIDTH // LANES, n_pages * PAGE_SIZE, LANES), f32),
                        pltpu.SemaphoreType.DMA((2,))],
    )
    return pl.pallas_call(
        functools.partial(_compress_kernel, n_pages=n_pages, feature_major=feature_major),
        grid_spec=grid_spec,
        out_shape=jax.ShapeDtypeStruct((nseq, N_COMBO, n_seg, HEAD_DIM), f32),
        compiler_params=_cparams(("arbitrary",)),
        name="compress",
    )(page_table, pages, wa, wb, pea, peb, b1, w2)


def _compress_weights(cmp_pe, cmp_w1, cmp_b1, cmp_w2):
    eye = jnp.eye(NSA_KV_HEADS, dtype=f32)

    def big(w_half):
        t = jnp.einsum('kjdh,gb->kjbdgh', w_half, eye)
        return t.reshape(2, CMP_STRIDE * KV_WIDTH, NSA_KV_HEADS * CMP_HIDDEN)

    w1 = cmp_w1.reshape(2, 2, CMP_STRIDE, HEAD_DIM, CMP_HIDDEN)
    wa = big(w1[:, 0]).astype(bf16)
    wb = big(w1[:, 1]).astype(bf16)
    pe = cmp_pe.reshape(2, 2, CMP_STRIDE, HEAD_DIM)

    def pe_row(p):
        t = jnp.broadcast_to(p.transpose(1, 0, 2)[:, :, None, :],
                             (CMP_STRIDE, 2, NSA_KV_HEADS, HEAD_DIM))
        return t.reshape(CMP_STRIDE, 1, ROW_WIDTH)

    pea, peb = pe_row(pe[:, 0]), pe_row(pe[:, 1])
    b1 = jnp.broadcast_to(cmp_b1[:, None, :], (2, NSA_KV_HEADS, CMP_HIDDEN)).reshape(2, 1, NSA_KV_HEADS * CMP_HIDDEN)
    return wa, wb, pea, peb, b1, cmp_w2.astype(bf16)


def _sel_map_matrix():
    wts = np.convolve(np.ones(SEL_RATIO), np.ones(CMP_RATIO))
    m = np.zeros((N_CMP_PAD, N_SELBLK), np.float32)
    for j in range(N_SELBLK):
        for k, wk in enumerate(wts):
            n = SEL_RATIO * j + k
            if n < N_CMP_PAD - 1:
                m[n, j] = wk
    return jnp.asarray(m, dtype=bf16)


def _expand_matrix(n_chunks):
    key = np.arange(n_chunks * SEL_CHUNK) // SEL_BLOCK
    e = (np.arange(N_SELBLK)[:, None] == key[None, :]).astype(np.float32)
    return jnp.asarray(e.reshape(N_SELBLK, n_chunks, SEL_CHUNK).transpose(1, 0, 2), dtype=bf16)


def _importance_to_blocks(imp, m):
    hi = imp.astype(bf16)
    lo = (imp - hi.astype(f32)).astype(bf16)
    return _dot(hi, m) + _dot(lo, m)


def _nsa_prompt_kernel(q_ref, gate_ref, kc_ref, vc_ref, kvs_ref, kvw_ref, m_ref, e_ref, o_ref,
                       acc_ref, mx_ref, used_ref):
    g = pl.program_id(1)
    qi = pl.program_id(2)
    q0 = qi * Q_BLOCK
    rows = NSA_GROUP * Q_BLOCK

    q64 = q_ref[0, 0, 0] * jnp.asarray(HEAD_DIM ** -0.5, bf16)
    q128 = jnp.concatenate([q64, jnp.zeros_like(q64)], axis=1)
    slope = [jnp.where(g == 0, 2.0 ** -(r + 1), 2.0 ** -(NSA_GROUP + r + 1)).astype(f32)
             for r in range(NSA_GROUP)]

    qrow = lax.broadcasted_iota(jnp.int32, (Q_BLOCK, 1), 0)

    def softmax_rows(s, dist, mask):
        es, ls = [], []
        for r in range(NSA_GROUP):
            sr = jnp.where(mask, s[r * Q_BLOCK:(r + 1) * Q_BLOCK] - slope[r] * dist, NEG)
            m = jnp.max(sr, axis=1, keepdims=True)
            e = jnp.where(mask, jnp.exp(sr - m), 0.0)
            es.append(e)
            ls.append(1.0 / jnp.maximum(jnp.sum(e, axis=1, keepdims=True), 1e-30))
        return es, ls

    kc = kc_ref[0, 0].astype(bf16)
    vc = vc_ref[0, 0].astype(bf16)
    ncol = lax.broadcasted_iota(jnp.int32, (1, N_CMP_PAD), 1)
    d_c = (q0 + qrow) - (ncol * CMP_STRIDE + (CMP_BLOCK - 1))
    es, ls = softmax_rows(_dot_nt(q64, kc), d_c.astype(f32), d_c >= 0)
    ps_c = [e * l for e, l in zip(es, ls)]
    o_c = _dot(jnp.concatenate(ps_c, axis=0).astype(bf16), vc)
    imp = ps_c[0] + ps_c[1] + ps_c[2] + ps_c[3]
    p_sel = _importance_to_blocks(imp, m_ref[...])

    ps_t = p_sel.T
    blk = lax.broadcasted_iota(jnp.int32, (N_SELBLK, Q_BLOCK), 0)
    tq = q0 + lax.broadcasted_iota(jnp.int32, (N_SELBLK, Q_BLOCK), 1)
    jt = jnp.right_shift(tq, SEL_BLOCK.bit_length() - 1)
    valid = blk <= jt
    forced = (blk == 0) | (blk == jt) | (blk == jt - 1)
    score = jnp.where(valid, ps_t + jnp.where(forced, FORCE_BONUS, 0.0), NEG)
    sel_t = jnp.zeros((N_SELBLK, Q_BLOCK), f32)
    for _ in range(N_SEL):
        best = jnp.max(score, axis=0, keepdims=True)
        idx = jnp.min(jnp.where(score == best, blk, N_SELBLK), axis=0, keepdims=True)
        hit = blk == idx
        sel_t = jnp.where(hit & (best > NEG * 0.5), 1.0, sel_t)
        score = jnp.where(hit, REMOVED, score)
    sel = sel_t.T.astype(bf16)

    acc_ref[...] = jnp.zeros_like(acc_ref)
    mx_ref[...] = jnp.full_like(mx_ref, NEG)
    kcol = lax.broadcasted_iota(jnp.int32, (1, SEL_CHUNK), 1)
    d_base = (qrow - kcol).astype(f32)

    blk_per_chunk = SEL_CHUNK // SEL_BLOCK
    for c in range(e_ref.shape[0]):
        picked = jnp.max(sel_t[c * blk_per_chunk:(c + 1) * blk_per_chunk, :])
        used_ref[c] = (picked > 0.5).astype(jnp.int32)

    def sel_chunk(c, carry):
        @pl.when(used_ref[c] > 0)
        def _():
            sel_chunk_body(c)
        return carry

    def sel_chunk_body(c):
        k0 = pl.multiple_of(c * SEL_CHUNK, SEL_CHUNK)
        kv = kvs_ref[0, 0, pl.ds(k0, SEL_CHUNK), :]
        s = _dot_nt(q128, kv[:, 0:LANES])
        dist = d_base + (q0 - k0).astype(f32)
        allowed = (_dot(sel, e_ref[c]) > 0.5) & (dist >= 0.0)
        ps = []
        for r in range(NSA_GROUP):
            rs = slice(r * Q_BLOCK, (r + 1) * Q_BLOCK)
            sr = jnp.where(allowed, s[rs] - slope[r] * dist, NEG)
            m_old = mx_ref[rs]
            m_new = jnp.maximum(m_old, jnp.max(sr, axis=1, keepdims=True))
            acc_ref[rs] = acc_ref[rs] * jnp.exp(m_old - m_new)
            mx_ref[rs] = m_new
            ps.append(jnp.exp(sr - m_new).astype(bf16))
        acc_ref[...] += _dot(jnp.concatenate(ps, axis=0), kv[:, LANES:])

    n_chunks = (q0 + Q_BLOCK + SEL_CHUNK - 1) // SEL_CHUNK
    lax.fori_loop(0, n_chunks, sel_chunk, 0)
    acc = acc_ref[...]
    o_s = acc[:, 0:HEAD_DIM] * (1.0 / jnp.maximum(acc[:, HEAD_DIM:HEAD_DIM + 1], 1e-30))

    w0 = pl.multiple_of(jnp.maximum(q0 - WINDOW, 0), Q_BLOCK)
    kvw = kvw_ref[0, 0, pl.ds(w0, WIN_KEYS), :]
    wcol = lax.broadcasted_iota(jnp.int32, (1, WIN_KEYS), 1)
    d_w = (q0 - w0) + qrow - wcol
    es, _ = softmax_rows(_dot_nt(q128, kvw[:, 0:LANES]), d_w.astype(f32), (d_w >= 0) & (d_w <= WINDOW))
    acc_w = _dot(jnp.concatenate(es, axis=0).astype(bf16), kvw[:, LANES:])
    o_w = acc_w[:, 0:HEAD_DIM] * (1.0 / jnp.maximum(acc_w[:, HEAD_DIM:HEAD_DIM + 1], 1e-30))

    gate = _sigmoid(gate_ref[...])

    def gate_col(r, j):
        lo = gate[:, r * 3 + j:r * 3 + j + 1]
        hi = gate[:, (NSA_GROUP + r) * 3 + j:(NSA_GROUP + r) * 3 + j + 1]
        return jnp.where(g == 0, lo, hi)

    for r in range(NSA_GROUP):
        rs = slice(r * Q_BLOCK, (r + 1) * Q_BLOCK)
        o_ref[0, 0, 0, rs, :] = gate_col(r, 0) * o_c[rs] + gate_col(r, 1) * o_s[rs] + gate_col(r, 2) * o_w[rs]


def _nsa_prompt(q_st, z, comp, kv_slc, kv_win, m_mat, e_mat):
    bsz, nq = q_st.shape[0], q_st.shape[1]
    seq = kv_slc.shape[2]
    rows = NSA_GROUP * Q_BLOCK
    gate_blk = COL_GATE // LANES
    return pl.pallas_call(
        _nsa_prompt_kernel,
        grid=(bsz, NSA_KV_HEADS, nq),
        in_specs=[pl.BlockSpec((1, 1, 1, rows, HEAD_DIM), lambda b, g, i: (b, i, g, 0, 0)),
                  pl.BlockSpec((Q_BLOCK, LANES), lambda b, g, i: (b * nq + i, gate_blk)),
                  pl.BlockSpec((1, 1, N_CMP_PAD, HEAD_DIM), lambda b, g, i: (b, g, 0, 0)),
                  pl.BlockSpec((1, 1, N_CMP_PAD, HEAD_DIM), lambda b, g, i: (b, NSA_KV_HEADS + g, 0, 0)),
                  pl.BlockSpec((1, 1, seq, 2 * LANES), lambda b, g, i: (b, g, 0, 0)),
                  pl.BlockSpec((1, 1, seq, 2 * LANES), lambda b, g, i: (b, g, 0, 0)),
                  pl.BlockSpec((N_CMP_PAD, N_SELBLK), lambda b, g, i: (0, 0)),
                  pl.BlockSpec(e_mat.shape, lambda b, g, i: (0, 0, 0))],
        out_specs=pl.BlockSpec((1, 1, 1, rows, HEAD_DIM), lambda b, g, i: (b, i, g, 0, 0)),
        out_shape=jax.ShapeDtypeStruct((bsz, nq, NSA_KV_HEADS, rows, HEAD_DIM), f32),
        scratch_shapes=[pltpu.VMEM((rows, LANES), f32), pltpu.VMEM((rows, 1), f32),
                        pltpu.SMEM((e_mat.shape[0],), jnp.int32)],
        compiler_params=_cparams(("parallel", "parallel", "arbitrary")),
        name="nsa_prompt",
    )(q_st, z, comp, comp, kv_slc, kv_win, m_mat, e_mat)


def _attention_kv(piece, bsz, seq):
    t = piece.reshape(bsz, seq, 2, NSA_KV_HEADS, HEAD_DIM).transpose(2, 0, 3, 1, 4)
    zeros = jnp.zeros_like(t[0])
    ones = jnp.ones(t[0].shape[:-1] + (1,), f32)
    out = jnp.concatenate([t[0], zeros, t[1], ones, zeros[..., :HEAD_DIM - 1]], axis=-1)
    return out.astype(bf16)


LOG2E = 1.4426950408889634
MASK_BIG = 1e30
AUG_ROWS = 16
VT_ROWS = HEAD_DIM + 16
KEY_LOW = 256


def _nsa_prompt_t_kernel(qt_ref, gate_ref, kc_ref, vc_ref, ks_ref, vts_ref, kw_ref, vtw_ref, mt_ref, o_ref,
                         qa_ref, acc_ref, mg_ref, bias_ref, used_ref):
    g = pl.program_id(1)
    qi = pl.program_id(2)
    q0 = qi * Q_BLOCK
    cols = NSA_GROUP * Q_BLOCK

    lane = lax.broadcasted_iota(jnp.int32, (1, cols), 1)
    head = jnp.right_shift(lane, Q_BLOCK.bit_length() - 1)
    tq = q0 + jnp.bitwise_and(lane, Q_BLOCK - 1)
    c = jnp.exp2(-(head.astype(f32) + 1.0 + NSA_GROUP * g.astype(f32))) * LOG2E

    qt = (qt_ref[0, 0, 0] * (HEAD_DIM ** -0.5 * LOG2E)).astype(bf16)

    def softmax_cols(s, mask):
        s = jnp.where(mask, s, NEG)
        m = jnp.max(s, axis=0, keepdims=True)
        e = jnp.where(mask, jnp.exp2(s - m), 0.0)
        return e, 1.0 / jnp.maximum(jnp.sum(e, axis=0, keepdims=True), 1e-30)

    kc = kc_ref[0, 0].astype(bf16)
    vc = vc_ref[0, 0].astype(bf16)
    nrow = lax.broadcasted_iota(jnp.int32, (N_CMP_PAD, 1), 0)
    d_c = tq - (nrow * CMP_STRIDE + (CMP_BLOCK - 1))
    e, rl = softmax_cols(_dot(kc, qt) - c * d_c.astype(f32), d_c >= 0)
    p_c = e * rl
    oc_t = _dot_tn(vc, p_c.astype(bf16))
    imp_t = (p_c[:, 0:Q_BLOCK] + p_c[:, Q_BLOCK:2 * Q_BLOCK]
             + p_c[:, 2 * Q_BLOCK:3 * Q_BLOCK] + p_c[:, 3 * Q_BLOCK:4 * Q_BLOCK])
    hi = imp_t.astype(bf16)
    lo = (imp_t - hi.astype(f32)).astype(bf16)
    ps_t = _dot(mt_ref[...], hi) + _dot(mt_ref[...], lo)

    blk = lax.broadcasted_iota(jnp.int32, (N_SELBLK, Q_BLOCK), 0)
    jt = jnp.right_shift(q0 + lax.broadcasted_iota(jnp.int32, (N_SELBLK, Q_BLOCK), 1), SEL_BLOCK.bit_length() - 1)
    valid = blk <= jt
    forced = (blk == 0) | (blk == jt) | (blk == jt - 1)
    score = jnp.where(valid, ps_t + jnp.where(forced, FORCE_BONUS, 0.0), NEG)
    sel_t = jnp.zeros((N_SELBLK, Q_BLOCK), f32)
    for _ in range(N_SEL):
        best = jnp.max(score, axis=0, keepdims=True)
        idx = jnp.min(jnp.where(score == best, blk, N_SELBLK), axis=0, keepdims=True)
        hit = blk == idx
        sel_t = jnp.where(hit & (best > NEG * 0.5), 1.0, sel_t)
        score = jnp.where(hit, REMOVED, score)
    bias_ref[...] = (sel_t - 1.0) * MASK_BIG

    blk_per_chunk = SEL_CHUNK // SEL_BLOCK
    n_chunk_max = vts_ref.shape[2]
    for ck in range(n_chunk_max):
        picked = jnp.max(sel_t[ck * blk_per_chunk:(ck + 1) * blk_per_chunk, :])
        used_ref[ck] = (picked > 0.5).astype(jnp.int32)

    c_hi = c.astype(bf16).astype(f32)
    c_mid = (c - c_hi).astype(bf16).astype(f32)
    c_lo = (c - c_hi - c_mid).astype(bf16).astype(f32)
    arow = lax.broadcasted_iota(jnp.int32, (AUG_ROWS // 2, cols), 0)
    alibi_rows = jnp.where(arow >= 6, 0.0,
                           jnp.where((arow == 0) | (arow == 3), c_hi, jnp.where((arow == 1) | (arow == 4), c_mid, c_lo)))
    qa_ref[0:HEAD_DIM, :] = qt
    qa_ref[HEAD_DIM + AUG_ROWS:, :] = jnp.zeros((qa_ref.shape[0] - HEAD_DIM - AUG_ROWS, cols), bf16)
    acc_ref[...] = jnp.zeros_like(acc_ref)
    mg_ref[...] = jnp.full_like(mg_ref, NEG)
    krow = lax.broadcasted_iota(jnp.int32, (SEL_CHUNK, 1), 0)

    def sel_chunk(ck, causal):
        k0 = pl.multiple_of(ck * SEL_CHUNK, SEL_CHUNK)
        b8 = bias_ref[pl.ds(pl.multiple_of(ck * blk_per_chunk, blk_per_chunk), blk_per_chunk), :]
        qa_ref[HEAD_DIM:HEAD_DIM + AUG_ROWS, :] = jnp.concatenate(
            [jnp.concatenate([b8] * NSA_GROUP, axis=1), alibi_rows], axis=0).astype(bf16)
        s = _dot(ks_ref[0, 0, pl.ds(k0, SEL_CHUNK), :], qa_ref[...])
        if causal:
            s = jnp.where(k0 + krow <= tq, s, NEG)
        shift = c * k0.astype(f32)
        m_old = mg_ref[...] - shift
        m_new = jnp.maximum(m_old, jnp.max(s, axis=0, keepdims=True))
        p = jnp.exp2(s - m_new).astype(bf16)
        acc_ref[...] = acc_ref[...] * jnp.exp2(m_old - m_new) + _dot(vts_ref[0, 0, ck], p)
        mg_ref[...] = m_new + shift

    def past_chunk(ck, carry):
        @pl.when(used_ref[ck] > 0)
        def _():
            sel_chunk(ck, causal=False)
        return carry

    last = q0 // SEL_CHUNK
    lax.fori_loop(0, last, past_chunk, 0)
    sel_chunk(last, causal=True)
    acc = acc_ref[...]
    os_t = acc[0:HEAD_DIM] * (1.0 / jnp.maximum(acc[HEAD_DIM:HEAD_DIM + 1], 1e-30))

    w0 = pl.multiple_of(jnp.maximum(q0 - WINDOW, 0), Q_BLOCK)
    kw = kw_ref[0, 0, pl.ds(w0, WIN_KEYS), :]
    q_pad = jnp.concatenate([qt, jnp.zeros((LANES - HEAD_DIM, cols), bf16)], axis=0)
    wrow = lax.broadcasted_iota(jnp.int32, (WIN_KEYS, 1), 0)
    d_w = tq - (w0 + wrow)
    e, _ = softmax_cols(_dot(kw, q_pad) - c * d_w.astype(f32), (d_w >= 0) & (d_w <= WINDOW))
    e = e.astype(bf16)
    t0 = w0 // Q_BLOCK
    acc_w = jnp.zeros((VT_ROWS, cols), f32)
    for t in range(WIN_KEYS // Q_BLOCK):
        acc_w = acc_w + _dot(vtw_ref[0, 0, t0 + t], e[t * Q_BLOCK:(t + 1) * Q_BLOCK])
    ow_t = acc_w[0:HEAD_DIM] * (1.0 / jnp.maximum(acc_w[HEAD_DIM:HEAD_DIM + 1], 1e-30))

    gate_t = _sigmoid(gate_ref[...]).T

    def gate_row(r, j):
        lo_row = gate_t[r * 3 + j:r * 3 + j + 1, :]
        hi_row = gate_t[(NSA_GROUP + r) * 3 + j:(NSA_GROUP + r) * 3 + j + 1, :]
        return jnp.where(g == 0, lo_row, hi_row)

    outs = []
    for r in range(NSA_GROUP):
        cs = slice(r * Q_BLOCK, (r + 1) * Q_BLOCK)
        outs.append(gate_row(r, 0) * oc_t[:, cs] + gate_row(r, 1) * os_t[:, cs] + gate_row(r, 2) * ow_t[:, cs])
    for pair in range(NSA_GROUP // 2):
        both = jnp.concatenate([outs[2 * pair], outs[2 * pair + 1]], axis=0)
        o_ref[:, pair * LANES:(pair + 1) * LANES] = both.T


def _nsa_prompt_t(q_t, z, comp, k_slc, vt_slc, k_win, vt_win, m_t):
    bsz, _, nq = q_t.shape[:3]
    seq = k_slc.shape[2]
    cols = NSA_GROUP * Q_BLOCK
    gate_blk = COL_GATE // LANES
    per_bg = lambda b, g, i: (b, g, 0, 0)
    per_bg5 = lambda b, g, i: (b, g, 0, 0, 0)
    return pl.pallas_call(
        _nsa_prompt_t_kernel,
        grid=(bsz, NSA_KV_HEADS, nq),
        in_specs=[pl.BlockSpec((1, 1, 1, HEAD_DIM, cols), lambda b, g, i: (b, g, i, 0, 0)),
                  pl.BlockSpec((Q_BLOCK, LANES), lambda b, g, i: (b * nq + i, gate_blk)),
                  pl.BlockSpec((1, 1, N_CMP_PAD, HEAD_DIM), per_bg),
                  pl.BlockSpec((1, 1, N_CMP_PAD, HEAD_DIM), lambda b, g, i: (b, NSA_KV_HEADS + g, 0, 0)),
                  pl.BlockSpec((1, 1, seq, LANES), per_bg),
                  pl.BlockSpec((1, 1) + vt_slc.shape[2:], per_bg5),
                  pl.BlockSpec((1, 1, seq, LANES), per_bg),
                  pl.BlockSpec((1, 1) + vt_win.shape[2:], per_bg5),
                  pl.BlockSpec((N_SELBLK, N_CMP_PAD), lambda b, g, i: (0, 0))],
        out_specs=pl.BlockSpec((Q_BLOCK, NSA_GROUP * HEAD_DIM), lambda b, g, i: (b * nq + i, g)),
        out_shape=jax.ShapeDtypeStruct((bsz * seq, NSA_WIDTH), f32),
        scratch_shapes=[pltpu.VMEM((LANES, cols), bf16),
                        pltpu.VMEM((VT_ROWS, cols), f32),
                        pltpu.VMEM((1, cols), f32),
                        pltpu.VMEM((N_SELBLK, Q_BLOCK), f32),
                        pltpu.SMEM((vt_slc.shape[2],), jnp.int32)],
        compiler_params=_cparams(("parallel", "parallel", "arbitrary")),
        name="nsa_prompt",
    )(q_t, z, comp, comp, k_slc, vt_slc, k_win, vt_win, m_t)


def _attention_keys(piece, bsz, seq, augment):
    tile = SEL_CHUNK if augment else Q_BLOCK
    t = piece.reshape(bsz, seq, 2, NSA_KV_HEADS, HEAD_DIM)
    k = t[:, :, 0].transpose(0, 2, 1, 3)
    pos = np.arange(seq)
    extra = np.zeros((seq, LANES - HEAD_DIM), np.float32)
    if augment:
        extra[np.arange(seq), (pos // SEL_BLOCK) % (SEL_CHUNK // SEL_BLOCK)] = 1.0
        extra[:, 8:11] = (pos % KEY_LOW)[:, None]
        extra[:, 11:14] = (pos % SEL_CHUNK - pos % KEY_LOW)[:, None]
    k_aug = jnp.concatenate([k, jnp.broadcast_to(jnp.asarray(extra), k.shape[:2] + extra.shape)], axis=-1)
    v_t = t[:, :, 1].transpose(0, 2, 3, 1)
    ones = jnp.ones(v_t.shape[:2] + (1, seq), f32)
    zeros = jnp.zeros(v_t.shape[:2] + (VT_ROWS - HEAD_DIM - 1, seq), f32)
    v_aug = jnp.concatenate([v_t, ones, zeros], axis=2)
    v_aug = v_aug.reshape(bsz, NSA_KV_HEADS, VT_ROWS, seq // tile, tile).transpose(0, 1, 3, 2, 4)
    return k_aug.astype(bf16), v_aug.astype(bf16)


HG_CHUNK = 128
HG_BAND = 8


def _lower_bound(lo):
    m = jnp.max(lo, axis=0, keepdims=True)
    e = jnp.exp(lo - m)
    return e[0:1] / jnp.sum(e, axis=0, keepdims=True)


def _hgrn_gates(q, fl, lb):
    f = lb + (1.0 - lb) * _sigmoid(fl)
    k = (1.0 - lb) * _sigmoid(-fl)
    return _silu(q), k, jnp.log(f)


def _hgrn_out(o, nw, zb):
    o = o * lax.rsqrt(jnp.mean(o * o, axis=-1, keepdims=True) + EPS) * nw
    return o * _silu(zb)


def _hgrn_prompt_kernel(q_ref, f_ref, i_ref, zb_ref, lo_ref, nw_ref, y_ref, s_ref, st_ref):
    c = pl.program_id(2)
    n = HG_CHUNK

    @pl.when(c == 0)
    def _():
        st_ref[...] = jnp.zeros_like(st_ref)

    lb = _lower_bound(lo_ref[...])
    qq, kk, logf = _hgrn_gates(q_ref[...], f_ref[...], lb)
    v = i_ref[...]
    row = lax.broadcasted_iota(jnp.int32, (n, n), 0)
    col = lax.broadcasted_iota(jnp.int32, (n, n), 1)

    b = logf
    sh = 1
    while sh < n:
        b = b + jnp.where(row >= sh, pltpu.roll(b, sh, axis=0), 0.0)
        sh *= 2

    a = jnp.zeros((n, n), f32)
    sub = jnp.bitwise_and(row, HG_BAND - 1)
    for d in range(HG_BAND):
        if d == 0:
            term = qq * kk
        else:
            ok = sub >= d
            decay = jnp.exp(jnp.where(ok, b - pltpu.roll(b, d, axis=0), 0.0))
            term = jnp.where(ok, qq * pltpu.roll(kk, d, axis=0) * decay, 0.0)
        a = a + jnp.where(col == row - d, jnp.sum(term, axis=1, keepdims=True), 0.0)

    size = 2 * HG_BAND
    while size <= n:
        half = size // 2
        qs, ks = [], []
        zero = jnp.zeros((half, HG_DIM), f32)
        for r0 in range(0, n, size):
            beta = b[r0 + half - 1:r0 + half]
            ks += [kk[r0:r0 + half] * jnp.exp(beta - b[r0:r0 + half]), zero]
            qs += [zero, qq[r0 + half:r0 + size] * jnp.exp(b[r0 + half:r0 + size] - beta)]
        part = _dot_nt(jnp.concatenate(qs, axis=0).astype(bf16), jnp.concatenate(ks, axis=0).astype(bf16))
        if size < n:
            sh = size.bit_length() - 1
            part = jnp.where(jnp.right_shift(row, sh) == jnp.right_shift(col, sh), part, 0.0)
        a = a + part
        size *= 2

    st = st_ref[...]
    o = _dot(a.astype(bf16), v.astype(bf16)) + _dot_nt((qq * jnp.exp(b)).astype(bf16), st.astype(bf16))
    b_last = b[n - 1:n]
    k_tail = kk * jnp.exp(b_last - b)
    st_new = st * jnp.exp(b_last) + _dot_tn(v.astype(bf16), k_tail.astype(bf16))
    st_ref[...] = st_new
    y_ref[...] = _hgrn_out(o, nw_ref[...], zb_ref[...])

    @pl.when(c == pl.num_programs(2) - 1)
    def _():
        s_ref[0, 0] = st_new.T


def _hgrn_prompt(z, hg_lower, hg_norm, bsz, seq):
    nch = seq // HG_CHUNK

    def zcol(base):
        return pl.BlockSpec((HG_CHUNK, HG_DIM), lambda b, h, c: (b * nch + c, base // HG_DIM + h))

    return pl.pallas_call(
        _hgrn_prompt_kernel,
        grid=(bsz, HG_HEADS, nch),
        in_specs=[zcol(COL_QB), zcol(COL_FB), zcol(COL_IB), zcol(COL_ZB),
                  pl.BlockSpec((hg_lower.shape[0], HG_DIM), lambda b, h, c: (0, h)),
                  pl.BlockSpec((1, HG_DIM), lambda b, h, c: (0, 0))],
        out_specs=[pl.BlockSpec((HG_CHUNK, HG_DIM), lambda b, h, c: (b * nch + c, h)),
                   pl.BlockSpec((1, 1, HG_DIM, HG_DIM), lambda b, h, c: (b, h, 0, 0))],
        out_shape=[jax.ShapeDtypeStruct((bsz * seq, HG_WIDTH), f32),
                   jax.ShapeDtypeStruct((bsz, HG_HEADS, HG_DIM, HG_DIM), f32)],
        scratch_shapes=[pltpu.VMEM((HG_DIM, HG_DIM), f32)],
        compiler_params=_cparams(("parallel", "parallel", "arbitrary")),
        name="hgrn_prompt",
    )(z, z, z, z, hg_lower, hg_norm)


def _hgrn_sample_kernel(q_ref, f_ref, i_ref, zb_ref, lo_ref, nw_ref, s_ref, y_ref, so_ref, o_scr):
    nb = q_ref.shape[0]
    lb = _lower_bound(lo_ref[...])
    qq, kk, logf = _hgrn_gates(q_ref[...], f_ref[...], lb)
    v = i_ref[...]
    pad = jnp.zeros((HG_DIM - nb, HG_DIM), f32)
    to_cols = lambda x: jnp.concatenate([x, pad], axis=0).T
    f_t, k_t, q_t = to_cols(jnp.exp(logf)), to_cols(kk), to_cols(qq)
    for b in range(nb):
        s_new = f_t[:, b:b + 1] * s_ref[0, b, 0] + k_t[:, b:b + 1] * v[b:b + 1, :]
        so_ref[0, b, 0] = s_new
        o_scr[b:b + 1, :] = jnp.sum(s_new * q_t[:, b:b + 1], axis=0, keepdims=True)
    y_ref[...] = _hgrn_out(o_scr[...], nw_ref[...], zb_ref[...])


def _hgrn_sample(z, state, hg_lower, hg_norm):
    nb = z.shape[0]

    def zcol(base):
        return pl.BlockSpec((nb, HG_DIM), lambda h: (0, base // HG_DIM + h))

    sblk = pl.BlockSpec((1, nb, 1, HG_DIM, HG_DIM), lambda h: (0, 0, h, 0, 0))
    return pl.pallas_call(
        _hgrn_sample_kernel,
        grid=(HG_HEADS,),
        in_specs=[zcol(COL_QB), zcol(COL_FB), zcol(COL_IB), zcol(COL_ZB),
                  pl.BlockSpec((hg_lower.shape[0], HG_DIM), lambda h: (0, h)),
                  pl.BlockSpec((1, HG_DIM), lambda h: (0, 0)),
                  sblk],
        out_specs=[pl.BlockSpec((nb, HG_DIM), lambda h: (0, h)), sblk],
        out_shape=[jax.ShapeDtypeStruct((nb, HG_WIDTH), f32),
                   jax.ShapeDtypeStruct(state.shape, f32)],
        scratch_shapes=[pltpu.VMEM((nb, HG_DIM), f32)],
        compiler_params=_cparams(("parallel",)),
        name="hgrn_sample",
    )(z, z, z, z, hg_lower, hg_norm, state)


N_PICK = N_SEL - 1


def _head_slopes():
    h = lax.broadcasted_iota(jnp.int32, (NSA_HEADS, 1), 0).astype(f32)
    return jnp.exp2(-(h + 1.0))


def _nsa_sample_cmp_kernel(q_ref, comp_ref, m_ref, oc_ref, ps_ref, *, t_pos):
    q = q_ref[0] * jnp.asarray(HEAD_DIM ** -0.5, bf16)
    slope = _head_slopes()
    head = lax.broadcasted_iota(jnp.int32, (NSA_HEADS, 1), 0)
    ncol = lax.broadcasted_iota(jnp.int32, (1, N_CMP_PAD), 1)
    d_c = t_pos - (ncol * CMP_STRIDE + (CMP_BLOCK - 1))
    mask = d_c >= 0
    o_c = jnp.zeros((NSA_HEADS, HEAD_DIM), f32)
    ps_ref[...] = jnp.zeros_like(ps_ref)
    for g in range(NSA_KV_HEADS):
        kc = comp_ref[0, g].astype(bf16)
        vc = comp_ref[0, NSA_KV_HEADS + g].astype(bf16)
        s = jnp.where(mask, _dot_nt(q, kc) - slope * d_c.astype(f32), NEG)
        m = jnp.max(s, axis=1, keepdims=True)
        e = jnp.where(mask, jnp.exp(s - m), 0.0)
        p = e * (1.0 / jnp.maximum(jnp.sum(e, axis=1, keepdims=True), 1e-30))
        mine = jnp.right_shift(head, NSA_GROUP.bit_length() - 1) == g
        o_c = jnp.where(mine, _dot(p.astype(bf16), vc), o_c)
        imp = jnp.sum(jnp.where(mine, p, 0.0), axis=0, keepdims=True)
        ps_ref[0, g:g + 1, :] = _importance_to_blocks(imp, m_ref[...])
    oc_ref[0] = o_c


def _nsa_sample_cmp(q8, comp, m_mat, t_pos):
    nb = q8.shape[0]
    return pl.pallas_call(
        functools.partial(_nsa_sample_cmp_kernel, t_pos=t_pos),
        grid=(nb,),
        in_specs=[pl.BlockSpec((1, NSA_HEADS, HEAD_DIM), lambda b: (b, 0, 0)),
                  pl.BlockSpec((1, N_COMBO, N_CMP_PAD, HEAD_DIM), lambda b: (b, 0, 0, 0)),
                  pl.BlockSpec((N_CMP_PAD, N_SELBLK), lambda b: (0, 0))],
        out_specs=[pl.BlockSpec((1, NSA_HEADS, HEAD_DIM), lambda b: (b, 0, 0)),
                   pl.BlockSpec((1, 8, N_SELBLK), lambda b: (b, 0, 0))],
        out_shape=[jax.ShapeDtypeStruct((nb, NSA_HEADS, HEAD_DIM), f32),
                   jax.ShapeDtypeStruct((nb, 8, N_SELBLK), f32)],
        compiler_params=_cparams(("parallel",)),
        name="nsa_sample_cmp",
    )(q8, comp, m_mat)


def _nsa_sample_topk_kernel(ps_ref, idx_ref, *, last_blk):
    ps = ps_ref[...]
    lane = lax.broadcasted_iota(jnp.int32, ps.shape, 1)
    score = ps + jnp.where((lane == 0) | (lane == last_blk), FORCE_BONUS, 0.0)
    out = jnp.zeros(ps.shape, jnp.int32)
    for r in range(N_PICK):
        best = jnp.max(score, axis=1, keepdims=True)
        idx = jnp.min(jnp.where(score == best, lane, N_SELBLK), axis=1, keepdims=True)
        out = jnp.where(lane == r, idx, out)
        score = jnp.where(lane == idx, REMOVED, score)
    idx_ref[...] = out


def _nsa_sample_topk(ps_rows, last_blk):
    return pl.pallas_call(
        functools.partial(_nsa_sample_topk_kernel, last_blk=last_blk),
        out_shape=jax.ShapeDtypeStruct(ps_rows.shape, jnp.int32),
        name="nsa_sample_topk",
    )(ps_rows)


def _nsa_sample_sel_kernel(pt_ref, idx_ref, slc_ref, q_ref, oc_ref, gate_ref, slc_new_ref, win_new_ref, wc_ref,
                           o_ref, wout_ref, kv_scr, sem_ref, *, t_pos):
    n_blk = NSA_KV_HEADS * N_PICK
    halves = PAGE_SIZE // SEL_BLOCK
    b = pl.program_id(0)
    nb = pl.num_programs(0)
    slot = lax.rem(b, 2)

    def picked(seq, k):
        return idx_ref[(seq * NSA_KV_HEADS + k // N_PICK) * N_SEL + k % N_PICK]

    def blk_copies(seq, s, k):
        page = pt_ref[seq, picked(seq, k) // halves]
        g = k // N_PICK
        return [pltpu.make_async_copy(slc_ref.at[page, kind * NSA_KV_HEADS + g], kv_scr.at[s, kind, k],
                                      sem_ref.at[s]) for kind in range(2)]

    def start_all(seq, s):
        for k in range(n_blk):
            for cp in blk_copies(seq, s, k):
                cp.start()

    @pl.when(b == 0)
    def _():
        start_all(0, 0)

    @pl.when(b + 1 < nb)
    def _():
        start_all(b + 1, 1 - slot)

    for k in range(n_blk):
        for cp in blk_copies(b, slot, k):
            cp.wait()

    q = q_ref[0] * jnp.asarray(HEAD_DIM ** -0.5, bf16)
    slope = _head_slopes()
    head = lax.broadcasted_iota(jnp.int32, (NSA_HEADS, 1), 0)
    lane = lax.broadcasted_iota(jnp.int32, (1, PAGE_SIZE), 1)
    new_slc = slc_new_ref[0]
    new_win = win_new_ref[0]
    wl = wc_ref.shape[3]
    wlane = lax.broadcasted_iota(jnp.int32, (1, wl), 1)
    d_w = (wl - wlane).astype(f32)
    qf = q.astype(f32)

    def attend(s, mask, v_t, new_row, g):
        k_new = new_row[:, g * HEAD_DIM:(g + 1) * HEAD_DIM].astype(bf16).astype(f32)
        v_new = new_row[:, KV_WIDTH + g * HEAD_DIM:KV_WIDTH + (g + 1) * HEAD_DIM].astype(bf16).astype(f32)
        s_new = jnp.sum(qf * k_new, axis=1, keepdims=True)
        m = jnp.maximum(jnp.max(s, axis=1, keepdims=True), s_new)
        e, e_new = jnp.exp(s - m), jnp.exp(s_new - m)
        if mask is not None:
            e = jnp.where(mask, e, 0.0)
        den = jnp.maximum(jnp.sum(e, axis=1, keepdims=True) + e_new, 1e-30)
        num = _dot_nt(e.astype(bf16), v_t.astype(bf16)) + e_new.astype(bf16).astype(f32) * v_new
        return num * (1.0 / den)

    o_s = jnp.zeros((NSA_HEADS, HEAD_DIM), f32)
    o_w = jnp.zeros((NSA_HEADS, HEAD_DIM), f32)
    for g in range(NSA_KV_HEADS):
        mine = jnp.right_shift(head, NSA_GROUP.bit_length() - 1) == g
        pos, member = [], []
        for n in range(N_PICK):
            j = picked(b, g * N_PICK + n)
            pos.append((j // halves) * PAGE_SIZE + lane)
            member.append(jnp.right_shift(lane, SEL_BLOCK.bit_length() - 1) == j % halves)
        pos = jnp.concatenate(pos, axis=1)
        member = jnp.concatenate(member, axis=1)
        k_t = jnp.concatenate([kv_scr[slot, 0, g * N_PICK + n] for n in range(N_PICK)], axis=1)
        v_t = jnp.concatenate([kv_scr[slot, 1, g * N_PICK + n] for n in range(N_PICK)], axis=1)
        s = jnp.where(member, _dot(q, k_t.astype(bf16)) - slope * (t_pos - pos).astype(f32), NEG)
        o_s = jnp.where(mine, attend(s, member, v_t, new_slc, g), o_s)
        s = _dot(q, wc_ref[0, g].astype(bf16)) - slope * d_w
        o_w = jnp.where(mine, attend(s, None, wc_ref[0, NSA_KV_HEADS + g], new_win, g), o_w)

    gate = jnp.broadcast_to(_sigmoid(gate_ref[0]), (NSA_HEADS, LANES))
    glane = lax.broadcasted_iota(jnp.int32, (NSA_HEADS, LANES), 1)
    gcol = [jnp.sum(jnp.where(glane == 3 * head + j, gate, 0.0), axis=1, keepdims=True) for j in range(3)]
    o_ref[0] = gcol[0] * oc_ref[0] + gcol[1] * o_s + gcol[2] * o_w
    eye = lax.broadcasted_iota(jnp.int32, (HEAD_DIM, HEAD_DIM), 0) == \
        lax.broadcasted_iota(jnp.int32, (HEAD_DIM, HEAD_DIM), 1)
    for c in range(N_COMBO):
        piece = jnp.broadcast_to(new_win[:, c * HEAD_DIM:(c + 1) * HEAD_DIM], (HEAD_DIM, HEAD_DIM))
        new_col = jnp.sum(jnp.where(eye, piece, 0.0), axis=1, keepdims=True)
        wout_ref[0, c] = jnp.where(wlane == wl - 1, new_col, pltpu.roll(wc_ref[0, c], wl - 1, axis=1))


def _nsa_sample_sel(page_table, idx_flat, slc_pages_t, q8, o_c, gates, slc_new, win_new, cache_win_t, t_pos):
    nb = q8.shape[0]
    wl = cache_win_t.shape[3]
    n_blk = NSA_KV_HEADS * N_PICK
    per_seq = lambda b, pt, idx: (b, 0, 0)
    win_blk = pl.BlockSpec((1, N_COMBO, HEAD_DIM, wl), lambda b, pt, idx: (b, 0, 0, 0))
    grid_spec = pltpu.PrefetchScalarGridSpec(
        num_scalar_prefetch=2,
        grid=(nb,),
        in_specs=[pl.BlockSpec(memory_space=pl.ANY),
                  pl.BlockSpec((1, NSA_HEADS, HEAD_DIM), per_seq),
                  pl.BlockSpec((1, NSA_HEADS, HEAD_DIM), per_seq),
                  pl.BlockSpec((1, 1, LANES), per_seq),
                  pl.BlockSpec((1, 1, 2 * KV_WIDTH), per_seq),
                  pl.BlockSpec((1, 1, 2 * KV_WIDTH), per_seq),
                  win_blk],
        out_specs=[pl.BlockSpec((1, NSA_HEADS, HEAD_DIM), per_seq), win_blk],
        scratch_shapes=[pltpu.VMEM((2, 2, n_blk, HEAD_DIM, PAGE_SIZE), f32),
                        pltpu.SemaphoreType.DMA((2,))],
    )
    return pl.pallas_call(
        functools.partial(_nsa_sample_sel_kernel, t_pos=t_pos),
        grid_spec=grid_spec,
        out_shape=[jax.ShapeDtypeStruct((nb, NSA_HEADS, HEAD_DIM), f32),
                   jax.ShapeDtypeStruct(cache_win_t.shape, f32)],
        compiler_params=_cparams(("arbitrary",)),
        name="nsa_sample_sel",
    )(page_table, idx_flat, slc_pages_t, q8, o_c, gates, slc_new, win_new, cache_win_t)


def kernel(x_prompt, x_sample, cache_cmp_kv, cache_slc_kv, cache_win_kv, state_hgrn, page_table, p_prompt, p_sample,
           w_in, g_pre, cmp_pe, cmp_w1, cmp_b1, cmp_w2, hg_lower, hg_norm, w_out, g_post, ple_proj, ple_gate):
    bsz, seq, _ = x_prompt.shape
    nb = x_sample.shape[0]
    n_pool = cache_cmp_kv.shape[1]
    n_pages = page_table.shape[1]
    past = n_pages * PAGE_SIZE
    nq = seq // Q_BLOCK
    kv_tail = (2, NSA_KV_HEADS, HEAD_DIM)

    o = _ORIG
    w = w_in[0]
    w_r = jnp.concatenate([w[:, o['q_a']:o['k_cmp']], w[:, o['z_a']:o['end']], w[:, o['k_cmp']:o['gate_a']],
                           w[:, o['gate_a']:o['z_a']],
                           jnp.zeros((D_MODEL, PROJ_WIDTH - o['end']), f32)], axis=1).astype(bf16)
    cw = _compress_weights(cmp_pe[0], cmp_w1[0], cmp_b1[0], cmp_w2[0])
    m_mat = _sel_map_matrix()
    w_out_b, ple_proj_b, ple_gate_b = w_out[0].astype(bf16), ple_proj[0].astype(bf16), ple_gate[0].astype(bf16)

    xp = x_prompt.reshape(bsz * seq, D_MODEL)
    zp = _proj(xp, g_pre, w_r, 1024)
    cmp_p = zp[:, COL_CMP:COL_CMP + 2 * KV_WIDTH]
    slc_p = zp[:, COL_SLC:COL_SLC + 2 * KV_WIDTH]
    win_p = zp[:, COL_WIN:COL_WIN + 2 * KV_WIDTH]
    pages_p = cmp_p.reshape(bsz * seq // PAGE_SIZE, PAGE_SIZE, ROW_WIDTH)
    table_p = jnp.arange(bsz * seq // PAGE_SIZE, dtype=jnp.int32).reshape(bsz, seq // PAGE_SIZE)
    comp_p = _compress(pages_p, table_p, *cw, feature_major=False)
    q_t = zp[:, COL_QA:COL_QA + NSA_WIDTH].reshape(bsz, nq, Q_BLOCK, NSA_KV_HEADS, NSA_GROUP, HEAD_DIM)
    q_t = q_t.transpose(0, 3, 1, 5, 4, 2).reshape(bsz, NSA_KV_HEADS, nq, HEAD_DIM, NSA_GROUP * Q_BLOCK)
    k_slc, vt_slc = _attention_keys(slc_p, bsz, seq, augment=True)
    k_win, vt_win = _attention_keys(win_p, bsz, seq, augment=False)
    o_a = _nsa_prompt_t(q_t, zp, comp_p, k_slc, vt_slc, k_win, vt_win, m_mat.T)
    yb_p, hg_p = _hgrn_prompt(zp, hg_lower, hg_norm, bsz, seq)
    h_p = _finish(xp, o_a, zp, yb_p, p_prompt[0].reshape(bsz * seq, PLE_DIM),
                  w_out_b, g_post, ple_proj_b, ple_gate_b, 512)

    wl_p = min(WINDOW, seq)
    out_prompt = (h_p.reshape(bsz, seq, D_MODEL),
                  cmp_p.reshape((1, bsz, seq) + kv_tail),
                  slc_p.reshape((1, bsz, seq) + kv_tail),
                  win_p.reshape((bsz, seq) + kv_tail)[:, seq - wl_p:][None],
                  hg_p[None])

    xs = x_sample.reshape(nb, D_MODEL)
    zs = _proj(xs, g_pre, w_r, nb)
    cmp_t = cache_cmp_kv.transpose(0, 1, 3, 4, 5, 2).reshape(n_pool, ROW_WIDTH, PAGE_SIZE)
    comp_s = _compress(cmp_t, page_table, *cw, feature_major=True)
    q8 = zs[:, COL_QA:COL_QA + NSA_WIDTH].astype(bf16).reshape(nb, NSA_HEADS, HEAD_DIM)
    oc_s, ps_s = _nsa_sample_cmp(q8, comp_s, m_mat, past)
    ps_rows = ps_s[:, :NSA_KV_HEADS, :].reshape(nb * NSA_KV_HEADS, N_SELBLK)
    idx = _nsa_sample_topk(ps_rows, past // SEL_BLOCK - 1)
    gates_s = zs[:, COL_GATE:COL_GATE + LANES].reshape(nb, 1, LANES)
    slc_t =cache_slc_kv.transpose(0, 1, 3, 4, 5, 2).reshape(n_pool, N_COMBO, HEAD_DIM, PAGE_SIZE)
    wl_s = cache_win_kv.shape[2]
    win_t = cache_win_kv.transpose(0, 1, 3, 4, 5, 2).reshape(nb, N_COMBO, HEAD_DIM, wl_s)
    oa_s, win_s = _nsa_sample_sel(page_table, idx[:, :N_SEL].reshape(-1), slc_t, q8, oc_s, gates_s,
                                  zs[:, COL_SLC:COL_SLC + 2 * KV_WIDTH].reshape(nb, 1, 2 * KV_WIDTH),
                                  zs[:, COL_WIN:COL_WIN + 2 * KV_WIDTH].reshape(nb, 1, 2 * KV_WIDTH),
                                  win_t, past)
    win_s = win_s.reshape((1, nb) + kv_tail + (wl_s,)).transpose(0, 1, 5, 2, 3, 4)
    yb_s, hg_s = _hgrn_sample(zs, state_hgrn, hg_lower, hg_norm)
    h_s = _finish(xs, oa_s.reshape(nb, NSA_WIDTH), zs, yb_s, p_sample[0].reshape(nb, PLE_DIM),
                  w_out_b, g_post, ple_proj_b, ple_gate_b, nb)

    out_sample = (h_s.reshape(nb, 1, D_MODEL),
                  zs[:, COL_CMP:COL_CMP + 2 * KV_WIDTH].reshape((1, nb, 1) + kv_tail),
                  zs[:, COL_SLC:COL_SLC + 2 * KV_WIDTH].reshape((1, nb, 1) + kv_tail),
                  win_s,
                  hg_s)
    return (out_prompt[0], out_sample[0]) + out_prompt[1:] + out_sample[1:]
```

```python
import functools

import numpy as np
import jax
import jax.numpy as jnp
from jax import lax
from jax.experimental import pallas as pl
from jax.experimental.pallas import tpu as pltpu

D_MODEL = 1024
PAGE_SIZE = 128
NSA_HEADS = 8
NSA_KV_HEADS = 2
NSA_GROUP = NSA_HEADS // NSA_KV_HEADS
HEAD_DIM = 64
NSA_WIDTH = NSA_HEADS * HEAD_DIM
KV_WIDTH = NSA_KV_HEADS * HEAD_DIM
CMP_BLOCK = 32
CMP_STRIDE = 16
CMP_RATIO = CMP_BLOCK // CMP_STRIDE
CMP_HIDDEN = 128
SEL_BLOCK = 64
SEL_RATIO = SEL_BLOCK // CMP_STRIDE
N_SEL = 16
WINDOW = 512
Q_BLOCK = 128
HG_HEADS = 4
HG_DIM = 128
HG_WIDTH = HG_HEADS * HG_DIM
PLE_DIM = 256
EPS = 1e-6
NEG = -1e30
FORCE_BONUS = 1e6
REMOVED = -3e38

COL_QA, COL_ZA, COL_QB, COL_FB, COL_IB, COL_ZB = 0, 512, 1024, 1536, 2048, 2560
COL_CMP, COL_SLC, COL_WIN, COL_GATE = 3072, 3328, 3584, 3840
PROJ_WIDTH = 4096
_ORIG = dict(q_a=0, k_cmp=512, gate_a=1280, z_a=1304, q_b=1816, f_b=2328, i_b=2840, z_b=3352, end=3864)

LANES = 128
VMEM_LIMIT = 56 * 1024 * 1024

SEL_CHUNK = 512
WIN_KEYS = WINDOW + Q_BLOCK
N_CMP_PAD = 512
N_SELBLK = 128

f32 = jnp.float32
bf16 = jnp.bfloat16


def _cparams(sem):
    return pltpu.CompilerParams(dimension_semantics=sem, vmem_limit_bytes=VMEM_LIMIT)


def _dot(a, b):
    return jnp.dot(a, b, preferred_element_type=f32)


def _dot_nt(a, b):
    return lax.dot_general(a, b, (((1,), (1,)), ((), ())), preferred_element_type=f32)


def _dot_tn(a, b):
    return lax.dot_general(a, b, (((0,), (0,)), ((), ())), preferred_element_type=f32)


def _sigmoid(x):
    return 1.0 / (1.0 + jnp.exp(-x))


def _silu(x):
    return x * _sigmoid(x)


def _proj_kernel(x_ref, g_ref, w_ref, z_ref, xn_ref):
    @pl.when(pl.program_id(1) == 0)
    def _():
        x = x_ref[...]
        y = x * lax.rsqrt(jnp.mean(x * x, axis=-1, keepdims=True) + EPS)
        xn_ref[...] = (y * g_ref[...]).astype(bf16)

    z_ref[...] = _dot(xn_ref[...], w_ref[...])


def _proj(x, g, w, tm):
    n = x.shape[0]
    tn = 1024
    return pl.pallas_call(
        _proj_kernel,
        grid=(n // tm, PROJ_WIDTH // tn),
        in_specs=[pl.BlockSpec((tm, D_MODEL), lambda i, j: (i, 0)),
                  pl.BlockSpec((1, D_MODEL), lambda i, j: (0, 0)),
                  pl.BlockSpec((D_MODEL, tn), lambda i, j: (0, j))],
        out_specs=pl.BlockSpec((tm, tn), lambda i, j: (i, j)),
        out_shape=jax.ShapeDtypeStruct((n, PROJ_WIDTH), f32),
        scratch_shapes=[pltpu.VMEM((tm, D_MODEL), bf16)],
        compiler_params=_cparams(("parallel", "arbitrary")),
        name="proj",
    )(x, g, w)


def _finish_kernel(x_ref, oa_ref, za_ref, yb_ref, p_ref, wo_ref, gp_ref, pp_ref, pg_ref, h_ref):
    ya = (oa_ref[...] * _silu(za_ref[...])).astype(bf16)
    yb = yb_ref[...].astype(bf16)
    mix = _dot(ya, wo_ref[0:NSA_WIDTH, :]) + _dot(yb, wo_ref[NSA_WIDTH:, :])
    nrm = mix * lax.rsqrt(jnp.mean(mix * mix, axis=-1, keepdims=True) + EPS) * gp_ref[...]
    h = x_ref[...] + nrm
    gate = _sigmoid(_dot(h.astype(bf16), pg_ref[...]))
    h_ref[...] = h + _dot(p_ref[...].astype(bf16), pp_ref[...]) * gate


def _finish(x, oa, z, yb, p, w_out, g_post, ple_proj, ple_gate, tm):
    n = x.shape[0]
    za_blk = COL_ZA // NSA_WIDTH
    row = lambda i: (i, 0)
    const = lambda i: (0, 0)
    return pl.pallas_call(
        _finish_kernel,
        grid=(n // tm,),
        in_specs=[pl.BlockSpec((tm, D_MODEL), row),
                  pl.BlockSpec((tm, NSA_WIDTH), row),
                  pl.BlockSpec((tm, NSA_WIDTH), lambda i: (i, za_blk)),
                  pl.BlockSpec((tm, HG_WIDTH), row),
                  pl.BlockSpec((tm, PLE_DIM), row),
                  pl.BlockSpec((D_MODEL, D_MODEL), const),
                  pl.BlockSpec((1, D_MODEL), const),
                  pl.BlockSpec((PLE_DIM, D_MODEL), const),
                  pl.BlockSpec((D_MODEL, D_MODEL), const)],
        out_specs=pl.BlockSpec((tm, D_MODEL), row),
        out_shape=jax.ShapeDtypeStruct((n, D_MODEL), f32),
        compiler_params=_cparams(("parallel",)),
        name="finish",
    )(x, oa, z, yb, p, w_out, g_post, ple_proj, ple_gate)


SEGS_PER_PAGE = PAGE_SIZE // CMP_STRIDE
ROW_WIDTH = 2 * KV_WIDTH
SEG_WIDTH = CMP_STRIDE * ROW_WIDTH
N_COMBO = 2 * NSA_KV_HEADS


def _compress_kernel(pt_ref, pages_ref, wa_ref, wb_ref, pea_ref, peb_ref, b1_ref, w2_ref, out_ref,
                     stage_ref, rows_ref, sem_ref, *, n_pages, feature_major):
    b = pl.program_id(0)
    nb = pl.num_programs(0)
    slot = lax.rem(b, 2)

    def page_copy(seq, s, j):
        return pltpu.make_async_copy(pages_ref.at[pt_ref[seq, j]], stage_ref.at[s, j], sem_ref.at[s])

    def start_all(seq, s):
        def body(j, c):
            page_copy(seq, s, j).start()
            return c
        lax.fori_loop(0, n_pages, body, 0)

    @pl.when(b == 0)
    def _():
        start_all(0, 0)

    @pl.when(b + 1 < nb)
    def _():
        start_all(b + 1, 1 - slot)

    def wait_body(j, c):
        page_copy(b, slot, j).wait()
        return c
    lax.fori_loop(0, n_pages, wait_body, 0)

    lane_halves = ROW_WIDTH // LANES

    def to_rows(j, c):
        tile = stage_ref[slot, j]
        rows = tile.T if feature_major else tile
        r0 = pl.multiple_of(j * PAGE_SIZE, PAGE_SIZE)
        for h in range(lane_halves):
            rows_ref[h, pl.ds(r0, PAGE_SIZE), :] = rows[:, h * LANES:(h + 1) * LANES]
        return c
    lax.fori_loop(0, n_pages, to_rows, 0, unroll=4)

    n_seg = n_pages * SEGS_PER_PAGE
    for kind in range(lane_halves):
        cols = slice(kind * LANES, (kind + 1) * LANES)
        seg_a, seg_b = [], []
        for j in range(CMP_STRIDE):
            r = rows_ref[kind, pl.ds(j, n_seg, stride=CMP_STRIDE), :]
            seg_a.append((r + pea_ref[j][:, cols]).astype(bf16))
            seg_b.append((r + peb_ref[j][:, cols]).astype(bf16))
        pa = _dot(jnp.concatenate(seg_a, axis=1), wa_ref[kind])
        pb = _dot(jnp.concatenate(seg_b, axis=1), wb_ref[kind])
        pb = pltpu.roll(pb, n_seg - 1, axis=0)
        hid = _silu(pa + pb + b1_ref[kind])
        for g in range(NSA_KV_HEADS):
            hg = hid[:, g * CMP_HIDDEN:(g + 1) * CMP_HIDDEN].astype(bf16)
            out_ref[0, kind * NSA_KV_HEADS + g] = _dot(hg, w2_ref[kind])


def _compress(pages, page_table, wa, wb, pea, peb, b1, w2, feature_major):
    nseq, n_pages = page_table.shape
    n_seg = n_pages * SEGS_PER_PAGE
    const2 = lambda b, pt: (0, 0)
    const3 = lambda b, pt: (0, 0, 0)
    once = pl.Buffered(1)
    grid_spec = pltpu.PrefetchScalarGridSpec(
        num_scalar_prefetch=1,
        grid=(nseq,),
        in_specs=[pl.BlockSpec(memory_space=pl.ANY),
                  pl.BlockSpec(wa.shape, const3, pipeline_mode=once),
                  pl.BlockSpec(wb.shape, const3, pipeline_mode=once),
                  pl.BlockSpec((CMP_STRIDE, 1, ROW_WIDTH), const3),
                  pl.BlockSpec((CMP_STRIDE, 1, ROW_WIDTH), const3),
                  pl.BlockSpec(b1.shape, const3),
                  pl.BlockSpec((2, CMP_HIDDEN, HEAD_DIM), const3)],
        out_specs=pl.BlockSpec((1, N_COMBO, n_seg, HEAD_DIM), lambda b, pt: (b, 0, 0, 0)),
        scratch_shapes=[pltpu.VMEM((2, n_pages) + pages.shape[1:], f32),
                        pltpu.VMEM((ROW_WIDTH // LANES, n_pages * PAGE_SIZE, LANES), f32),
                        pltpu.SemaphoreType.DMA((2,))],
    )
    return pl.pallas_call(
        functools.partial(_compress_kernel, n_pages=n_pages, feature_major=feature_major),
        grid_spec=grid_spec,
        out_shape=jax.ShapeDtypeStruct((nseq, N_COMBO, n_seg, HEAD_DIM), f32),
        compiler_params=_cparams(("arbitrary",)),
        name="compress",
    )(page_table, pages, wa, wb, pea, peb, b1, w2)


def _compress_weights(cmp_pe, cmp_w1, cmp_b1, cmp_w2):
    eye = jnp.eye(NSA_KV_HEADS, dtype=f32)

    def big(w_half):
        t = jnp.einsum('kjdh,gb->kjbdgh', w_half, eye)
        return t.reshape(2, CMP_STRIDE * KV_WIDTH, NSA_KV_HEADS * CMP_HIDDEN)

    w1 = cmp_w1.reshape(2, 2, CMP_STRIDE, HEAD_DIM, CMP_HIDDEN)
    wa = big(w1[:, 0]).astype(bf16)
    wb = big(w1[:, 1]).astype(bf16)
    pe = cmp_pe.reshape(2, 2, CMP_STRIDE, HEAD_DIM)

    def pe_row(p):
        t = jnp.broadcast_to(p.transpose(1, 0, 2)[:, :, None, :],
                             (CMP_STRIDE, 2, NSA_KV_HEADS, HEAD_DIM))
        return t.reshape(CMP_STRIDE, 1, ROW_WIDTH)

    pea, peb = pe_row(pe[:, 0]), pe_row(pe[:, 1])
    b1 = jnp.broadcast_to(cmp_b1[:, None, :], (2, NSA_KV_HEADS, CMP_HIDDEN)).reshape(2, 1, NSA_KV_HEADS * CMP_HIDDEN)
    return wa, wb, pea, peb, b1, cmp_w2.astype(bf16)


def _sel_map_matrix():
    wts = np.convolve(np.ones(SEL_RATIO), np.ones(CMP_RATIO))
    m = np.zeros((N_CMP_PAD, N_SELBLK), np.float32)
    for j in range(N_SELBLK):
        for k, wk in enumerate(wts):
            n = SEL_RATIO * j + k
            if n < N_CMP_PAD - 1:
                m[n, j] = wk
    return jnp.asarray(m, dtype=bf16)


def _expand_matrix(n_chunks):
    key = np.arange(n_chunks * SEL_CHUNK) // SEL_BLOCK
    e = (np.arange(N_SELBLK)[:, None] == key[None, :]).astype(np.float32)
    return jnp.asarray(e.reshape(N_SELBLK, n_chunks, SEL_CHUNK).transpose(1, 0, 2), dtype=bf16)


def _importance_to_blocks(imp, m):
    hi = imp.astype(bf16)
    lo = (imp - hi.astype(f32)).astype(bf16)
    return _dot(hi, m) + _dot(lo, m)


def _nsa_prompt_kernel(q_ref, gate_ref, kc_ref, vc_ref, kvs_ref, kvw_ref, m_ref, e_ref, o_ref,
                       acc_ref, mx_ref, used_ref):
    g = pl.program_id(1)
    qi = pl.program_id(2)
    q0 = qi * Q_BLOCK
    rows = NSA_GROUP * Q_BLOCK

    q64 = q_ref[0, 0, 0] * jnp.asarray(HEAD_DIM ** -0.5, bf16)
    q128 = jnp.concatenate([q64, jnp.zeros_like(q64)], axis=1)
    slope = [jnp.where(g == 0, 2.0 ** -(r + 1), 2.0 ** -(NSA_GROUP + r + 1)).astype(f32)
             for r in range(NSA_GROUP)]

    qrow = lax.broadcasted_iota(jnp.int32, (Q_BLOCK, 1), 0)

    def softmax_rows(s, dist, mask):
        es, ls = [], []
        for r in range(NSA_GROUP):
            sr = jnp.where(mask, s[r * Q_BLOCK:(r + 1) * Q_BLOCK] - slope[r] * dist, NEG)
            m = jnp.max(sr, axis=1, keepdims=True)
            e = jnp.where(mask, jnp.exp(sr - m), 0.0)
            es.append(e)
            ls.append(1.0 / jnp.maximum(jnp.sum(e, axis=1, keepdims=True), 1e-30))
        return es, ls

    kc = kc_ref[0, 0].astype(bf16)
    vc = vc_ref[0, 0].astype(bf16)
    ncol = lax.broadcasted_iota(jnp.int32, (1, N_CMP_PAD), 1)
    d_c = (q0 + qrow) - (ncol * CMP_STRIDE + (CMP_BLOCK - 1))
    es, ls = softmax_rows(_dot_nt(q64, kc), d_c.astype(f32), d_c >= 0)
    ps_c = [e * l for e, l in zip(es, ls)]
    o_c = _dot(jnp.concatenate(ps_c, axis=0).astype(bf16), vc)
    imp = ps_c[0] + ps_c[1] + ps_c[2] + ps_c[3]
    p_sel = _importance_to_blocks(imp, m_ref[...])

    ps_t = p_sel.T
    blk = lax.broadcasted_iota(jnp.int32, (N_SELBLK, Q_BLOCK), 0)
    tq = q0 + lax.broadcasted_iota(jnp.int32, (N_SELBLK, Q_BLOCK), 1)
    jt = jnp.right_shift(tq, SEL_BLOCK.bit_length() - 1)
    valid = blk <= jt
    forced = (blk == 0) | (blk == jt) | (blk == jt - 1)
    score = jnp.where(valid, ps_t + jnp.where(forced, FORCE_BONUS, 0.0), NEG)
    sel_t = jnp.zeros((N_SELBLK, Q_BLOCK), f32)
    for _ in range(N_SEL):
        best = jnp.max(score, axis=0, keepdims=True)
        idx = jnp.min(jnp.where(score == best, blk, N_SELBLK), axis=0, keepdims=True)
        hit = blk == idx
        sel_t = jnp.where(hit & (best > NEG * 0.5), 1.0, sel_t)
        score = jnp.where(hit, REMOVED, score)
    sel = sel_t.T.astype(bf16)

    acc_ref[...] = jnp.zeros_like(acc_ref)
    mx_ref[...] = jnp.full_like(mx_ref, NEG)
    kcol = lax.broadcasted_iota(jnp.int32, (1, SEL_CHUNK), 1)
    d_base = (qrow - kcol).astype(f32)

    blk_per_chunk = SEL_CHUNK // SEL_BLOCK
    for c in range(e_ref.shape[0]):
        picked = jnp.max(sel_t[c * blk_per_chunk:(c + 1) * blk_per_chunk, :])
        used_ref[c] = (picked > 0.5).astype(jnp.int32)

    def sel_chunk(c, carry):
        @pl.when(used_ref[c] > 0)
        def _():
            sel_chunk_body(c)
        return carry

    def sel_chunk_body(c):
        k0 = pl.multiple_of(c * SEL_CHUNK, SEL_CHUNK)
        kv = kvs_ref[0, 0, pl.ds(k0, SEL_CHUNK), :]
        s = _dot_nt(q128, kv[:, 0:LANES])
        dist = d_base + (q0 - k0).astype(f32)
        allowed = (_dot(sel, e_ref[c]) > 0.5) & (dist >= 0.0)
        ps = []
        for r in range(NSA_GROUP):
            rs = slice(r * Q_BLOCK, (r + 1) * Q_BLOCK)
            sr = jnp.where(allowed, s[rs] - slope[r] * dist, NEG)
            m_old = mx_ref[rs]
            m_new = jnp.maximum(m_old, jnp.max(sr, axis=1, keepdims=True))
            acc_ref[rs] = acc_ref[rs] * jnp.exp(m_old - m_new)
            mx_ref[rs] = m_new
            ps.append(jnp.exp(sr - m_new).astype(bf16))
        acc_ref[...] += _dot(jnp.concatenate(ps, axis=0), kv[:, LANES:])

    n_chunks = (q0 + Q_BLOCK + SEL_CHUNK - 1) // SEL_CHUNK
    lax.fori_loop(0, n_chunks, sel_chunk, 0)
    acc = acc_ref[...]
    o_s = acc[:, 0:HEAD_DIM] * (1.0 / jnp.maximum(acc[:, HEAD_DIM:HEAD_DIM + 1], 1e-30))

    w0 = pl.multiple_of(jnp.maximum(q0 - WINDOW, 0), Q_BLOCK)
    kvw = kvw_ref[0, 0, pl.ds(w0, WIN_KEYS), :]
    wcol = lax.broadcasted_iota(jnp.int32, (1, WIN_KEYS), 1)
    d_w = (q0 - w0) + qrow - wcol
    es, _ = softmax_rows(_dot_nt(q128, kvw[:, 0:LANES]), d_w.astype(f32), (d_w >= 0) & (d_w <= WINDOW))
    acc_w = _dot(jnp.concatenate(es, axis=0).astype(bf16), kvw[:, LANES:])
    o_w = acc_w[:, 0:HEAD_DIM] * (1.0 / jnp.maximum(acc_w[:, HEAD_DIM:HEAD_DIM + 1], 1e-30))

    gate = _sigmoid(gate_ref[...])

    def gate_col(r, j):
        lo = gate[:, r * 3 + j:r * 3 + j + 1]
        hi = gate[:, (NSA_GROUP + r) * 3 + j:(NSA_GROUP + r) * 3 + j + 1]
        return jnp.where(g == 0, lo, hi)

    for r in range(NSA_GROUP):
        rs = slice(r * Q_BLOCK, (r + 1) * Q_BLOCK)
        o_ref[0, 0, 0, rs, :] = gate_col(r, 0) * o_c[rs] + gate_col(r, 1) * o_s[rs] + gate_col(r, 2) * o_w[rs]


def _nsa_prompt(q_st, z, comp, kv_slc, kv_win, m_mat, e_mat):
    bsz, nq = q_st.shape[0], q_st.shape[1]
    seq = kv_slc.shape[2]
    rows = NSA_GROUP * Q_BLOCK
    gate_blk = COL_GATE // LANES
    return pl.pallas_call(
        _nsa_prompt_kernel,
        grid=(bsz, NSA_KV_HEADS, nq),
        in_specs=[pl.BlockSpec((1, 1, 1, rows, HEAD_DIM), lambda b, g, i: (b, i, g, 0, 0)),
                  pl.BlockSpec((Q_BLOCK, LANES), lambda b, g, i: (b * nq + i, gate_blk)),
                  pl.BlockSpec((1, 1, N_CMP_PAD, HEAD_DIM), lambda b, g, i: (b, g, 0, 0)),
                  pl.BlockSpec((1, 1, N_CMP_PAD, HEAD_DIM), lambda b, g, i: (b, NSA_KV_HEADS + g, 0, 0)),
                  pl.BlockSpec((1, 1, seq, 2 * LANES), lambda b, g, i: (b, g, 0, 0)),
                  pl.BlockSpec((1, 1, seq, 2 * LANES), lambda b, g, i: (b, g, 0, 0)),
                  pl.BlockSpec((N_CMP_PAD, N_SELBLK), lambda b, g, i: (0, 0)),
                  pl.BlockSpec(e_mat.shape, lambda b, g, i: (0, 0, 0))],
        out_specs=pl.BlockSpec((1, 1, 1, rows, HEAD_DIM), lambda b, g, i: (b, i, g, 0, 0)),
        out_shape=jax.ShapeDtypeStruct((bsz, nq, NSA_KV_HEADS, rows, HEAD_DIM), f32),
        scratch_shapes=[pltpu.VMEM((rows, LANES), f32), pltpu.VMEM((rows, 1), f32),
                        pltpu.SMEM((e_mat.shape[0],), jnp.int32)],
        compiler_params=_cparams(("parallel", "parallel", "arbitrary")),
        name="nsa_prompt",
    )(q_st, z, comp, comp, kv_slc, kv_win, m_mat, e_mat)


def _attention_kv(piece, bsz, seq):
    t = piece.reshape(bsz, seq, 2, NSA_KV_HEADS, HEAD_DIM).transpose(2, 0, 3, 1, 4)
    zeros = jnp.zeros_like(t[0])
    ones = jnp.ones(t[0].shape[:-1] + (1,), f32)
    out = jnp.concatenate([t[0], zeros, t[1], ones, zeros[..., :HEAD_DIM - 1]], axis=-1)
    return out.astype(bf16)


LOG2E = 1.4426950408889634
MASK_BIG = 1e30
AUG_ROWS = 16
VT_ROWS = HEAD_DIM + 16
KEY_LOW = 256
N_FORCED = 3


def _nsa_prompt_t_kernel(qt_ref, gate_ref, kc_ref, vc_ref, ks_ref, vts_ref, kw_ref, vtw_ref, mt_ref, o_ref,
                         qa_ref, acc_ref, mg_ref, bias_ref, used_ref):
    g = pl.program_id(1)
    qi = pl.program_id(2)
    q0 = qi * Q_BLOCK
    cols = NSA_GROUP * Q_BLOCK

    lane = lax.broadcasted_iota(jnp.int32, (1, cols), 1)
    head = jnp.right_shift(lane, Q_BLOCK.bit_length() - 1)
    tq = q0 + jnp.bitwise_and(lane, Q_BLOCK - 1)
    c = jnp.exp2(-(head.astype(f32) + 1.0 + NSA_GROUP * g.astype(f32))) * LOG2E

    qt = (qt_ref[0, 0, 0] * (HEAD_DIM ** -0.5 * LOG2E)).astype(bf16)

    def softmax_cols(s, mask):
        s = jnp.where(mask, s, NEG)
        m = jnp.max(s, axis=0, keepdims=True)
        e = jnp.where(mask, jnp.exp2(s - m), 0.0)
        return e, 1.0 / jnp.maximum(jnp.sum(e, axis=0, keepdims=True), 1e-30)

    kc = kc_ref[0, 0].astype(bf16)
    vc = vc_ref[0, 0].astype(bf16)
    nrow = lax.broadcasted_iota(jnp.int32, (N_CMP_PAD, 1), 0)
    d_c = tq - (nrow * CMP_STRIDE + (CMP_BLOCK - 1))
    e, rl = softmax_cols(_dot(kc, qt) - c * d_c.astype(f32), d_c >= 0)
    p_c = e * rl
    oc_t = _dot_tn(vc, p_c.astype(bf16))
    imp_t = (p_c[:, 0:Q_BLOCK] + p_c[:, Q_BLOCK:2 * Q_BLOCK]
             + p_c[:, 2 * Q_BLOCK:3 * Q_BLOCK] + p_c[:, 3 * Q_BLOCK:4 * Q_BLOCK])
    hi = imp_t.astype(bf16)
    lo = (imp_t - hi.astype(f32)).astype(bf16)
    ps_t = _dot(mt_ref[...], hi) + _dot(mt_ref[...], lo)

    blk = lax.broadcasted_iota(jnp.int32, (N_SELBLK, Q_BLOCK), 0)
    jt = jnp.right_shift(q0 + lax.broadcasted_iota(jnp.int32, (N_SELBLK, Q_BLOCK), 1), SEL_BLOCK.bit_length() - 1)
    valid = blk <= jt
    forced = (blk == 0) | (blk == jt) | (blk == jt - 1)
    sel_t = jnp.where(valid & forced, 1.0, 0.0)
    score = jnp.where(valid & jnp.logical_not(forced), ps_t, NEG)
    for _ in range(N_SEL - N_FORCED):
        best = jnp.max(score, axis=0, keepdims=True)
        idx = jnp.min(jnp.where(score == best, blk, N_SELBLK), axis=0, keepdims=True)
        hit = blk == idx
        sel_t = jnp.where(hit & (best > NEG * 0.5), 1.0, sel_t)
        score = jnp.where(hit, REMOVED, score)
    bias_ref[...] = (sel_t - 1.0) * MASK_BIG

    blk_per_chunk = SEL_CHUNK // SEL_BLOCK
    n_chunk_max = vts_ref.shape[2]
    for ck in range(n_chunk_max):
        picked = jnp.max(sel_t[ck * blk_per_chunk:(ck + 1) * blk_per_chunk, :])
        used_ref[ck] = (picked > 0.5).astype(jnp.int32)

    c_hi = c.astype(bf16).astype(f32)
    c_mid = (c - c_hi).astype(bf16).astype(f32)
    c_lo = (c - c_hi - c_mid).astype(bf16).astype(f32)
    arow = lax.broadcasted_iota(jnp.int32, (AUG_ROWS // 2, cols), 0)
    alibi_rows = jnp.where(arow >= 6, 0.0,
                           jnp.where((arow == 0) | (arow == 3), c_hi, jnp.where((arow == 1) | (arow == 4), c_mid, c_lo)))
    qa_ref[0:HEAD_DIM, :] = qt
    qa_ref[HEAD_DIM + AUG_ROWS:, :] = jnp.zeros((qa_ref.shape[0] - HEAD_DIM - AUG_ROWS, cols), bf16)
    acc_ref[...] = jnp.zeros_like(acc_ref)
    mg_ref[...] = jnp.full_like(mg_ref, NEG)
    krow = lax.broadcasted_iota(jnp.int32, (SEL_CHUNK, 1), 0)

    def sel_chunk(ck, causal):
        k0 = pl.multiple_of(ck * SEL_CHUNK, SEL_CHUNK)
        b8 = bias_ref[pl.ds(pl.multiple_of(ck * blk_per_chunk, blk_per_chunk), blk_per_chunk), :]
        qa_ref[HEAD_DIM:HEAD_DIM + AUG_ROWS, :] = jnp.concatenate(
            [jnp.concatenate([b8] * NSA_GROUP, axis=1), alibi_rows], axis=0).astype(bf16)
        s = _dot(ks_ref[0, 0, pl.ds(k0, SEL_CHUNK), :], qa_ref[...])
        if causal:
            s = jnp.where(k0 + krow <= tq, s, NEG)
        shift = c * k0.astype(f32)
        m_old = mg_ref[...] - shift
        m_new = jnp.maximum(m_old, jnp.max(s, axis=0, keepdims=True))
        p = jnp.exp2(s - m_new).astype(bf16)
        acc_ref[...] = acc_ref[...] * jnp.exp2(m_old - m_new) + _dot(vts_ref[0, 0, ck], p)
        mg_ref[...] = m_new + shift

    def past_chunk(ck, carry):
        @pl.when(used_ref[ck] > 0)
        def _():
            sel_chunk(ck, causal=False)
        return carry

    last = q0 // SEL_CHUNK
    sel_chunk(last, causal=True)

    w0 = pl.multiple_of(jnp.maximum(q0 - WINDOW, 0), Q_BLOCK)
    kw = kw_ref[0, 0, pl.ds(w0, WIN_KEYS), :]
    q_pad = jnp.concatenate([qt, jnp.zeros((LANES - HEAD_DIM, cols), bf16)], axis=0)
    wrow = lax.broadcasted_iota(jnp.int32, (WIN_KEYS, 1), 0)
    d_w = tq - (w0 + wrow)
    e, _ = softmax_cols(_dot(kw, q_pad) - c * d_w.astype(f32), (d_w >= 0) & (d_w <= WINDOW))
    e = e.astype(bf16)
    t0 = w0 // Q_BLOCK
    acc_w = jnp.zeros((VT_ROWS, cols), f32)
    for t in range(WIN_KEYS // Q_BLOCK):
        acc_w = acc_w + _dot(vtw_ref[0, 0, t0 + t], e[t * Q_BLOCK:(t + 1) * Q_BLOCK])
    ow_t = acc_w[0:HEAD_DIM] * (1.0 / jnp.maximum(acc_w[HEAD_DIM:HEAD_DIM + 1], 1e-30))

    lax.fori_loop(0, last, past_chunk, 0)
    acc = acc_ref[...]
    os_t = acc[0:HEAD_DIM] * (1.0 / jnp.maximum(acc[HEAD_DIM:HEAD_DIM + 1], 1e-30))

    gate_t = _sigmoid(gate_ref[...]).T

    def gate_row(r, j):
        lo_row = gate_t[r * 3 + j:r * 3 + j + 1, :]
        hi_row = gate_t[(NSA_GROUP + r) * 3 + j:(NSA_GROUP + r) * 3 + j + 1, :]
        return jnp.where(g == 0, lo_row, hi_row)

    outs = []
    for r in range(NSA_GROUP):
        cs = slice(r * Q_BLOCK, (r + 1) * Q_BLOCK)
        outs.append(gate_row(r, 0) * oc_t[:, cs] + gate_row(r, 1) * os_t[:, cs] + gate_row(r, 2) * ow_t[:, cs])
    for pair in range(NSA_GROUP // 2):
        both = jnp.concatenate([outs[2 * pair], outs[2 * pair + 1]], axis=0)
        o_ref[:, pair * LANES:(pair + 1) * LANES] = both.T


def _nsa_prompt_t(q_t, z, comp, k_slc, vt_slc, k_win, vt_win, m_t):
    bsz, _, nq = q_t.shape[:3]
    seq = k_slc.shape[2]
    cols = NSA_GROUP * Q_BLOCK
    gate_blk = COL_GATE // LANES
    per_bg = lambda b, g, i: (b, g, 0, 0)
    per_bg5 = lambda b, g, i: (b, g, 0, 0, 0)
    return pl.pallas_call(
        _nsa_prompt_t_kernel,
        grid=(bsz, NSA_KV_HEADS, nq),
        in_specs=[pl.BlockSpec((1, 1, 1, HEAD_DIM, cols), lambda b, g, i: (b, g, i, 0, 0)),
                  pl.BlockSpec((Q_BLOCK, LANES), lambda b, g, i: (b * nq + i, gate_blk)),
                  pl.BlockSpec((1, 1, N_CMP_PAD, HEAD_DIM), per_bg),
                  pl.BlockSpec((1, 1, N_CMP_PAD, HEAD_DIM), lambda b, g, i: (b, NSA_KV_HEADS + g, 0, 0)),
                  pl.BlockSpec((1, 1, seq, LANES), per_bg),
                  pl.BlockSpec((1, 1) + vt_slc.shape[2:], per_bg5),
                  pl.BlockSpec((1, 1, seq, LANES), per_bg),
                  pl.BlockSpec((1, 1) + vt_win.shape[2:], per_bg5),
                  pl.BlockSpec((N_SELBLK, N_CMP_PAD), lambda b, g, i: (0, 0))],
        out_specs=pl.BlockSpec((Q_BLOCK, NSA_GROUP * HEAD_DIM), lambda b, g, i: (b * nq + i, g)),
        out_shape=jax.ShapeDtypeStruct((bsz * seq, NSA_WIDTH), f32),
        scratch_shapes=[pltpu.VMEM((LANES, cols), bf16),
                        pltpu.VMEM((VT_ROWS, cols), f32),
                        pltpu.VMEM((1, cols), f32),
                        pltpu.VMEM((N_SELBLK, Q_BLOCK), f32),
                        pltpu.SMEM((vt_slc.shape[2],), jnp.int32)],
        compiler_params=_cparams(("parallel", "parallel", "arbitrary")),
        name="nsa_prompt",
    )(q_t, z, comp, comp, k_slc, vt_slc, k_win, vt_win, m_t)


def _nsa_prompt_pair_kernel(qt_ref, gate_ref, comp_ref, caug_ref, ks_ref, vts_ref, kw_ref, vtw_ref, mt_ref, o_ref,
                            qa_ref, acc_ref, mg_ref, bias_ref, used_ref):
    qi = pl.program_id(1)
    q0 = qi * Q_BLOCK
    cols = NSA_GROUP * Q_BLOCK
    heads = range(NSA_KV_HEADS)

    lane = lax.broadcasted_iota(jnp.int32, (1, cols), 1)
    head = jnp.right_shift(lane, Q_BLOCK.bit_length() - 1)
    tq = q0 + jnp.bitwise_and(lane, Q_BLOCK - 1)
    blk = lax.broadcasted_iota(jnp.int32, (N_SELBLK, Q_BLOCK), 0)
    jt = jnp.right_shift(q0 + lax.broadcasted_iota(jnp.int32, (N_SELBLK, Q_BLOCK), 1), SEL_BLOCK.bit_length() - 1)
    valid = blk <= jt
    forced = (blk == 0) | (blk == jt) | (blk == jt - 1)
    c_end = lax.broadcasted_iota(jnp.int32, (N_CMP_PAD, 1), 0) * CMP_STRIDE + (CMP_BLOCK - 1)
    krow = lax.broadcasted_iota(jnp.int32, (SEL_CHUNK, 1), 0)
    arow = lax.broadcasted_iota(jnp.int32, (AUG_ROWS // 2, cols), 0)
    blk_per_chunk = SEL_CHUNK // SEL_BLOCK
    n_chunk_max = vts_ref.shape[2]
    last = q0 // SEL_CHUNK
    w0 = pl.multiple_of(jnp.maximum(q0 - WINDOW, 0), Q_BLOCK)
    wkey = w0 + lax.broadcasted_iota(jnp.int32, (WIN_KEYS, 1), 0)
    in_window = (wkey <= tq) & (wkey >= tq - WINDOW)
    gate_t = _sigmoid(gate_ref[...]).T

    def softmax_cols(s, mask):
        s = jnp.where(mask, s, NEG)
        m = jnp.max(s, axis=0, keepdims=True)
        e = jnp.where(mask, jnp.exp2(s - m), 0.0)
        return e, 1.0 / jnp.maximum(jnp.sum(e, axis=0, keepdims=True), 1e-30)

    slope = [jnp.exp2(-(head.astype(f32) + 1.0 + NSA_GROUP * g)) * LOG2E for g in heads]
    qts = [(qt_ref[0, g, 0] * (HEAD_DIM ** -0.5 * LOG2E)).astype(bf16) for g in heads]

    def sel_chunk(g, ck, causal):
        c = slope[g]
        k0 = pl.multiple_of(ck * SEL_CHUNK, SEL_CHUNK)
        b8 = bias_ref[g, pl.ds(pl.multiple_of(ck * blk_per_chunk, blk_per_chunk), blk_per_chunk), :]
        qa_ref[g, HEAD_DIM:HEAD_DIM + AUG_ROWS, :] = jnp.concatenate(
            [jnp.concatenate([b8] * NSA_GROUP, axis=1), alibi_rows[g]], axis=0).astype(bf16)
        s = _dot(ks_ref[0, g, pl.ds(k0, SEL_CHUNK), :], qa_ref[g])
        if causal:
            s = jnp.where(k0 + krow <= tq, s, NEG)
        shift = c * k0.astype(f32)
        m_old = mg_ref[g] - shift
        m_new = jnp.maximum(m_old, jnp.max(s, axis=0, keepdims=True))
        p = jnp.exp2(s - m_new).astype(bf16)
        acc_ref[g] = acc_ref[g] * jnp.exp2(m_old - m_new) + _dot(vts_ref[0, g, ck], p)
        mg_ref[g] = m_new + shift

    oc_t, ow_t = [], []
    alibi_rows = [None] * NSA_KV_HEADS
    for g in heads:
        c, qt = slope[g], qts[g]
        c_hi = c.astype(bf16).astype(f32)
        c_mid = (c - c_hi).astype(bf16).astype(f32)
        c_lo = (c - c_hi - c_mid).astype(bf16).astype(f32)
        c3 = jnp.where((arow == 0) | (arow == 3), c_hi, jnp.where((arow == 1) | (arow == 4), c_mid, c_lo))
        pos_rows = jnp.where(arow >= 6, 0.0, jnp.where(arow >= 3, KEY_LOW * c3, c3))
        pos_rows = jnp.concatenate([pos_rows, jnp.zeros_like(pos_rows)], axis=0).astype(bf16)
        q_pos = jnp.concatenate([qt, pos_rows, jnp.zeros((LANES - HEAD_DIM - AUG_ROWS, cols), bf16)], axis=0)

        kc = jnp.concatenate([comp_ref[0, g].astype(bf16), caug_ref[...]], axis=1)
        vc = comp_ref[0, NSA_KV_HEADS + g].astype(bf16)
        e, rl = softmax_cols(_dot(kc, q_pos), c_end <= tq)
        p_c = e * rl
        oc_t.append(_dot_tn(vc, p_c.astype(bf16)))
        imp_t = (p_c[:, 0:Q_BLOCK] + p_c[:, Q_BLOCK:2 * Q_BLOCK]
                 + p_c[:, 2 * Q_BLOCK:3 * Q_BLOCK] + p_c[:, 3 * Q_BLOCK:4 * Q_BLOCK])
        hi = imp_t.astype(bf16)
        lo = (imp_t - hi.astype(f32)).astype(bf16)
        ps_t = _dot(mt_ref[...], hi) + _dot(mt_ref[...], lo)

        sel_t = jnp.where(valid & forced, 1.0, 0.0)
        score = jnp.where(valid & jnp.logical_not(forced), ps_t, NEG)
        for _ in range(N_SEL - N_FORCED):
            best = jnp.max(score, axis=0, keepdims=True)
            idx = jnp.min(jnp.where(score == best, blk, N_SELBLK), axis=0, keepdims=True)
            hit = blk == idx
            sel_t = jnp.where(hit & (best > NEG * 0.5), 1.0, sel_t)
            score = jnp.where(hit, REMOVED, score)
        bias_ref[g] = (sel_t - 1.0) * MASK_BIG
        for ck in range(n_chunk_max):
            picked = jnp.max(sel_t[ck * blk_per_chunk:(ck + 1) * blk_per_chunk, :])
            used_ref[g, ck] = (picked > 0.5).astype(jnp.int32)

        alibi_rows[g] = jnp.where(arow >= 6, 0.0, c3)
        qa_ref[g, 0:HEAD_DIM, :] = qt
        qa_ref[g, HEAD_DIM + AUG_ROWS:, :] = jnp.zeros((qa_ref.shape[1] - HEAD_DIM - AUG_ROWS, cols), bf16)
        acc_ref[g] = jnp.zeros(acc_ref.shape[1:], f32)
        mg_ref[g] = jnp.full(mg_ref.shape[1:], NEG, f32)
        sel_chunk(g, last, causal=True)

        kw = kw_ref[0, g, pl.ds(w0, WIN_KEYS), :]
        s_w = jnp.where(in_window, _dot(kw, q_pos), NEG)
        e = jnp.exp2(s_w - jnp.max(s_w, axis=0, keepdims=True)).astype(bf16)
        t0 = w0 // Q_BLOCK
        acc_w = jnp.zeros((VT_ROWS, cols), f32)
        for t in range(WIN_KEYS // Q_BLOCK):
            acc_w = acc_w + _dot(vtw_ref[0, g, t0 + t], e[t * Q_BLOCK:(t + 1) * Q_BLOCK])
        ow_t.append(acc_w[0:HEAD_DIM] * (1.0 / jnp.maximum(acc_w[HEAD_DIM:HEAD_DIM + 1], 1e-30)))

    def past_chunk(ck, carry):
        @pl.when((used_ref[0, ck] + used_ref[1, ck]) > 0)
        def _():
            for g in heads:
                sel_chunk(g, ck, causal=False)
        return carry

    lax.fori_loop(0, last, past_chunk, 0)

    for g in heads:
        acc = acc_ref[g]
        os_t = acc[0:HEAD_DIM] * (1.0 / jnp.maximum(acc[HEAD_DIM:HEAD_DIM + 1], 1e-30))
        outs = []
        for r in range(NSA_GROUP):
            cs = slice(r * Q_BLOCK, (r + 1) * Q_BLOCK)
            row = (g * NSA_GROUP + r) * 3
            outs.append(gate_t[row:row + 1, :] * oc_t[g][:, cs] + gate_t[row + 1:row + 2, :] * os_t[:, cs]
                        + gate_t[row + 2:row + 3, :] * ow_t[g][:, cs])
        for pair in range(NSA_GROUP // 2):
            both = jnp.concatenate([outs[2 * pair], outs[2 * pair + 1]], axis=0)
            lane0 = (g * NSA_GROUP // 2 + pair) * LANES
            o_ref[:, lane0:lane0 + LANES] = both.T


def _nsa_prompt_pair(q_t, z, comp, k_slc, vt_slc, k_win, vt_win, m_t):
    bsz, _, nq = q_t.shape[:3]
    seq = k_slc.shape[2]
    cols = NSA_GROUP * Q_BLOCK
    gate_blk = COL_GATE // LANES
    per_b4 = lambda b, i: (b, 0, 0, 0)
    per_b5 = lambda b, i: (b, 0, 0, 0, 0)
    return pl.pallas_call(
        _nsa_prompt_pair_kernel,
        grid=(bsz, nq),
        in_specs=[pl.BlockSpec((1, NSA_KV_HEADS, 1, HEAD_DIM, cols), lambda b, i: (b, 0, i, 0, 0)),
                  pl.BlockSpec((Q_BLOCK, LANES), lambda b, i: (b * nq + i, gate_blk)),
                  pl.BlockSpec((1, N_COMBO, N_CMP_PAD, HEAD_DIM), per_b4),
                  pl.BlockSpec((N_CMP_PAD, HEAD_DIM), lambda b, i: (0, 0)),
                  pl.BlockSpec((1, NSA_KV_HEADS, seq, LANES), per_b4),
                  pl.BlockSpec((1,) + vt_slc.shape[1:], per_b5),
                  pl.BlockSpec((1, NSA_KV_HEADS, seq, LANES), per_b4),
                  pl.BlockSpec((1,) + vt_win.shape[1:], per_b5),
                  pl.BlockSpec((N_SELBLK, N_CMP_PAD), lambda b, i: (0, 0))],
        out_specs=pl.BlockSpec((Q_BLOCK, NSA_WIDTH), lambda b, i: (b * nq + i, 0)),
        out_shape=jax.ShapeDtypeStruct((bsz * seq, NSA_WIDTH), f32),
        scratch_shapes=[pltpu.VMEM((NSA_KV_HEADS, LANES, cols), bf16),
                        pltpu.VMEM((NSA_KV_HEADS, VT_ROWS, cols), f32),
                        pltpu.VMEM((NSA_KV_HEADS, 1, cols), f32),
                        pltpu.VMEM((NSA_KV_HEADS, N_SELBLK, Q_BLOCK), f32),
                        pltpu.SMEM((NSA_KV_HEADS, vt_slc.shape[2]), jnp.int32)],
        compiler_params=_cparams(("parallel", "arbitrary")),
        name="nsa_prompt",
    )(q_t, z, comp, _cmp_position_columns(), k_slc, vt_slc, k_win, vt_win, m_t)


def _position_columns(pos):
    low, high = (pos % KEY_LOW)[:, None], (pos // KEY_LOW)[:, None]
    return np.concatenate([low, low, low, high, high, high], axis=1).astype(np.float32)


def _cmp_position_columns():
    aug = np.zeros((N_CMP_PAD, HEAD_DIM), np.float32)
    aug[:, 0:6] = _position_columns(np.arange(N_CMP_PAD) * CMP_STRIDE + CMP_BLOCK - 1)
    return jnp.asarray(aug, dtype=bf16)


def _attention_keys(piece, bsz, seq, augment):
    tile = SEL_CHUNK if augment else Q_BLOCK
    t = piece.reshape(bsz, seq, 2, NSA_KV_HEADS, HEAD_DIM)
    k = t[:, :, 0].transpose(0, 2, 1, 3)
    pos = np.arange(seq)
    extra = np.zeros((seq, LANES - HEAD_DIM), np.float32)
    if augment:
        extra[np.arange(seq), (pos // SEL_BLOCK) % (SEL_CHUNK // SEL_BLOCK)] = 1.0
        extra[:, 8:11] = (pos % KEY_LOW)[:, None]
        extra[:, 11:14] = (pos % SEL_CHUNK - pos % KEY_LOW)[:, None]
    else:
        extra[:, 0:6] = _position_columns(pos)
    k_aug = jnp.concatenate([k, jnp.broadcast_to(jnp.asarray(extra), k.shape[:2] + extra.shape)], axis=-1)
    v_t = t[:, :, 1].transpose(0, 2, 3, 1)
    ones = jnp.ones(v_t.shape[:2] + (1, seq), f32)
    zeros = jnp.zeros(v_t.shape[:2] + (VT_ROWS - HEAD_DIM - 1, seq), f32)
    v_aug = jnp.concatenate([v_t, ones, zeros], axis=2)
    v_aug = v_aug.reshape(bsz, NSA_KV_HEADS, VT_ROWS, seq // tile, tile).transpose(0, 1, 3, 2, 4)
    return k_aug.astype(bf16), v_aug.astype(bf16)


HG_CHUNK = 128
HG_BAND = 8


def _lower_bound(lo):
    m = jnp.max(lo, axis=0, keepdims=True)
    e = jnp.exp(lo - m)
    return e[0:1] / jnp.sum(e, axis=0, keepdims=True)


def _hgrn_gates(q, fl, lb):
    f = lb + (1.0 - lb) * _sigmoid(fl)
    k = (1.0 - lb) * _sigmoid(-fl)
    return _silu(q), k, jnp.log(f)


def _hgrn_out(o, nw, zb):
    o = o * lax.rsqrt(jnp.mean(o * o, axis=-1, keepdims=True) + EPS) * nw
    return o * _silu(zb)


def _hgrn_prompt_kernel(q_ref, f_ref, i_ref, zb_ref, lo_ref, nw_ref, y_ref, s_ref, st_ref):
    c = pl.program_id(2)
    n = HG_CHUNK

    @pl.when(c == 0)
    def _():
        st_ref[...] = jnp.zeros_like(st_ref)

    lb = _lower_bound(lo_ref[...])
    qq, kk, logf = _hgrn_gates(q_ref[...], f_ref[...], lb)
    v = i_ref[...]
    row = lax.broadcasted_iota(jnp.int32, (n, n), 0)
    col = lax.broadcasted_iota(jnp.int32, (n, n), 1)

    b = logf
    sh = 1
    while sh < n:
        b = b + jnp.where(row >= sh, pltpu.roll(b, sh, axis=0), 0.0)
        sh *= 2

    a = jnp.zeros((n, n), f32)
    sub = jnp.bitwise_and(row, HG_BAND - 1)
    for d in range(HG_BAND):
        if d == 0:
            term = qq * kk
        else:
            ok = sub >= d
            decay = jnp.exp(jnp.where(ok, b - pltpu.roll(b, d, axis=0), 0.0))
            term = jnp.where(ok, qq * pltpu.roll(kk, d, axis=0) * decay, 0.0)
        a = a + jnp.where(col == row - d, jnp.sum(term, axis=1, keepdims=True), 0.0)

    size = 2 * HG_BAND
    while size <= n:
        half = size // 2
        qs, ks = [], []
        zero = jnp.zeros((half, HG_DIM), f32)
        for r0 in range(0, n, size):
            beta = b[r0 + half - 1:r0 + half]
            ks += [kk[r0:r0 + half] * jnp.exp(beta - b[r0:r0 + half]), zero]
            qs += [zero, qq[r0 + half:r0 + size] * jnp.exp(b[r0 + half:r0 + size] - beta)]
        part = _dot_nt(jnp.concatenate(qs, axis=0).astype(bf16), jnp.concatenate(ks, axis=0).astype(bf16))
        if size < n:
            sh = size.bit_length() - 1
            part = jnp.where(jnp.right_shift(row, sh) == jnp.right_shift(col, sh), part, 0.0)
        a = a + part
        size *= 2

    st = st_ref[...]
    o = _dot(a.astype(bf16), v.astype(bf16)) + _dot_nt((qq * jnp.exp(b)).astype(bf16), st.astype(bf16))
    b_last = b[n - 1:n]
    k_tail = kk * jnp.exp(b_last - b)
    st_new = st * jnp.exp(b_last) + _dot_tn(v.astype(bf16), k_tail.astype(bf16))
    st_ref[...] = st_new
    y_ref[...] = _hgrn_out(o, nw_ref[...], zb_ref[...])

    @pl.when(c == pl.num_programs(2) - 1)
    def _():
        s_ref[0, 0] = st_new.T


def _hgrn_prompt(z, hg_lower, hg_norm, bsz, seq):
    nch = seq // HG_CHUNK

    def zcol(base):
        return pl.BlockSpec((HG_CHUNK, HG_DIM), lambda b, h, c: (b * nch + c, base // HG_DIM + h))

    return pl.pallas_call(
        _hgrn_prompt_kernel,
        grid=(bsz, HG_HEADS, nch),
        in_specs=[zcol(COL_QB), zcol(COL_FB), zcol(COL_IB), zcol(COL_ZB),
                  pl.BlockSpec((hg_lower.shape[0], HG_DIM), lambda b, h, c: (0, h)),
                  pl.BlockSpec((1, HG_DIM), lambda b, h, c: (0, 0))],
        out_specs=[pl.BlockSpec((HG_CHUNK, HG_DIM), lambda b, h, c: (b * nch + c, h)),
                   pl.BlockSpec((1, 1, HG_DIM, HG_DIM), lambda b, h, c: (b, h, 0, 0))],
        out_shape=[jax.ShapeDtypeStruct((bsz * seq, HG_WIDTH), f32),
                   jax.ShapeDtypeStruct((bsz, HG_HEADS, HG_DIM, HG_DIM), f32)],
        scratch_shapes=[pltpu.VMEM((HG_DIM, HG_DIM), f32)],
        compiler_params=_cparams(("parallel", "parallel", "arbitrary")),
        name="hgrn_prompt",
    )(z, z, z, z, hg_lower, hg_norm)


def _hgrn_sample_kernel(q_ref, f_ref, i_ref, zb_ref, lo_ref, nw_ref, s_ref, y_ref, so_ref, o_scr):
    nb = q_ref.shape[0]
    lb = _lower_bound(lo_ref[...])
    qq, kk, logf = _hgrn_gates(q_ref[...], f_ref[...], lb)
    v = i_ref[...]
    pad = jnp.zeros((HG_DIM - nb, HG_DIM), f32)
    to_cols = lambda x: jnp.concatenate([x, pad], axis=0).T
    f_t, k_t, q_t = to_cols(jnp.exp(logf)), to_cols(kk), to_cols(qq)
    for b in range(nb):
        s_new = f_t[:, b:b + 1] * s_ref[0, b, 0] + k_t[:, b:b + 1] * v[b:b + 1, :]
        so_ref[0, b, 0] = s_new
        o_scr[b:b + 1, :] = jnp.sum(s_new * q_t[:, b:b + 1], axis=0, keepdims=True)
    y_ref[...] = _hgrn_out(o_scr[...], nw_ref[...], zb_ref[...])


def _hgrn_sample(z, state, hg_lower, hg_norm):
    nb = z.shape[0]

    def zcol(base):
        return pl.BlockSpec((nb, HG_DIM), lambda h: (0, base // HG_DIM + h))

    sblk = pl.BlockSpec((1, nb, 1, HG_DIM, HG_DIM), lambda h: (0, 0, h, 0, 0))
    return pl.pallas_call(
        _hgrn_sample_kernel,
        grid=(HG_HEADS,),
        in_specs=[zcol(COL_QB), zcol(COL_FB), zcol(COL_IB), zcol(COL_ZB),
                  pl.BlockSpec((hg_lower.shape[0], HG_DIM), lambda h: (0, h)),
                  pl.BlockSpec((1, HG_DIM), lambda h: (0, 0)),
                  sblk],
        out_specs=[pl.BlockSpec((nb, HG_DIM), lambda h: (0, h)), sblk],
        out_shape=[jax.ShapeDtypeStruct((nb, HG_WIDTH), f32),
                   jax.ShapeDtypeStruct(state.shape, f32)],
        scratch_shapes=[pltpu.VMEM((nb, HG_DIM), f32)],
        compiler_params=_cparams(("parallel",)),
        name="hgrn_sample",
    )(z, z, z, z, hg_lower, hg_norm, state)


N_PICK = N_SEL - 1


def _head_slopes():
    h = lax.broadcasted_iota(jnp.int32, (NSA_HEADS, 1), 0).astype(f32)
    return jnp.exp2(-(h + 1.0))


def _nsa_sample_cmp_kernel(q_ref, comp_ref, m_ref, oc_ref, ps_ref, *, t_pos):
    q = q_ref[0] * jnp.asarray(HEAD_DIM ** -0.5, bf16)
    slope = _head_slopes()
    head = lax.broadcasted_iota(jnp.int32, (NSA_HEADS, 1), 0)
    ncol = lax.broadcasted_iota(jnp.int32, (1, N_CMP_PAD), 1)
    d_c = t_pos - (ncol * CMP_STRIDE + (CMP_BLOCK - 1))
    mask = d_c >= 0
    o_c = jnp.zeros((NSA_HEADS, HEAD_DIM), f32)
    ps_ref[...] = jnp.zeros_like(ps_ref)
    for g in range(NSA_KV_HEADS):
        kc = comp_ref[0, g].astype(bf16)
        vc = comp_ref[0, NSA_KV_HEADS + g].astype(bf16)
        s = jnp.where(mask, _dot_nt(q, kc) - slope * d_c.astype(f32), NEG)
        m = jnp.max(s, axis=1, keepdims=True)
        e = jnp.where(mask, jnp.exp(s - m), 0.0)
        p = e * (1.0 / jnp.maximum(jnp.sum(e, axis=1, keepdims=True), 1e-30))
        mine = jnp.right_shift(head, NSA_GROUP.bit_length() - 1) == g
        o_c = jnp.where(mine, _dot(p.astype(bf16), vc), o_c)
        imp = jnp.sum(jnp.where(mine, p, 0.0), axis=0, keepdims=True)
        ps_ref[0, g:g + 1, :] = _importance_to_blocks(imp, m_ref[...])
    oc_ref[0] = o_c


def _nsa_sample_cmp(q8, comp, m_mat, t_pos):
    nb = q8.shape[0]
    return pl.pallas_call(
        functools.partial(_nsa_sample_cmp_kernel, t_pos=t_pos),
        grid=(nb,),
        in_specs=[pl.BlockSpec((1, NSA_HEADS, HEAD_DIM), lambda b: (b, 0, 0)),
                  pl.BlockSpec((1, N_COMBO, N_CMP_PAD, HEAD_DIM), lambda b: (b, 0, 0, 0)),
                  pl.BlockSpec((N_CMP_PAD, N_SELBLK), lambda b: (0, 0))],
        out_specs=[pl.BlockSpec((1, NSA_HEADS, HEAD_DIM), lambda b: (b, 0, 0)),
                   pl.BlockSpec((1, 8, N_SELBLK), lambda b: (b, 0, 0))],
        out_shape=[jax.ShapeDtypeStruct((nb, NSA_HEADS, HEAD_DIM), f32),
                   jax.ShapeDtypeStruct((nb, 8, N_SELBLK), f32)],
        compiler_params=_cparams(("parallel",)),
        name="nsa_sample_cmp",
    )(q8, comp, m_mat)


def _nsa_sample_topk_kernel(ps_ref, idx_ref, *, last_blk):
    ps = ps_ref[...]
    lane = lax.broadcasted_iota(jnp.int32, ps.shape, 1)
    score = ps + jnp.where((lane == 0) | (lane == last_blk), FORCE_BONUS, 0.0)
    out = jnp.zeros(ps.shape, jnp.int32)
    for r in range(N_PICK):
        best = jnp.max(score, axis=1, keepdims=True)
        idx = jnp.min(jnp.where(score == best, lane, N_SELBLK), axis=1, keepdims=True)
        out = jnp.where(lane == r, idx, out)
        score = jnp.where(lane == idx, REMOVED, score)
    idx_ref[...] = out


def _nsa_sample_topk(ps_rows, last_blk):
    return pl.pallas_call(
        functools.partial(_nsa_sample_topk_kernel, last_blk=last_blk),
        out_shape=jax.ShapeDtypeStruct(ps_rows.shape, jnp.int32),
        name="nsa_sample_topk",
    )(ps_rows)


def _nsa_sample_sel_kernel(pt_ref, idx_ref, slc_ref, q_ref, oc_ref, gate_ref, slc_new_ref, win_new_ref, wc_ref,
                           o_ref, wout_ref, kv_scr, sem_ref, *, t_pos):
    n_blk = NSA_KV_HEADS * N_PICK
    halves = PAGE_SIZE // SEL_BLOCK
    b = pl.program_id(0)
    nb = pl.num_programs(0)
    slot = lax.rem(b, 2)

    def picked(seq, k):
        return idx_ref[(seq * NSA_KV_HEADS + k // N_PICK) * N_SEL + k % N_PICK]

    def blk_copies(seq, s, k):
        page = pt_ref[seq, picked(seq, k) // halves]
        g = k // N_PICK
        return [pltpu.make_async_copy(slc_ref.at[page, kind * NSA_KV_HEADS + g], kv_scr.at[s, kind, k],
                                      sem_ref.at[s]) for kind in range(2)]

    def start_all(seq, s):
        for k in range(n_blk):
            for cp in blk_copies(seq, s, k):
                cp.start()

    @pl.when(b == 0)
    def _():
        start_all(0, 0)

    @pl.when(b + 1 < nb)
    def _():
        start_all(b + 1, 1 - slot)

    for k in range(n_blk):
        for cp in blk_copies(b, slot, k):
            cp.wait()

    q = q_ref[0] * jnp.asarray(HEAD_DIM ** -0.5, bf16)
    slope = _head_slopes()
    head = lax.broadcasted_iota(jnp.int32, (NSA_HEADS, 1), 0)
    lane = lax.broadcasted_iota(jnp.int32, (1, PAGE_SIZE), 1)
    new_slc = slc_new_ref[0]
    new_win = win_new_ref[0]
    wl = wc_ref.shape[3]
    wlane = lax.broadcasted_iota(jnp.int32, (1, wl), 1)
    d_w = (wl - wlane).astype(f32)
    qf = q.astype(f32)

    def attend(s, mask, v_t, new_row, g):
        k_new = new_row[:, g * HEAD_DIM:(g + 1) * HEAD_DIM].astype(bf16).astype(f32)
        v_new = new_row[:, KV_WIDTH + g * HEAD_DIM:KV_WIDTH + (g + 1) * HEAD_DIM].astype(bf16).astype(f32)
        s_new = jnp.sum(qf * k_new, axis=1, keepdims=True)
        m = jnp.maximum(jnp.max(s, axis=1, keepdims=True), s_new)
        e, e_new = jnp.exp(s - m), jnp.exp(s_new - m)
        if mask is not None:
            e = jnp.where(mask, e, 0.0)
        den = jnp.maximum(jnp.sum(e, axis=1, keepdims=True) + e_new, 1e-30)
        num = _dot_nt(e.astype(bf16), v_t.astype(bf16)) + e_new.astype(bf16).astype(f32) * v_new
        return num * (1.0 / den)

    o_s = jnp.zeros((NSA_HEADS, HEAD_DIM), f32)
    o_w = jnp.zeros((NSA_HEADS, HEAD_DIM), f32)
    for g in range(NSA_KV_HEADS):
        mine = jnp.right_shift(head, NSA_GROUP.bit_length() - 1) == g
        pos, member = [], []
        for n in range(N_PICK):
            j = picked(b, g * N_PICK + n)
            pos.append((j // halves) * PAGE_SIZE + lane)
            member.append(jnp.right_shift(lane, SEL_BLOCK.bit_length() - 1) == j % halves)
        pos = jnp.concatenate(pos, axis=1)
        member = jnp.concatenate(member, axis=1)
        k_t = jnp.concatenate([kv_scr[slot, 0, g * N_PICK + n] for n in range(N_PICK)], axis=1)
        v_t = jnp.concatenate([kv_scr[slot, 1, g * N_PICK + n] for n in range(N_PICK)], axis=1)
        s = jnp.where(member, _dot(q, k_t.astype(bf16)) - slope * (t_pos - pos).astype(f32), NEG)
        o_s = jnp.where(mine, attend(s, member, v_t, new_slc, g), o_s)
        s = _dot(q, wc_ref[0, g].astype(bf16)) - slope * d_w
        o_w = jnp.where(mine, attend(s, None, wc_ref[0, NSA_KV_HEADS + g], new_win, g), o_w)

    gate = jnp.broadcast_to(_sigmoid(gate_ref[0]), (NSA_HEADS, LANES))
    glane = lax.broadcasted_iota(jnp.int32, (NSA_HEADS, LANES), 1)
    gcol = [jnp.sum(jnp.where(glane == 3 * head + j, gate, 0.0), axis=1, keepdims=True) for j in range(3)]
    o_ref[0] = gcol[0] * oc_ref[0] + gcol[1] * o_s + gcol[2] * o_w
    eye = lax.broadcasted_iota(jnp.int32, (HEAD_DIM, HEAD_DIM), 0) == \
        lax.broadcasted_iota(jnp.int32, (HEAD_DIM, HEAD_DIM), 1)
    for c in range(N_COMBO):
        piece = jnp.broadcast_to(new_win[:, c * HEAD_DIM:(c + 1) * HEAD_DIM], (HEAD_DIM, HEAD_DIM))
        new_col = jnp.sum(jnp.where(eye, piece, 0.0), axis=1, keepdims=True)
        wout_ref[0, c] = jnp.where(wlane == wl - 1, new_col, pltpu.roll(wc_ref[0, c], wl - 1, axis=1))


def _nsa_sample_sel(page_table, idx_flat, slc_pages_t, q8, o_c, gates, slc_new, win_new, cache_win_t, t_pos):
    nb = q8.shape[0]
    wl = cache_win_t.shape[3]
    n_blk = NSA_KV_HEADS * N_PICK
    per_seq = lambda b, pt, idx: (b, 0, 0)
    win_blk = pl.BlockSpec((1, N_COMBO, HEAD_DIM, wl), lambda b, pt, idx: (b, 0, 0, 0))
    grid_spec = pltpu.PrefetchScalarGridSpec(
        num_scalar_prefetch=2,
        grid=(nb,),
        in_specs=[pl.BlockSpec(memory_space=pl.ANY),
                  pl.BlockSpec((1, NSA_HEADS, HEAD_DIM), per_seq),
                  pl.BlockSpec((1, NSA_HEADS, HEAD_DIM), per_seq),
                  pl.BlockSpec((1, 1, LANES), per_seq),
                  pl.BlockSpec((1, 1, 2 * KV_WIDTH), per_seq),
                  pl.BlockSpec((1, 1, 2 * KV_WIDTH), per_seq),
                  win_blk],
        out_specs=[pl.BlockSpec((1, NSA_HEADS, HEAD_DIM), per_seq), win_blk],
        scratch_shapes=[pltpu.VMEM((2, 2, n_blk, HEAD_DIM, PAGE_SIZE), f32),
                        pltpu.SemaphoreType.DMA((2,))],
    )
    return pl.pallas_call(
        functools.partial(_nsa_sample_sel_kernel, t_pos=t_pos),
        grid_spec=grid_spec,
        out_shape=[jax.ShapeDtypeStruct((nb, NSA_HEADS, HEAD_DIM), f32),
                   jax.ShapeDtypeStruct(cache_win_t.shape, f32)],
        compiler_params=_cparams(("arbitrary",)),
        name="nsa_sample_sel",
    )(page_table, idx_flat, slc_pages_t, q8, o_c, gates, slc_new, win_new, cache_win_t)


def kernel(x_prompt, x_sample, cache_cmp_kv, cache_slc_kv, cache_win_kv, state_hgrn, page_table, p_prompt, p_sample,
           w_in, g_pre, cmp_pe, cmp_w1, cmp_b1, cmp_w2, hg_lower, hg_norm, w_out, g_post, ple_proj, ple_gate):
    bsz, seq, _ = x_prompt.shape
    nb = x_sample.shape[0]
    n_pool = cache_cmp_kv.shape[1]
    n_pages = page_table.shape[1]
    past = n_pages * PAGE_SIZE
    nq = seq // Q_BLOCK
    kv_tail = (2, NSA_KV_HEADS, HEAD_DIM)

    o = _ORIG
    w = w_in[0]
    w_r = jnp.concatenate([w[:, o['q_a']:o['k_cmp']], w[:, o['z_a']:o['end']], w[:, o['k_cmp']:o['gate_a']],
                           w[:, o['gate_a']:o['z_a']],
                           jnp.zeros((D_MODEL, PROJ_WIDTH - o['end']), f32)], axis=1).astype(bf16)
    cw = _compress_weights(cmp_pe[0], cmp_w1[0], cmp_b1[0], cmp_w2[0])
    m_mat = _sel_map_matrix()
    w_out_b, ple_proj_b, ple_gate_b = w_out[0].astype(bf16), ple_proj[0].astype(bf16), ple_gate[0].astype(bf16)

    xp = x_prompt.reshape(bsz * seq, D_MODEL)
    zp = _proj(xp, g_pre, w_r, 1024)
    cmp_p = zp[:, COL_CMP:COL_CMP + 2 * KV_WIDTH]
    slc_p = zp[:, COL_SLC:COL_SLC + 2 * KV_WIDTH]
    win_p = zp[:, COL_WIN:COL_WIN + 2 * KV_WIDTH]
    pages_p = cmp_p.reshape(bsz * seq // PAGE_SIZE, PAGE_SIZE, ROW_WIDTH)
    table_p = jnp.arange(bsz * seq // PAGE_SIZE, dtype=jnp.int32).reshape(bsz, seq // PAGE_SIZE)
    comp_p = _compress(pages_p, table_p, *cw, feature_major=False)
    q_t = zp[:, COL_QA:COL_QA + NSA_WIDTH].reshape(bsz, nq, Q_BLOCK, NSA_KV_HEADS, NSA_GROUP, HEAD_DIM)
    q_t = q_t.transpose(0, 3, 1, 5, 4, 2).reshape(bsz, NSA_KV_HEADS, nq, HEAD_DIM, NSA_GROUP * Q_BLOCK)
    k_slc, vt_slc = _attention_keys(slc_p, bsz, seq, augment=True)
    k_win, vt_win = _attention_keys(win_p, bsz, seq, augment=False)
    o_a = _nsa_prompt_pair(q_t, zp, comp_p, k_slc, vt_slc, k_win, vt_win, m_mat.T)
    yb_p, hg_p = _hgrn_prompt(zp, hg_lower, hg_norm, bsz, seq)
    h_p = _finish(xp, o_a, zp, yb_p, p_prompt[0].reshape(bsz * seq, PLE_DIM),
                  w_out_b, g_post, ple_proj_b, ple_gate_b, 512)

    wl_p = min(WINDOW, seq)
    out_prompt = (h_p.reshape(bsz, seq, D_MODEL),
                  cmp_p.reshape((1, bsz, seq) + kv_tail),
                  slc_p.reshape((1, bsz, seq) + kv_tail),
                  win_p.reshape((bsz, seq) + kv_tail)[:, seq - wl_p:][None],
                  hg_p[None])

    xs = x_sample.reshape(nb, D_MODEL)
    zs = _proj(xs, g_pre, w_r, nb)
    cmp_t = cache_cmp_kv.transpose(0, 1, 3, 4, 5, 2).reshape(n_pool, ROW_WIDTH, PAGE_SIZE)
    comp_s = _compress(cmp_t, page_table, *cw, feature_major=True)
    q8 = zs[:, COL_QA:COL_QA + NSA_WIDTH].astype(bf16).reshape(nb, NSA_HEADS, HEAD_DIM)
    oc_s, ps_s = _nsa_sample_cmp(q8, comp_s, m_mat, past)
    ps_rows = ps_s[:, :NSA_KV_HEADS, :].reshape(nb * NSA_KV_HEADS, N_SELBLK)
    idx = _nsa_sample_topk(ps_rows, past // SEL_BLOCK - 1)
    gates_s = zs[:, COL_GATE:COL_GATE + LANES].reshape(nb, 1, LANES)
    slc_t = cache_slc_kv.transpose(0, 1, 3, 4, 5, 2).reshape(n_pool, N_COMBO, HEAD_DIM, PAGE_SIZE)
    wl_s = cache_win_kv.shape[2]
    win_t = cache_win_kv.transpose(0, 1, 3, 4, 5, 2).reshape(nb, N_COMBO, HEAD_DIM, wl_s)
    oa_s, win_s = _nsa_sample_sel(page_table, idx[:, :N_SEL].reshape(-1), slc_t, q8, oc_s, gates_s,
                                  zs[:, COL_SLC:COL_SLC + 2 * KV_WIDTH].reshape(nb, 1, 2 * KV_WIDTH),
                                  zs[:, COL_WIN:COL_WIN + 2 * KV_WIDTH].reshape(nb, 1, 2 * KV_WIDTH),
                                  win_t, past)
    win_s = win_s.reshape((1, nb) + kv_tail + (wl_s,)).transpose(0, 1, 5, 2, 3, 4)
    yb_s, hg_s = _hgrn_sample(zs, state_hgrn, hg_lower, hg_norm)
    h_s = _finish(xs, oa_s.reshape(nb, NSA_WIDTH), zs, yb_s, p_sample[0].reshape(nb, PLE_DIM),
                  w_out_b, g_post, ple_proj_b, ple_gate_b, nb)

    out_sample = (h_s.reshape(nb, 1, D_MODEL),
                  zs[:, COL_CMP:COL_CMP + 2 * KV_WIDTH].reshape((1, nb, 1) + kv_tail),
                  zs[:, COL_SLC:COL_SLC + 2 * KV_WIDTH].reshape((1, nb, 1) + kv_tail),
                  win_s,
                  hg_s)
    return (out_prompt[0], out_sample[0]) + out_prompt[1:] + out_sample[1:]
```

```python
import functools

import numpy as np
import jax
import jax.numpy as jnp
from jax import lax
from jax.experimental import pallas as pl
from jax.experimental.pallas import tpu as pltpu

D_MODEL = 1024
PAGE_SIZE = 128
NSA_HEADS = 8
NSA_KV_HEADS = 2
NSA_GROUP = NSA_HEADS // NSA_KV_HEADS
HEAD_DIM = 64
NSA_WIDTH = NSA_HEADS * HEAD_DIM
KV_WIDTH = NSA_KV_HEADS * HEAD_DIM
CMP_BLOCK = 32
CMP_STRIDE = 16
CMP_RATIO = CMP_BLOCK // CMP_STRIDE
CMP_HIDDEN = 128
SEL_BLOCK = 64
SEL_RATIO = SEL_BLOCK // CMP_STRIDE
N_SEL = 16
WINDOW = 512
Q_BLOCK = 128
HG_HEADS = 4
HG_DIM = 128
HG_WIDTH = HG_HEADS * HG_DIM
PLE_DIM = 256
EPS = 1e-6
NEG = -1e30
FORCE_BONUS = 1e6
REMOVED = -3e38

COL_QA, COL_ZA, COL_QB, COL_FB, COL_IB, COL_ZB = 0, 512, 1024, 1536, 2048, 2560
COL_CMP, COL_SLC, COL_WIN, COL_GATE = 3072, 3328, 3584, 3840
PROJ_WIDTH = 4096
_ORIG = dict(q_a=0, k_cmp=512, gate_a=1280, z_a=1304, q_b=1816, f_b=2328, i_b=2840, z_b=3352, end=3864)

LANES = 128
VMEM_LIMIT = 56 * 1024 * 1024

SEL_CHUNK = 512
WIN_KEYS = WINDOW + Q_BLOCK
N_CMP_PAD = 512
N_SELBLK = 128

f32 = jnp.float32
bf16 = jnp.bfloat16


def _cparams(sem):
    return pltpu.CompilerParams(dimension_semantics=sem, vmem_limit_bytes=VMEM_LIMIT)


def _dot(a, b):
    return jnp.dot(a, b, preferred_element_type=f32)


def _dot_nt(a, b):
    return lax.dot_general(a, b, (((1,), (1,)), ((), ())), preferred_element_type=f32)


def _dot_tn(a, b):
    return lax.dot_general(a, b, (((0,), (0,)), ((), ())), preferred_element_type=f32)


def _sigmoid(x):
    return 1.0 / (1.0 + jnp.exp(-x))


def _silu(x):
    return x * _sigmoid(x)


PROJ_TILE = 1024
MAIN_TILES = COL_CMP // PROJ_TILE


def _proj_kernel(x_ref, g_ref, w_ref, z_ref, cmp_ref, slc_ref, win_ref, gate_ref, xn_ref):
    j = pl.program_id(1)

    @pl.when(j == 0)
    def _():
        x = x_ref[...]
        y = x * lax.rsqrt(jnp.mean(x * x, axis=-1, keepdims=True) + EPS)
        xn_ref[...] = (y * g_ref[...]).astype(bf16)

    res = _dot(xn_ref[...], w_ref[...])

    @pl.when(j < MAIN_TILES)
    def _():
        z_ref[...] = res

    @pl.when(j == MAIN_TILES)
    def _():
        base = COL_CMP
        cmp_ref[...] = res[:, COL_CMP - base:COL_SLC - base]
        slc_ref[...] = res[:, COL_SLC - base:COL_WIN - base]
        win_ref[...] = res[:, COL_WIN - base:COL_GATE - base]
        gate_ref[...] = res[:, COL_GATE - base:COL_GATE - base + LANES]


def _proj(x, g, w, tm):
    n = x.shape[0]
    row = lambda i, j: (i, 0)
    return pl.pallas_call(
        _proj_kernel,
        grid=(n // tm, PROJ_WIDTH // PROJ_TILE),
        in_specs=[pl.BlockSpec((tm, D_MODEL), row),
                  pl.BlockSpec((1, D_MODEL), lambda i, j: (0, 0)),
                  pl.BlockSpec((D_MODEL, PROJ_TILE), lambda i, j: (0, j))],
        out_specs=[pl.BlockSpec((tm, PROJ_TILE), lambda i, j: (i, jnp.minimum(j, MAIN_TILES - 1))),
                   pl.BlockSpec((tm, ROW_WIDTH), row), pl.BlockSpec((tm, ROW_WIDTH), row),
                   pl.BlockSpec((tm, ROW_WIDTH), row), pl.BlockSpec((tm, LANES), row)],
        out_shape=[jax.ShapeDtypeStruct((n, COL_CMP), f32),
                   jax.ShapeDtypeStruct((n, ROW_WIDTH), f32), jax.ShapeDtypeStruct((n, ROW_WIDTH), f32),
                   jax.ShapeDtypeStruct((n, ROW_WIDTH), f32), jax.ShapeDtypeStruct((n, LANES), f32)],
        scratch_shapes=[pltpu.VMEM((tm, D_MODEL), bf16)],
        compiler_params=_cparams(("parallel", "arbitrary")),
        name="proj",
    )(x, g, w)


def _finish_kernel(x_ref, oa_ref, za_ref, yb_ref, p_ref, wo_ref, gp_ref, pp_ref, pg_ref, h_ref):
    ya = (oa_ref[...] * _silu(za_ref[...])).astype(bf16)
    yb = yb_ref[...].astype(bf16)
    mix = _dot(ya, wo_ref[0:NSA_WIDTH, :]) + _dot(yb, wo_ref[NSA_WIDTH:, :])
    nrm = mix * lax.rsqrt(jnp.mean(mix * mix, axis=-1, keepdims=True) + EPS) * gp_ref[...]
    h = x_ref[...] + nrm
    gate = _sigmoid(_dot(h.astype(bf16), pg_ref[...]))
    h_ref[...] = h + _dot(p_ref[...].astype(bf16), pp_ref[...]) * gate


def _finish(x, oa, z, yb, p, w_out, g_post, ple_proj, ple_gate, tm):
    n = x.shape[0]
    za_blk = COL_ZA // NSA_WIDTH
    row = lambda i: (i, 0)
    const = lambda i: (0, 0)
    return pl.pallas_call(
        _finish_kernel,
        grid=(n // tm,),
        in_specs=[pl.BlockSpec((tm, D_MODEL), row),
                  pl.BlockSpec((tm, NSA_WIDTH), row),
                  pl.BlockSpec((tm, NSA_WIDTH), lambda i: (i, za_blk)),
                  pl.BlockSpec((tm, HG_WIDTH), row),
                  pl.BlockSpec((tm, PLE_DIM), row),
                  pl.BlockSpec((D_MODEL, D_MODEL), const),
                  pl.BlockSpec((1, D_MODEL), const),
                  pl.BlockSpec((PLE_DIM, D_MODEL), const),
                  pl.BlockSpec((D_MODEL, D_MODEL), const)],
        out_specs=pl.BlockSpec((tm, D_MODEL), row),
        out_shape=jax.ShapeDtypeStruct((n, D_MODEL), f32),
        compiler_params=_cparams(("parallel",)),
        name="finish",
    )(x, oa, z, yb, p, w_out, g_post, ple_proj, ple_gate)


SEGS_PER_PAGE = PAGE_SIZE // CMP_STRIDE
ROW_WIDTH = 2 * KV_WIDTH
SEG_WIDTH = CMP_STRIDE * ROW_WIDTH
N_COMBO = 2 * NSA_KV_HEADS


def _compress_kernel(pt_ref, pages_ref, wa_ref, wb_ref, pea_ref, peb_ref, b1_ref, w2_ref, out_ref,
                     stage_ref, rows_ref, sem_ref, *, n_pages, feature_major):
    b = pl.program_id(0)
    nb = pl.num_programs(0)
    slot = lax.rem(b, 2)

    def page_copy(seq, s, j):
        return pltpu.make_async_copy(pages_ref.at[pt_ref[seq, j]], stage_ref.at[s, j], sem_ref.at[s])

    def start_all(seq, s):
        def body(j, c):
            page_copy(seq, s, j).start()
            return c
        lax.fori_loop(0, n_pages, body, 0)

    @pl.when(b == 0)
    def _():
        start_all(0, 0)

    @pl.when(b + 1 < nb)
    def _():
        start_all(b + 1, 1 - slot)

    def wait_body(j, c):
        page_copy(b, slot, j).wait()
        return c
    lax.fori_loop(0, n_pages, wait_body, 0)

    lane_halves = ROW_WIDTH // LANES

    def to_rows(j, c):
        tile = stage_ref[slot, j]
        rows = tile.T if feature_major else tile
        r0 = pl.multiple_of(j * PAGE_SIZE, PAGE_SIZE)
        for h in range(lane_halves):
            rows_ref[h, pl.ds(r0, PAGE_SIZE), :] = rows[:, h * LANES:(h + 1) * LANES]
        return c
    lax.fori_loop(0, n_pages, to_rows, 0, unroll=4)

    n_seg = n_pages * SEGS_PER_PAGE
    for kind in range(lane_halves):
        cols = slice(kind * LANES, (kind + 1) * LANES)
        seg_a, seg_b = [], []
        for j in range(CMP_STRIDE):
            r = rows_ref[kind, pl.ds(j, n_seg, stride=CMP_STRIDE), :]
            seg_a.append((r + pea_ref[j][:, cols]).astype(bf16))
            seg_b.append((r + peb_ref[j][:, cols]).astype(bf16))
        pa = _dot(jnp.concatenate(seg_a, axis=1), wa_ref[kind])
        pb = _dot(jnp.concatenate(seg_b, axis=1), wb_ref[kind])
        pb = pltpu.roll(pb, n_seg - 1, axis=0)
        hid = _silu(pa + pb + b1_ref[kind])
        for g in range(NSA_KV_HEADS):
            hg = hid[:, g * CMP_HIDDEN:(g + 1) * CMP_HIDDEN].astype(bf16)
            out_ref[0, kind * NSA_KV_HEADS + g] = _dot(hg, w2_ref[kind])


def _compress(pages, page_table, wa, wb, pea, peb, b1, w2, feature_major):
    nseq, n_pages = page_table.shape
    n_seg = n_pages * SEGS_PER_PAGE
    const2 = lambda b, pt: (0, 0)
    const3 = lambda b, pt: (0, 0, 0)
    once = pl.Buffered(1)
    grid_spec = pltpu.PrefetchScalarGridSpec(
        num_scalar_prefetch=1,
        grid=(nseq,),
        in_specs=[pl.BlockSpec(memory_space=pl.ANY),
                  pl.BlockSpec(wa.shape, const3, pipeline_mode=once),
                  pl.BlockSpec(wb.shape, const3, pipeline_mode=once),
                  pl.BlockSpec((CMP_STRIDE, 1, ROW_WIDTH), const3),
                  pl.BlockSpec((CMP_STRIDE, 1, ROW_WIDTH), const3),
                  pl.BlockSpec(b1.shape, const3),
                  pl.BlockSpec((2, CMP_HIDDEN, HEAD_DIM), const3)],
        out_specs=pl.BlockSpec((1, N_COMBO, n_seg, HEAD_DIM), lambda b, pt: (b, 0, 0, 0)),
        scratch_shapes=[pltpu.VMEM((2, n_pages) + pages.shape[1:], f32),
                        pltpu.VMEM((ROW_WIDTH // LANES, n_pages * PAGE_SIZE, LANES), f32),
                        pltpu.SemaphoreType.DMA((2,))],
    )
    return pl.pallas_call(
        functools.partial(_compress_kernel, n_pages=n_pages, feature_major=feature_major),
        grid_spec=grid_spec,
        out_shape=jax.ShapeDtypeStruct((nseq, N_COMBO, n_seg, HEAD_DIM), f32),
        compiler_params=_cparams(("arbitrary",)),
        name="compress",
    )(page_table, pages, wa, wb, pea, peb, b1, w2)


def _compress_weights(cmp_pe, cmp_w1, cmp_b1, cmp_w2):
    eye = jnp.eye(NSA_KV_HEADS, dtype=f32)

    def big(w_half):
        t = jnp.einsum('kjdh,gb->kjbdgh', w_half, eye)
        return t.reshape(2, CMP_STRIDE * KV_WIDTH, NSA_KV_HEADS * CMP_HIDDEN)

    w1 = cmp_w1.reshape(2, 2, CMP_STRIDE, HEAD_DIM, CMP_HIDDEN)
    wa = big(w1[:, 0]).astype(bf16)
    wb = big(w1[:, 1]).astype(bf16)
    pe = cmp_pe.reshape(2, 2, CMP_STRIDE, HEAD_DIM)

    def pe_row(p):
        t = jnp.broadcast_to(p.transpose(1, 0, 2)[:, :, None, :],
                             (CMP_STRIDE, 2, NSA_KV_HEADS, HEAD_DIM))
        return t.reshape(CMP_STRIDE, 1, ROW_WIDTH)

    pea, peb = pe_row(pe[:, 0]), pe_row(pe[:, 1])
    b1 = jnp.broadcast_to(cmp_b1[:, None, :], (2, NSA_KV_HEADS, CMP_HIDDEN)).reshape(2, 1, NSA_KV_HEADS * CMP_HIDDEN)
    return wa, wb, pea, peb, b1, cmp_w2.astype(bf16)


def _sel_map_matrix():
    wts = np.convolve(np.ones(SEL_RATIO), np.ones(CMP_RATIO))
    m = np.zeros((N_CMP_PAD, N_SELBLK), np.float32)
    for j in range(N_SELBLK):
        for k, wk in enumerate(wts):
            n = SEL_RATIO * j + k
            if n < N_CMP_PAD - 1:
                m[n, j] = wk
    return jnp.asarray(m, dtype=bf16)


def _expand_matrix(n_chunks):
    key = np.arange(n_chunks * SEL_CHUNK) // SEL_BLOCK
    e = (np.arange(N_SELBLK)[:, None] == key[None, :]).astype(np.float32)
    return jnp.asarray(e.reshape(N_SELBLK, n_chunks, SEL_CHUNK).transpose(1, 0, 2), dtype=bf16)


def _importance_to_blocks(imp, m):
    hi = imp.astype(bf16)
    lo = (imp - hi.astype(f32)).astype(bf16)
    return _dot(hi, m) + _dot(lo, m)


def _nsa_prompt_kernel(q_ref, gate_ref, kc_ref, vc_ref, kvs_ref, kvw_ref, m_ref, e_ref, o_ref,
                       acc_ref, mx_ref, used_ref):
    g = pl.program_id(1)
    qi = pl.program_id(2)
    q0 = qi * Q_BLOCK
    rows = NSA_GROUP * Q_BLOCK

    q64 = q_ref[0, 0, 0] * jnp.asarray(HEAD_DIM ** -0.5, bf16)
    q128 = jnp.concatenate([q64, jnp.zeros_like(q64)], axis=1)
    slope = [jnp.where(g == 0, 2.0 ** -(r + 1), 2.0 ** -(NSA_GROUP + r + 1)).astype(f32)
             for r in range(NSA_GROUP)]

    qrow = lax.broadcasted_iota(jnp.int32, (Q_BLOCK, 1), 0)

    def softmax_rows(s, dist, mask):
        es, ls = [], []
        for r in range(NSA_GROUP):
            sr = jnp.where(mask, s[r * Q_BLOCK:(r + 1) * Q_BLOCK] - slope[r] * dist, NEG)
            m = jnp.max(sr, axis=1, keepdims=True)
            e = jnp.where(mask, jnp.exp(sr - m), 0.0)
            es.append(e)
            ls.append(1.0 / jnp.maximum(jnp.sum(e, axis=1, keepdims=True), 1e-30))
        return es, ls

    kc = kc_ref[0, 0].astype(bf16)
    vc = vc_ref[0, 0].astype(bf16)
    ncol = lax.broadcasted_iota(jnp.int32, (1, N_CMP_PAD), 1)
    d_c = (q0 + qrow) - (ncol * CMP_STRIDE + (CMP_BLOCK - 1))
    es, ls = softmax_rows(_dot_nt(q64, kc), d_c.astype(f32), d_c >= 0)
    ps_c = [e * l for e, l in zip(es, ls)]
    o_c = _dot(jnp.concatenate(ps_c, axis=0).astype(bf16), vc)
    imp = ps_c[0] + ps_c[1] + ps_c[2] + ps_c[3]
    p_sel = _importance_to_blocks(imp, m_ref[...])

    ps_t = p_sel.T
    blk = lax.broadcasted_iota(jnp.int32, (N_SELBLK, Q_BLOCK), 0)
    tq = q0 + lax.broadcasted_iota(jnp.int32, (N_SELBLK, Q_BLOCK), 1)
    jt = jnp.right_shift(tq, SEL_BLOCK.bit_length() - 1)
    valid = blk <= jt
    forced = (blk == 0) | (blk == jt) | (blk == jt - 1)
    score = jnp.where(valid, ps_t + jnp.where(forced, FORCE_BONUS, 0.0), NEG)
    sel_t = jnp.zeros((N_SELBLK, Q_BLOCK), f32)
    for _ in range(N_SEL):
        best = jnp.max(score, axis=0, keepdims=True)
        idx = jnp.min(jnp.where(score == best, blk, N_SELBLK), axis=0, keepdims=True)
        hit = blk == idx
        sel_t = jnp.where(hit & (best > NEG * 0.5), 1.0, sel_t)
        score = jnp.where(hit, REMOVED, score)
    sel = sel_t.T.astype(bf16)

    acc_ref[...] = jnp.zeros_like(acc_ref)
    mx_ref[...] = jnp.full_like(mx_ref, NEG)
    kcol = lax.broadcasted_iota(jnp.int32, (1, SEL_CHUNK), 1)
    d_base = (qrow - kcol).astype(f32)

    blk_per_chunk = SEL_CHUNK // SEL_BLOCK
    for c in range(e_ref.shape[0]):
        picked = jnp.max(sel_t[c * blk_per_chunk:(c + 1) * blk_per_chunk, :])
        used_ref[c] = (picked > 0.5).astype(jnp.int32)

    def sel_chunk(c, carry):
        @pl.when(used_ref[c] > 0)
        def _():
            sel_chunk_body(c)
        return carry

    def sel_chunk_body(c):
        k0 = pl.multiple_of(c * SEL_CHUNK, SEL_CHUNK)
        kv = kvs_ref[0, 0, pl.ds(k0, SEL_CHUNK), :]
        s = _dot_nt(q128, kv[:, 0:LANES])
        dist = d_base + (q0 - k0).astype(f32)
        allowed = (_dot(sel, e_ref[c]) > 0.5) & (dist >= 0.0)
        ps = []
        for r in range(NSA_GROUP):
            rs = slice(r * Q_BLOCK, (r + 1) * Q_BLOCK)
            sr = jnp.where(allowed, s[rs] - slope[r] * dist, NEG)
            m_old = mx_ref[rs]
            m_new = jnp.maximum(m_old, jnp.max(sr, axis=1, keepdims=True))
            acc_ref[rs] = acc_ref[rs] * jnp.exp(m_old - m_new)
            mx_ref[rs] = m_new
            ps.append(jnp.exp(sr - m_new).astype(bf16))
        acc_ref[...] += _dot(jnp.concatenate(ps, axis=0), kv[:, LANES:])

    n_chunks = (q0 + Q_BLOCK + SEL_CHUNK - 1) // SEL_CHUNK
    lax.fori_loop(0, n_chunks, sel_chunk, 0)
    acc = acc_ref[...]
    o_s = acc[:, 0:HEAD_DIM] * (1.0 / jnp.maximum(acc[:, HEAD_DIM:HEAD_DIM + 1], 1e-30))

    w0 = pl.multiple_of(jnp.maximum(q0 - WINDOW, 0), Q_BLOCK)
    kvw = kvw_ref[0, 0, pl.ds(w0, WIN_KEYS), :]
    wcol = lax.broadcasted_iota(jnp.int32, (1, WIN_KEYS), 1)
    d_w = (q0 - w0) + qrow - wcol
    es, _ = softmax_rows(_dot_nt(q128, kvw[:, 0:LANES]), d_w.astype(f32), (d_w >= 0) & (d_w <= WINDOW))
    acc_w = _dot(jnp.concatenate(es, axis=0).astype(bf16), kvw[:, LANES:])
    o_w = acc_w[:, 0:HEAD_DIM] * (1.0 / jnp.maximum(acc_w[:, HEAD_DIM:HEAD_DIM + 1], 1e-30))

    gate = _sigmoid(gate_ref[...])

    def gate_col(r, j):
        lo = gate[:, r * 3 + j:r * 3 + j + 1]
        hi = gate[:, (NSA_GROUP + r) * 3 + j:(NSA_GROUP + r) * 3 + j + 1]
        return jnp.where(g == 0, lo, hi)

    for r in range(NSA_GROUP):
        rs = slice(r * Q_BLOCK, (r + 1) * Q_BLOCK)
        o_ref[0, 0, 0, rs, :] = gate_col(r, 0) * o_c[rs] + gate_col(r, 1) * o_s[rs] + gate_col(r, 2) * o_w[rs]


def _nsa_prompt(q_st, z, comp, kv_slc, kv_win, m_mat, e_mat):
    bsz, nq = q_st.shape[0], q_st.shape[1]
    seq = kv_slc.shape[2]
    rows = NSA_GROUP * Q_BLOCK
    gate_blk = COL_GATE // LANES
    return pl.pallas_call(
        _nsa_prompt_kernel,
        grid=(bsz, NSA_KV_HEADS, nq),
        in_specs=[pl.BlockSpec((1, 1, 1, rows, HEAD_DIM), lambda b, g, i: (b, i, g, 0, 0)),
                  pl.BlockSpec((Q_BLOCK, LANES), lambda b, g, i: (b * nq + i, gate_blk)),
                  pl.BlockSpec((1, 1, N_CMP_PAD, HEAD_DIM), lambda b, g, i: (b, g, 0, 0)),
                  pl.BlockSpec((1, 1, N_CMP_PAD, HEAD_DIM), lambda b, g, i: (b, NSA_KV_HEADS + g, 0, 0)),
                  pl.BlockSpec((1, 1, seq, 2 * LANES), lambda b, g, i: (b, g, 0, 0)),
                  pl.BlockSpec((1, 1, seq, 2 * LANES), lambda b, g, i: (b, g, 0, 0)),
                  pl.BlockSpec((N_CMP_PAD, N_SELBLK), lambda b, g, i: (0, 0)),
                  pl.BlockSpec(e_mat.shape, lambda b, g, i: (0, 0, 0))],
        out_specs=pl.BlockSpec((1, 1, 1, rows, HEAD_DIM), lambda b, g, i: (b, i, g, 0, 0)),
        out_shape=jax.ShapeDtypeStruct((bsz, nq, NSA_KV_HEADS, rows, HEAD_DIM), f32),
        scratch_shapes=[pltpu.VMEM((rows, LANES), f32), pltpu.VMEM((rows, 1), f32),
                        pltpu.SMEM((e_mat.shape[0],), jnp.int32)],
        compiler_params=_cparams(("parallel", "parallel", "arbitrary")),
        name="nsa_prompt",
    )(q_st, z, comp, comp, kv_slc, kv_win, m_mat, e_mat)


def _attention_kv(piece, bsz, seq):
    t = piece.reshape(bsz, seq, 2, NSA_KV_HEADS, HEAD_DIM).transpose(2, 0, 3, 1, 4)
    zeros = jnp.zeros_like(t[0])
    ones = jnp.ones(t[0].shape[:-1] + (1,), f32)
    out = jnp.concatenate([t[0], zeros, t[1], ones, zeros[..., :HEAD_DIM - 1]], axis=-1)
    return out.astype(bf16)


LOG2E = 1.4426950408889634
MASK_BIG = 1e30
AUG_ROWS = 16
VT_ROWS = HEAD_DIM + 16
KEY_LOW = 256
N_FORCED = 3


def _nsa_prompt_t_kernel(qt_ref, gate_ref, kc_ref, vc_ref, ks_ref, vts_ref, kw_ref, vtw_ref, mt_ref, o_ref,
                         qa_ref, acc_ref, mg_ref, bias_ref, used_ref):
    g = pl.program_id(1)
    qi = pl.program_id(2)
    q0 = qi * Q_BLOCK
    cols = NSA_GROUP * Q_BLOCK

    lane = lax.broadcasted_iota(jnp.int32, (1, cols), 1)
    head = jnp.right_shift(lane, Q_BLOCK.bit_length() - 1)
    tq = q0 + jnp.bitwise_and(lane, Q_BLOCK - 1)
    c = jnp.exp2(-(head.astype(f32) + 1.0 + NSA_GROUP * g.astype(f32))) * LOG2E

    qt = (qt_ref[0, 0, 0] * (HEAD_DIM ** -0.5 * LOG2E)).astype(bf16)

    def softmax_cols(s, mask):
        s = jnp.where(mask, s, NEG)
        m = jnp.max(s, axis=0, keepdims=True)
        e = jnp.where(mask, jnp.exp2(s - m), 0.0)
        return e, 1.0 / jnp.maximum(jnp.sum(e, axis=0, keepdims=True), 1e-30)

    kc = kc_ref[0, 0].astype(bf16)
    vc = vc_ref[0, 0].astype(bf16)
    nrow = lax.broadcasted_iota(jnp.int32, (N_CMP_PAD, 1), 0)
    d_c = tq - (nrow * CMP_STRIDE + (CMP_BLOCK - 1))
    e, rl = softmax_cols(_dot(kc, qt) - c * d_c.astype(f32), d_c >= 0)
    p_c = e * rl
    oc_t = _dot_tn(vc, p_c.astype(bf16))
    imp_t = (p_c[:, 0:Q_BLOCK] + p_c[:, Q_BLOCK:2 * Q_BLOCK]
             + p_c[:, 2 * Q_BLOCK:3 * Q_BLOCK] + p_c[:, 3 * Q_BLOCK:4 * Q_BLOCK])
    hi = imp_t.astype(bf16)
    lo = (imp_t - hi.astype(f32)).astype(bf16)
    ps_t = _dot(mt_ref[...], hi) + _dot(mt_ref[...], lo)

    blk = lax.broadcasted_iota(jnp.int32, (N_SELBLK, Q_BLOCK), 0)
    jt = jnp.right_shift(q0 + lax.broadcasted_iota(jnp.int32, (N_SELBLK, Q_BLOCK), 1), SEL_BLOCK.bit_length() - 1)
    valid = blk <= jt
    forced = (blk == 0) | (blk == jt) | (blk == jt - 1)
    sel_t = jnp.where(valid & forced, 1.0, 0.0)
    score = jnp.where(valid & jnp.logical_not(forced), ps_t, NEG)
    for _ in range(N_SEL - N_FORCED):
        best = jnp.max(score, axis=0, keepdims=True)
        idx = jnp.min(jnp.where(score == best, blk, N_SELBLK), axis=0, keepdims=True)
        hit = blk == idx
        sel_t = jnp.where(hit & (best > NEG * 0.5), 1.0, sel_t)
        score = jnp.where(hit, REMOVED, score)
    bias_ref[...] = (sel_t - 1.0) * MASK_BIG

    blk_per_chunk = SEL_CHUNK // SEL_BLOCK
    n_chunk_max = vts_ref.shape[2]
    for ck in range(n_chunk_max):
        picked = jnp.max(sel_t[ck * blk_per_chunk:(ck + 1) * blk_per_chunk, :])
        used_ref[ck] = (picked > 0.5).astype(jnp.int32)

    c_hi = c.astype(bf16).astype(f32)
    c_mid = (c - c_hi).astype(bf16).astype(f32)
    c_lo = (c - c_hi - c_mid).astype(bf16).astype(f32)
    arow = lax.broadcasted_iota(jnp.int32, (AUG_ROWS // 2, cols), 0)
    alibi_rows = jnp.where(arow >= 6, 0.0,
                           jnp.where((arow == 0) | (arow == 3), c_hi, jnp.where((arow == 1) | (arow == 4), c_mid, c_lo)))
    qa_ref[0:HEAD_DIM, :] = qt
    qa_ref[HEAD_DIM + AUG_ROWS:, :] = jnp.zeros((qa_ref.shape[0] - HEAD_DIM - AUG_ROWS, cols), bf16)
    acc_ref[...] = jnp.zeros_like(acc_ref)
    mg_ref[...] = jnp.full_like(mg_ref, NEG)
    krow = lax.broadcasted_iota(jnp.int32, (SEL_CHUNK, 1), 0)

    def sel_chunk(ck, causal):
        k0 = pl.multiple_of(ck * SEL_CHUNK, SEL_CHUNK)
        b8 = bias_ref[pl.ds(pl.multiple_of(ck * blk_per_chunk, blk_per_chunk), blk_per_chunk), :]
        qa_ref[HEAD_DIM:HEAD_DIM + AUG_ROWS, :] = jnp.concatenate(
            [jnp.concatenate([b8] * NSA_GROUP, axis=1), alibi_rows], axis=0).astype(bf16)
        s = _dot(ks_ref[0, 0, pl.ds(k0, SEL_CHUNK), :], qa_ref[...])
        if causal:
            s = jnp.where(k0 + krow <= tq, s, NEG)
        shift = c * k0.astype(f32)
        m_old = mg_ref[...] - shift
        m_new = jnp.maximum(m_old, jnp.max(s, axis=0, keepdims=True))
        p = jnp.exp2(s - m_new).astype(bf16)
        acc_ref[...] = acc_ref[...] * jnp.exp2(m_old - m_new) + _dot(vts_ref[0, 0, ck], p)
        mg_ref[...] = m_new + shift

    def past_chunk(ck, carry):
        @pl.when(used_ref[ck] > 0)
        def _():
            sel_chunk(ck, causal=False)
        return carry

    last = q0 // SEL_CHUNK
    sel_chunk(last, causal=True)

    w0 = pl.multiple_of(jnp.maximum(q0 - WINDOW, 0), Q_BLOCK)
    kw = kw_ref[0, 0, pl.ds(w0, WIN_KEYS), :]
    q_pad = jnp.concatenate([qt, jnp.zeros((LANES - HEAD_DIM, cols), bf16)], axis=0)
    wrow = lax.broadcasted_iota(jnp.int32, (WIN_KEYS, 1), 0)
    d_w = tq - (w0 + wrow)
    e, _ = softmax_cols(_dot(kw, q_pad) - c * d_w.astype(f32), (d_w >= 0) & (d_w <= WINDOW))
    e = e.astype(bf16)
    t0 = w0 // Q_BLOCK
    acc_w = jnp.zeros((VT_ROWS, cols), f32)
    for t in range(WIN_KEYS // Q_BLOCK):
        acc_w = acc_w + _dot(vtw_ref[0, 0, t0 + t], e[t * Q_BLOCK:(t + 1) * Q_BLOCK])
    ow_t = acc_w[0:HEAD_DIM] * (1.0 / jnp.maximum(acc_w[HEAD_DIM:HEAD_DIM + 1], 1e-30))

    lax.fori_loop(0, last, past_chunk, 0)
    acc = acc_ref[...]
    os_t = acc[0:HEAD_DIM] * (1.0 / jnp.maximum(acc[HEAD_DIM:HEAD_DIM + 1], 1e-30))

    gate_t = _sigmoid(gate_ref[...]).T

    def gate_row(r, j):
        lo_row = gate_t[r * 3 + j:r * 3 + j + 1, :]
        hi_row = gate_t[(NSA_GROUP + r) * 3 + j:(NSA_GROUP + r) * 3 + j + 1, :]
        return jnp.where(g == 0, lo_row, hi_row)

    outs = []
    for r in range(NSA_GROUP):
        cs = slice(r * Q_BLOCK, (r + 1) * Q_BLOCK)
        outs.append(gate_row(r, 0) * oc_t[:, cs] + gate_row(r, 1) * os_t[:, cs] + gate_row(r, 2) * ow_t[:, cs])
    for pair in range(NSA_GROUP // 2):
        both = jnp.concatenate([outs[2 * pair], outs[2 * pair + 1]], axis=0)
        o_ref[:, pair * LANES:(pair + 1) * LANES] = both.T


def _nsa_prompt_t(q_t, z, comp, k_slc, vt_slc, k_win, vt_win, m_t):
    bsz, _, nq = q_t.shape[:3]
    seq = k_slc.shape[2]
    cols = NSA_GROUP * Q_BLOCK
    gate_blk = COL_GATE // LANES
    per_bg = lambda b, g, i: (b, g, 0, 0)
    per_bg5 = lambda b, g, i: (b, g, 0, 0, 0)
    return pl.pallas_call(
        _nsa_prompt_t_kernel,
        grid=(bsz, NSA_KV_HEADS, nq),
        in_specs=[pl.BlockSpec((1, 1, 1, HEAD_DIM, cols), lambda b, g, i: (b, g, i, 0, 0)),
                  pl.BlockSpec((Q_BLOCK, LANES), lambda b, g, i: (b * nq + i, gate_blk)),
                  pl.BlockSpec((1, 1, N_CMP_PAD, HEAD_DIM), per_bg),
                  pl.BlockSpec((1, 1, N_CMP_PAD, HEAD_DIM), lambda b, g, i: (b, NSA_KV_HEADS + g, 0, 0)),
                  pl.BlockSpec((1, 1, seq, LANES), per_bg),
                  pl.BlockSpec((1, 1) + vt_slc.shape[2:], per_bg5),
                  pl.BlockSpec((1, 1, seq, LANES), per_bg),
                  pl.BlockSpec((1, 1) + vt_win.shape[2:], per_bg5),
                  pl.BlockSpec((N_SELBLK, N_CMP_PAD), lambda b, g, i: (0, 0))],
        out_specs=pl.BlockSpec((Q_BLOCK, NSA_GROUP * HEAD_DIM), lambda b, g, i: (b * nq + i, g)),
        out_shape=jax.ShapeDtypeStruct((bsz * seq, NSA_WIDTH), f32),
        scratch_shapes=[pltpu.VMEM((LANES, cols), bf16),
                        pltpu.VMEM((VT_ROWS, cols), f32),
                        pltpu.VMEM((1, cols), f32),
                        pltpu.VMEM((N_SELBLK, Q_BLOCK), f32),
                        pltpu.SMEM((vt_slc.shape[2],), jnp.int32)],
        compiler_params=_cparams(("parallel", "parallel", "arbitrary")),
        name="nsa_prompt",
    )(q_t, z, comp, comp, k_slc, vt_slc, k_win, vt_win, m_t)


def _nsa_prompt_pair_kernel(qt_ref, gate_ref, comp_ref, caug_ref, ks_ref, vts_ref, kw_ref, vtw_ref, mt_ref, o_ref,
                            qa_ref, acc_ref, mg_ref, bias_ref, used_ref):
    qi = pl.program_id(1)
    q0 = qi * Q_BLOCK
    cols = NSA_GROUP * Q_BLOCK
    heads = range(NSA_KV_HEADS)

    lane = lax.broadcasted_iota(jnp.int32, (1, cols), 1)
    head = jnp.right_shift(lane, Q_BLOCK.bit_length() - 1)
    tq = q0 + jnp.bitwise_and(lane, Q_BLOCK - 1)
    blk = lax.broadcasted_iota(jnp.int32, (N_SELBLK, Q_BLOCK), 0)
    jt = jnp.right_shift(q0 + lax.broadcasted_iota(jnp.int32, (N_SELBLK, Q_BLOCK), 1), SEL_BLOCK.bit_length() - 1)
    valid = blk <= jt
    forced = (blk == 0) | (blk == jt) | (blk == jt - 1)
    c_end = lax.broadcasted_iota(jnp.int32, (N_CMP_PAD, 1), 0) * CMP_STRIDE + (CMP_BLOCK - 1)
    krow = lax.broadcasted_iota(jnp.int32, (SEL_CHUNK, 1), 0)
    arow = lax.broadcasted_iota(jnp.int32, (AUG_ROWS // 2, cols), 0)
    blk_per_chunk = SEL_CHUNK // SEL_BLOCK
    n_chunk_max = vts_ref.shape[2]
    last = q0 // SEL_CHUNK
    w0 = pl.multiple_of(jnp.maximum(q0 - WINDOW, 0), Q_BLOCK)
    wkey = w0 + lax.broadcasted_iota(jnp.int32, (WIN_KEYS, 1), 0)
    in_window = (wkey <= tq) & (wkey >= tq - WINDOW)
    gate_t = _sigmoid(gate_ref[...]).T

    def softmax_cols(s, mask):
        s = jnp.where(mask, s, NEG)
        m = jnp.max(s, axis=0, keepdims=True)
        e = jnp.where(mask, jnp.exp2(s - m), 0.0)
        return e, 1.0 / jnp.maximum(jnp.sum(e, axis=0, keepdims=True), 1e-30)

    slope = [jnp.exp2(-(head.astype(f32) + 1.0 + NSA_GROUP * g)) * LOG2E for g in heads]
    qts = [(qt_ref[0, g, 0] * (HEAD_DIM ** -0.5 * LOG2E)).astype(bf16) for g in heads]

    def sel_chunk(g, ck, causal):
        c = slope[g]
        k0 = pl.multiple_of(ck * SEL_CHUNK, SEL_CHUNK)
        b8 = bias_ref[g, pl.ds(pl.multiple_of(ck * blk_per_chunk, blk_per_chunk), blk_per_chunk), :]
        qa_ref[g, HEAD_DIM:HEAD_DIM + AUG_ROWS, :] = jnp.concatenate(
            [jnp.concatenate([b8] * NSA_GROUP, axis=1), alibi_rows[g]], axis=0).astype(bf16)
        s = _dot(ks_ref[0, g, pl.ds(k0, SEL_CHUNK), :], qa_ref[g])
        if causal:
            s = jnp.where(k0 + krow <= tq, s, NEG)
        shift = c * k0.astype(f32)
        m_old = mg_ref[g] - shift
        m_new = jnp.maximum(m_old, jnp.max(s, axis=0, keepdims=True))
        p = jnp.exp2(s - m_new).astype(bf16)
        acc_ref[g] = acc_ref[g] * jnp.exp2(m_old - m_new) + _dot(vts_ref[0, g, ck], p)
        mg_ref[g] = m_new + shift

    oc_t, ow_t = [], []
    alibi_rows = [None] * NSA_KV_HEADS
    for g in heads:
        c, qt = slope[g], qts[g]
        c_hi = c.astype(bf16).astype(f32)
        c_mid = (c - c_hi).astype(bf16).astype(f32)
        c_lo = (c - c_hi - c_mid).astype(bf16).astype(f32)
        c3 = jnp.where((arow == 0) | (arow == 3), c_hi, jnp.where((arow == 1) | (arow == 4), c_mid, c_lo))
        pos_rows = jnp.where(arow >= 6, 0.0, jnp.where(arow >= 3, KEY_LOW * c3, c3))
        pos_rows = jnp.concatenate([pos_rows, jnp.zeros_like(pos_rows)], axis=0).astype(bf16)
        q_pos = jnp.concatenate([qt, pos_rows, jnp.zeros((LANES - HEAD_DIM - AUG_ROWS, cols), bf16)], axis=0)

        kc = jnp.concatenate([comp_ref[0, g].astype(bf16), caug_ref[...]], axis=1)
        vc = comp_ref[0, NSA_KV_HEADS + g].astype(bf16)
        e, rl = softmax_cols(_dot(kc, q_pos), c_end <= tq)
        p_c = e * rl
        oc_t.append(_dot_tn(vc, p_c.astype(bf16)))
        imp_t = (p_c[:, 0:Q_BLOCK] + p_c[:, Q_BLOCK:2 * Q_BLOCK]
                 + p_c[:, 2 * Q_BLOCK:3 * Q_BLOCK] + p_c[:, 3 * Q_BLOCK:4 * Q_BLOCK])
        hi = imp_t.astype(bf16)
        lo = (imp_t - hi.astype(f32)).astype(bf16)
        ps_t = _dot(mt_ref[...], hi) + _dot(mt_ref[...], lo)

        sel_t = jnp.where(valid & forced, 1.0, 0.0)
        score = jnp.where(valid & jnp.logical_not(forced), ps_t, NEG)
        for _ in range(N_SEL - N_FORCED):
            best = jnp.max(score, axis=0, keepdims=True)
            idx = jnp.min(jnp.where(score == best, blk, N_SELBLK), axis=0, keepdims=True)
            hit = blk == idx
            sel_t = jnp.where(hit & (best > NEG * 0.5), 1.0, sel_t)
            score = jnp.where(hit, REMOVED, score)
        bias_ref[g] = (sel_t - 1.0) * MASK_BIG
        for ck in range(n_chunk_max):
            picked = jnp.max(sel_t[ck * blk_per_chunk:(ck + 1) * blk_per_chunk, :])
            used_ref[g, ck] = (picked > 0.5).astype(jnp.int32)

        alibi_rows[g] = jnp.where(arow >= 6, 0.0, c3)
        qa_ref[g, 0:HEAD_DIM, :] = qt
        qa_ref[g, HEAD_DIM + AUG_ROWS:, :] = jnp.zeros((qa_ref.shape[1] - HEAD_DIM - AUG_ROWS, cols), bf16)
        acc_ref[g] = jnp.zeros(acc_ref.shape[1:], f32)
        mg_ref[g] = jnp.full(mg_ref.shape[1:], NEG, f32)
        sel_chunk(g, last, causal=True)

        kw = kw_ref[0, g, pl.ds(w0, WIN_KEYS), :]
        s_w = jnp.where(in_window, _dot(kw, q_pos), NEG)
        e = jnp.exp2(s_w - jnp.max(s_w, axis=0, keepdims=True)).astype(bf16)
        t0 = w0 // Q_BLOCK
        acc_w = jnp.zeros((VT_ROWS, cols), f32)
        for t in range(WIN_KEYS // Q_BLOCK):
            acc_w = acc_w + _dot(vtw_ref[0, g, t0 + t], e[t * Q_BLOCK:(t + 1) * Q_BLOCK])
        ow_t.append(acc_w[0:HEAD_DIM] * (1.0 / jnp.maximum(acc_w[HEAD_DIM:HEAD_DIM + 1], 1e-30)))

    def past_chunk(ck, carry):
        @pl.when((used_ref[0, ck] + used_ref[1, ck]) > 0)
        def _():
            for g in heads:
                sel_chunk(g, ck, causal=False)
        return carry

    lax.fori_loop(0, last, past_chunk, 0)

    for g in heads:
        acc = acc_ref[g]
        os_t = acc[0:HEAD_DIM] * (1.0 / jnp.maximum(acc[HEAD_DIM:HEAD_DIM + 1], 1e-30))
        outs = []
        for r in range(NSA_GROUP):
            cs = slice(r * Q_BLOCK, (r + 1) * Q_BLOCK)
            row = (g * NSA_GROUP + r) * 3
            outs.append(gate_t[row:row + 1, :] * oc_t[g][:, cs] + gate_t[row + 1:row + 2, :] * os_t[:, cs]
                        + gate_t[row + 2:row + 3, :] * ow_t[g][:, cs])
        for pair in range(NSA_GROUP // 2):
            both = jnp.concatenate([outs[2 * pair], outs[2 * pair + 1]], axis=0)
            lane0 = (g * NSA_GROUP // 2 + pair) * LANES
            o_ref[:, lane0:lane0 + LANES] = both.T


def _nsa_prompt_pair(q_t, z, comp, k_slc, vt_slc, k_win, vt_win, m_t):
    bsz, _, nq = q_t.shape[:3]
    seq = k_slc.shape[2]
    cols = NSA_GROUP * Q_BLOCK
    gate_blk = COL_GATE // LANES
    per_b4 = lambda b, i: (b, 0, 0, 0)
    per_b5 = lambda b, i: (b, 0, 0, 0, 0)
    return pl.pallas_call(
        _nsa_prompt_pair_kernel,
        grid=(bsz, nq),
        in_specs=[pl.BlockSpec((1, NSA_KV_HEADS, 1, HEAD_DIM, cols), lambda b, i: (b, 0, i, 0, 0)),
                  pl.BlockSpec((Q_BLOCK, LANES), lambda b, i: (b * nq + i, 0)),
                  pl.BlockSpec((1, N_COMBO, N_CMP_PAD, HEAD_DIM), per_b4),
                  pl.BlockSpec((N_CMP_PAD, HEAD_DIM), lambda b, i: (0, 0)),
                  pl.BlockSpec((1, NSA_KV_HEADS, seq, LANES), per_b4),
                  pl.BlockSpec((1,) + vt_slc.shape[1:], per_b5),
                  pl.BlockSpec((1, NSA_KV_HEADS, seq, LANES), per_b4),
                  pl.BlockSpec((1,) + vt_win.shape[1:], per_b5),
                  pl.BlockSpec((N_SELBLK, N_CMP_PAD), lambda b, i: (0, 0))],
        out_specs=pl.BlockSpec((Q_BLOCK, NSA_WIDTH), lambda b, i: (b * nq + i, 0)),
        out_shape=jax.ShapeDtypeStruct((bsz * seq, NSA_WIDTH), f32),
        scratch_shapes=[pltpu.VMEM((NSA_KV_HEADS, LANES, cols), bf16),
                        pltpu.VMEM((NSA_KV_HEADS, VT_ROWS, cols), f32),
                        pltpu.VMEM((NSA_KV_HEADS, 1, cols), f32),
                        pltpu.VMEM((NSA_KV_HEADS, N_SELBLK, Q_BLOCK), f32),
                        pltpu.SMEM((NSA_KV_HEADS, vt_slc.shape[2]), jnp.int32)],
        compiler_params=_cparams(("parallel", "arbitrary")),
        name="nsa_prompt",
    )(q_t, z, comp, _cmp_position_columns(), k_slc, vt_slc, k_win, vt_win, m_t)


def _position_columns(pos):
    low, high = (pos % KEY_LOW)[:, None], (pos // KEY_LOW)[:, None]
    return np.concatenate([low, low, low, high, high, high], axis=1).astype(np.float32)


def _cmp_position_columns():
    aug = np.zeros((N_CMP_PAD, HEAD_DIM), np.float32)
    aug[:, 0:6] = _position_columns(np.arange(N_CMP_PAD) * CMP_STRIDE + CMP_BLOCK - 1)
    return jnp.asarray(aug, dtype=bf16)


def _attention_keys(piece, bsz, seq, augment):
    tile = SEL_CHUNK if augment else Q_BLOCK
    t = piece.reshape(bsz, seq, 2, NSA_KV_HEADS, HEAD_DIM)
    k = t[:, :, 0].transpose(0, 2, 1, 3)
    pos = np.arange(seq)
    extra = np.zeros((seq, LANES - HEAD_DIM), np.float32)
    if augment:
        extra[np.arange(seq), (pos // SEL_BLOCK) % (SEL_CHUNK // SEL_BLOCK)] = 1.0
        extra[:, 8:11] = (pos % KEY_LOW)[:, None]
        extra[:, 11:14] = (pos % SEL_CHUNK - pos % KEY_LOW)[:, None]
    else:
        extra[:, 0:6] = _position_columns(pos)
    k_aug = jnp.concatenate([k, jnp.broadcast_to(jnp.asarray(extra), k.shape[:2] + extra.shape)], axis=-1)
    v_t = t[:, :, 1].transpose(0, 2, 3, 1)
    ones = jnp.ones(v_t.shape[:2] + (1, seq), f32)
    zeros = jnp.zeros(v_t.shape[:2] + (VT_ROWS - HEAD_DIM - 1, seq), f32)
    v_aug = jnp.concatenate([v_t, ones, zeros], axis=2)
    v_aug = v_aug.reshape(bsz, NSA_KV_HEADS, VT_ROWS, seq // tile, tile).transpose(0, 1, 3, 2, 4)
    return k_aug.astype(bf16), v_aug.astype(bf16)


HG_CHUNK = 128
HG_BAND = 8


def _lower_bound(lo):
    m = jnp.max(lo, axis=0, keepdims=True)
    e = jnp.exp(lo - m)
    return e[0:1] / jnp.sum(e, axis=0, keepdims=True)


def _hgrn_gates(q, fl, lb):
    sg = _sigmoid(fl)
    f = lb + (1.0 - lb) * sg
    k = (1.0 - lb) * (1.0 - sg)
    return _silu(q), k, jnp.log(f)


def _hgrn_out(o, nw, zb):
    o = o * lax.rsqrt(jnp.mean(o * o, axis=-1, keepdims=True) + EPS) * nw
    return o * _silu(zb)


def _hgrn_prompt_kernel(q_ref, f_ref, i_ref, zb_ref, lo_ref, nw_ref, y_ref, s_ref, st_ref):
    c = pl.program_id(1)

    @pl.when(c == 0)
    def _():
        st_ref[...] = jnp.zeros_like(st_ref)

    lb_all = _lower_bound(lo_ref[...])
    for h in range(HG_HEADS):
        hs = slice(h * HG_DIM, (h + 1) * HG_DIM)
        st_new, y = _hgrn_chunk(q_ref[:, hs], f_ref[:, hs], i_ref[:, hs], lb_all[:, hs], st_ref[h])
        st_ref[h] = st_new
        y_ref[:, hs] = _hgrn_out(y, nw_ref[...], zb_ref[:, hs])

        @pl.when(c == pl.num_programs(1) - 1)
        def _():
            s_ref[0, h] = st_new.T


def _hgrn_chunk(q, fl, v, lb, st):
    n = HG_CHUNK
    qq, kk, logf = _hgrn_gates(q, fl, lb)
    row = lax.broadcasted_iota(jnp.int32, (n, n), 0)
    col = lax.broadcasted_iota(jnp.int32, (n, n), 1)

    tril = jnp.where(col <= row, 1.0, 0.0).astype(bf16)
    hi = logf.astype(bf16)
    r1 = logf - hi.astype(f32)
    mid = r1.astype(bf16)
    lo = (r1 - mid.astype(f32)).astype(bf16)
    b = _dot(tril, hi) + _dot(tril, mid) + _dot(tril, lo)

    a = jnp.zeros((n, n), f32)
    sub = jnp.bitwise_and(row, HG_BAND - 1)
    for d in range(HG_BAND):
        if d == 0:
            term = qq * kk
        else:
            ok = sub >= d
            decay = jnp.exp(jnp.where(ok, b - pltpu.roll(b, d, axis=0), 0.0))
            term = jnp.where(ok, qq * pltpu.roll(kk, d, axis=0) * decay, 0.0)
        a = a + jnp.where(col == row - d, jnp.sum(term, axis=1, keepdims=True), 0.0)

    size = 2 * HG_BAND
    while size <= n:
        half = size // 2
        qs, ks = [], []
        zero = jnp.zeros((half, HG_DIM), f32)
        for r0 in range(0, n, size):
            beta = b[r0 + half - 1:r0 + half]
            ks += [kk[r0:r0 + half] * jnp.exp(beta - b[r0:r0 + half]), zero]
            qs += [zero, qq[r0 + half:r0 + size] * jnp.exp(b[r0 + half:r0 + size] - beta)]
        part = _dot_nt(jnp.concatenate(qs, axis=0).astype(bf16), jnp.concatenate(ks, axis=0).astype(bf16))
        if size < n:
            sh = size.bit_length() - 1
            part = jnp.where(jnp.right_shift(row, sh) == jnp.right_shift(col, sh), part, 0.0)
        a = a + part
        size *= 2

    o = _dot(a.astype(bf16), v.astype(bf16)) + _dot_nt((qq * jnp.exp(b)).astype(bf16), st.astype(bf16))
    b_last = b[n - 1:n]
    k_tail = kk * jnp.exp(b_last - b)
    st_new = st * jnp.exp(b_last) + _dot_tn(v.astype(bf16), k_tail.astype(bf16))
    return st_new, o


def _hgrn_prompt(z, hg_lower, hg_norm, bsz, seq):
    nch = seq // HG_CHUNK

    def zcol(base):
        return pl.BlockSpec((HG_CHUNK, HG_WIDTH), lambda b, c: (b * nch + c, base // HG_WIDTH))

    return pl.pallas_call(
        _hgrn_prompt_kernel,
        grid=(bsz, nch),
        in_specs=[zcol(COL_QB), zcol(COL_FB), zcol(COL_IB), zcol(COL_ZB),
                  pl.BlockSpec(hg_lower.shape, lambda b, c: (0, 0)),
                  pl.BlockSpec((1, HG_DIM), lambda b, c: (0, 0))],
        out_specs=[pl.BlockSpec((HG_CHUNK, HG_WIDTH), lambda b, c: (b * nch + c, 0)),
                   pl.BlockSpec((1, HG_HEADS, HG_DIM, HG_DIM), lambda b, c: (b, 0, 0, 0))],
        out_shape=[jax.ShapeDtypeStruct((bsz * seq, HG_WIDTH), f32),
                   jax.ShapeDtypeStruct((bsz, HG_HEADS, HG_DIM, HG_DIM), f32)],
        scratch_shapes=[pltpu.VMEM((HG_HEADS, HG_DIM, HG_DIM), f32)],
        compiler_params=_cparams(("parallel", "arbitrary")),
        name="hgrn_prompt",
    )(z, z, z, z, hg_lower, hg_norm)


def _hgrn_sample_kernel(q_ref, f_ref, i_ref, zb_ref, lo_ref, nw_ref, s_ref, y_ref, so_ref, o_scr):
    nb = q_ref.shape[0]
    lb = _lower_bound(lo_ref[...])
    qq, kk, logf = _hgrn_gates(q_ref[...], f_ref[...], lb)
    v = i_ref[...]
    pad = jnp.zeros((HG_DIM - nb, HG_DIM), f32)
    to_cols = lambda x: jnp.concatenate([x, pad], axis=0).T
    f_t, k_t, q_t = to_cols(jnp.exp(logf)), to_cols(kk), to_cols(qq)
    for b in range(nb):
        s_new = f_t[:, b:b + 1] * s_ref[0, b, 0] + k_t[:, b:b + 1] * v[b:b + 1, :]
        so_ref[0, b, 0] = s_new
        o_scr[b:b + 1, :] = jnp.sum(s_new * q_t[:, b:b + 1], axis=0, keepdims=True)
    y_ref[...] = _hgrn_out(o_scr[...], nw_ref[...], zb_ref[...])


def _hgrn_sample(z, state, hg_lower, hg_norm):
    nb = z.shape[0]

    def zcol(base):
        return pl.BlockSpec((nb, HG_DIM), lambda h: (0, base // HG_DIM + h))

    sblk = pl.BlockSpec((1, nb, 1, HG_DIM, HG_DIM), lambda h: (0, 0, h, 0, 0))
    return pl.pallas_call(
        _hgrn_sample_kernel,
        grid=(HG_HEADS,),
        in_specs=[zcol(COL_QB), zcol(COL_FB), zcol(COL_IB), zcol(COL_ZB),
                  pl.BlockSpec((hg_lower.shape[0], HG_DIM), lambda h: (0, h)),
                  pl.BlockSpec((1, HG_DIM), lambda h: (0, 0)),
                  sblk],
        out_specs=[pl.BlockSpec((nb, HG_DIM), lambda h: (0, h)), sblk],
        out_shape=[jax.ShapeDtypeStruct((nb, HG_WIDTH), f32),
                   jax.ShapeDtypeStruct(state.shape, f32)],
        scratch_shapes=[pltpu.VMEM((nb, HG_DIM), f32)],
        compiler_params=_cparams(("parallel",)),
        name="hgrn_sample",
    )(z, z, z, z, hg_lower, hg_norm, state)


N_PICK = N_SEL - 1


def _head_slopes():
    h = lax.broadcasted_iota(jnp.int32, (NSA_HEADS, 1), 0).astype(f32)
    return jnp.exp2(-(h + 1.0))


def _nsa_sample_cmp_kernel(q_ref, comp_ref, m_ref, oc_ref, ps_ref, *, t_pos):
    q = q_ref[0] * jnp.asarray(HEAD_DIM ** -0.5, bf16)
    slope = _head_slopes()
    head = lax.broadcasted_iota(jnp.int32, (NSA_HEADS, 1), 0)
    ncol = lax.broadcasted_iota(jnp.int32, (1, N_CMP_PAD), 1)
    d_c = t_pos - (ncol * CMP_STRIDE + (CMP_BLOCK - 1))
    mask = d_c >= 0
    o_c = jnp.zeros((NSA_HEADS, HEAD_DIM), f32)
    ps_ref[...] = jnp.zeros_like(ps_ref)
    for g in range(NSA_KV_HEADS):
        kc = comp_ref[0, g].astype(bf16)
        vc = comp_ref[0, NSA_KV_HEADS + g].astype(bf16)
        s = jnp.where(mask, _dot_nt(q, kc) - slope * d_c.astype(f32), NEG)
        m = jnp.max(s, axis=1, keepdims=True)
        e = jnp.where(mask, jnp.exp(s - m), 0.0)
        p = e * (1.0 / jnp.maximum(jnp.sum(e, axis=1, keepdims=True), 1e-30))
        mine = jnp.right_shift(head, NSA_GROUP.bit_length() - 1) == g
        o_c = jnp.where(mine, _dot(p.astype(bf16), vc), o_c)
        imp = jnp.sum(jnp.where(mine, p, 0.0), axis=0, keepdims=True)
        ps_ref[0, g:g + 1, :] = _importance_to_blocks(imp, m_ref[...])
    oc_ref[0] = o_c


def _nsa_sample_cmp(q8, comp, m_mat, t_pos):
    nb = q8.shape[0]
    return pl.pallas_call(
        functools.partial(_nsa_sample_cmp_kernel, t_pos=t_pos),
        grid=(nb,),
        in_specs=[pl.BlockSpec((1, NSA_HEADS, HEAD_DIM), lambda b: (b, 0, 0)),
                  pl.BlockSpec((1, N_COMBO, N_CMP_PAD, HEAD_DIM), lambda b: (b, 0, 0, 0)),
                  pl.BlockSpec((N_CMP_PAD, N_SELBLK), lambda b: (0, 0))],
        out_specs=[pl.BlockSpec((1, NSA_HEADS, HEAD_DIM), lambda b: (b, 0, 0)),
                   pl.BlockSpec((1, 8, N_SELBLK), lambda b: (b, 0, 0))],
        out_shape=[jax.ShapeDtypeStruct((nb, NSA_HEADS, HEAD_DIM), f32),
                   jax.ShapeDtypeStruct((nb, 8, N_SELBLK), f32)],
        compiler_params=_cparams(("parallel",)),
        name="nsa_sample_cmp",
    )(q8, comp, m_mat)


def _nsa_sample_topk_kernel(ps_ref, idx_ref, *, last_blk):
    ps = ps_ref[...]
    lane = lax.broadcasted_iota(jnp.int32, ps.shape, 1)
    score = ps + jnp.where((lane == 0) | (lane == last_blk), FORCE_BONUS, 0.0)
    out = jnp.zeros(ps.shape, jnp.int32)
    for r in range(N_PICK):
        best = jnp.max(score, axis=1, keepdims=True)
        idx = jnp.min(jnp.where(score == best, lane, N_SELBLK), axis=1, keepdims=True)
        out = jnp.where(lane == r, idx, out)
        score = jnp.where(lane == idx, REMOVED, score)
    idx_ref[...] = out


def _nsa_sample_topk(ps_rows, last_blk):
    return pl.pallas_call(
        functools.partial(_nsa_sample_topk_kernel, last_blk=last_blk),
        out_shape=jax.ShapeDtypeStruct(ps_rows.shape, jnp.int32),
        name="nsa_sample_topk",
    )(ps_rows)


def _nsa_sample_sel_kernel(pt_ref, idx_ref, slc_ref, q_ref, oc_ref, gate_ref, slc_new_ref, win_new_ref, wc_ref,
                           o_ref, wout_ref, kv_scr, sem_ref, *, t_pos):
    n_blk = NSA_KV_HEADS * N_PICK
    halves = PAGE_SIZE // SEL_BLOCK
    b = pl.program_id(0)
    nb = pl.num_programs(0)
    slot = lax.rem(b, 2)

    def picked(seq, k):
        return idx_ref[(seq * NSA_KV_HEADS + k // N_PICK) * N_SEL + k % N_PICK]

    def blk_copies(seq, s, k):
        page = pt_ref[seq, picked(seq, k) // halves]
        g = k // N_PICK
        return [pltpu.make_async_copy(slc_ref.at[page, kind * NSA_KV_HEADS + g], kv_scr.at[s, kind, k],
                                      sem_ref.at[s]) for kind in range(2)]

    def start_all(seq, s):
        for k in range(n_blk):
            for cp in blk_copies(seq, s, k):
                cp.start()

    @pl.when(b == 0)
    def _():
        start_all(0, 0)

    @pl.when(b + 1 < nb)
    def _():
        start_all(b + 1, 1 - slot)

    for k in range(n_blk):
        for cp in blk_copies(b, slot, k):
            cp.wait()

    q = q_ref[0] * jnp.asarray(HEAD_DIM ** -0.5, bf16)
    slope = _head_slopes()
    head = lax.broadcasted_iota(jnp.int32, (NSA_HEADS, 1), 0)
    lane = lax.broadcasted_iota(jnp.int32, (1, PAGE_SIZE), 1)
    new_slc = slc_new_ref[0]
    new_win = win_new_ref[0]
    wl = wc_ref.shape[3]
    wlane = lax.broadcasted_iota(jnp.int32, (1, wl), 1)
    d_w = (wl - wlane).astype(f32)
    qf = q.astype(f32)

    def attend(s, mask, v_t, new_row, g):
        k_new = new_row[:, g * HEAD_DIM:(g + 1) * HEAD_DIM].astype(bf16).astype(f32)
        v_new = new_row[:, KV_WIDTH + g * HEAD_DIM:KV_WIDTH + (g + 1) * HEAD_DIM].astype(bf16).astype(f32)
        s_new = jnp.sum(qf * k_new, axis=1, keepdims=True)
        m = jnp.maximum(jnp.max(s, axis=1, keepdims=True), s_new)
        e, e_new = jnp.exp(s - m), jnp.exp(s_new - m)
        if mask is not None:
            e = jnp.where(mask, e, 0.0)
        den = jnp.maximum(jnp.sum(e, axis=1, keepdims=True) + e_new, 1e-30)
        num = _dot_nt(e.astype(bf16), v_t.astype(bf16)) + e_new.astype(bf16).astype(f32) * v_new
        return num * (1.0 / den)

    o_s = jnp.zeros((NSA_HEADS, HEAD_DIM), f32)
    o_w = jnp.zeros((NSA_HEADS, HEAD_DIM), f32)
    for g in range(NSA_KV_HEADS):
        mine = jnp.right_shift(head, NSA_GROUP.bit_length() - 1) == g
        pos, member = [], []
        for n in range(N_PICK):
            j = picked(b, g * N_PICK + n)
            pos.append((j // halves) * PAGE_SIZE + lane)
            member.append(jnp.right_shift(lane, SEL_BLOCK.bit_length() - 1) == j % halves)
        pos = jnp.concatenate(pos, axis=1)
        member = jnp.concatenate(member, axis=1)
        k_t = jnp.concatenate([kv_scr[slot, 0, g * N_PICK + n] for n in range(N_PICK)], axis=1)
        v_t = jnp.concatenate([kv_scr[slot, 1, g * N_PICK + n] for n in range(N_PICK)], axis=1)
        s = jnp.where(member, _dot(q, k_t.astype(bf16)) - slope * (t_pos - pos).astype(f32), NEG)
        o_s = jnp.where(mine, attend(s, member, v_t, new_slc, g), o_s)
        s = _dot(q, wc_ref[0, g].astype(bf16)) - slope * d_w
        o_w = jnp.where(mine, attend(s, None, wc_ref[0, NSA_KV_HEADS + g], new_win, g), o_w)

    gate = jnp.broadcast_to(_sigmoid(gate_ref[0]), (NSA_HEADS, LANES))
    glane = lax.broadcasted_iota(jnp.int32, (NSA_HEADS, LANES), 1)
    gcol = [jnp.sum(jnp.where(glane == 3 * head + j, gate, 0.0), axis=1, keepdims=True) for j in range(3)]
    o_ref[0] = gcol[0] * oc_ref[0] + gcol[1] * o_s + gcol[2] * o_w
    eye = lax.broadcasted_iota(jnp.int32, (HEAD_DIM, HEAD_DIM), 0) == \
        lax.broadcasted_iota(jnp.int32, (HEAD_DIM, HEAD_DIM), 1)
    for c in range(N_COMBO):
        piece = jnp.broadcast_to(new_win[:, c * HEAD_DIM:(c + 1) * HEAD_DIM], (HEAD_DIM, HEAD_DIM))
        new_col = jnp.sum(jnp.where(eye, piece, 0.0), axis=1, keepdims=True)
        wout_ref[0, c] = jnp.where(wlane == wl - 1, new_col, pltpu.roll(wc_ref[0, c], wl - 1, axis=1))


def _nsa_sample_sel(page_table, idx_flat, slc_pages_t, q8, o_c, gates, slc_new, win_new, cache_win_t, t_pos):
    nb = q8.shape[0]
    wl = cache_win_t.shape[3]
    n_blk = NSA_KV_HEADS * N_PICK
    per_seq = lambda b, pt, idx: (b, 0, 0)
    win_blk = pl.BlockSpec((1, N_COMBO, HEAD_DIM, wl), lambda b, pt, idx: (b, 0, 0, 0))
    grid_spec = pltpu.PrefetchScalarGridSpec(
        num_scalar_prefetch=2,
        grid=(nb,),
        in_specs=[pl.BlockSpec(memory_space=pl.ANY),
                  pl.BlockSpec((1, NSA_HEADS, HEAD_DIM), per_seq),
                  pl.BlockSpec((1, NSA_HEADS, HEAD_DIM), per_seq),
                  pl.BlockSpec((1, 1, LANES), per_seq),
                  pl.BlockSpec((1, 1, 2 * KV_WIDTH), per_seq),
                  pl.BlockSpec((1, 1, 2 * KV_WIDTH), per_seq),
                  win_blk],
        out_specs=[pl.BlockSpec((1, NSA_HEADS, HEAD_DIM), per_seq), win_blk],
        scratch_shapes=[pltpu.VMEM((2, 2, n_blk, HEAD_DIM, PAGE_SIZE), f32),
                        pltpu.SemaphoreType.DMA((2,))],
    )
    return pl.pallas_call(
        functools.partial(_nsa_sample_sel_kernel, t_pos=t_pos),
        grid_spec=grid_spec,
        out_shape=[jax.ShapeDtypeStruct((nb, NSA_HEADS, HEAD_DIM), f32),
                   jax.ShapeDtypeStruct(cache_win_t.shape, f32)],
        compiler_params=_cparams(("arbitrary",)),
        name="nsa_sample_sel",
    )(page_table, idx_flat, slc_pages_t, q8, o_c, gates, slc_new, win_new, cache_win_t)


def kernel(x_prompt, x_sample, cache_cmp_kv, cache_slc_kv, cache_win_kv, state_hgrn, page_table, p_prompt, p_sample,
           w_in, g_pre, cmp_pe, cmp_w1, cmp_b1, cmp_w2, hg_lower, hg_norm, w_out, g_post, ple_proj, ple_gate):
    bsz, seq, _ = x_prompt.shape
    nb = x_sample.shape[0]
    n_pool = cache_cmp_kv.shape[1]
    n_pages = page_table.shape[1]
    past = n_pages * PAGE_SIZE
    nq = seq // Q_BLOCK
    kv_tail = (2, NSA_KV_HEADS, HEAD_DIM)

    o = _ORIG
    w = w_in[0]
    w_r = jnp.concatenate([w[:, o['q_a']:o['k_cmp']], w[:, o['z_a']:o['end']], w[:, o['k_cmp']:o['gate_a']],
                           w[:, o['gate_a']:o['z_a']],
                           jnp.zeros((D_MODEL, PROJ_WIDTH - o['end']), f32)], axis=1).astype(bf16)
    cw = _compress_weights(cmp_pe[0], cmp_w1[0], cmp_b1[0], cmp_w2[0])
    m_mat = _sel_map_matrix()
    w_out_b, ple_proj_b, ple_gate_b = w_out[0].astype(bf16), ple_proj[0].astype(bf16), ple_gate[0].astype(bf16)

    xp = x_prompt.reshape(bsz * seq, D_MODEL)
    zp, cmp_p, slc_p, win_p, gates_p = _proj(xp, g_pre, w_r, 1024)
    pages_p = cmp_p.reshape(bsz * seq // PAGE_SIZE, PAGE_SIZE, ROW_WIDTH)
    table_p = jnp.arange(bsz * seq // PAGE_SIZE, dtype=jnp.int32).reshape(bsz, seq // PAGE_SIZE)
    comp_p = _compress(pages_p, table_p, *cw, feature_major=False)
    q_t = zp[:, COL_QA:COL_QA + NSA_WIDTH].reshape(bsz, nq, Q_BLOCK, NSA_KV_HEADS, NSA_GROUP, HEAD_DIM)
    q_t = q_t.transpose(0, 3, 1, 5, 4, 2).reshape(bsz, NSA_KV_HEADS, nq, HEAD_DIM, NSA_GROUP * Q_BLOCK)
    k_slc, vt_slc = _attention_keys(slc_p, bsz, seq, augment=True)
    k_win, vt_win = _attention_keys(win_p, bsz, seq, augment=False)
    o_a = _nsa_prompt_pair(q_t, gates_p, comp_p, k_slc, vt_slc, k_win, vt_win, m_mat.T)
    yb_p, hg_p = _hgrn_prompt(zp, hg_lower, hg_norm, bsz, seq)
    h_p = _finish(xp, o_a, zp, yb_p, p_prompt[0].reshape(bsz * seq, PLE_DIM),
                  w_out_b, g_post, ple_proj_b, ple_gate_b, 512)

    wl_p = min(WINDOW, seq)
    out_prompt = (h_p.reshape(bsz, seq, D_MODEL),
                  cmp_p.reshape((1, bsz, seq) + kv_tail),
                  slc_p.reshape((1, bsz, seq) + kv_tail),
                  win_p.reshape((bsz, seq) + kv_tail)[:, seq - wl_p:][None],
                  hg_p[None])

    xs = x_sample.reshape(nb, D_MODEL)
    zs, cmp_s, slc_s, win_new, gates_s = _proj(xs, g_pre, w_r, nb)
    cmp_t = cache_cmp_kv.transpose(0, 1, 3, 4, 5, 2).reshape(n_pool, ROW_WIDTH, PAGE_SIZE)
    comp_s = _compress(cmp_t, page_table, *cw, feature_major=True)
    q8 = zs[:, COL_QA:COL_QA + NSA_WIDTH].astype(bf16).reshape(nb, NSA_HEADS, HEAD_DIM)
    oc_s, ps_s = _nsa_sample_cmp(q8, comp_s, m_mat, past)
    ps_rows = ps_s[:, :NSA_KV_HEADS, :].reshape(nb * NSA_KV_HEADS, N_SELBLK)
    idx = _nsa_sample_topk(ps_rows, past // SEL_BLOCK - 1)
    slc_t = cache_slc_kv.transpose(0, 1, 3, 4, 5, 2).reshape(n_pool, N_COMBO, HEAD_DIM, PAGE_SIZE)
    wl_s = cache_win_kv.shape[2]
    win_t = cache_win_kv.transpose(0, 1, 3, 4, 5, 2).reshape(nb, N_COMBO, HEAD_DIM, wl_s)
    oa_s, win_s = _nsa_sample_sel(page_table, idx[:, :N_SEL].reshape(-1), slc_t, q8, oc_s,
                                  gates_s.reshape(nb, 1, LANES), slc_s.reshape(nb, 1, ROW_WIDTH),
                                  win_new.reshape(nb, 1, ROW_WIDTH), win_t, past)
    win_s = win_s.reshape((1, nb) + kv_tail + (wl_s,)).transpose(0, 1, 5, 2, 3, 4)
    yb_s, hg_s = _hgrn_sample(zs, state_hgrn, hg_lower, hg_norm)
    h_s = _finish(xs, oa_s.reshape(nb, NSA_WIDTH), zs, yb_s, p_sample[0].reshape(nb, PLE_DIM),
                  w_out_b, g_post, ple_proj_b, ple_gate_b, nb)

    out_sample = (h_s.reshape(nb, 1, D_MODEL),
                  cmp_s.reshape((1, nb, 1) + kv_tail),
                  slc_s.reshape((1, nb, 1) + kv_tail),
                  win_s,
                  hg_s)
    return (out_prompt[0], out_sample[0]) + out_prompt[1:] + out_sample[1:]
```

```python
import functools

import numpy as np
import jax
import jax.numpy as jnp
from jax import lax
from jax.experimental import pallas as pl
from jax.experimental.pallas import tpu as pltpu

D_MODEL = 1024
PAGE_SIZE = 128
NSA_HEADS = 8
NSA_KV_HEADS = 2
NSA_GROUP = NSA_HEADS // NSA_KV_HEADS
HEAD_DIM = 64
NSA_WIDTH = NSA_HEADS * HEAD_DIM
KV_WIDTH = NSA_KV_HEADS * HEAD_DIM
CMP_BLOCK = 32
CMP_STRIDE = 16
CMP_RATIO = CMP_BLOCK // CMP_STRIDE
CMP_HIDDEN = 128
SEL_BLOCK = 64
SEL_RATIO = SEL_BLOCK // CMP_STRIDE
N_SEL = 16
WINDOW = 512
Q_BLOCK = 128
HG_HEADS = 4
HG_DIM = 128
HG_WIDTH = HG_HEADS * HG_DIM
PLE_DIM = 256
EPS = 1e-6
NEG = -1e30
FORCE_BONUS = 1e6
REMOVED = -3e38

COL_QA, COL_ZA, COL_QB, COL_FB, COL_IB, COL_ZB = 0, 512, 1024, 1536, 2048, 2560
COL_CMP, COL_SLC, COL_WIN, COL_GATE = 3072, 3328, 3584, 3840
PROJ_WIDTH = 4096
_ORIG = dict(q_a=0, k_cmp=512, gate_a=1280, z_a=1304, q_b=1816, f_b=2328, i_b=2840, z_b=3352, end=3864)

LANES = 128
VMEM_LIMIT = 56 * 1024 * 1024

SEL_CHUNK = 512
WIN_KEYS = WINDOW + Q_BLOCK
N_CMP_PAD = 512
N_SELBLK = 128

f32 = jnp.float32
bf16 = jnp.bfloat16


def _cparams(sem):
    return pltpu.CompilerParams(dimension_semantics=sem, vmem_limit_bytes=VMEM_LIMIT)


def _dot(a, b):
    return jnp.dot(a, b, preferred_element_type=f32)


def _dot_nt(a, b):
    return lax.dot_general(a, b, (((1,), (1,)), ((), ())), preferred_element_type=f32)


def _dot_tn(a, b):
    return lax.dot_general(a, b, (((0,), (0,)), ((), ())), preferred_element_type=f32)


def _sigmoid(x):
    return 1.0 / (1.0 + jnp.exp(-x))


def _silu(x):
    return x * _sigmoid(x)


PROJ_TILE = 1024
MAIN_TILES = COL_CMP // PROJ_TILE


def _proj_kernel(x_ref, g_ref, w_ref, z_ref, cmp_ref, slc_ref, win_ref, gate_ref, xn_ref):
    j = pl.program_id(1)

    @pl.when(j == 0)
    def _():
        x = x_ref[...]
        y = x * lax.rsqrt(jnp.mean(x * x, axis=-1, keepdims=True) + EPS)
        xn_ref[...] = (y * g_ref[...]).astype(bf16)

    res = _dot(xn_ref[...], w_ref[...])

    @pl.when(j < MAIN_TILES)
    def _():
        z_ref[...] = res

    @pl.when(j == MAIN_TILES)
    def _():
        base = COL_CMP
        cmp_ref[...] = res[:, COL_CMP - base:COL_SLC - base]
        slc_ref[...] = res[:, COL_SLC - base:COL_WIN - base]
        win_ref[...] = res[:, COL_WIN - base:COL_GATE - base]
        gate_ref[...] = res[:, COL_GATE - base:COL_GATE - base + LANES]


def _proj(x, g, w, tm):
    n = x.shape[0]
    row = lambda i, j: (i, 0)
    return pl.pallas_call(
        _proj_kernel,
        grid=(n // tm, PROJ_WIDTH // PROJ_TILE),
        in_specs=[pl.BlockSpec((tm, D_MODEL), row),
                  pl.BlockSpec((1, D_MODEL), lambda i, j: (0, 0)),
                  pl.BlockSpec((D_MODEL, PROJ_TILE), lambda i, j: (0, j))],
        out_specs=[pl.BlockSpec((tm, PROJ_TILE), lambda i, j: (i, jnp.minimum(j, MAIN_TILES - 1))),
                   pl.BlockSpec((tm, ROW_WIDTH), row), pl.BlockSpec((tm, ROW_WIDTH), row),
                   pl.BlockSpec((tm, ROW_WIDTH), row), pl.BlockSpec((tm, LANES), row)],
        out_shape=[jax.ShapeDtypeStruct((n, COL_CMP), f32),
                   jax.ShapeDtypeStruct((n, ROW_WIDTH), f32), jax.ShapeDtypeStruct((n, ROW_WIDTH), f32),
                   jax.ShapeDtypeStruct((n, ROW_WIDTH), f32), jax.ShapeDtypeStruct((n, LANES), f32)],
        scratch_shapes=[pltpu.VMEM((tm, D_MODEL), bf16)],
        compiler_params=_cparams(("parallel", "arbitrary")),
        name="proj",
    )(x, g, w)


def _finish_kernel(x_ref, oa_ref, za_ref, yb_ref, p_ref, wo_ref, gp_ref, pp_ref, pg_ref, h_ref):
    ya = (oa_ref[...] * _silu(za_ref[...])).astype(bf16)
    yb = yb_ref[...].astype(bf16)
    mix = _dot(ya, wo_ref[0:NSA_WIDTH, :]) + _dot(yb, wo_ref[NSA_WIDTH:, :])
    nrm = mix * lax.rsqrt(jnp.mean(mix * mix, axis=-1, keepdims=True) + EPS) * gp_ref[...]
    h = x_ref[...] + nrm
    gate = _sigmoid(_dot(h.astype(bf16), pg_ref[...]))
    h_ref[...] = h + _dot(p_ref[...].astype(bf16), pp_ref[...]) * gate


def _finish(x, oa, z, yb, p, w_out, g_post, ple_proj, ple_gate, tm):
    n = x.shape[0]
    za_blk = COL_ZA // NSA_WIDTH
    row = lambda i: (i, 0)
    const = lambda i: (0, 0)
    return pl.pallas_call(
        _finish_kernel,
        grid=(n // tm,),
        in_specs=[pl.BlockSpec((tm, D_MODEL), row),
                  pl.BlockSpec((tm, NSA_WIDTH), row),
                  pl.BlockSpec((tm, NSA_WIDTH), lambda i: (i, za_blk)),
                  pl.BlockSpec((tm, HG_WIDTH), row),
                  pl.BlockSpec((tm, PLE_DIM), row),
                  pl.BlockSpec((D_MODEL, D_MODEL), const),
                  pl.BlockSpec((1, D_MODEL), const),
                  pl.BlockSpec((PLE_DIM, D_MODEL), const),
                  pl.BlockSpec((D_MODEL, D_MODEL), const)],
        out_specs=pl.BlockSpec((tm, D_MODEL), row),
        out_shape=jax.ShapeDtypeStruct((n, D_MODEL), f32),
        compiler_params=_cparams(("parallel",)),
        name="finish",
    )(x, oa, z, yb, p, w_out, g_post, ple_proj, ple_gate)


SEGS_PER_PAGE = PAGE_SIZE // CMP_STRIDE
ROW_WIDTH = 2 * KV_WIDTH
SEG_WIDTH = CMP_STRIDE * ROW_WIDTH
N_COMBO = 2 * NSA_KV_HEADS


def _compress_kernel(pt_ref, pages_ref, wa_ref, wb_ref, pea_ref, peb_ref, b1_ref, w2_ref, out_ref,
                     stage_ref, rows_ref, sem_ref, *, n_pages, feature_major):
    b = pl.program_id(0)
    nb = pl.num_programs(0)
    slot = lax.rem(b, 2)

    def page_copy(seq, s, j):
        return pltpu.make_async_copy(pages_ref.at[pt_ref[seq, j]], stage_ref.at[s, j], sem_ref.at[s])

    def start_all(seq, s):
        def body(j, c):
            page_copy(seq, s, j).start()
            return c
        lax.fori_loop(0, n_pages, body, 0)

    @pl.when(b == 0)
    def _():
        start_all(0, 0)

    @pl.when(b + 1 < nb)
    def _():
        start_all(b + 1, 1 - slot)

    def wait_body(j, c):
        page_copy(b, slot, j).wait()
        return c
    lax.fori_loop(0, n_pages, wait_body, 0)

    lane_halves = ROW_WIDTH // LANES

    def to_rows(j, c):
        tile = stage_ref[slot, j]
        rows = tile.T if feature_major else tile
        r0 = pl.multiple_of(j * PAGE_SIZE, PAGE_SIZE)
        for h in range(lane_halves):
            rows_ref[h, pl.ds(r0, PAGE_SIZE), :] = rows[:, h * LANES:(h + 1) * LANES]
        return c
    lax.fori_loop(0, n_pages, to_rows, 0, unroll=4)

    n_seg = n_pages * SEGS_PER_PAGE
    for kind in range(lane_halves):
        cols = slice(kind * LANES, (kind + 1) * LANES)
        seg_a, seg_b = [], []
        for j in range(CMP_STRIDE):
            r = rows_ref[kind, pl.ds(j, n_seg, stride=CMP_STRIDE), :]
            seg_a.append((r + pea_ref[j][:, cols]).astype(bf16))
            seg_b.append((r + peb_ref[j][:, cols]).astype(bf16))
        pa = _dot(jnp.concatenate(seg_a, axis=1), wa_ref[kind])
        pb = _dot(jnp.concatenate(seg_b, axis=1), wb_ref[kind])
        pb = pltpu.roll(pb, n_seg - 1, axis=0)
        hid = _silu(pa + pb + b1_ref[kind])
        for g in range(NSA_KV_HEADS):
            hg = hid[:, g * CMP_HIDDEN:(g + 1) * CMP_HIDDEN].astype(bf16)
            out_ref[0, kind * NSA_KV_HEADS + g] = _dot(hg, w2_ref[kind])


def _compress(pages, page_table, wa, wb, pea, peb, b1, w2, feature_major):
    nseq, n_pages = page_table.shape
    n_seg = n_pages * SEGS_PER_PAGE
    const2 = lambda b, pt: (0, 0)
    const3 = lambda b, pt: (0, 0, 0)
    once = pl.Buffered(1)
    grid_spec = pltpu.PrefetchScalarGridSpec(
        num_scalar_prefetch=1,
        grid=(nseq,),
        in_specs=[pl.BlockSpec(memory_space=pl.ANY),
                  pl.BlockSpec(wa.shape, const3, pipeline_mode=once),
                  pl.BlockSpec(wb.shape, const3, pipeline_mode=once),
                  pl.BlockSpec((CMP_STRIDE, 1, ROW_WIDTH), const3),
                  pl.BlockSpec((CMP_STRIDE, 1, ROW_WIDTH), const3),
                  pl.BlockSpec(b1.shape, const3),
                  pl.BlockSpec((2, CMP_HIDDEN, HEAD_DIM), const3)],
        out_specs=pl.BlockSpec((1, N_COMBO, n_seg, HEAD_DIM), lambda b, pt: (b, 0, 0, 0)),
        scratch_shapes=[pltpu.VMEM((2, n_pages) + pages.shape[1:], f32),
                        pltpu.VMEM((ROW_WIDTH // LANES, n_pages * PAGE_SIZE, LANES), f32),
                        pltpu.SemaphoreType.DMA((2,))],
    )
    return pl.pallas_call(
        functools.partial(_compress_kernel, n_pages=n_pages, feature_major=feature_major),
        grid_spec=grid_spec,
        out_shape=jax.ShapeDtypeStruct((nseq, N_COMBO, n_seg, HEAD_DIM), f32),
        compiler_params=_cparams(("arbitrary",)),
        name="compress",
    )(page_table, pages, wa, wb, pea, peb, b1, w2)


def _compress_weights(cmp_pe, cmp_w1, cmp_b1, cmp_w2):
    eye = jnp.eye(NSA_KV_HEADS, dtype=f32)

    def big(w_half):
        t = jnp.einsum('kjdh,gb->kjbdgh', w_half, eye)
        return t.reshape(2, CMP_STRIDE * KV_WIDTH, NSA_KV_HEADS * CMP_HIDDEN)

    w1 = cmp_w1.reshape(2, 2, CMP_STRIDE, HEAD_DIM, CMP_HIDDEN)
    wa = big(w1[:, 0]).astype(bf16)
    wb = big(w1[:, 1]).astype(bf16)
    pe = cmp_pe.reshape(2, 2, CMP_STRIDE, HEAD_DIM)

    def pe_row(p):
        t = jnp.broadcast_to(p.transpose(1, 0, 2)[:, :, None, :],
                             (CMP_STRIDE, 2, NSA_KV_HEADS, HEAD_DIM))
        return t.reshape(CMP_STRIDE, 1, ROW_WIDTH)

    pea, peb = pe_row(pe[:, 0]), pe_row(pe[:, 1])
    b1 = jnp.broadcast_to(cmp_b1[:, None, :], (2, NSA_KV_HEADS, CMP_HIDDEN)).reshape(2, 1, NSA_KV_HEADS * CMP_HIDDEN)
    return wa, wb, pea, peb, b1, cmp_w2.astype(bf16)


def _sel_map_matrix():
    wts = np.convolve(np.ones(SEL_RATIO), np.ones(CMP_RATIO))
    m = np.zeros((N_CMP_PAD, N_SELBLK), np.float32)
    for j in range(N_SELBLK):
        for k, wk in enumerate(wts):
            n = SEL_RATIO * j + k
            if n < N_CMP_PAD - 1:
                m[n, j] = wk
    return jnp.asarray(m, dtype=bf16)


def _expand_matrix(n_chunks):
    key = np.arange(n_chunks * SEL_CHUNK) // SEL_BLOCK
    e = (np.arange(N_SELBLK)[:, None] == key[None, :]).astype(np.float32)
    return jnp.asarray(e.reshape(N_SELBLK, n_chunks, SEL_CHUNK).transpose(1, 0, 2), dtype=bf16)


def _importance_to_blocks(imp, m):
    hi = imp.astype(bf16)
    lo = (imp - hi.astype(f32)).astype(bf16)
    return _dot(hi, m) + _dot(lo, m)


def _nsa_prompt_kernel(q_ref, gate_ref, kc_ref, vc_ref, kvs_ref, kvw_ref, m_ref, e_ref, o_ref,
                       acc_ref, mx_ref, used_ref):
    g = pl.program_id(1)
    qi = pl.program_id(2)
    q0 = qi * Q_BLOCK
    rows = NSA_GROUP * Q_BLOCK

    q64 = q_ref[0, 0, 0] * jnp.asarray(HEAD_DIM ** -0.5, bf16)
    q128 = jnp.concatenate([q64, jnp.zeros_like(q64)], axis=1)
    slope = [jnp.where(g == 0, 2.0 ** -(r + 1), 2.0 ** -(NSA_GROUP + r + 1)).astype(f32)
             for r in range(NSA_GROUP)]

    qrow = lax.broadcasted_iota(jnp.int32, (Q_BLOCK, 1), 0)

    def softmax_rows(s, dist, mask):
        es, ls = [], []
        for r in range(NSA_GROUP):
            sr = jnp.where(mask, s[r * Q_BLOCK:(r + 1) * Q_BLOCK] - slope[r] * dist, NEG)
            m = jnp.max(sr, axis=1, keepdims=True)
            e = jnp.where(mask, jnp.exp(sr - m), 0.0)
            es.append(e)
            ls.append(1.0 / jnp.maximum(jnp.sum(e, axis=1, keepdims=True), 1e-30))
        return es, ls

    kc = kc_ref[0, 0].astype(bf16)
    vc = vc_ref[0, 0].astype(bf16)
    ncol = lax.broadcasted_iota(jnp.int32, (1, N_CMP_PAD), 1)
    d_c = (q0 + qrow) - (ncol * CMP_STRIDE + (CMP_BLOCK - 1))
    es, ls = softmax_rows(_dot_nt(q64, kc), d_c.astype(f32), d_c >= 0)
    ps_c = [e * l for e, l in zip(es, ls)]
    o_c = _dot(jnp.concatenate(ps_c, axis=0).astype(bf16), vc)
    imp = ps_c[0] + ps_c[1] + ps_c[2] + ps_c[3]
    p_sel = _importance_to_blocks(imp, m_ref[...])

    ps_t = p_sel.T
    blk = lax.broadcasted_iota(jnp.int32, (N_SELBLK, Q_BLOCK), 0)
    tq = q0 + lax.broadcasted_iota(jnp.int32, (N_SELBLK, Q_BLOCK), 1)
    jt = jnp.right_shift(tq, SEL_BLOCK.bit_length() - 1)
    valid = blk <= jt
    forced = (blk == 0) | (blk == jt) | (blk == jt - 1)
    score = jnp.where(valid, ps_t + jnp.where(forced, FORCE_BONUS, 0.0), NEG)
    sel_t = jnp.zeros((N_SELBLK, Q_BLOCK), f32)
    for _ in range(N_SEL):
        best = jnp.max(score, axis=0, keepdims=True)
        idx = jnp.min(jnp.where(score == best, blk, N_SELBLK), axis=0, keepdims=True)
        hit = blk == idx
        sel_t = jnp.where(hit & (best > NEG * 0.5), 1.0, sel_t)
        score = jnp.where(hit, REMOVED, score)
    sel = sel_t.T.astype(bf16)

    acc_ref[...] = jnp.zeros_like(acc_ref)
    mx_ref[...] = jnp.full_like(mx_ref, NEG)
    kcol = lax.broadcasted_iota(jnp.int32, (1, SEL_CHUNK), 1)
    d_base = (qrow - kcol).astype(f32)

    blk_per_chunk = SEL_CHUNK // SEL_BLOCK
    for c in range(e_ref.shape[0]):
        picked = jnp.max(sel_t[c * blk_per_chunk:(c + 1) * blk_per_chunk, :])
        used_ref[c] = (picked > 0.5).astype(jnp.int32)

    def sel_chunk(c, carry):
        @pl.when(used_ref[c] > 0)
        def _():
            sel_chunk_body(c)
        return carry

    def sel_chunk_body(c):
        k0 = pl.multiple_of(c * SEL_CHUNK, SEL_CHUNK)
        kv = kvs_ref[0, 0, pl.ds(k0, SEL_CHUNK), :]
        s = _dot_nt(q128, kv[:, 0:LANES])
        dist = d_base + (q0 - k0).astype(f32)
        allowed = (_dot(sel, e_ref[c]) > 0.5) & (dist >= 0.0)
        ps = []
        for r in range(NSA_GROUP):
            rs = slice(r * Q_BLOCK, (r + 1) * Q_BLOCK)
            sr = jnp.where(allowed, s[rs] - slope[r] * dist, NEG)
            m_old = mx_ref[rs]
            m_new = jnp.maximum(m_old, jnp.max(sr, axis=1, keepdims=True))
            acc_ref[rs] = acc_ref[rs] * jnp.exp(m_old - m_new)
            mx_ref[rs] = m_new
            ps.append(jnp.exp(sr - m_new).astype(bf16))
        acc_ref[...] += _dot(jnp.concatenate(ps, axis=0), kv[:, LANES:])

    n_chunks = (q0 + Q_BLOCK + SEL_CHUNK - 1) // SEL_CHUNK
    lax.fori_loop(0, n_chunks, sel_chunk, 0)
    acc = acc_ref[...]
    o_s = acc[:, 0:HEAD_DIM] * (1.0 / jnp.maximum(acc[:, HEAD_DIM:HEAD_DIM + 1], 1e-30))

    w0 = pl.multiple_of(jnp.maximum(q0 - WINDOW, 0), Q_BLOCK)
    kvw = kvw_ref[0, 0, pl.ds(w0, WIN_KEYS), :]
    wcol = lax.broadcasted_iota(jnp.int32, (1, WIN_KEYS), 1)
    d_w = (q0 - w0) + qrow - wcol
    es, _ = softmax_rows(_dot_nt(q128, kvw[:, 0:LANES]), d_w.astype(f32), (d_w >= 0) & (d_w <= WINDOW))
    acc_w = _dot(jnp.concatenate(es, axis=0).astype(bf16), kvw[:, LANES:])
    o_w = acc_w[:, 0:HEAD_DIM] * (1.0 / jnp.maximum(acc_w[:, HEAD_DIM:HEAD_DIM + 1], 1e-30))

    gate = _sigmoid(gate_ref[...])

    def gate_col(r, j):
        lo = gate[:, r * 3 + j:r * 3 + j + 1]
        hi = gate[:, (NSA_GROUP + r) * 3 + j:(NSA_GROUP + r) * 3 + j + 1]
        return jnp.where(g == 0, lo, hi)

    for r in range(NSA_GROUP):
        rs = slice(r * Q_BLOCK, (r + 1) * Q_BLOCK)
        o_ref[0, 0, 0, rs, :] = gate_col(r, 0) * o_c[rs] + gate_col(r, 1) * o_s[rs] + gate_col(r, 2) * o_w[rs]


def _nsa_prompt(q_st, z, comp, kv_slc, kv_win, m_mat, e_mat):
    bsz, nq = q_st.shape[0], q_st.shape[1]
    seq = kv_slc.shape[2]
    rows = NSA_GROUP * Q_BLOCK
    gate_blk = COL_GATE // LANES
    return pl.pallas_call(
        _nsa_prompt_kernel,
        grid=(bsz, NSA_KV_HEADS, nq),
        in_specs=[pl.BlockSpec((1, 1, 1, rows, HEAD_DIM), lambda b, g, i: (b, i, g, 0, 0)),
                  pl.BlockSpec((Q_BLOCK, LANES), lambda b, g, i: (b * nq + i, gate_blk)),
                  pl.BlockSpec((1, 1, N_CMP_PAD, HEAD_DIM), lambda b, g, i: (b, g, 0, 0)),
                  pl.BlockSpec((1, 1, N_CMP_PAD, HEAD_DIM), lambda b, g, i: (b, NSA_KV_HEADS + g, 0, 0)),
                  pl.BlockSpec((1, 1, seq, 2 * LANES), lambda b, g, i: (b, g, 0, 0)),
                  pl.BlockSpec((1, 1, seq, 2 * LANES), lambda b, g, i: (b, g, 0, 0)),
                  pl.BlockSpec((N_CMP_PAD, N_SELBLK), lambda b, g, i: (0, 0)),
                  pl.BlockSpec(e_mat.shape, lambda b, g, i: (0, 0, 0))],
        out_specs=pl.BlockSpec((1, 1, 1, rows, HEAD_DIM), lambda b, g, i: (b, i, g, 0, 0)),
        out_shape=jax.ShapeDtypeStruct((bsz, nq, NSA_KV_HEADS, rows, HEAD_DIM), f32),
        scratch_shapes=[pltpu.VMEM((rows, LANES), f32), pltpu.VMEM((rows, 1), f32),
                        pltpu.SMEM((e_mat.shape[0],), jnp.int32)],
        compiler_params=_cparams(("parallel", "parallel", "arbitrary")),
        name="nsa_prompt",
    )(q_st, z, comp, comp, kv_slc, kv_win, m_mat, e_mat)


def _attention_kv(piece, bsz, seq):
    t = piece.reshape(bsz, seq, 2, NSA_KV_HEADS, HEAD_DIM).transpose(2, 0, 3, 1, 4)
    zeros = jnp.zeros_like(t[0])
    ones = jnp.ones(t[0].shape[:-1] + (1,), f32)
    out = jnp.concatenate([t[0], zeros, t[1], ones, zeros[..., :HEAD_DIM - 1]], axis=-1)
    return out.astype(bf16)


LOG2E = 1.4426950408889634
MASK_BIG = 1e30
AUG_ROWS = 16
VT_ROWS = HEAD_DIM + 16
KEY_LOW = 256
N_FORCED = 3


def _nsa_prompt_t_kernel(qt_ref, gate_ref, kc_ref, vc_ref, ks_ref, vts_ref, kw_ref, vtw_ref, mt_ref, o_ref,
                         qa_ref, acc_ref, mg_ref, bias_ref, used_ref):
    g = pl.program_id(1)
    qi = pl.program_id(2)
    q0 = qi * Q_BLOCK
    cols = NSA_GROUP * Q_BLOCK

    lane = lax.broadcasted_iota(jnp.int32, (1, cols), 1)
    head = jnp.right_shift(lane, Q_BLOCK.bit_length() - 1)
    tq = q0 + jnp.bitwise_and(lane, Q_BLOCK - 1)
    c = jnp.exp2(-(head.astype(f32) + 1.0 + NSA_GROUP * g.astype(f32))) * LOG2E

    qt = (qt_ref[0, 0, 0] * (HEAD_DIM ** -0.5 * LOG2E)).astype(bf16)

    def softmax_cols(s, mask):
        s = jnp.where(mask, s, NEG)
        m = jnp.max(s, axis=0, keepdims=True)
        e = jnp.where(mask, jnp.exp2(s - m), 0.0)
        return e, 1.0 / jnp.maximum(jnp.sum(e, axis=0, keepdims=True), 1e-30)

    kc = kc_ref[0, 0].astype(bf16)
    vc = vc_ref[0, 0].astype(bf16)
    nrow = lax.broadcasted_iota(jnp.int32, (N_CMP_PAD, 1), 0)
    d_c = tq - (nrow * CMP_STRIDE + (CMP_BLOCK - 1))
    e, rl = softmax_cols(_dot(kc, qt) - c * d_c.astype(f32), d_c >= 0)
    p_c = e * rl
    oc_t = _dot_tn(vc, p_c.astype(bf16))
    imp_t = (p_c[:, 0:Q_BLOCK] + p_c[:, Q_BLOCK:2 * Q_BLOCK]
             + p_c[:, 2 * Q_BLOCK:3 * Q_BLOCK] + p_c[:, 3 * Q_BLOCK:4 * Q_BLOCK])
    hi = imp_t.astype(bf16)
    lo = (imp_t - hi.astype(f32)).astype(bf16)
    ps_t = _dot(mt_ref[...], hi) + _dot(mt_ref[...], lo)

    blk = lax.broadcasted_iota(jnp.int32, (N_SELBLK, Q_BLOCK), 0)
    jt = jnp.right_shift(q0 + lax.broadcasted_iota(jnp.int32, (N_SELBLK, Q_BLOCK), 1), SEL_BLOCK.bit_length() - 1)
    valid = blk <= jt
    forced = (blk == 0) | (blk == jt) | (blk == jt - 1)
    sel_t = jnp.where(valid & forced, 1.0, 0.0)
    score = jnp.where(valid & jnp.logical_not(forced), ps_t, NEG)
    for _ in range(N_SEL - N_FORCED):
        best = jnp.max(score, axis=0, keepdims=True)
        idx = jnp.min(jnp.where(score == best, blk, N_SELBLK), axis=0, keepdims=True)
        hit = blk == idx
        sel_t = jnp.where(hit & (best > NEG * 0.5), 1.0, sel_t)
        score = jnp.where(hit, REMOVED, score)
    bias_ref[...] = (sel_t - 1.0) * MASK_BIG

    blk_per_chunk = SEL_CHUNK // SEL_BLOCK
    n_chunk_max = vts_ref.shape[2]
    for ck in range(n_chunk_max):
        picked = jnp.max(sel_t[ck * blk_per_chunk:(ck + 1) * blk_per_chunk, :])
        used_ref[ck] = (picked > 0.5).astype(jnp.int32)

    c_hi = c.astype(bf16).astype(f32)
    c_mid = (c - c_hi).astype(bf16).astype(f32)
    c_lo = (c - c_hi - c_mid).astype(bf16).astype(f32)
    arow = lax.broadcasted_iota(jnp.int32, (AUG_ROWS // 2, cols), 0)
    alibi_rows = jnp.where(arow >= 6, 0.0,
                           jnp.where((arow == 0) | (arow == 3), c_hi, jnp.where((arow == 1) | (arow == 4), c_mid, c_lo)))
    qa_ref[0:HEAD_DIM, :] = qt
    qa_ref[HEAD_DIM + AUG_ROWS:, :] = jnp.zeros((qa_ref.shape[0] - HEAD_DIM - AUG_ROWS, cols), bf16)
    acc_ref[...] = jnp.zeros_like(acc_ref)
    mg_ref[...] = jnp.full_like(mg_ref, NEG)
    krow = lax.broadcasted_iota(jnp.int32, (SEL_CHUNK, 1), 0)

    def sel_chunk(ck, causal):
        k0 = pl.multiple_of(ck * SEL_CHUNK, SEL_CHUNK)
        b8 = bias_ref[pl.ds(pl.multiple_of(ck * blk_per_chunk, blk_per_chunk), blk_per_chunk), :]
        qa_ref[HEAD_DIM:HEAD_DIM + AUG_ROWS, :] = jnp.concatenate(
            [jnp.concatenate([b8] * NSA_GROUP, axis=1), alibi_rows], axis=0).astype(bf16)
        s = _dot(ks_ref[0, 0, pl.ds(k0, SEL_CHUNK), :], qa_ref[...])
        if causal:
            s = jnp.where(k0 + krow <= tq, s, NEG)
        shift = c * k0.astype(f32)
        m_old = mg_ref[...] - shift
        m_new = jnp.maximum(m_old, jnp.max(s, axis=0, keepdims=True))
        p = jnp.exp2(s - m_new).astype(bf16)
        acc_ref[...] = acc_ref[...] * jnp.exp2(m_old - m_new) + _dot(vts_ref[0, 0, ck], p)
        mg_ref[...] = m_new + shift

    def past_chunk(ck, carry):
        @pl.when(used_ref[ck] > 0)
        def _():
            sel_chunk(ck, causal=False)
        return carry

    last = q0 // SEL_CHUNK
    sel_chunk(last, causal=True)

    w0 = pl.multiple_of(jnp.maximum(q0 - WINDOW, 0), Q_BLOCK)
    kw = kw_ref[0, 0, pl.ds(w0, WIN_KEYS), :]
    q_pad = jnp.concatenate([qt, jnp.zeros((LANES - HEAD_DIM, cols), bf16)], axis=0)
    wrow = lax.broadcasted_iota(jnp.int32, (WIN_KEYS, 1), 0)
    d_w = tq - (w0 + wrow)
    e, _ = softmax_cols(_dot(kw, q_pad) - c * d_w.astype(f32), (d_w >= 0) & (d_w <= WINDOW))
    e = e.astype(bf16)
    t0 = w0 // Q_BLOCK
    acc_w = jnp.zeros((VT_ROWS, cols), f32)
    for t in range(WIN_KEYS // Q_BLOCK):
        acc_w = acc_w + _dot(vtw_ref[0, 0, t0 + t], e[t * Q_BLOCK:(t + 1) * Q_BLOCK])
    ow_t = acc_w[0:HEAD_DIM] * (1.0 / jnp.maximum(acc_w[HEAD_DIM:HEAD_DIM + 1], 1e-30))

    lax.fori_loop(0, last, past_chunk, 0)
    acc = acc_ref[...]
    os_t = acc[0:HEAD_DIM] * (1.0 / jnp.maximum(acc[HEAD_DIM:HEAD_DIM + 1], 1e-30))

    gate_t = _sigmoid(gate_ref[...]).T

    def gate_row(r, j):
        lo_row = gate_t[r * 3 + j:r * 3 + j + 1, :]
        hi_row = gate_t[(NSA_GROUP + r) * 3 + j:(NSA_GROUP + r) * 3 + j + 1, :]
        return jnp.where(g == 0, lo_row, hi_row)

    outs = []
    for r in range(NSA_GROUP):
        cs = slice(r * Q_BLOCK, (r + 1) * Q_BLOCK)
        outs.append(gate_row(r, 0) * oc_t[:, cs] + gate_row(r, 1) * os_t[:, cs] + gate_row(r, 2) * ow_t[:, cs])
    for pair in range(NSA_GROUP // 2):
        both = jnp.concatenate([outs[2 * pair], outs[2 * pair + 1]], axis=0)
        o_ref[:, pair * LANES:(pair + 1) * LANES] = both.T


def _nsa_prompt_t(q_t, z, comp, k_slc, vt_slc, k_win, vt_win, m_t):
    bsz, _, nq = q_t.shape[:3]
    seq = k_slc.shape[2]
    cols = NSA_GROUP * Q_BLOCK
    gate_blk = COL_GATE // LANES
    per_bg = lambda b, g, i: (b, g, 0, 0)
    per_bg5 = lambda b, g, i: (b, g, 0, 0, 0)
    return pl.pallas_call(
        _nsa_prompt_t_kernel,
        grid=(bsz, NSA_KV_HEADS, nq),
        in_specs=[pl.BlockSpec((1, 1, 1, HEAD_DIM, cols), lambda b, g, i: (b, g, i, 0, 0)),
                  pl.BlockSpec((Q_BLOCK, LANES), lambda b, g, i: (b * nq + i, gate_blk)),
                  pl.BlockSpec((1, 1, N_CMP_PAD, HEAD_DIM), per_bg),
                  pl.BlockSpec((1, 1, N_CMP_PAD, HEAD_DIM), lambda b, g, i: (b, NSA_KV_HEADS + g, 0, 0)),
                  pl.BlockSpec((1, 1, seq, LANES), per_bg),
                  pl.BlockSpec((1, 1) + vt_slc.shape[2:], per_bg5),
                  pl.BlockSpec((1, 1, seq, LANES), per_bg),
                  pl.BlockSpec((1, 1) + vt_win.shape[2:], per_bg5),
                  pl.BlockSpec((N_SELBLK, N_CMP_PAD), lambda b, g, i: (0, 0))],
        out_specs=pl.BlockSpec((Q_BLOCK, NSA_GROUP * HEAD_DIM), lambda b, g, i: (b * nq + i, g)),
        out_shape=jax.ShapeDtypeStruct((bsz * seq, NSA_WIDTH), f32),
        scratch_shapes=[pltpu.VMEM((LANES, cols), bf16),
                        pltpu.VMEM((VT_ROWS, cols), f32),
                        pltpu.VMEM((1, cols), f32),
                        pltpu.VMEM((N_SELBLK, Q_BLOCK), f32),
                        pltpu.SMEM((vt_slc.shape[2],), jnp.int32)],
        compiler_params=_cparams(("parallel", "parallel", "arbitrary")),
        name="nsa_prompt",
    )(q_t, z, comp, comp, k_slc, vt_slc, k_win, vt_win, m_t)


def _nsa_prompt_pair_kernel(qt_ref, gate_ref, comp_ref, caug_ref, ks_ref, vts_ref, kw_ref, vtw_ref, mt_ref, o_ref,
                            qa_ref, acc_ref, mg_ref, bias_ref, need_ref):
    qi = pl.program_id(1)
    q0 = qi * Q_BLOCK
    cols = NSA_GROUP * Q_BLOCK
    heads = range(NSA_KV_HEADS)

    lane = lax.broadcasted_iota(jnp.int32, (1, cols), 1)
    head = jnp.right_shift(lane, Q_BLOCK.bit_length() - 1)
    tq = q0 + jnp.bitwise_and(lane, Q_BLOCK - 1)
    blk = lax.broadcasted_iota(jnp.int32, (N_SELBLK, Q_BLOCK), 0)
    jt = jnp.right_shift(q0 + lax.broadcasted_iota(jnp.int32, (N_SELBLK, Q_BLOCK), 1), SEL_BLOCK.bit_length() - 1)
    valid = blk <= jt
    forced = (blk == 0) | (blk == jt) | (blk == jt - 1)
    c_end = lax.broadcasted_iota(jnp.int32, (N_CMP_PAD, 1), 0) * CMP_STRIDE + (CMP_BLOCK - 1)
    krow = lax.broadcasted_iota(jnp.int32, (SEL_CHUNK, 1), 0)
    arow = lax.broadcasted_iota(jnp.int32, (AUG_ROWS // 2, cols), 0)
    blk_per_chunk = SEL_CHUNK // SEL_BLOCK
    n_chunk_max = vts_ref.shape[2]
    last = q0 // SEL_CHUNK
    w0 = pl.multiple_of(jnp.maximum(q0 - WINDOW, 0), Q_BLOCK)
    wkey = w0 + lax.broadcasted_iota(jnp.int32, (WIN_KEYS, 1), 0)
    in_window = (wkey <= tq) & (wkey >= tq - WINDOW)
    gate_t = _sigmoid(gate_ref[...]).T

    def softmax_cols(s, mask):
        s = jnp.where(mask, s, NEG)
        m = jnp.max(s, axis=0, keepdims=True)
        e = jnp.where(mask, jnp.exp2(s - m), 0.0)
        return e, 1.0 / jnp.maximum(jnp.sum(e, axis=0, keepdims=True), 1e-30)

    slope = [jnp.exp2(-(head.astype(f32) + 1.0 + NSA_GROUP * g)) * LOG2E for g in heads]
    qts = [(qt_ref[0, g, 0] * (HEAD_DIM ** -0.5 * LOG2E)).astype(bf16) for g in heads]

    def chunk_scores(g, ck, slot=0, live=None):
        k0 = pl.multiple_of(ck * SEL_CHUNK, SEL_CHUNK)
        b8 = bias_ref[g, pl.ds(pl.multiple_of(ck * blk_per_chunk, blk_per_chunk), blk_per_chunk), :]
        if live is not None:
            b8 = jnp.where(live, b8, -MASK_BIG)
        qa_ref[g, slot, HEAD_DIM:HEAD_DIM + AUG_ROWS, :] = jnp.concatenate(
            [jnp.concatenate([b8] * NSA_GROUP, axis=1), alibi_rows[g]], axis=0).astype(bf16)
        return _dot(ks_ref[0, g, pl.ds(k0, SEL_CHUNK), :], qa_ref[g, slot])

    def sel_chunk(g, ck, causal, s=None):
        c = slope[g]
        k0 = pl.multiple_of(ck * SEL_CHUNK, SEL_CHUNK)
        if s is None:
            s = chunk_scores(g, ck)
        if causal:
            s = jnp.where(k0 + krow <= tq, s, NEG)
        shift = c * k0.astype(f32)
        m_old = mg_ref[g] - shift
        m_new = jnp.maximum(m_old, jnp.max(s, axis=0, keepdims=True))
        p = jnp.exp2(s - m_new).astype(bf16)
        acc_ref[g] = acc_ref[g] * jnp.exp2(m_old - m_new) + _dot(vts_ref[0, g, ck], p)
        mg_ref[g] = m_new + shift

    alibi_rows, s_cmp, s_win = [], [], []
    for g in heads:
        c, qt = slope[g], qts[g]
        c_hi = c.astype(bf16).astype(f32)
        c_mid = (c - c_hi).astype(bf16).astype(f32)
        c_lo = (c - c_hi - c_mid).astype(bf16).astype(f32)
        c3 = jnp.where((arow == 0) | (arow == 3), c_hi, jnp.where((arow == 1) | (arow == 4), c_mid, c_lo))
        alibi_rows.append(jnp.where(arow >= 6, 0.0, c3))
        pos_rows = jnp.where(arow >= 6, 0.0, jnp.where(arow >= 3, KEY_LOW * c3, c3))
        pos_rows = jnp.concatenate([pos_rows, jnp.zeros_like(pos_rows)], axis=0).astype(bf16)
        q_pos = jnp.concatenate([qt, pos_rows, jnp.zeros((LANES - HEAD_DIM - AUG_ROWS, cols), bf16)], axis=0)
        kc = jnp.concatenate([comp_ref[0, g].astype(bf16), caug_ref[...]], axis=1)
        s_cmp.append(_dot(kc, q_pos))
        kw = kw_ref[0, g, pl.ds(w0, WIN_KEYS), :]
        s_win.append(_dot(kw, q_pos))

    oc_t, ps_t = [], []
    for g in heads:
        e, rl = softmax_cols(s_cmp[g], c_end <= tq)
        p_c = e * rl
        oc_t.append(_dot_tn(comp_ref[0, NSA_KV_HEADS + g].astype(bf16), p_c.astype(bf16)))
        imp_t = (p_c[:, 0:Q_BLOCK] + p_c[:, Q_BLOCK:2 * Q_BLOCK]
                 + p_c[:, 2 * Q_BLOCK:3 * Q_BLOCK] + p_c[:, 3 * Q_BLOCK:4 * Q_BLOCK])
        hi = imp_t.astype(bf16)
        lo = (imp_t - hi.astype(f32)).astype(bf16)
        ps_t.append(_dot(mt_ref[...], hi) + _dot(mt_ref[...], lo))

    ow_t = []
    t0 = w0 // Q_BLOCK
    for g in heads:
        s_w = jnp.where(in_window, s_win[g], NEG)
        e = jnp.exp2(s_w - jnp.max(s_w, axis=0, keepdims=True)).astype(bf16)
        acc_w = jnp.zeros((VT_ROWS, cols), f32)
        for t in range(WIN_KEYS // Q_BLOCK):
            acc_w = acc_w + _dot(vtw_ref[0, g, t0 + t], e[t * Q_BLOCK:(t + 1) * Q_BLOCK])
        ow_t.append(acc_w[0:HEAD_DIM] * (1.0 / jnp.maximum(acc_w[HEAD_DIM:HEAD_DIM + 1], 1e-30)))

    sel_t = [jnp.where(valid & forced, 1.0, 0.0)] * NSA_KV_HEADS
    score = [jnp.where(valid & jnp.logical_not(forced), ps_t[g], NEG) for g in heads]
    for _ in range(N_SEL - N_FORCED):
        for g in heads:
            best = jnp.max(score[g], axis=0, keepdims=True)
            idx = jnp.min(jnp.where(score[g] == best, blk, N_SELBLK), axis=0, keepdims=True)
            hit = blk == idx
            sel_t[g] = jnp.where(hit & (best > NEG * 0.5), 1.0, sel_t[g])
            score[g] = jnp.where(hit, REMOVED, score[g])

    for g in heads:
        bias_ref[g] = (sel_t[g] - 1.0) * MASK_BIG
        for slot in range(qa_ref.shape[1]):
            qa_ref[g, slot, 0:HEAD_DIM, :] = qts[g]
            qa_ref[g, slot, HEAD_DIM + AUG_ROWS:, :] = jnp.zeros((LANES - HEAD_DIM - AUG_ROWS, cols), bf16)
        acc_ref[g] = jnp.zeros(acc_ref.shape[1:], f32)
        mg_ref[g] = jnp.full(mg_ref.shape[1:], NEG, f32)
    own = [chunk_scores(g, last) for g in heads]
    for g in heads:
        sel_chunk(g, last, causal=True, s=own[g])

    n_need = jnp.int32(0)
    for ck in range(n_chunk_max):
        picked = jnp.maximum(*[jnp.max(sel_t[g][ck * blk_per_chunk:(ck + 1) * blk_per_chunk, :]) for g in heads])
        need_ref[n_need] = ck
        n_need = n_need + ((picked > 0.5) & (ck < last)).astype(jnp.int32)

    def chunk_pair(i, carry):
        first = 2 * i
        live = first + 1 < n_need
        cks = [need_ref[first], need_ref[jnp.minimum(first + 1, n_need - 1)]]
        scores = [[chunk_scores(g, cks[slot], slot, live if slot else None) for g in heads] for slot in range(2)]
        for slot in range(2):
            for g in heads:
                sel_chunk(g, cks[slot], causal=False, s=scores[slot][g])
        return carry

    lax.fori_loop(0, (n_need + 1) // 2, chunk_pair, 0)

    for g in heads:
        acc = acc_ref[g]
        os_t = acc[0:HEAD_DIM] * (1.0 / jnp.maximum(acc[HEAD_DIM:HEAD_DIM + 1], 1e-30))
        outs = []
        for r in range(NSA_GROUP):
            cs = slice(r * Q_BLOCK, (r + 1) * Q_BLOCK)
            row = (g * NSA_GROUP + r) * 3
            outs.append(gate_t[row:row + 1, :] * oc_t[g][:, cs] + gate_t[row + 1:row + 2, :] * os_t[:, cs]
                        + gate_t[row + 2:row + 3, :] * ow_t[g][:, cs])
        for pair in range(NSA_GROUP // 2):
            both = jnp.concatenate([outs[2 * pair], outs[2 * pair + 1]], axis=0)
            lane0 = (g * NSA_GROUP // 2 + pair) * LANES
            o_ref[:, lane0:lane0 + LANES] = both.T


def _nsa_prompt_pair(q_t, z, comp, k_slc, vt_slc, k_win, vt_win, m_t):
    bsz, _, nq = q_t.shape[:3]
    seq = k_slc.shape[2]
    cols = NSA_GROUP * Q_BLOCK
    gate_blk = COL_GATE // LANES
    per_b4 = lambda b, i: (b, 0, 0, 0)
    per_b5 = lambda b, i: (b, 0, 0, 0, 0)
    return pl.pallas_call(
        _nsa_prompt_pair_kernel,
        grid=(bsz, nq),
        in_specs=[pl.BlockSpec((1, NSA_KV_HEADS, 1, HEAD_DIM, cols), lambda b, i: (b, 0, i, 0, 0)),
                  pl.BlockSpec((Q_BLOCK, LANES), lambda b, i: (b * nq + i, 0)),
                  pl.BlockSpec((1, N_COMBO, N_CMP_PAD, HEAD_DIM), per_b4),
                  pl.BlockSpec((N_CMP_PAD, HEAD_DIM), lambda b, i: (0, 0)),
                  pl.BlockSpec((1, NSA_KV_HEADS, seq, LANES), per_b4),
                  pl.BlockSpec((1,) + vt_slc.shape[1:], per_b5),
                  pl.BlockSpec((1, NSA_KV_HEADS, seq, LANES), per_b4),
                  pl.BlockSpec((1,) + vt_win.shape[1:], per_b5),
                  pl.BlockSpec((N_SELBLK, N_CMP_PAD), lambda b, i: (0, 0))],
        out_specs=pl.BlockSpec((Q_BLOCK, NSA_WIDTH), lambda b, i: (b * nq + i, 0)),
        out_shape=jax.ShapeDtypeStruct((bsz * seq, NSA_WIDTH), f32),
        scratch_shapes=[pltpu.VMEM((NSA_KV_HEADS, 2, LANES, cols), bf16),
                        pltpu.VMEM((NSA_KV_HEADS, VT_ROWS, cols), f32),
                        pltpu.VMEM((NSA_KV_HEADS, 1, cols), f32),
                        pltpu.VMEM((NSA_KV_HEADS, N_SELBLK, Q_BLOCK), f32),
                        pltpu.SMEM((vt_slc.shape[2],), jnp.int32)],
        compiler_params=_cparams(("parallel", "arbitrary")),
        name="nsa_prompt",
    )(q_t, z, comp, _cmp_position_columns(), k_slc, vt_slc, k_win, vt_win, m_t)


def _position_columns(pos):
    low, high = (pos % KEY_LOW)[:, None], (pos // KEY_LOW)[:, None]
    return np.concatenate([low, low, low, high, high, high], axis=1).astype(np.float32)


def _cmp_position_columns():
    aug = np.zeros((N_CMP_PAD, HEAD_DIM), np.float32)
    aug[:, 0:6] = _position_columns(np.arange(N_CMP_PAD) * CMP_STRIDE + CMP_BLOCK - 1)
    return jnp.asarray(aug, dtype=bf16)


def _attention_keys(piece, bsz, seq, augment):
    tile = SEL_CHUNK if augment else Q_BLOCK
    t = piece.reshape(bsz, seq, 2, NSA_KV_HEADS, HEAD_DIM)
    k = t[:, :, 0].transpose(0, 2, 1, 3)
    pos = np.arange(seq)
    extra = np.zeros((seq, LANES - HEAD_DIM), np.float32)
    if augment:
        extra[np.arange(seq), (pos // SEL_BLOCK) % (SEL_CHUNK // SEL_BLOCK)] = 1.0
        extra[:, 8:11] = (pos % KEY_LOW)[:, None]
        extra[:, 11:14] = (pos % SEL_CHUNK - pos % KEY_LOW)[:, None]
    else:
        extra[:, 0:6] = _position_columns(pos)
    k_aug = jnp.concatenate([k, jnp.broadcast_to(jnp.asarray(extra), k.shape[:2] + extra.shape)], axis=-1)
    v_t = t[:, :, 1].transpose(0, 2, 3, 1)
    ones = jnp.ones(v_t.shape[:2] + (1, seq), f32)
    zeros = jnp.zeros(v_t.shape[:2] + (VT_ROWS - HEAD_DIM - 1, seq), f32)
    v_aug = jnp.concatenate([v_t, ones, zeros], axis=2)
    v_aug = v_aug.reshape(bsz, NSA_KV_HEADS, VT_ROWS, seq // tile, tile).transpose(0, 1, 3, 2, 4)
    return k_aug.astype(bf16), v_aug.astype(bf16)


HG_CHUNK = 128
HG_BAND = 8


def _lower_bound(lo):
    m = jnp.max(lo, axis=0, keepdims=True)
    e = jnp.exp(lo - m)
    return e[0:1] / jnp.sum(e, axis=0, keepdims=True)


def _hgrn_gates(q, fl, lb):
    sg = _sigmoid(fl)
    f = lb + (1.0 - lb) * sg
    k = (1.0 - lb) * (1.0 - sg)
    return _silu(q), k, jnp.log(f)


def _hgrn_out(o, nw, zb):
    o = o * lax.rsqrt(jnp.mean(o * o, axis=-1, keepdims=True) + EPS) * nw
    return o * _silu(zb)


def _hgrn_prompt_kernel(q_ref, f_ref, i_ref, zb_ref, lo_ref, nw_ref, y_ref, s_ref, st_ref):
    c = pl.program_id(1)

    @pl.when(c == 0)
    def _():
        st_ref[...] = jnp.zeros_like(st_ref)

    n = HG_CHUNK
    row = lax.broadcasted_iota(jnp.int32, (n, n), 0)
    col = lax.broadcasted_iota(jnp.int32, (n, n), 1)
    heads = [slice(h * HG_DIM, (h + 1) * HG_DIM) for h in range(HG_HEADS)]

    qq, kk, logf = _hgrn_gates(q_ref[...], f_ref[...], _lower_bound(lo_ref[...]))
    b = logf
    trow = lax.broadcasted_iota(jnp.int32, (n, 1), 0)
    step = 1
    while step < n:
        b = b + jnp.where(trow >= step, pltpu.roll(b, step, axis=0), 0.0)
        step *= 2

    intra = [_hgrn_intra(qq[:, hs], kk[:, hs], b[:, hs], row, col) for hs in heads]
    v = i_ref[...]
    for h, hs in enumerate(heads):
        st = st_ref[h]
        bh, kh, vh = b[:, hs], kk[:, hs], v[:, hs].astype(bf16)
        o = _dot(intra[h].astype(bf16), vh) + _dot_nt((qq[:, hs] * jnp.exp(bh)).astype(bf16), st.astype(bf16))
        b_last = bh[n - 1:n]
        st_new = st * jnp.exp(b_last) + _dot_tn(vh, (kh * jnp.exp(b_last - bh)).astype(bf16))
        st_ref[h] = st_new
        y_ref[:, hs] = _hgrn_out(o, nw_ref[...], zb_ref[:, hs])

        @pl.when(c == pl.num_programs(1) - 1)
        def _():
            s_ref[0, h] = st_new.T


def _hgrn_intra(qq, kk, b, row, col):
    n = HG_CHUNK
    a = jnp.zeros((n, n), f32)
    sub = jnp.bitwise_and(row, HG_BAND - 1)
    for d in range(HG_BAND):
        if d == 0:
            term = qq * kk
        else:
            ok = sub >= d
            decay = jnp.exp(jnp.where(ok, b - pltpu.roll(b, d, axis=0), 0.0))
            term = jnp.where(ok, qq * pltpu.roll(kk, d, axis=0) * decay, 0.0)
        a = a + jnp.where(col == row - d, jnp.sum(term, axis=1, keepdims=True), 0.0)

    size = 2 * HG_BAND
    while size <= n:
        half = size // 2
        qs, ks = [], []
        zero = jnp.zeros((half, HG_DIM), f32)
        for r0 in range(0, n, size):
            beta = b[r0 + half - 1:r0 + half]
            ks += [kk[r0:r0 + half] * jnp.exp(beta - b[r0:r0 + half]), zero]
            qs += [zero, qq[r0 + half:r0 + size] * jnp.exp(b[r0 + half:r0 + size] - beta)]
        part = _dot_nt(jnp.concatenate(qs, axis=0).astype(bf16), jnp.concatenate(ks, axis=0).astype(bf16))
        if size < n:
            sh = size.bit_length() - 1
            part = jnp.where(jnp.right_shift(row, sh) == jnp.right_shift(col, sh), part, 0.0)
        a = a + part
        size *= 2
    return a


def _hgrn_prompt(z, hg_lower, hg_norm, bsz, seq):
    nch = seq // HG_CHUNK

    def zcol(base):
        return pl.BlockSpec((HG_CHUNK, HG_WIDTH), lambda b, c: (b * nch + c, base // HG_WIDTH))

    return pl.pallas_call(
        _hgrn_prompt_kernel,
        grid=(bsz, nch),
        in_specs=[zcol(COL_QB), zcol(COL_FB), zcol(COL_IB), zcol(COL_ZB),
                  pl.BlockSpec(hg_lower.shape, lambda b, c: (0, 0)),
                  pl.BlockSpec((1, HG_DIM), lambda b, c: (0, 0))],
        out_specs=[pl.BlockSpec((HG_CHUNK, HG_WIDTH), lambda b, c: (b * nch + c, 0)),
                   pl.BlockSpec((1, HG_HEADS, HG_DIM, HG_DIM), lambda b, c: (b, 0, 0, 0))],
        out_shape=[jax.ShapeDtypeStruct((bsz * seq, HG_WIDTH), f32),
                   jax.ShapeDtypeStruct((bsz, HG_HEADS, HG_DIM, HG_DIM), f32)],
        scratch_shapes=[pltpu.VMEM((HG_HEADS, HG_DIM, HG_DIM), f32)],
        compiler_params=_cparams(("parallel", "arbitrary")),
        name="hgrn_prompt",
    )(z, z, z, z, hg_lower, hg_norm)


def _hgrn_sample_kernel(q_ref, f_ref, i_ref, zb_ref, lo_ref, nw_ref, s_ref, y_ref, so_ref, o_scr):
    nb = q_ref.shape[0]
    lb = _lower_bound(lo_ref[...])
    qq, kk, logf = _hgrn_gates(q_ref[...], f_ref[...], lb)
    v = i_ref[...]
    pad = jnp.zeros((HG_DIM - nb, HG_DIM), f32)
    to_cols = lambda x: jnp.concatenate([x, pad], axis=0).T
    f_t, k_t, q_t = to_cols(jnp.exp(logf)), to_cols(kk), to_cols(qq)
    for b in range(nb):
        s_new = f_t[:, b:b + 1] * s_ref[0, b, 0] + k_t[:, b:b + 1] * v[b:b + 1, :]
        so_ref[0, b, 0] = s_new
        o_scr[b:b + 1, :] = jnp.sum(s_new * q_t[:, b:b + 1], axis=0, keepdims=True)
    y_ref[...] = _hgrn_out(o_scr[...], nw_ref[...], zb_ref[...])


def _hgrn_sample(z, state, hg_lower, hg_norm):
    nb = z.shape[0]

    def zcol(base):
        return pl.BlockSpec((nb, HG_DIM), lambda h: (0, base // HG_DIM + h))

    sblk = pl.BlockSpec((1, nb, 1, HG_DIM, HG_DIM), lambda h: (0, 0, h, 0, 0))
    return pl.pallas_call(
        _hgrn_sample_kernel,
        grid=(HG_HEADS,),
        in_specs=[zcol(COL_QB), zcol(COL_FB), zcol(COL_IB), zcol(COL_ZB),
                  pl.BlockSpec((hg_lower.shape[0], HG_DIM), lambda h: (0, h)),
                  pl.BlockSpec((1, HG_DIM), lambda h: (0, 0)),
                  sblk],
        out_specs=[pl.BlockSpec((nb, HG_DIM), lambda h: (0, h)), sblk],
        out_shape=[jax.ShapeDtypeStruct((nb, HG_WIDTH), f32),
                   jax.ShapeDtypeStruct(state.shape, f32)],
        scratch_shapes=[pltpu.VMEM((nb, HG_DIM), f32)],
        compiler_params=_cparams(("parallel",)),
        name="hgrn_sample",
    )(z, z, z, z, hg_lower, hg_norm, state)


N_PICK = N_SEL - 1


def _head_slopes():
    h = lax.broadcasted_iota(jnp.int32, (NSA_HEADS, 1), 0).astype(f32)
    return jnp.exp2(-(h + 1.0))


def _nsa_sample_cmp_kernel(q_ref, comp_ref, m_ref, oc_ref, ps_ref, *, t_pos):
    q = q_ref[0] * jnp.asarray(HEAD_DIM ** -0.5, bf16)
    slope = _head_slopes()
    head = lax.broadcasted_iota(jnp.int32, (NSA_HEADS, 1), 0)
    ncol = lax.broadcasted_iota(jnp.int32, (1, N_CMP_PAD), 1)
    d_c = t_pos - (ncol * CMP_STRIDE + (CMP_BLOCK - 1))
    mask = d_c >= 0
    o_c = jnp.zeros((NSA_HEADS, HEAD_DIM), f32)
    ps_ref[...] = jnp.zeros_like(ps_ref)
    for g in range(NSA_KV_HEADS):
        kc = comp_ref[0, g].astype(bf16)
        vc = comp_ref[0, NSA_KV_HEADS + g].astype(bf16)
        s = jnp.where(mask, _dot_nt(q, kc) - slope * d_c.astype(f32), NEG)
        m = jnp.max(s, axis=1, keepdims=True)
        e = jnp.where(mask, jnp.exp(s - m), 0.0)
        p = e * (1.0 / jnp.maximum(jnp.sum(e, axis=1, keepdims=True), 1e-30))
        mine = jnp.right_shift(head, NSA_GROUP.bit_length() - 1) == g
        o_c = jnp.where(mine, _dot(p.astype(bf16), vc), o_c)
        imp = jnp.sum(jnp.where(mine, p, 0.0), axis=0, keepdims=True)
        ps_ref[0, g:g + 1, :] = _importance_to_blocks(imp, m_ref[...])
    oc_ref[0] = o_c


def _nsa_sample_cmp(q8, comp, m_mat, t_pos):
    nb = q8.shape[0]
    return pl.pallas_call(
        functools.partial(_nsa_sample_cmp_kernel, t_pos=t_pos),
        grid=(nb,),
        in_specs=[pl.BlockSpec((1, NSA_HEADS, HEAD_DIM), lambda b: (b, 0, 0)),
                  pl.BlockSpec((1, N_COMBO, N_CMP_PAD, HEAD_DIM), lambda b: (b, 0, 0, 0)),
                  pl.BlockSpec((N_CMP_PAD, N_SELBLK), lambda b: (0, 0))],
        out_specs=[pl.BlockSpec((1, NSA_HEADS, HEAD_DIM), lambda b: (b, 0, 0)),
                   pl.BlockSpec((1, 8, N_SELBLK), lambda b: (b, 0, 0))],
        out_shape=[jax.ShapeDtypeStruct((nb, NSA_HEADS, HEAD_DIM), f32),
                   jax.ShapeDtypeStruct((nb, 8, N_SELBLK), f32)],
        compiler_params=_cparams(("parallel",)),
        name="nsa_sample_cmp",
    )(q8, comp, m_mat)


def _nsa_sample_topk_kernel(ps_ref, idx_ref, *, last_blk):
    ps = ps_ref[...]
    lane = lax.broadcasted_iota(jnp.int32, ps.shape, 1)
    score = ps + jnp.where((lane == 0) | (lane == last_blk), FORCE_BONUS, 0.0)
    out = jnp.zeros(ps.shape, jnp.int32)
    for r in range(N_PICK):
        best = jnp.max(score, axis=1, keepdims=True)
        idx = jnp.min(jnp.where(score == best, lane, N_SELBLK), axis=1, keepdims=True)
        out = jnp.where(lane == r, idx, out)
        score = jnp.where(lane == idx, REMOVED, score)
    idx_ref[...] = out


def _nsa_sample_topk(ps_rows, last_blk):
    return pl.pallas_call(
        functools.partial(_nsa_sample_topk_kernel, last_blk=last_blk),
        out_shape=jax.ShapeDtypeStruct(ps_rows.shape, jnp.int32),
        name="nsa_sample_topk",
    )(ps_rows)


def _nsa_sample_sel_kernel(pt_ref, idx_ref, slc_ref, q_ref, oc_ref, gate_ref, slc_new_ref, win_new_ref, wc_ref,
                           o_ref, wout_ref, kv_scr, sem_ref, *, t_pos):
    n_blk = NSA_KV_HEADS * N_PICK
    halves = PAGE_SIZE // SEL_BLOCK
    b = pl.program_id(0)
    nb = pl.num_programs(0)
    slot = lax.rem(b, 2)

    def picked(seq, k):
        return idx_ref[(seq * NSA_KV_HEADS + k // N_PICK) * N_SEL + k % N_PICK]

    def blk_copies(seq, s, k):
        page = pt_ref[seq, picked(seq, k) // halves]
        g = k // N_PICK
        return [pltpu.make_async_copy(slc_ref.at[page, kind * NSA_KV_HEADS + g], kv_scr.at[s, kind, k],
                                      sem_ref.at[s]) for kind in range(2)]

    def start_all(seq, s):
        for k in range(n_blk):
            for cp in blk_copies(seq, s, k):
                cp.start()

    @pl.when(b == 0)
    def _():
        start_all(0, 0)

    @pl.when(b + 1 < nb)
    def _():
        start_all(b + 1, 1 - slot)

    for k in range(n_blk):
        for cp in blk_copies(b, slot, k):
            cp.wait()

    q = q_ref[0] * jnp.asarray(HEAD_DIM ** -0.5, bf16)
    slope = _head_slopes()
    head = lax.broadcasted_iota(jnp.int32, (NSA_HEADS, 1), 0)
    lane = lax.broadcasted_iota(jnp.int32, (1, PAGE_SIZE), 1)
    new_slc = slc_new_ref[0]
    new_win = win_new_ref[0]
    wl = wc_ref.shape[3]
    wlane = lax.broadcasted_iota(jnp.int32, (1, wl), 1)
    d_w = (wl - wlane).astype(f32)
    qf = q.astype(f32)

    def attend(s, mask, v_t, new_row, g):
        k_new = new_row[:, g * HEAD_DIM:(g + 1) * HEAD_DIM].astype(bf16).astype(f32)
        v_new = new_row[:, KV_WIDTH + g * HEAD_DIM:KV_WIDTH + (g + 1) * HEAD_DIM].astype(bf16).astype(f32)
        s_new = jnp.sum(qf * k_new, axis=1, keepdims=True)
        m = jnp.maximum(jnp.max(s, axis=1, keepdims=True), s_new)
        e, e_new = jnp.exp(s - m), jnp.exp(s_new - m)
        if mask is not None:
            e = jnp.where(mask, e, 0.0)
        den = jnp.maximum(jnp.sum(e, axis=1, keepdims=True) + e_new, 1e-30)
        num = _dot_nt(e.astype(bf16), v_t.astype(bf16)) + e_new.astype(bf16).astype(f32) * v_new
        return num * (1.0 / den)

    o_s = jnp.zeros((NSA_HEADS, HEAD_DIM), f32)
    o_w = jnp.zeros((NSA_HEADS, HEAD_DIM), f32)
    for g in range(NSA_KV_HEADS):
        mine = jnp.right_shift(head, NSA_GROUP.bit_length() - 1) == g
        pos, member = [], []
        for n in range(N_PICK):
            j = picked(b, g * N_PICK + n)
            pos.append((j // halves) * PAGE_SIZE + lane)
            member.append(jnp.right_shift(lane, SEL_BLOCK.bit_length() - 1) == j % halves)
        pos = jnp.concatenate(pos, axis=1)
        member = jnp.concatenate(member, axis=1)
        k_t = jnp.concatenate([kv_scr[slot, 0, g * N_PICK + n] for n in range(N_PICK)], axis=1)
        v_t = jnp.concatenate([kv_scr[slot, 1, g * N_PICK + n] for n in range(N_PICK)], axis=1)
        s = jnp.where(member, _dot(q, k_t.astype(bf16)) - slope * (t_pos - pos).astype(f32), NEG)
        o_s = jnp.where(mine, attend(s, member, v_t, new_slc, g), o_s)
        s = _dot(q, wc_ref[0, g].astype(bf16)) - slope * d_w
        o_w = jnp.where(mine, attend(s, None, wc_ref[0, NSA_KV_HEADS + g], new_win, g), o_w)

    gate = jnp.broadcast_to(_sigmoid(gate_ref[0]), (NSA_HEADS, LANES))
    glane = lax.broadcasted_iota(jnp.int32, (NSA_HEADS, LANES), 1)
    gcol = [jnp.sum(jnp.where(glane == 3 * head + j, gate, 0.0), axis=1, keepdims=True) for j in range(3)]
    o_ref[0] = gcol[0] * oc_ref[0] + gcol[1] * o_s + gcol[2] * o_w
    eye = lax.broadcasted_iota(jnp.int32, (HEAD_DIM, HEAD_DIM), 0) == \
        lax.broadcasted_iota(jnp.int32, (HEAD_DIM, HEAD_DIM), 1)
    for c in range(N_COMBO):
        piece = jnp.broadcast_to(new_win[:, c * HEAD_DIM:(c + 1) * HEAD_DIM], (HEAD_DIM, HEAD_DIM))
        new_col = jnp.sum(jnp.where(eye, piece, 0.0), axis=1, keepdims=True)
        wout_ref[0, c] = jnp.where(wlane == wl - 1, new_col, pltpu.roll(wc_ref[0, c], wl - 1, axis=1))


def _nsa_sample_sel(page_table, idx_flat, slc_pages_t, q8, o_c, gates, slc_new, win_new, cache_win_t, t_pos):
    nb = q8.shape[0]
    wl = cache_win_t.shape[3]
    n_blk = NSA_KV_HEADS * N_PICK
    per_seq = lambda b, pt, idx: (b, 0, 0)
    win_blk = pl.BlockSpec((1, N_COMBO, HEAD_DIM, wl), lambda b, pt, idx: (b, 0, 0, 0))
    grid_spec = pltpu.PrefetchScalarGridSpec(
        num_scalar_prefetch=2,
        grid=(nb,),
        in_specs=[pl.BlockSpec(memory_space=pl.ANY),
                  pl.BlockSpec((1, NSA_HEADS, HEAD_DIM), per_seq),
                  pl.BlockSpec((1, NSA_HEADS, HEAD_DIM), per_seq),
                  pl.BlockSpec((1, 1, LANES), per_seq),
                  pl.BlockSpec((1, 1, 2 * KV_WIDTH), per_seq),
                  pl.BlockSpec((1, 1, 2 * KV_WIDTH), per_seq),
                  win_blk],
        out_specs=[pl.BlockSpec((1, NSA_HEADS, HEAD_DIM), per_seq), win_blk],
        scratch_shapes=[pltpu.VMEM((2, 2, n_blk, HEAD_DIM, PAGE_SIZE), f32),
                        pltpu.SemaphoreType.DMA((2,))],
    )
    return pl.pallas_call(
        functools.partial(_nsa_sample_sel_kernel, t_pos=t_pos),
        grid_spec=grid_spec,
        out_shape=[jax.ShapeDtypeStruct((nb, NSA_HEADS, HEAD_DIM), f32),
                   jax.ShapeDtypeStruct(cache_win_t.shape, f32)],
        compiler_params=_cparams(("arbitrary",)),
        name="nsa_sample_sel",
    )(page_table, idx_flat, slc_pages_t, q8, o_c, gates, slc_new, win_new, cache_win_t)


def kernel(x_prompt, x_sample, cache_cmp_kv, cache_slc_kv, cache_win_kv, state_hgrn, page_table, p_prompt, p_sample,
           w_in, g_pre, cmp_pe, cmp_w1, cmp_b1, cmp_w2, hg_lower, hg_norm, w_out, g_post, ple_proj, ple_gate):
    bsz, seq, _ = x_prompt.shape
    nb = x_sample.shape[0]
    n_pool = cache_cmp_kv.shape[1]
    n_pages = page_table.shape[1]
    past = n_pages * PAGE_SIZE
    nq = seq // Q_BLOCK
    kv_tail = (2, NSA_KV_HEADS, HEAD_DIM)

    o = _ORIG
    w = w_in[0]
    w_r = jnp.concatenate([w[:, o['q_a']:o['k_cmp']], w[:, o['z_a']:o['end']], w[:, o['k_cmp']:o['gate_a']],
                           w[:, o['gate_a']:o['z_a']],
                           jnp.zeros((D_MODEL, PROJ_WIDTH - o['end']), f32)], axis=1).astype(bf16)
    cw = _compress_weights(cmp_pe[0], cmp_w1[0], cmp_b1[0], cmp_w2[0])
    m_mat = _sel_map_matrix()
    w_out_b, ple_proj_b, ple_gate_b = w_out[0].astype(bf16), ple_proj[0].astype(bf16), ple_gate[0].astype(bf16)

    xp = x_prompt.reshape(bsz * seq, D_MODEL)
    zp, cmp_p, slc_p, win_p, gates_p = _proj(xp, g_pre, w_r, 1024)
    pages_p = cmp_p.reshape(bsz * seq // PAGE_SIZE, PAGE_SIZE, ROW_WIDTH)
    table_p = jnp.arange(bsz * seq // PAGE_SIZE, dtype=jnp.int32).reshape(bsz, seq // PAGE_SIZE)
    comp_p = _compress(pages_p, table_p, *cw, feature_major=False)
    q_t = zp[:, COL_QA:COL_QA + NSA_WIDTH].reshape(bsz, nq, Q_BLOCK, NSA_KV_HEADS, NSA_GROUP, HEAD_DIM)
    q_t = q_t.transpose(0, 3, 1, 5, 4, 2).reshape(bsz, NSA_KV_HEADS, nq, HEAD_DIM, NSA_GROUP * Q_BLOCK)
    k_slc, vt_slc = _attention_keys(slc_p, bsz, seq, augment=True)
    k_win, vt_win = _attention_keys(win_p, bsz, seq, augment=False)
    o_a = _nsa_prompt_pair(q_t, gates_p, comp_p, k_slc, vt_slc, k_win, vt_win, m_mat.T)
    yb_p, hg_p = _hgrn_prompt(zp, hg_lower, hg_norm, bsz, seq)
    h_p = _finish(xp, o_a, zp, yb_p, p_prompt[0].reshape(bsz * seq, PLE_DIM),
                  w_out_b, g_post, ple_proj_b, ple_gate_b, 512)

    wl_p = min(WINDOW, seq)
    out_prompt = (h_p.reshape(bsz, seq, D_MODEL),
                  cmp_p.reshape((1, bsz, seq) + kv_tail),
                  slc_p.reshape((1, bsz, seq) + kv_tail),
                  win_p.reshape((bsz, seq) + kv_tail)[:, seq - wl_p:][None],
                  hg_p[None])

    xs = x_sample.reshape(nb, D_MODEL)
    zs, cmp_s, slc_s, win_new, gates_s = _proj(xs, g_pre, w_r, nb)
    cmp_t = cache_cmp_kv.transpose(0, 1, 3, 4, 5, 2).reshape(n_pool, ROW_WIDTH, PAGE_SIZE)
    comp_s = _compress(cmp_t, page_table, *cw, feature_major=True)
    q8 = zs[:, COL_QA:COL_QA + NSA_WIDTH].astype(bf16).reshape(nb, NSA_HEADS, HEAD_DIM)
    oc_s, ps_s = _nsa_sample_cmp(q8, comp_s, m_mat, past)
    ps_rows = ps_s[:, :NSA_KV_HEADS, :].reshape(nb * NSA_KV_HEADS, N_SELBLK)
    idx = _nsa_sample_topk(ps_rows, past // SEL_BLOCK - 1)
    slc_t = cache_slc_kv.transpose(0, 1, 3, 4, 5, 2).reshape(n_pool, N_COMBO, HEAD_DIM, PAGE_SIZE)
    wl_s = cache_win_kv.shape[2]
    win_t = cache_win_kv.transpose(0, 1, 3, 4, 5, 2).reshape(nb, N_COMBO, HEAD_DIM, wl_s)
    oa_s, win_s = _nsa_sample_sel(page_table, idx[:, :N_SEL].reshape(-1), slc_t, q8, oc_s,
                                  gates_s.reshape(nb, 1, LANES), slc_s.reshape(nb, 1, ROW_WIDTH),
                                  win_new.reshape(nb, 1, ROW_WIDTH), win_t, past)
    win_s = win_s.reshape((1, nb) + kv_tail + (wl_s,)).transpose(0, 1, 5, 2, 3, 4)
    yb_s, hg_s = _hgrn_sample(zs, state_hgrn, hg_lower, hg_norm)
    h_s = _finish(xs, oa_s.reshape(nb, NSA_WIDTH), zs, yb_s, p_sample[0].reshape(nb, PLE_DIM),
                  w_out_b, g_post, ple_proj_b, ple_gate_b, nb)

    out_sample = (h_s.reshape(nb, 1, D_MODEL),
                  cmp_s.reshape((1, nb, 1) + kv_tail),
                  slc_s.reshape((1, nb, 1) + kv_tail),
                  win_s,
                  hg_s)
    return (out_prompt[0], out_sample[0]) + out_prompt[1:] + out_sample[1:]
```

```python
import functools

import numpy as np
import jax
import jax.numpy as jnp
from jax import lax
from jax.experimental import pallas as pl
from jax.experimental.pallas import tpu as pltpu

D_MODEL = 1024
PAGE_SIZE = 128
NSA_HEADS = 8
NSA_KV_HEADS = 2
NSA_GROUP = NSA_HEADS // NSA_KV_HEADS
HEAD_DIM = 64
NSA_WIDTH = NSA_HEADS * HEAD_DIM
KV_WIDTH = NSA_KV_HEADS * HEAD_DIM
CMP_BLOCK = 32
CMP_STRIDE = 16
CMP_RATIO = CMP_BLOCK // CMP_STRIDE
CMP_HIDDEN = 128
SEL_BLOCK = 64
SEL_RATIO = SEL_BLOCK // CMP_STRIDE
N_SEL = 16
WINDOW = 512
Q_BLOCK = 128
HG_HEADS = 4
HG_DIM = 128
HG_WIDTH = HG_HEADS * HG_DIM
PLE_DIM = 256
EPS = 1e-6
NEG = -1e30
FORCE_BONUS = 1e6
REMOVED = -3e38

COL_QA, COL_ZA, COL_QB, COL_FB, COL_IB, COL_ZB = 0, 512, 1024, 1536, 2048, 2560
COL_CMP, COL_SLC, COL_WIN, COL_GATE = 3072, 3328, 3584, 3840
PROJ_WIDTH = 4096
_ORIG = dict(q_a=0, k_cmp=512, gate_a=1280, z_a=1304, q_b=1816, f_b=2328, i_b=2840, z_b=3352, end=3864)

LANES = 128
VMEM_LIMIT = 56 * 1024 * 1024

SEL_CHUNK = 512
WIN_KEYS = WINDOW + Q_BLOCK
N_CMP_PAD = 512
N_SELBLK = 128

f32 = jnp.float32
bf16 = jnp.bfloat16


def _cparams(sem):
    return pltpu.CompilerParams(dimension_semantics=sem, vmem_limit_bytes=VMEM_LIMIT)


def _dot(a, b):
    return jnp.dot(a, b, preferred_element_type=f32)


def _dot_nt(a, b):
    return lax.dot_general(a, b, (((1,), (1,)), ((), ())), preferred_element_type=f32)


def _dot_tn(a, b):
    return lax.dot_general(a, b, (((0,), (0,)), ((), ())), preferred_element_type=f32)


def _sigmoid(x):
    return 1.0 / (1.0 + jnp.exp(-x))


def _silu(x):
    return x * _sigmoid(x)


PROJ_TILE = 1024
MAIN_TILES = COL_CMP // PROJ_TILE


def _proj_kernel(x_ref, g_ref, w_ref, z_ref, cmp_ref, slc_ref, win_ref, gate_ref, xn_ref):
    j = pl.program_id(1)

    @pl.when(j == 0)
    def _():
        x = x_ref[...]
        y = x * lax.rsqrt(jnp.mean(x * x, axis=-1, keepdims=True) + EPS)
        xn_ref[...] = (y * g_ref[...]).astype(bf16)

    res = _dot(xn_ref[...], w_ref[...])

    @pl.when(j < MAIN_TILES)
    def _():
        z_ref[...] = res

    @pl.when(j == MAIN_TILES)
    def _():
        base = COL_CMP
        cmp_ref[...] = res[:, COL_CMP - base:COL_SLC - base]
        slc_ref[...] = res[:, COL_SLC - base:COL_WIN - base]
        win_ref[...] = res[:, COL_WIN - base:COL_GATE - base]
        gate_ref[...] = res[:, COL_GATE - base:COL_GATE - base + LANES]


def _proj(x, g, w, tm):
    n = x.shape[0]
    row = lambda i, j: (i, 0)
    return pl.pallas_call(
        _proj_kernel,
        grid=(n // tm, PROJ_WIDTH // PROJ_TILE),
        in_specs=[pl.BlockSpec((tm, D_MODEL), row),
                  pl.BlockSpec((1, D_MODEL), lambda i, j: (0, 0)),
                  pl.BlockSpec((D_MODEL, PROJ_TILE), lambda i, j: (0, j))],
        out_specs=[pl.BlockSpec((tm, PROJ_TILE), lambda i, j: (i, jnp.minimum(j, MAIN_TILES - 1))),
                   pl.BlockSpec((tm, ROW_WIDTH), row), pl.BlockSpec((tm, ROW_WIDTH), row),
                   pl.BlockSpec((tm, ROW_WIDTH), row), pl.BlockSpec((tm, LANES), row)],
        out_shape=[jax.ShapeDtypeStruct((n, COL_CMP), f32),
                   jax.ShapeDtypeStruct((n, ROW_WIDTH), f32), jax.ShapeDtypeStruct((n, ROW_WIDTH), f32),
                   jax.ShapeDtypeStruct((n, ROW_WIDTH), f32), jax.ShapeDtypeStruct((n, LANES), f32)],
        scratch_shapes=[pltpu.VMEM((tm, D_MODEL), bf16)],
        compiler_params=_cparams(("parallel", "arbitrary")),
        name="proj",
    )(x, g, w)


def _finish_kernel(x_ref, oa_ref, za_ref, yb_ref, p_ref, wo_ref, gp_ref, pp_ref, pg_ref, h_ref):
    ya = (oa_ref[...] * _silu(za_ref[...])).astype(bf16)
    yb = yb_ref[...].astype(bf16)
    mix = _dot(ya, wo_ref[0:NSA_WIDTH, :]) + _dot(yb, wo_ref[NSA_WIDTH:, :])
    nrm = mix * lax.rsqrt(jnp.mean(mix * mix, axis=-1, keepdims=True) + EPS) * gp_ref[...]
    h = x_ref[...] + nrm
    gate = _sigmoid(_dot(h.astype(bf16), pg_ref[...]))
    h_ref[...] = h + _dot(p_ref[...].astype(bf16), pp_ref[...]) * gate


def _finish(x, oa, z, yb, p, w_out, g_post, ple_proj, ple_gate, tm):
    n = x.shape[0]
    za_blk = COL_ZA // NSA_WIDTH
    row = lambda i: (i, 0)
    const = lambda i: (0, 0)
    return pl.pallas_call(
        _finish_kernel,
        grid=(n // tm,),
        in_specs=[pl.BlockSpec((tm, D_MODEL), row),
                  pl.BlockSpec((tm, NSA_WIDTH), row),
                  pl.BlockSpec((tm, NSA_WIDTH), lambda i: (i, za_blk)),
                  pl.BlockSpec((tm, HG_WIDTH), row),
                  pl.BlockSpec((tm, PLE_DIM), row),
                  pl.BlockSpec((D_MODEL, D_MODEL), const),
                  pl.BlockSpec((1, D_MODEL), const),
                  pl.BlockSpec((PLE_DIM, D_MODEL), const),
                  pl.BlockSpec((D_MODEL, D_MODEL), const)],
        out_specs=pl.BlockSpec((tm, D_MODEL), row),
        out_shape=jax.ShapeDtypeStruct((n, D_MODEL), f32),
        compiler_params=_cparams(("parallel",)),
        name="finish",
    )(x, oa, z, yb, p, w_out, g_post, ple_proj, ple_gate)


SEGS_PER_PAGE = PAGE_SIZE // CMP_STRIDE
ROW_WIDTH = 2 * KV_WIDTH
SEG_WIDTH = CMP_STRIDE * ROW_WIDTH
N_COMBO = 2 * NSA_KV_HEADS


def _compress_kernel(pt_ref, pages_ref, wa_ref, wb_ref, pea_ref, peb_ref, b1_ref, w2_ref, out_ref,
                     stage_ref, rows_ref, sem_ref, *, n_pages, feature_major):
    b = pl.program_id(0)
    nb = pl.num_programs(0)
    slot = lax.rem(b, 2)

    def page_copy(seq, s, j):
        return pltpu.make_async_copy(pages_ref.at[pt_ref[seq, j]], stage_ref.at[s, j], sem_ref.at[s])

    def start_all(seq, s):
        def body(j, c):
            page_copy(seq, s, j).start()
            return c
        lax.fori_loop(0, n_pages, body, 0)

    @pl.when(b == 0)
    def _():
        start_all(0, 0)

    @pl.when(b + 1 < nb)
    def _():
        start_all(b + 1, 1 - slot)

    def wait_body(j, c):
        page_copy(b, slot, j).wait()
        return c
    lax.fori_loop(0, n_pages, wait_body, 0)

    lane_halves = ROW_WIDTH // LANES

    def to_rows(j, c):
        tile = stage_ref[slot, j]
        rows = tile.T if feature_major else tile
        r0 = pl.multiple_of(j * PAGE_SIZE, PAGE_SIZE)
        for h in range(lane_halves):
            rows_ref[h, pl.ds(r0, PAGE_SIZE), :] = rows[:, h * LANES:(h + 1) * LANES]
        return c
    lax.fori_loop(0, n_pages, to_rows, 0, unroll=4)

    n_seg = n_pages * SEGS_PER_PAGE
    for kind in range(lane_halves):
        cols = slice(kind * LANES, (kind + 1) * LANES)
        seg_a, seg_b = [], []
        for j in range(CMP_STRIDE):
            r = rows_ref[kind, pl.ds(j, n_seg, stride=CMP_STRIDE), :]
            seg_a.append((r + pea_ref[j][:, cols]).astype(bf16))
            seg_b.append((r + peb_ref[j][:, cols]).astype(bf16))
        pa = _dot(jnp.concatenate(seg_a, axis=1), wa_ref[kind])
        pb = _dot(jnp.concatenate(seg_b, axis=1), wb_ref[kind])
        pb = pltpu.roll(pb, n_seg - 1, axis=0)
        hid = _silu(pa + pb + b1_ref[kind])
        for g in range(NSA_KV_HEADS):
            hg = hid[:, g * CMP_HIDDEN:(g + 1) * CMP_HIDDEN].astype(bf16)
            out_ref[0, kind * NSA_KV_HEADS + g] = _dot(hg, w2_ref[kind])


def _compress(pages, page_table, wa, wb, pea, peb, b1, w2, feature_major):
    nseq, n_pages = page_table.shape
    n_seg = n_pages * SEGS_PER_PAGE
    const2 = lambda b, pt: (0, 0)
    const3 = lambda b, pt: (0, 0, 0)
    once = pl.Buffered(1)
    grid_spec = pltpu.PrefetchScalarGridSpec(
        num_scalar_prefetch=1,
        grid=(nseq,),
        in_specs=[pl.BlockSpec(memory_space=pl.ANY),
                  pl.BlockSpec(wa.shape, const3, pipeline_mode=once),
                  pl.BlockSpec(wb.shape, const3, pipeline_mode=once),
                  pl.BlockSpec((CMP_STRIDE, 1, ROW_WIDTH), const3),
                  pl.BlockSpec((CMP_STRIDE, 1, ROW_WIDTH), const3),
                  pl.BlockSpec(b1.shape, const3),
                  pl.BlockSpec((2, CMP_HIDDEN, HEAD_DIM), const3)],
        out_specs=pl.BlockSpec((1, N_COMBO, n_seg, HEAD_DIM), lambda b, pt: (b, 0, 0, 0)),
        scratch_shapes=[pltpu.VMEM((2, n_pages) + pages.shape[1:], f32),
                        pltpu.VMEM((ROW_WIDTH // LANES, n_pages * PAGE_SIZE, LANES), f32),
                        pltpu.SemaphoreType.DMA((2,))],
    )
    return pl.pallas_call(
        functools.partial(_compress_kernel, n_pages=n_pages, feature_major=feature_major),
        grid_spec=grid_spec,
        out_shape=jax.ShapeDtypeStruct((nseq, N_COMBO, n_seg, HEAD_DIM), f32),
        compiler_params=_cparams(("arbitrary",)),
        name="compress",
    )(page_table, pages, wa, wb, pea, peb, b1, w2)


def _compress_weights(cmp_pe, cmp_w1, cmp_b1, cmp_w2):
    eye = jnp.eye(NSA_KV_HEADS, dtype=f32)

    def big(w_half):
        t = jnp.einsum('kjdh,gb->kjbdgh', w_half, eye)
        return t.reshape(2, CMP_STRIDE * KV_WIDTH, NSA_KV_HEADS * CMP_HIDDEN)

    w1 = cmp_w1.reshape(2, 2, CMP_STRIDE, HEAD_DIM, CMP_HIDDEN)
    wa = big(w1[:, 0]).astype(bf16)
    wb = big(w1[:, 1]).astype(bf16)
    pe = cmp_pe.reshape(2, 2, CMP_STRIDE, HEAD_DIM)

    def pe_row(p):
        t = jnp.broadcast_to(p.transpose(1, 0, 2)[:, :, None, :],
                             (CMP_STRIDE, 2, NSA_KV_HEADS, HEAD_DIM))
        return t.reshape(CMP_STRIDE, 1, ROW_WIDTH)

    pea, peb = pe_row(pe[:, 0]), pe_row(pe[:, 1])
    b1 = jnp.broadcast_to(cmp_b1[:, None, :], (2, NSA_KV_HEADS, CMP_HIDDEN)).reshape(2, 1, NSA_KV_HEADS * CMP_HIDDEN)
    return wa, wb, pea, peb, b1, cmp_w2.astype(bf16)


def _sel_map_matrix():
    wts = np.convolve(np.ones(SEL_RATIO), np.ones(CMP_RATIO))
    m = np.zeros((N_CMP_PAD, N_SELBLK), np.float32)
    for j in range(N_SELBLK):
        for k, wk in enumerate(wts):
            n = SEL_RATIO * j + k
            if n < N_CMP_PAD - 1:
                m[n, j] = wk
    return jnp.asarray(m, dtype=bf16)


def _expand_matrix(n_chunks):
    key = np.arange(n_chunks * SEL_CHUNK) // SEL_BLOCK
    e = (np.arange(N_SELBLK)[:, None] == key[None, :]).astype(np.float32)
    return jnp.asarray(e.reshape(N_SELBLK, n_chunks, SEL_CHUNK).transpose(1, 0, 2), dtype=bf16)


def _importance_to_blocks(imp, m):
    hi = imp.astype(bf16)
    lo = (imp - hi.astype(f32)).astype(bf16)
    return _dot(hi, m) + _dot(lo, m)


def _nsa_prompt_kernel(q_ref, gate_ref, kc_ref, vc_ref, kvs_ref, kvw_ref, m_ref, e_ref, o_ref,
                       acc_ref, mx_ref, used_ref):
    g = pl.program_id(1)
    qi = pl.program_id(2)
    q0 = qi * Q_BLOCK
    rows = NSA_GROUP * Q_BLOCK

    q64 = q_ref[0, 0, 0] * jnp.asarray(HEAD_DIM ** -0.5, bf16)
    q128 = jnp.concatenate([q64, jnp.zeros_like(q64)], axis=1)
    slope = [jnp.where(g == 0, 2.0 ** -(r + 1), 2.0 ** -(NSA_GROUP + r + 1)).astype(f32)
             for r in range(NSA_GROUP)]

    qrow = lax.broadcasted_iota(jnp.int32, (Q_BLOCK, 1), 0)

    def softmax_rows(s, dist, mask):
        es, ls = [], []
        for r in range(NSA_GROUP):
            sr = jnp.where(mask, s[r * Q_BLOCK:(r + 1) * Q_BLOCK] - slope[r] * dist, NEG)
            m = jnp.max(sr, axis=1, keepdims=True)
            e = jnp.where(mask, jnp.exp(sr - m), 0.0)
            es.append(e)
            ls.append(1.0 / jnp.maximum(jnp.sum(e, axis=1, keepdims=True), 1e-30))
        return es, ls

    kc = kc_ref[0, 0].astype(bf16)
    vc = vc_ref[0, 0].astype(bf16)
    ncol = lax.broadcasted_iota(jnp.int32, (1, N_CMP_PAD), 1)
    d_c = (q0 + qrow) - (ncol * CMP_STRIDE + (CMP_BLOCK - 1))
    es, ls = softmax_rows(_dot_nt(q64, kc), d_c.astype(f32), d_c >= 0)
    ps_c = [e * l for e, l in zip(es, ls)]
    o_c = _dot(jnp.concatenate(ps_c, axis=0).astype(bf16), vc)
    imp = ps_c[0] + ps_c[1] + ps_c[2] + ps_c[3]
    p_sel = _importance_to_blocks(imp, m_ref[...])

    ps_t = p_sel.T
    blk = lax.broadcasted_iota(jnp.int32, (N_SELBLK, Q_BLOCK), 0)
    tq = q0 + lax.broadcasted_iota(jnp.int32, (N_SELBLK, Q_BLOCK), 1)
    jt = jnp.right_shift(tq, SEL_BLOCK.bit_length() - 1)
    valid = blk <= jt
    forced = (blk == 0) | (blk == jt) | (blk == jt - 1)
    score = jnp.where(valid, ps_t + jnp.where(forced, FORCE_BONUS, 0.0), NEG)
    sel_t = jnp.zeros((N_SELBLK, Q_BLOCK), f32)
    for _ in range(N_SEL):
        best = jnp.max(score, axis=0, keepdims=True)
        idx = jnp.min(jnp.where(score == best, blk, N_SELBLK), axis=0, keepdims=True)
        hit = blk == idx
        sel_t = jnp.where(hit & (best > NEG * 0.5), 1.0, sel_t)
        score = jnp.where(hit, REMOVED, score)
    sel = sel_t.T.astype(bf16)

    acc_ref[...] = jnp.zeros_like(acc_ref)
    mx_ref[...] = jnp.full_like(mx_ref, NEG)
    kcol = lax.broadcasted_iota(jnp.int32, (1, SEL_CHUNK), 1)
    d_base = (qrow - kcol).astype(f32)

    blk_per_chunk = SEL_CHUNK // SEL_BLOCK
    for c in range(e_ref.shape[0]):
        picked = jnp.max(sel_t[c * blk_per_chunk:(c + 1) * blk_per_chunk, :])
        used_ref[c] = (picked > 0.5).astype(jnp.int32)

    def sel_chunk(c, carry):
        @pl.when(used_ref[c] > 0)
        def _():
            sel_chunk_body(c)
        return carry

    def sel_chunk_body(c):
        k0 = pl.multiple_of(c * SEL_CHUNK, SEL_CHUNK)
        kv = kvs_ref[0, 0, pl.ds(k0, SEL_CHUNK), :]
        s = _dot_nt(q128, kv[:, 0:LANES])
        dist = d_base + (q0 - k0).astype(f32)
        allowed = (_dot(sel, e_ref[c]) > 0.5) & (dist >= 0.0)
        ps = []
        for r in range(NSA_GROUP):
            rs = slice(r * Q_BLOCK, (r + 1) * Q_BLOCK)
            sr = jnp.where(allowed, s[rs] - slope[r] * dist, NEG)
            m_old = mx_ref[rs]
            m_new = jnp.maximum(m_old, jnp.max(sr, axis=1, keepdims=True))
            acc_ref[rs] = acc_ref[rs] * jnp.exp(m_old - m_new)
            mx_ref[rs] = m_new
            ps.append(jnp.exp(sr - m_new).astype(bf16))
        acc_ref[...] += _dot(jnp.concatenate(ps, axis=0), kv[:, LANES:])

    n_chunks = (q0 + Q_BLOCK + SEL_CHUNK - 1) // SEL_CHUNK
    lax.fori_loop(0, n_chunks, sel_chunk, 0)
    acc = acc_ref[...]
    o_s = acc[:, 0:HEAD_DIM] * (1.0 / jnp.maximum(acc[:, HEAD_DIM:HEAD_DIM + 1], 1e-30))

    w0 = pl.multiple_of(jnp.maximum(q0 - WINDOW, 0), Q_BLOCK)
    kvw = kvw_ref[0, 0, pl.ds(w0, WIN_KEYS), :]
    wcol = lax.broadcasted_iota(jnp.int32, (1, WIN_KEYS), 1)
    d_w = (q0 - w0) + qrow - wcol
    es, _ = softmax_rows(_dot_nt(q128, kvw[:, 0:LANES]), d_w.astype(f32), (d_w >= 0) & (d_w <= WINDOW))
    acc_w = _dot(jnp.concatenate(es, axis=0).astype(bf16), kvw[:, LANES:])
    o_w = acc_w[:, 0:HEAD_DIM] * (1.0 / jnp.maximum(acc_w[:, HEAD_DIM:HEAD_DIM + 1], 1e-30))

    gate = _sigmoid(gate_ref[...])

    def gate_col(r, j):
        lo = gate[:, r * 3 + j:r * 3 + j + 1]
        hi = gate[:, (NSA_GROUP + r) * 3 + j:(NSA_GROUP + r) * 3 + j + 1]
        return jnp.where(g == 0, lo, hi)

    for r in range(NSA_GROUP):
        rs = slice(r * Q_BLOCK, (r + 1) * Q_BLOCK)
        o_ref[0, 0, 0, rs, :] = gate_col(r, 0) * o_c[rs] + gate_col(r, 1) * o_s[rs] + gate_col(r, 2) * o_w[rs]


def _nsa_prompt(q_st, z, comp, kv_slc, kv_win, m_mat, e_mat):
    bsz, nq = q_st.shape[0], q_st.shape[1]
    seq = kv_slc.shape[2]
    rows = NSA_GROUP * Q_BLOCK
    gate_blk = COL_GATE // LANES
    return pl.pallas_call(
        _nsa_prompt_kernel,
        grid=(bsz, NSA_KV_HEADS, nq),
        in_specs=[pl.BlockSpec((1, 1, 1, rows, HEAD_DIM), lambda b, g, i: (b, i, g, 0, 0)),
                  pl.BlockSpec((Q_BLOCK, LANES), lambda b, g, i: (b * nq + i, gate_blk)),
                  pl.BlockSpec((1, 1, N_CMP_PAD, HEAD_DIM), lambda b, g, i: (b, g, 0, 0)),
                  pl.BlockSpec((1, 1, N_CMP_PAD, HEAD_DIM), lambda b, g, i: (b, NSA_KV_HEADS + g, 0, 0)),
                  pl.BlockSpec((1, 1, seq, 2 * LANES), lambda b, g, i: (b, g, 0, 0)),
                  pl.BlockSpec((1, 1, seq, 2 * LANES), lambda b, g, i: (b, g, 0, 0)),
                  pl.BlockSpec((N_CMP_PAD, N_SELBLK), lambda b, g, i: (0, 0)),
                  pl.BlockSpec(e_mat.shape, lambda b, g, i: (0, 0, 0))],
        out_specs=pl.BlockSpec((1, 1, 1, rows, HEAD_DIM), lambda b, g, i: (b, i, g, 0, 0)),
        out_shape=jax.ShapeDtypeStruct((bsz, nq, NSA_KV_HEADS, rows, HEAD_DIM), f32),
        scratch_shapes=[pltpu.VMEM((rows, LANES), f32), pltpu.VMEM((rows, 1), f32),
                        pltpu.SMEM((e_mat.shape[0],), jnp.int32)],
        compiler_params=_cparams(("parallel", "parallel", "arbitrary")),
        name="nsa_prompt",
    )(q_st, z, comp, comp, kv_slc, kv_win, m_mat, e_mat)


def _attention_kv(piece, bsz, seq):
    t = piece.reshape(bsz, seq, 2, NSA_KV_HEADS, HEAD_DIM).transpose(2, 0, 3, 1, 4)
    zeros = jnp.zeros_like(t[0])
    ones = jnp.ones(t[0].shape[:-1] + (1,), f32)
    out = jnp.concatenate([t[0], zeros, t[1], ones, zeros[..., :HEAD_DIM - 1]], axis=-1)
    return out.astype(bf16)


LOG2E = 1.4426950408889634
MASK_BIG = 1e30
AUG_ROWS = 16
VT_ROWS = HEAD_DIM + 16
KEY_LOW = 256
N_FORCED = 3


def _nsa_prompt_t_kernel(qt_ref, gate_ref, kc_ref, vc_ref, ks_ref, vts_ref, kw_ref, vtw_ref, mt_ref, o_ref,
                         qa_ref, acc_ref, mg_ref, bias_ref, used_ref):
    g = pl.program_id(1)
    qi = pl.program_id(2)
    q0 = qi * Q_BLOCK
    cols = NSA_GROUP * Q_BLOCK

    lane = lax.broadcasted_iota(jnp.int32, (1, cols), 1)
    head = jnp.right_shift(lane, Q_BLOCK.bit_length() - 1)
    tq = q0 + jnp.bitwise_and(lane, Q_BLOCK - 1)
    c = jnp.exp2(-(head.astype(f32) + 1.0 + NSA_GROUP * g.astype(f32))) * LOG2E

    qt = (qt_ref[0, 0, 0] * (HEAD_DIM ** -0.5 * LOG2E)).astype(bf16)

    def softmax_cols(s, mask):
        s = jnp.where(mask, s, NEG)
        m = jnp.max(s, axis=0, keepdims=True)
        e = jnp.where(mask, jnp.exp2(s - m), 0.0)
        return e, 1.0 / jnp.maximum(jnp.sum(e, axis=0, keepdims=True), 1e-30)

    kc = kc_ref[0, 0].astype(bf16)
    vc = vc_ref[0, 0].astype(bf16)
    nrow = lax.broadcasted_iota(jnp.int32, (N_CMP_PAD, 1), 0)
    d_c = tq - (nrow * CMP_STRIDE + (CMP_BLOCK - 1))
    e, rl = softmax_cols(_dot(kc, qt) - c * d_c.astype(f32), d_c >= 0)
    p_c = e * rl
    oc_t = _dot_tn(vc, p_c.astype(bf16))
    imp_t = (p_c[:, 0:Q_BLOCK] + p_c[:, Q_BLOCK:2 * Q_BLOCK]
             + p_c[:, 2 * Q_BLOCK:3 * Q_BLOCK] + p_c[:, 3 * Q_BLOCK:4 * Q_BLOCK])
    hi = imp_t.astype(bf16)
    lo = (imp_t - hi.astype(f32)).astype(bf16)
    ps_t = _dot(mt_ref[...], hi) + _dot(mt_ref[...], lo)

    blk = lax.broadcasted_iota(jnp.int32, (N_SELBLK, Q_BLOCK), 0)
    jt = jnp.right_shift(q0 + lax.broadcasted_iota(jnp.int32, (N_SELBLK, Q_BLOCK), 1), SEL_BLOCK.bit_length() - 1)
    valid = blk <= jt
    forced = (blk == 0) | (blk == jt) | (blk == jt - 1)
    sel_t = jnp.where(valid & forced, 1.0, 0.0)
    score = jnp.where(valid & jnp.logical_not(forced), ps_t, NEG)
    for _ in range(N_SEL - N_FORCED):
        best = jnp.max(score, axis=0, keepdims=True)
        idx = jnp.min(jnp.where(score == best, blk, N_SELBLK), axis=0, keepdims=True)
        hit = blk == idx
        sel_t = jnp.where(hit & (best > NEG * 0.5), 1.0, sel_t)
        score = jnp.where(hit, REMOVED, score)
    bias_ref[...] = (sel_t - 1.0) * MASK_BIG

    blk_per_chunk = SEL_CHUNK // SEL_BLOCK
    n_chunk_max = vts_ref.shape[2]
    for ck in range(n_chunk_max):
        picked = jnp.max(sel_t[ck * blk_per_chunk:(ck + 1) * blk_per_chunk, :])
        used_ref[ck] = (picked > 0.5).astype(jnp.int32)

    c_hi = c.astype(bf16).astype(f32)
    c_mid = (c - c_hi).astype(bf16).astype(f32)
    c_lo = (c - c_hi - c_mid).astype(bf16).astype(f32)
    arow = lax.broadcasted_iota(jnp.int32, (AUG_ROWS // 2, cols), 0)
    alibi_rows = jnp.where(arow >= 6, 0.0,
                           jnp.where((arow == 0) | (arow == 3), c_hi, jnp.where((arow == 1) | (arow == 4), c_mid, c_lo)))
    qa_ref[0:HEAD_DIM, :] = qt
    qa_ref[HEAD_DIM + AUG_ROWS:, :] = jnp.zeros((qa_ref.shape[0] - HEAD_DIM - AUG_ROWS, cols), bf16)
    acc_ref[...] = jnp.zeros_like(acc_ref)
    mg_ref[...] = jnp.full_like(mg_ref, NEG)
    krow = lax.broadcasted_iota(jnp.int32, (SEL_CHUNK, 1), 0)

    def sel_chunk(ck, causal):
        k0 = pl.multiple_of(ck * SEL_CHUNK, SEL_CHUNK)
        b8 = bias_ref[pl.ds(pl.multiple_of(ck * blk_per_chunk, blk_per_chunk), blk_per_chunk), :]
        qa_ref[HEAD_DIM:HEAD_DIM + AUG_ROWS, :] = jnp.concatenate(
            [jnp.concatenate([b8] * NSA_GROUP, axis=1), alibi_rows], axis=0).astype(bf16)
        s = _dot(ks_ref[0, 0, pl.ds(k0, SEL_CHUNK), :], qa_ref[...])
        if causal:
            s = jnp.where(k0 + krow <= tq, s, NEG)
        shift = c * k0.astype(f32)
        m_old = mg_ref[...] - shift
        m_new = jnp.maximum(m_old, jnp.max(s, axis=0, keepdims=True))
        p = jnp.exp2(s - m_new).astype(bf16)
        acc_ref[...] = acc_ref[...] * jnp.exp2(m_old - m_new) + _dot(vts_ref[0, 0, ck], p)
        mg_ref[...] = m_new + shift

    def past_chunk(ck, carry):
        @pl.when(used_ref[ck] > 0)
        def _():
            sel_chunk(ck, causal=False)
        return carry

    last = q0 // SEL_CHUNK
    sel_chunk(last, causal=True)

    w0 = pl.multiple_of(jnp.maximum(q0 - WINDOW, 0), Q_BLOCK)
    kw = kw_ref[0, 0, pl.ds(w0, WIN_KEYS), :]
    q_pad = jnp.concatenate([qt, jnp.zeros((LANES - HEAD_DIM, cols), bf16)], axis=0)
    wrow = lax.broadcasted_iota(jnp.int32, (WIN_KEYS, 1), 0)
    d_w = tq - (w0 + wrow)
    e, _ = softmax_cols(_dot(kw, q_pad) - c * d_w.astype(f32), (d_w >= 0) & (d_w <= WINDOW))
    e = e.astype(bf16)
    t0 = w0 // Q_BLOCK
    acc_w = jnp.zeros((VT_ROWS, cols), f32)
    for t in range(WIN_KEYS // Q_BLOCK):
        acc_w = acc_w + _dot(vtw_ref[0, 0, t0 + t], e[t * Q_BLOCK:(t + 1) * Q_BLOCK])
    ow_t = acc_w[0:HEAD_DIM] * (1.0 / jnp.maximum(acc_w[HEAD_DIM:HEAD_DIM + 1], 1e-30))

    lax.fori_loop(0, last, past_chunk, 0)
    acc = acc_ref[...]
    os_t = acc[0:HEAD_DIM] * (1.0 / jnp.maximum(acc[HEAD_DIM:HEAD_DIM + 1], 1e-30))

    gate_t = _sigmoid(gate_ref[...]).T

    def gate_row(r, j):
        lo_row = gate_t[r * 3 + j:r * 3 + j + 1, :]
        hi_row = gate_t[(NSA_GROUP + r) * 3 + j:(NSA_GROUP + r) * 3 + j + 1, :]
        return jnp.where(g == 0, lo_row, hi_row)

    outs = []
    for r in range(NSA_GROUP):
        cs = slice(r * Q_BLOCK, (r + 1) * Q_BLOCK)
        outs.append(gate_row(r, 0) * oc_t[:, cs] + gate_row(r, 1) * os_t[:, cs] + gate_row(r, 2) * ow_t[:, cs])
    for pair in range(NSA_GROUP // 2):
        both = jnp.concatenate([outs[2 * pair], outs[2 * pair + 1]], axis=0)
        o_ref[:, pair * LANES:(pair + 1) * LANES] = both.T


def _nsa_prompt_t(q_t, z, comp, k_slc, vt_slc, k_win, vt_win, m_t):
    bsz, _, nq = q_t.shape[:3]
    seq = k_slc.shape[2]
    cols = NSA_GROUP * Q_BLOCK
    gate_blk = COL_GATE // LANES
    per_bg = lambda b, g, i: (b, g, 0, 0)
    per_bg5 = lambda b, g, i: (b, g, 0, 0, 0)
    return pl.pallas_call(
        _nsa_prompt_t_kernel,
        grid=(bsz, NSA_KV_HEADS, nq),
        in_specs=[pl.BlockSpec((1, 1, 1, HEAD_DIM, cols), lambda b, g, i: (b, g, i, 0, 0)),
                  pl.BlockSpec((Q_BLOCK, LANES), lambda b, g, i: (b * nq + i, gate_blk)),
                  pl.BlockSpec((1, 1, N_CMP_PAD, HEAD_DIM), per_bg),
                  pl.BlockSpec((1, 1, N_CMP_PAD, HEAD_DIM), lambda b, g, i: (b, NSA_KV_HEADS + g, 0, 0)),
                  pl.BlockSpec((1, 1, seq, LANES), per_bg),
                  pl.BlockSpec((1, 1) + vt_slc.shape[2:], per_bg5),
                  pl.BlockSpec((1, 1, seq, LANES), per_bg),
                  pl.BlockSpec((1, 1) + vt_win.shape[2:], per_bg5),
                  pl.BlockSpec((N_SELBLK, N_CMP_PAD), lambda b, g, i: (0, 0))],
        out_specs=pl.BlockSpec((Q_BLOCK, NSA_GROUP * HEAD_DIM), lambda b, g, i: (b * nq + i, g)),
        out_shape=jax.ShapeDtypeStruct((bsz * seq, NSA_WIDTH), f32),
        scratch_shapes=[pltpu.VMEM((LANES, cols), bf16),
                        pltpu.VMEM((VT_ROWS, cols), f32),
                        pltpu.VMEM((1, cols), f32),
                        pltpu.VMEM((N_SELBLK, Q_BLOCK), f32),
                        pltpu.SMEM((vt_slc.shape[2],), jnp.int32)],
        compiler_params=_cparams(("parallel", "parallel", "arbitrary")),
        name="nsa_prompt",
    )(q_t, z, comp, comp, k_slc, vt_slc, k_win, vt_win, m_t)


def _nsa_prompt_pair_kernel(qt_ref, gate_ref, comp_ref, caug_ref, ks_ref, vts_ref, kw_ref, vtw_ref, mt_ref, o_ref,
                            qa_ref, acc_ref, mg_ref, bias_ref, need_ref):
    qi = pl.program_id(1)
    q0 = qi * Q_BLOCK
    cols = NSA_GROUP * Q_BLOCK
    heads = range(NSA_KV_HEADS)

    lane = lax.broadcasted_iota(jnp.int32, (1, cols), 1)
    head = jnp.right_shift(lane, Q_BLOCK.bit_length() - 1)
    tq = q0 + jnp.bitwise_and(lane, Q_BLOCK - 1)
    blk = lax.broadcasted_iota(jnp.int32, (N_SELBLK, Q_BLOCK), 0)
    jt = jnp.right_shift(q0 + lax.broadcasted_iota(jnp.int32, (N_SELBLK, Q_BLOCK), 1), SEL_BLOCK.bit_length() - 1)
    valid = blk <= jt
    forced = (blk == 0) | (blk == jt) | (blk == jt - 1)
    c_end = lax.broadcasted_iota(jnp.int32, (N_CMP_PAD, 1), 0) * CMP_STRIDE + (CMP_BLOCK - 1)
    krow = lax.broadcasted_iota(jnp.int32, (SEL_CHUNK, 1), 0)
    arow = lax.broadcasted_iota(jnp.int32, (AUG_ROWS // 2, cols), 0)
    blk_per_chunk = SEL_CHUNK // SEL_BLOCK
    n_chunk_max = vts_ref.shape[2]
    last = q0 // SEL_CHUNK
    w0 = pl.multiple_of(jnp.maximum(q0 - WINDOW, 0), Q_BLOCK)
    wkey = w0 + lax.broadcasted_iota(jnp.int32, (WIN_KEYS, 1), 0)
    in_window = (wkey <= tq) & (wkey >= tq - WINDOW)
    gate_t = _sigmoid(gate_ref[...]).T

    def softmax_cols(s, mask):
        s = jnp.where(mask, s, NEG)
        m = jnp.max(s, axis=0, keepdims=True)
        e = jnp.where(mask, jnp.exp2(s - m), 0.0)
        return e, 1.0 / jnp.maximum(jnp.sum(e, axis=0, keepdims=True), 1e-30)

    slope = [jnp.exp2(-(head.astype(f32) + 1.0 + NSA_GROUP * g)) * LOG2E for g in heads]
    qts = [qt_ref[0, g, 0] for g in heads]

    def chunk_scores(g, ck, slot=0, live=None):
        k0 = pl.multiple_of(ck * SEL_CHUNK, SEL_CHUNK)
        b8 = bias_ref[g, pl.ds(pl.multiple_of(ck * blk_per_chunk, blk_per_chunk), blk_per_chunk), :]
        if live is not None:
            b8 = jnp.where(live, b8, -MASK_BIG)
        qa_ref[g, slot, HEAD_DIM:HEAD_DIM + AUG_ROWS, :] = jnp.concatenate(
            [jnp.concatenate([b8] * NSA_GROUP, axis=1), alibi_rows[g]], axis=0).astype(bf16)
        return _dot(ks_ref[0, g, pl.ds(k0, SEL_CHUNK), :], qa_ref[g, slot])

    def sel_chunk(g, ck, causal, s=None):
        c = slope[g]
        k0 = pl.multiple_of(ck * SEL_CHUNK, SEL_CHUNK)
        if s is None:
            s = chunk_scores(g, ck)
        if causal:
            s = jnp.where(k0 + krow <= tq, s, NEG)
        shift = c * k0.astype(f32)
        m_old = mg_ref[g] - shift
        m_new = jnp.maximum(m_old, jnp.max(s, axis=0, keepdims=True))
        p = jnp.exp2(s - m_new).astype(bf16)
        acc_ref[g] = acc_ref[g] * jnp.exp2(m_old - m_new) + _dot(vts_ref[0, g, ck], p)
        mg_ref[g] = m_new + shift

    alibi_rows, s_cmp, s_win = [], [], []
    for g in heads:
        c, qt = slope[g], qts[g]
        c_hi = c.astype(bf16).astype(f32)
        c_mid = (c - c_hi).astype(bf16).astype(f32)
        c_lo = (c - c_hi - c_mid).astype(bf16).astype(f32)
        c3 = jnp.where((arow == 0) | (arow == 3), c_hi, jnp.where((arow == 1) | (arow == 4), c_mid, c_lo))
        alibi_rows.append(jnp.where(arow >= 6, 0.0, c3))
        pos_rows = jnp.where(arow >= 6, 0.0, jnp.where(arow >= 3, KEY_LOW * c3, c3))
        pos_rows = jnp.concatenate([pos_rows, jnp.zeros_like(pos_rows)], axis=0).astype(bf16)
        q_pos = jnp.concatenate([qt, pos_rows, jnp.zeros((LANES - HEAD_DIM - AUG_ROWS, cols), bf16)], axis=0)
        kc = jnp.concatenate([comp_ref[0, g].astype(bf16), caug_ref[...]], axis=1)
        s_cmp.append(_dot(kc, q_pos))
        kw = kw_ref[0, g, pl.ds(w0, WIN_KEYS), :]
        s_win.append(_dot(kw, q_pos))

    oc_t, ps_t = [], []
    for g in heads:
        e, rl = softmax_cols(s_cmp[g], c_end <= tq)
        p_c = e * rl
        oc_t.append(_dot_tn(comp_ref[0, NSA_KV_HEADS + g].astype(bf16), p_c.astype(bf16)))
        imp_t = (p_c[:, 0:Q_BLOCK] + p_c[:, Q_BLOCK:2 * Q_BLOCK]
                 + p_c[:, 2 * Q_BLOCK:3 * Q_BLOCK] + p_c[:, 3 * Q_BLOCK:4 * Q_BLOCK])
        hi = imp_t.astype(bf16)
        lo = (imp_t - hi.astype(f32)).astype(bf16)
        ps_t.append(_dot(mt_ref[...], hi) + _dot(mt_ref[...], lo))

    ow_t = []
    t0 = w0 // Q_BLOCK
    for g in heads:
        s_w = jnp.where(in_window, s_win[g], NEG)
        e = jnp.exp2(s_w - jnp.max(s_w, axis=0, keepdims=True)).astype(bf16)
        acc_w = jnp.zeros((VT_ROWS, cols), f32)
        for t in range(WIN_KEYS // Q_BLOCK):
            acc_w = acc_w + _dot(vtw_ref[0, g, t0 + t], e[t * Q_BLOCK:(t + 1) * Q_BLOCK])
        ow_t.append(acc_w[0:HEAD_DIM] * (1.0 / jnp.maximum(acc_w[HEAD_DIM:HEAD_DIM + 1], 1e-30)))

    sel_t = [jnp.where(valid & forced, 1.0, 0.0)] * NSA_KV_HEADS
    score = [jnp.where(valid & jnp.logical_not(forced), ps_t[g], NEG) for g in heads]
    for _ in range(N_SEL - N_FORCED):
        for g in heads:
            best = jnp.max(score[g], axis=0, keepdims=True)
            idx = jnp.min(jnp.where(score[g] == best, blk, N_SELBLK), axis=0, keepdims=True)
            hit = blk == idx
            sel_t[g] = jnp.where(hit & (best > NEG * 0.5), 1.0, sel_t[g])
            score[g] = jnp.where(hit, REMOVED, score[g])

    for g in heads:
        bias_ref[g] = (sel_t[g] - 1.0) * MASK_BIG
        for slot in range(qa_ref.shape[1]):
            qa_ref[g, slot, 0:HEAD_DIM, :] = qts[g]
            qa_ref[g, slot, HEAD_DIM + AUG_ROWS:, :] = jnp.zeros((LANES - HEAD_DIM - AUG_ROWS, cols), bf16)
        acc_ref[g] = jnp.zeros(acc_ref.shape[1:], f32)
        mg_ref[g] = jnp.full(mg_ref.shape[1:], NEG, f32)
    own = [chunk_scores(g, last) for g in heads]
    for g in heads:
        sel_chunk(g, last, causal=True, s=own[g])

    n_need = jnp.int32(0)
    for ck in range(n_chunk_max):
        picked = jnp.maximum(*[jnp.max(sel_t[g][ck * blk_per_chunk:(ck + 1) * blk_per_chunk, :]) for g in heads])
        need_ref[n_need] = ck
        n_need = n_need + ((picked > 0.5) & (ck < last)).astype(jnp.int32)

    def chunk_pair(i, carry):
        first = 2 * i
        live = first + 1 < n_need
        cks = [need_ref[first], need_ref[jnp.minimum(first + 1, n_need - 1)]]
        scores = [[chunk_scores(g, cks[slot], slot, live if slot else None) for g in heads] for slot in range(2)]
        for slot in range(2):
            for g in heads:
                sel_chunk(g, cks[slot], causal=False, s=scores[slot][g])
        return carry

    lax.fori_loop(0, (n_need + 1) // 2, chunk_pair, 0)

    for g in heads:
        acc = acc_ref[g]
        os_t = acc[0:HEAD_DIM] * (1.0 / jnp.maximum(acc[HEAD_DIM:HEAD_DIM + 1], 1e-30))
        outs = []
        for r in range(NSA_GROUP):
            cs = slice(r * Q_BLOCK, (r + 1) * Q_BLOCK)
            row = (g * NSA_GROUP + r) * 3
            outs.append(gate_t[row:row + 1, :] * oc_t[g][:, cs] + gate_t[row + 1:row + 2, :] * os_t[:, cs]
                        + gate_t[row + 2:row + 3, :] * ow_t[g][:, cs])
        for pair in range(NSA_GROUP // 2):
            both = jnp.concatenate([outs[2 * pair], outs[2 * pair + 1]], axis=0)
            lane0 = (g * NSA_GROUP // 2 + pair) * LANES
            o_ref[:, lane0:lane0 + LANES] = both.T


def _nsa_prompt_pair(q_t, z, comp, k_slc, vt_slc, k_win, vt_win, m_t):
    bsz, _, nq = q_t.shape[:3]
    seq = k_slc.shape[2]
    cols = NSA_GROUP * Q_BLOCK
    gate_blk = COL_GATE // LANES
    per_b4 = lambda b, i: (b, 0, 0, 0)
    per_b5 = lambda b, i: (b, 0, 0, 0, 0)
    return pl.pallas_call(
        _nsa_prompt_pair_kernel,
        grid=(bsz, nq),
        in_specs=[pl.BlockSpec((1, NSA_KV_HEADS, 1, HEAD_DIM, cols), lambda b, i: (b, 0, i, 0, 0)),
                  pl.BlockSpec((Q_BLOCK, LANES), lambda b, i: (b * nq + i, 0)),
                  pl.BlockSpec((1, N_COMBO, N_CMP_PAD, HEAD_DIM), per_b4),
                  pl.BlockSpec((N_CMP_PAD, HEAD_DIM), lambda b, i: (0, 0)),
                  pl.BlockSpec((1, NSA_KV_HEADS, seq, LANES), per_b4),
                  pl.BlockSpec((1,) + vt_slc.shape[1:], per_b5),
                  pl.BlockSpec((1, NSA_KV_HEADS, seq, LANES), per_b4),
                  pl.BlockSpec((1,) + vt_win.shape[1:], per_b5),
                  pl.BlockSpec((N_SELBLK, N_CMP_PAD), lambda b, i: (0, 0))],
        out_specs=pl.BlockSpec((Q_BLOCK, NSA_WIDTH), lambda b, i: (b * nq + i, 0)),
        out_shape=jax.ShapeDtypeStruct((bsz * seq, NSA_WIDTH), f32),
        scratch_shapes=[pltpu.VMEM((NSA_KV_HEADS, 2, LANES, cols), bf16),
                        pltpu.VMEM((NSA_KV_HEADS, VT_ROWS, cols), f32),
                        pltpu.VMEM((NSA_KV_HEADS, 1, cols), f32),
                        pltpu.VMEM((NSA_KV_HEADS, N_SELBLK, Q_BLOCK), f32),
                        pltpu.SMEM((vt_slc.shape[2],), jnp.int32)],
        compiler_params=_cparams(("parallel", "arbitrary")),
        name="nsa_prompt",
    )(q_t, z, comp, _cmp_position_columns(), k_slc, vt_slc, k_win, vt_win, m_t)


def _position_columns(pos):
    low, high = (pos % KEY_LOW)[:, None], (pos // KEY_LOW)[:, None]
    return np.concatenate([low, low, low, high, high, high], axis=1).astype(np.float32)


def _cmp_position_columns():
    aug = np.zeros((N_CMP_PAD, HEAD_DIM), np.float32)
    aug[:, 0:6] = _position_columns(np.arange(N_CMP_PAD) * CMP_STRIDE + CMP_BLOCK - 1)
    return jnp.asarray(aug, dtype=bf16)


LAYOUT_ROWS = 1024


def _attention_layout_kernel(q_ref, slc_ref, win_ref, xs_ref, xw_ref, qt_ref, ks_ref, vts_ref, kw_ref, vtw_ref):
    tm = q_ref.shape[0]
    q = q_ref[...] * (HEAD_DIM ** -0.5 * LOG2E)
    for qb in range(tm // Q_BLOCK):
        rows = slice(qb * Q_BLOCK, (qb + 1) * Q_BLOCK)
        for pair in range(NSA_HEADS // 2):
            t = q[rows, pair * LANES:(pair + 1) * LANES].T
            for k in range(2):
                g, r = divmod(2 * pair + k, NSA_GROUP)
                qt_ref[0, g, qb, :, r * Q_BLOCK:(r + 1) * Q_BLOCK] = t[k * HEAD_DIM:(k + 1) * HEAD_DIM].astype(bf16)

    for src_ref, extra_ref, k_ref, vt_ref in ((slc_ref, xs_ref, ks_ref, vts_ref), (win_ref, xw_ref, kw_ref, vtw_ref)):
        tile = vt_ref.shape[4]
        s = src_ref[...]
        v_t = s[:, KV_WIDTH:].T
        tail = jnp.where(lax.broadcasted_iota(jnp.int32, (VT_ROWS - HEAD_DIM, tile), 0) == 0, 1.0, 0.0)
        for g in range(NSA_KV_HEADS):
            k_ref[0, g] = jnp.concatenate([s[:, g * HEAD_DIM:(g + 1) * HEAD_DIM].astype(bf16), extra_ref[...]], axis=1)
            for c in range(tm // tile):
                blk = v_t[g * HEAD_DIM:(g + 1) * HEAD_DIM, c * tile:(c + 1) * tile]
                vt_ref[0, g, c] = jnp.concatenate([blk, tail], axis=0).astype(bf16)


def _attention_layout(z, slc, win, bsz, seq):
    tm = LAYOUT_ROWS
    per_b = seq // tm
    pos = np.arange(seq)
    xs = np.zeros((seq, HEAD_DIM), np.float32)
    xs[pos, (pos // SEL_BLOCK) % (SEL_CHUNK // SEL_BLOCK)] = 1.0
    xs[:, 8:11] = (pos % KEY_LOW)[:, None]
    xs[:, 11:14] = (pos % SEL_CHUNK - pos % KEY_LOW)[:, None]
    xw = np.zeros((seq, HEAD_DIM), np.float32)
    xw[:, 0:6] = _position_columns(pos)
    row = lambda i: (i, 0)
    rep = lambda i: (i % per_b, 0)
    out4 = lambda i: (i // per_b, 0, i % per_b, 0)
    out5 = lambda i: (i // per_b, 0, i % per_b, 0, 0)
    k_shape = jax.ShapeDtypeStruct((bsz, NSA_KV_HEADS, seq, LANES), bf16)
    vt_shape = lambda tile: jax.ShapeDtypeStruct((bsz, NSA_KV_HEADS, seq // tile, VT_ROWS, tile), bf16)
    vt_spec = lambda tile: pl.BlockSpec((1, NSA_KV_HEADS, tm // tile, VT_ROWS, tile), out5)
    return pl.pallas_call(
        _attention_layout_kernel,
        grid=(bsz * per_b,),
        in_specs=[pl.BlockSpec((tm, NSA_WIDTH), row), pl.BlockSpec((tm, ROW_WIDTH), row),
                  pl.BlockSpec((tm, ROW_WIDTH), row),
                  pl.BlockSpec((tm, HEAD_DIM), rep), pl.BlockSpec((tm, HEAD_DIM), rep)],
        out_specs=[pl.BlockSpec((1, NSA_KV_HEADS, tm // Q_BLOCK, HEAD_DIM, NSA_GROUP * Q_BLOCK), out5),
                   pl.BlockSpec((1, NSA_KV_HEADS, tm, LANES), out4), vt_spec(SEL_CHUNK),
                   pl.BlockSpec((1, NSA_KV_HEADS, tm, LANES), out4), vt_spec(Q_BLOCK)],
        out_shape=[jax.ShapeDtypeStruct((bsz, NSA_KV_HEADS, seq // Q_BLOCK, HEAD_DIM, NSA_GROUP * Q_BLOCK), bf16),
                   k_shape, vt_shape(SEL_CHUNK), k_shape, vt_shape(Q_BLOCK)],
        compiler_params=_cparams(("parallel",)),
        name="attention_layout",
    )(z, slc, win, jnp.asarray(xs, dtype=bf16), jnp.asarray(xw, dtype=bf16))


def _attention_keys(piece, bsz, seq, augment):
    tile = SEL_CHUNK if augment else Q_BLOCK
    t = piece.reshape(bsz, seq, 2, NSA_KV_HEADS, HEAD_DIM)
    k = t[:, :, 0].transpose(0, 2, 1, 3)
    pos = np.arange(seq)
    extra = np.zeros((seq, LANES - HEAD_DIM), np.float32)
    if augment:
        extra[np.arange(seq), (pos // SEL_BLOCK) % (SEL_CHUNK // SEL_BLOCK)] = 1.0
        extra[:, 8:11] = (pos % KEY_LOW)[:, None]
        extra[:, 11:14] = (pos % SEL_CHUNK - pos % KEY_LOW)[:, None]
    else:
        extra[:, 0:6] = _position_columns(pos)
    k_aug = jnp.concatenate([k, jnp.broadcast_to(jnp.asarray(extra), k.shape[:2] + extra.shape)], axis=-1)
    v_t = t[:, :, 1].transpose(0, 2, 3, 1)
    ones = jnp.ones(v_t.shape[:2] + (1, seq), f32)
    zeros = jnp.zeros(v_t.shape[:2] + (VT_ROWS - HEAD_DIM - 1, seq), f32)
    v_aug = jnp.concatenate([v_t, ones, zeros], axis=2)
    v_aug = v_aug.reshape(bsz, NSA_KV_HEADS, VT_ROWS, seq // tile, tile).transpose(0, 1, 3, 2, 4)
    return k_aug.astype(bf16), v_aug.astype(bf16)


HG_CHUNK = 128
HG_BAND = 8


def _lower_bound(lo):
    m = jnp.max(lo, axis=0, keepdims=True)
    e = jnp.exp(lo - m)
    return e[0:1] / jnp.sum(e, axis=0, keepdims=True)


def _hgrn_gates(q, fl, lb):
    sg = _sigmoid(fl)
    f = lb + (1.0 - lb) * sg
    k = (1.0 - lb) * (1.0 - sg)
    return _silu(q), k, jnp.log(f)


def _hgrn_out(o, nw, zb):
    o = o * lax.rsqrt(jnp.mean(o * o, axis=-1, keepdims=True) + EPS) * nw
    return o * _silu(zb)


def _hgrn_prompt_kernel(q_ref, f_ref, i_ref, zb_ref, lo_ref, nw_ref, y_ref, s_ref, st_ref):
    c = pl.program_id(1)

    @pl.when(c == 0)
    def _():
        st_ref[...] = jnp.zeros_like(st_ref)

    n = HG_CHUNK
    row = lax.broadcasted_iota(jnp.int32, (n, n), 0)
    col = lax.broadcasted_iota(jnp.int32, (n, n), 1)
    heads = [slice(h * HG_DIM, (h + 1) * HG_DIM) for h in range(HG_HEADS)]

    qq, kk, logf = _hgrn_gates(q_ref[...], f_ref[...], _lower_bound(lo_ref[...]))
    b = logf
    trow = lax.broadcasted_iota(jnp.int32, (n, 1), 0)
    step = 1
    while step < n:
        b = b + jnp.where(trow >= step, pltpu.roll(b, step, axis=0), 0.0)
        step *= 2

    intra = [_hgrn_intra(qq[:, hs], kk[:, hs], b[:, hs], row, col) for hs in heads]
    v = i_ref[...]
    for h, hs in enumerate(heads):
        st = st_ref[h]
        bh, kh, vh = b[:, hs], kk[:, hs], v[:, hs].astype(bf16)
        o = _dot(intra[h].astype(bf16), vh) + _dot_nt((qq[:, hs] * jnp.exp(bh)).astype(bf16), st.astype(bf16))
        b_last = bh[n - 1:n]
        st_new = st * jnp.exp(b_last) + _dot_tn(vh, (kh * jnp.exp(b_last - bh)).astype(bf16))
        st_ref[h] = st_new
        y_ref[:, hs] = _hgrn_out(o, nw_ref[...], zb_ref[:, hs])

        @pl.when(c == pl.num_programs(1) - 1)
        def _():
            s_ref[0, h] = st_new.T


def _hgrn_intra(qq, kk, b, row, col):
    n = HG_CHUNK
    a = jnp.zeros((n, n), f32)
    sub = jnp.bitwise_and(row, HG_BAND - 1)
    for d in range(HG_BAND):
        if d == 0:
            term = qq * kk
        else:
            ok = sub >= d
            decay = jnp.exp(jnp.where(ok, b - pltpu.roll(b, d, axis=0), 0.0))
            term = jnp.where(ok, qq * pltpu.roll(kk, d, axis=0) * decay, 0.0)
        a = a + jnp.where(col == row - d, jnp.sum(term, axis=1, keepdims=True), 0.0)

    size = 2 * HG_BAND
    while size <= n:
        half = size // 2
        qs, ks = [], []
        zero = jnp.zeros((half, HG_DIM), f32)
        for r0 in range(0, n, size):
            beta = b[r0 + half - 1:r0 + half]
            ks += [kk[r0:r0 + half] * jnp.exp(beta - b[r0:r0 + half]), zero]
            qs += [zero, qq[r0 + half:r0 + size] * jnp.exp(b[r0 + half:r0 + size] - beta)]
        part = _dot_nt(jnp.concatenate(qs, axis=0).astype(bf16), jnp.concatenate(ks, axis=0).astype(bf16))
        if size < n:
            sh = size.bit_length() - 1
            part = jnp.where(jnp.right_shift(row, sh) == jnp.right_shift(col, sh), part, 0.0)
        a = a + part
        size *= 2
    return a


def _hgrn_prompt(z, hg_lower, hg_norm, bsz, seq):
    nch = seq // HG_CHUNK

    def zcol(base):
        return pl.BlockSpec((HG_CHUNK, HG_WIDTH), lambda b, c: (b * nch + c, base // HG_WIDTH))

    return pl.pallas_call(
        _hgrn_prompt_kernel,
        grid=(bsz, nch),
        in_specs=[zcol(COL_QB), zcol(COL_FB), zcol(COL_IB), zcol(COL_ZB),
                  pl.BlockSpec(hg_lower.shape, lambda b, c: (0, 0)),
                  pl.BlockSpec((1, HG_DIM), lambda b, c: (0, 0))],
        out_specs=[pl.BlockSpec((HG_CHUNK, HG_WIDTH), lambda b, c: (b * nch + c, 0)),
                   pl.BlockSpec((1, HG_HEADS, HG_DIM, HG_DIM), lambda b, c: (b, 0, 0, 0))],
        out_shape=[jax.ShapeDtypeStruct((bsz * seq, HG_WIDTH), f32),
                   jax.ShapeDtypeStruct((bsz, HG_HEADS, HG_DIM, HG_DIM), f32)],
        scratch_shapes=[pltpu.VMEM((HG_HEADS, HG_DIM, HG_DIM), f32)],
        compiler_params=_cparams(("parallel", "arbitrary")),
        name="hgrn_prompt",
    )(z, z, z, z, hg_lower, hg_norm)


def _hgrn_sample_kernel(q_ref, f_ref, i_ref, zb_ref, lo_ref, nw_ref, s_ref, y_ref, so_ref, o_scr):
    nb = q_ref.shape[0]
    lb = _lower_bound(lo_ref[...])
    qq, kk, logf = _hgrn_gates(q_ref[...], f_ref[...], lb)
    v = i_ref[...]
    pad = jnp.zeros((HG_DIM - nb, HG_DIM), f32)
    to_cols = lambda x: jnp.concatenate([x, pad], axis=0).T
    f_t, k_t, q_t = to_cols(jnp.exp(logf)), to_cols(kk), to_cols(qq)
    for b in range(nb):
        s_new = f_t[:, b:b + 1] * s_ref[0, b, 0] + k_t[:, b:b + 1] * v[b:b + 1, :]
        so_ref[0, b, 0] = s_new
        o_scr[b:b + 1, :] = jnp.sum(s_new * q_t[:, b:b + 1], axis=0, keepdims=True)
    y_ref[...] = _hgrn_out(o_scr[...], nw_ref[...], zb_ref[...])


def _hgrn_sample(z, state, hg_lower, hg_norm):
    nb = z.shape[0]

    def zcol(base):
        return pl.BlockSpec((nb, HG_DIM), lambda h: (0, base // HG_DIM + h))

    sblk = pl.BlockSpec((1, nb, 1, HG_DIM, HG_DIM), lambda h: (0, 0, h, 0, 0))
    return pl.pallas_call(
        _hgrn_sample_kernel,
        grid=(HG_HEADS,),
        in_specs=[zcol(COL_QB), zcol(COL_FB), zcol(COL_IB), zcol(COL_ZB),
                  pl.BlockSpec((hg_lower.shape[0], HG_DIM), lambda h: (0, h)),
                  pl.BlockSpec((1, HG_DIM), lambda h: (0, 0)),
                  sblk],
        out_specs=[pl.BlockSpec((nb, HG_DIM), lambda h: (0, h)), sblk],
        out_shape=[jax.ShapeDtypeStruct((nb, HG_WIDTH), f32),
                   jax.ShapeDtypeStruct(state.shape, f32)],
        scratch_shapes=[pltpu.VMEM((nb, HG_DIM), f32)],
        compiler_params=_cparams(("parallel",)),
        name="hgrn_sample",
    )(z, z, z, z, hg_lower, hg_norm, state)


N_PICK = N_SEL - 1


def _head_slopes():
    h = lax.broadcasted_iota(jnp.int32, (NSA_HEADS, 1), 0).astype(f32)
    return jnp.exp2(-(h + 1.0))


def _nsa_sample_cmp_kernel(q_ref, comp_ref, m_ref, oc_ref, ps_ref, *, t_pos):
    q = q_ref[0] * jnp.asarray(HEAD_DIM ** -0.5, bf16)
    slope = _head_slopes()
    head = lax.broadcasted_iota(jnp.int32, (NSA_HEADS, 1), 0)
    ncol = lax.broadcasted_iota(jnp.int32, (1, N_CMP_PAD), 1)
    d_c = t_pos - (ncol * CMP_STRIDE + (CMP_BLOCK - 1))
    mask = d_c >= 0
    o_c = jnp.zeros((NSA_HEADS, HEAD_DIM), f32)
    ps_ref[...] = jnp.zeros_like(ps_ref)
    for g in range(NSA_KV_HEADS):
        kc = comp_ref[0, g].astype(bf16)
        vc = comp_ref[0, NSA_KV_HEADS + g].astype(bf16)
        s = jnp.where(mask, _dot_nt(q, kc) - slope * d_c.astype(f32), NEG)
        m = jnp.max(s, axis=1, keepdims=True)
        e = jnp.where(mask, jnp.exp(s - m), 0.0)
        p = e * (1.0 / jnp.maximum(jnp.sum(e, axis=1, keepdims=True), 1e-30))
        mine = jnp.right_shift(head, NSA_GROUP.bit_length() - 1) == g
        o_c = jnp.where(mine, _dot(p.astype(bf16), vc), o_c)
        imp = jnp.sum(jnp.where(mine, p, 0.0), axis=0, keepdims=True)
        ps_ref[0, g:g + 1, :] = _importance_to_blocks(imp, m_ref[...])
    oc_ref[0] = o_c


def _nsa_sample_cmp(q8, comp, m_mat, t_pos):
    nb = q8.shape[0]
    return pl.pallas_call(
        functools.partial(_nsa_sample_cmp_kernel, t_pos=t_pos),
        grid=(nb,),
        in_specs=[pl.BlockSpec((1, NSA_HEADS, HEAD_DIM), lambda b: (b, 0, 0)),
                  pl.BlockSpec((1, N_COMBO, N_CMP_PAD, HEAD_DIM), lambda b: (b, 0, 0, 0)),
                  pl.BlockSpec((N_CMP_PAD, N_SELBLK), lambda b: (0, 0))],
        out_specs=[pl.BlockSpec((1, NSA_HEADS, HEAD_DIM), lambda b: (b, 0, 0)),
                   pl.BlockSpec((1, 8, N_SELBLK), lambda b: (b, 0, 0))],
        out_shape=[jax.ShapeDtypeStruct((nb, NSA_HEADS, HEAD_DIM), f32),
                   jax.ShapeDtypeStruct((nb, 8, N_SELBLK), f32)],
        compiler_params=_cparams(("parallel",)),
        name="nsa_sample_cmp",
    )(q8, comp, m_mat)


def _nsa_sample_topk_kernel(ps_ref, idx_ref, *, last_blk):
    ps = ps_ref[...]
    lane = lax.broadcasted_iota(jnp.int32, ps.shape, 1)
    score = ps + jnp.where((lane == 0) | (lane == last_blk), FORCE_BONUS, 0.0)
    out = jnp.zeros(ps.shape, jnp.int32)
    for r in range(N_PICK):
        best = jnp.max(score, axis=1, keepdims=True)
        idx = jnp.min(jnp.where(score == best, lane, N_SELBLK), axis=1, keepdims=True)
        out = jnp.where(lane == r, idx, out)
        score = jnp.where(lane == idx, REMOVED, score)
    idx_ref[...] = out


def _nsa_sample_topk(ps_rows, last_blk):
    return pl.pallas_call(
        functools.partial(_nsa_sample_topk_kernel, last_blk=last_blk),
        out_shape=jax.ShapeDtypeStruct(ps_rows.shape, jnp.int32),
        name="nsa_sample_topk",
    )(ps_rows)


def _nsa_sample_sel_kernel(pt_ref, idx_ref, slc_ref, q_ref, oc_ref, gate_ref, slc_new_ref, win_new_ref, wc_ref,
                           o_ref, wout_ref, kv_scr, sem_ref, *, t_pos):
    n_blk = NSA_KV_HEADS * N_PICK
    halves = PAGE_SIZE // SEL_BLOCK
    b = pl.program_id(0)
    nb = pl.num_programs(0)
    slot = lax.rem(b, 2)

    def picked(seq, k):
        return idx_ref[(seq * NSA_KV_HEADS + k // N_PICK) * N_SEL + k % N_PICK]

    def blk_copies(seq, s, k):
        page = pt_ref[seq, picked(seq, k) // halves]
        g = k // N_PICK
        return [pltpu.make_async_copy(slc_ref.at[page, kind * NSA_KV_HEADS + g], kv_scr.at[s, kind, k],
                                      sem_ref.at[s]) for kind in range(2)]

    def start_all(seq, s):
        for k in range(n_blk):
            for cp in blk_copies(seq, s, k):
                cp.start()

    @pl.when(b == 0)
    def _():
        start_all(0, 0)

    @pl.when(b + 1 < nb)
    def _():
        start_all(b + 1, 1 - slot)

    for k in range(n_blk):
        for cp in blk_copies(b, slot, k):
            cp.wait()

    q = q_ref[0] * jnp.asarray(HEAD_DIM ** -0.5, bf16)
    slope = _head_slopes()
    head = lax.broadcasted_iota(jnp.int32, (NSA_HEADS, 1), 0)
    lane = lax.broadcasted_iota(jnp.int32, (1, PAGE_SIZE), 1)
    new_slc = slc_new_ref[0]
    new_win = win_new_ref[0]
    wl = wc_ref.shape[3]
    wlane = lax.broadcasted_iota(jnp.int32, (1, wl), 1)
    d_w = (wl - wlane).astype(f32)
    qf = q.astype(f32)

    def attend(s, mask, v_t, new_row, g):
        k_new = new_row[:, g * HEAD_DIM:(g + 1) * HEAD_DIM].astype(bf16).astype(f32)
        v_new = new_row[:, KV_WIDTH + g * HEAD_DIM:KV_WIDTH + (g + 1) * HEAD_DIM].astype(bf16).astype(f32)
        s_new = jnp.sum(qf * k_new, axis=1, keepdims=True)
        m = jnp.maximum(jnp.max(s, axis=1, keepdims=True), s_new)
        e, e_new = jnp.exp(s - m), jnp.exp(s_new - m)
        if mask is not None:
            e = jnp.where(mask, e, 0.0)
        den = jnp.maximum(jnp.sum(e, axis=1, keepdims=True) + e_new, 1e-30)
        num = _dot_nt(e.astype(bf16), v_t.astype(bf16)) + e_new.astype(bf16).astype(f32) * v_new
        return num * (1.0 / den)

    o_s = jnp.zeros((NSA_HEADS, HEAD_DIM), f32)
    o_w = jnp.zeros((NSA_HEADS, HEAD_DIM), f32)
    for g in range(NSA_KV_HEADS):
        mine = jnp.right_shift(head, NSA_GROUP.bit_length() - 1) == g
        pos, member = [], []
        for n in range(N_PICK):
            j = picked(b, g * N_PICK + n)
            pos.append((j // halves) * PAGE_SIZE + lane)
            member.append(jnp.right_shift(lane, SEL_BLOCK.bit_length() - 1) == j % halves)
        pos = jnp.concatenate(pos, axis=1)
        member = jnp.concatenate(member, axis=1)
        k_t = jnp.concatenate([kv_scr[slot, 0, g * N_PICK + n] for n in range(N_PICK)], axis=1)
        v_t = jnp.concatenate([kv_scr[slot, 1, g * N_PICK + n] for n in range(N_PICK)], axis=1)
        s = jnp.where(member, _dot(q, k_t.astype(bf16)) - slope * (t_pos - pos).astype(f32), NEG)
        o_s = jnp.where(mine, attend(s, member, v_t, new_slc, g), o_s)
        s = _dot(q, wc_ref[0, g].astype(bf16)) - slope * d_w
        o_w = jnp.where(mine, attend(s, None, wc_ref[0, NSA_KV_HEADS + g], new_win, g), o_w)

    gate = jnp.broadcast_to(_sigmoid(gate_ref[0]), (NSA_HEADS, LANES))
    glane = lax.broadcasted_iota(jnp.int32, (NSA_HEADS, LANES), 1)
    gcol = [jnp.sum(jnp.where(glane == 3 * head + j, gate, 0.0), axis=1, keepdims=True) for j in range(3)]
    o_ref[0] = gcol[0] * oc_ref[0] + gcol[1] * o_s + gcol[2] * o_w
    eye = lax.broadcasted_iota(jnp.int32, (HEAD_DIM, HEAD_DIM), 0) == \
        lax.broadcasted_iota(jnp.int32, (HEAD_DIM, HEAD_DIM), 1)
    for c in range(N_COMBO):
        piece = jnp.broadcast_to(new_win[:, c * HEAD_DIM:(c + 1) * HEAD_DIM], (HEAD_DIM, HEAD_DIM))
        new_col = jnp.sum(jnp.where(eye, piece, 0.0), axis=1, keepdims=True)
        wout_ref[0, c] = jnp.where(wlane == wl - 1, new_col, pltpu.roll(wc_ref[0, c], wl - 1, axis=1))


def _nsa_sample_sel(page_table, idx_flat, slc_pages_t, q8, o_c, gates, slc_new, win_new, cache_win_t, t_pos):
    nb = q8.shape[0]
    wl = cache_win_t.shape[3]
    n_blk = NSA_KV_HEADS * N_PICK
    per_seq = lambda b, pt, idx: (b, 0, 0)
    win_blk = pl.BlockSpec((1, N_COMBO, HEAD_DIM, wl), lambda b, pt, idx: (b, 0, 0, 0))
    grid_spec = pltpu.PrefetchScalarGridSpec(
        num_scalar_prefetch=2,
        grid=(nb,),
        in_specs=[pl.BlockSpec(memory_space=pl.ANY),
                  pl.BlockSpec((1, NSA_HEADS, HEAD_DIM), per_seq),
                  pl.BlockSpec((1, NSA_HEADS, HEAD_DIM), per_seq),
                  pl.BlockSpec((1, 1, LANES), per_seq),
                  pl.BlockSpec((1, 1, 2 * KV_WIDTH), per_seq),
                  pl.BlockSpec((1, 1, 2 * KV_WIDTH), per_seq),
                  win_blk],
        out_specs=[pl.BlockSpec((1, NSA_HEADS, HEAD_DIM), per_seq), win_blk],
        scratch_shapes=[pltpu.VMEM((2, 2, n_blk, HEAD_DIM, PAGE_SIZE), f32),
                        pltpu.SemaphoreType.DMA((2,))],
    )
    return pl.pallas_call(
        functools.partial(_nsa_sample_sel_kernel, t_pos=t_pos),
        grid_spec=grid_spec,
        out_shape=[jax.ShapeDtypeStruct((nb, NSA_HEADS, HEAD_DIM), f32),
                   jax.ShapeDtypeStruct(cache_win_t.shape, f32)],
        compiler_params=_cparams(("arbitrary",)),
        name="nsa_sample_sel",
    )(page_table, idx_flat, slc_pages_t, q8, o_c, gates, slc_new, win_new, cache_win_t)


def kernel(x_prompt, x_sample, cache_cmp_kv, cache_slc_kv, cache_win_kv, state_hgrn, page_table, p_prompt, p_sample,
           w_in, g_pre, cmp_pe, cmp_w1, cmp_b1, cmp_w2, hg_lower, hg_norm, w_out, g_post, ple_proj, ple_gate):
    bsz, seq, _ = x_prompt.shape
    nb = x_sample.shape[0]
    n_pool = cache_cmp_kv.shape[1]
    n_pages = page_table.shape[1]
    past = n_pages * PAGE_SIZE
    nq = seq // Q_BLOCK
    kv_tail = (2, NSA_KV_HEADS, HEAD_DIM)

    o = _ORIG
    w = w_in[0]
    w_r = jnp.concatenate([w[:, o['q_a']:o['k_cmp']], w[:, o['z_a']:o['end']], w[:, o['k_cmp']:o['gate_a']],
                           w[:, o['gate_a']:o['z_a']],
                           jnp.zeros((D_MODEL, PROJ_WIDTH - o['end']), f32)], axis=1).astype(bf16)
    cw = _compress_weights(cmp_pe[0], cmp_w1[0], cmp_b1[0], cmp_w2[0])
    m_mat = _sel_map_matrix()
    w_out_b, ple_proj_b, ple_gate_b = w_out[0].astype(bf16), ple_proj[0].astype(bf16), ple_gate[0].astype(bf16)

    xp = x_prompt.reshape(bsz * seq, D_MODEL)
    zp, cmp_p, slc_p, win_p, gates_p = _proj(xp, g_pre, w_r, 1024)
    pages_p = cmp_p.reshape(bsz * seq // PAGE_SIZE, PAGE_SIZE, ROW_WIDTH)
    table_p = jnp.arange(bsz * seq // PAGE_SIZE, dtype=jnp.int32).reshape(bsz, seq // PAGE_SIZE)
    comp_p = _compress(pages_p, table_p, *cw, feature_major=False)
    q_t, k_slc, vt_slc, k_win, vt_win = _attention_layout(zp, slc_p, win_p, bsz, seq)
    o_a = _nsa_prompt_pair(q_t, gates_p, comp_p, k_slc, vt_slc, k_win, vt_win, m_mat.T)
    yb_p, hg_p = _hgrn_prompt(zp, hg_lower, hg_norm, bsz, seq)
    h_p = _finish(xp, o_a, zp, yb_p, p_prompt[0].reshape(bsz * seq, PLE_DIM),
                  w_out_b, g_post, ple_proj_b, ple_gate_b, 512)

    wl_p = min(WINDOW, seq)
    out_prompt = (h_p.reshape(bsz, seq, D_MODEL),
                  cmp_p.reshape((1, bsz, seq) + kv_tail),
                  slc_p.reshape((1, bsz, seq) + kv_tail),
                  win_p.reshape((bsz, seq) + kv_tail)[:, seq - wl_p:][None],
                  hg_p[None])

    xs = x_sample.reshape(nb, D_MODEL)
    zs, cmp_s, slc_s, win_new, gates_s = _proj(xs, g_pre, w_r, nb)
    cmp_t = cache_cmp_kv.transpose(0, 1, 3, 4, 5, 2).reshape(n_pool, ROW_WIDTH, PAGE_SIZE)
    comp_s = _compress(cmp_t, page_table, *cw, feature_major=True)
    q8 = zs[:, COL_QA:COL_QA + NSA_WIDTH].astype(bf16).reshape(nb, NSA_HEADS, HEAD_DIM)
    oc_s, ps_s = _nsa_sample_cmp(q8, comp_s, m_mat, past)
    ps_rows = ps_s[:, :NSA_KV_HEADS, :].reshape(nb * NSA_KV_HEADS, N_SELBLK)
    idx = _nsa_sample_topk(ps_rows, past // SEL_BLOCK - 1)
    slc_t = cache_slc_kv.transpose(0, 1, 3, 4, 5, 2).reshape(n_pool, N_COMBO, HEAD_DIM, PAGE_SIZE)
    wl_s = cache_win_kv.shape[2]
    win_t = cache_win_kv.transpose(0, 1, 3, 4, 5, 2).reshape(nb, N_COMBO, HEAD_DIM, wl_s)
    oa_s, win_s = _nsa_sample_sel(page_table, idx[:, :N_SEL].reshape(-1), slc_t, q8, oc_s,
                                  gates_s.reshape(nb, 1, LANES), slc_s.reshape(nb, 1, ROW_WIDTH),
                                  win_new.reshape(nb, 1, ROW_WIDTH), win_t, past)
    win_s = win_s.reshape((1, nb) + kv_tail + (wl_s,)).transpose(0, 1, 5, 2, 3, 4)
    yb_s, hg_s = _hgrn_sample(zs, state_hgrn, hg_lower, hg_norm)
    h_s = _finish(xs, oa_s.reshape(nb, NSA_WIDTH), zs, yb_s, p_sample[0].reshape(nb, PLE_DIM),
                  w_out_b, g_post, ple_proj_b, ple_gate_b, nb)

    out_sample = (h_s.reshape(nb, 1, D_MODEL),
                  cmp_s.reshape((1, nb, 1) + kv_tail),
                  slc_s.reshape((1, nb, 1) + kv_tail),
                  win_s,
                  hg_s)
    return (out_prompt[0], out_sample[0]) + out_prompt[1:] + out_sample[1:]
```

```python
import functools

import numpy as np
import jax
import jax.numpy as jnp
from jax import lax
from jax.experimental import pallas as pl
from jax.experimental.pallas import tpu as pltpu

D_MODEL = 1024
PAGE_SIZE = 128
NSA_HEADS = 8
NSA_KV_HEADS = 2
NSA_GROUP = NSA_HEADS // NSA_KV_HEADS
HEAD_DIM = 64
NSA_WIDTH = NSA_HEADS * HEAD_DIM
KV_WIDTH = NSA_KV_HEADS * HEAD_DIM
CMP_BLOCK = 32
CMP_STRIDE = 16
CMP_RATIO = CMP_BLOCK // CMP_STRIDE
CMP_HIDDEN = 128
SEL_BLOCK = 64
SEL_RATIO = SEL_BLOCK // CMP_STRIDE
N_SEL = 16
WINDOW = 512
Q_BLOCK = 128
HG_HEADS = 4
HG_DIM = 128
HG_WIDTH = HG_HEADS * HG_DIM
PLE_DIM = 256
EPS = 1e-6
NEG = -1e30
FORCE_BONUS = 1e6
REMOVED = -3e38

COL_QA, COL_ZA, COL_QB, COL_FB, COL_IB, COL_ZB = 0, 512, 1024, 1536, 2048, 2560
COL_CMP, COL_SLC, COL_WIN, COL_GATE = 3072, 3328, 3584, 3840
PROJ_WIDTH = 4096
_ORIG = dict(q_a=0, k_cmp=512, gate_a=1280, z_a=1304, q_b=1816, f_b=2328, i_b=2840, z_b=3352, end=3864)

LANES = 128
VMEM_LIMIT = 56 * 1024 * 1024

SEL_CHUNK = 512
WIN_KEYS = WINDOW + Q_BLOCK
N_CMP_PAD = 512
N_SELBLK = 128

f32 = jnp.float32
bf16 = jnp.bfloat16


def _cparams(sem):
    return pltpu.CompilerParams(dimension_semantics=sem, vmem_limit_bytes=VMEM_LIMIT)


def _dot(a, b):
    return jnp.dot(a, b, preferred_element_type=f32)


def _dot_nt(a, b):
    return lax.dot_general(a, b, (((1,), (1,)), ((), ())), preferred_element_type=f32)


def _dot_tn(a, b):
    return lax.dot_general(a, b, (((0,), (0,)), ((), ())), preferred_element_type=f32)


def _sigmoid(x):
    return 1.0 / (1.0 + jnp.exp(-x))


def _silu(x):
    return x * _sigmoid(x)


PROJ_TILE = 1024
MAIN_TILES = COL_CMP // PROJ_TILE


def _proj_kernel(x_ref, g_ref, w_ref, z_ref, cmp_ref, slc_ref, win_ref, gate_ref, xn_ref):
    j = pl.program_id(1)

    @pl.when(j == 0)
    def _():
        x = x_ref[...]
        y = x * lax.rsqrt(jnp.mean(x * x, axis=-1, keepdims=True) + EPS)
        xn_ref[...] = (y * g_ref[...]).astype(bf16)

    res = _dot(xn_ref[...], w_ref[...])

    @pl.when(j < MAIN_TILES)
    def _():
        z_ref[...] = res

    @pl.when(j == MAIN_TILES)
    def _():
        base = COL_CMP
        cmp_ref[...] = res[:, COL_CMP - base:COL_SLC - base]
        slc_ref[...] = res[:, COL_SLC - base:COL_WIN - base]
        win_ref[...] = res[:, COL_WIN - base:COL_GATE - base]
        gate_ref[...] = res[:, COL_GATE - base:COL_GATE - base + LANES]


def _proj(x, g, w, tm):
    n = x.shape[0]
    row = lambda i, j: (i, 0)
    return pl.pallas_call(
        _proj_kernel,
        grid=(n // tm, PROJ_WIDTH // PROJ_TILE),
        in_specs=[pl.BlockSpec((tm, D_MODEL), row),
                  pl.BlockSpec((1, D_MODEL), lambda i, j: (0, 0)),
                  pl.BlockSpec((D_MODEL, PROJ_TILE), lambda i, j: (0, j))],
        out_specs=[pl.BlockSpec((tm, PROJ_TILE), lambda i, j: (i, jnp.minimum(j, MAIN_TILES - 1))),
                   pl.BlockSpec((tm, ROW_WIDTH), row), pl.BlockSpec((tm, ROW_WIDTH), row),
                   pl.BlockSpec((tm, ROW_WIDTH), row), pl.BlockSpec((tm, LANES), row)],
        out_shape=[jax.ShapeDtypeStruct((n, COL_CMP), f32),
                   jax.ShapeDtypeStruct((n, ROW_WIDTH), f32), jax.ShapeDtypeStruct((n, ROW_WIDTH), f32),
                   jax.ShapeDtypeStruct((n, ROW_WIDTH), f32), jax.ShapeDtypeStruct((n, LANES), f32)],
        scratch_shapes=[pltpu.VMEM((tm, D_MODEL), bf16)],
        compiler_params=_cparams(("parallel", "arbitrary")),
        name="proj",
    )(x, g, w)


def _finish_kernel(x_ref, oa_ref, za_ref, yb_ref, p_ref, wo_ref, gp_ref, pp_ref, pg_ref, h_ref):
    ya = (oa_ref[...] * _silu(za_ref[...])).astype(bf16)
    yb = yb_ref[...].astype(bf16)
    mix = _dot(ya, wo_ref[0:NSA_WIDTH, :]) + _dot(yb, wo_ref[NSA_WIDTH:, :])
    nrm = mix * lax.rsqrt(jnp.mean(mix * mix, axis=-1, keepdims=True) + EPS) * gp_ref[...]
    h = x_ref[...] + nrm
    gate = _sigmoid(_dot(h.astype(bf16), pg_ref[...]))
    h_ref[...] = h + _dot(p_ref[...].astype(bf16), pp_ref[...]) * gate


def _finish(x, oa, z, yb, p, w_out, g_post, ple_proj, ple_gate, tm):
    n = x.shape[0]
    za_blk = COL_ZA // NSA_WIDTH
    row = lambda i: (i, 0)
    const = lambda i: (0, 0)
    return pl.pallas_call(
        _finish_kernel,
        grid=(n // tm,),
        in_specs=[pl.BlockSpec((tm, D_MODEL), row),
                  pl.BlockSpec((tm, NSA_WIDTH), row),
                  pl.BlockSpec((tm, NSA_WIDTH), lambda i: (i, za_blk)),
                  pl.BlockSpec((tm, HG_WIDTH), row),
                  pl.BlockSpec((tm, PLE_DIM), row),
                  pl.BlockSpec((D_MODEL, D_MODEL), const),
                  pl.BlockSpec((1, D_MODEL), const),
                  pl.BlockSpec((PLE_DIM, D_MODEL), const),
                  pl.BlockSpec((D_MODEL, D_MODEL), const)],
        out_specs=pl.BlockSpec((tm, D_MODEL), row),
        out_shape=jax.ShapeDtypeStruct((n, D_MODEL), f32),
        compiler_params=_cparams(("parallel",)),
        name="finish",
    )(x, oa, z, yb, p, w_out, g_post, ple_proj, ple_gate)


SEGS_PER_PAGE = PAGE_SIZE // CMP_STRIDE
ROW_WIDTH = 2 * KV_WIDTH
SEG_WIDTH = CMP_STRIDE * ROW_WIDTH
N_COMBO = 2 * NSA_KV_HEADS


def _compress_kernel(pt_ref, pages_ref, wa_ref, wb_ref, pea_ref, peb_ref, b1_ref, w2_ref, out_ref,
                     stage_ref, rows_ref, sem_ref, *, n_pages, feature_major):
    b = pl.program_id(0)
    nb = pl.num_programs(0)
    slot = lax.rem(b, 2)

    def page_copy(seq, s, j):
        return pltpu.make_async_copy(pages_ref.at[pt_ref[seq, j]], stage_ref.at[s, j], sem_ref.at[s])

    def start_all(seq, s):
        def body(j, c):
            page_copy(seq, s, j).start()
            return c
        lax.fori_loop(0, n_pages, body, 0)

    @pl.when(b == 0)
    def _():
        start_all(0, 0)

    @pl.when(b + 1 < nb)
    def _():
        start_all(b + 1, 1 - slot)

    def wait_body(j, c):
        page_copy(b, slot, j).wait()
        return c
    lax.fori_loop(0, n_pages, wait_body, 0)

    lane_halves = ROW_WIDTH // LANES

    def to_rows(j, c):
        tile = stage_ref[slot, j]
        rows = tile.T if feature_major else tile
        r0 = pl.multiple_of(j * PAGE_SIZE, PAGE_SIZE)
        for h in range(lane_halves):
            rows_ref[h, pl.ds(r0, PAGE_SIZE), :] = rows[:, h * LANES:(h + 1) * LANES]
        return c
    lax.fori_loop(0, n_pages, to_rows, 0, unroll=4)

    n_seg = n_pages * SEGS_PER_PAGE
    for kind in range(lane_halves):
        cols = slice(kind * LANES, (kind + 1) * LANES)
        seg_a, seg_b = [], []
        for j in range(CMP_STRIDE):
            r = rows_ref[kind, pl.ds(j, n_seg, stride=CMP_STRIDE), :]
            seg_a.append((r + pea_ref[j][:, cols]).astype(bf16))
            seg_b.append((r + peb_ref[j][:, cols]).astype(bf16))
        pa = _dot(jnp.concatenate(seg_a, axis=1), wa_ref[kind])
        pb = _dot(jnp.concatenate(seg_b, axis=1), wb_ref[kind])
        pb = pltpu.roll(pb, n_seg - 1, axis=0)
        hid = _silu(pa + pb + b1_ref[kind])
        for g in range(NSA_KV_HEADS):
            hg = hid[:, g * CMP_HIDDEN:(g + 1) * CMP_HIDDEN].astype(bf16)
            out_ref[0, kind * NSA_KV_HEADS + g] = _dot(hg, w2_ref[kind])


def _compress(pages, page_table, wa, wb, pea, peb, b1, w2, feature_major):
    nseq, n_pages = page_table.shape
    n_seg = n_pages * SEGS_PER_PAGE
    const2 = lambda b, pt: (0, 0)
    const3 = lambda b, pt: (0, 0, 0)
    once = pl.Buffered(1)
    grid_spec = pltpu.PrefetchScalarGridSpec(
        num_scalar_prefetch=1,
        grid=(nseq,),
        in_specs=[pl.BlockSpec(memory_space=pl.ANY),
                  pl.BlockSpec(wa.shape, const3, pipeline_mode=once),
                  pl.BlockSpec(wb.shape, const3, pipeline_mode=once),
                  pl.BlockSpec((CMP_STRIDE, 1, ROW_WIDTH), const3),
                  pl.BlockSpec((CMP_STRIDE, 1, ROW_WIDTH), const3),
                  pl.BlockSpec(b1.shape, const3),
                  pl.BlockSpec((2, CMP_HIDDEN, HEAD_DIM), const3)],
        out_specs=pl.BlockSpec((1, N_COMBO, n_seg, HEAD_DIM), lambda b, pt: (b, 0, 0, 0)),
        scratch_shapes=[pltpu.VMEM((2, n_pages) + pages.shape[1:], f32),
                        pltpu.VMEM((ROW_WIDTH // LANES, n_pages * PAGE_SIZE, LANES), f32),
                        pltpu.SemaphoreType.DMA((2,))],
    )
    return pl.pallas_call(
        functools.partial(_compress_kernel, n_pages=n_pages, feature_major=feature_major),
        grid_spec=grid_spec,
        out_shape=jax.ShapeDtypeStruct((nseq, N_COMBO, n_seg, HEAD_DIM), f32),
        compiler_params=_cparams(("arbitrary",)),
        name="compress",
    )(page_table, pages, wa, wb, pea, peb, b1, w2)


def _compress_weights(cmp_pe, cmp_w1, cmp_b1, cmp_w2):
    eye = jnp.eye(NSA_KV_HEADS, dtype=f32)

    def big(w_half):
        t = jnp.einsum('kjdh,gb->kjbdgh', w_half, eye)
        return t.reshape(2, CMP_STRIDE * KV_WIDTH, NSA_KV_HEADS * CMP_HIDDEN)

    w1 = cmp_w1.reshape(2, 2, CMP_STRIDE, HEAD_DIM, CMP_HIDDEN)
    wa = big(w1[:, 0]).astype(bf16)
    wb = big(w1[:, 1]).astype(bf16)
    pe = cmp_pe.reshape(2, 2, CMP_STRIDE, HEAD_DIM)

    def pe_row(p):
        t = jnp.broadcast_to(p.transpose(1, 0, 2)[:, :, None, :],
                             (CMP_STRIDE, 2, NSA_KV_HEADS, HEAD_DIM))
        return t.reshape(CMP_STRIDE, 1, ROW_WIDTH)

    pea, peb = pe_row(pe[:, 0]), pe_row(pe[:, 1])
    b1 = jnp.broadcast_to(cmp_b1[:, None, :], (2, NSA_KV_HEADS, CMP_HIDDEN)).reshape(2, 1, NSA_KV_HEADS * CMP_HIDDEN)
    return wa, wb, pea, peb, b1, cmp_w2.astype(bf16)


def _sel_map_matrix():
    wts = np.convolve(np.ones(SEL_RATIO), np.ones(CMP_RATIO))
    m = np.zeros((N_CMP_PAD, N_SELBLK), np.float32)
    for j in range(N_SELBLK):
        for k, wk in enumerate(wts):
            n = SEL_RATIO * j + k
            if n < N_CMP_PAD - 1:
                m[n, j] = wk
    return jnp.asarray(m, dtype=bf16)


def _expand_matrix(n_chunks):
    key = np.arange(n_chunks * SEL_CHUNK) // SEL_BLOCK
    e = (np.arange(N_SELBLK)[:, None] == key[None, :]).astype(np.float32)
    return jnp.asarray(e.reshape(N_SELBLK, n_chunks, SEL_CHUNK).transpose(1, 0, 2), dtype=bf16)


def _importance_to_blocks(imp, m):
    hi = imp.astype(bf16)
    lo = (imp - hi.astype(f32)).astype(bf16)
    return _dot(hi, m) + _dot(lo, m)


def _nsa_prompt_kernel(q_ref, gate_ref, kc_ref, vc_ref, kvs_ref, kvw_ref, m_ref, e_ref, o_ref,
                       acc_ref, mx_ref, used_ref):
    g = pl.program_id(1)
    qi = pl.program_id(2)
    q0 = qi * Q_BLOCK
    rows = NSA_GROUP * Q_BLOCK

    q64 = q_ref[0, 0, 0] * jnp.asarray(HEAD_DIM ** -0.5, bf16)
    q128 = jnp.concatenate([q64, jnp.zeros_like(q64)], axis=1)
    slope = [jnp.where(g == 0, 2.0 ** -(r + 1), 2.0 ** -(NSA_GROUP + r + 1)).astype(f32)
             for r in range(NSA_GROUP)]

    qrow = lax.broadcasted_iota(jnp.int32, (Q_BLOCK, 1), 0)

    def softmax_rows(s, dist, mask):
        es, ls = [], []
        for r in range(NSA_GROUP):
            sr = jnp.where(mask, s[r * Q_BLOCK:(r + 1) * Q_BLOCK] - slope[r] * dist, NEG)
            m = jnp.max(sr, axis=1, keepdims=True)
            e = jnp.where(mask, jnp.exp(sr - m), 0.0)
            es.append(e)
            ls.append(1.0 / jnp.maximum(jnp.sum(e, axis=1, keepdims=True), 1e-30))
        return es, ls

    kc = kc_ref[0, 0].astype(bf16)
    vc = vc_ref[0, 0].astype(bf16)
    ncol = lax.broadcasted_iota(jnp.int32, (1, N_CMP_PAD), 1)
    d_c = (q0 + qrow) - (ncol * CMP_STRIDE + (CMP_BLOCK - 1))
    es, ls = softmax_rows(_dot_nt(q64, kc), d_c.astype(f32), d_c >= 0)
    ps_c = [e * l for e, l in zip(es, ls)]
    o_c = _dot(jnp.concatenate(ps_c, axis=0).astype(bf16), vc)
    imp = ps_c[0] + ps_c[1] + ps_c[2] + ps_c[3]
    p_sel = _importance_to_blocks(imp, m_ref[...])

    ps_t = p_sel.T
    blk = lax.broadcasted_iota(jnp.int32, (N_SELBLK, Q_BLOCK), 0)
    tq = q0 + lax.broadcasted_iota(jnp.int32, (N_SELBLK, Q_BLOCK), 1)
    jt = jnp.right_shift(tq, SEL_BLOCK.bit_length() - 1)
    valid = blk <= jt
    forced = (blk == 0) | (blk == jt) | (blk == jt - 1)
    score = jnp.where(valid, ps_t + jnp.where(forced, FORCE_BONUS, 0.0), NEG)
    sel_t = jnp.zeros((N_SELBLK, Q_BLOCK), f32)
    for _ in range(N_SEL):
        best = jnp.max(score, axis=0, keepdims=True)
        idx = jnp.min(jnp.where(score == best, blk, N_SELBLK), axis=0, keepdims=True)
        hit = blk == idx
        sel_t = jnp.where(hit & (best > NEG * 0.5), 1.0, sel_t)
        score = jnp.where(hit, REMOVED, score)
    sel = sel_t.T.astype(bf16)

    acc_ref[...] = jnp.zeros_like(acc_ref)
    mx_ref[...] = jnp.full_like(mx_ref, NEG)
    kcol = lax.broadcasted_iota(jnp.int32, (1, SEL_CHUNK), 1)
    d_base = (qrow - kcol).astype(f32)

    blk_per_chunk = SEL_CHUNK // SEL_BLOCK
    for c in range(e_ref.shape[0]):
        picked = jnp.max(sel_t[c * blk_per_chunk:(c + 1) * blk_per_chunk, :])
        used_ref[c] = (picked > 0.5).astype(jnp.int32)

    def sel_chunk(c, carry):
        @pl.when(used_ref[c] > 0)
        def _():
            sel_chunk_body(c)
        return carry

    def sel_chunk_body(c):
        k0 = pl.multiple_of(c * SEL_CHUNK, SEL_CHUNK)
        kv = kvs_ref[0, 0, pl.ds(k0, SEL_CHUNK), :]
        s = _dot_nt(q128, kv[:, 0:LANES])
        dist = d_base + (q0 - k0).astype(f32)
        allowed = (_dot(sel, e_ref[c]) > 0.5) & (dist >= 0.0)
        ps = []
        for r in range(NSA_GROUP):
            rs = slice(r * Q_BLOCK, (r + 1) * Q_BLOCK)
            sr = jnp.where(allowed, s[rs] - slope[r] * dist, NEG)
            m_old = mx_ref[rs]
            m_new = jnp.maximum(m_old, jnp.max(sr, axis=1, keepdims=True))
            acc_ref[rs] = acc_ref[rs] * jnp.exp(m_old - m_new)
            mx_ref[rs] = m_new
            ps.append(jnp.exp(sr - m_new).astype(bf16))
        acc_ref[...] += _dot(jnp.concatenate(ps, axis=0), kv[:, LANES:])

    n_chunks = (q0 + Q_BLOCK + SEL_CHUNK - 1) // SEL_CHUNK
    lax.fori_loop(0, n_chunks, sel_chunk, 0)
    acc = acc_ref[...]
    o_s = acc[:, 0:HEAD_DIM] * (1.0 / jnp.maximum(acc[:, HEAD_DIM:HEAD_DIM + 1], 1e-30))

    w0 = pl.multiple_of(jnp.maximum(q0 - WINDOW, 0), Q_BLOCK)
    kvw = kvw_ref[0, 0, pl.ds(w0, WIN_KEYS), :]
    wcol = lax.broadcasted_iota(jnp.int32, (1, WIN_KEYS), 1)
    d_w = (q0 - w0) + qrow - wcol
    es, _ = softmax_rows(_dot_nt(q128, kvw[:, 0:LANES]), d_w.astype(f32), (d_w >= 0) & (d_w <= WINDOW))
    acc_w = _dot(jnp.concatenate(es, axis=0).astype(bf16), kvw[:, LANES:])
    o_w = acc_w[:, 0:HEAD_DIM] * (1.0 / jnp.maximum(acc_w[:, HEAD_DIM:HEAD_DIM + 1], 1e-30))

    gate = _sigmoid(gate_ref[...])

    def gate_col(r, j):
        lo = gate[:, r * 3 + j:r * 3 + j + 1]
        hi = gate[:, (NSA_GROUP + r) * 3 + j:(NSA_GROUP + r) * 3 + j + 1]
        return jnp.where(g == 0, lo, hi)

    for r in range(NSA_GROUP):
        rs = slice(r * Q_BLOCK, (r + 1) * Q_BLOCK)
        o_ref[0, 0, 0, rs, :] = gate_col(r, 0) * o_c[rs] + gate_col(r, 1) * o_s[rs] + gate_col(r, 2) * o_w[rs]


def _nsa_prompt(q_st, z, comp, kv_slc, kv_win, m_mat, e_mat):
    bsz, nq = q_st.shape[0], q_st.shape[1]
    seq = kv_slc.shape[2]
    rows = NSA_GROUP * Q_BLOCK
    gate_blk = COL_GATE // LANES
    return pl.pallas_call(
        _nsa_prompt_kernel,
        grid=(bsz, NSA_KV_HEADS, nq),
        in_specs=[pl.BlockSpec((1, 1, 1, rows, HEAD_DIM), lambda b, g, i: (b, i, g, 0, 0)),
                  pl.BlockSpec((Q_BLOCK, LANES), lambda b, g, i: (b * nq + i, gate_blk)),
                  pl.BlockSpec((1, 1, N_CMP_PAD, HEAD_DIM), lambda b, g, i: (b, g, 0, 0)),
                  pl.BlockSpec((1, 1, N_CMP_PAD, HEAD_DIM), lambda b, g, i: (b, NSA_KV_HEADS + g, 0, 0)),
                  pl.BlockSpec((1, 1, seq, 2 * LANES), lambda b, g, i: (b, g, 0, 0)),
                  pl.BlockSpec((1, 1, seq, 2 * LANES), lambda b, g, i: (b, g, 0, 0)),
                  pl.BlockSpec((N_CMP_PAD, N_SELBLK), lambda b, g, i: (0, 0)),
                  pl.BlockSpec(e_mat.shape, lambda b, g, i: (0, 0, 0))],
        out_specs=pl.BlockSpec((1, 1, 1, rows, HEAD_DIM), lambda b, g, i: (b, i, g, 0, 0)),
        out_shape=jax.ShapeDtypeStruct((bsz, nq, NSA_KV_HEADS, rows, HEAD_DIM), f32),
        scratch_shapes=[pltpu.VMEM((rows, LANES), f32), pltpu.VMEM((rows, 1), f32),
                        pltpu.SMEM((e_mat.shape[0],), jnp.int32)],
        compiler_params=_cparams(("parallel", "parallel", "arbitrary")),
        name="nsa_prompt",
    )(q_st, z, comp, comp, kv_slc, kv_win, m_mat, e_mat)


def _attention_kv(piece, bsz, seq):
    t = piece.reshape(bsz, seq, 2, NSA_KV_HEADS, HEAD_DIM).transpose(2, 0, 3, 1, 4)
    zeros = jnp.zeros_like(t[0])
    ones = jnp.ones(t[0].shape[:-1] + (1,), f32)
    out = jnp.concatenate([t[0], zeros, t[1], ones, zeros[..., :HEAD_DIM - 1]], axis=-1)
    return out.astype(bf16)


LOG2E = 1.4426950408889634
MASK_BIG = 1e30
AUG_ROWS = 16
VT_ROWS = HEAD_DIM + 16
KEY_LOW = 256
N_FORCED = 3


def _nsa_prompt_t_kernel(qt_ref, gate_ref, kc_ref, vc_ref, ks_ref, vts_ref, kw_ref, vtw_ref, mt_ref, o_ref,
                         qa_ref, acc_ref, mg_ref, bias_ref, used_ref):
    g = pl.program_id(1)
    qi = pl.program_id(2)
    q0 = qi * Q_BLOCK
    cols = NSA_GROUP * Q_BLOCK

    lane = lax.broadcasted_iota(jnp.int32, (1, cols), 1)
    head = jnp.right_shift(lane, Q_BLOCK.bit_length() - 1)
    tq = q0 + jnp.bitwise_and(lane, Q_BLOCK - 1)
    c = jnp.exp2(-(head.astype(f32) + 1.0 + NSA_GROUP * g.astype(f32))) * LOG2E

    qt = (qt_ref[0, 0, 0] * (HEAD_DIM ** -0.5 * LOG2E)).astype(bf16)

    def softmax_cols(s, mask):
        s = jnp.where(mask, s, NEG)
        m = jnp.max(s, axis=0, keepdims=True)
        e = jnp.where(mask, jnp.exp2(s - m), 0.0)
        return e, 1.0 / jnp.maximum(jnp.sum(e, axis=0, keepdims=True), 1e-30)

    kc = kc_ref[0, 0].astype(bf16)
    vc = vc_ref[0, 0].astype(bf16)
    nrow = lax.broadcasted_iota(jnp.int32, (N_CMP_PAD, 1), 0)
    d_c = tq - (nrow * CMP_STRIDE + (CMP_BLOCK - 1))
    e, rl = softmax_cols(_dot(kc, qt) - c * d_c.astype(f32), d_c >= 0)
    p_c = e * rl
    oc_t = _dot_tn(vc, p_c.astype(bf16))
    imp_t = (p_c[:, 0:Q_BLOCK] + p_c[:, Q_BLOCK:2 * Q_BLOCK]
             + p_c[:, 2 * Q_BLOCK:3 * Q_BLOCK] + p_c[:, 3 * Q_BLOCK:4 * Q_BLOCK])
    hi = imp_t.astype(bf16)
    lo = (imp_t - hi.astype(f32)).astype(bf16)
    ps_t = _dot(mt_ref[...], hi) + _dot(mt_ref[...], lo)

    blk = lax.broadcasted_iota(jnp.int32, (N_SELBLK, Q_BLOCK), 0)
    jt = jnp.right_shift(q0 + lax.broadcasted_iota(jnp.int32, (N_SELBLK, Q_BLOCK), 1), SEL_BLOCK.bit_length() - 1)
    valid = blk <= jt
    forced = (blk == 0) | (blk == jt) | (blk == jt - 1)
    sel_t = jnp.where(valid & forced, 1.0, 0.0)
    score = jnp.where(valid & jnp.logical_not(forced), ps_t, NEG)
    for _ in range(N_SEL - N_FORCED):
        best = jnp.max(score, axis=0, keepdims=True)
        idx = jnp.min(jnp.where(score == best, blk, N_SELBLK), axis=0, keepdims=True)
        hit = blk == idx
        sel_t = jnp.where(hit & (best > NEG * 0.5), 1.0, sel_t)
        score = jnp.where(hit, REMOVED, score)
    bias_ref[...] = (sel_t - 1.0) * MASK_BIG

    blk_per_chunk = SEL_CHUNK // SEL_BLOCK
    n_chunk_max = vts_ref.shape[2]
    for ck in range(n_chunk_max):
        picked = jnp.max(sel_t[ck * blk_per_chunk:(ck + 1) * blk_per_chunk, :])
        used_ref[ck] = (picked > 0.5).astype(jnp.int32)

    c_hi = c.astype(bf16).astype(f32)
    c_mid = (c - c_hi).astype(bf16).astype(f32)
    c_lo = (c - c_hi - c_mid).astype(bf16).astype(f32)
    arow = lax.broadcasted_iota(jnp.int32, (AUG_ROWS // 2, cols), 0)
    alibi_rows = jnp.where(arow >= 6, 0.0,
                           jnp.where((arow == 0) | (arow == 3), c_hi, jnp.where((arow == 1) | (arow == 4), c_mid, c_lo)))
    qa_ref[0:HEAD_DIM, :] = qt
    qa_ref[HEAD_DIM + AUG_ROWS:, :] = jnp.zeros((qa_ref.shape[0] - HEAD_DIM - AUG_ROWS, cols), bf16)
    acc_ref[...] = jnp.zeros_like(acc_ref)
    mg_ref[...] = jnp.full_like(mg_ref, NEG)
    krow = lax.broadcasted_iota(jnp.int32, (SEL_CHUNK, 1), 0)

    def sel_chunk(ck, causal):
        k0 = pl.multiple_of(ck * SEL_CHUNK, SEL_CHUNK)
        b8 = bias_ref[pl.ds(pl.multiple_of(ck * blk_per_chunk, blk_per_chunk), blk_per_chunk), :]
        qa_ref[HEAD_DIM:HEAD_DIM + AUG_ROWS, :] = jnp.concatenate(
            [jnp.concatenate([b8] * NSA_GROUP, axis=1), alibi_rows], axis=0).astype(bf16)
        s = _dot(ks_ref[0, 0, pl.ds(k0, SEL_CHUNK), :], qa_ref[...])
        if causal:
            s = jnp.where(k0 + krow <= tq, s, NEG)
        shift = c * k0.astype(f32)
        m_old = mg_ref[...] - shift
        m_new = jnp.maximum(m_old, jnp.max(s, axis=0, keepdims=True))
        p = jnp.exp2(s - m_new).astype(bf16)
        acc_ref[...] = acc_ref[...] * jnp.exp2(m_old - m_new) + _dot(vts_ref[0, 0, ck], p)
        mg_ref[...] = m_new + shift

    def past_chunk(ck, carry):
        @pl.when(used_ref[ck] > 0)
        def _():
            sel_chunk(ck, causal=False)
        return carry

    last = q0 // SEL_CHUNK
    sel_chunk(last, causal=True)

    w0 = pl.multiple_of(jnp.maximum(q0 - WINDOW, 0), Q_BLOCK)
    kw = kw_ref[0, 0, pl.ds(w0, WIN_KEYS), :]
    q_pad = jnp.concatenate([qt, jnp.zeros((LANES - HEAD_DIM, cols), bf16)], axis=0)
    wrow = lax.broadcasted_iota(jnp.int32, (WIN_KEYS, 1), 0)
    d_w = tq - (w0 + wrow)
    e, _ = softmax_cols(_dot(kw, q_pad) - c * d_w.astype(f32), (d_w >= 0) & (d_w <= WINDOW))
    e = e.astype(bf16)
    t0 = w0 // Q_BLOCK
    acc_w = jnp.zeros((VT_ROWS, cols), f32)
    for t in range(WIN_KEYS // Q_BLOCK):
        acc_w = acc_w + _dot(vtw_ref[0, 0, t0 + t], e[t * Q_BLOCK:(t + 1) * Q_BLOCK])
    ow_t = acc_w[0:HEAD_DIM] * (1.0 / jnp.maximum(acc_w[HEAD_DIM:HEAD_DIM + 1], 1e-30))

    lax.fori_loop(0, last, past_chunk, 0)
    acc = acc_ref[...]
    os_t = acc[0:HEAD_DIM] * (1.0 / jnp.maximum(acc[HEAD_DIM:HEAD_DIM + 1], 1e-30))

    gate_t = _sigmoid(gate_ref[...]).T

    def gate_row(r, j):
        lo_row = gate_t[r * 3 + j:r * 3 + j + 1, :]
        hi_row = gate_t[(NSA_GROUP + r) * 3 + j:(NSA_GROUP + r) * 3 + j + 1, :]
        return jnp.where(g == 0, lo_row, hi_row)

    outs = []
    for r in range(NSA_GROUP):
        cs = slice(r * Q_BLOCK, (r + 1) * Q_BLOCK)
        outs.append(gate_row(r, 0) * oc_t[:, cs] + gate_row(r, 1) * os_t[:, cs] + gate_row(r, 2) * ow_t[:, cs])
    for pair in range(NSA_GROUP // 2):
        both = jnp.concatenate([outs[2 * pair], outs[2 * pair + 1]], axis=0)
        o_ref[:, pair * LANES:(pair + 1) * LANES] = both.T


def _nsa_prompt_t(q_t, z, comp, k_slc, vt_slc, k_win, vt_win, m_t):
    bsz, _, nq = q_t.shape[:3]
    seq = k_slc.shape[2]
    cols = NSA_GROUP * Q_BLOCK
    gate_blk = COL_GATE // LANES
    per_bg = lambda b, g, i: (b, g, 0, 0)
    per_bg5 = lambda b, g, i: (b, g, 0, 0, 0)
    return pl.pallas_call(
        _nsa_prompt_t_kernel,
        grid=(bsz, NSA_KV_HEADS, nq),
        in_specs=[pl.BlockSpec((1, 1, 1, HEAD_DIM, cols), lambda b, g, i: (b, g, i, 0, 0)),
                  pl.BlockSpec((Q_BLOCK, LANES), lambda b, g, i: (b * nq + i, gate_blk)),
                  pl.BlockSpec((1, 1, N_CMP_PAD, HEAD_DIM), per_bg),
                  pl.BlockSpec((1, 1, N_CMP_PAD, HEAD_DIM), lambda b, g, i: (b, NSA_KV_HEADS + g, 0, 0)),
                  pl.BlockSpec((1, 1, seq, LANES), per_bg),
                  pl.BlockSpec((1, 1) + vt_slc.shape[2:], per_bg5),
                  pl.BlockSpec((1, 1, seq, LANES), per_bg),
                  pl.BlockSpec((1, 1) + vt_win.shape[2:], per_bg5),
                  pl.BlockSpec((N_SELBLK, N_CMP_PAD), lambda b, g, i: (0, 0))],
        out_specs=pl.BlockSpec((Q_BLOCK, NSA_GROUP * HEAD_DIM), lambda b, g, i: (b * nq + i, g)),
        out_shape=jax.ShapeDtypeStruct((bsz * seq, NSA_WIDTH), f32),
        scratch_shapes=[pltpu.VMEM((LANES, cols), bf16),
                        pltpu.VMEM((VT_ROWS, cols), f32),
                        pltpu.VMEM((1, cols), f32),
                        pltpu.VMEM((N_SELBLK, Q_BLOCK), f32),
                        pltpu.SMEM((vt_slc.shape[2],), jnp.int32)],
        compiler_params=_cparams(("parallel", "parallel", "arbitrary")),
        name="nsa_prompt",
    )(q_t, z, comp, comp, k_slc, vt_slc, k_win, vt_win, m_t)


def _nsa_prompt_pair_kernel(qt_ref, gate_ref, comp_ref, caug_ref, ks_ref, vts_ref, kw_ref, vtw_ref, mt_ref, o_ref,
                            qa_ref, acc_ref, mg_ref, bias_ref, need_ref):
    qi = pl.program_id(1)
    q0 = qi * Q_BLOCK
    cols = NSA_GROUP * Q_BLOCK
    heads = range(NSA_KV_HEADS)

    lane = lax.broadcasted_iota(jnp.int32, (1, cols), 1)
    head = jnp.right_shift(lane, Q_BLOCK.bit_length() - 1)
    tq = q0 + jnp.bitwise_and(lane, Q_BLOCK - 1)
    blk = lax.broadcasted_iota(jnp.int32, (N_SELBLK, Q_BLOCK), 0)
    jt = jnp.right_shift(q0 + lax.broadcasted_iota(jnp.int32, (N_SELBLK, Q_BLOCK), 1), SEL_BLOCK.bit_length() - 1)
    valid = blk <= jt
    forced = (blk == 0) | (blk == jt) | (blk == jt - 1)
    c_end = lax.broadcasted_iota(jnp.int32, (N_CMP_PAD, 1), 0) * CMP_STRIDE + (CMP_BLOCK - 1)
    krow = lax.broadcasted_iota(jnp.int32, (SEL_CHUNK, 1), 0)
    arow = lax.broadcasted_iota(jnp.int32, (AUG_ROWS // 2, cols), 0)
    blk_per_chunk = SEL_CHUNK // SEL_BLOCK
    n_chunk_max = vts_ref.shape[2]
    last = q0 // SEL_CHUNK
    w0 = pl.multiple_of(jnp.maximum(q0 - WINDOW, 0), Q_BLOCK)
    wkey = w0 + lax.broadcasted_iota(jnp.int32, (WIN_KEYS, 1), 0)
    in_window = (wkey <= tq) & (wkey >= tq - WINDOW)
    gate_t = _sigmoid(gate_ref[...]).T

    def softmax_cols(s, mask):
        s = jnp.where(mask, s, NEG)
        m = jnp.max(s, axis=0, keepdims=True)
        e = jnp.where(mask, jnp.exp2(s - m), 0.0)
        return e, 1.0 / jnp.maximum(jnp.sum(e, axis=0, keepdims=True), 1e-30)

    slope = [jnp.exp2(-(head.astype(f32) + 1.0 + NSA_GROUP * g)) * LOG2E for g in heads]
    qts = [qt_ref[0, g, 0] for g in heads]

    def chunk_scores(g, ck, slot=0, live=None):
        k0 = pl.multiple_of(ck * SEL_CHUNK, SEL_CHUNK)
        b8 = bias_ref[g, pl.ds(pl.multiple_of(ck * blk_per_chunk, blk_per_chunk), blk_per_chunk), :]
        if live is not None:
            b8 = jnp.where(live, b8, -MASK_BIG)
        qa_ref[g, slot, HEAD_DIM:HEAD_DIM + AUG_ROWS, :] = jnp.concatenate(
            [jnp.concatenate([b8] * NSA_GROUP, axis=1), alibi_rows[g]], axis=0).astype(bf16)
        return _dot(ks_ref[0, g, pl.ds(k0, SEL_CHUNK), :], qa_ref[g, slot])

    def sel_chunk(g, ck, causal, s=None):
        c = slope[g]
        k0 = pl.multiple_of(ck * SEL_CHUNK, SEL_CHUNK)
        if s is None:
            s = chunk_scores(g, ck)
        if causal:
            s = jnp.where(k0 + krow <= tq, s, NEG)
        shift = c * k0.astype(f32)
        m_old = mg_ref[g] - shift
        m_new = jnp.maximum(m_old, jnp.max(s, axis=0, keepdims=True))
        p = jnp.exp2(s - m_new).astype(bf16)
        acc_ref[g] = acc_ref[g] * jnp.exp2(m_old - m_new) + _dot(vts_ref[0, g, ck], p)
        mg_ref[g] = m_new + shift

    alibi_rows, s_cmp, s_win = [], [], []
    for g in heads:
        c, qt = slope[g], qts[g]
        c_hi = c.astype(bf16).astype(f32)
        c_mid = (c - c_hi).astype(bf16).astype(f32)
        c_lo = (c - c_hi - c_mid).astype(bf16).astype(f32)
        c3 = jnp.where((arow == 0) | (arow == 3), c_hi, jnp.where((arow == 1) | (arow == 4), c_mid, c_lo))
        alibi_rows.append(jnp.where(arow >= 6, 0.0, c3))
        pos_rows = jnp.where(arow >= 6, 0.0, jnp.where(arow >= 3, KEY_LOW * c3, c3))
        pos_rows = jnp.concatenate([pos_rows, jnp.zeros_like(pos_rows)], axis=0).astype(bf16)
        q_pos = jnp.concatenate([qt, pos_rows, jnp.zeros((LANES - HEAD_DIM - AUG_ROWS, cols), bf16)], axis=0)
        kc = jnp.concatenate([comp_ref[0, g].astype(bf16), caug_ref[...]], axis=1)
        s_cmp.append(_dot(kc, q_pos))
        kw = kw_ref[0, g, pl.ds(w0, WIN_KEYS), :]
        s_win.append(_dot(kw, q_pos))

    oc_t, ps_t = [], []
    for g in heads:
        e, rl = softmax_cols(s_cmp[g], c_end <= tq)
        p_c = e * rl
        oc_t.append(_dot_tn(comp_ref[0, NSA_KV_HEADS + g].astype(bf16), p_c.astype(bf16)))
        imp_t = (p_c[:, 0:Q_BLOCK] + p_c[:, Q_BLOCK:2 * Q_BLOCK]
                 + p_c[:, 2 * Q_BLOCK:3 * Q_BLOCK] + p_c[:, 3 * Q_BLOCK:4 * Q_BLOCK])
        hi = imp_t.astype(bf16)
        lo = (imp_t - hi.astype(f32)).astype(bf16)
        ps_t.append(_dot(mt_ref[...], hi) + _dot(mt_ref[...], lo))

    ow_t = []
    t0 = w0 // Q_BLOCK
    for g in heads:
        s_w = jnp.where(in_window, s_win[g], NEG)
        e = jnp.exp2(s_w - jnp.max(s_w, axis=0, keepdims=True)).astype(bf16)
        acc_w = jnp.zeros((VT_ROWS, cols), f32)
        for t in range(WIN_KEYS // Q_BLOCK):
            acc_w = acc_w + _dot(vtw_ref[0, g, t0 + t], e[t * Q_BLOCK:(t + 1) * Q_BLOCK])
        ow_t.append(acc_w[0:HEAD_DIM] * (1.0 / jnp.maximum(acc_w[HEAD_DIM:HEAD_DIM + 1], 1e-30)))

    sel_t = [jnp.where(valid & forced, 1.0, 0.0)] * NSA_KV_HEADS
    score = [jnp.where(valid & jnp.logical_not(forced), ps_t[g], NEG) for g in heads]
    for _ in range(N_SEL - N_FORCED):
        for g in heads:
            best = jnp.max(score[g], axis=0, keepdims=True)
            idx = jnp.min(jnp.where(score[g] == best, blk, N_SELBLK), axis=0, keepdims=True)
            hit = blk == idx
            sel_t[g] = jnp.where(hit & (best > NEG * 0.5), 1.0, sel_t[g])
            score[g] = jnp.where(hit, REMOVED, score[g])

    for g in heads:
        bias_ref[g] = (sel_t[g] - 1.0) * MASK_BIG
        for slot in range(qa_ref.shape[1]):
            qa_ref[g, slot, 0:HEAD_DIM, :] = qts[g]
            qa_ref[g, slot, HEAD_DIM + AUG_ROWS:, :] = jnp.zeros((LANES - HEAD_DIM - AUG_ROWS, cols), bf16)
        acc_ref[g] = jnp.zeros(acc_ref.shape[1:], f32)
        mg_ref[g] = jnp.full(mg_ref.shape[1:], NEG, f32)
    own = [chunk_scores(g, last) for g in heads]
    for g in heads:
        sel_chunk(g, last, causal=True, s=own[g])

    n_need = jnp.int32(0)
    for ck in range(n_chunk_max):
        picked = jnp.maximum(*[jnp.max(sel_t[g][ck * blk_per_chunk:(ck + 1) * blk_per_chunk, :]) for g in heads])
        need_ref[n_need] = ck
        n_need = n_need + ((picked > 0.5) & (ck < last)).astype(jnp.int32)

    def chunk_pair(i, carry):
        first = 2 * i
        live = first + 1 < n_need
        cks = [need_ref[first], need_ref[jnp.minimum(first + 1, n_need - 1)]]
        scores = [[chunk_scores(g, cks[slot], slot, live if slot else None) for g in heads] for slot in range(2)]
        for slot in range(2):
            for g in heads:
                sel_chunk(g, cks[slot], causal=False, s=scores[slot][g])
        return carry

    lax.fori_loop(0, (n_need + 1) // 2, chunk_pair, 0)

    for g in heads:
        acc = acc_ref[g]
        os_t = acc[0:HEAD_DIM] * (1.0 / jnp.maximum(acc[HEAD_DIM:HEAD_DIM + 1], 1e-30))
        outs = []
        for r in range(NSA_GROUP):
            cs = slice(r * Q_BLOCK, (r + 1) * Q_BLOCK)
            row = (g * NSA_GROUP + r) * 3
            outs.append(gate_t[row:row + 1, :] * oc_t[g][:, cs] + gate_t[row + 1:row + 2, :] * os_t[:, cs]
                        + gate_t[row + 2:row + 3, :] * ow_t[g][:, cs])
        for pair in range(NSA_GROUP // 2):
            both = jnp.concatenate([outs[2 * pair], outs[2 * pair + 1]], axis=0)
            lane0 = (g * NSA_GROUP // 2 + pair) * LANES
            o_ref[:, lane0:lane0 + LANES] = both.T


def _nsa_prompt_pair(q_t, z, comp, k_slc, vt_slc, k_win, vt_win, m_t):
    bsz, _, nq = q_t.shape[:3]
    seq = k_slc.shape[2]
    cols = NSA_GROUP * Q_BLOCK
    gate_blk = COL_GATE // LANES
    per_b4 = lambda b, i: (b, 0, 0, 0)
    per_b5 = lambda b, i: (b, 0, 0, 0, 0)
    return pl.pallas_call(
        _nsa_prompt_pair_kernel,
        grid=(bsz, nq),
        in_specs=[pl.BlockSpec((1, NSA_KV_HEADS, 1, HEAD_DIM, cols), lambda b, i: (b, 0, i, 0, 0)),
                  pl.BlockSpec((Q_BLOCK, LANES), lambda b, i: (b * nq + i, 0)),
                  pl.BlockSpec((1, N_COMBO, N_CMP_PAD, HEAD_DIM), per_b4),
                  pl.BlockSpec((N_CMP_PAD, HEAD_DIM), lambda b, i: (0, 0)),
                  pl.BlockSpec((1, NSA_KV_HEADS, seq, LANES), per_b4),
                  pl.BlockSpec((1,) + vt_slc.shape[1:], per_b5),
                  pl.BlockSpec((1, NSA_KV_HEADS, seq, LANES), per_b4),
                  pl.BlockSpec((1,) + vt_win.shape[1:], per_b5),
                  pl.BlockSpec((N_SELBLK, N_CMP_PAD), lambda b, i: (0, 0))],
        out_specs=pl.BlockSpec((Q_BLOCK, NSA_WIDTH), lambda b, i: (b * nq + i, 0)),
        out_shape=jax.ShapeDtypeStruct((bsz * seq, NSA_WIDTH), f32),
        scratch_shapes=[pltpu.VMEM((NSA_KV_HEADS, 2, LANES, cols), bf16),
                        pltpu.VMEM((NSA_KV_HEADS, VT_ROWS, cols), f32),
                        pltpu.VMEM((NSA_KV_HEADS, 1, cols), f32),
                        pltpu.VMEM((NSA_KV_HEADS, N_SELBLK, Q_BLOCK), f32),
                        pltpu.SMEM((vt_slc.shape[2],), jnp.int32)],
        compiler_params=_cparams(("parallel", "arbitrary")),
        name="nsa_prompt",
    )(q_t, z, comp, _cmp_position_columns(), k_slc, vt_slc, k_win, vt_win, m_t)


def _position_columns(pos):
    low, high = (pos % KEY_LOW)[:, None], (pos // KEY_LOW)[:, None]
    return np.concatenate([low, low, low, high, high, high], axis=1).astype(np.float32)


def _cmp_position_columns():
    aug = np.zeros((N_CMP_PAD, HEAD_DIM), np.float32)
    aug[:, 0:6] = _position_columns(np.arange(N_CMP_PAD) * CMP_STRIDE + CMP_BLOCK - 1)
    return jnp.asarray(aug, dtype=bf16)


LAYOUT_ROWS = 1024


def _attention_layout_kernel(q_ref, cmp_ref, slc_ref, win_ref, xs_ref, xw_ref,
                             qt_ref, ks_ref, vts_ref, kw_ref, vtw_ref, cmp_t_ref, slc_t_ref, win_t_ref):
    cmp_t_ref[0] = cmp_ref[...].T
    tm = q_ref.shape[0]
    q = q_ref[...] * (HEAD_DIM ** -0.5 * LOG2E)
    for qb in range(tm // Q_BLOCK):
        rows = slice(qb * Q_BLOCK, (qb + 1) * Q_BLOCK)
        for pair in range(NSA_HEADS // 2):
            t = q[rows, pair * LANES:(pair + 1) * LANES].T
            for k in range(2):
                g, r = divmod(2 * pair + k, NSA_GROUP)
                qt_ref[0, g, qb, :, r * Q_BLOCK:(r + 1) * Q_BLOCK] = t[k * HEAD_DIM:(k + 1) * HEAD_DIM].astype(bf16)

    for src_ref, extra_ref, k_ref, vt_ref, t_ref in ((slc_ref, xs_ref, ks_ref, vts_ref, slc_t_ref),
                                                     (win_ref, xw_ref, kw_ref, vtw_ref, win_t_ref)):
        tile = vt_ref.shape[4]
        s = src_ref[...]
        s_t = s.T
        t_ref[0] = s_t
        v_t = s_t[KV_WIDTH:]
        tail = jnp.where(lax.broadcasted_iota(jnp.int32, (VT_ROWS - HEAD_DIM, tile), 0) == 0, 1.0, 0.0)
        for g in range(NSA_KV_HEADS):
            k_ref[0, g] = jnp.concatenate([s[:, g * HEAD_DIM:(g + 1) * HEAD_DIM].astype(bf16), extra_ref[...]], axis=1)
            for c in range(tm // tile):
                blk = v_t[g * HEAD_DIM:(g + 1) * HEAD_DIM, c * tile:(c + 1) * tile]
                vt_ref[0, g, c] = jnp.concatenate([blk, tail], axis=0).astype(bf16)


def _attention_layout(z, cmp, slc, win, bsz, seq):
    tm = LAYOUT_ROWS
    per_b = seq // tm
    pos = np.arange(seq)
    xs = np.zeros((seq, HEAD_DIM), np.float32)
    xs[pos, (pos // SEL_BLOCK) % (SEL_CHUNK // SEL_BLOCK)] = 1.0
    xs[:, 8:11] = (pos % KEY_LOW)[:, None]
    xs[:, 11:14] = (pos % SEL_CHUNK - pos % KEY_LOW)[:, None]
    xw = np.zeros((seq, HEAD_DIM), np.float32)
    xw[:, 0:6] = _position_columns(pos)
    row = lambda i: (i, 0)
    rep = lambda i: (i % per_b, 0)
    out4 = lambda i: (i // per_b, 0, i % per_b, 0)
    out5 = lambda i: (i // per_b, 0, i % per_b, 0, 0)
    k_shape = jax.ShapeDtypeStruct((bsz, NSA_KV_HEADS, seq, LANES), bf16)
    vt_shape = lambda tile: jax.ShapeDtypeStruct((bsz, NSA_KV_HEADS, seq // tile, VT_ROWS, tile), bf16)
    vt_spec = lambda tile: pl.BlockSpec((1, NSA_KV_HEADS, tm // tile, VT_ROWS, tile), out5)
    rows_t_shape = jax.ShapeDtypeStruct((bsz, ROW_WIDTH, seq), f32)
    rows_t_spec = pl.BlockSpec((1, ROW_WIDTH, tm), lambda i: (i // per_b, 0, i % per_b))
    return pl.pallas_call(
        _attention_layout_kernel,
        grid=(bsz * per_b,),
        in_specs=[pl.BlockSpec((tm, NSA_WIDTH), row), pl.BlockSpec((tm, ROW_WIDTH), row),
                  pl.BlockSpec((tm, ROW_WIDTH), row), pl.BlockSpec((tm, ROW_WIDTH), row),
                  pl.BlockSpec((tm, HEAD_DIM), rep), pl.BlockSpec((tm, HEAD_DIM), rep)],
        out_specs=[pl.BlockSpec((1, NSA_KV_HEADS, tm // Q_BLOCK, HEAD_DIM, NSA_GROUP * Q_BLOCK), out5),
                   pl.BlockSpec((1, NSA_KV_HEADS, tm, LANES), out4), vt_spec(SEL_CHUNK),
                   pl.BlockSpec((1, NSA_KV_HEADS, tm, LANES), out4), vt_spec(Q_BLOCK),
                   rows_t_spec, rows_t_spec, rows_t_spec],
        out_shape=[jax.ShapeDtypeStruct((bsz, NSA_KV_HEADS, seq // Q_BLOCK, HEAD_DIM, NSA_GROUP * Q_BLOCK), bf16),
                   k_shape, vt_shape(SEL_CHUNK), k_shape, vt_shape(Q_BLOCK),
                   rows_t_shape, rows_t_shape, rows_t_shape],
        compiler_params=_cparams(("parallel",)),
        name="attention_layout",
    )(z, cmp, slc, win, jnp.asarray(xs, dtype=bf16), jnp.asarray(xw, dtype=bf16))


def _attention_keys(piece, bsz, seq, augment):
    tile = SEL_CHUNK if augment else Q_BLOCK
    t = piece.reshape(bsz, seq, 2, NSA_KV_HEADS, HEAD_DIM)
    k = t[:, :, 0].transpose(0, 2, 1, 3)
    pos = np.arange(seq)
    extra = np.zeros((seq, LANES - HEAD_DIM), np.float32)
    if augment:
        extra[np.arange(seq), (pos // SEL_BLOCK) % (SEL_CHUNK // SEL_BLOCK)] = 1.0
        extra[:, 8:11] = (pos % KEY_LOW)[:, None]
        extra[:, 11:14] = (pos % SEL_CHUNK - pos % KEY_LOW)[:, None]
    else:
        extra[:, 0:6] = _position_columns(pos)
    k_aug = jnp.concatenate([k, jnp.broadcast_to(jnp.asarray(extra), k.shape[:2] + extra.shape)], axis=-1)
    v_t = t[:, :, 1].transpose(0, 2, 3, 1)
    ones = jnp.ones(v_t.shape[:2] + (1, seq), f32)
    zeros = jnp.zeros(v_t.shape[:2] + (VT_ROWS - HEAD_DIM - 1, seq), f32)
    v_aug = jnp.concatenate([v_t, ones, zeros], axis=2)
    v_aug = v_aug.reshape(bsz, NSA_KV_HEADS, VT_ROWS, seq // tile, tile).transpose(0, 1, 3, 2, 4)
    return k_aug.astype(bf16), v_aug.astype(bf16)


HG_CHUNK = 128
HG_BAND = 8


def _lower_bound(lo):
    m = jnp.max(lo, axis=0, keepdims=True)
    e = jnp.exp(lo - m)
    return e[0:1] / jnp.sum(e, axis=0, keepdims=True)


def _hgrn_gates(q, fl, lb):
    sg = _sigmoid(fl)
    f = lb + (1.0 - lb) * sg
    k = (1.0 - lb) * (1.0 - sg)
    return _silu(q), k, jnp.log(f)


def _hgrn_out(o, nw, zb):
    o = o * lax.rsqrt(jnp.mean(o * o, axis=-1, keepdims=True) + EPS) * nw
    return o * _silu(zb)


def _hgrn_prompt_kernel(q_ref, f_ref, i_ref, zb_ref, lo_ref, nw_ref, y_ref, s_ref, st_ref):
    c = pl.program_id(1)

    @pl.when(c == 0)
    def _():
        st_ref[...] = jnp.zeros_like(st_ref)

    n = HG_CHUNK
    row = lax.broadcasted_iota(jnp.int32, (n, n), 0)
    col = lax.broadcasted_iota(jnp.int32, (n, n), 1)
    heads = [slice(h * HG_DIM, (h + 1) * HG_DIM) for h in range(HG_HEADS)]

    qq, kk, logf = _hgrn_gates(q_ref[...], f_ref[...], _lower_bound(lo_ref[...]))
    b = logf
    trow = lax.broadcasted_iota(jnp.int32, (n, 1), 0)
    step = 1
    while step < n:
        b = b + jnp.where(trow >= step, pltpu.roll(b, step, axis=0), 0.0)
        step *= 2

    intra = [_hgrn_intra(qq[:, hs], kk[:, hs], b[:, hs], row, col) for hs in heads]
    v = i_ref[...]
    for h, hs in enumerate(heads):
        st = st_ref[h]
        bh, kh, vh = b[:, hs], kk[:, hs], v[:, hs].astype(bf16)
        o = _dot(intra[h].astype(bf16), vh) + _dot_nt((qq[:, hs] * jnp.exp(bh)).astype(bf16), st.astype(bf16))
        b_last = bh[n - 1:n]
        st_new = st * jnp.exp(b_last) + _dot_tn(vh, (kh * jnp.exp(b_last - bh)).astype(bf16))
        st_ref[h] = st_new
        y_ref[:, hs] = _hgrn_out(o, nw_ref[...], zb_ref[:, hs])

        @pl.when(c == pl.num_programs(1) - 1)
        def _():
            s_ref[0, h] = st_new.T


def _hgrn_intra(qq, kk, b, row, col):
    n = HG_CHUNK
    a = jnp.zeros((n, n), f32)
    sub = jnp.bitwise_and(row, HG_BAND - 1)
    for d in range(HG_BAND):
        if d == 0:
            term = qq * kk
        else:
            ok = sub >= d
            decay = jnp.exp(jnp.where(ok, b - pltpu.roll(b, d, axis=0), 0.0))
            term = jnp.where(ok, qq * pltpu.roll(kk, d, axis=0) * decay, 0.0)
        a = a + jnp.where(col == row - d, jnp.sum(term, axis=1, keepdims=True), 0.0)

    size = 2 * HG_BAND
    while size <= n:
        half = size // 2
        qs, ks = [], []
        zero = jnp.zeros((half, HG_DIM), f32)
        for r0 in range(0, n, size):
            beta = b[r0 + half - 1:r0 + half]
            ks += [kk[r0:r0 + half] * jnp.exp(beta - b[r0:r0 + half]), zero]
            qs += [zero, qq[r0 + half:r0 + size] * jnp.exp(b[r0 + half:r0 + size] - beta)]
        part = _dot_nt(jnp.concatenate(qs, axis=0).astype(bf16), jnp.concatenate(ks, axis=0).astype(bf16))
        if size < n:
            sh = size.bit_length() - 1
            part = jnp.where(jnp.right_shift(row, sh) == jnp.right_shift(col, sh), part, 0.0)
        a = a + part
        size *= 2
    return a


def _hgrn_prompt(z, hg_lower, hg_norm, bsz, seq):
    nch = seq // HG_CHUNK

    def zcol(base):
        return pl.BlockSpec((HG_CHUNK, HG_WIDTH), lambda b, c: (b * nch + c, base // HG_WIDTH))

    return pl.pallas_call(
        _hgrn_prompt_kernel,
        grid=(bsz, nch),
        in_specs=[zcol(COL_QB), zcol(COL_FB), zcol(COL_IB), zcol(COL_ZB),
                  pl.BlockSpec(hg_lower.shape, lambda b, c: (0, 0)),
                  pl.BlockSpec((1, HG_DIM), lambda b, c: (0, 0))],
        out_specs=[pl.BlockSpec((HG_CHUNK, HG_WIDTH), lambda b, c: (b * nch + c, 0)),
                   pl.BlockSpec((1, HG_HEADS, HG_DIM, HG_DIM), lambda b, c: (b, 0, 0, 0))],
        out_shape=[jax.ShapeDtypeStruct((bsz * seq, HG_WIDTH), f32),
                   jax.ShapeDtypeStruct((bsz, HG_HEADS, HG_DIM, HG_DIM), f32)],
        scratch_shapes=[pltpu.VMEM((HG_HEADS, HG_DIM, HG_DIM), f32)],
        compiler_params=_cparams(("parallel", "arbitrary")),
        name="hgrn_prompt",
    )(z, z, z, z, hg_lower, hg_norm)


def _hgrn_sample_kernel(q_ref, f_ref, i_ref, zb_ref, lo_ref, nw_ref, s_ref, y_ref, so_ref, o_scr):
    nb = q_ref.shape[0]
    lb = _lower_bound(lo_ref[...])
    qq, kk, logf = _hgrn_gates(q_ref[...], f_ref[...], lb)
    v = i_ref[...]
    pad = jnp.zeros((HG_DIM - nb, HG_DIM), f32)
    to_cols = lambda x: jnp.concatenate([x, pad], axis=0).T
    f_t, k_t, q_t = to_cols(jnp.exp(logf)), to_cols(kk), to_cols(qq)
    for b in range(nb):
        s_new = f_t[:, b:b + 1] * s_ref[0, b, 0] + k_t[:, b:b + 1] * v[b:b + 1, :]
        so_ref[0, b, 0] = s_new
        o_scr[b:b + 1, :] = jnp.sum(s_new * q_t[:, b:b + 1], axis=0, keepdims=True)
    y_ref[...] = _hgrn_out(o_scr[...], nw_ref[...], zb_ref[...])


def _hgrn_sample(z, state, hg_lower, hg_norm):
    nb = z.shape[0]

    def zcol(base):
        return pl.BlockSpec((nb, HG_DIM), lambda h: (0, base // HG_DIM + h))

    sblk = pl.BlockSpec((1, nb, 1, HG_DIM, HG_DIM), lambda h: (0, 0, h, 0, 0))
    return pl.pallas_call(
        _hgrn_sample_kernel,
        grid=(HG_HEADS,),
        in_specs=[zcol(COL_QB), zcol(COL_FB), zcol(COL_IB), zcol(COL_ZB),
                  pl.BlockSpec((hg_lower.shape[0], HG_DIM), lambda h: (0, h)),
                  pl.BlockSpec((1, HG_DIM), lambda h: (0, 0)),
                  sblk],
        out_specs=[pl.BlockSpec((nb, HG_DIM), lambda h: (0, h)), sblk],
        out_shape=[jax.ShapeDtypeStruct((nb, HG_WIDTH), f32),
                   jax.ShapeDtypeStruct(state.shape, f32)],
        scratch_shapes=[pltpu.VMEM((nb, HG_DIM), f32)],
        compiler_params=_cparams(("parallel",)),
        name="hgrn_sample",
    )(z, z, z, z, hg_lower, hg_norm, state)


N_PICK = N_SEL - 1


def _head_slopes():
    h = lax.broadcasted_iota(jnp.int32, (NSA_HEADS, 1), 0).astype(f32)
    return jnp.exp2(-(h + 1.0))


def _nsa_sample_cmp_kernel(q_ref, comp_ref, m_ref, oc_ref, ps_ref, *, t_pos):
    q = q_ref[0] * jnp.asarray(HEAD_DIM ** -0.5, bf16)
    slope = _head_slopes()
    head = lax.broadcasted_iota(jnp.int32, (NSA_HEADS, 1), 0)
    ncol = lax.broadcasted_iota(jnp.int32, (1, N_CMP_PAD), 1)
    d_c = t_pos - (ncol * CMP_STRIDE + (CMP_BLOCK - 1))
    mask = d_c >= 0
    o_c = jnp.zeros((NSA_HEADS, HEAD_DIM), f32)
    ps_ref[...] = jnp.zeros_like(ps_ref)
    for g in range(NSA_KV_HEADS):
        kc = comp_ref[0, g].astype(bf16)
        vc = comp_ref[0, NSA_KV_HEADS + g].astype(bf16)
        s = jnp.where(mask, _dot_nt(q, kc) - slope * d_c.astype(f32), NEG)
        m = jnp.max(s, axis=1, keepdims=True)
        e = jnp.where(mask, jnp.exp(s - m), 0.0)
        p = e * (1.0 / jnp.maximum(jnp.sum(e, axis=1, keepdims=True), 1e-30))
        mine = jnp.right_shift(head, NSA_GROUP.bit_length() - 1) == g
        o_c = jnp.where(mine, _dot(p.astype(bf16), vc), o_c)
        imp = jnp.sum(jnp.where(mine, p, 0.0), axis=0, keepdims=True)
        ps_ref[0, g:g + 1, :] = _importance_to_blocks(imp, m_ref[...])
    oc_ref[0] = o_c


def _nsa_sample_cmp(q8, comp, m_mat, t_pos):
    nb = q8.shape[0]
    return pl.pallas_call(
        functools.partial(_nsa_sample_cmp_kernel, t_pos=t_pos),
        grid=(nb,),
        in_specs=[pl.BlockSpec((1, NSA_HEADS, HEAD_DIM), lambda b: (b, 0, 0)),
                  pl.BlockSpec((1, N_COMBO, N_CMP_PAD, HEAD_DIM), lambda b: (b, 0, 0, 0)),
                  pl.BlockSpec((N_CMP_PAD, N_SELBLK), lambda b: (0, 0))],
        out_specs=[pl.BlockSpec((1, NSA_HEADS, HEAD_DIM), lambda b: (b, 0, 0)),
                   pl.BlockSpec((1, 8, N_SELBLK), lambda b: (b, 0, 0))],
        out_shape=[jax.ShapeDtypeStruct((nb, NSA_HEADS, HEAD_DIM), f32),
                   jax.ShapeDtypeStruct((nb, 8, N_SELBLK), f32)],
        compiler_params=_cparams(("parallel",)),
        name="nsa_sample_cmp",
    )(q8, comp, m_mat)


def _nsa_sample_topk_kernel(ps_ref, idx_ref, *, last_blk):
    ps = ps_ref[...]
    lane = lax.broadcasted_iota(jnp.int32, ps.shape, 1)
    score = ps + jnp.where((lane == 0) | (lane == last_blk), FORCE_BONUS, 0.0)
    out = jnp.zeros(ps.shape, jnp.int32)
    for r in range(N_PICK):
        best = jnp.max(score, axis=1, keepdims=True)
        idx = jnp.min(jnp.where(score == best, lane, N_SELBLK), axis=1, keepdims=True)
        out = jnp.where(lane == r, idx, out)
        score = jnp.where(lane == idx, REMOVED, score)
    idx_ref[...] = out


def _nsa_sample_topk(ps_rows, last_blk):
    return pl.pallas_call(
        functools.partial(_nsa_sample_topk_kernel, last_blk=last_blk),
        out_shape=jax.ShapeDtypeStruct(ps_rows.shape, jnp.int32),
        name="nsa_sample_topk",
    )(ps_rows)


def _nsa_sample_sel_kernel(pt_ref, idx_ref, slc_ref, q_ref, oc_ref, gate_ref, slc_new_ref, win_new_ref, wc_ref,
                           o_ref, wout_ref, kv_scr, sem_ref, *, t_pos):
    n_blk = NSA_KV_HEADS * N_PICK
    halves = PAGE_SIZE // SEL_BLOCK
    b = pl.program_id(0)
    nb = pl.num_programs(0)
    slot = lax.rem(b, 2)

    def picked(seq, k):
        return idx_ref[(seq * NSA_KV_HEADS + k // N_PICK) * N_SEL + k % N_PICK]

    def blk_copies(seq, s, k):
        page = pt_ref[seq, picked(seq, k) // halves]
        g = k // N_PICK
        return [pltpu.make_async_copy(slc_ref.at[page, kind * NSA_KV_HEADS + g], kv_scr.at[s, kind, k],
                                      sem_ref.at[s]) for kind in range(2)]

    def start_all(seq, s):
        for k in range(n_blk):
            for cp in blk_copies(seq, s, k):
                cp.start()

    @pl.when(b == 0)
    def _():
        start_all(0, 0)

    @pl.when(b + 1 < nb)
    def _():
        start_all(b + 1, 1 - slot)

    for k in range(n_blk):
        for cp in blk_copies(b, slot, k):
            cp.wait()

    q = q_ref[0] * jnp.asarray(HEAD_DIM ** -0.5, bf16)
    slope = _head_slopes()
    head = lax.broadcasted_iota(jnp.int32, (NSA_HEADS, 1), 0)
    lane = lax.broadcasted_iota(jnp.int32, (1, PAGE_SIZE), 1)
    new_slc = slc_new_ref[0]
    new_win = win_new_ref[0]
    wl = wc_ref.shape[3]
    wlane = lax.broadcasted_iota(jnp.int32, (1, wl), 1)
    d_w = (wl - wlane).astype(f32)
    qf = q.astype(f32)

    def attend(s, mask, v_t, new_row, g):
        k_new = new_row[:, g * HEAD_DIM:(g + 1) * HEAD_DIM].astype(bf16).astype(f32)
        v_new = new_row[:, KV_WIDTH + g * HEAD_DIM:KV_WIDTH + (g + 1) * HEAD_DIM].astype(bf16).astype(f32)
        s_new = jnp.sum(qf * k_new, axis=1, keepdims=True)
        m = jnp.maximum(jnp.max(s, axis=1, keepdims=True), s_new)
        e, e_new = jnp.exp(s - m), jnp.exp(s_new - m)
        if mask is not None:
            e = jnp.where(mask, e, 0.0)
        den = jnp.maximum(jnp.sum(e, axis=1, keepdims=True) + e_new, 1e-30)
        num = _dot_nt(e.astype(bf16), v_t.astype(bf16)) + e_new.astype(bf16).astype(f32) * v_new
        return num * (1.0 / den)

    o_s = jnp.zeros((NSA_HEADS, HEAD_DIM), f32)
    o_w = jnp.zeros((NSA_HEADS, HEAD_DIM), f32)
    for g in range(NSA_KV_HEADS):
        mine = jnp.right_shift(head, NSA_GROUP.bit_length() - 1) == g
        pos, member = [], []
        for n in range(N_PICK):
            j = picked(b, g * N_PICK + n)
            pos.append((j // halves) * PAGE_SIZE + lane)
            member.append(jnp.right_shift(lane, SEL_BLOCK.bit_length() - 1) == j % halves)
        pos = jnp.concatenate(pos, axis=1)
        member = jnp.concatenate(member, axis=1)
        k_t = jnp.concatenate([kv_scr[slot, 0, g * N_PICK + n] for n in range(N_PICK)], axis=1)
        v_t = jnp.concatenate([kv_scr[slot, 1, g * N_PICK + n] for n in range(N_PICK)], axis=1)
        s = jnp.where(member, _dot(q, k_t.astype(bf16)) - slope * (t_pos - pos).astype(f32), NEG)
        o_s = jnp.where(mine, attend(s, member, v_t, new_slc, g), o_s)
        s = _dot(q, wc_ref[0, g].astype(bf16)) - slope * d_w
        o_w = jnp.where(mine, attend(s, None, wc_ref[0, NSA_KV_HEADS + g], new_win, g), o_w)

    gate = jnp.broadcast_to(_sigmoid(gate_ref[0]), (NSA_HEADS, LANES))
    glane = lax.broadcasted_iota(jnp.int32, (NSA_HEADS, LANES), 1)
    gcol = [jnp.sum(jnp.where(glane == 3 * head + j, gate, 0.0), axis=1, keepdims=True) for j in range(3)]
    o_ref[0] = gcol[0] * oc_ref[0] + gcol[1] * o_s + gcol[2] * o_w
    eye = lax.broadcasted_iota(jnp.int32, (HEAD_DIM, HEAD_DIM), 0) == \
        lax.broadcasted_iota(jnp.int32, (HEAD_DIM, HEAD_DIM), 1)
    for c in range(N_COMBO):
        piece = jnp.broadcast_to(new_win[:, c * HEAD_DIM:(c + 1) * HEAD_DIM], (HEAD_DIM, HEAD_DIM))
        new_col = jnp.sum(jnp.where(eye, piece, 0.0), axis=1, keepdims=True)
        wout_ref[0, c] = jnp.where(wlane == wl - 1, new_col, pltpu.roll(wc_ref[0, c], wl - 1, axis=1))


def _nsa_sample_sel(page_table, idx_flat, slc_pages_t, q8, o_c, gates, slc_new, win_new, cache_win_t, t_pos):
    nb = q8.shape[0]
    wl = cache_win_t.shape[3]
    n_blk = NSA_KV_HEADS * N_PICK
    per_seq = lambda b, pt, idx: (b, 0, 0)
    win_blk = pl.BlockSpec((1, N_COMBO, HEAD_DIM, wl), lambda b, pt, idx: (b, 0, 0, 0))
    grid_spec = pltpu.PrefetchScalarGridSpec(
        num_scalar_prefetch=2,
        grid=(nb,),
        in_specs=[pl.BlockSpec(memory_space=pl.ANY),
                  pl.BlockSpec((1, NSA_HEADS, HEAD_DIM), per_seq),
                  pl.BlockSpec((1, NSA_HEADS, HEAD_DIM), per_seq),
                  pl.BlockSpec((1, 1, LANES), per_seq),
                  pl.BlockSpec((1, 1, 2 * KV_WIDTH), per_seq),
                  pl.BlockSpec((1, 1, 2 * KV_WIDTH), per_seq),
                  win_blk],
        out_specs=[pl.BlockSpec((1, NSA_HEADS, HEAD_DIM), per_seq), win_blk],
        scratch_shapes=[pltpu.VMEM((2, 2, n_blk, HEAD_DIM, PAGE_SIZE), f32),
                        pltpu.SemaphoreType.DMA((2,))],
    )
    return pl.pallas_call(
        functools.partial(_nsa_sample_sel_kernel, t_pos=t_pos),
        grid_spec=grid_spec,
        out_shape=[jax.ShapeDtypeStruct((nb, NSA_HEADS, HEAD_DIM), f32),
                   jax.ShapeDtypeStruct(cache_win_t.shape, f32)],
        compiler_params=_cparams(("arbitrary",)),
        name="nsa_sample_sel",
    )(page_table, idx_flat, slc_pages_t, q8, o_c, gates, slc_new, win_new, cache_win_t)


def kernel(x_prompt, x_sample, cache_cmp_kv, cache_slc_kv, cache_win_kv, state_hgrn, page_table, p_prompt, p_sample,
           w_in, g_pre, cmp_pe, cmp_w1, cmp_b1, cmp_w2, hg_lower, hg_norm, w_out, g_post, ple_proj, ple_gate):
    bsz, seq, _ = x_prompt.shape
    nb = x_sample.shape[0]
    n_pool = cache_cmp_kv.shape[1]
    n_pages = page_table.shape[1]
    past = n_pages * PAGE_SIZE
    nq = seq // Q_BLOCK
    kv_tail = (2, NSA_KV_HEADS, HEAD_DIM)

    o = _ORIG
    w = w_in[0]
    w_r = jnp.concatenate([w[:, o['q_a']:o['k_cmp']], w[:, o['z_a']:o['end']], w[:, o['k_cmp']:o['gate_a']],
                           w[:, o['gate_a']:o['z_a']],
                           jnp.zeros((D_MODEL, PROJ_WIDTH - o['end']), f32)], axis=1).astype(bf16)
    cw = _compress_weights(cmp_pe[0], cmp_w1[0], cmp_b1[0], cmp_w2[0])
    m_mat = _sel_map_matrix()
    w_out_b, ple_proj_b, ple_gate_b = w_out[0].astype(bf16), ple_proj[0].astype(bf16), ple_gate[0].astype(bf16)

    xp = x_prompt.reshape(bsz * seq, D_MODEL)
    zp, cmp_p, slc_p, win_p, gates_p = _proj(xp, g_pre, w_r, 1024)
    pages_p = cmp_p.reshape(bsz * seq // PAGE_SIZE, PAGE_SIZE, ROW_WIDTH)
    table_p = jnp.arange(bsz * seq // PAGE_SIZE, dtype=jnp.int32).reshape(bsz, seq // PAGE_SIZE)
    comp_p = _compress(pages_p, table_p, *cw, feature_major=False)
    q_t, k_slc, vt_slc, k_win, vt_win, cmp_tp, slc_tp, win_tp = _attention_layout(zp, cmp_p, slc_p, win_p, bsz, seq)
    o_a = _nsa_prompt_pair(q_t, gates_p, comp_p, k_slc, vt_slc, k_win, vt_win, m_mat.T)
    yb_p, hg_p = _hgrn_prompt(zp, hg_lower, hg_norm, bsz, seq)
    h_p = _finish(xp, o_a, zp, yb_p, p_prompt[0].reshape(bsz * seq, PLE_DIM),
                  w_out_b, g_post, ple_proj_b, ple_gate_b, 512)

    wl_p = min(WINDOW, seq)

    def cache_rows(rows_t):
        return rows_t.reshape((1, bsz) + kv_tail + rows_t.shape[2:]).transpose(0, 1, 5, 2, 3, 4)

    out_prompt = (h_p.reshape(bsz, seq, D_MODEL), cache_rows(cmp_tp), cache_rows(slc_tp),
                  cache_rows(win_tp[:, :, seq - wl_p:]), hg_p[None])

    xs = x_sample.reshape(nb, D_MODEL)
    zs, cmp_s, slc_s, win_new, gates_s = _proj(xs, g_pre, w_r, nb)
    cmp_t = cache_cmp_kv.transpose(0, 1, 3, 4, 5, 2).reshape(n_pool, ROW_WIDTH, PAGE_SIZE)
    comp_s = _compress(cmp_t, page_table, *cw, feature_major=True)
    q8 = zs[:, COL_QA:COL_QA + NSA_WIDTH].astype(bf16).reshape(nb, NSA_HEADS, HEAD_DIM)
    oc_s, ps_s = _nsa_sample_cmp(q8, comp_s, m_mat, past)
    ps_rows = ps_s[:, :NSA_KV_HEADS, :].reshape(nb * NSA_KV_HEADS, N_SELBLK)
    idx = _nsa_sample_topk(ps_rows, past // SEL_BLOCK - 1)
    slc_t = cache_slc_kv.transpose(0, 1, 3, 4, 5, 2).reshape(n_pool, N_COMBO, HEAD_DIM, PAGE_SIZE)
    wl_s = cache_win_kv.shape[2]
    win_t = cache_win_kv.transpose(0, 1, 3, 4, 5, 2).reshape(nb, N_COMBO, HEAD_DIM, wl_s)
    oa_s, win_s = _nsa_sample_sel(page_table, idx[:, :N_SEL].reshape(-1), slc_t, q8, oc_s,
                                  gates_s.reshape(nb, 1, LANES), slc_s.reshape(nb, 1, ROW_WIDTH),
                                  win_new.reshape(nb, 1, ROW_WIDTH), win_t, past)
    win_s = win_s.reshape((1, nb) + kv_tail + (wl_s,)).transpose(0, 1, 5, 2, 3, 4)
    yb_s, hg_s = _hgrn_sample(zs, state_hgrn, hg_lower, hg_norm)
    h_s = _finish(xs, oa_s.reshape(nb, NSA_WIDTH), zs, yb_s, p_sample[0].reshape(nb, PLE_DIM),
                  w_out_b, g_post, ple_proj_b, ple_gate_b, nb)

    out_sample = (h_s.reshape(nb, 1, D_MODEL),
                  cmp_s.reshape((1, nb, 1) + kv_tail),
                  slc_s.reshape((1, nb, 1) + kv_tail),
                  win_s,
                  hg_s)
    return (out_prompt[0], out_sample[0]) + out_prompt[1:] + out_sample[1:]
```

```python
import functools

import numpy as np
import jax
import jax.numpy as jnp
from jax import lax
from jax.experimental import pallas as pl
from jax.experimental.pallas import tpu as pltpu

D_MODEL = 1024
PAGE_SIZE = 128
NSA_HEADS = 8
NSA_KV_HEADS = 2
NSA_GROUP = NSA_HEADS // NSA_KV_HEADS
HEAD_DIM = 64
NSA_WIDTH = NSA_HEADS * HEAD_DIM
KV_WIDTH = NSA_KV_HEADS * HEAD_DIM
CMP_BLOCK = 32
CMP_STRIDE = 16
CMP_RATIO = CMP_BLOCK // CMP_STRIDE
CMP_HIDDEN = 128
SEL_BLOCK = 64
SEL_RATIO = SEL_BLOCK // CMP_STRIDE
N_SEL = 16
WINDOW = 512
Q_BLOCK = 128
HG_HEADS = 4
HG_DIM = 128
HG_WIDTH = HG_HEADS * HG_DIM
PLE_DIM = 256
EPS = 1e-6
NEG = -1e30
FORCE_BONUS = 1e6
REMOVED = -3e38

COL_QA, COL_ZA, COL_QB, COL_FB, COL_IB, COL_ZB = 0, 512, 1024, 1536, 2048, 2560
COL_CMP, COL_SLC, COL_WIN, COL_GATE = 3072, 3328, 3584, 3840
PROJ_WIDTH = 4096
_ORIG = dict(q_a=0, k_cmp=512, gate_a=1280, z_a=1304, q_b=1816, f_b=2328, i_b=2840, z_b=3352, end=3864)

LANES = 128
VMEM_LIMIT = 56 * 1024 * 1024

SEL_CHUNK = 512
WIN_KEYS = WINDOW + Q_BLOCK
N_CMP_PAD = 512
N_SELBLK = 128

f32 = jnp.float32
bf16 = jnp.bfloat16


def _cparams(sem):
    return pltpu.CompilerParams(dimension_semantics=sem, vmem_limit_bytes=VMEM_LIMIT)


def _dot(a, b):
    return jnp.dot(a, b, preferred_element_type=f32)


def _dot_nt(a, b):
    return lax.dot_general(a, b, (((1,), (1,)), ((), ())), preferred_element_type=f32)


def _dot_tn(a, b):
    return lax.dot_general(a, b, (((0,), (0,)), ((), ())), preferred_element_type=f32)


def _sigmoid(x):
    return 1.0 / (1.0 + jnp.exp(-x))


def _silu(x):
    return x * _sigmoid(x)


PROJ_TILE = 1024
MAIN_TILES = COL_CMP // PROJ_TILE


def _proj_kernel(x_ref, g_ref, w_ref, z_ref, cmp_ref, slc_ref, win_ref, gate_ref, xn_ref):
    j = pl.program_id(1)

    @pl.when(j == 0)
    def _():
        x = x_ref[...]
        y = x * lax.rsqrt(jnp.mean(x * x, axis=-1, keepdims=True) + EPS)
        xn_ref[...] = (y * g_ref[...]).astype(bf16)

    res = _dot(xn_ref[...], w_ref[...])

    @pl.when(j < MAIN_TILES)
    def _():
        z_ref[...] = res

    @pl.when(j == MAIN_TILES)
    def _():
        base = COL_CMP
        cmp_ref[...] = res[:, COL_CMP - base:COL_SLC - base]
        slc_ref[...] = res[:, COL_SLC - base:COL_WIN - base]
        win_ref[...] = res[:, COL_WIN - base:COL_GATE - base]
        gate_ref[...] = res[:, COL_GATE - base:COL_GATE - base + LANES]


def _proj(x, g, w, tm):
    n = x.shape[0]
    row = lambda i, j: (i, 0)
    return pl.pallas_call(
        _proj_kernel,
        grid=(n // tm, PROJ_WIDTH // PROJ_TILE),
        in_specs=[pl.BlockSpec((tm, D_MODEL), row),
                  pl.BlockSpec((1, D_MODEL), lambda i, j: (0, 0)),
                  pl.BlockSpec((D_MODEL, PROJ_TILE), lambda i, j: (0, j))],
        out_specs=[pl.BlockSpec((tm, PROJ_TILE), lambda i, j: (i, jnp.minimum(j, MAIN_TILES - 1))),
                   pl.BlockSpec((tm, ROW_WIDTH), row), pl.BlockSpec((tm, ROW_WIDTH), row),
                   pl.BlockSpec((tm, ROW_WIDTH), row), pl.BlockSpec((tm, LANES), row)],
        out_shape=[jax.ShapeDtypeStruct((n, COL_CMP), f32),
                   jax.ShapeDtypeStruct((n, ROW_WIDTH), f32), jax.ShapeDtypeStruct((n, ROW_WIDTH), f32),
                   jax.ShapeDtypeStruct((n, ROW_WIDTH), f32), jax.ShapeDtypeStruct((n, LANES), f32)],
        scratch_shapes=[pltpu.VMEM((tm, D_MODEL), bf16)],
        compiler_params=_cparams(("parallel", "arbitrary")),
        name="proj",
    )(x, g, w)


def _finish_kernel(x_ref, oa_ref, za_ref, yb_ref, p_ref, wo_ref, gp_ref, pp_ref, pg_ref, h_ref):
    ya = (oa_ref[...] * _silu(za_ref[...])).astype(bf16)
    yb = yb_ref[...].astype(bf16)
    mix = _dot(ya, wo_ref[0:NSA_WIDTH, :]) + _dot(yb, wo_ref[NSA_WIDTH:, :])
    nrm = mix * lax.rsqrt(jnp.mean(mix * mix, axis=-1, keepdims=True) + EPS) * gp_ref[...]
    h = x_ref[...] + nrm
    gate = _sigmoid(_dot(h.astype(bf16), pg_ref[...]))
    h_ref[...] = h + _dot(p_ref[...].astype(bf16), pp_ref[...]) * gate


def _finish(x, oa, z, yb, p, w_out, g_post, ple_proj, ple_gate, tm):
    n = x.shape[0]
    za_blk = COL_ZA // NSA_WIDTH
    row = lambda i: (i, 0)
    const = lambda i: (0, 0)
    return pl.pallas_call(
        _finish_kernel,
        grid=(n // tm,),
        in_specs=[pl.BlockSpec((tm, D_MODEL), row),
                  pl.BlockSpec((tm, NSA_WIDTH), row),
                  pl.BlockSpec((tm, NSA_WIDTH), lambda i: (i, za_blk)),
                  pl.BlockSpec((tm, HG_WIDTH), row),
                  pl.BlockSpec((tm, PLE_DIM), row),
                  pl.BlockSpec((D_MODEL, D_MODEL), const),
                  pl.BlockSpec((1, D_MODEL), const),
                  pl.BlockSpec((PLE_DIM, D_MODEL), const),
                  pl.BlockSpec((D_MODEL, D_MODEL), const)],
        out_specs=pl.BlockSpec((tm, D_MODEL), row),
        out_shape=jax.ShapeDtypeStruct((n, D_MODEL), f32),
        compiler_params=_cparams(("parallel",)),
        name="finish",
    )(x, oa, z, yb, p, w_out, g_post, ple_proj, ple_gate)


SEGS_PER_PAGE = PAGE_SIZE // CMP_STRIDE
ROW_WIDTH = 2 * KV_WIDTH
SEG_WIDTH = CMP_STRIDE * ROW_WIDTH
N_COMBO = 2 * NSA_KV_HEADS


def _compress_kernel(pt_ref, pages_ref, wa_ref, wb_ref, pea_ref, peb_ref, b1_ref, w2_ref, out_ref,
                     stage_ref, rows_ref, sem_ref, *, n_pages, feature_major):
    b = pl.program_id(0)
    nb = pl.num_programs(0)
    slot = lax.rem(b, 2)

    def page_copy(seq, s, j):
        return pltpu.make_async_copy(pages_ref.at[pt_ref[seq, j]], stage_ref.at[s, j], sem_ref.at[s])

    def start_all(seq, s):
        def body(j, c):
            page_copy(seq, s, j).start()
            return c
        lax.fori_loop(0, n_pages, body, 0)

    @pl.when(b == 0)
    def _():
        start_all(0, 0)

    @pl.when(b + 1 < nb)
    def _():
        start_all(b + 1, 1 - slot)

    def wait_body(j, c):
        page_copy(b, slot, j).wait()
        return c
    lax.fori_loop(0, n_pages, wait_body, 0)

    lane_halves = ROW_WIDTH // LANES

    def to_rows(j, c):
        tile = stage_ref[slot, j]
        rows = tile.T if feature_major else tile
        r0 = pl.multiple_of(j * PAGE_SIZE, PAGE_SIZE)
        for h in range(lane_halves):
            rows_ref[h, pl.ds(r0, PAGE_SIZE), :] = rows[:, h * LANES:(h + 1) * LANES]
        return c
    lax.fori_loop(0, n_pages, to_rows, 0, unroll=4)

    n_seg = n_pages * SEGS_PER_PAGE
    for kind in range(lane_halves):
        cols = slice(kind * LANES, (kind + 1) * LANES)
        seg_a, seg_b = [], []
        for j in range(CMP_STRIDE):
            r = rows_ref[kind, pl.ds(j, n_seg, stride=CMP_STRIDE), :]
            seg_a.append((r + pea_ref[j][:, cols]).astype(bf16))
            seg_b.append((r + peb_ref[j][:, cols]).astype(bf16))
        pa = _dot(jnp.concatenate(seg_a, axis=1), wa_ref[kind])
        pb = _dot(jnp.concatenate(seg_b, axis=1), wb_ref[kind])
        pb = pltpu.roll(pb, n_seg - 1, axis=0)
        hid = _silu(pa + pb + b1_ref[kind])
        for g in range(NSA_KV_HEADS):
            hg = hid[:, g * CMP_HIDDEN:(g + 1) * CMP_HIDDEN].astype(bf16)
            out_ref[0, kind * NSA_KV_HEADS + g] = _dot(hg, w2_ref[kind])


def _compress(pages, page_table, wa, wb, pea, peb, b1, w2, feature_major):
    nseq, n_pages = page_table.shape
    n_seg = n_pages * SEGS_PER_PAGE
    const2 = lambda b, pt: (0, 0)
    const3 = lambda b, pt: (0, 0, 0)
    once = pl.Buffered(1)
    grid_spec = pltpu.PrefetchScalarGridSpec(
        num_scalar_prefetch=1,
        grid=(nseq,),
        in_specs=[pl.BlockSpec(memory_space=pl.ANY),
                  pl.BlockSpec(wa.shape, const3, pipeline_mode=once),
                  pl.BlockSpec(wb.shape, const3, pipeline_mode=once),
                  pl.BlockSpec((CMP_STRIDE, 1, ROW_WIDTH), const3),
                  pl.BlockSpec((CMP_STRIDE, 1, ROW_WIDTH), const3),
                  pl.BlockSpec(b1.shape, const3),
                  pl.BlockSpec((2, CMP_HIDDEN, HEAD_DIM), const3)],
        out_specs=pl.BlockSpec((1, N_COMBO, n_seg, HEAD_DIM), lambda b, pt: (b, 0, 0, 0)),
        scratch_shapes=[pltpu.VMEM((2, n_pages) + pages.shape[1:], f32),
                        pltpu.VMEM((ROW_WIDTH // LANES, n_pages * PAGE_SIZE, LANES), f32),
                        pltpu.SemaphoreType.DMA((2,))],
    )
    return pl.pallas_call(
        functools.partial(_compress_kernel, n_pages=n_pages, feature_major=feature_major),
        grid_spec=grid_spec,
        out_shape=jax.ShapeDtypeStruct((nseq, N_COMBO, n_seg, HEAD_DIM), f32),
        compiler_params=_cparams(("arbitrary",)),
        name="compress",
    )(page_table, pages, wa, wb, pea, peb, b1, w2)


def _compress_weights(cmp_pe, cmp_w1, cmp_b1, cmp_w2):
    eye = jnp.eye(NSA_KV_HEADS, dtype=f32)

    def big(w_half):
        t = jnp.einsum('kjdh,gb->kjbdgh', w_half, eye)
        return t.reshape(2, CMP_STRIDE * KV_WIDTH, NSA_KV_HEADS * CMP_HIDDEN)

    w1 = cmp_w1.reshape(2, 2, CMP_STRIDE, HEAD_DIM, CMP_HIDDEN)
    wa = big(w1[:, 0]).astype(bf16)
    wb = big(w1[:, 1]).astype(bf16)
    pe = cmp_pe.reshape(2, 2, CMP_STRIDE, HEAD_DIM)

    def pe_row(p):
        t = jnp.broadcast_to(p.transpose(1, 0, 2)[:, :, None, :],
                             (CMP_STRIDE, 2, NSA_KV_HEADS, HEAD_DIM))
        return t.reshape(CMP_STRIDE, 1, ROW_WIDTH)

    pea, peb = pe_row(pe[:, 0]), pe_row(pe[:, 1])
    b1 = jnp.broadcast_to(cmp_b1[:, None, :], (2, NSA_KV_HEADS, CMP_HIDDEN)).reshape(2, 1, NSA_KV_HEADS * CMP_HIDDEN)
    return wa, wb, pea, peb, b1, cmp_w2.astype(bf16)


def _sel_map_matrix():
    wts = np.convolve(np.ones(SEL_RATIO), np.ones(CMP_RATIO))
    m = np.zeros((N_CMP_PAD, N_SELBLK), np.float32)
    for j in range(N_SELBLK):
        for k, wk in enumerate(wts):
            n = SEL_RATIO * j + k
            if n < N_CMP_PAD - 1:
                m[n, j] = wk
    return jnp.asarray(m, dtype=bf16)


def _expand_matrix(n_chunks):
    key = np.arange(n_chunks * SEL_CHUNK) // SEL_BLOCK
    e = (np.arange(N_SELBLK)[:, None] == key[None, :]).astype(np.float32)
    return jnp.asarray(e.reshape(N_SELBLK, n_chunks, SEL_CHUNK).transpose(1, 0, 2), dtype=bf16)


def _importance_to_blocks(imp, m):
    hi = imp.astype(bf16)
    lo = (imp - hi.astype(f32)).astype(bf16)
    return _dot(hi, m) + _dot(lo, m)


def _nsa_prompt_kernel(q_ref, gate_ref, kc_ref, vc_ref, kvs_ref, kvw_ref, m_ref, e_ref, o_ref,
                       acc_ref, mx_ref, used_ref):
    g = pl.program_id(1)
    qi = pl.program_id(2)
    q0 = qi * Q_BLOCK
    rows = NSA_GROUP * Q_BLOCK

    q64 = q_ref[0, 0, 0] * jnp.asarray(HEAD_DIM ** -0.5, bf16)
    q128 = jnp.concatenate([q64, jnp.zeros_like(q64)], axis=1)
    slope = [jnp.where(g == 0, 2.0 ** -(r + 1), 2.0 ** -(NSA_GROUP + r + 1)).astype(f32)
             for r in range(NSA_GROUP)]

    qrow = lax.broadcasted_iota(jnp.int32, (Q_BLOCK, 1), 0)

    def softmax_rows(s, dist, mask):
        es, ls = [], []
        for r in range(NSA_GROUP):
            sr = jnp.where(mask, s[r * Q_BLOCK:(r + 1) * Q_BLOCK] - slope[r] * dist, NEG)
            m = jnp.max(sr, axis=1, keepdims=True)
            e = jnp.where(mask, jnp.exp(sr - m), 0.0)
            es.append(e)
            ls.append(1.0 / jnp.maximum(jnp.sum(e, axis=1, keepdims=True), 1e-30))
        return es, ls

    kc = kc_ref[0, 0].astype(bf16)
    vc = vc_ref[0, 0].astype(bf16)
    ncol = lax.broadcasted_iota(jnp.int32, (1, N_CMP_PAD), 1)
    d_c = (q0 + qrow) - (ncol * CMP_STRIDE + (CMP_BLOCK - 1))
    es, ls = softmax_rows(_dot_nt(q64, kc), d_c.astype(f32), d_c >= 0)
    ps_c = [e * l for e, l in zip(es, ls)]
    o_c = _dot(jnp.concatenate(ps_c, axis=0).astype(bf16), vc)
    imp = ps_c[0] + ps_c[1] + ps_c[2] + ps_c[3]
    p_sel = _importance_to_blocks(imp, m_ref[...])

    ps_t = p_sel.T
    blk = lax.broadcasted_iota(jnp.int32, (N_SELBLK, Q_BLOCK), 0)
    tq = q0 + lax.broadcasted_iota(jnp.int32, (N_SELBLK, Q_BLOCK), 1)
    jt = jnp.right_shift(tq, SEL_BLOCK.bit_length() - 1)
    valid = blk <= jt
    forced = (blk == 0) | (blk == jt) | (blk == jt - 1)
    score = jnp.where(valid, ps_t + jnp.where(forced, FORCE_BONUS, 0.0), NEG)
    sel_t = jnp.zeros((N_SELBLK, Q_BLOCK), f32)
    for _ in range(N_SEL):
        best = jnp.max(score, axis=0, keepdims=True)
        idx = jnp.min(jnp.where(score == best, blk, N_SELBLK), axis=0, keepdims=True)
        hit = blk == idx
        sel_t = jnp.where(hit & (best > NEG * 0.5), 1.0, sel_t)
        score = jnp.where(hit, REMOVED, score)
    sel = sel_t.T.astype(bf16)

    acc_ref[...] = jnp.zeros_like(acc_ref)
    mx_ref[...] = jnp.full_like(mx_ref, NEG)
    kcol = lax.broadcasted_iota(jnp.int32, (1, SEL_CHUNK), 1)
    d_base = (qrow - kcol).astype(f32)

    blk_per_chunk = SEL_CHUNK // SEL_BLOCK
    for c in range(e_ref.shape[0]):
        picked = jnp.max(sel_t[c * blk_per_chunk:(c + 1) * blk_per_chunk, :])
        used_ref[c] = (picked > 0.5).astype(jnp.int32)

    def sel_chunk(c, carry):
        @pl.when(used_ref[c] > 0)
        def _():
            sel_chunk_body(c)
        return carry

    def sel_chunk_body(c):
        k0 = pl.multiple_of(c * SEL_CHUNK, SEL_CHUNK)
        kv = kvs_ref[0, 0, pl.ds(k0, SEL_CHUNK), :]
        s = _dot_nt(q128, kv[:, 0:LANES])
        dist = d_base + (q0 - k0).astype(f32)
        allowed = (_dot(sel, e_ref[c]) > 0.5) & (dist >= 0.0)
        ps = []
        for r in range(NSA_GROUP):
            rs = slice(r * Q_BLOCK, (r + 1) * Q_BLOCK)
            sr = jnp.where(allowed, s[rs] - slope[r] * dist, NEG)
            m_old = mx_ref[rs]
            m_new = jnp.maximum(m_old, jnp.max(sr, axis=1, keepdims=True))
            acc_ref[rs] = acc_ref[rs] * jnp.exp(m_old - m_new)
            mx_ref[rs] = m_new
            ps.append(jnp.exp(sr - m_new).astype(bf16))
        acc_ref[...] += _dot(jnp.concatenate(ps, axis=0), kv[:, LANES:])

    n_chunks = (q0 + Q_BLOCK + SEL_CHUNK - 1) // SEL_CHUNK
    lax.fori_loop(0, n_chunks, sel_chunk, 0)
    acc = acc_ref[...]
    o_s = acc[:, 0:HEAD_DIM] * (1.0 / jnp.maximum(acc[:, HEAD_DIM:HEAD_DIM + 1], 1e-30))

    w0 = pl.multiple_of(jnp.maximum(q0 - WINDOW, 0), Q_BLOCK)
    kvw = kvw_ref[0, 0, pl.ds(w0, WIN_KEYS), :]
    wcol = lax.broadcasted_iota(jnp.int32, (1, WIN_KEYS), 1)
    d_w = (q0 - w0) + qrow - wcol
    es, _ = softmax_rows(_dot_nt(q128, kvw[:, 0:LANES]), d_w.astype(f32), (d_w >= 0) & (d_w <= WINDOW))
    acc_w = _dot(jnp.concatenate(es, axis=0).astype(bf16), kvw[:, LANES:])
    o_w = acc_w[:, 0:HEAD_DIM] * (1.0 / jnp.maximum(acc_w[:, HEAD_DIM:HEAD_DIM + 1], 1e-30))

    gate = _sigmoid(gate_ref[...])

    def gate_col(r, j):
        lo = gate[:, r * 3 + j:r * 3 + j + 1]
        hi = gate[:, (NSA_GROUP + r) * 3 + j:(NSA_GROUP + r) * 3 + j + 1]
        return jnp.where(g == 0, lo, hi)

    for r in range(NSA_GROUP):
        rs = slice(r * Q_BLOCK, (r + 1) * Q_BLOCK)
        o_ref[0, 0, 0, rs, :] = gate_col(r, 0) * o_c[rs] + gate_col(r, 1) * o_s[rs] + gate_col(r, 2) * o_w[rs]


def _nsa_prompt(q_st, z, comp, kv_slc, kv_win, m_mat, e_mat):
    bsz, nq = q_st.shape[0], q_st.shape[1]
    seq = kv_slc.shape[2]
    rows = NSA_GROUP * Q_BLOCK
    gate_blk = COL_GATE // LANES
    return pl.pallas_call(
        _nsa_prompt_kernel,
        grid=(bsz, NSA_KV_HEADS, nq),
        in_specs=[pl.BlockSpec((1, 1, 1, rows, HEAD_DIM), lambda b, g, i: (b, i, g, 0, 0)),
                  pl.BlockSpec((Q_BLOCK, LANES), lambda b, g, i: (b * nq + i, gate_blk)),
                  pl.BlockSpec((1, 1, N_CMP_PAD, HEAD_DIM), lambda b, g, i: (b, g, 0, 0)),
                  pl.BlockSpec((1, 1, N_CMP_PAD, HEAD_DIM), lambda b, g, i: (b, NSA_KV_HEADS + g, 0, 0)),
                  pl.BlockSpec((1, 1, seq, 2 * LANES), lambda b, g, i: (b, g, 0, 0)),
                  pl.BlockSpec((1, 1, seq, 2 * LANES), lambda b, g, i: (b, g, 0, 0)),
                  pl.BlockSpec((N_CMP_PAD, N_SELBLK), lambda b, g, i: (0, 0)),
                  pl.BlockSpec(e_mat.shape, lambda b, g, i: (0, 0, 0))],
        out_specs=pl.BlockSpec((1, 1, 1, rows, HEAD_DIM), lambda b, g, i: (b, i, g, 0, 0)),
        out_shape=jax.ShapeDtypeStruct((bsz, nq, NSA_KV_HEADS, rows, HEAD_DIM), f32),
        scratch_shapes=[pltpu.VMEM((rows, LANES), f32), pltpu.VMEM((rows, 1), f32),
                        pltpu.SMEM((e_mat.shape[0],), jnp.int32)],
        compiler_params=_cparams(("parallel", "parallel", "arbitrary")),
        name="nsa_prompt",
    )(q_st, z, comp, comp, kv_slc, kv_win, m_mat, e_mat)


def _attention_kv(piece, bsz, seq):
    t = piece.reshape(bsz, seq, 2, NSA_KV_HEADS, HEAD_DIM).transpose(2, 0, 3, 1, 4)
    zeros = jnp.zeros_like(t[0])
    ones = jnp.ones(t[0].shape[:-1] + (1,), f32)
    out = jnp.concatenate([t[0], zeros, t[1], ones, zeros[..., :HEAD_DIM - 1]], axis=-1)
    return out.astype(bf16)


LOG2E = 1.4426950408889634
MASK_BIG = 1e30
AUG_ROWS = 16
VT_ROWS = HEAD_DIM + 16
KEY_LOW = 256
N_FORCED = 3


def _nsa_prompt_t_kernel(qt_ref, gate_ref, kc_ref, vc_ref, ks_ref, vts_ref, kw_ref, vtw_ref, mt_ref, o_ref,
                         qa_ref, acc_ref, mg_ref, bias_ref, used_ref):
    g = pl.program_id(1)
    qi = pl.program_id(2)
    q0 = qi * Q_BLOCK
    cols = NSA_GROUP * Q_BLOCK

    lane = lax.broadcasted_iota(jnp.int32, (1, cols), 1)
    head = jnp.right_shift(lane, Q_BLOCK.bit_length() - 1)
    tq = q0 + jnp.bitwise_and(lane, Q_BLOCK - 1)
    c = jnp.exp2(-(head.astype(f32) + 1.0 + NSA_GROUP * g.astype(f32))) * LOG2E

    qt = (qt_ref[0, 0, 0] * (HEAD_DIM ** -0.5 * LOG2E)).astype(bf16)

    def softmax_cols(s, mask):
        s = jnp.where(mask, s, NEG)
        m = jnp.max(s, axis=0, keepdims=True)
        e = jnp.where(mask, jnp.exp2(s - m), 0.0)
        return e, 1.0 / jnp.maximum(jnp.sum(e, axis=0, keepdims=True), 1e-30)

    kc = kc_ref[0, 0].astype(bf16)
    vc = vc_ref[0, 0].astype(bf16)
    nrow = lax.broadcasted_iota(jnp.int32, (N_CMP_PAD, 1), 0)
    d_c = tq - (nrow * CMP_STRIDE + (CMP_BLOCK - 1))
    e, rl = softmax_cols(_dot(kc, qt) - c * d_c.astype(f32), d_c >= 0)
    p_c = e * rl
    oc_t = _dot_tn(vc, p_c.astype(bf16))
    imp_t = (p_c[:, 0:Q_BLOCK] + p_c[:, Q_BLOCK:2 * Q_BLOCK]
             + p_c[:, 2 * Q_BLOCK:3 * Q_BLOCK] + p_c[:, 3 * Q_BLOCK:4 * Q_BLOCK])
    hi = imp_t.astype(bf16)
    lo = (imp_t - hi.astype(f32)).astype(bf16)
    ps_t = _dot(mt_ref[...], hi) + _dot(mt_ref[...], lo)

    blk = lax.broadcasted_iota(jnp.int32, (N_SELBLK, Q_BLOCK), 0)
    jt = jnp.right_shift(q0 + lax.broadcasted_iota(jnp.int32, (N_SELBLK, Q_BLOCK), 1), SEL_BLOCK.bit_length() - 1)
    valid = blk <= jt
    forced = (blk == 0) | (blk == jt) | (blk == jt - 1)
    sel_t = jnp.where(valid & forced, 1.0, 0.0)
    score = jnp.where(valid & jnp.logical_not(forced), ps_t, NEG)
    for _ in range(N_SEL - N_FORCED):
        best = jnp.max(score, axis=0, keepdims=True)
        idx = jnp.min(jnp.where(score == best, blk, N_SELBLK), axis=0, keepdims=True)
        hit = blk == idx
        sel_t = jnp.where(hit & (best > NEG * 0.5), 1.0, sel_t)
        score = jnp.where(hit, REMOVED, score)
    bias_ref[...] = (sel_t - 1.0) * MASK_BIG

    blk_per_chunk = SEL_CHUNK // SEL_BLOCK
    n_chunk_max = vts_ref.shape[2]
    for ck in range(n_chunk_max):
        picked = jnp.max(sel_t[ck * blk_per_chunk:(ck + 1) * blk_per_chunk, :])
        used_ref[ck] = (picked > 0.5).astype(jnp.int32)

    c_hi = c.astype(bf16).astype(f32)
    c_mid = (c - c_hi).astype(bf16).astype(f32)
    c_lo = (c - c_hi - c_mid).astype(bf16).astype(f32)
    arow = lax.broadcasted_iota(jnp.int32, (AUG_ROWS // 2, cols), 0)
    alibi_rows = jnp.where(arow >= 6, 0.0,
                           jnp.where((arow == 0) | (arow == 3), c_hi, jnp.where((arow == 1) | (arow == 4), c_mid, c_lo)))
    qa_ref[0:HEAD_DIM, :] = qt
    qa_ref[HEAD_DIM + AUG_ROWS:, :] = jnp.zeros((qa_ref.shape[0] - HEAD_DIM - AUG_ROWS, cols), bf16)
    acc_ref[...] = jnp.zeros_like(acc_ref)
    mg_ref[...] = jnp.full_like(mg_ref, NEG)
    krow = lax.broadcasted_iota(jnp.int32, (SEL_CHUNK, 1), 0)

    def sel_chunk(ck, causal):
        k0 = pl.multiple_of(ck * SEL_CHUNK, SEL_CHUNK)
        b8 = bias_ref[pl.ds(pl.multiple_of(ck * blk_per_chunk, blk_per_chunk), blk_per_chunk), :]
        qa_ref[HEAD_DIM:HEAD_DIM + AUG_ROWS, :] = jnp.concatenate(
            [jnp.concatenate([b8] * NSA_GROUP, axis=1), alibi_rows], axis=0).astype(bf16)
        s = _dot(ks_ref[0, 0, pl.ds(k0, SEL_CHUNK), :], qa_ref[...])
        if causal:
            s = jnp.where(k0 + krow <= tq, s, NEG)
        shift = c * k0.astype(f32)
        m_old = mg_ref[...] - shift
        m_new = jnp.maximum(m_old, jnp.max(s, axis=0, keepdims=True))
        p = jnp.exp2(s - m_new).astype(bf16)
        acc_ref[...] = acc_ref[...] * jnp.exp2(m_old - m_new) + _dot(vts_ref[0, 0, ck], p)
        mg_ref[...] = m_new + shift

    def past_chunk(ck, carry):
        @pl.when(used_ref[ck] > 0)
        def _():
            sel_chunk(ck, causal=False)
        return carry

    last = q0 // SEL_CHUNK
    sel_chunk(last, causal=True)

    w0 = pl.multiple_of(jnp.maximum(q0 - WINDOW, 0), Q_BLOCK)
    kw = kw_ref[0, 0, pl.ds(w0, WIN_KEYS), :]
    q_pad = jnp.concatenate([qt, jnp.zeros((LANES - HEAD_DIM, cols), bf16)], axis=0)
    wrow = lax.broadcasted_iota(jnp.int32, (WIN_KEYS, 1), 0)
    d_w = tq - (w0 + wrow)
    e, _ = softmax_cols(_dot(kw, q_pad) - c * d_w.astype(f32), (d_w >= 0) & (d_w <= WINDOW))
    e = e.astype(bf16)
    t0 = w0 // Q_BLOCK
    acc_w = jnp.zeros((VT_ROWS, cols), f32)
    for t in range(WIN_KEYS // Q_BLOCK):
        acc_w = acc_w + _dot(vtw_ref[0, 0, t0 + t], e[t * Q_BLOCK:(t + 1) * Q_BLOCK])
    ow_t = acc_w[0:HEAD_DIM] * (1.0 / jnp.maximum(acc_w[HEAD_DIM:HEAD_DIM + 1], 1e-30))

    lax.fori_loop(0, last, past_chunk, 0)
    acc = acc_ref[...]
    os_t = acc[0:HEAD_DIM] * (1.0 / jnp.maximum(acc[HEAD_DIM:HEAD_DIM + 1], 1e-30))

    gate_t = _sigmoid(gate_ref[...]).T

    def gate_row(r, j):
        lo_row = gate_t[r * 3 + j:r * 3 + j + 1, :]
        hi_row = gate_t[(NSA_GROUP + r) * 3 + j:(NSA_GROUP + r) * 3 + j + 1, :]
        return jnp.where(g == 0, lo_row, hi_row)

    outs = []
    for r in range(NSA_GROUP):
        cs = slice(r * Q_BLOCK, (r + 1) * Q_BLOCK)
        outs.append(gate_row(r, 0) * oc_t[:, cs] + gate_row(r, 1) * os_t[:, cs] + gate_row(r, 2) * ow_t[:, cs])
    for pair in range(NSA_GROUP // 2):
        both = jnp.concatenate([outs[2 * pair], outs[2 * pair + 1]], axis=0)
        o_ref[:, pair * LANES:(pair + 1) * LANES] = both.T


def _nsa_prompt_t(q_t, z, comp, k_slc, vt_slc, k_win, vt_win, m_t):
    bsz, _, nq = q_t.shape[:3]
    seq = k_slc.shape[2]
    cols = NSA_GROUP * Q_BLOCK
    gate_blk = COL_GATE // LANES
    per_bg = lambda b, g, i: (b, g, 0, 0)
    per_bg5 = lambda b, g, i: (b, g, 0, 0, 0)
    return pl.pallas_call(
        _nsa_prompt_t_kernel,
        grid=(bsz, NSA_KV_HEADS, nq),
        in_specs=[pl.BlockSpec((1, 1, 1, HEAD_DIM, cols), lambda b, g, i: (b, g, i, 0, 0)),
                  pl.BlockSpec((Q_BLOCK, LANES), lambda b, g, i: (b * nq + i, gate_blk)),
                  pl.BlockSpec((1, 1, N_CMP_PAD, HEAD_DIM), per_bg),
                  pl.BlockSpec((1, 1, N_CMP_PAD, HEAD_DIM), lambda b, g, i: (b, NSA_KV_HEADS + g, 0, 0)),
                  pl.BlockSpec((1, 1, seq, LANES), per_bg),
                  pl.BlockSpec((1, 1) + vt_slc.shape[2:], per_bg5),
                  pl.BlockSpec((1, 1, seq, LANES), per_bg),
                  pl.BlockSpec((1, 1) + vt_win.shape[2:], per_bg5),
                  pl.BlockSpec((N_SELBLK, N_CMP_PAD), lambda b, g, i: (0, 0))],
        out_specs=pl.BlockSpec((Q_BLOCK, NSA_GROUP * HEAD_DIM), lambda b, g, i: (b * nq + i, g)),
        out_shape=jax.ShapeDtypeStruct((bsz * seq, NSA_WIDTH), f32),
        scratch_shapes=[pltpu.VMEM((LANES, cols), bf16),
                        pltpu.VMEM((VT_ROWS, cols), f32),
                        pltpu.VMEM((1, cols), f32),
                        pltpu.VMEM((N_SELBLK, Q_BLOCK), f32),
                        pltpu.SMEM((vt_slc.shape[2],), jnp.int32)],
        compiler_params=_cparams(("parallel", "parallel", "arbitrary")),
        name="nsa_prompt",
    )(q_t, z, comp, comp, k_slc, vt_slc, k_win, vt_win, m_t)


def _nsa_prompt_pair_kernel(qt_ref, gate_ref, comp_ref, caug_ref, ks_ref, vts_ref, kw_ref, vtw_ref, mt_ref, o_ref,
                            qa_ref, acc_ref, mg_ref, bias_ref, need_ref):
    qi = pl.program_id(1)
    q0 = qi * Q_BLOCK
    cols = NSA_GROUP * Q_BLOCK
    heads = range(NSA_KV_HEADS)

    lane = lax.broadcasted_iota(jnp.int32, (1, cols), 1)
    head = jnp.right_shift(lane, Q_BLOCK.bit_length() - 1)
    tq = q0 + jnp.bitwise_and(lane, Q_BLOCK - 1)
    blk = lax.broadcasted_iota(jnp.int32, (N_SELBLK, Q_BLOCK), 0)
    jt = jnp.right_shift(q0 + lax.broadcasted_iota(jnp.int32, (N_SELBLK, Q_BLOCK), 1), SEL_BLOCK.bit_length() - 1)
    valid = blk <= jt
    forced = (blk == 0) | (blk == jt) | (blk == jt - 1)
    c_end = lax.broadcasted_iota(jnp.int32, (N_CMP_PAD, 1), 0) * CMP_STRIDE + (CMP_BLOCK - 1)
    krow = lax.broadcasted_iota(jnp.int32, (SEL_CHUNK, 1), 0)
    arow = lax.broadcasted_iota(jnp.int32, (AUG_ROWS // 2, cols), 0)
    blk_per_chunk = SEL_CHUNK // SEL_BLOCK
    n_chunk_max = vts_ref.shape[2]
    last = q0 // SEL_CHUNK
    w0 = pl.multiple_of(jnp.maximum(q0 - WINDOW, 0), Q_BLOCK)
    wkey = w0 + lax.broadcasted_iota(jnp.int32, (WIN_KEYS, 1), 0)
    in_window = (wkey <= tq) & (wkey >= tq - WINDOW)
    gate_t = _sigmoid(gate_ref[...]).T

    def softmax_cols(s, mask):
        s = jnp.where(mask, s, NEG)
        m = jnp.max(s, axis=0, keepdims=True)
        e = jnp.where(mask, jnp.exp2(s - m), 0.0)
        return e, 1.0 / jnp.maximum(jnp.sum(e, axis=0, keepdims=True), 1e-30)

    slope = [jnp.exp2(-(head.astype(f32) + 1.0 + NSA_GROUP * g)) * LOG2E for g in heads]
    qts = [qt_ref[0, g, 0] for g in heads]

    def chunk_scores(g, ck, slot=0, live=None):
        k0 = pl.multiple_of(ck * SEL_CHUNK, SEL_CHUNK)
        b8 = bias_ref[g, pl.ds(pl.multiple_of(ck * blk_per_chunk, blk_per_chunk), blk_per_chunk), :]
        if live is not None:
            b8 = jnp.where(live, b8, -MASK_BIG)
        qa_ref[g, slot, HEAD_DIM:HEAD_DIM + AUG_ROWS, :] = jnp.concatenate(
            [jnp.concatenate([b8] * NSA_GROUP, axis=1), alibi_rows[g]], axis=0).astype(bf16)
        return _dot(ks_ref[0, g, pl.ds(k0, SEL_CHUNK), :], qa_ref[g, slot])

    def sel_chunk(g, ck, causal, s=None):
        c = slope[g]
        k0 = pl.multiple_of(ck * SEL_CHUNK, SEL_CHUNK)
        if s is None:
            s = chunk_scores(g, ck)
        if causal:
            s = jnp.where(k0 + krow <= tq, s, NEG)
        shift = c * k0.astype(f32)
        m_old = mg_ref[g] - shift
        m_new = jnp.maximum(m_old, jnp.max(s, axis=0, keepdims=True))
        p = jnp.exp2(s - m_new).astype(bf16)
        acc_ref[g] = acc_ref[g] * jnp.exp2(m_old - m_new) + _dot(vts_ref[0, g, ck], p)
        mg_ref[g] = m_new + shift

    alibi_rows, s_cmp, s_win = [], [], []
    for g in heads:
        c, qt = slope[g], qts[g]
        c_hi = c.astype(bf16).astype(f32)
        c_mid = (c - c_hi).astype(bf16).astype(f32)
        c_lo = (c - c_hi - c_mid).astype(bf16).astype(f32)
        c3 = jnp.where((arow == 0) | (arow == 3), c_hi, jnp.where((arow == 1) | (arow == 4), c_mid, c_lo))
        alibi_rows.append(jnp.where(arow >= 6, 0.0, c3))
        pos_rows = jnp.where(arow >= 6, 0.0, jnp.where(arow >= 3, KEY_LOW * c3, c3))
        pos_rows = jnp.concatenate([pos_rows, jnp.zeros_like(pos_rows)], axis=0).astype(bf16)
        q_pos = jnp.concatenate([qt, pos_rows, jnp.zeros((LANES - HEAD_DIM - AUG_ROWS, cols), bf16)], axis=0)
        kc = jnp.concatenate([comp_ref[0, g].astype(bf16), caug_ref[...]], axis=1)
        s_cmp.append(_dot(kc, q_pos))
        kw = kw_ref[0, g, pl.ds(w0, WIN_KEYS), :]
        s_win.append(_dot(kw, q_pos))

    oc_t, ps_t = [], []
    for g in heads:
        e, rl = softmax_cols(s_cmp[g], c_end <= tq)
        p_c = e * rl
        oc_t.append(_dot_tn(comp_ref[0, NSA_KV_HEADS + g].astype(bf16), p_c.astype(bf16)))
        imp_t = (p_c[:, 0:Q_BLOCK] + p_c[:, Q_BLOCK:2 * Q_BLOCK]
                 + p_c[:, 2 * Q_BLOCK:3 * Q_BLOCK] + p_c[:, 3 * Q_BLOCK:4 * Q_BLOCK])
        hi = imp_t.astype(bf16)
        lo = (imp_t - hi.astype(f32)).astype(bf16)
        ps_t.append(_dot(mt_ref[...], hi) + _dot(mt_ref[...], lo))

    ow_t = []
    t0 = w0 // Q_BLOCK
    for g in heads:
        s_w = jnp.where(in_window, s_win[g], NEG)
        e = jnp.exp2(s_w - jnp.max(s_w, axis=0, keepdims=True)).astype(bf16)
        acc_w = jnp.zeros((VT_ROWS, cols), f32)
        for t in range(WIN_KEYS // Q_BLOCK):
            acc_w = acc_w + _dot(vtw_ref[0, g, t0 + t], e[t * Q_BLOCK:(t + 1) * Q_BLOCK])
        ow_t.append(acc_w[0:HEAD_DIM] * (1.0 / jnp.maximum(acc_w[HEAD_DIM:HEAD_DIM + 1], 1e-30)))

    sel_t = [jnp.where(forced, 1.0, 0.0)] * NSA_KV_HEADS
    score = [jnp.where(valid & jnp.logical_not(forced), ps_t[g], NEG) for g in heads]
    for _ in range(N_SEL - N_FORCED):
        for g in heads:
            best = jnp.max(score[g], axis=0, keepdims=True)
            idx = jnp.min(jnp.where(score[g] == best, blk, N_SELBLK), axis=0, keepdims=True)
            hit = blk == idx
            sel_t[g] = jnp.where(hit, 1.0, sel_t[g])
            score[g] = jnp.where(hit, REMOVED, score[g])
    sel_t = [jnp.where(valid, s, 0.0) for s in sel_t]

    for g in heads:
        bias_ref[g] = (sel_t[g] - 1.0) * MASK_BIG
        for slot in range(qa_ref.shape[1]):
            qa_ref[g, slot, 0:HEAD_DIM, :] = qts[g]
            qa_ref[g, slot, HEAD_DIM + AUG_ROWS:, :] = jnp.zeros((LANES - HEAD_DIM - AUG_ROWS, cols), bf16)
        acc_ref[g] = jnp.zeros(acc_ref.shape[1:], f32)
        mg_ref[g] = jnp.full(mg_ref.shape[1:], NEG, f32)
    own = [chunk_scores(g, last) for g in heads]
    for g in heads:
        sel_chunk(g, last, causal=True, s=own[g])

    n_need = jnp.int32(0)
    for ck in range(n_chunk_max):
        picked = jnp.maximum(*[jnp.max(sel_t[g][ck * blk_per_chunk:(ck + 1) * blk_per_chunk, :]) for g in heads])
        need_ref[n_need] = ck
        n_need = n_need + ((picked > 0.5) & (ck < last)).astype(jnp.int32)

    def chunk_pair(i, carry):
        first = 2 * i
        live = first + 1 < n_need
        cks = [need_ref[first], need_ref[jnp.minimum(first + 1, n_need - 1)]]
        scores = [[chunk_scores(g, cks[slot], slot, live if slot else None) for g in heads] for slot in range(2)]
        for slot in range(2):
            for g in heads:
                sel_chunk(g, cks[slot], causal=False, s=scores[slot][g])
        return carry

    lax.fori_loop(0, (n_need + 1) // 2, chunk_pair, 0)

    for g in heads:
        acc = acc_ref[g]
        os_t = acc[0:HEAD_DIM] * (1.0 / jnp.maximum(acc[HEAD_DIM:HEAD_DIM + 1], 1e-30))
        outs = []
        for r in range(NSA_GROUP):
            cs = slice(r * Q_BLOCK, (r + 1) * Q_BLOCK)
            row = (g * NSA_GROUP + r) * 3
            outs.append(gate_t[row:row + 1, :] * oc_t[g][:, cs] + gate_t[row + 1:row + 2, :] * os_t[:, cs]
                        + gate_t[row + 2:row + 3, :] * ow_t[g][:, cs])
        for pair in range(NSA_GROUP // 2):
            both = jnp.concatenate([outs[2 * pair], outs[2 * pair + 1]], axis=0)
            lane0 = (g * NSA_GROUP // 2 + pair) * LANES
            o_ref[:, lane0:lane0 + LANES] = both.T


def _nsa_prompt_pair(q_t, z, comp, k_slc, vt_slc, k_win, vt_win, m_t):
    bsz, _, nq = q_t.shape[:3]
    seq = k_slc.shape[2]
    cols = NSA_GROUP * Q_BLOCK
    gate_blk = COL_GATE // LANES
    per_b4 = lambda b, i: (b, 0, 0, 0)
    per_b5 = lambda b, i: (b, 0, 0, 0, 0)
    return pl.pallas_call(
        _nsa_prompt_pair_kernel,
        grid=(bsz, nq),
        in_specs=[pl.BlockSpec((1, NSA_KV_HEADS, 1, HEAD_DIM, cols), lambda b, i: (b, 0, i, 0, 0)),
                  pl.BlockSpec((Q_BLOCK, LANES), lambda b, i: (b * nq + i, 0)),
                  pl.BlockSpec((1, N_COMBO, N_CMP_PAD, HEAD_DIM), per_b4),
                  pl.BlockSpec((N_CMP_PAD, HEAD_DIM), lambda b, i: (0, 0)),
                  pl.BlockSpec((1, NSA_KV_HEADS, seq, LANES), per_b4),
                  pl.BlockSpec((1,) + vt_slc.shape[1:], per_b5),
                  pl.BlockSpec((1, NSA_KV_HEADS, seq, LANES), per_b4),
                  pl.BlockSpec((1,) + vt_win.shape[1:], per_b5),
                  pl.BlockSpec((N_SELBLK, N_CMP_PAD), lambda b, i: (0, 0))],
        out_specs=pl.BlockSpec((Q_BLOCK, NSA_WIDTH), lambda b, i: (b * nq + i, 0)),
        out_shape=jax.ShapeDtypeStruct((bsz * seq, NSA_WIDTH), f32),
        scratch_shapes=[pltpu.VMEM((NSA_KV_HEADS, 2, LANES, cols), bf16),
                        pltpu.VMEM((NSA_KV_HEADS, VT_ROWS, cols), f32),
                        pltpu.VMEM((NSA_KV_HEADS, 1, cols), f32),
                        pltpu.VMEM((NSA_KV_HEADS, N_SELBLK, Q_BLOCK), f32),
                        pltpu.SMEM((vt_slc.shape[2],), jnp.int32)],
        compiler_params=_cparams(("parallel", "arbitrary")),
        name="nsa_prompt",
    )(q_t, z, comp, _cmp_position_columns(), k_slc, vt_slc, k_win, vt_win, m_t)


def _position_columns(pos):
    low, high = (pos % KEY_LOW)[:, None], (pos // KEY_LOW)[:, None]
    return np.concatenate([low, low, low, high, high, high], axis=1).astype(np.float32)


def _cmp_position_columns():
    aug = np.zeros((N_CMP_PAD, HEAD_DIM), np.float32)
    aug[:, 0:6] = _position_columns(np.arange(N_CMP_PAD) * CMP_STRIDE + CMP_BLOCK - 1)
    return jnp.asarray(aug, dtype=bf16)


LAYOUT_ROWS = 1024


def _attention_layout_kernel(q_ref, cmp_ref, slc_ref, win_ref, xs_ref, xw_ref,
                             qt_ref, ks_ref, vts_ref, kw_ref, vtw_ref, cmp_t_ref, slc_t_ref, win_t_ref):
    cmp_t_ref[0] = cmp_ref[...].T
    tm = q_ref.shape[0]
    q = q_ref[...] * (HEAD_DIM ** -0.5 * LOG2E)
    for qb in range(tm // Q_BLOCK):
        rows = slice(qb * Q_BLOCK, (qb + 1) * Q_BLOCK)
        for pair in range(NSA_HEADS // 2):
            t = q[rows, pair * LANES:(pair + 1) * LANES].T
            for k in range(2):
                g, r = divmod(2 * pair + k, NSA_GROUP)
                qt_ref[0, g, qb, :, r * Q_BLOCK:(r + 1) * Q_BLOCK] = t[k * HEAD_DIM:(k + 1) * HEAD_DIM].astype(bf16)

    for src_ref, extra_ref, k_ref, vt_ref, t_ref in ((slc_ref, xs_ref, ks_ref, vts_ref, slc_t_ref),
                                                     (win_ref, xw_ref, kw_ref, vtw_ref, win_t_ref)):
        tile = vt_ref.shape[4]
        s = src_ref[...]
        s_t = s.T
        t_ref[0] = s_t
        v_t = s_t[KV_WIDTH:]
        tail = jnp.where(lax.broadcasted_iota(jnp.int32, (VT_ROWS - HEAD_DIM, tile), 0) == 0, 1.0, 0.0)
        for g in range(NSA_KV_HEADS):
            k_ref[0, g] = jnp.concatenate([s[:, g * HEAD_DIM:(g + 1) * HEAD_DIM].astype(bf16), extra_ref[...]], axis=1)
            for c in range(tm // tile):
                blk = v_t[g * HEAD_DIM:(g + 1) * HEAD_DIM, c * tile:(c + 1) * tile]
                vt_ref[0, g, c] = jnp.concatenate([blk, tail], axis=0).astype(bf16)


def _attention_layout(z, cmp, slc, win, bsz, seq):
    tm = LAYOUT_ROWS
    per_b = seq // tm
    pos = np.arange(seq)
    xs = np.zeros((seq, HEAD_DIM), np.float32)
    xs[pos, (pos // SEL_BLOCK) % (SEL_CHUNK // SEL_BLOCK)] = 1.0
    xs[:, 8:11] = (pos % KEY_LOW)[:, None]
    xs[:, 11:14] = (pos % SEL_CHUNK - pos % KEY_LOW)[:, None]
    xw = np.zeros((seq, HEAD_DIM), np.float32)
    xw[:, 0:6] = _position_columns(pos)
    row = lambda i: (i, 0)
    rep = lambda i: (i % per_b, 0)
    out4 = lambda i: (i // per_b, 0, i % per_b, 0)
    out5 = lambda i: (i // per_b, 0, i % per_b, 0, 0)
    k_shape = jax.ShapeDtypeStruct((bsz, NSA_KV_HEADS, seq, LANES), bf16)
    vt_shape = lambda tile: jax.ShapeDtypeStruct((bsz, NSA_KV_HEADS, seq // tile, VT_ROWS, tile), bf16)
    vt_spec = lambda tile: pl.BlockSpec((1, NSA_KV_HEADS, tm // tile, VT_ROWS, tile), out5)
    rows_t_shape = jax.ShapeDtypeStruct((bsz, ROW_WIDTH, seq), f32)
    rows_t_spec = pl.BlockSpec((1, ROW_WIDTH, tm), lambda i: (i // per_b, 0, i % per_b))
    return pl.pallas_call(
        _attention_layout_kernel,
        grid=(bsz * per_b,),
        in_specs=[pl.BlockSpec((tm, NSA_WIDTH), row), pl.BlockSpec((tm, ROW_WIDTH), row),
                  pl.BlockSpec((tm, ROW_WIDTH), row), pl.BlockSpec((tm, ROW_WIDTH), row),
                  pl.BlockSpec((tm, HEAD_DIM), rep), pl.BlockSpec((tm, HEAD_DIM), rep)],
        out_specs=[pl.BlockSpec((1, NSA_KV_HEADS, tm // Q_BLOCK, HEAD_DIM, NSA_GROUP * Q_BLOCK), out5),
                   pl.BlockSpec((1, NSA_KV_HEADS, tm, LANES), out4), vt_spec(SEL_CHUNK),
                   pl.BlockSpec((1, NSA_KV_HEADS, tm, LANES), out4), vt_spec(Q_BLOCK),
                   rows_t_spec, rows_t_spec, rows_t_spec],
        out_shape=[jax.ShapeDtypeStruct((bsz, NSA_KV_HEADS, seq // Q_BLOCK, HEAD_DIM, NSA_GROUP * Q_BLOCK), bf16),
                   k_shape, vt_shape(SEL_CHUNK), k_shape, vt_shape(Q_BLOCK),
                   rows_t_shape, rows_t_shape, rows_t_shape],
        compiler_params=_cparams(("parallel",)),
        name="attention_layout",
    )(z, cmp, slc, win, jnp.asarray(xs, dtype=bf16), jnp.asarray(xw, dtype=bf16))


def _attention_keys(piece, bsz, seq, augment):
    tile = SEL_CHUNK if augment else Q_BLOCK
    t = piece.reshape(bsz, seq, 2, NSA_KV_HEADS, HEAD_DIM)
    k = t[:, :, 0].transpose(0, 2, 1, 3)
    pos = np.arange(seq)
    extra = np.zeros((seq, LANES - HEAD_DIM), np.float32)
    if augment:
        extra[np.arange(seq), (pos // SEL_BLOCK) % (SEL_CHUNK // SEL_BLOCK)] = 1.0
        extra[:, 8:11] = (pos % KEY_LOW)[:, None]
        extra[:, 11:14] = (pos % SEL_CHUNK - pos % KEY_LOW)[:, None]
    else:
        extra[:, 0:6] = _position_columns(pos)
    k_aug = jnp.concatenate([k, jnp.broadcast_to(jnp.asarray(extra), k.shape[:2] + extra.shape)], axis=-1)
    v_t = t[:, :, 1].transpose(0, 2, 3, 1)
    ones = jnp.ones(v_t.shape[:2] + (1, seq), f32)
    zeros = jnp.zeros(v_t.shape[:2] + (VT_ROWS - HEAD_DIM - 1, seq), f32)
    v_aug = jnp.concatenate([v_t, ones, zeros], axis=2)
    v_aug = v_aug.reshape(bsz, NSA_KV_HEADS, VT_ROWS, seq // tile, tile).transpose(0, 1, 3, 2, 4)
    return k_aug.astype(bf16), v_aug.astype(bf16)


HG_CHUNK = 128
HG_BAND = 8


def _lower_bound(lo):
    m = jnp.max(lo, axis=0, keepdims=True)
    e = jnp.exp(lo - m)
    return e[0:1] / jnp.sum(e, axis=0, keepdims=True)


def _hgrn_gates(q, fl, lb):
    sg = _sigmoid(fl)
    f = lb + (1.0 - lb) * sg
    k = (1.0 - lb) * (1.0 - sg)
    return _silu(q), k, jnp.log(f)


def _hgrn_out(o, nw, zb):
    o = o * lax.rsqrt(jnp.mean(o * o, axis=-1, keepdims=True) + EPS) * nw
    return o * _silu(zb)


def _hgrn_prompt_kernel(q_ref, f_ref, i_ref, zb_ref, lo_ref, nw_ref, y_ref, s_ref, st_ref):
    c = pl.program_id(1)

    @pl.when(c == 0)
    def _():
        st_ref[...] = jnp.zeros_like(st_ref)

    n = HG_CHUNK
    row = lax.broadcasted_iota(jnp.int32, (n, n), 0)
    col = lax.broadcasted_iota(jnp.int32, (n, n), 1)
    heads = [slice(h * HG_DIM, (h + 1) * HG_DIM) for h in range(HG_HEADS)]

    qq, kk, logf = _hgrn_gates(q_ref[...], f_ref[...], _lower_bound(lo_ref[...]))
    b = logf
    trow = lax.broadcasted_iota(jnp.int32, (n, 1), 0)
    step = 1
    while step < n:
        b = b + jnp.where(trow >= step, pltpu.roll(b, step, axis=0), 0.0)
        step *= 2

    intra = [_hgrn_intra(qq[:, hs], kk[:, hs], b[:, hs], row, col) for hs in heads]
    v = i_ref[...]
    for h, hs in enumerate(heads):
        st = st_ref[h]
        bh, kh, vh = b[:, hs], kk[:, hs], v[:, hs].astype(bf16)
        o = _dot(intra[h].astype(bf16), vh) + _dot_nt((qq[:, hs] * jnp.exp(bh)).astype(bf16), st.astype(bf16))
        b_last = bh[n - 1:n]
        st_new = st * jnp.exp(b_last) + _dot_tn(vh, (kh * jnp.exp(b_last - bh)).astype(bf16))
        st_ref[h] = st_new
        y_ref[:, hs] = _hgrn_out(o, nw_ref[...], zb_ref[:, hs])

        @pl.when(c == pl.num_programs(1) - 1)
        def _():
            s_ref[0, h] = st_new.T


def _hgrn_intra(qq, kk, b, row, col):
    n = HG_CHUNK
    a = jnp.where(row == col, _dot_nt(qq.astype(bf16), kk.astype(bf16)), 0.0)

    sub = jnp.bitwise_and(row, HG_BAND - 1)
    b3 = b.reshape(n // HG_BAND, HG_BAND, HG_DIM)
    size = 2
    while size <= HG_BAND:
        half = size // 2
        beta = None
        for s0 in range(0, HG_BAND, size):
            piece = jnp.broadcast_to(b3[:, s0 + half - 1:s0 + half, :], b3.shape).reshape(n, HG_DIM)
            beta = piece if beta is None else jnp.where(sub >= s0, piece, beta)
        left = jnp.bitwise_and(sub, size - 1) < half
        decay = jnp.exp(jnp.where(left, beta - b, b - beta))
        part = _dot_nt(jnp.where(left, 0.0, qq * decay).astype(bf16), jnp.where(left, kk * decay, 0.0).astype(bf16))
        sh = size.bit_length() - 1
        a = a + jnp.where(jnp.right_shift(row, sh) == jnp.right_shift(col, sh), part, 0.0)
        size *= 2

    while size <= n:
        half = size // 2
        qs, ks = [], []
        zero = jnp.zeros((half, HG_DIM), f32)
        for r0 in range(0, n, size):
            beta = b[r0 + half - 1:r0 + half]
            ks += [kk[r0:r0 + half] * jnp.exp(beta - b[r0:r0 + half]), zero]
            qs += [zero, qq[r0 + half:r0 + size] * jnp.exp(b[r0 + half:r0 + size] - beta)]
        part = _dot_nt(jnp.concatenate(qs, axis=0).astype(bf16), jnp.concatenate(ks, axis=0).astype(bf16))
        if size < n:
            sh = size.bit_length() - 1
            part = jnp.where(jnp.right_shift(row, sh) == jnp.right_shift(col, sh), part, 0.0)
        a = a + part
        size *= 2
    return a


def _hgrn_prompt(z, hg_lower, hg_norm, bsz, seq):
    nch = seq // HG_CHUNK

    def zcol(base):
        return pl.BlockSpec((HG_CHUNK, HG_WIDTH), lambda b, c: (b * nch + c, base // HG_WIDTH))

    return pl.pallas_call(
        _hgrn_prompt_kernel,
        grid=(bsz, nch),
        in_specs=[zcol(COL_QB), zcol(COL_FB), zcol(COL_IB), zcol(COL_ZB),
                  pl.BlockSpec(hg_lower.shape, lambda b, c: (0, 0)),
                  pl.BlockSpec((1, HG_DIM), lambda b, c: (0, 0))],
        out_specs=[pl.BlockSpec((HG_CHUNK, HG_WIDTH), lambda b, c: (b * nch + c, 0)),
                   pl.BlockSpec((1, HG_HEADS, HG_DIM, HG_DIM), lambda b, c: (b, 0, 0, 0))],
        out_shape=[jax.ShapeDtypeStruct((bsz * seq, HG_WIDTH), f32),
                   jax.ShapeDtypeStruct((bsz, HG_HEADS, HG_DIM, HG_DIM), f32)],
        scratch_shapes=[pltpu.VMEM((HG_HEADS, HG_DIM, HG_DIM), f32)],
        compiler_params=_cparams(("parallel", "arbitrary")),
        name="hgrn_prompt",
    )(z, z, z, z, hg_lower, hg_norm)


def _hgrn_sample_kernel(q_ref, f_ref, i_ref, zb_ref, lo_ref, nw_ref, s_ref, y_ref, so_ref, o_scr):
    nb = q_ref.shape[0]
    lb = _lower_bound(lo_ref[...])
    qq, kk, logf = _hgrn_gates(q_ref[...], f_ref[...], lb)
    v = i_ref[...]
    pad = jnp.zeros((HG_DIM - nb, HG_DIM), f32)
    to_cols = lambda x: jnp.concatenate([x, pad], axis=0).T
    f_t, k_t, q_t = to_cols(jnp.exp(logf)), to_cols(kk), to_cols(qq)
    for b in range(nb):
        s_new = f_t[:, b:b + 1] * s_ref[0, b, 0] + k_t[:, b:b + 1] * v[b:b + 1, :]
        so_ref[0, b, 0] = s_new
        o_scr[b:b + 1, :] = jnp.sum(s_new * q_t[:, b:b + 1], axis=0, keepdims=True)
    y_ref[...] = _hgrn_out(o_scr[...], nw_ref[...], zb_ref[...])


def _hgrn_sample(z, state, hg_lower, hg_norm):
    nb = z.shape[0]

    def zcol(base):
        return pl.BlockSpec((nb, HG_DIM), lambda h: (0, base // HG_DIM + h))

    sblk = pl.BlockSpec((1, nb, 1, HG_DIM, HG_DIM), lambda h: (0, 0, h, 0, 0))
    return pl.pallas_call(
        _hgrn_sample_kernel,
        grid=(HG_HEADS,),
        in_specs=[zcol(COL_QB), zcol(COL_FB), zcol(COL_IB), zcol(COL_ZB),
                  pl.BlockSpec((hg_lower.shape[0], HG_DIM), lambda h: (0, h)),
                  pl.BlockSpec((1, HG_DIM), lambda h: (0, 0)),
                  sblk],
        out_specs=[pl.BlockSpec((nb, HG_DIM), lambda h: (0, h)), sblk],
        out_shape=[jax.ShapeDtypeStruct((nb, HG_WIDTH), f32),
                   jax.ShapeDtypeStruct(state.shape, f32)],
        scratch_shapes=[pltpu.VMEM((nb, HG_DIM), f32)],
        compiler_params=_cparams(("parallel",)),
        name="hgrn_sample",
    )(z, z, z, z, hg_lower, hg_norm, state)


N_PICK = N_SEL - 1


def _head_slopes():
    h = lax.broadcasted_iota(jnp.int32, (NSA_HEADS, 1), 0).astype(f32)
    return jnp.exp2(-(h + 1.0))


def _nsa_sample_cmp_kernel(q_ref, comp_ref, m_ref, oc_ref, ps_ref, *, t_pos):
    q = q_ref[0] * jnp.asarray(HEAD_DIM ** -0.5, bf16)
    slope = _head_slopes()
    head = lax.broadcasted_iota(jnp.int32, (NSA_HEADS, 1), 0)
    ncol = lax.broadcasted_iota(jnp.int32, (1, N_CMP_PAD), 1)
    d_c = t_pos - (ncol * CMP_STRIDE + (CMP_BLOCK - 1))
    mask = d_c >= 0
    o_c = jnp.zeros((NSA_HEADS, HEAD_DIM), f32)
    ps_ref[...] = jnp.zeros_like(ps_ref)
    for g in range(NSA_KV_HEADS):
        kc = comp_ref[0, g].astype(bf16)
        vc = comp_ref[0, NSA_KV_HEADS + g].astype(bf16)
        s = jnp.where(mask, _dot_nt(q, kc) - slope * d_c.astype(f32), NEG)
        m = jnp.max(s, axis=1, keepdims=True)
        e = jnp.where(mask, jnp.exp(s - m), 0.0)
        p = e * (1.0 / jnp.maximum(jnp.sum(e, axis=1, keepdims=True), 1e-30))
        mine = jnp.right_shift(head, NSA_GROUP.bit_length() - 1) == g
        o_c = jnp.where(mine, _dot(p.astype(bf16), vc), o_c)
        imp = jnp.sum(jnp.where(mine, p, 0.0), axis=0, keepdims=True)
        ps_ref[0, g:g + 1, :] = _importance_to_blocks(imp, m_ref[...])
    oc_ref[0] = o_c


def _nsa_sample_cmp(q8, comp, m_mat, t_pos):
    nb = q8.shape[0]
    return pl.pallas_call(
        functools.partial(_nsa_sample_cmp_kernel, t_pos=t_pos),
        grid=(nb,),
        in_specs=[pl.BlockSpec((1, NSA_HEADS, HEAD_DIM), lambda b: (b, 0, 0)),
                  pl.BlockSpec((1, N_COMBO, N_CMP_PAD, HEAD_DIM), lambda b: (b, 0, 0, 0)),
                  pl.BlockSpec((N_CMP_PAD, N_SELBLK), lambda b: (0, 0))],
        out_specs=[pl.BlockSpec((1, NSA_HEADS, HEAD_DIM), lambda b: (b, 0, 0)),
                   pl.BlockSpec((1, 8, N_SELBLK), lambda b: (b, 0, 0))],
        out_shape=[jax.ShapeDtypeStruct((nb, NSA_HEADS, HEAD_DIM), f32),
                   jax.ShapeDtypeStruct((nb, 8, N_SELBLK), f32)],
        compiler_params=_cparams(("parallel",)),
        name="nsa_sample_cmp",
    )(q8, comp, m_mat)


def _nsa_sample_topk_kernel(ps_ref, idx_ref, *, last_blk):
    ps = ps_ref[...]
    lane = lax.broadcasted_iota(jnp.int32, ps.shape, 1)
    score = ps + jnp.where((lane == 0) | (lane == last_blk), FORCE_BONUS, 0.0)
    out = jnp.zeros(ps.shape, jnp.int32)
    for r in range(N_PICK):
        best = jnp.max(score, axis=1, keepdims=True)
        idx = jnp.min(jnp.where(score == best, lane, N_SELBLK), axis=1, keepdims=True)
        out = jnp.where(lane == r, idx, out)
        score = jnp.where(lane == idx, REMOVED, score)
    idx_ref[...] = out


def _nsa_sample_topk(ps_rows, last_blk):
    return pl.pallas_call(
        functools.partial(_nsa_sample_topk_kernel, last_blk=last_blk),
        out_shape=jax.ShapeDtypeStruct(ps_rows.shape, jnp.int32),
        name="nsa_sample_topk",
    )(ps_rows)


def _nsa_sample_sel_kernel(pt_ref, idx_ref, slc_ref, q_ref, oc_ref, gate_ref, slc_new_ref, win_new_ref, wc_ref,
                           o_ref, wout_ref, kv_scr, sem_ref, *, t_pos):
    n_blk = NSA_KV_HEADS * N_PICK
    halves = PAGE_SIZE // SEL_BLOCK
    b = pl.program_id(0)
    nb = pl.num_programs(0)
    slot = lax.rem(b, 2)

    def picked(seq, k):
        return idx_ref[(seq * NSA_KV_HEADS + k // N_PICK) * N_SEL + k % N_PICK]

    def blk_copies(seq, s, k):
        page = pt_ref[seq, picked(seq, k) // halves]
        g = k // N_PICK
        return [pltpu.make_async_copy(slc_ref.at[page, kind * NSA_KV_HEADS + g], kv_scr.at[s, kind, k],
                                      sem_ref.at[s]) for kind in range(2)]

    def start_all(seq, s):
        for k in range(n_blk):
            for cp in blk_copies(seq, s, k):
                cp.start()

    @pl.when(b == 0)
    def _():
        start_all(0, 0)

    @pl.when(b + 1 < nb)
    def _():
        start_all(b + 1, 1 - slot)

    for k in range(n_blk):
        for cp in blk_copies(b, slot, k):
            cp.wait()

    q = q_ref[0] * jnp.asarray(HEAD_DIM ** -0.5, bf16)
    slope = _head_slopes()
    head = lax.broadcasted_iota(jnp.int32, (NSA_HEADS, 1), 0)
    lane = lax.broadcasted_iota(jnp.int32, (1, PAGE_SIZE), 1)
    new_slc = slc_new_ref[0]
    new_win = win_new_ref[0]
    wl = wc_ref.shape[3]
    wlane = lax.broadcasted_iota(jnp.int32, (1, wl), 1)
    d_w = (wl - wlane).astype(f32)
    qf = q.astype(f32)

    def attend(s, mask, v_t, new_row, g):
        k_new = new_row[:, g * HEAD_DIM:(g + 1) * HEAD_DIM].astype(bf16).astype(f32)
        v_new = new_row[:, KV_WIDTH + g * HEAD_DIM:KV_WIDTH + (g + 1) * HEAD_DIM].astype(bf16).astype(f32)
        s_new = jnp.sum(qf * k_new, axis=1, keepdims=True)
        m = jnp.maximum(jnp.max(s, axis=1, keepdims=True), s_new)
        e, e_new = jnp.exp(s - m), jnp.exp(s_new - m)
        if mask is not None:
            e = jnp.where(mask, e, 0.0)
        den = jnp.maximum(jnp.sum(e, axis=1, keepdims=True) + e_new, 1e-30)
        num = _dot_nt(e.astype(bf16), v_t.astype(bf16)) + e_new.astype(bf16).astype(f32) * v_new
        return num * (1.0 / den)

    o_s = jnp.zeros((NSA_HEADS, HEAD_DIM), f32)
    o_w = jnp.zeros((NSA_HEADS, HEAD_DIM), f32)
    for g in range(NSA_KV_HEADS):
        mine = jnp.right_shift(head, NSA_GROUP.bit_length() - 1) == g
        pos, member = [], []
        for n in range(N_PICK):
            j = picked(b, g * N_PICK + n)
            pos.append((j // halves) * PAGE_SIZE + lane)
            member.append(jnp.right_shift(lane, SEL_BLOCK.bit_length() - 1) == j % halves)
        pos = jnp.concatenate(pos, axis=1)
        member = jnp.concatenate(member, axis=1)
        k_t = jnp.concatenate([kv_scr[slot, 0, g * N_PICK + n] for n in range(N_PICK)], axis=1)
        v_t = jnp.concatenate([kv_scr[slot, 1, g * N_PICK + n] for n in range(N_PICK)], axis=1)
        s = jnp.where(member, _dot(q, k_t.astype(bf16)) - slope * (t_pos - pos).astype(f32), NEG)
        o_s = jnp.where(mine, attend(s, member, v_t, new_slc, g), o_s)
        s = _dot(q, wc_ref[0, g].astype(bf16)) - slope * d_w
        o_w = jnp.where(mine, attend(s, None, wc_ref[0, NSA_KV_HEADS + g], new_win, g), o_w)

    gate = jnp.broadcast_to(_sigmoid(gate_ref[0]), (NSA_HEADS, LANES))
    glane = lax.broadcasted_iota(jnp.int32, (NSA_HEADS, LANES), 1)
    gcol = [jnp.sum(jnp.where(glane == 3 * head + j, gate, 0.0), axis=1, keepdims=True) for j in range(3)]
    o_ref[0] = gcol[0] * oc_ref[0] + gcol[1] * o_s + gcol[2] * o_w
    eye = lax.broadcasted_iota(jnp.int32, (HEAD_DIM, HEAD_DIM), 0) == \
        lax.broadcasted_iota(jnp.int32, (HEAD_DIM, HEAD_DIM), 1)
    for c in range(N_COMBO):
        piece = jnp.broadcast_to(new_win[:, c * HEAD_DIM:(c + 1) * HEAD_DIM], (HEAD_DIM, HEAD_DIM))
        new_col = jnp.sum(jnp.where(eye, piece, 0.0), axis=1, keepdims=True)
        wout_ref[0, c] = jnp.where(wlane == wl - 1, new_col, pltpu.roll(wc_ref[0, c], wl - 1, axis=1))


def _nsa_sample_sel(page_table, idx_flat, slc_pages_t, q8, o_c, gates, slc_new, win_new, cache_win_t, t_pos):
    nb = q8.shape[0]
    wl = cache_win_t.shape[3]
    n_blk = NSA_KV_HEADS * N_PICK
    per_seq = lambda b, pt, idx: (b, 0, 0)
    win_blk = pl.BlockSpec((1, N_COMBO, HEAD_DIM, wl), lambda b, pt, idx: (b, 0, 0, 0))
    grid_spec = pltpu.PrefetchScalarGridSpec(
        num_scalar_prefetch=2,
        grid=(nb,),
        in_specs=[pl.BlockSpec(memory_space=pl.ANY),
                  pl.BlockSpec((1, NSA_HEADS, HEAD_DIM), per_seq),
                  pl.BlockSpec((1, NSA_HEADS, HEAD_DIM), per_seq),
                  pl.BlockSpec((1, 1, LANES), per_seq),
                  pl.BlockSpec((1, 1, 2 * KV_WIDTH), per_seq),
                  pl.BlockSpec((1, 1, 2 * KV_WIDTH), per_seq),
                  win_blk],
        out_specs=[pl.BlockSpec((1, NSA_HEADS, HEAD_DIM), per_seq), win_blk],
        scratch_shapes=[pltpu.VMEM((2, 2, n_blk, HEAD_DIM, PAGE_SIZE), f32),
                        pltpu.SemaphoreType.DMA((2,))],
    )
    return pl.pallas_call(
        functools.partial(_nsa_sample_sel_kernel, t_pos=t_pos),
        grid_spec=grid_spec,
        out_shape=[jax.ShapeDtypeStruct((nb, NSA_HEADS, HEAD_DIM), f32),
                   jax.ShapeDtypeStruct(cache_win_t.shape, f32)],
        compiler_params=_cparams(("arbitrary",)),
        name="nsa_sample_sel",
    )(page_table, idx_flat, slc_pages_t, q8, o_c, gates, slc_new, win_new, cache_win_t)


def kernel(x_prompt, x_sample, cache_cmp_kv, cache_slc_kv, cache_win_kv, state_hgrn, page_table, p_prompt, p_sample,
           w_in, g_pre, cmp_pe, cmp_w1, cmp_b1, cmp_w2, hg_lower, hg_norm, w_out, g_post, ple_proj, ple_gate):
    bsz, seq, _ = x_prompt.shape
    nb = x_sample.shape[0]
    n_pool = cache_cmp_kv.shape[1]
    n_pages = page_table.shape[1]
    past = n_pages * PAGE_SIZE
    nq = seq // Q_BLOCK
    kv_tail = (2, NSA_KV_HEADS, HEAD_DIM)

    o = _ORIG
    w = w_in[0]
    w_r = jnp.concatenate([w[:, o['q_a']:o['k_cmp']], w[:, o['z_a']:o['end']], w[:, o['k_cmp']:o['gate_a']],
                           w[:, o['gate_a']:o['z_a']],
                           jnp.zeros((D_MODEL, PROJ_WIDTH - o['end']), f32)], axis=1).astype(bf16)
    cw = _compress_weights(cmp_pe[0], cmp_w1[0], cmp_b1[0], cmp_w2[0])
    m_mat = _sel_map_matrix()
    w_out_b, ple_proj_b, ple_gate_b = w_out[0].astype(bf16), ple_proj[0].astype(bf16), ple_gate[0].astype(bf16)

    xp = x_prompt.reshape(bsz * seq, D_MODEL)
    zp, cmp_p, slc_p, win_p, gates_p = _proj(xp, g_pre, w_r, 1024)
    pages_p = cmp_p.reshape(bsz * seq // PAGE_SIZE, PAGE_SIZE, ROW_WIDTH)
    table_p = jnp.arange(bsz * seq // PAGE_SIZE, dtype=jnp.int32).reshape(bsz, seq // PAGE_SIZE)
    comp_p = _compress(pages_p, table_p, *cw, feature_major=False)
    q_t, k_slc, vt_slc, k_win, vt_win, cmp_tp, slc_tp, win_tp = _attention_layout(zp, cmp_p, slc_p, win_p, bsz, seq)
    o_a = _nsa_prompt_pair(q_t, gates_p, comp_p, k_slc, vt_slc, k_win, vt_win, m_mat.T)
    yb_p, hg_p = _hgrn_prompt(zp, hg_lower, hg_norm, bsz, seq)
    h_p = _finish(xp, o_a, zp, yb_p, p_prompt[0].reshape(bsz * seq, PLE_DIM),
                  w_out_b, g_post, ple_proj_b, ple_gate_b, 512)

    wl_p = min(WINDOW, seq)

    def cache_rows(rows_t):
        return rows_t.reshape((1, bsz) + kv_tail + rows_t.shape[2:]).transpose(0, 1, 5, 2, 3, 4)

    out_prompt = (h_p.reshape(bsz, seq, D_MODEL), cache_rows(cmp_tp), cache_rows(slc_tp),
                  cache_rows(win_tp[:, :, seq - wl_p:]), hg_p[None])

    xs = x_sample.reshape(nb, D_MODEL)
    zs, cmp_s, slc_s, win_new, gates_s = _proj(xs, g_pre, w_r, nb)
    cmp_t = cache_cmp_kv.transpose(0, 1, 3, 4, 5, 2).reshape(n_pool, ROW_WIDTH, PAGE_SIZE)
    comp_s = _compress(cmp_t, page_table, *cw, feature_major=True)
    q8 = zs[:, COL_QA:COL_QA + NSA_WIDTH].astype(bf16).reshape(nb, NSA_HEADS, HEAD_DIM)
    oc_s, ps_s = _nsa_sample_cmp(q8, comp_s, m_mat, past)
    ps_rows = ps_s[:, :NSA_KV_HEADS, :].reshape(nb * NSA_KV_HEADS, N_SELBLK)
    idx = _nsa_sample_topk(ps_rows, past // SEL_BLOCK - 1)
    slc_t = cache_slc_kv.transpose(0, 1, 3, 4, 5, 2).reshape(n_pool, N_COMBO, HEAD_DIM, PAGE_SIZE)
    wl_s = cache_win_kv.shape[2]
    win_t = cache_win_kv.transpose(0, 1, 3, 4, 5, 2).reshape(nb, N_COMBO, HEAD_DIM, wl_s)
    oa_s, win_s = _nsa_sample_sel(page_table, idx[:, :N_SEL].reshape(-1), slc_t, q8, oc_s,
                                  gates_s.reshape(nb, 1, LANES), slc_s.reshape(nb, 1, ROW_WIDTH),
                                  win_new.reshape(nb, 1, ROW_WIDTH), win_t, past)
    win_s = win_s.reshape((1, nb) + kv_tail + (wl_s,)).transpose(0, 1, 5, 2, 3, 4)
    yb_s, hg_s = _hgrn_sample(zs, state_hgrn, hg_lower, hg_norm)
    h_s = _finish(xs, oa_s.reshape(nb, NSA_WIDTH), zs, yb_s, p_sample[0].reshape(nb, PLE_DIM),
                  w_out_b, g_post, ple_proj_b, ple_gate_b, nb)

    out_sample = (h_s.reshape(nb, 1, D_MODEL),
                  cmp_s.reshape((1, nb, 1) + kv_tail),
                  slc_s.reshape((1, nb, 1) + kv_tail),
                  win_s,
                  hg_s)
    return (out_prompt[0], out_sample[0]) + out_prompt[1:] + out_sample[1:]
```

```python
import functools

import numpy as np
import jax
import jax.numpy as jnp
from jax import lax
from jax.experimental import pallas as pl
from jax.experimental.pallas import tpu as pltpu

D_MODEL = 1024
PAGE_SIZE = 128
NSA_HEADS = 8
NSA_KV_HEADS = 2
NSA_GROUP = NSA_HEADS // NSA_KV_HEADS
HEAD_DIM = 64
NSA_WIDTH = NSA_HEADS * HEAD_DIM
KV_WIDTH = NSA_KV_HEADS * HEAD_DIM
CMP_BLOCK = 32
CMP_STRIDE = 16
CMP_RATIO = CMP_BLOCK // CMP_STRIDE
CMP_HIDDEN = 128
SEL_BLOCK = 64
SEL_RATIO = SEL_BLOCK // CMP_STRIDE
N_SEL = 16
WINDOW = 512
Q_BLOCK = 128
HG_HEADS = 4
HG_DIM = 128
HG_WIDTH = HG_HEADS * HG_DIM
PLE_DIM = 256
EPS = 1e-6
NEG = -1e30
FORCE_BONUS = 1e6
REMOVED = -3e38

COL_QA, COL_ZA, COL_QB, COL_FB, COL_IB, COL_ZB = 0, 512, 1024, 1536, 2048, 2560
COL_CMP, COL_SLC, COL_WIN, COL_GATE = 3072, 3328, 3584, 3840
PROJ_WIDTH = 4096
_ORIG = dict(q_a=0, k_cmp=512, gate_a=1280, z_a=1304, q_b=1816, f_b=2328, i_b=2840, z_b=3352, end=3864)

LANES = 128
VMEM_LIMIT = 56 * 1024 * 1024

ROW_WIDTH = 2 * KV_WIDTH
N_COMBO = 2 * NSA_KV_HEADS
SEL_CHUNK = 512
WIN_KEYS = WINDOW + Q_BLOCK
N_CMP_PAD = 512
N_SELBLK = 128

f32 = jnp.float32
bf16 = jnp.bfloat16


def _cparams(sem):
    return pltpu.CompilerParams(dimension_semantics=sem, vmem_limit_bytes=VMEM_LIMIT)


def _dot(a, b):
    return jnp.dot(a, b, preferred_element_type=f32)


def _dot_nt(a, b):
    return lax.dot_general(a, b, (((1,), (1,)), ((), ())), preferred_element_type=f32)


def _dot_tn(a, b):
    return lax.dot_general(a, b, (((0,), (0,)), ((), ())), preferred_element_type=f32)


def _sigmoid(x):
    return 1.0 / (1.0 + jnp.exp(-x))


def _silu(x):
    return x * _sigmoid(x)


PROJ_TILE = 1024
MAIN_TILES = COL_CMP // PROJ_TILE


def _proj_kernel(x_ref, g_ref, w_ref, z_ref, cmp_ref, slc_ref, win_ref, gate_ref, xn_ref):
    j = pl.program_id(1)

    @pl.when(j == 0)
    def _():
        x = x_ref[...]
        y = x * lax.rsqrt(jnp.mean(x * x, axis=-1, keepdims=True) + EPS)
        xn_ref[...] = (y * g_ref[...]).astype(bf16)

    res = _dot(xn_ref[...], w_ref[...])

    @pl.when(j < MAIN_TILES)
    def _():
        z_ref[...] = res

    @pl.when(j == MAIN_TILES)
    def _():
        base = COL_CMP
        cmp_ref[...] = res[:, COL_CMP - base:COL_SLC - base]
        slc_ref[...] = res[:, COL_SLC - base:COL_WIN - base]
        win_ref[...] = res[:, COL_WIN - base:COL_GATE - base]
        gate_ref[...] = res[:, COL_GATE - base:COL_GATE - base + LANES]


def _proj(x, g, w, tm):
    n = x.shape[0]
    row = lambda i, j: (i, 0)
    return pl.pallas_call(
        _proj_kernel,
        grid=(n // tm, PROJ_WIDTH // PROJ_TILE),
        in_specs=[pl.BlockSpec((tm, D_MODEL), row),
                  pl.BlockSpec((1, D_MODEL), lambda i, j: (0, 0)),
                  pl.BlockSpec((D_MODEL, PROJ_TILE), lambda i, j: (0, j))],
        out_specs=[pl.BlockSpec((tm, PROJ_TILE), lambda i, j: (i, jnp.minimum(j, MAIN_TILES - 1))),
                   pl.BlockSpec((tm, ROW_WIDTH), row), pl.BlockSpec((tm, ROW_WIDTH), row),
                   pl.BlockSpec((tm, ROW_WIDTH), row), pl.BlockSpec((tm, LANES), row)],
        out_shape=[jax.ShapeDtypeStruct((n, COL_CMP), f32),
                   jax.ShapeDtypeStruct((n, ROW_WIDTH), f32), jax.ShapeDtypeStruct((n, ROW_WIDTH), f32),
                   jax.ShapeDtypeStruct((n, ROW_WIDTH), f32), jax.ShapeDtypeStruct((n, LANES), f32)],
        scratch_shapes=[pltpu.VMEM((tm, D_MODEL), bf16)],
        compiler_params=_cparams(("parallel", "arbitrary")),
        name="proj",
    )(x, g, w)


def _finish_kernel(x_ref, oa_ref, za_ref, yb_ref, p_ref, wo_ref, gp_ref, pp_ref, pg_ref, h_ref):
    ya = (oa_ref[...] * _silu(za_ref[...])).astype(bf16)
    yb = yb_ref[...].astype(bf16)
    mix = _dot(ya, wo_ref[0:NSA_WIDTH, :]) + _dot(yb, wo_ref[NSA_WIDTH:, :])
    nrm = mix * lax.rsqrt(jnp.mean(mix * mix, axis=-1, keepdims=True) + EPS) * gp_ref[...]
    h = x_ref[...] + nrm
    gate = _sigmoid(_dot(h.astype(bf16), pg_ref[...]))
    h_ref[...] = h + _dot(p_ref[...].astype(bf16), pp_ref[...]) * gate


def _finish(x, oa, z, yb, p, w_out, g_post, ple_proj, ple_gate, tm):
    n = x.shape[0]
    za_blk = COL_ZA // NSA_WIDTH
    row = lambda i: (i, 0)
    const = lambda i: (0, 0)
    return pl.pallas_call(
        _finish_kernel,
        grid=(n // tm,),
        in_specs=[pl.BlockSpec((tm, D_MODEL), row),
                  pl.BlockSpec((tm, NSA_WIDTH), row),
                  pl.BlockSpec((tm, NSA_WIDTH), lambda i: (i, za_blk)),
                  pl.BlockSpec((tm, HG_WIDTH), row),
                  pl.BlockSpec((tm, PLE_DIM), row),
                  pl.BlockSpec((D_MODEL, D_MODEL), const),
                  pl.BlockSpec((1, D_MODEL), const),
                  pl.BlockSpec((PLE_DIM, D_MODEL), const),
                  pl.BlockSpec((D_MODEL, D_MODEL), const)],
        out_specs=pl.BlockSpec((tm, D_MODEL), row),
        out_shape=jax.ShapeDtypeStruct((n, D_MODEL), f32),
        compiler_params=_cparams(("parallel",)),
        name="finish",
    )(x, oa, z, yb, p, w_out, g_post, ple_proj, ple_gate)


SEGS_PER_PAGE = PAGE_SIZE // CMP_STRIDE


def _compress_kernel(pt_ref, pages_ref, wa_ref, wb_ref, pea_ref, peb_ref, b1_ref, w2_ref, out_ref,
                     stage_ref, rows_ref, sem_ref, *, n_pages, feature_major):
    b = pl.program_id(0)
    nb = pl.num_programs(0)
    slot = lax.rem(b, 2)

    def page_copy(seq, s, j):
        return pltpu.make_async_copy(pages_ref.at[pt_ref[seq, j]], stage_ref.at[s, j], sem_ref.at[s])

    def start_all(seq, s):
        def body(j, c):
            page_copy(seq, s, j).start()
            return c
        lax.fori_loop(0, n_pages, body, 0)

    @pl.when(b == 0)
    def _():
        start_all(0, 0)

    @pl.when(b + 1 < nb)
    def _():
        start_all(b + 1, 1 - slot)

    def wait_body(j, c):
        page_copy(b, slot, j).wait()
        return c
    lax.fori_loop(0, n_pages, wait_body, 0)

    lane_halves = ROW_WIDTH // LANES

    def to_rows(j, c):
        tile = stage_ref[slot, j]
        rows = tile.T if feature_major else tile
        r0 = pl.multiple_of(j * PAGE_SIZE, PAGE_SIZE)
        for h in range(lane_halves):
            rows_ref[h, pl.ds(r0, PAGE_SIZE), :] = rows[:, h * LANES:(h + 1) * LANES]
        return c
    lax.fori_loop(0, n_pages, to_rows, 0, unroll=4)

    n_seg = n_pages * SEGS_PER_PAGE
    for kind in range(lane_halves):
        cols = slice(kind * LANES, (kind + 1) * LANES)
        seg_a, seg_b = [], []
        for j in range(CMP_STRIDE):
            r = rows_ref[kind, pl.ds(j, n_seg, stride=CMP_STRIDE), :]
            seg_a.append((r + pea_ref[j][:, cols]).astype(bf16))
            seg_b.append((r + peb_ref[j][:, cols]).astype(bf16))
        pa = _dot(jnp.concatenate(seg_a, axis=1), wa_ref[kind])
        pb = _dot(jnp.concatenate(seg_b, axis=1), wb_ref[kind])
        pb = pltpu.roll(pb, n_seg - 1, axis=0)
        hid = _silu(pa + pb + b1_ref[kind])
        for g in range(NSA_KV_HEADS):
            hg = hid[:, g * CMP_HIDDEN:(g + 1) * CMP_HIDDEN].astype(bf16)
            out_ref[0, kind * NSA_KV_HEADS + g] = _dot(hg, w2_ref[kind])


def _compress(pages, page_table, wa, wb, pea, peb, b1, w2, feature_major):
    nseq, n_pages = page_table.shape
    n_seg = n_pages * SEGS_PER_PAGE
    const3 = lambda b, pt: (0, 0, 0)
    once = pl.Buffered(1)
    grid_spec = pltpu.PrefetchScalarGridSpec(
        num_scalar_prefetch=1,
        grid=(nseq,),
        in_specs=[pl.BlockSpec(memory_space=pl.ANY),
                  pl.BlockSpec(wa.shape, const3, pipeline_mode=once),
                  pl.BlockSpec(wb.shape, const3, pipeline_mode=once),
                  pl.BlockSpec((CMP_STRIDE, 1, ROW_WIDTH), const3),
                  pl.BlockSpec((CMP_STRIDE, 1, ROW_WIDTH), const3),
                  pl.BlockSpec(b1.shape, const3),
                  pl.BlockSpec((2, CMP_HIDDEN, HEAD_DIM), const3)],
        out_specs=pl.BlockSpec((1, N_COMBO, n_seg, HEAD_DIM), lambda b, pt: (b, 0, 0, 0)),
        scratch_shapes=[pltpu.VMEM((2, n_pages) + pages.shape[1:], f32),
                        pltpu.VMEM((ROW_WIDTH // LANES, n_pages * PAGE_SIZE, LANES), f32),
                        pltpu.SemaphoreType.DMA((2,))],
    )
    return pl.pallas_call(
        functools.partial(_compress_kernel, n_pages=n_pages, feature_major=feature_major),
        grid_spec=grid_spec,
        out_shape=jax.ShapeDtypeStruct((nseq, N_COMBO, n_seg, HEAD_DIM), f32),
        compiler_params=_cparams(("arbitrary",)),
        name="compress",
    )(page_table, pages, wa, wb, pea, peb, b1, w2)


def _compress_weights(cmp_pe, cmp_w1, cmp_b1, cmp_w2):
    eye = jnp.eye(NSA_KV_HEADS, dtype=f32)

    def big(w_half):
        t = jnp.einsum('kjdh,gb->kjbdgh', w_half, eye)
        return t.reshape(2, CMP_STRIDE * KV_WIDTH, NSA_KV_HEADS * CMP_HIDDEN)

    w1 = cmp_w1.reshape(2, 2, CMP_STRIDE, HEAD_DIM, CMP_HIDDEN)
    wa = big(w1[:, 0]).astype(bf16)
    wb = big(w1[:, 1]).astype(bf16)
    pe = cmp_pe.reshape(2, 2, CMP_STRIDE, HEAD_DIM)

    def pe_row(p):
        t = jnp.broadcast_to(p.transpose(1, 0, 2)[:, :, None, :],
                             (CMP_STRIDE, 2, NSA_KV_HEADS, HEAD_DIM))
        return t.reshape(CMP_STRIDE, 1, ROW_WIDTH)

    pea, peb = pe_row(pe[:, 0]), pe_row(pe[:, 1])
    b1 = jnp.broadcast_to(cmp_b1[:, None, :], (2, NSA_KV_HEADS, CMP_HIDDEN)).reshape(2, 1, NSA_KV_HEADS * CMP_HIDDEN)
    return wa, wb, pea, peb, b1, cmp_w2.astype(bf16)


def _sel_map_matrix():
    wts = np.convolve(np.ones(SEL_RATIO), np.ones(CMP_RATIO))
    m = np.zeros((N_CMP_PAD, N_SELBLK), np.float32)
    for j in range(N_SELBLK):
        for k, wk in enumerate(wts):
            n = SEL_RATIO * j + k
            if n < N_CMP_PAD - 1:
                m[n, j] = wk
    return jnp.asarray(m, dtype=bf16)


def _importance_to_blocks(imp, m):
    hi = imp.astype(bf16)
    lo = (imp - hi.astype(f32)).astype(bf16)
    return _dot(hi, m) + _dot(lo, m)


LOG2E = 1.4426950408889634
MASK_BIG = 1e30
AUG_ROWS = 16
VT_ROWS = HEAD_DIM + 16
KEY_LOW = 256
N_FORCED = 3


def _nsa_prompt_pair_kernel(qt_ref, gate_ref, comp_ref, caug_ref, ks_ref, vts_ref, kw_ref, vtw_ref, mt_ref, o_ref,
                            qa_ref, acc_ref, mg_ref, bias_ref, need_ref):
    qi = pl.program_id(1)
    q0 = qi * Q_BLOCK
    cols = NSA_GROUP * Q_BLOCK
    heads = range(NSA_KV_HEADS)

    lane = lax.broadcasted_iota(jnp.int32, (1, cols), 1)
    head = jnp.right_shift(lane, Q_BLOCK.bit_length() - 1)
    tq = q0 + jnp.bitwise_and(lane, Q_BLOCK - 1)
    blk = lax.broadcasted_iota(jnp.int32, (N_SELBLK, Q_BLOCK), 0)
    jt = jnp.right_shift(q0 + lax.broadcasted_iota(jnp.int32, (N_SELBLK, Q_BLOCK), 1), SEL_BLOCK.bit_length() - 1)
    valid = blk <= jt
    forced = (blk == 0) | (blk == jt) | (blk == jt - 1)
    c_end = lax.broadcasted_iota(jnp.int32, (N_CMP_PAD, 1), 0) * CMP_STRIDE + (CMP_BLOCK - 1)
    krow = lax.broadcasted_iota(jnp.int32, (SEL_CHUNK, 1), 0)
    arow = lax.broadcasted_iota(jnp.int32, (AUG_ROWS // 2, cols), 0)
    blk_per_chunk = SEL_CHUNK // SEL_BLOCK
    n_chunk_max = vts_ref.shape[2]
    last = q0 // SEL_CHUNK
    w0 = pl.multiple_of(jnp.maximum(q0 - WINDOW, 0), Q_BLOCK)
    wkey = w0 + lax.broadcasted_iota(jnp.int32, (WIN_KEYS, 1), 0)
    in_window = (wkey <= tq) & (wkey >= tq - WINDOW)
    gate_t = _sigmoid(gate_ref[...]).T

    def softmax_cols(s, mask):
        s = jnp.where(mask, s, NEG)
        m = jnp.max(s, axis=0, keepdims=True)
        e = jnp.where(mask, jnp.exp2(s - m), 0.0)
        return e, 1.0 / jnp.maximum(jnp.sum(e, axis=0, keepdims=True), 1e-30)

    slope = [jnp.exp2(-(head.astype(f32) + 1.0 + NSA_GROUP * g)) * LOG2E for g in heads]
    qts = [qt_ref[0, g, 0] for g in heads]

    def chunk_scores(g, ck, slot=0, live=None):
        k0 = pl.multiple_of(ck * SEL_CHUNK, SEL_CHUNK)
        b8 = bias_ref[g, pl.ds(pl.multiple_of(ck * blk_per_chunk, blk_per_chunk), blk_per_chunk), :]
        if live is not None:
            b8 = jnp.where(live, b8, -MASK_BIG)
        qa_ref[g, slot, HEAD_DIM:HEAD_DIM + AUG_ROWS, :] = jnp.concatenate(
            [jnp.concatenate([b8] * NSA_GROUP, axis=1), alibi_rows[g]], axis=0).astype(bf16)
        return _dot(ks_ref[0, g, pl.ds(k0, SEL_CHUNK), :], qa_ref[g, slot])

    def sel_chunk(g, ck, causal, s):
        c = slope[g]
        k0 = pl.multiple_of(ck * SEL_CHUNK, SEL_CHUNK)
        if causal:
            s = jnp.where(k0 + krow <= tq, s, NEG)
        shift = c * k0.astype(f32)
        m_old = mg_ref[g] - shift
        m_new = jnp.maximum(m_old, jnp.max(s, axis=0, keepdims=True))
        p = jnp.exp2(s - m_new).astype(bf16)
        acc_ref[g] = acc_ref[g] * jnp.exp2(m_old - m_new) + _dot(vts_ref[0, g, ck], p)
        mg_ref[g] = m_new + shift

    alibi_rows, s_cmp, s_win = [], [], []
    for g in heads:
        c, qt = slope[g], qts[g]
        c_hi = c.astype(bf16).astype(f32)
        c_mid = (c - c_hi).astype(bf16).astype(f32)
        c_lo = (c - c_hi - c_mid).astype(bf16).astype(f32)
        c3 = jnp.where((arow == 0) | (arow == 3), c_hi, jnp.where((arow == 1) | (arow == 4), c_mid, c_lo))
        alibi_rows.append(jnp.where(arow >= 6, 0.0, c3))
        pos_rows = jnp.where(arow >= 6, 0.0, jnp.where(arow >= 3, KEY_LOW * c3, c3))
        pos_rows = jnp.concatenate([pos_rows, jnp.zeros_like(pos_rows)], axis=0).astype(bf16)
        q_pos = jnp.concatenate([qt, pos_rows, jnp.zeros((LANES - HEAD_DIM - AUG_ROWS, cols), bf16)], axis=0)
        kc = jnp.concatenate([comp_ref[0, g].astype(bf16), caug_ref[...]], axis=1)
        s_cmp.append(_dot(kc, q_pos))
        kw = kw_ref[0, g, pl.ds(w0, WIN_KEYS), :]
        s_win.append(_dot(kw, q_pos))

    oc_t, ps_t = [], []
    for g in heads:
        e, rl = softmax_cols(s_cmp[g], c_end <= tq)
        p_c = e * rl
        oc_t.append(_dot_tn(comp_ref[0, NSA_KV_HEADS + g].astype(bf16), p_c.astype(bf16)))
        imp_t = (p_c[:, 0:Q_BLOCK] + p_c[:, Q_BLOCK:2 * Q_BLOCK]
                 + p_c[:, 2 * Q_BLOCK:3 * Q_BLOCK] + p_c[:, 3 * Q_BLOCK:4 * Q_BLOCK])
        hi = imp_t.astype(bf16)
        lo = (imp_t - hi.astype(f32)).astype(bf16)
        ps_t.append(_dot(mt_ref[...], hi) + _dot(mt_ref[...], lo))

    ow_t = []
    t0 = w0 // Q_BLOCK
    for g in heads:
        s_w = jnp.where(in_window, s_win[g], NEG)
        e = jnp.exp2(s_w - jnp.max(s_w, axis=0, keepdims=True)).astype(bf16)
        acc_w = jnp.zeros((VT_ROWS, cols), f32)
        for t in range(WIN_KEYS // Q_BLOCK):
            acc_w = acc_w + _dot(vtw_ref[0, g, t0 + t], e[t * Q_BLOCK:(t + 1) * Q_BLOCK])
        ow_t.append(acc_w[0:HEAD_DIM] * (1.0 / jnp.maximum(acc_w[HEAD_DIM:HEAD_DIM + 1], 1e-30)))

    sel_t = [jnp.where(forced, 1.0, 0.0)] * NSA_KV_HEADS
    score = [jnp.where(valid & jnp.logical_not(forced), ps_t[g], NEG) for g in heads]
    for _ in range(N_SEL - N_FORCED):
        for g in heads:
            best = jnp.max(score[g], axis=0, keepdims=True)
            idx = jnp.min(jnp.where(score[g] == best, blk, N_SELBLK), axis=0, keepdims=True)
            hit = blk == idx
            sel_t[g] = jnp.where(hit, 1.0, sel_t[g])
            score[g] = jnp.where(hit, REMOVED, score[g])
    sel_t = [jnp.where(valid, s, 0.0) for s in sel_t]

    for g in heads:
        bias_ref[g] = (sel_t[g] - 1.0) * MASK_BIG
        for slot in range(qa_ref.shape[1]):
            qa_ref[g, slot, 0:HEAD_DIM, :] = qts[g]
            qa_ref[g, slot, HEAD_DIM + AUG_ROWS:, :] = jnp.zeros((LANES - HEAD_DIM - AUG_ROWS, cols), bf16)
        acc_ref[g] = jnp.zeros(acc_ref.shape[1:], f32)
        mg_ref[g] = jnp.full(mg_ref.shape[1:], NEG, f32)
    own = [chunk_scores(g, last) for g in heads]
    for g in heads:
        sel_chunk(g, last, True, own[g])

    n_need = jnp.int32(0)
    for ck in range(n_chunk_max):
        picked = jnp.maximum(*[jnp.max(sel_t[g][ck * blk_per_chunk:(ck + 1) * blk_per_chunk, :]) for g in heads])
        need_ref[n_need] = ck
        n_need = n_need + ((picked > 0.5) & (ck < last)).astype(jnp.int32)

    def chunk_pair(i, carry):
        first = 2 * i
        live = first + 1 < n_need
        cks = [need_ref[first], need_ref[jnp.minimum(first + 1, n_need - 1)]]
        scores = [[chunk_scores(g, cks[slot], slot, live if slot else None) for g in heads] for slot in range(2)]
        for slot in range(2):
            for g in heads:
                sel_chunk(g, cks[slot], False, scores[slot][g])
        return carry

    lax.fori_loop(0, (n_need + 1) // 2, chunk_pair, 0)

    for g in heads:
        acc = acc_ref[g]
        os_t = acc[0:HEAD_DIM] * (1.0 / jnp.maximum(acc[HEAD_DIM:HEAD_DIM + 1], 1e-30))
        outs = []
        for r in range(NSA_GROUP):
            cs = slice(r * Q_BLOCK, (r + 1) * Q_BLOCK)
            row = (g * NSA_GROUP + r) * 3
            outs.append(gate_t[row:row + 1, :] * oc_t[g][:, cs] + gate_t[row + 1:row + 2, :] * os_t[:, cs]
                        + gate_t[row + 2:row + 3, :] * ow_t[g][:, cs])
        for pair in range(NSA_GROUP // 2):
            both = jnp.concatenate([outs[2 * pair], outs[2 * pair + 1]], axis=0)
            lane0 = (g * NSA_GROUP // 2 + pair) * LANES
            o_ref[:, lane0:lane0 + LANES] = both.T


def _nsa_prompt_pair(q_t, gates, comp, k_slc, vt_slc, k_win, vt_win, m_t):
    bsz, _, nq = q_t.shape[:3]
    seq = k_slc.shape[2]
    cols = NSA_GROUP * Q_BLOCK
    per_b4 = lambda b, i: (b, 0, 0, 0)
    per_b5 = lambda b, i: (b, 0, 0, 0, 0)
    return pl.pallas_call(
        _nsa_prompt_pair_kernel,
        grid=(bsz, nq),
        in_specs=[pl.BlockSpec((1, NSA_KV_HEADS, 1, HEAD_DIM, cols), lambda b, i: (b, 0, i, 0, 0)),
                  pl.BlockSpec((Q_BLOCK, LANES), lambda b, i: (b * nq + i, 0)),
                  pl.BlockSpec((1, N_COMBO, N_CMP_PAD, HEAD_DIM), per_b4),
                  pl.BlockSpec((N_CMP_PAD, HEAD_DIM), lambda b, i: (0, 0)),
                  pl.BlockSpec((1, NSA_KV_HEADS, seq, LANES), per_b4),
                  pl.BlockSpec((1,) + vt_slc.shape[1:], per_b5),
                  pl.BlockSpec((1, NSA_KV_HEADS, seq, LANES), per_b4),
                  pl.BlockSpec((1,) + vt_win.shape[1:], per_b5),
                  pl.BlockSpec((N_SELBLK, N_CMP_PAD), lambda b, i: (0, 0))],
        out_specs=pl.BlockSpec((Q_BLOCK, NSA_WIDTH), lambda b, i: (b * nq + i, 0)),
        out_shape=jax.ShapeDtypeStruct((bsz * seq, NSA_WIDTH), f32),
        scratch_shapes=[pltpu.VMEM((NSA_KV_HEADS, 2, LANES, cols), bf16),
                        pltpu.VMEM((NSA_KV_HEADS, VT_ROWS, cols), f32),
                        pltpu.VMEM((NSA_KV_HEADS, 1, cols), f32),
                        pltpu.VMEM((NSA_KV_HEADS, N_SELBLK, Q_BLOCK), f32),
                        pltpu.SMEM((vt_slc.shape[2],), jnp.int32)],
        compiler_params=_cparams(("parallel", "arbitrary")),
        name="nsa_prompt",
    )(q_t, gates, comp, _cmp_position_columns(), k_slc, vt_slc, k_win, vt_win, m_t)


def _position_columns(pos):
    low, high = (pos % KEY_LOW)[:, None], (pos // KEY_LOW)[:, None]
    return np.concatenate([low, low, low, high, high, high], axis=1).astype(np.float32)


def _cmp_position_columns():
    aug = np.zeros((N_CMP_PAD, HEAD_DIM), np.float32)
    aug[:, 0:6] = _position_columns(np.arange(N_CMP_PAD) * CMP_STRIDE + CMP_BLOCK - 1)
    return jnp.asarray(aug, dtype=bf16)


LAYOUT_ROWS = 1024


def _attention_layout_kernel(q_ref, cmp_ref, slc_ref, win_ref, xs_ref, xw_ref,
                             qt_ref, ks_ref, vts_ref, kw_ref, vtw_ref, cmp_t_ref, slc_t_ref, win_t_ref):
    cmp_t_ref[0] = cmp_ref[...].T
    tm = q_ref.shape[0]
    q = q_ref[...] * (HEAD_DIM ** -0.5 * LOG2E)
    for qb in range(tm // Q_BLOCK):
        rows = slice(qb * Q_BLOCK, (qb + 1) * Q_BLOCK)
        for pair in range(NSA_HEADS // 2):
            t = q[rows, pair * LANES:(pair + 1) * LANES].T
            for k in range(2):
                g, r = divmod(2 * pair + k, NSA_GROUP)
                qt_ref[0, g, qb, :, r * Q_BLOCK:(r + 1) * Q_BLOCK] = t[k * HEAD_DIM:(k + 1) * HEAD_DIM].astype(bf16)

    for src_ref, extra_ref, k_ref, vt_ref, t_ref in ((slc_ref, xs_ref, ks_ref, vts_ref, slc_t_ref),
                                                     (win_ref, xw_ref, kw_ref, vtw_ref, win_t_ref)):
        tile = vt_ref.shape[4]
        s = src_ref[...]
        s_t = s.T
        t_ref[0] = s_t
        v_t = s_t[KV_WIDTH:]
        tail = jnp.where(lax.broadcasted_iota(jnp.int32, (VT_ROWS - HEAD_DIM, tile), 0) == 0, 1.0, 0.0)
        for g in range(NSA_KV_HEADS):
            k_ref[0, g] = jnp.concatenate([s[:, g * HEAD_DIM:(g + 1) * HEAD_DIM].astype(bf16), extra_ref[...]], axis=1)
            for c in range(tm // tile):
                blk = v_t[g * HEAD_DIM:(g + 1) * HEAD_DIM, c * tile:(c + 1) * tile]
                vt_ref[0, g, c] = jnp.concatenate([blk, tail], axis=0).astype(bf16)


def _attention_layout(z, cmp, slc, win, bsz, seq):
    tm = LAYOUT_ROWS
    per_b = seq // tm
    pos = np.arange(seq)
    xs = np.zeros((seq, HEAD_DIM), np.float32)
    xs[pos, (pos // SEL_BLOCK) % (SEL_CHUNK // SEL_BLOCK)] = 1.0
    xs[:, 8:11] = (pos % KEY_LOW)[:, None]
    xs[:, 11:14] = (pos % SEL_CHUNK - pos % KEY_LOW)[:, None]
    xw = np.zeros((seq, HEAD_DIM), np.float32)
    xw[:, 0:6] = _position_columns(pos)
    row = lambda i: (i, 0)
    rep = lambda i: (i % per_b, 0)
    out4 = lambda i: (i // per_b, 0, i % per_b, 0)
    out5 = lambda i: (i // per_b, 0, i % per_b, 0, 0)
    k_shape = jax.ShapeDtypeStruct((bsz, NSA_KV_HEADS, seq, LANES), bf16)
    vt_shape = lambda tile: jax.ShapeDtypeStruct((bsz, NSA_KV_HEADS, seq // tile, VT_ROWS, tile), bf16)
    vt_spec = lambda tile: pl.BlockSpec((1, NSA_KV_HEADS, tm // tile, VT_ROWS, tile), out5)
    rows_t_shape = jax.ShapeDtypeStruct((bsz, ROW_WIDTH, seq), f32)
    rows_t_spec = pl.BlockSpec((1, ROW_WIDTH, tm), lambda i: (i // per_b, 0, i % per_b))
    return pl.pallas_call(
        _attention_layout_kernel,
        grid=(bsz * per_b,),
        in_specs=[pl.BlockSpec((tm, NSA_WIDTH), row), pl.BlockSpec((tm, ROW_WIDTH), row),
                  pl.BlockSpec((tm, ROW_WIDTH), row), pl.BlockSpec((tm, ROW_WIDTH), row),
                  pl.BlockSpec((tm, HEAD_DIM), rep), pl.BlockSpec((tm, HEAD_DIM), rep)],
        out_specs=[pl.BlockSpec((1, NSA_KV_HEADS, tm // Q_BLOCK, HEAD_DIM, NSA_GROUP * Q_BLOCK), out5),
                   pl.BlockSpec((1, NSA_KV_HEADS, tm, LANES), out4), vt_spec(SEL_CHUNK),
                   pl.BlockSpec((1, NSA_KV_HEADS, tm, LANES), out4), vt_spec(Q_BLOCK),
                   rows_t_spec, rows_t_spec, rows_t_spec],
        out_shape=[jax.ShapeDtypeStruct((bsz, NSA_KV_HEADS, seq // Q_BLOCK, HEAD_DIM, NSA_GROUP * Q_BLOCK), bf16),
                   k_shape, vt_shape(SEL_CHUNK), k_shape, vt_shape(Q_BLOCK),
                   rows_t_shape, rows_t_shape, rows_t_shape],
        compiler_params=_cparams(("parallel",)),
        name="attention_layout",
    )(z, cmp, slc, win, jnp.asarray(xs, dtype=bf16), jnp.asarray(xw, dtype=bf16))


HG_CHUNK = 128
HG_BAND = 8


def _lower_bound(lo):
    m = jnp.max(lo, axis=0, keepdims=True)
    e = jnp.exp(lo - m)
    return e[0:1] / jnp.sum(e, axis=0, keepdims=True)


def _hgrn_gates(q, fl, lb):
    sg = _sigmoid(fl)
    f = lb + (1.0 - lb) * sg
    k = (1.0 - lb) * (1.0 - sg)
    return _silu(q), k, jnp.log(f)


def _hgrn_out(o, nw, zb):
    o = o * lax.rsqrt(jnp.mean(o * o, axis=-1, keepdims=True) + EPS) * nw
    return o * _silu(zb)


def _hgrn_prompt_kernel(q_ref, f_ref, i_ref, zb_ref, lo_ref, nw_ref, y_ref, s_ref, st_ref):
    c = pl.program_id(1)

    @pl.when(c == 0)
    def _():
        st_ref[...] = jnp.zeros_like(st_ref)

    n = HG_CHUNK
    row = lax.broadcasted_iota(jnp.int32, (n, n), 0)
    col = lax.broadcasted_iota(jnp.int32, (n, n), 1)
    heads = [slice(h * HG_DIM, (h + 1) * HG_DIM) for h in range(HG_HEADS)]

    qq, kk, logf = _hgrn_gates(q_ref[...], f_ref[...], _lower_bound(lo_ref[...]))
    b = logf
    trow = lax.broadcasted_iota(jnp.int32, (n, 1), 0)
    step = 1
    while step < n:
        b = b + jnp.where(trow >= step, pltpu.roll(b, step, axis=0), 0.0)
        step *= 2

    intra = [_hgrn_intra(qq[:, hs], kk[:, hs], b[:, hs], row, col) for hs in heads]
    v = i_ref[...]
    for h, hs in enumerate(heads):
        st = st_ref[h]
        bh, kh, vh = b[:, hs], kk[:, hs], v[:, hs].astype(bf16)
        o = _dot(intra[h].astype(bf16), vh) + _dot_nt((qq[:, hs] * jnp.exp(bh)).astype(bf16), st.astype(bf16))
        b_last = bh[n - 1:n]
        st_new = st * jnp.exp(b_last) + _dot_tn(vh, (kh * jnp.exp(b_last - bh)).astype(bf16))
        st_ref[h] = st_new
        y_ref[:, hs] = _hgrn_out(o, nw_ref[...], zb_ref[:, hs])

        @pl.when(c == pl.num_programs(1) - 1)
        def _():
            s_ref[0, h] = st_new.T


def _hgrn_intra(qq, kk, b, row, col):
    n = HG_CHUNK
    a = jnp.where(row == col, _dot_nt(qq.astype(bf16), kk.astype(bf16)), 0.0)

    sub = jnp.bitwise_and(row, HG_BAND - 1)
    b3 = b.reshape(n // HG_BAND, HG_BAND, HG_DIM)
    size = 2
    while size <= HG_BAND:
        half = size // 2
        beta = None
        for s0 in range(0, HG_BAND, size):
            piece = jnp.broadcast_to(b3[:, s0 + half - 1:s0 + half, :], b3.shape).reshape(n, HG_DIM)
            beta = piece if beta is None else jnp.where(sub >= s0, piece, beta)
        left = jnp.bitwise_and(sub, size - 1) < half
        decay = jnp.exp(jnp.where(left, beta - b, b - beta))
        part = _dot_nt(jnp.where(left, 0.0, qq * decay).astype(bf16), jnp.where(left, kk * decay, 0.0).astype(bf16))
        sh = size.bit_length() - 1
        a = a + jnp.where(jnp.right_shift(row, sh) == jnp.right_shift(col, sh), part, 0.0)
        size *= 2

    while size <= n:
        half = size // 2
        qs, ks = [], []
        zero = jnp.zeros((half, HG_DIM), f32)
        for r0 in range(0, n, size):
            beta = b[r0 + half - 1:r0 + half]
            ks += [kk[r0:r0 + half] * jnp.exp(beta - b[r0:r0 + half]), zero]
            qs += [zero, qq[r0 + half:r0 + size] * jnp.exp(b[r0 + half:r0 + size] - beta)]
        part = _dot_nt(jnp.concatenate(qs, axis=0).astype(bf16), jnp.concatenate(ks, axis=0).astype(bf16))
        if size < n:
            sh = size.bit_length() - 1
            part = jnp.where(jnp.right_shift(row, sh) == jnp.right_shift(col, sh), part, 0.0)
        a = a + part
        size *= 2
    return a


def _hgrn_prompt(z, hg_lower, hg_norm, bsz, seq):
    nch = seq // HG_CHUNK

    def zcol(base):
        return pl.BlockSpec((HG_CHUNK, HG_WIDTH), lambda b, c: (b * nch + c, base // HG_WIDTH))

    return pl.pallas_call(
        _hgrn_prompt_kernel,
        grid=(bsz, nch),
        in_specs=[zcol(COL_QB), zcol(COL_FB), zcol(COL_IB), zcol(COL_ZB),
                  pl.BlockSpec(hg_lower.shape, lambda b, c: (0, 0)),
                  pl.BlockSpec((1, HG_DIM), lambda b, c: (0, 0))],
        out_specs=[pl.BlockSpec((HG_CHUNK, HG_WIDTH), lambda b, c: (b * nch + c, 0)),
                   pl.BlockSpec((1, HG_HEADS, HG_DIM, HG_DIM), lambda b, c: (b, 0, 0, 0))],
        out_shape=[jax.ShapeDtypeStruct((bsz * seq, HG_WIDTH), f32),
                   jax.ShapeDtypeStruct((bsz, HG_HEADS, HG_DIM, HG_DIM), f32)],
        scratch_shapes=[pltpu.VMEM((HG_HEADS, HG_DIM, HG_DIM), f32)],
        compiler_params=_cparams(("parallel", "arbitrary")),
        name="hgrn_prompt",
    )(z, z, z, z, hg_lower, hg_norm)


def _hgrn_sample_kernel(q_ref, f_ref, i_ref, zb_ref, lo_ref, nw_ref, s_ref, y_ref, so_ref, o_scr):
    nb = q_ref.shape[0]
    lb = _lower_bound(lo_ref[...])
    qq, kk, logf = _hgrn_gates(q_ref[...], f_ref[...], lb)
    v = i_ref[...]
    pad = jnp.zeros((HG_DIM - nb, HG_DIM), f32)
    to_cols = lambda x: jnp.concatenate([x, pad], axis=0).T
    f_t, k_t, q_t = to_cols(jnp.exp(logf)), to_cols(kk), to_cols(qq)
    for b in range(nb):
        s_new = f_t[:, b:b + 1] * s_ref[0, b, 0] + k_t[:, b:b + 1] * v[b:b + 1, :]
        so_ref[0, b, 0] = s_new
        o_scr[b:b + 1, :] = jnp.sum(s_new * q_t[:, b:b + 1], axis=0, keepdims=True)
    y_ref[...] = _hgrn_out(o_scr[...], nw_ref[...], zb_ref[...])


def _hgrn_sample(z, state, hg_lower, hg_norm):
    nb = z.shape[0]

    def zcol(base):
        return pl.BlockSpec((nb, HG_DIM), lambda h: (0, base // HG_DIM + h))

    sblk = pl.BlockSpec((1, nb, 1, HG_DIM, HG_DIM), lambda h: (0, 0, h, 0, 0))
    return pl.pallas_call(
        _hgrn_sample_kernel,
        grid=(HG_HEADS,),
        in_specs=[zcol(COL_QB), zcol(COL_FB), zcol(COL_IB), zcol(COL_ZB),
                  pl.BlockSpec((hg_lower.shape[0], HG_DIM), lambda h: (0, h)),
                  pl.BlockSpec((1, HG_DIM), lambda h: (0, 0)),
                  sblk],
        out_specs=[pl.BlockSpec((nb, HG_DIM), lambda h: (0, h)), sblk],
        out_shape=[jax.ShapeDtypeStruct((nb, HG_WIDTH), f32),
                   jax.ShapeDtypeStruct(state.shape, f32)],
        scratch_shapes=[pltpu.VMEM((nb, HG_DIM), f32)],
        compiler_params=_cparams(("parallel",)),
        name="hgrn_sample",
    )(z, z, z, z, hg_lower, hg_norm, state)


N_PICK = N_SEL - 1


def _head_slopes():
    h = lax.broadcasted_iota(jnp.int32, (NSA_HEADS, 1), 0).astype(f32)
    return jnp.exp2(-(h + 1.0))


def _nsa_sample_cmp_kernel(q_ref, comp_ref, m_ref, oc_ref, ps_ref, *, t_pos):
    q = q_ref[0] * jnp.asarray(HEAD_DIM ** -0.5, bf16)
    slope = _head_slopes()
    head = lax.broadcasted_iota(jnp.int32, (NSA_HEADS, 1), 0)
    ncol = lax.broadcasted_iota(jnp.int32, (1, N_CMP_PAD), 1)
    d_c = t_pos - (ncol * CMP_STRIDE + (CMP_BLOCK - 1))
    mask = d_c >= 0
    o_c = jnp.zeros((NSA_HEADS, HEAD_DIM), f32)
    ps_ref[...] = jnp.zeros_like(ps_ref)
    for g in range(NSA_KV_HEADS):
        kc = comp_ref[0, g].astype(bf16)
        vc = comp_ref[0, NSA_KV_HEADS + g].astype(bf16)
        s = jnp.where(mask, _dot_nt(q, kc) - slope * d_c.astype(f32), NEG)
        m = jnp.max(s, axis=1, keepdims=True)
        e = jnp.where(mask, jnp.exp(s - m), 0.0)
        p = e * (1.0 / jnp.maximum(jnp.sum(e, axis=1, keepdims=True), 1e-30))
        mine = jnp.right_shift(head, NSA_GROUP.bit_length() - 1) == g
        o_c = jnp.where(mine, _dot(p.astype(bf16), vc), o_c)
        imp = jnp.sum(jnp.where(mine, p, 0.0), axis=0, keepdims=True)
        ps_ref[0, g:g + 1, :] = _importance_to_blocks(imp, m_ref[...])
    oc_ref[0] = o_c


def _nsa_sample_cmp(q8, comp, m_mat, t_pos):
    nb = q8.shape[0]
    return pl.pallas_call(
        functools.partial(_nsa_sample_cmp_kernel, t_pos=t_pos),
        grid=(nb,),
        in_specs=[pl.BlockSpec((1, NSA_HEADS, HEAD_DIM), lambda b: (b, 0, 0)),
                  pl.BlockSpec((1, N_COMBO, N_CMP_PAD, HEAD_DIM), lambda b: (b, 0, 0, 0)),
                  pl.BlockSpec((N_CMP_PAD, N_SELBLK), lambda b: (0, 0))],
        out_specs=[pl.BlockSpec((1, NSA_HEADS, HEAD_DIM), lambda b: (b, 0, 0)),
                   pl.BlockSpec((1, 8, N_SELBLK), lambda b: (b, 0, 0))],
        out_shape=[jax.ShapeDtypeStruct((nb, NSA_HEADS, HEAD_DIM), f32),
                   jax.ShapeDtypeStruct((nb, 8, N_SELBLK), f32)],
        compiler_params=_cparams(("parallel",)),
        name="nsa_sample_cmp",
    )(q8, comp, m_mat)


def _nsa_sample_topk_kernel(ps_ref, idx_ref, *, last_blk):
    ps = ps_ref[...]
    lane = lax.broadcasted_iota(jnp.int32, ps.shape, 1)
    score = ps + jnp.where((lane == 0) | (lane == last_blk), FORCE_BONUS, 0.0)
    out = jnp.zeros(ps.shape, jnp.int32)
    for r in range(N_PICK):
        best = jnp.max(score, axis=1, keepdims=True)
        idx = jnp.min(jnp.where(score == best, lane, N_SELBLK), axis=1, keepdims=True)
        out = jnp.where(lane == r, idx, out)
        score = jnp.where(lane == idx, REMOVED, score)
    idx_ref[...] = out


def _nsa_sample_topk(ps_rows, last_blk):
    return pl.pallas_call(
        functools.partial(_nsa_sample_topk_kernel, last_blk=last_blk),
        out_shape=jax.ShapeDtypeStruct(ps_rows.shape, jnp.int32),
        name="nsa_sample_topk",
    )(ps_rows)


def _nsa_sample_sel_kernel(pt_ref, idx_ref, slc_ref, q_ref, oc_ref, gate_ref, slc_new_ref, win_new_ref, wc_ref,
                           o_ref, wout_ref, kv_scr, sem_ref, *, t_pos):
    n_blk = NSA_KV_HEADS * N_PICK
    halves = PAGE_SIZE // SEL_BLOCK
    b = pl.program_id(0)
    nb = pl.num_programs(0)
    slot = lax.rem(b, 2)

    def picked(seq, k):
        return idx_ref[(seq * NSA_KV_HEADS + k // N_PICK) * N_SEL + k % N_PICK]

    def blk_copies(seq, s, k):
        page = pt_ref[seq, picked(seq, k) // halves]
        g = k // N_PICK
        return [pltpu.make_async_copy(slc_ref.at[page, kind * NSA_KV_HEADS + g], kv_scr.at[s, kind, k],
                                      sem_ref.at[s]) for kind in range(2)]

    def start_all(seq, s):
        for k in range(n_blk):
            for cp in blk_copies(seq, s, k):
                cp.start()

    @pl.when(b == 0)
    def _():
        start_all(0, 0)

    @pl.when(b + 1 < nb)
    def _():
        start_all(b + 1, 1 - slot)

    for k in range(n_blk):
        for cp in blk_copies(b, slot, k):
            cp.wait()

    q = q_ref[0] * jnp.asarray(HEAD_DIM ** -0.5, bf16)
    slope = _head_slopes()
    head = lax.broadcasted_iota(jnp.int32, (NSA_HEADS, 1), 0)
    lane = lax.broadcasted_iota(jnp.int32, (1, PAGE_SIZE), 1)
    new_slc = slc_new_ref[0]
    new_win = win_new_ref[0]
    wl = wc_ref.shape[3]
    wlane = lax.broadcasted_iota(jnp.int32, (1, wl), 1)
    d_w = (wl - wlane).astype(f32)
    qf = q.astype(f32)

    def attend(s, mask, v_t, new_row, g):
        k_new = new_row[:, g * HEAD_DIM:(g + 1) * HEAD_DIM].astype(bf16).astype(f32)
        v_new = new_row[:, KV_WIDTH + g * HEAD_DIM:KV_WIDTH + (g + 1) * HEAD_DIM].astype(bf16).astype(f32)
        s_new = jnp.sum(qf * k_new, axis=1, keepdims=True)
        m = jnp.maximum(jnp.max(s, axis=1, keepdims=True), s_new)
        e, e_new = jnp.exp(s - m), jnp.exp(s_new - m)
        if mask is not None:
            e = jnp.where(mask, e, 0.0)
        den = jnp.maximum(jnp.sum(e, axis=1, keepdims=True) + e_new, 1e-30)
        num = _dot_nt(e.astype(bf16), v_t.astype(bf16)) + e_new.astype(bf16).astype(f32) * v_new
        return num * (1.0 / den)

    o_s = jnp.zeros((NSA_HEADS, HEAD_DIM), f32)
    o_w = jnp.zeros((NSA_HEADS, HEAD_DIM), f32)
    for g in range(NSA_KV_HEADS):
        mine = jnp.right_shift(head, NSA_GROUP.bit_length() - 1) == g
        pos, member = [], []
        for n in range(N_PICK):
            j = picked(b, g * N_PICK + n)
            pos.append((j // halves) * PAGE_SIZE + lane)
            member.append(jnp.right_shift(lane, SEL_BLOCK.bit_length() - 1) == j % halves)
        pos = jnp.concatenate(pos, axis=1)
        member = jnp.concatenate(member, axis=1)
        k_t = jnp.concatenate([kv_scr[slot, 0, g * N_PICK + n] for n in range(N_PICK)], axis=1)
        v_t = jnp.concatenate([kv_scr[slot, 1, g * N_PICK + n] for n in range(N_PICK)], axis=1)
        s = jnp.where(member, _dot(q, k_t.astype(bf16)) - slope * (t_pos - pos).astype(f32), NEG)
        o_s = jnp.where(mine, attend(s, member, v_t, new_slc, g), o_s)
        s = _dot(q, wc_ref[0, g].astype(bf16)) - slope * d_w
        o_w = jnp.where(mine, attend(s, None, wc_ref[0, NSA_KV_HEADS + g], new_win, g), o_w)

    gate = jnp.broadcast_to(_sigmoid(gate_ref[0]), (NSA_HEADS, LANES))
    glane = lax.broadcasted_iota(jnp.int32, (NSA_HEADS, LANES), 1)
    gcol = [jnp.sum(jnp.where(glane == 3 * head + j, gate, 0.0), axis=1, keepdims=True) for j in range(3)]
    o_ref[0] = gcol[0] * oc_ref[0] + gcol[1] * o_s + gcol[2] * o_w
    eye = lax.broadcasted_iota(jnp.int32, (HEAD_DIM, HEAD_DIM), 0) == \
        lax.broadcasted_iota(jnp.int32, (HEAD_DIM, HEAD_DIM), 1)
    for c in range(N_COMBO):
        piece = jnp.broadcast_to(new_win[:, c * HEAD_DIM:(c + 1) * HEAD_DIM], (HEAD_DIM, HEAD_DIM))
        new_col = jnp.sum(jnp.where(eye, piece, 0.0), axis=1, keepdims=True)
        wout_ref[0, c] = jnp.where(wlane == wl - 1, new_col, pltpu.roll(wc_ref[0, c], wl - 1, axis=1))


def _nsa_sample_sel(page_table, idx_flat, slc_pages_t, q8, o_c, gates, slc_new, win_new, cache_win_t, t_pos):
    nb = q8.shape[0]
    wl = cache_win_t.shape[3]
    n_blk = NSA_KV_HEADS * N_PICK
    per_seq = lambda b, pt, idx: (b, 0, 0)
    win_blk = pl.BlockSpec((1, N_COMBO, HEAD_DIM, wl), lambda b, pt, idx: (b, 0, 0, 0))
    grid_spec = pltpu.PrefetchScalarGridSpec(
        num_scalar_prefetch=2,
        grid=(nb,),
        in_specs=[pl.BlockSpec(memory_space=pl.ANY),
                  pl.BlockSpec((1, NSA_HEADS, HEAD_DIM), per_seq),
                  pl.BlockSpec((1, NSA_HEADS, HEAD_DIM), per_seq),
                  pl.BlockSpec((1, 1, LANES), per_seq),
                  pl.BlockSpec((1, 1, ROW_WIDTH), per_seq),
                  pl.BlockSpec((1, 1, ROW_WIDTH), per_seq),
                  win_blk],
        out_specs=[pl.BlockSpec((1, NSA_HEADS, HEAD_DIM), per_seq), win_blk],
        scratch_shapes=[pltpu.VMEM((2, 2, n_blk, HEAD_DIM, PAGE_SIZE), f32),
                        pltpu.SemaphoreType.DMA((2,))],
    )
    return pl.pallas_call(
        functools.partial(_nsa_sample_sel_kernel, t_pos=t_pos),
        grid_spec=grid_spec,
        out_shape=[jax.ShapeDtypeStruct((nb, NSA_HEADS, HEAD_DIM), f32),
                   jax.ShapeDtypeStruct(cache_win_t.shape, f32)],
        compiler_params=_cparams(("arbitrary",)),
        name="nsa_sample_sel",
    )(page_table, idx_flat, slc_pages_t, q8, o_c, gates, slc_new, win_new, cache_win_t)


def kernel(x_prompt, x_sample, cache_cmp_kv, cache_slc_kv, cache_win_kv, state_hgrn, page_table, p_prompt, p_sample,
           w_in, g_pre, cmp_pe, cmp_w1, cmp_b1, cmp_w2, hg_lower, hg_norm, w_out, g_post, ple_proj, ple_gate):
    bsz, seq, _ = x_prompt.shape
    nb = x_sample.shape[0]
    n_pool = cache_cmp_kv.shape[1]
    n_pages = page_table.shape[1]
    past = n_pages * PAGE_SIZE
    kv_tail = (2, NSA_KV_HEADS, HEAD_DIM)

    o = _ORIG
    w = w_in[0]
    w_r = jnp.concatenate([w[:, o['q_a']:o['k_cmp']], w[:, o['z_a']:o['end']], w[:, o['k_cmp']:o['gate_a']],
                           w[:, o['gate_a']:o['z_a']],
                           jnp.zeros((D_MODEL, PROJ_WIDTH - o['end']), f32)], axis=1).astype(bf16)
    cw = _compress_weights(cmp_pe[0], cmp_w1[0], cmp_b1[0], cmp_w2[0])
    m_mat = _sel_map_matrix()
    w_out_b, ple_proj_b, ple_gate_b = w_out[0].astype(bf16), ple_proj[0].astype(bf16), ple_gate[0].astype(bf16)

    xp = x_prompt.reshape(bsz * seq, D_MODEL)
    zp, cmp_p, slc_p, win_p, gates_p = _proj(xp, g_pre, w_r, 1024)
    pages_p = cmp_p.reshape(bsz * seq // PAGE_SIZE, PAGE_SIZE, ROW_WIDTH)
    table_p = jnp.arange(bsz * seq // PAGE_SIZE, dtype=jnp.int32).reshape(bsz, seq // PAGE_SIZE)
    comp_p = _compress(pages_p, table_p, *cw, feature_major=False)
    q_t, k_slc, vt_slc, k_win, vt_win, cmp_tp, slc_tp, win_tp = _attention_layout(zp, cmp_p, slc_p, win_p, bsz, seq)
    o_a = _nsa_prompt_pair(q_t, gates_p, comp_p, k_slc, vt_slc, k_win, vt_win, m_mat.T)
    yb_p, hg_p = _hgrn_prompt(zp, hg_lower, hg_norm, bsz, seq)
    h_p = _finish(xp, o_a, zp, yb_p, p_prompt[0].reshape(bsz * seq, PLE_DIM),
                  w_out_b, g_post, ple_proj_b, ple_gate_b, 512)

    wl_p = min(WINDOW, seq)

    def cache_rows(rows_t):
        return rows_t.reshape((1, bsz) + kv_tail + rows_t.shape[2:]).transpose(0, 1, 5, 2, 3, 4)

    out_prompt = (h_p.reshape(bsz, seq, D_MODEL), cache_rows(cmp_tp), cache_rows(slc_tp),
                  cache_rows(win_tp[:, :, seq - wl_p:]), hg_p[None])

    xs = x_sample.reshape(nb, D_MODEL)
    zs, cmp_s, slc_s, win_new, gates_s = _proj(xs, g_pre, w_r, nb)
    cmp_t = cache_cmp_kv.transpose(0, 1, 3, 4, 5, 2).reshape(n_pool, ROW_WIDTH, PAGE_SIZE)
    comp_s = _compress(cmp_t, page_table, *cw, feature_major=True)
    q8 = zs[:, COL_QA:COL_QA + NSA_WIDTH].astype(bf16).reshape(nb, NSA_HEADS, HEAD_DIM)
    oc_s, ps_s = _nsa_sample_cmp(q8, comp_s, m_mat, past)
    ps_rows = ps_s[:, :NSA_KV_HEADS, :].reshape(nb * NSA_KV_HEADS, N_SELBLK)
    idx = _nsa_sample_topk(ps_rows, past // SEL_BLOCK - 1)
    slc_t = cache_slc_kv.transpose(0, 1, 3, 4, 5, 2).reshape(n_pool, N_COMBO, HEAD_DIM, PAGE_SIZE)
    wl_s = cache_win_kv.shape[2]
    win_t = cache_win_kv.transpose(0, 1, 3, 4, 5, 2).reshape(nb, N_COMBO, HEAD_DIM, wl_s)
    oa_s, win_s = _nsa_sample_sel(page_table, idx[:, :N_SEL].reshape(-1), slc_t, q8, oc_s,
                                  gates_s.reshape(nb, 1, LANES), slc_s.reshape(nb, 1, ROW_WIDTH),
                                  win_new.reshape(nb, 1, ROW_WIDTH), win_t, past)
    win_s = win_s.reshape((1, nb) + kv_tail + (wl_s,)).transpose(0, 1, 5, 2, 3, 4)
    yb_s, hg_s = _hgrn_sample(zs, state_hgrn, hg_lower, hg_norm)
    h_s = _finish(xs, oa_s.reshape(nb, NSA_WIDTH), zs, yb_s, p_sample[0].reshape(nb, PLE_DIM),
                  w_out_b, g_post, ple_proj_b, ple_gate_b, nb)

    out_sample = (h_s.reshape(nb, 1, D_MODEL),
                  cmp_s.reshape((1, nb, 1) + kv_tail),
                  slc_s.reshape((1, nb, 1) + kv_tail),
                  win_s,
                  hg_s)
    return (out_prompt[0], out_sample[0]) + out_prompt[1:] + out_sample[1:]
```

```python
import functools

import numpy as np
import jax
import jax.numpy as jnp
from jax import lax
from jax.experimental import pallas as pl
from jax.experimental.pallas import tpu as pltpu

D_MODEL = 1024
PAGE_SIZE = 128
NSA_HEADS = 8
NSA_KV_HEADS = 2
NSA_GROUP = NSA_HEADS // NSA_KV_HEADS
HEAD_DIM = 64
NSA_WIDTH = NSA_HEADS * HEAD_DIM
KV_WIDTH = NSA_KV_HEADS * HEAD_DIM
CMP_BLOCK = 32
CMP_STRIDE = 16
CMP_RATIO = CMP_BLOCK // CMP_STRIDE
CMP_HIDDEN = 128
SEL_BLOCK = 64
SEL_RATIO = SEL_BLOCK // CMP_STRIDE
N_SEL = 16
WINDOW = 512
Q_BLOCK = 128
HG_HEADS = 4
HG_DIM = 128
HG_WIDTH = HG_HEADS * HG_DIM
PLE_DIM = 256
EPS = 1e-6
NEG = -1e30
FORCE_BONUS = 1e6
REMOVED = -3e38

COL_QA, COL_ZA, COL_QB, COL_FB, COL_IB, COL_ZB = 0, 512, 1024, 1536, 2048, 2560
COL_CMP, COL_SLC, COL_WIN, COL_GATE = 3072, 3328, 3584, 3840
PROJ_WIDTH = 4096
_ORIG = dict(q_a=0, k_cmp=512, gate_a=1280, z_a=1304, q_b=1816, f_b=2328, i_b=2840, z_b=3352, end=3864)

LANES = 128
VMEM_LIMIT = 56 * 1024 * 1024

ROW_WIDTH = 2 * KV_WIDTH
N_COMBO = 2 * NSA_KV_HEADS
SEL_CHUNK = 512
WIN_KEYS = WINDOW + Q_BLOCK
N_CMP_PAD = 512
N_SELBLK = 128

f32 = jnp.float32
bf16 = jnp.bfloat16


def _cparams(sem):
    return pltpu.CompilerParams(dimension_semantics=sem, vmem_limit_bytes=VMEM_LIMIT)


def _dot(a, b):
    return jnp.dot(a, b, preferred_element_type=f32)


def _dot_nt(a, b):
    return lax.dot_general(a, b, (((1,), (1,)), ((), ())), preferred_element_type=f32)


def _dot_tn(a, b):
    return lax.dot_general(a, b, (((0,), (0,)), ((), ())), preferred_element_type=f32)


def _sigmoid(x):
    return 1.0 / (1.0 + jnp.exp(-x))


def _silu(x):
    return x * _sigmoid(x)


PROJ_TILE = 1024
MAIN_TILES = COL_CMP // PROJ_TILE


def _proj_kernel(x_ref, g_ref, w_ref, z_ref, cmp_ref, slc_ref, win_ref, gate_ref, xn_ref):
    j = pl.program_id(1)

    @pl.when(j == 0)
    def _():
        x = x_ref[...]
        y = x * lax.rsqrt(jnp.mean(x * x, axis=-1, keepdims=True) + EPS)
        xn_ref[...] = (y * g_ref[...]).astype(bf16)

    res = _dot(xn_ref[...], w_ref[...])

    @pl.when(j < MAIN_TILES)
    def _():
        z_ref[...] = res

    @pl.when(j == MAIN_TILES)
    def _():
        base = COL_CMP
        cmp_ref[...] = res[:, COL_CMP - base:COL_SLC - base]
        slc_ref[...] = res[:, COL_SLC - base:COL_WIN - base]
        win_ref[...] = res[:, COL_WIN - base:COL_GATE - base]
        gate_ref[...] = res[:, COL_GATE - base:COL_GATE - base + LANES]


def _proj(x, g, w, tm):
    n = x.shape[0]
    row = lambda i, j: (i, 0)
    return pl.pallas_call(
        _proj_kernel,
        grid=(n // tm, PROJ_WIDTH // PROJ_TILE),
        in_specs=[pl.BlockSpec((tm, D_MODEL), row),
                  pl.BlockSpec((1, D_MODEL), lambda i, j: (0, 0)),
                  pl.BlockSpec((D_MODEL, PROJ_TILE), lambda i, j: (0, j))],
        out_specs=[pl.BlockSpec((tm, PROJ_TILE), lambda i, j: (i, jnp.minimum(j, MAIN_TILES - 1))),
                   pl.BlockSpec((tm, ROW_WIDTH), row), pl.BlockSpec((tm, ROW_WIDTH), row),
                   pl.BlockSpec((tm, ROW_WIDTH), row), pl.BlockSpec((tm, LANES), row)],
        out_shape=[jax.ShapeDtypeStruct((n, COL_CMP), f32),
                   jax.ShapeDtypeStruct((n, ROW_WIDTH), f32), jax.ShapeDtypeStruct((n, ROW_WIDTH), f32),
                   jax.ShapeDtypeStruct((n, ROW_WIDTH), f32), jax.ShapeDtypeStruct((n, LANES), f32)],
        scratch_shapes=[pltpu.VMEM((tm, D_MODEL), bf16)],
        compiler_params=_cparams(("parallel", "arbitrary")),
        name="proj",
    )(x, g, w)


def _finish_kernel(x_ref, oa_ref, za_ref, yb_ref, p_ref, wo_ref, gp_ref, pp_ref, pg_ref, h_ref):
    ya = (oa_ref[...] * _silu(za_ref[...])).astype(bf16)
    yb = yb_ref[...].astype(bf16)
    mix = _dot(ya, wo_ref[0:NSA_WIDTH, :]) + _dot(yb, wo_ref[NSA_WIDTH:, :])
    nrm = mix * lax.rsqrt(jnp.mean(mix * mix, axis=-1, keepdims=True) + EPS) * gp_ref[...]
    h = x_ref[...] + nrm
    gate = _sigmoid(_dot(h.astype(bf16), pg_ref[...]))
    h_ref[...] = h + _dot(p_ref[...].astype(bf16), pp_ref[...]) * gate


def _finish(x, oa, z, yb, p, w_out, g_post, ple_proj, ple_gate, tm):
    n = x.shape[0]
    za_blk = COL_ZA // NSA_WIDTH
    row = lambda i: (i, 0)
    const = lambda i: (0, 0)
    return pl.pallas_call(
        _finish_kernel,
        grid=(n // tm,),
        in_specs=[pl.BlockSpec((tm, D_MODEL), row),
                  pl.BlockSpec((tm, NSA_WIDTH), row),
                  pl.BlockSpec((tm, NSA_WIDTH), lambda i: (i, za_blk)),
                  pl.BlockSpec((tm, HG_WIDTH), row),
                  pl.BlockSpec((tm, PLE_DIM), row),
                  pl.BlockSpec((D_MODEL, D_MODEL), const),
                  pl.BlockSpec((1, D_MODEL), const),
                  pl.BlockSpec((PLE_DIM, D_MODEL), const),
                  pl.BlockSpec((D_MODEL, D_MODEL), const)],
        out_specs=pl.BlockSpec((tm, D_MODEL), row),
        out_shape=jax.ShapeDtypeStruct((n, D_MODEL), f32),
        compiler_params=_cparams(("parallel",)),
        name="finish",
    )(x, oa, z, yb, p, w_out, g_post, ple_proj, ple_gate)


SEGS_PER_PAGE = PAGE_SIZE // CMP_STRIDE


def _compress_kernel(pt_ref, pages_ref, wa_ref, wb_ref, pea_ref, peb_ref, b1_ref, w2_ref, out_ref,
                     stage_ref, rows_ref, sem_ref, *, n_pages, feature_major):
    b = pl.program_id(0)
    nb = pl.num_programs(0)
    slot = lax.rem(b, 2)

    def page_copy(seq, s, j):
        return pltpu.make_async_copy(pages_ref.at[pt_ref[seq, j]], stage_ref.at[s, j], sem_ref.at[s])

    def start_all(seq, s):
        def body(j, c):
            page_copy(seq, s, j).start()
            return c
        lax.fori_loop(0, n_pages, body, 0)

    @pl.when(b == 0)
    def _():
        start_all(0, 0)

    @pl.when(b + 1 < nb)
    def _():
        start_all(b + 1, 1 - slot)

    def wait_body(j, c):
        page_copy(b, slot, j).wait()
        return c
    lax.fori_loop(0, n_pages, wait_body, 0)

    lane_halves = ROW_WIDTH // LANES

    def to_rows(j, c):
        tile = stage_ref[slot, j]
        rows = tile.T if feature_major else tile
        r0 = pl.multiple_of(j * PAGE_SIZE, PAGE_SIZE)
        for h in range(lane_halves):
            rows_ref[h, pl.ds(r0, PAGE_SIZE), :] = rows[:, h * LANES:(h + 1) * LANES]
        return c
    lax.fori_loop(0, n_pages, to_rows, 0, unroll=4)

    n_seg = n_pages * SEGS_PER_PAGE
    for kind in range(lane_halves):
        cols = slice(kind * LANES, (kind + 1) * LANES)
        seg_a, seg_b = [], []
        for j in range(CMP_STRIDE):
            r = rows_ref[kind, pl.ds(j, n_seg, stride=CMP_STRIDE), :]
            seg_a.append((r + pea_ref[j][:, cols]).astype(bf16))
            seg_b.append((r + peb_ref[j][:, cols]).astype(bf16))
        pa = _dot(jnp.concatenate(seg_a, axis=1), wa_ref[kind])
        pb = _dot(jnp.concatenate(seg_b, axis=1), wb_ref[kind])
        pb = pltpu.roll(pb, n_seg - 1, axis=0)
        hid = _silu(pa + pb + b1_ref[kind])
        for g in range(NSA_KV_HEADS):
            hg = hid[:, g * CMP_HIDDEN:(g + 1) * CMP_HIDDEN].astype(bf16)
            out_ref[0, kind * NSA_KV_HEADS + g] = _dot(hg, w2_ref[kind])


def _compress(pages, page_table, wa, wb, pea, peb, b1, w2, feature_major):
    nseq, n_pages = page_table.shape
    n_seg = n_pages * SEGS_PER_PAGE
    const3 = lambda b, pt: (0, 0, 0)
    once = pl.Buffered(1)
    grid_spec = pltpu.PrefetchScalarGridSpec(
        num_scalar_prefetch=1,
        grid=(nseq,),
        in_specs=[pl.BlockSpec(memory_space=pl.ANY),
                  pl.BlockSpec(wa.shape, const3, pipeline_mode=once),
                  pl.BlockSpec(wb.shape, const3, pipeline_mode=once),
                  pl.BlockSpec((CMP_STRIDE, 1, ROW_WIDTH), const3),
                  pl.BlockSpec((CMP_STRIDE, 1, ROW_WIDTH), const3),
                  pl.BlockSpec(b1.shape, const3),
                  pl.BlockSpec((2, CMP_HIDDEN, HEAD_DIM), const3)],
        out_specs=pl.BlockSpec((1, N_COMBO, n_seg, HEAD_DIM), lambda b, pt: (b, 0, 0, 0)),
        scratch_shapes=[pltpu.VMEM((2, n_pages) + pages.shape[1:], f32),
                        pltpu.VMEM((ROW_WIDTH // LANES, n_pages * PAGE_SIZE, LANES), f32),
                        pltpu.SemaphoreType.DMA((2,))],
    )
    return pl.pallas_call(
        functools.partial(_compress_kernel, n_pages=n_pages, feature_major=feature_major),
        grid_spec=grid_spec,
        out_shape=jax.ShapeDtypeStruct((nseq, N_COMBO, n_seg, HEAD_DIM), f32),
        compiler_params=_cparams(("arbitrary",)),
        name="compress",
    )(page_table, pages, wa, wb, pea, peb, b1, w2)


def _compress_weights(cmp_pe, cmp_w1, cmp_b1, cmp_w2):
    eye = jnp.eye(NSA_KV_HEADS, dtype=f32)

    def big(w_half):
        t = jnp.einsum('kjdh,gb->kjbdgh', w_half, eye)
        return t.reshape(2, CMP_STRIDE * KV_WIDTH, NSA_KV_HEADS * CMP_HIDDEN)

    w1 = cmp_w1.reshape(2, 2, CMP_STRIDE, HEAD_DIM, CMP_HIDDEN)
    wa = big(w1[:, 0]).astype(bf16)
    wb = big(w1[:, 1]).astype(bf16)
    pe = cmp_pe.reshape(2, 2, CMP_STRIDE, HEAD_DIM)

    def pe_row(p):
        t = jnp.broadcast_to(p.transpose(1, 0, 2)[:, :, None, :],
                             (CMP_STRIDE, 2, NSA_KV_HEADS, HEAD_DIM))
        return t.reshape(CMP_STRIDE, 1, ROW_WIDTH)

    pea, peb = pe_row(pe[:, 0]), pe_row(pe[:, 1])
    b1 = jnp.broadcast_to(cmp_b1[:, None, :], (2, NSA_KV_HEADS, CMP_HIDDEN)).reshape(2, 1, NSA_KV_HEADS * CMP_HIDDEN)
    return wa, wb, pea, peb, b1, cmp_w2.astype(bf16)


def _sel_map_matrix():
    wts = np.convolve(np.ones(SEL_RATIO), np.ones(CMP_RATIO))
    m = np.zeros((N_CMP_PAD, N_SELBLK), np.float32)
    for j in range(N_SELBLK):
        for k, wk in enumerate(wts):
            n = SEL_RATIO * j + k
            if n < N_CMP_PAD - 1:
                m[n, j] = wk
    return jnp.asarray(m, dtype=bf16)


def _importance_to_blocks(imp, m):
    hi = imp.astype(bf16)
    lo = (imp - hi.astype(f32)).astype(bf16)
    return _dot(hi, m) + _dot(lo, m)


LOG2E = 1.4426950408889634
MASK_BIG = 1e30
AUG_ROWS = 16
VT_ROWS = HEAD_DIM + 16
KEY_LOW = 256
N_FORCED = 3


def _nsa_prompt_pair_kernel(qt_ref, gate_ref, comp_ref, caug_ref, ks_ref, vts_ref, kw_ref, vtw_ref, mt_ref, o_ref,
                            qa_ref, acc_ref, mg_ref, bias_ref, need_ref):
    qi = pl.program_id(1)
    q0 = qi * Q_BLOCK
    cols = NSA_GROUP * Q_BLOCK
    heads = range(NSA_KV_HEADS)

    lane = lax.broadcasted_iota(jnp.int32, (1, cols), 1)
    head = jnp.right_shift(lane, Q_BLOCK.bit_length() - 1)
    tq = q0 + jnp.bitwise_and(lane, Q_BLOCK - 1)
    blk = lax.broadcasted_iota(jnp.int32, (N_SELBLK, Q_BLOCK), 0)
    jt = jnp.right_shift(q0 + lax.broadcasted_iota(jnp.int32, (N_SELBLK, Q_BLOCK), 1), SEL_BLOCK.bit_length() - 1)
    valid = blk <= jt
    forced = (blk == 0) | (blk == jt) | (blk == jt - 1)
    c_end = lax.broadcasted_iota(jnp.int32, (N_CMP_PAD, 1), 0) * CMP_STRIDE + (CMP_BLOCK - 1)
    krow = lax.broadcasted_iota(jnp.int32, (SEL_CHUNK, 1), 0)
    arow = lax.broadcasted_iota(jnp.int32, (AUG_ROWS // 2, cols), 0)
    blk_per_chunk = SEL_CHUNK // SEL_BLOCK
    n_chunk_max = vts_ref.shape[2]
    last = q0 // SEL_CHUNK
    w0 = pl.multiple_of(jnp.maximum(q0 - WINDOW, 0), Q_BLOCK)
    wkey = w0 + lax.broadcasted_iota(jnp.int32, (WIN_KEYS, 1), 0)
    in_window = (wkey <= tq) & (wkey >= tq - WINDOW)
    gate_t = _sigmoid(gate_ref[...]).T


    slope = [jnp.exp2(-(head.astype(f32) + 1.0 + NSA_GROUP * g)) * LOG2E for g in heads]
    qts = [qt_ref[0, g, 0] for g in heads]

    def chunk_scores(g, ck, slot=0, live=None):
        k0 = pl.multiple_of(ck * SEL_CHUNK, SEL_CHUNK)
        b8 = bias_ref[g, pl.ds(pl.multiple_of(ck * blk_per_chunk, blk_per_chunk), blk_per_chunk), :]
        if live is not None:
            b8 = jnp.where(live, b8, -MASK_BIG)
        qa_ref[g, slot, HEAD_DIM:HEAD_DIM + AUG_ROWS, :] = jnp.concatenate(
            [jnp.concatenate([b8] * NSA_GROUP, axis=1), alibi_rows[g]], axis=0).astype(bf16)
        return _dot(ks_ref[0, g, pl.ds(k0, SEL_CHUNK), :], qa_ref[g, slot])

    def sel_chunk(g, ck, causal, s):
        c = slope[g]
        k0 = pl.multiple_of(ck * SEL_CHUNK, SEL_CHUNK)
        if causal:
            s = jnp.where(k0 + krow <= tq, s, NEG)
        shift = c * k0.astype(f32)
        m_old = mg_ref[g] - shift
        m_new = jnp.maximum(m_old, jnp.max(s, axis=0, keepdims=True))
        p = jnp.exp2(s - m_new).astype(bf16)
        acc_ref[g] = acc_ref[g] * jnp.exp2(m_old - m_new) + _dot(vts_ref[0, g, ck], p)
        mg_ref[g] = m_new + shift

    alibi_rows, s_cmp, s_win = [], [], []
    for g in heads:
        c, qt = slope[g], qts[g]
        c_hi = c.astype(bf16).astype(f32)
        c_mid = (c - c_hi).astype(bf16).astype(f32)
        c_lo = (c - c_hi - c_mid).astype(bf16).astype(f32)
        c3 = jnp.where((arow == 0) | (arow == 3), c_hi, jnp.where((arow == 1) | (arow == 4), c_mid, c_lo))
        alibi_rows.append(jnp.where(arow >= 6, 0.0, c3))
        pos_rows = jnp.where(arow >= 6, 0.0, jnp.where(arow >= 3, KEY_LOW * c3, c3))
        pos_rows = jnp.concatenate([pos_rows, jnp.zeros_like(pos_rows)], axis=0).astype(bf16)
        q_pos = jnp.concatenate([qt, pos_rows, jnp.zeros((LANES - HEAD_DIM - AUG_ROWS, cols), bf16)], axis=0)
        kc = jnp.concatenate([comp_ref[0, g].astype(bf16), caug_ref[...]], axis=1)
        s_cmp.append(_dot(kc, q_pos))
        kw = kw_ref[0, g, pl.ds(w0, WIN_KEYS), :]
        s_win.append(_dot(kw, q_pos))

    oc_t, ps_t = [], []
    seen = c_end <= tq
    any_seen = tq >= CMP_BLOCK - 1
    for g in heads:
        s_c = jnp.where(seen, s_cmp[g], NEG)
        e = jnp.exp2(s_c - jnp.max(s_c, axis=0, keepdims=True))
        rl = jnp.where(any_seen, 1.0 / jnp.maximum(jnp.sum(e, axis=0, keepdims=True), 1e-30), 0.0)
        p_c = e * rl
        oc_t.append(_dot_tn(comp_ref[0, NSA_KV_HEADS + g].astype(bf16), p_c.astype(bf16)))
        imp_t = (p_c[:, 0:Q_BLOCK] + p_c[:, Q_BLOCK:2 * Q_BLOCK]
                 + p_c[:, 2 * Q_BLOCK:3 * Q_BLOCK] + p_c[:, 3 * Q_BLOCK:4 * Q_BLOCK])
        hi = imp_t.astype(bf16)
        lo = (imp_t - hi.astype(f32)).astype(bf16)
        ps_t.append(_dot(mt_ref[...], hi) + _dot(mt_ref[...], lo))

    ow_t = []
    t0 = w0 // Q_BLOCK
    for g in heads:
        s_w = jnp.where(in_window, s_win[g], NEG)
        e = jnp.exp2(s_w - jnp.max(s_w, axis=0, keepdims=True)).astype(bf16)
        acc_w = jnp.zeros((VT_ROWS, cols), f32)
        for t in range(WIN_KEYS // Q_BLOCK):
            acc_w = acc_w + _dot(vtw_ref[0, g, t0 + t], e[t * Q_BLOCK:(t + 1) * Q_BLOCK])
        ow_t.append(acc_w[0:HEAD_DIM] * (1.0 / jnp.maximum(acc_w[HEAD_DIM:HEAD_DIM + 1], 1e-30)))

    sel_t = [jnp.where(forced, 1.0, 0.0)] * NSA_KV_HEADS
    score = [jnp.where(valid & jnp.logical_not(forced), ps_t[g], NEG) for g in heads]
    for _ in range(N_SEL - N_FORCED):
        for g in heads:
            best = jnp.max(score[g], axis=0, keepdims=True)
            idx = jnp.min(jnp.where(score[g] == best, blk, N_SELBLK), axis=0, keepdims=True)
            hit = blk == idx
            sel_t[g] = jnp.where(hit, 1.0, sel_t[g])
            score[g] = jnp.where(hit, REMOVED, score[g])
    sel_t = [jnp.where(valid, s, 0.0) for s in sel_t]

    for g in heads:
        bias_ref[g] = (sel_t[g] - 1.0) * MASK_BIG
        for slot in range(qa_ref.shape[1]):
            qa_ref[g, slot, 0:HEAD_DIM, :] = qts[g]
            qa_ref[g, slot, HEAD_DIM + AUG_ROWS:, :] = jnp.zeros((LANES - HEAD_DIM - AUG_ROWS, cols), bf16)
        acc_ref[g] = jnp.zeros(acc_ref.shape[1:], f32)
        mg_ref[g] = jnp.full(mg_ref.shape[1:], NEG, f32)
    own = [chunk_scores(g, last) for g in heads]
    for g in heads:
        sel_chunk(g, last, True, own[g])

    n_need = jnp.int32(0)
    for ck in range(n_chunk_max):
        picked = jnp.maximum(*[jnp.max(sel_t[g][ck * blk_per_chunk:(ck + 1) * blk_per_chunk, :]) for g in heads])
        need_ref[n_need] = ck
        n_need = n_need + ((picked > 0.5) & (ck < last)).astype(jnp.int32)

    def chunk_pair(i, carry):
        first = 2 * i
        live = first + 1 < n_need
        cks = [need_ref[first], need_ref[jnp.minimum(first + 1, n_need - 1)]]
        scores = [[chunk_scores(g, cks[slot], slot, live if slot else None) for g in heads] for slot in range(2)]
        for slot in range(2):
            for g in heads:
                sel_chunk(g, cks[slot], False, scores[slot][g])
        return carry

    lax.fori_loop(0, (n_need + 1) // 2, chunk_pair, 0)

    for g in heads:
        acc = acc_ref[g]
        os_t = acc[0:HEAD_DIM] * (1.0 / jnp.maximum(acc[HEAD_DIM:HEAD_DIM + 1], 1e-30))
        outs = []
        for r in range(NSA_GROUP):
            cs = slice(r * Q_BLOCK, (r + 1) * Q_BLOCK)
            row = (g * NSA_GROUP + r) * 3
            outs.append(gate_t[row:row + 1, :] * oc_t[g][:, cs] + gate_t[row + 1:row + 2, :] * os_t[:, cs]
                        + gate_t[row + 2:row + 3, :] * ow_t[g][:, cs])
        for pair in range(NSA_GROUP // 2):
            both = jnp.concatenate([outs[2 * pair], outs[2 * pair + 1]], axis=0)
            lane0 = (g * NSA_GROUP // 2 + pair) * LANES
            o_ref[:, lane0:lane0 + LANES] = both.T


def _nsa_prompt_pair(q_t, gates, comp, k_slc, vt_slc, k_win, vt_win, m_t):
    bsz, _, nq = q_t.shape[:3]
    seq = k_slc.shape[2]
    cols = NSA_GROUP * Q_BLOCK
    per_b4 = lambda b, i: (b, 0, 0, 0)
    per_b5 = lambda b, i: (b, 0, 0, 0, 0)
    return pl.pallas_call(
        _nsa_prompt_pair_kernel,
        grid=(bsz, nq),
        in_specs=[pl.BlockSpec((1, NSA_KV_HEADS, 1, HEAD_DIM, cols), lambda b, i: (b, 0, i, 0, 0)),
                  pl.BlockSpec((Q_BLOCK, LANES), lambda b, i: (b * nq + i, 0)),
                  pl.BlockSpec((1, N_COMBO, N_CMP_PAD, HEAD_DIM), per_b4),
                  pl.BlockSpec((N_CMP_PAD, HEAD_DIM), lambda b, i: (0, 0)),
                  pl.BlockSpec((1, NSA_KV_HEADS, seq, LANES), per_b4),
                  pl.BlockSpec((1,) + vt_slc.shape[1:], per_b5),
                  pl.BlockSpec((1, NSA_KV_HEADS, seq, LANES), per_b4),
                  pl.BlockSpec((1,) + vt_win.shape[1:], per_b5),
                  pl.BlockSpec((N_SELBLK, N_CMP_PAD), lambda b, i: (0, 0))],
        out_specs=pl.BlockSpec((Q_BLOCK, NSA_WIDTH), lambda b, i: (b * nq + i, 0)),
        out_shape=jax.ShapeDtypeStruct((bsz * seq, NSA_WIDTH), f32),
        scratch_shapes=[pltpu.VMEM((NSA_KV_HEADS, 2, LANES, cols), bf16),
                        pltpu.VMEM((NSA_KV_HEADS, VT_ROWS, cols), f32),
                        pltpu.VMEM((NSA_KV_HEADS, 1, cols), f32),
                        pltpu.VMEM((NSA_KV_HEADS, N_SELBLK, Q_BLOCK), f32),
                        pltpu.SMEM((vt_slc.shape[2],), jnp.int32)],
        compiler_params=_cparams(("parallel", "arbitrary")),
        name="nsa_prompt",
    )(q_t, gates, comp, _cmp_position_columns(), k_slc, vt_slc, k_win, vt_win, m_t)


def _position_columns(pos):
    low, high = (pos % KEY_LOW)[:, None], (pos // KEY_LOW)[:, None]
    return np.concatenate([low, low, low, high, high, high], axis=1).astype(np.float32)


def _cmp_position_columns():
    aug = np.zeros((N_CMP_PAD, HEAD_DIM), np.float32)
    aug[:, 0:6] = _position_columns(np.arange(N_CMP_PAD) * CMP_STRIDE + CMP_BLOCK - 1)
    return jnp.asarray(aug, dtype=bf16)


LAYOUT_ROWS = 1024


def _attention_layout_kernel(q_ref, cmp_ref, slc_ref, win_ref, xs_ref, xw_ref,
                             qt_ref, ks_ref, vts_ref, kw_ref, vtw_ref, cmp_t_ref, slc_t_ref, win_t_ref):
    cmp_t_ref[0] = cmp_ref[...].T
    tm = q_ref.shape[0]
    q = q_ref[...] * (HEAD_DIM ** -0.5 * LOG2E)
    for qb in range(tm // Q_BLOCK):
        rows = slice(qb * Q_BLOCK, (qb + 1) * Q_BLOCK)
        for pair in range(NSA_HEADS // 2):
            t = q[rows, pair * LANES:(pair + 1) * LANES].T
            for k in range(2):
                g, r = divmod(2 * pair + k, NSA_GROUP)
                qt_ref[0, g, qb, :, r * Q_BLOCK:(r + 1) * Q_BLOCK] = t[k * HEAD_DIM:(k + 1) * HEAD_DIM].astype(bf16)

    for src_ref, extra_ref, k_ref, vt_ref, t_ref in ((slc_ref, xs_ref, ks_ref, vts_ref, slc_t_ref),
                                                     (win_ref, xw_ref, kw_ref, vtw_ref, win_t_ref)):
        tile = vt_ref.shape[4]
        s = src_ref[...]
        s_t = s.T
        t_ref[0] = s_t
        v_t = s_t[KV_WIDTH:]
        tail = jnp.where(lax.broadcasted_iota(jnp.int32, (VT_ROWS - HEAD_DIM, tile), 0) == 0, 1.0, 0.0)
        for g in range(NSA_KV_HEADS):
            k_ref[0, g] = jnp.concatenate([s[:, g * HEAD_DIM:(g + 1) * HEAD_DIM].astype(bf16), extra_ref[...]], axis=1)
            for c in range(tm // tile):
                blk = v_t[g * HEAD_DIM:(g + 1) * HEAD_DIM, c * tile:(c + 1) * tile]
                vt_ref[0, g, c] = jnp.concatenate([blk, tail], axis=0).astype(bf16)


def _attention_layout(z, cmp, slc, win, bsz, seq):
    tm = LAYOUT_ROWS
    per_b = seq // tm
    pos = np.arange(seq)
    xs = np.zeros((seq, HEAD_DIM), np.float32)
    xs[pos, (pos // SEL_BLOCK) % (SEL_CHUNK // SEL_BLOCK)] = 1.0
    xs[:, 8:11] = (pos % KEY_LOW)[:, None]
    xs[:, 11:14] = (pos % SEL_CHUNK - pos % KEY_LOW)[:, None]
    xw = np.zeros((seq, HEAD_DIM), np.float32)
    xw[:, 0:6] = _position_columns(pos)
    row = lambda i: (i, 0)
    rep = lambda i: (i % per_b, 0)
    out4 = lambda i: (i // per_b, 0, i % per_b, 0)
    out5 = lambda i: (i // per_b, 0, i % per_b, 0, 0)
    k_shape = jax.ShapeDtypeStruct((bsz, NSA_KV_HEADS, seq, LANES), bf16)
    vt_shape = lambda tile: jax.ShapeDtypeStruct((bsz, NSA_KV_HEADS, seq // tile, VT_ROWS, tile), bf16)
    vt_spec = lambda tile: pl.BlockSpec((1, NSA_KV_HEADS, tm // tile, VT_ROWS, tile), out5)
    rows_t_shape = jax.ShapeDtypeStruct((bsz, ROW_WIDTH, seq), f32)
    rows_t_spec = pl.BlockSpec((1, ROW_WIDTH, tm), lambda i: (i // per_b, 0, i % per_b))
    return pl.pallas_call(
        _attention_layout_kernel,
        grid=(bsz * per_b,),
        in_specs=[pl.BlockSpec((tm, NSA_WIDTH), row), pl.BlockSpec((tm, ROW_WIDTH), row),
                  pl.BlockSpec((tm, ROW_WIDTH), row), pl.BlockSpec((tm, ROW_WIDTH), row),
                  pl.BlockSpec((tm, HEAD_DIM), rep), pl.BlockSpec((tm, HEAD_DIM), rep)],
        out_specs=[pl.BlockSpec((1, NSA_KV_HEADS, tm // Q_BLOCK, HEAD_DIM, NSA_GROUP * Q_BLOCK), out5),
                   pl.BlockSpec((1, NSA_KV_HEADS, tm, LANES), out4), vt_spec(SEL_CHUNK),
                   pl.BlockSpec((1, NSA_KV_HEADS, tm, LANES), out4), vt_spec(Q_BLOCK),
                   rows_t_spec, rows_t_spec, rows_t_spec],
        out_shape=[jax.ShapeDtypeStruct((bsz, NSA_KV_HEADS, seq // Q_BLOCK, HEAD_DIM, NSA_GROUP * Q_BLOCK), bf16),
                   k_shape, vt_shape(SEL_CHUNK), k_shape, vt_shape(Q_BLOCK),
                   rows_t_shape, rows_t_shape, rows_t_shape],
        compiler_params=_cparams(("parallel",)),
        name="attention_layout",
    )(z, cmp, slc, win, jnp.asarray(xs, dtype=bf16), jnp.asarray(xw, dtype=bf16))


HG_CHUNK = 128
HG_STEP_CHUNKS = 4
HG_BAND = 8


def _lower_bound(lo):
    m = jnp.max(lo, axis=0, keepdims=True)
    e = jnp.exp(lo - m)
    return e[0:1] / jnp.sum(e, axis=0, keepdims=True)


def _hgrn_gates(q, fl, lb):
    sg = _sigmoid(fl)
    f = lb + (1.0 - lb) * sg
    k = (1.0 - lb) * (1.0 - sg)
    return _silu(q), k, jnp.log(f)


def _hgrn_out(o, nw, zb):
    o = o * lax.rsqrt(jnp.mean(o * o, axis=-1, keepdims=True) + EPS) * nw
    return o * _silu(zb)


def _hgrn_prompt_kernel(q_ref, f_ref, i_ref, zb_ref, lo_ref, nw_ref, y_ref, s_ref, st_ref):
    c = pl.program_id(1)

    @pl.when(c == 0)
    def _():
        st_ref[...] = jnp.zeros_like(st_ref)

    n = HG_CHUNK
    row = lax.broadcasted_iota(jnp.int32, (n, n), 0)
    col = lax.broadcasted_iota(jnp.int32, (n, n), 1)
    heads = [slice(h * HG_DIM, (h + 1) * HG_DIM) for h in range(HG_HEADS)]

    qq, kk, logf = _hgrn_gates(q_ref[...], f_ref[...], _lower_bound(lo_ref[...]))
    v = i_ref[...]
    trow = lax.broadcasted_iota(jnp.int32, (n, 1), 0)
    chunks = [slice(i * n, (i + 1) * n) for i in range(q_ref.shape[0] // n)]
    bs = []
    for rs in chunks:
        b = logf[rs]
        step = 1
        while step < n:
            b = b + jnp.where(trow >= step, pltpu.roll(b, step, axis=0), 0.0)
            step *= 2
        bs.append(b)

    intra = [[_hgrn_intra(qq[rs, hs], kk[rs, hs], b[:, hs], row, col) for hs in heads] for rs, b in zip(chunks, bs)]
    for i, rs in enumerate(chunks):
        for h, hs in enumerate(heads):
            st = st_ref[h]
            bh, kh, vh = bs[i][:, hs], kk[rs, hs], v[rs, hs].astype(bf16)
            o = _dot(intra[i][h].astype(bf16), vh) + _dot_nt((qq[rs, hs] * jnp.exp(bh)).astype(bf16), st.astype(bf16))
            b_last = bh[n - 1:n]
            st_new = st * jnp.exp(b_last) + _dot_tn(vh, (kh * jnp.exp(b_last - bh)).astype(bf16))
            st_ref[h] = st_new
            y_ref[rs, hs] = _hgrn_out(o, nw_ref[...], zb_ref[rs, hs])

    @pl.when(c == pl.num_programs(1) - 1)
    def _():
        for h in range(HG_HEADS):
            s_ref[0, h] = st_ref[h].T


def _hgrn_intra(qq, kk, b, row, col):
    n = HG_CHUNK
    a = jnp.where(row == col, _dot_nt(qq.astype(bf16), kk.astype(bf16)), 0.0)

    sub = jnp.bitwise_and(row, HG_BAND - 1)
    b3 = b.reshape(n // HG_BAND, HG_BAND, HG_DIM)
    size = 2
    while size <= HG_BAND:
        half = size // 2
        beta = None
        for s0 in range(0, HG_BAND, size):
            piece = jnp.broadcast_to(b3[:, s0 + half - 1:s0 + half, :], b3.shape).reshape(n, HG_DIM)
            beta = piece if beta is None else jnp.where(sub >= s0, piece, beta)
        left = jnp.bitwise_and(sub, size - 1) < half
        decay = jnp.exp(jnp.where(left, beta - b, b - beta))
        part = _dot_nt(jnp.where(left, 0.0, qq * decay).astype(bf16), jnp.where(left, kk * decay, 0.0).astype(bf16))
        sh = size.bit_length() - 1
        a = a + jnp.where(jnp.right_shift(row, sh) == jnp.right_shift(col, sh), part, 0.0)
        size *= 2

    while size <= n:
        half = size // 2
        qs, ks = [], []
        zero = jnp.zeros((half, HG_DIM), f32)
        for r0 in range(0, n, size):
            beta = b[r0 + half - 1:r0 + half]
            ks += [kk[r0:r0 + half] * jnp.exp(beta - b[r0:r0 + half]), zero]
            qs += [zero, qq[r0 + half:r0 + size] * jnp.exp(b[r0 + half:r0 + size] - beta)]
        part = _dot_nt(jnp.concatenate(qs, axis=0).astype(bf16), jnp.concatenate(ks, axis=0).astype(bf16))
        if size < n:
            sh = size.bit_length() - 1
            part = jnp.where(jnp.right_shift(row, sh) == jnp.right_shift(col, sh), part, 0.0)
        a = a + part
        size *= 2
    return a


def _hgrn_prompt(z, hg_lower, hg_norm, bsz, seq):
    rows = HG_STEP_CHUNKS * HG_CHUNK
    nch = seq // rows

    def zcol(base):
        return pl.BlockSpec((rows, HG_WIDTH), lambda b, c: (b * nch + c, base // HG_WIDTH))

    return pl.pallas_call(
        _hgrn_prompt_kernel,
        grid=(bsz, nch),
        in_specs=[zcol(COL_QB), zcol(COL_FB), zcol(COL_IB), zcol(COL_ZB),
                  pl.BlockSpec(hg_lower.shape, lambda b, c: (0, 0)),
                  pl.BlockSpec((1, HG_DIM), lambda b, c: (0, 0))],
        out_specs=[pl.BlockSpec((rows, HG_WIDTH), lambda b, c: (b * nch + c, 0)),
                   pl.BlockSpec((1, HG_HEADS, HG_DIM, HG_DIM), lambda b, c: (b, 0, 0, 0))],
        out_shape=[jax.ShapeDtypeStruct((bsz * seq, HG_WIDTH), f32),
                   jax.ShapeDtypeStruct((bsz, HG_HEADS, HG_DIM, HG_DIM), f32)],
        scratch_shapes=[pltpu.VMEM((HG_HEADS, HG_DIM, HG_DIM), f32)],
        compiler_params=_cparams(("parallel", "arbitrary")),
        name="hgrn_prompt",
    )(z, z, z, z, hg_lower, hg_norm)


def _hgrn_sample_kernel(q_ref, f_ref, i_ref, zb_ref, lo_ref, nw_ref, s_ref, y_ref, so_ref, o_scr):
    nb = q_ref.shape[0]
    lb = _lower_bound(lo_ref[...])
    qq, kk, logf = _hgrn_gates(q_ref[...], f_ref[...], lb)
    v = i_ref[...]
    pad = jnp.zeros((HG_DIM - nb, HG_DIM), f32)
    to_cols = lambda x: jnp.concatenate([x, pad], axis=0).T
    f_t, k_t, q_t = to_cols(jnp.exp(logf)), to_cols(kk), to_cols(qq)
    for b in range(nb):
        s_new = f_t[:, b:b + 1] * s_ref[0, b, 0] + k_t[:, b:b + 1] * v[b:b + 1, :]
        so_ref[0, b, 0] = s_new
        o_scr[b:b + 1, :] = jnp.sum(s_new * q_t[:, b:b + 1], axis=0, keepdims=True)
    y_ref[...] = _hgrn_out(o_scr[...], nw_ref[...], zb_ref[...])


def _hgrn_sample(z, state, hg_lower, hg_norm):
    nb = z.shape[0]

    def zcol(base):
        return pl.BlockSpec((nb, HG_DIM), lambda h: (0, base // HG_DIM + h))

    sblk = pl.BlockSpec((1, nb, 1, HG_DIM, HG_DIM), lambda h: (0, 0, h, 0, 0))
    return pl.pallas_call(
        _hgrn_sample_kernel,
        grid=(HG_HEADS,),
        in_specs=[zcol(COL_QB), zcol(COL_FB), zcol(COL_IB), zcol(COL_ZB),
                  pl.BlockSpec((hg_lower.shape[0], HG_DIM), lambda h: (0, h)),
                  pl.BlockSpec((1, HG_DIM), lambda h: (0, 0)),
                  sblk],
        out_specs=[pl.BlockSpec((nb, HG_DIM), lambda h: (0, h)), sblk],
        out_shape=[jax.ShapeDtypeStruct((nb, HG_WIDTH), f32),
                   jax.ShapeDtypeStruct(state.shape, f32)],
        scratch_shapes=[pltpu.VMEM((nb, HG_DIM), f32)],
        compiler_params=_cparams(("parallel",)),
        name="hgrn_sample",
    )(z, z, z, z, hg_lower, hg_norm, state)


N_PICK = N_SEL - 1


def _head_slopes():
    h = lax.broadcasted_iota(jnp.int32, (NSA_HEADS, 1), 0).astype(f32)
    return jnp.exp2(-(h + 1.0))


def _nsa_sample_cmp_kernel(q_ref, comp_ref, m_ref, oc_ref, ps_ref, *, t_pos):
    q = q_ref[0] * jnp.asarray(HEAD_DIM ** -0.5, bf16)
    slope = _head_slopes()
    head = lax.broadcasted_iota(jnp.int32, (NSA_HEADS, 1), 0)
    ncol = lax.broadcasted_iota(jnp.int32, (1, N_CMP_PAD), 1)
    d_c = t_pos - (ncol * CMP_STRIDE + (CMP_BLOCK - 1))
    mask = d_c >= 0
    o_c = jnp.zeros((NSA_HEADS, HEAD_DIM), f32)
    ps_ref[...] = jnp.zeros_like(ps_ref)
    for g in range(NSA_KV_HEADS):
        kc = comp_ref[0, g].astype(bf16)
        vc = comp_ref[0, NSA_KV_HEADS + g].astype(bf16)
        s = jnp.where(mask, _dot_nt(q, kc) - slope * d_c.astype(f32), NEG)
        m = jnp.max(s, axis=1, keepdims=True)
        e = jnp.where(mask, jnp.exp(s - m), 0.0)
        p = e * (1.0 / jnp.maximum(jnp.sum(e, axis=1, keepdims=True), 1e-30))
        mine = jnp.right_shift(head, NSA_GROUP.bit_length() - 1) == g
        o_c = jnp.where(mine, _dot(p.astype(bf16), vc), o_c)
        imp = jnp.sum(jnp.where(mine, p, 0.0), axis=0, keepdims=True)
        ps_ref[0, g:g + 1, :] = _importance_to_blocks(imp, m_ref[...])
    oc_ref[0] = o_c


def _nsa_sample_cmp(q8, comp, m_mat, t_pos):
    nb = q8.shape[0]
    return pl.pallas_call(
        functools.partial(_nsa_sample_cmp_kernel, t_pos=t_pos),
        grid=(nb,),
        in_specs=[pl.BlockSpec((1, NSA_HEADS, HEAD_DIM), lambda b: (b, 0, 0)),
                  pl.BlockSpec((1, N_COMBO, N_CMP_PAD, HEAD_DIM), lambda b: (b, 0, 0, 0)),
                  pl.BlockSpec((N_CMP_PAD, N_SELBLK), lambda b: (0, 0))],
        out_specs=[pl.BlockSpec((1, NSA_HEADS, HEAD_DIM), lambda b: (b, 0, 0)),
                   pl.BlockSpec((1, 8, N_SELBLK), lambda b: (b, 0, 0))],
        out_shape=[jax.ShapeDtypeStruct((nb, NSA_HEADS, HEAD_DIM), f32),
                   jax.ShapeDtypeStruct((nb, 8, N_SELBLK), f32)],
        compiler_params=_cparams(("parallel",)),
        name="nsa_sample_cmp",
    )(q8, comp, m_mat)


def _nsa_sample_topk_kernel(ps_ref, idx_ref, *, last_blk):
    ps = ps_ref[...]
    lane = lax.broadcasted_iota(jnp.int32, ps.shape, 1)
    score = ps + jnp.where((lane == 0) | (lane == last_blk), FORCE_BONUS, 0.0)
    out = jnp.zeros(ps.shape, jnp.int32)
    for r in range(N_PICK):
        best = jnp.max(score, axis=1, keepdims=True)
        idx = jnp.min(jnp.where(score == best, lane, N_SELBLK), axis=1, keepdims=True)
        out = jnp.where(lane == r, idx, out)
        score = jnp.where(lane == idx, REMOVED, score)
    idx_ref[...] = out


def _nsa_sample_topk(ps_rows, last_blk):
    return pl.pallas_call(
        functools.partial(_nsa_sample_topk_kernel, last_blk=last_blk),
        out_shape=jax.ShapeDtypeStruct(ps_rows.shape, jnp.int32),
        name="nsa_sample_topk",
    )(ps_rows)


def _nsa_sample_sel_kernel(pt_ref, idx_ref, slc_ref, q_ref, oc_ref, gate_ref, slc_new_ref, win_new_ref, wc_ref,
                           o_ref, wout_ref, kv_scr, sem_ref, *, t_pos):
    n_blk = NSA_KV_HEADS * N_PICK
    halves = PAGE_SIZE // SEL_BLOCK
    b = pl.program_id(0)
    nb = pl.num_programs(0)
    slot = lax.rem(b, 2)

    def picked(seq, k):
        return idx_ref[(seq * NSA_KV_HEADS + k // N_PICK) * N_SEL + k % N_PICK]

    def blk_copies(seq, s, k):
        page = pt_ref[seq, picked(seq, k) // halves]
        g = k // N_PICK
        return [pltpu.make_async_copy(slc_ref.at[page, kind * NSA_KV_HEADS + g], kv_scr.at[s, kind, k],
                                      sem_ref.at[s]) for kind in range(2)]

    def start_all(seq, s):
        for k in range(n_blk):
            for cp in blk_copies(seq, s, k):
                cp.start()

    @pl.when(b == 0)
    def _():
        start_all(0, 0)

    @pl.when(b + 1 < nb)
    def _():
        start_all(b + 1, 1 - slot)

    for k in range(n_blk):
        for cp in blk_copies(b, slot, k):
            cp.wait()

    q = q_ref[0] * jnp.asarray(HEAD_DIM ** -0.5, bf16)
    slope = _head_slopes()
    head = lax.broadcasted_iota(jnp.int32, (NSA_HEADS, 1), 0)
    lane = lax.broadcasted_iota(jnp.int32, (1, PAGE_SIZE), 1)
    new_slc = slc_new_ref[0]
    new_win = win_new_ref[0]
    wl = wc_ref.shape[3]
    wlane = lax.broadcasted_iota(jnp.int32, (1, wl), 1)
    d_w = (wl - wlane).astype(f32)
    qf = q.astype(f32)

    def attend(s, mask, v_t, new_row, g):
        k_new = new_row[:, g * HEAD_DIM:(g + 1) * HEAD_DIM].astype(bf16).astype(f32)
        v_new = new_row[:, KV_WIDTH + g * HEAD_DIM:KV_WIDTH + (g + 1) * HEAD_DIM].astype(bf16).astype(f32)
        s_new = jnp.sum(qf * k_new, axis=1, keepdims=True)
        m = jnp.maximum(jnp.max(s, axis=1, keepdims=True), s_new)
        e, e_new = jnp.exp(s - m), jnp.exp(s_new - m)
        if mask is not None:
            e = jnp.where(mask, e, 0.0)
        den = jnp.maximum(jnp.sum(e, axis=1, keepdims=True) + e_new, 1e-30)
        num = _dot_nt(e.astype(bf16), v_t.astype(bf16)) + e_new.astype(bf16).astype(f32) * v_new
        return num * (1.0 / den)

    o_s = jnp.zeros((NSA_HEADS, HEAD_DIM), f32)
    o_w = jnp.zeros((NSA_HEADS, HEAD_DIM), f32)
    for g in range(NSA_KV_HEADS):
        mine = jnp.right_shift(head, NSA_GROUP.bit_length() - 1) == g
        pos, member = [], []
        for n in range(N_PICK):
            j = picked(b, g * N_PICK + n)
            pos.append((j // halves) * PAGE_SIZE + lane)
            member.append(jnp.right_shift(lane, SEL_BLOCK.bit_length() - 1) == j % halves)
        pos = jnp.concatenate(pos, axis=1)
        member = jnp.concatenate(member, axis=1)
        k_t = jnp.concatenate([kv_scr[slot, 0, g * N_PICK + n] for n in range(N_PICK)], axis=1)
        v_t = jnp.concatenate([kv_scr[slot, 1, g * N_PICK + n] for n in range(N_PICK)], axis=1)
        s = jnp.where(member, _dot(q, k_t.astype(bf16)) - slope * (t_pos - pos).astype(f32), NEG)
        o_s = jnp.where(mine, attend(s, member, v_t, new_slc, g), o_s)
        s = _dot(q, wc_ref[0, g].astype(bf16)) - slope * d_w
        o_w = jnp.where(mine, attend(s, None, wc_ref[0, NSA_KV_HEADS + g], new_win, g), o_w)

    gate = jnp.broadcast_to(_sigmoid(gate_ref[0]), (NSA_HEADS, LANES))
    glane = lax.broadcasted_iota(jnp.int32, (NSA_HEADS, LANES), 1)
    gcol = [jnp.sum(jnp.where(glane == 3 * head + j, gate, 0.0), axis=1, keepdims=True) for j in range(3)]
    o_ref[0] = gcol[0] * oc_ref[0] + gcol[1] * o_s + gcol[2] * o_w
    eye = lax.broadcasted_iota(jnp.int32, (HEAD_DIM, HEAD_DIM), 0) == \
        lax.broadcasted_iota(jnp.int32, (HEAD_DIM, HEAD_DIM), 1)
    for c in range(N_COMBO):
        piece = jnp.broadcast_to(new_win[:, c * HEAD_DIM:(c + 1) * HEAD_DIM], (HEAD_DIM, HEAD_DIM))
        new_col = jnp.sum(jnp.where(eye, piece, 0.0), axis=1, keepdims=True)
        wout_ref[0, c] = jnp.where(wlane == wl - 1, new_col, pltpu.roll(wc_ref[0, c], wl - 1, axis=1))


def _nsa_sample_sel(page_table, idx_flat, slc_pages_t, q8, o_c, gates, slc_new, win_new, cache_win_t, t_pos):
    nb = q8.shape[0]
    wl = cache_win_t.shape[3]
    n_blk = NSA_KV_HEADS * N_PICK
    per_seq = lambda b, pt, idx: (b, 0, 0)
    win_blk = pl.BlockSpec((1, N_COMBO, HEAD_DIM, wl), lambda b, pt, idx: (b, 0, 0, 0))
    grid_spec = pltpu.PrefetchScalarGridSpec(
        num_scalar_prefetch=2,
        grid=(nb,),
        in_specs=[pl.BlockSpec(memory_space=pl.ANY),
                  pl.BlockSpec((1, NSA_HEADS, HEAD_DIM), per_seq),
                  pl.BlockSpec((1, NSA_HEADS, HEAD_DIM), per_seq),
                  pl.BlockSpec((1, 1, LANES), per_seq),
                  pl.BlockSpec((1, 1, ROW_WIDTH), per_seq),
                  pl.BlockSpec((1, 1, ROW_WIDTH), per_seq),
                  win_blk],
        out_specs=[pl.BlockSpec((1, NSA_HEADS, HEAD_DIM), per_seq), win_blk],
        scratch_shapes=[pltpu.VMEM((2, 2, n_blk, HEAD_DIM, PAGE_SIZE), f32),
                        pltpu.SemaphoreType.DMA((2,))],
    )
    return pl.pallas_call(
        functools.partial(_nsa_sample_sel_kernel, t_pos=t_pos),
        grid_spec=grid_spec,
        out_shape=[jax.ShapeDtypeStruct((nb, NSA_HEADS, HEAD_DIM), f32),
                   jax.ShapeDtypeStruct(cache_win_t.shape, f32)],
        compiler_params=_cparams(("arbitrary",)),
        name="nsa_sample_sel",
    )(page_table, idx_flat, slc_pages_t, q8, o_c, gates, slc_new, win_new, cache_win_t)


def kernel(x_prompt, x_sample, cache_cmp_kv, cache_slc_kv, cache_win_kv, state_hgrn, page_table, p_prompt, p_sample,
           w_in, g_pre, cmp_pe, cmp_w1, cmp_b1, cmp_w2, hg_lower, hg_norm, w_out, g_post, ple_proj, ple_gate):
    bsz, seq, _ = x_prompt.shape
    nb = x_sample.shape[0]
    n_pool = cache_cmp_kv.shape[1]
    n_pages = page_table.shape[1]
    past = n_pages * PAGE_SIZE
    kv_tail = (2, NSA_KV_HEADS, HEAD_DIM)

    o = _ORIG
    w = w_in[0]
    w_r = jnp.concatenate([w[:, o['q_a']:o['k_cmp']], w[:, o['z_a']:o['end']], w[:, o['k_cmp']:o['gate_a']],
                           w[:, o['gate_a']:o['z_a']],
                           jnp.zeros((D_MODEL, PROJ_WIDTH - o['end']), f32)], axis=1).astype(bf16)
    cw = _compress_weights(cmp_pe[0], cmp_w1[0], cmp_b1[0], cmp_w2[0])
    m_mat = _sel_map_matrix()
    w_out_b, ple_proj_b, ple_gate_b = w_out[0].astype(bf16), ple_proj[0].astype(bf16), ple_gate[0].astype(bf16)

    xp = x_prompt.reshape(bsz * seq, D_MODEL)
    zp, cmp_p, slc_p, win_p, gates_p = _proj(xp, g_pre, w_r, 1024)
    pages_p = cmp_p.reshape(bsz * seq // PAGE_SIZE, PAGE_SIZE, ROW_WIDTH)
    table_p = jnp.arange(bsz * seq // PAGE_SIZE, dtype=jnp.int32).reshape(bsz, seq // PAGE_SIZE)
    comp_p = _compress(pages_p, table_p, *cw, feature_major=False)
    q_t, k_slc, vt_slc, k_win, vt_win, cmp_tp, slc_tp, win_tp = _attention_layout(zp, cmp_p, slc_p, win_p, bsz, seq)
    o_a = _nsa_prompt_pair(q_t, gates_p, comp_p, k_slc, vt_slc, k_win, vt_win, m_mat.T)
    yb_p, hg_p = _hgrn_prompt(zp, hg_lower, hg_norm, bsz, seq)
    h_p = _finish(xp, o_a, zp, yb_p, p_prompt[0].reshape(bsz * seq, PLE_DIM),
                  w_out_b, g_post, ple_proj_b, ple_gate_b, 512)

    wl_p = min(WINDOW, seq)

    def cache_rows(rows_t):
        return rows_t.reshape((1, bsz) + kv_tail + rows_t.shape[2:]).transpose(0, 1, 5, 2, 3, 4)

    out_prompt = (h_p.reshape(bsz, seq, D_MODEL), cache_rows(cmp_tp), cache_rows(slc_tp),
                  cache_rows(win_tp[:, :, seq - wl_p:]), hg_p[None])

    xs = x_sample.reshape(nb, D_MODEL)
    zs, cmp_s, slc_s, win_new, gates_s = _proj(xs, g_pre, w_r, nb)
    cmp_t = cache_cmp_kv.transpose(0, 1, 3, 4, 5, 2).reshape(n_pool, ROW_WIDTH, PAGE_SIZE)
    comp_s = _compress(cmp_t, page_table, *cw, feature_major=True)
    q8 = zs[:, COL_QA:COL_QA + NSA_WIDTH].astype(bf16).reshape(nb, NSA_HEADS, HEAD_DIM)
    oc_s, ps_s = _nsa_sample_cmp(q8, comp_s, m_mat, past)
    ps_rows = ps_s[:, :NSA_KV_HEADS, :].reshape(nb * NSA_KV_HEADS, N_SELBLK)
    idx = _nsa_sample_topk(ps_rows, past // SEL_BLOCK - 1)
    slc_t = cache_slc_kv.transpose(0, 1, 3, 4, 5, 2).reshape(n_pool, N_COMBO, HEAD_DIM, PAGE_SIZE)
    wl_s = cache_win_kv.shape[2]
    win_t = cache_win_kv.transpose(0, 1, 3, 4, 5, 2).reshape(nb, N_COMBO, HEAD_DIM, wl_s)
    oa_s, win_s = _nsa_sample_sel(page_table, idx[:, :N_SEL].reshape(-1), slc_t, q8, oc_s,
                                  gates_s.reshape(nb, 1, LANES), slc_s.reshape(nb, 1, ROW_WIDTH),
                                  win_new.reshape(nb, 1, ROW_WIDTH), win_t, past)
    win_s = win_s.reshape((1, nb) + kv_tail + (wl_s,)).transpose(0, 1, 5, 2, 3, 4)
    yb_s, hg_s = _hgrn_sample(zs, state_hgrn, hg_lower, hg_norm)
    h_s = _finish(xs, oa_s.reshape(nb, NSA_WIDTH), zs, yb_s, p_sample[0].reshape(nb, PLE_DIM),
                  w_out_b, g_post, ple_proj_b, ple_gate_b, nb)

    out_sample = (h_s.reshape(nb, 1, D_MODEL),
                  cmp_s.reshape((1, nb, 1) + kv_tail),
                  slc_s.reshape((1, nb, 1) + kv_tail),
                  win_s,
                  hg_s)
    return (out_prompt[0], out_sample[0]) + out_prompt[1:] + out_sample[1:]
```

```python
import functools

import numpy as np
import jax
import jax.numpy as jnp
from jax import lax
from jax.experimental import pallas as pl
from jax.experimental.pallas import tpu as pltpu

D_MODEL = 1024
PAGE_SIZE = 128
NSA_HEADS = 8
NSA_KV_HEADS = 2
NSA_GROUP = NSA_HEADS // NSA_KV_HEADS
HEAD_DIM = 64
NSA_WIDTH = NSA_HEADS * HEAD_DIM
KV_WIDTH = NSA_KV_HEADS * HEAD_DIM
CMP_BLOCK = 32
CMP_STRIDE = 16
CMP_RATIO = CMP_BLOCK // CMP_STRIDE
CMP_HIDDEN = 128
SEL_BLOCK = 64
SEL_RATIO = SEL_BLOCK // CMP_STRIDE
N_SEL = 16
WINDOW = 512
Q_BLOCK = 128
HG_HEADS = 4
HG_DIM = 128
HG_WIDTH = HG_HEADS * HG_DIM
PLE_DIM = 256
EPS = 1e-6
NEG = -1e30
FORCE_BONUS = 1e6
REMOVED = -3e38

COL_QA, COL_ZA, COL_QB, COL_FB, COL_IB, COL_ZB = 0, 512, 1024, 1536, 2048, 2560
COL_CMP, COL_SLC, COL_WIN, COL_GATE = 3072, 3328, 3584, 3840
PROJ_WIDTH = 4096
_ORIG = dict(q_a=0, k_cmp=512, gate_a=1280, z_a=1304, q_b=1816, f_b=2328, i_b=2840, z_b=3352, end=3864)

LANES = 128
VMEM_LIMIT = 56 * 1024 * 1024

ROW_WIDTH = 2 * KV_WIDTH
N_COMBO = 2 * NSA_KV_HEADS
SEL_CHUNK = 512
WIN_KEYS = WINDOW + Q_BLOCK
N_CMP_PAD = 512
N_SELBLK = 128

f32 = jnp.float32
bf16 = jnp.bfloat16


def _cparams(sem):
    return pltpu.CompilerParams(dimension_semantics=sem, vmem_limit_bytes=VMEM_LIMIT)


def _dot(a, b):
    return jnp.dot(a, b, preferred_element_type=f32)


def _dot_nt(a, b):
    return lax.dot_general(a, b, (((1,), (1,)), ((), ())), preferred_element_type=f32)


def _dot_tn(a, b):
    return lax.dot_general(a, b, (((0,), (0,)), ((), ())), preferred_element_type=f32)


def _sigmoid(x):
    return 1.0 / (1.0 + jnp.exp(-x))


def _silu(x):
    return x * _sigmoid(x)


PROJ_TILE = 1024
MAIN_TILES = COL_CMP // PROJ_TILE


def _proj_kernel(x_ref, g_ref, w_ref, z_ref, cmp_ref, slc_ref, win_ref, gate_ref, xn_ref):
    j = pl.program_id(1)

    @pl.when(j == 0)
    def _():
        x = x_ref[...]
        y = x * lax.rsqrt(jnp.mean(x * x, axis=-1, keepdims=True) + EPS)
        xn_ref[...] = (y * g_ref[...]).astype(bf16)

    res = _dot(xn_ref[...], w_ref[...])

    @pl.when(j < MAIN_TILES)
    def _():
        z_ref[...] = res

    @pl.when(j == MAIN_TILES)
    def _():
        base = COL_CMP
        cmp_ref[...] = res[:, COL_CMP - base:COL_SLC - base]
        slc_ref[...] = res[:, COL_SLC - base:COL_WIN - base]
        win_ref[...] = res[:, COL_WIN - base:COL_GATE - base]
        gate_ref[...] = res[:, COL_GATE - base:COL_GATE - base + LANES]


def _proj(x, g, w, tm):
    n = x.shape[0]
    row = lambda i, j: (i, 0)
    return pl.pallas_call(
        _proj_kernel,
        grid=(n // tm, PROJ_WIDTH // PROJ_TILE),
        in_specs=[pl.BlockSpec((tm, D_MODEL), row),
                  pl.BlockSpec((1, D_MODEL), lambda i, j: (0, 0)),
                  pl.BlockSpec((D_MODEL, PROJ_TILE), lambda i, j: (0, j))],
        out_specs=[pl.BlockSpec((tm, PROJ_TILE), lambda i, j: (i, jnp.minimum(j, MAIN_TILES - 1))),
                   pl.BlockSpec((tm, ROW_WIDTH), row), pl.BlockSpec((tm, ROW_WIDTH), row),
                   pl.BlockSpec((tm, ROW_WIDTH), row), pl.BlockSpec((tm, LANES), row)],
        out_shape=[jax.ShapeDtypeStruct((n, COL_CMP), f32),
                   jax.ShapeDtypeStruct((n, ROW_WIDTH), f32), jax.ShapeDtypeStruct((n, ROW_WIDTH), f32),
                   jax.ShapeDtypeStruct((n, ROW_WIDTH), f32), jax.ShapeDtypeStruct((n, LANES), f32)],
        scratch_shapes=[pltpu.VMEM((tm, D_MODEL), bf16)],
        compiler_params=_cparams(("parallel", "arbitrary")),
        name="proj",
    )(x, g, w)


def _finish_kernel(x_ref, oa_ref, za_ref, yb_ref, p_ref, wo_ref, gp_ref, pp_ref, pg_ref, h_ref):
    ya = (oa_ref[...] * _silu(za_ref[...])).astype(bf16)
    yb = yb_ref[...].astype(bf16)
    mix = _dot(ya, wo_ref[0:NSA_WIDTH, :]) + _dot(yb, wo_ref[NSA_WIDTH:, :])
    nrm = mix * lax.rsqrt(jnp.mean(mix * mix, axis=-1, keepdims=True) + EPS) * gp_ref[...]
    h = x_ref[...] + nrm
    gate = _sigmoid(_dot(h.astype(bf16), pg_ref[...]))
    h_ref[...] = h + _dot(p_ref[...].astype(bf16), pp_ref[...]) * gate


def _finish(x, oa, z, yb, p, w_out, g_post, ple_proj, ple_gate, tm):
    n = x.shape[0]
    za_blk = COL_ZA // NSA_WIDTH
    row = lambda i: (i, 0)
    const = lambda i: (0, 0)
    return pl.pallas_call(
        _finish_kernel,
        grid=(n // tm,),
        in_specs=[pl.BlockSpec((tm, D_MODEL), row),
                  pl.BlockSpec((tm, NSA_WIDTH), row),
                  pl.BlockSpec((tm, NSA_WIDTH), lambda i: (i, za_blk)),
                  pl.BlockSpec((tm, HG_WIDTH), row),
                  pl.BlockSpec((tm, PLE_DIM), row),
                  pl.BlockSpec((D_MODEL, D_MODEL), const),
                  pl.BlockSpec((1, D_MODEL), const),
                  pl.BlockSpec((PLE_DIM, D_MODEL), const),
                  pl.BlockSpec((D_MODEL, D_MODEL), const)],
        out_specs=pl.BlockSpec((tm, D_MODEL), row),
        out_shape=jax.ShapeDtypeStruct((n, D_MODEL), f32),
        compiler_params=_cparams(("parallel",)),
        name="finish",
    )(x, oa, z, yb, p, w_out, g_post, ple_proj, ple_gate)


SEGS_PER_PAGE = PAGE_SIZE // CMP_STRIDE


def _compress_kernel(pt_ref, pages_ref, wa_ref, wb_ref, pea_ref, peb_ref, b1_ref, w2_ref, out_ref,
                     stage_ref, rows_ref, sem_ref, *, n_pages, feature_major):
    b = pl.program_id(0)
    nb = pl.num_programs(0)
    slot = lax.rem(b, 2)

    def page_copy(seq, s, j):
        return pltpu.make_async_copy(pages_ref.at[pt_ref[seq, j]], stage_ref.at[s, j], sem_ref.at[s])

    def start_all(seq, s):
        def body(j, c):
            page_copy(seq, s, j).start()
            return c
        lax.fori_loop(0, n_pages, body, 0)

    @pl.when(b == 0)
    def _():
        start_all(0, 0)

    @pl.when(b + 1 < nb)
    def _():
        start_all(b + 1, 1 - slot)

    def wait_body(j, c):
        page_copy(b, slot, j).wait()
        return c
    lax.fori_loop(0, n_pages, wait_body, 0)

    lane_halves = ROW_WIDTH // LANES

    def to_rows(j, c):
        tile = stage_ref[slot, j]
        rows = tile.T if feature_major else tile
        r0 = pl.multiple_of(j * PAGE_SIZE, PAGE_SIZE)
        for h in range(lane_halves):
            rows_ref[h, pl.ds(r0, PAGE_SIZE), :] = rows[:, h * LANES:(h + 1) * LANES]
        return c
    lax.fori_loop(0, n_pages, to_rows, 0, unroll=4)

    n_seg = n_pages * SEGS_PER_PAGE
    for kind in range(lane_halves):
        cols = slice(kind * LANES, (kind + 1) * LANES)
        seg_a, seg_b = [], []
        for j in range(CMP_STRIDE):
            r = rows_ref[kind, pl.ds(j, n_seg, stride=CMP_STRIDE), :]
            seg_a.append((r + pea_ref[j][:, cols]).astype(bf16))
            seg_b.append((r + peb_ref[j][:, cols]).astype(bf16))
        pa = _dot(jnp.concatenate(seg_a, axis=1), wa_ref[kind])
        pb = _dot(jnp.concatenate(seg_b, axis=1), wb_ref[kind])
        pb = pltpu.roll(pb, n_seg - 1, axis=0)
        hid = _silu(pa + pb + b1_ref[kind])
        for g in range(NSA_KV_HEADS):
            hg = hid[:, g * CMP_HIDDEN:(g + 1) * CMP_HIDDEN].astype(bf16)
            out_ref[0, kind * NSA_KV_HEADS + g] = _dot(hg, w2_ref[kind])


def _compress(pages, page_table, wa, wb, pea, peb, b1, w2, feature_major):
    nseq, n_pages = page_table.shape
    n_seg = n_pages * SEGS_PER_PAGE
    const3 = lambda b, pt: (0, 0, 0)
    once = pl.Buffered(1)
    grid_spec = pltpu.PrefetchScalarGridSpec(
        num_scalar_prefetch=1,
        grid=(nseq,),
        in_specs=[pl.BlockSpec(memory_space=pl.ANY),
                  pl.BlockSpec(wa.shape, const3, pipeline_mode=once),
                  pl.BlockSpec(wb.shape, const3, pipeline_mode=once),
                  pl.BlockSpec((CMP_STRIDE, 1, ROW_WIDTH), const3),
                  pl.BlockSpec((CMP_STRIDE, 1, ROW_WIDTH), const3),
                  pl.BlockSpec(b1.shape, const3),
                  pl.BlockSpec((2, CMP_HIDDEN, HEAD_DIM), const3)],
        out_specs=pl.BlockSpec((1, N_COMBO, n_seg, HEAD_DIM), lambda b, pt: (b, 0, 0, 0)),
        scratch_shapes=[pltpu.VMEM((2, n_pages) + pages.shape[1:], f32),
                        pltpu.VMEM((ROW_WIDTH // LANES, n_pages * PAGE_SIZE, LANES), f32),
                        pltpu.SemaphoreType.DMA((2,))],
    )
    return pl.pallas_call(
        functools.partial(_compress_kernel, n_pages=n_pages, feature_major=feature_major),
        grid_spec=grid_spec,
        out_shape=jax.ShapeDtypeStruct((nseq, N_COMBO, n_seg, HEAD_DIM), f32),
        compiler_params=_cparams(("arbitrary",)),
        name="compress",
    )(page_table, pages, wa, wb, pea, peb, b1, w2)


def _compress_weights(cmp_pe, cmp_w1, cmp_b1, cmp_w2):
    eye = jnp.eye(NSA_KV_HEADS, dtype=f32)

    def big(w_half):
        t = jnp.einsum('kjdh,gb->kjbdgh', w_half, eye)
        return t.reshape(2, CMP_STRIDE * KV_WIDTH, NSA_KV_HEADS * CMP_HIDDEN)

    w1 = cmp_w1.reshape(2, 2, CMP_STRIDE, HEAD_DIM, CMP_HIDDEN)
    wa = big(w1[:, 0]).astype(bf16)
    wb = big(w1[:, 1]).astype(bf16)
    pe = cmp_pe.reshape(2, 2, CMP_STRIDE, HEAD_DIM)

    def pe_row(p):
        t = jnp.broadcast_to(p.transpose(1, 0, 2)[:, :, None, :],
                             (CMP_STRIDE, 2, NSA_KV_HEADS, HEAD_DIM))
        return t.reshape(CMP_STRIDE, 1, ROW_WIDTH)

    pea, peb = pe_row(pe[:, 0]), pe_row(pe[:, 1])
    b1 = jnp.broadcast_to(cmp_b1[:, None, :], (2, NSA_KV_HEADS, CMP_HIDDEN)).reshape(2, 1, NSA_KV_HEADS * CMP_HIDDEN)
    return wa, wb, pea, peb, b1, cmp_w2.astype(bf16)


def _sel_map_matrix():
    wts = np.convolve(np.ones(SEL_RATIO), np.ones(CMP_RATIO))
    m = np.zeros((N_CMP_PAD, N_SELBLK), np.float32)
    for j in range(N_SELBLK):
        for k, wk in enumerate(wts):
            n = SEL_RATIO * j + k
            if n < N_CMP_PAD - 1:
                m[n, j] = wk
    return jnp.asarray(m, dtype=bf16)


def _importance_to_blocks(imp, m):
    hi = imp.astype(bf16)
    lo = (imp - hi.astype(f32)).astype(bf16)
    return _dot(hi, m) + _dot(lo, m)


LOG2E = 1.4426950408889634
MASK_BIG = 1e30
AUG_ROWS = 16
VT_ROWS = HEAD_DIM + 16
KEY_LOW = 256
N_FORCED = 3


def _nsa_prompt_pair_kernel(qt_ref, gate_ref, comp_ref, caug_ref, ks_ref, vts_ref, kw_ref, vtw_ref, mt_ref, o_ref,
                            qa_ref, acc_ref, mg_ref, bias_ref, need_ref):
    qi = pl.program_id(1)
    q0 = qi * Q_BLOCK
    cols = NSA_GROUP * Q_BLOCK
    heads = range(NSA_KV_HEADS)

    lane = lax.broadcasted_iota(jnp.int32, (1, cols), 1)
    head = jnp.right_shift(lane, Q_BLOCK.bit_length() - 1)
    tq = q0 + jnp.bitwise_and(lane, Q_BLOCK - 1)
    blk = lax.broadcasted_iota(jnp.int32, (N_SELBLK, Q_BLOCK), 0)
    jt = jnp.right_shift(q0 + lax.broadcasted_iota(jnp.int32, (N_SELBLK, Q_BLOCK), 1), SEL_BLOCK.bit_length() - 1)
    valid = blk <= jt
    forced = (blk == 0) | (blk == jt) | (blk == jt - 1)
    c_end = lax.broadcasted_iota(jnp.int32, (N_CMP_PAD, 1), 0) * CMP_STRIDE + (CMP_BLOCK - 1)
    krow = lax.broadcasted_iota(jnp.int32, (SEL_CHUNK, 1), 0)
    arow = lax.broadcasted_iota(jnp.int32, (AUG_ROWS // 2, cols), 0)
    blk_per_chunk = SEL_CHUNK // SEL_BLOCK
    n_chunk_max = vts_ref.shape[2]
    last = q0 // SEL_CHUNK
    w0 = pl.multiple_of(jnp.maximum(q0 - WINDOW, 0), Q_BLOCK)
    wkey = w0 + lax.broadcasted_iota(jnp.int32, (WIN_KEYS, 1), 0)
    in_window = (wkey <= tq) & (wkey >= tq - WINDOW)
    gate_t = _sigmoid(gate_ref[...]).T


    slope = [jnp.exp2(-(head.astype(f32) + 1.0 + NSA_GROUP * g)) * LOG2E for g in heads]
    qts = [qt_ref[0, g, 0] for g in heads]

    def chunk_scores(g, ck, slot=0, live=None):
        k0 = pl.multiple_of(ck * SEL_CHUNK, SEL_CHUNK)
        b8 = bias_ref[g, pl.ds(pl.multiple_of(ck * blk_per_chunk, blk_per_chunk), blk_per_chunk), :]
        if live is not None:
            b8 = jnp.where(live, b8, -MASK_BIG)
        qa_ref[g, slot, HEAD_DIM:HEAD_DIM + AUG_ROWS, :] = jnp.concatenate(
            [jnp.concatenate([b8] * NSA_GROUP, axis=1), alibi_rows[g]], axis=0).astype(bf16)
        return _dot(ks_ref[0, g, pl.ds(k0, SEL_CHUNK), :], qa_ref[g, slot])

    def sel_chunk(g, ck, causal, s):
        c = slope[g]
        k0 = pl.multiple_of(ck * SEL_CHUNK, SEL_CHUNK)
        if causal:
            s = jnp.where(k0 + krow <= tq, s, NEG)
        shift = c * k0.astype(f32)
        m_old = mg_ref[g] - shift
        m_new = jnp.maximum(m_old, jnp.max(s, axis=0, keepdims=True))
        p = jnp.exp2(s - m_new).astype(bf16)
        acc_ref[g] = acc_ref[g] * jnp.exp2(m_old - m_new) + _dot(vts_ref[0, g, ck], p)
        mg_ref[g] = m_new + shift

    alibi_rows, s_cmp, s_win = [], [], []
    for g in heads:
        c, qt = slope[g], qts[g]
        c_hi = c.astype(bf16).astype(f32)
        c_mid = (c - c_hi).astype(bf16).astype(f32)
        c_lo = (c - c_hi - c_mid).astype(bf16).astype(f32)
        c3 = jnp.where((arow == 0) | (arow == 3), c_hi, jnp.where((arow == 1) | (arow == 4), c_mid, c_lo))
        alibi_rows.append(jnp.where(arow >= 6, 0.0, c3))
        pos_rows = jnp.where(arow >= 6, 0.0, jnp.where(arow >= 3, KEY_LOW * c3, c3))
        pos_rows = jnp.concatenate([pos_rows, jnp.zeros_like(pos_rows)], axis=0).astype(bf16)
        q_pos = jnp.concatenate([qt, pos_rows, jnp.zeros((LANES - HEAD_DIM - AUG_ROWS, cols), bf16)], axis=0)
        kc = jnp.concatenate([comp_ref[0, g].astype(bf16), caug_ref[...]], axis=1)
        s_cmp.append(_dot(kc, q_pos))
        kw = kw_ref[0, g, pl.ds(w0, WIN_KEYS), :]
        s_win.append(_dot(kw, q_pos))

    oc_t, ps_t = [], []
    seen = c_end <= tq
    any_seen = tq >= CMP_BLOCK - 1
    for g in heads:
        s_c = jnp.where(seen, s_cmp[g], NEG)
        e = jnp.exp2(s_c - jnp.max(s_c, axis=0, keepdims=True))
        rl = jnp.where(any_seen, 1.0 / jnp.maximum(jnp.sum(e, axis=0, keepdims=True), 1e-30), 0.0)
        p_c = e * rl
        oc_t.append(_dot_tn(comp_ref[0, NSA_KV_HEADS + g].astype(bf16), p_c.astype(bf16)))
        imp_t = (p_c[:, 0:Q_BLOCK] + p_c[:, Q_BLOCK:2 * Q_BLOCK]
                 + p_c[:, 2 * Q_BLOCK:3 * Q_BLOCK] + p_c[:, 3 * Q_BLOCK:4 * Q_BLOCK])
        hi = imp_t.astype(bf16)
        lo = (imp_t - hi.astype(f32)).astype(bf16)
        ps_t.append(_dot(mt_ref[...], hi) + _dot(mt_ref[...], lo))

    ow_t = []
    t0 = w0 // Q_BLOCK
    for g in heads:
        s_w = jnp.where(in_window, s_win[g], NEG)
        e = jnp.exp2(s_w - jnp.max(s_w, axis=0, keepdims=True)).astype(bf16)
        acc_w = jnp.zeros((VT_ROWS, cols), f32)
        for t in range(WIN_KEYS // Q_BLOCK):
            acc_w = acc_w + _dot(vtw_ref[0, g, t0 + t], e[t * Q_BLOCK:(t + 1) * Q_BLOCK])
        ow_t.append(acc_w[0:HEAD_DIM] * (1.0 / jnp.maximum(acc_w[HEAD_DIM:HEAD_DIM + 1], 1e-30)))

    sel_t = [jnp.where(forced, 1.0, 0.0)] * NSA_KV_HEADS
    score = [jnp.where(valid & jnp.logical_not(forced), ps_t[g], NEG) for g in heads]
    for _ in range(N_SEL - N_FORCED):
        for g in heads:
            best = jnp.max(score[g], axis=0, keepdims=True)
            idx = jnp.min(jnp.where(score[g] == best, blk, N_SELBLK), axis=0, keepdims=True)
            hit = blk == idx
            sel_t[g] = jnp.where(hit, 1.0, sel_t[g])
            score[g] = jnp.where(hit, REMOVED, score[g])
    sel_t = [jnp.where(valid, s, 0.0) for s in sel_t]

    for g in heads:
        bias_ref[g] = (sel_t[g] - 1.0) * MASK_BIG
        for slot in range(qa_ref.shape[1]):
            qa_ref[g, slot, 0:HEAD_DIM, :] = qts[g]
            qa_ref[g, slot, HEAD_DIM + AUG_ROWS:, :] = jnp.zeros((LANES - HEAD_DIM - AUG_ROWS, cols), bf16)
        acc_ref[g] = jnp.zeros(acc_ref.shape[1:], f32)
        mg_ref[g] = jnp.full(mg_ref.shape[1:], NEG, f32)
    own = [chunk_scores(g, last) for g in heads]
    for g in heads:
        sel_chunk(g, last, True, own[g])

    n_need = jnp.int32(0)
    for ck in range(n_chunk_max):
        picked = jnp.maximum(*[jnp.max(sel_t[g][ck * blk_per_chunk:(ck + 1) * blk_per_chunk, :]) for g in heads])
        need_ref[n_need] = ck
        n_need = n_need + ((picked > 0.5) & (ck < last)).astype(jnp.int32)

    def chunk_pair(i, carry):
        first = 2 * i
        live = first + 1 < n_need
        cks = [need_ref[first], need_ref[jnp.minimum(first + 1, n_need - 1)]]
        scores = [[chunk_scores(g, cks[slot], slot, live if slot else None) for g in heads] for slot in range(2)]
        for slot in range(2):
            for g in heads:
                sel_chunk(g, cks[slot], False, scores[slot][g])
        return carry

    lax.fori_loop(0, (n_need + 1) // 2, chunk_pair, 0)

    for g in heads:
        acc = acc_ref[g]
        os_t = acc[0:HEAD_DIM] * (1.0 / jnp.maximum(acc[HEAD_DIM:HEAD_DIM + 1], 1e-30))
        outs = []
        for r in range(NSA_GROUP):
            cs = slice(r * Q_BLOCK, (r + 1) * Q_BLOCK)
            row = (g * NSA_GROUP + r) * 3
            outs.append(gate_t[row:row + 1, :] * oc_t[g][:, cs] + gate_t[row + 1:row + 2, :] * os_t[:, cs]
                        + gate_t[row + 2:row + 3, :] * ow_t[g][:, cs])
        for pair in range(NSA_GROUP // 2):
            both = jnp.concatenate([outs[2 * pair], outs[2 * pair + 1]], axis=0)
            lane0 = (g * NSA_GROUP // 2 + pair) * LANES
            o_ref[:, lane0:lane0 + LANES] = both.T


def _nsa_prompt_pair(q_t, gates, comp, k_slc, vt_slc, k_win, vt_win, m_t):
    bsz, _, nq = q_t.shape[:3]
    seq = k_slc.shape[2]
    cols = NSA_GROUP * Q_BLOCK
    per_b4 = lambda b, i: (b, 0, 0, 0)
    per_b5 = lambda b, i: (b, 0, 0, 0, 0)
    return pl.pallas_call(
        _nsa_prompt_pair_kernel,
        grid=(bsz, nq),
        in_specs=[pl.BlockSpec((1, NSA_KV_HEADS, 1, HEAD_DIM, cols), lambda b, i: (b, 0, i, 0, 0)),
                  pl.BlockSpec((Q_BLOCK, LANES), lambda b, i: (b * nq + i, 0)),
                  pl.BlockSpec((1, N_COMBO, N_CMP_PAD, HEAD_DIM), per_b4),
                  pl.BlockSpec((N_CMP_PAD, HEAD_DIM), lambda b, i: (0, 0)),
                  pl.BlockSpec((1, NSA_KV_HEADS, seq, LANES), per_b4),
                  pl.BlockSpec((1,) + vt_slc.shape[1:], per_b5),
                  pl.BlockSpec((1, NSA_KV_HEADS, seq, LANES), per_b4),
                  pl.BlockSpec((1,) + vt_win.shape[1:], per_b5),
                  pl.BlockSpec((N_SELBLK, N_CMP_PAD), lambda b, i: (0, 0))],
        out_specs=pl.BlockSpec((Q_BLOCK, NSA_WIDTH), lambda b, i: (b * nq + i, 0)),
        out_shape=jax.ShapeDtypeStruct((bsz * seq, NSA_WIDTH), f32),
        scratch_shapes=[pltpu.VMEM((NSA_KV_HEADS, 2, LANES, cols), bf16),
                        pltpu.VMEM((NSA_KV_HEADS, VT_ROWS, cols), f32),
                        pltpu.VMEM((NSA_KV_HEADS, 1, cols), f32),
                        pltpu.VMEM((NSA_KV_HEADS, N_SELBLK, Q_BLOCK), f32),
                        pltpu.SMEM((vt_slc.shape[2],), jnp.int32)],
        compiler_params=_cparams(("parallel", "arbitrary")),
        name="nsa_prompt",
    )(q_t, gates, comp, _cmp_position_columns(), k_slc, vt_slc, k_win, vt_win, m_t)


NSA_STEP_BLOCKS = 2


def _nsa_prompt_blocks_kernel(qt_ref, gate_ref, comp_ref, caug_ref, ks_ref, vts_ref, kw_ref, vtw_ref, mt_ref, o_ref,
                              qa_ref, acc_ref, mg_ref, bias_ref, need_ref):
    cols = NSA_GROUP * Q_BLOCK
    n_blocks = qt_ref.shape[2]
    blocks = range(n_blocks)
    heads = range(NSA_KV_HEADS)
    insts = [(qb, g) for qb in blocks for g in heads]
    blk_per_chunk = SEL_CHUNK // SEL_BLOCK
    n_chunk_max = vts_ref.shape[2]

    lane = lax.broadcasted_iota(jnp.int32, (1, cols), 1)
    head = jnp.right_shift(lane, Q_BLOCK.bit_length() - 1)
    blk = lax.broadcasted_iota(jnp.int32, (N_SELBLK, Q_BLOCK), 0)
    qcol = lax.broadcasted_iota(jnp.int32, (N_SELBLK, Q_BLOCK), 1)
    c_end = lax.broadcasted_iota(jnp.int32, (N_CMP_PAD, 1), 0) * CMP_STRIDE + (CMP_BLOCK - 1)
    krow = lax.broadcasted_iota(jnp.int32, (SEL_CHUNK, 1), 0)
    wrow = lax.broadcasted_iota(jnp.int32, (WIN_KEYS, 1), 0)
    arow = lax.broadcasted_iota(jnp.int32, (AUG_ROWS // 2, cols), 0)
    gate_all = _sigmoid(gate_ref[...])

    q0 = [(pl.program_id(1) * n_blocks + qb) * Q_BLOCK for qb in blocks]
    tq = [q0[qb] + jnp.bitwise_and(lane, Q_BLOCK - 1) for qb in blocks]
    last = [q0[qb] // SEL_CHUNK for qb in blocks]
    w0 = [pl.multiple_of(jnp.maximum(q0[qb] - WINDOW, 0), Q_BLOCK) for qb in blocks]
    gate_t = [gate_all[qb * Q_BLOCK:(qb + 1) * Q_BLOCK].T for qb in blocks]

    slope, alibi_rows, pos_rows = [], [], []
    for g in heads:
        c = jnp.exp2(-(head.astype(f32) + 1.0 + NSA_GROUP * g)) * LOG2E
        c_hi = c.astype(bf16).astype(f32)
        c_mid = (c - c_hi).astype(bf16).astype(f32)
        c_lo = (c - c_hi - c_mid).astype(bf16).astype(f32)
        c3 = jnp.where((arow == 0) | (arow == 3), c_hi, jnp.where((arow == 1) | (arow == 4), c_mid, c_lo))
        slope.append(c)
        alibi_rows.append(jnp.where(arow >= 6, 0.0, c3))
        rows = jnp.where(arow >= 6, 0.0, jnp.where(arow >= 3, KEY_LOW * c3, c3))
        pos_rows.append(jnp.concatenate([rows, jnp.zeros_like(rows)], axis=0).astype(bf16))

    def chunk_scores(qb, g, ck, slot=0, live=None):
        k0 = pl.multiple_of(ck * SEL_CHUNK, SEL_CHUNK)
        b8 = bias_ref[qb, g, pl.ds(pl.multiple_of(ck * blk_per_chunk, blk_per_chunk), blk_per_chunk), :]
        if live is not None:
            b8 = jnp.where(live, b8, -MASK_BIG)
        qa_ref[qb, g, slot, HEAD_DIM:HEAD_DIM + AUG_ROWS, :] = jnp.concatenate(
            [jnp.concatenate([b8] * NSA_GROUP, axis=1), alibi_rows[g]], axis=0).astype(bf16)
        return _dot(ks_ref[0, g, pl.ds(k0, SEL_CHUNK), :], qa_ref[qb, g, slot])

    def sel_chunk(qb, g, ck, causal, s):
        k0 = pl.multiple_of(ck * SEL_CHUNK, SEL_CHUNK)
        if causal:
            s = jnp.where(k0 + krow <= tq[qb], s, NEG)
        shift = slope[g] * k0.astype(f32)
        m_old = mg_ref[qb, g] - shift
        m_new = jnp.maximum(m_old, jnp.max(s, axis=0, keepdims=True))
        p = jnp.exp2(s - m_new).astype(bf16)
        acc_ref[qb, g] = acc_ref[qb, g] * jnp.exp2(m_old - m_new) + _dot(vts_ref[0, g, ck], p)
        mg_ref[qb, g] = m_new + shift

    s_cmp, s_win = {}, {}
    for qb, g in insts:
        q_pos = jnp.concatenate([qt_ref[0, g, qb], pos_rows[g],
                                 jnp.zeros((LANES - HEAD_DIM - AUG_ROWS, cols), bf16)], axis=0)
        kc = jnp.concatenate([comp_ref[0, g].astype(bf16), caug_ref[...]], axis=1)
        s_cmp[qb, g] = _dot(kc, q_pos)
        kw = kw_ref[0, g, pl.ds(w0[qb], WIN_KEYS), :]
        s_win[qb, g] = _dot(kw, q_pos)

    oc_t, ps_t = {}, {}
    for qb in blocks:
        seen = c_end <= tq[qb]
        any_seen = tq[qb] >= CMP_BLOCK - 1
        for g in heads:
            s_c = jnp.where(seen, s_cmp[qb, g], NEG)
            e = jnp.exp2(s_c - jnp.max(s_c, axis=0, keepdims=True))
            rl = jnp.where(any_seen, 1.0 / jnp.maximum(jnp.sum(e, axis=0, keepdims=True), 1e-30), 0.0)
            p_c = e * rl
            oc_t[qb, g] = _dot_tn(comp_ref[0, NSA_KV_HEADS + g].astype(bf16), p_c.astype(bf16))
            imp_t = (p_c[:, 0:Q_BLOCK] + p_c[:, Q_BLOCK:2 * Q_BLOCK]
                     + p_c[:, 2 * Q_BLOCK:3 * Q_BLOCK] + p_c[:, 3 * Q_BLOCK:4 * Q_BLOCK])
            hi = imp_t.astype(bf16)
            lo = (imp_t - hi.astype(f32)).astype(bf16)
            ps_t[qb, g] = _dot(mt_ref[...], hi) + _dot(mt_ref[...], lo)

    ow_t = {}
    for qb in blocks:
        wkey = w0[qb] + wrow
        in_window = (wkey <= tq[qb]) & (wkey >= tq[qb] - WINDOW)
        t0 = w0[qb] // Q_BLOCK
        for g in heads:
            s_w = jnp.where(in_window, s_win[qb, g], NEG)
            e = jnp.exp2(s_w - jnp.max(s_w, axis=0, keepdims=True)).astype(bf16)
            acc_w = jnp.zeros((VT_ROWS, cols), f32)
            for t in range(WIN_KEYS // Q_BLOCK):
                acc_w = acc_w + _dot(vtw_ref[0, g, t0 + t], e[t * Q_BLOCK:(t + 1) * Q_BLOCK])
            ow_t[qb, g] = acc_w[0:HEAD_DIM] * (1.0 / jnp.maximum(acc_w[HEAD_DIM:HEAD_DIM + 1], 1e-30))

    valid, sel_t, score = {}, {}, {}
    for qb in blocks:
        jt = jnp.right_shift(q0[qb] + qcol, SEL_BLOCK.bit_length() - 1)
        valid[qb] = blk <= jt
        forced = (blk == 0) | (blk == jt) | (blk == jt - 1)
        for g in heads:
            sel_t[qb, g] = jnp.where(forced, 1.0, 0.0)
            score[qb, g] = jnp.where(valid[qb] & jnp.logical_not(forced), ps_t[qb, g], NEG)
    for _ in range(N_SEL - N_FORCED):
        for inst in insts:
            best = jnp.max(score[inst], axis=0, keepdims=True)
            idx = jnp.min(jnp.where(score[inst] == best, blk, N_SELBLK), axis=0, keepdims=True)
            hit = blk == idx
            sel_t[inst] = jnp.where(hit, 1.0, sel_t[inst])
            score[inst] = jnp.where(hit, REMOVED, score[inst])
    for qb, g in insts:
        sel_t[qb, g] = jnp.where(valid[qb], sel_t[qb, g], 0.0)

    for qb, g in insts:
        bias_ref[qb, g] = (sel_t[qb, g] - 1.0) * MASK_BIG
        for slot in range(qa_ref.shape[2]):
            qa_ref[qb, g, slot, 0:HEAD_DIM, :] = qt_ref[0, g, qb]
            qa_ref[qb, g, slot, HEAD_DIM + AUG_ROWS:, :] = jnp.zeros((LANES - HEAD_DIM - AUG_ROWS, cols), bf16)
        acc_ref[qb, g] = jnp.zeros(acc_ref.shape[2:], f32)
        mg_ref[qb, g] = jnp.full(mg_ref.shape[2:], NEG, f32)
    own = {(qb, g): chunk_scores(qb, g, last[qb]) for qb, g in insts}
    for qb, g in insts:
        sel_chunk(qb, g, last[qb], True, own[qb, g])

    n_need = []
    for qb in blocks:
        count = jnp.int32(0)
        for ck in range(n_chunk_max):
            rows = slice(ck * blk_per_chunk, (ck + 1) * blk_per_chunk)
            picked = jnp.maximum(*[jnp.max(sel_t[qb, g][rows, :]) for g in heads])
            need_ref[qb, count] = ck
            count = count + ((picked > 0.5) & (ck < last[qb])).astype(jnp.int32)
        n_need.append(count)

    for qb in blocks:
        def chunk_pair(i, carry, qb=qb):
            first = 2 * i
            live = first + 1 < n_need[qb]
            cks = [need_ref[qb, first], need_ref[qb, jnp.minimum(first + 1, n_need[qb] - 1)]]
            scores = [[chunk_scores(qb, g, cks[slot], slot, live if slot else None) for g in heads]
                      for slot in range(2)]
            for slot in range(2):
                for g in heads:
                    sel_chunk(qb, g, cks[slot], False, scores[slot][g])
            return carry

        lax.fori_loop(0, (n_need[qb] + 1) // 2, chunk_pair, 0)

    for qb, g in insts:
        acc = acc_ref[qb, g]
        os_t = acc[0:HEAD_DIM] * (1.0 / jnp.maximum(acc[HEAD_DIM:HEAD_DIM + 1], 1e-30))
        outs = []
        for r in range(NSA_GROUP):
            cs = slice(r * Q_BLOCK, (r + 1) * Q_BLOCK)
            row = (g * NSA_GROUP + r) * 3
            outs.append(gate_t[qb][row:row + 1, :] * oc_t[qb, g][:, cs] + gate_t[qb][row + 1:row + 2, :] * os_t[:, cs]
                        + gate_t[qb][row + 2:row + 3, :] * ow_t[qb, g][:, cs])
        for pair in range(NSA_GROUP // 2):
            both = jnp.concatenate([outs[2 * pair], outs[2 * pair + 1]], axis=0)
            lane0 = (g * NSA_GROUP // 2 + pair) * LANES
            o_ref[qb * Q_BLOCK:(qb + 1) * Q_BLOCK, lane0:lane0 + LANES] = both.T


def _nsa_prompt_blocks(q_t, gates, comp, k_slc, vt_slc, k_win, vt_win, m_t):
    bsz, _, nq = q_t.shape[:3]
    seq = k_slc.shape[2]
    cols = NSA_GROUP * Q_BLOCK
    nblk = NSA_STEP_BLOCKS
    steps = nq // nblk
    per_b4 = lambda b, i: (b, 0, 0, 0)
    per_b5 = lambda b, i: (b, 0, 0, 0, 0)
    inst = (nblk, NSA_KV_HEADS)
    return pl.pallas_call(
        _nsa_prompt_blocks_kernel,
        grid=(bsz, steps),
        in_specs=[pl.BlockSpec((1, NSA_KV_HEADS, nblk, HEAD_DIM, cols), lambda b, i: (b, 0, i, 0, 0)),
                  pl.BlockSpec((nblk * Q_BLOCK, LANES), lambda b, i: (b * steps + i, 0)),
                  pl.BlockSpec((1, N_COMBO, N_CMP_PAD, HEAD_DIM), per_b4),
                  pl.BlockSpec((N_CMP_PAD, HEAD_DIM), lambda b, i: (0, 0)),
                  pl.BlockSpec((1, NSA_KV_HEADS, seq, LANES), per_b4),
                  pl.BlockSpec((1,) + vt_slc.shape[1:], per_b5),
                  pl.BlockSpec((1, NSA_KV_HEADS, seq, LANES), per_b4),
                  pl.BlockSpec((1,) + vt_win.shape[1:], per_b5),
                  pl.BlockSpec((N_SELBLK, N_CMP_PAD), lambda b, i: (0, 0))],
        out_specs=pl.BlockSpec((nblk * Q_BLOCK, NSA_WIDTH), lambda b, i: (b * steps + i, 0)),
        out_shape=jax.ShapeDtypeStruct((bsz * seq, NSA_WIDTH), f32),
        scratch_shapes=[pltpu.VMEM(inst + (2, LANES, cols), bf16),
                        pltpu.VMEM(inst + (VT_ROWS, cols), f32),
                        pltpu.VMEM(inst + (1, cols), f32),
                        pltpu.VMEM(inst + (N_SELBLK, Q_BLOCK), f32),
                        pltpu.SMEM((nblk, vt_slc.shape[2]), jnp.int32)],
        compiler_params=_cparams(("parallel", "arbitrary")),
        name="nsa_prompt",
    )(q_t, gates, comp, _cmp_position_columns(), k_slc, vt_slc, k_win, vt_win, m_t)


def _position_columns(pos):
    low, high = (pos % KEY_LOW)[:, None], (pos // KEY_LOW)[:, None]
    return np.concatenate([low, low, low, high, high, high], axis=1).astype(np.float32)


def _cmp_position_columns():
    aug = np.zeros((N_CMP_PAD, HEAD_DIM), np.float32)
    aug[:, 0:6] = _position_columns(np.arange(N_CMP_PAD) * CMP_STRIDE + CMP_BLOCK - 1)
    return jnp.asarray(aug, dtype=bf16)


LAYOUT_ROWS = 1024


def _attention_layout_kernel(q_ref, cmp_ref, slc_ref, win_ref, xs_ref, xw_ref,
                             qt_ref, ks_ref, vts_ref, kw_ref, vtw_ref, cmp_t_ref, slc_t_ref, win_t_ref):
    cmp_t_ref[0] = cmp_ref[...].T
    tm = q_ref.shape[0]
    q = q_ref[...] * (HEAD_DIM ** -0.5 * LOG2E)
    for qb in range(tm // Q_BLOCK):
        rows = slice(qb * Q_BLOCK, (qb + 1) * Q_BLOCK)
        for pair in range(NSA_HEADS // 2):
            t = q[rows, pair * LANES:(pair + 1) * LANES].T
            for k in range(2):
                g, r = divmod(2 * pair + k, NSA_GROUP)
                qt_ref[0, g, qb, :, r * Q_BLOCK:(r + 1) * Q_BLOCK] = t[k * HEAD_DIM:(k + 1) * HEAD_DIM].astype(bf16)

    for src_ref, extra_ref, k_ref, vt_ref, t_ref in ((slc_ref, xs_ref, ks_ref, vts_ref, slc_t_ref),
                                                     (win_ref, xw_ref, kw_ref, vtw_ref, win_t_ref)):
        tile = vt_ref.shape[4]
        s = src_ref[...]
        s_t = s.T
        t_ref[0] = s_t
        v_t = s_t[KV_WIDTH:]
        tail = jnp.where(lax.broadcasted_iota(jnp.int32, (VT_ROWS - HEAD_DIM, tile), 0) == 0, 1.0, 0.0)
        for g in range(NSA_KV_HEADS):
            k_ref[0, g] = jnp.concatenate([s[:, g * HEAD_DIM:(g + 1) * HEAD_DIM].astype(bf16), extra_ref[...]], axis=1)
            for c in range(tm // tile):
                blk = v_t[g * HEAD_DIM:(g + 1) * HEAD_DIM, c * tile:(c + 1) * tile]
                vt_ref[0, g, c] = jnp.concatenate([blk, tail], axis=0).astype(bf16)


def _attention_layout(z, cmp, slc, win, bsz, seq):
    tm = LAYOUT_ROWS
    per_b = seq // tm
    pos = np.arange(seq)
    xs = np.zeros((seq, HEAD_DIM), np.float32)
    xs[pos, (pos // SEL_BLOCK) % (SEL_CHUNK // SEL_BLOCK)] = 1.0
    xs[:, 8:11] = (pos % KEY_LOW)[:, None]
    xs[:, 11:14] = (pos % SEL_CHUNK - pos % KEY_LOW)[:, None]
    xw = np.zeros((seq, HEAD_DIM), np.float32)
    xw[:, 0:6] = _position_columns(pos)
    row = lambda i: (i, 0)
    rep = lambda i: (i % per_b, 0)
    out4 = lambda i: (i // per_b, 0, i % per_b, 0)
    out5 = lambda i: (i // per_b, 0, i % per_b, 0, 0)
    k_shape = jax.ShapeDtypeStruct((bsz, NSA_KV_HEADS, seq, LANES), bf16)
    vt_shape = lambda tile: jax.ShapeDtypeStruct((bsz, NSA_KV_HEADS, seq // tile, VT_ROWS, tile), bf16)
    vt_spec = lambda tile: pl.BlockSpec((1, NSA_KV_HEADS, tm // tile, VT_ROWS, tile), out5)
    rows_t_shape = jax.ShapeDtypeStruct((bsz, ROW_WIDTH, seq), f32)
    rows_t_spec = pl.BlockSpec((1, ROW_WIDTH, tm), lambda i: (i // per_b, 0, i % per_b))
    return pl.pallas_call(
        _attention_layout_kernel,
        grid=(bsz * per_b,),
        in_specs=[pl.BlockSpec((tm, NSA_WIDTH), row), pl.BlockSpec((tm, ROW_WIDTH), row),
                  pl.BlockSpec((tm, ROW_WIDTH), row), pl.BlockSpec((tm, ROW_WIDTH), row),
                  pl.BlockSpec((tm, HEAD_DIM), rep), pl.BlockSpec((tm, HEAD_DIM), rep)],
        out_specs=[pl.BlockSpec((1, NSA_KV_HEADS, tm // Q_BLOCK, HEAD_DIM, NSA_GROUP * Q_BLOCK), out5),
                   pl.BlockSpec((1, NSA_KV_HEADS, tm, LANES), out4), vt_spec(SEL_CHUNK),
                   pl.BlockSpec((1, NSA_KV_HEADS, tm, LANES), out4), vt_spec(Q_BLOCK),
                   rows_t_spec, rows_t_spec, rows_t_spec],
        out_shape=[jax.ShapeDtypeStruct((bsz, NSA_KV_HEADS, seq // Q_BLOCK, HEAD_DIM, NSA_GROUP * Q_BLOCK), bf16),
                   k_shape, vt_shape(SEL_CHUNK), k_shape, vt_shape(Q_BLOCK),
                   rows_t_shape, rows_t_shape, rows_t_shape],
        compiler_params=_cparams(("parallel",)),
        name="attention_layout",
    )(z, cmp, slc, win, jnp.asarray(xs, dtype=bf16), jnp.asarray(xw, dtype=bf16))


HG_CHUNK = 128
HG_STEP_CHUNKS = 4
HG_BAND = 8


def _lower_bound(lo):
    m = jnp.max(lo, axis=0, keepdims=True)
    e = jnp.exp(lo - m)
    return e[0:1] / jnp.sum(e, axis=0, keepdims=True)


def _hgrn_gates(q, fl, lb):
    sg = _sigmoid(fl)
    f = lb + (1.0 - lb) * sg
    k = (1.0 - lb) * (1.0 - sg)
    return _silu(q), k, jnp.log(f)


def _hgrn_out(o, nw, zb):
    o = o * lax.rsqrt(jnp.mean(o * o, axis=-1, keepdims=True) + EPS) * nw
    return o * _silu(zb)


def _hgrn_prompt_kernel(q_ref, f_ref, i_ref, zb_ref, lo_ref, nw_ref, y_ref, s_ref, st_ref):
    c = pl.program_id(1)

    @pl.when(c == 0)
    def _():
        st_ref[...] = jnp.zeros_like(st_ref)

    n = HG_CHUNK
    row = lax.broadcasted_iota(jnp.int32, (n, n), 0)
    col = lax.broadcasted_iota(jnp.int32, (n, n), 1)
    heads = [slice(h * HG_DIM, (h + 1) * HG_DIM) for h in range(HG_HEADS)]

    qq, kk, logf = _hgrn_gates(q_ref[...], f_ref[...], _lower_bound(lo_ref[...]))
    v = i_ref[...]
    trow = lax.broadcasted_iota(jnp.int32, (n, 1), 0)
    chunks = [slice(i * n, (i + 1) * n) for i in range(q_ref.shape[0] // n)]
    bs = []
    for rs in chunks:
        b = logf[rs]
        step = 1
        while step < n:
            b = b + jnp.where(trow >= step, pltpu.roll(b, step, axis=0), 0.0)
            step *= 2
        bs.append(b)

    intra = [[_hgrn_intra(qq[rs, hs], kk[rs, hs], b[:, hs], row, col) for hs in heads] for rs, b in zip(chunks, bs)]
    for i, rs in enumerate(chunks):
        for h, hs in enumerate(heads):
            st = st_ref[h]
            bh, kh, vh = bs[i][:, hs], kk[rs, hs], v[rs, hs].astype(bf16)
            o = _dot(intra[i][h].astype(bf16), vh) + _dot_nt((qq[rs, hs] * jnp.exp(bh)).astype(bf16), st.astype(bf16))
            b_last = bh[n - 1:n]
            st_new = st * jnp.exp(b_last) + _dot_tn(vh, (kh * jnp.exp(b_last - bh)).astype(bf16))
            st_ref[h] = st_new
            y_ref[rs, hs] = _hgrn_out(o, nw_ref[...], zb_ref[rs, hs])

    @pl.when(c == pl.num_programs(1) - 1)
    def _():
        for h in range(HG_HEADS):
            s_ref[0, h] = st_ref[h].T


def _hgrn_intra(qq, kk, b, row, col):
    n = HG_CHUNK
    a = jnp.where(row == col, _dot_nt(qq.astype(bf16), kk.astype(bf16)), 0.0)

    sub = jnp.bitwise_and(row, HG_BAND - 1)
    b3 = b.reshape(n // HG_BAND, HG_BAND, HG_DIM)
    size = 2
    while size <= HG_BAND:
        half = size // 2
        beta = None
        for s0 in range(0, HG_BAND, size):
            piece = jnp.broadcast_to(b3[:, s0 + half - 1:s0 + half, :], b3.shape).reshape(n, HG_DIM)
            beta = piece if beta is None else jnp.where(sub >= s0, piece, beta)
        left = jnp.bitwise_and(sub, size - 1) < half
        decay = jnp.exp(jnp.where(left, beta - b, b - beta))
        part = _dot_nt(jnp.where(left, 0.0, qq * decay).astype(bf16), jnp.where(left, kk * decay, 0.0).astype(bf16))
        sh = size.bit_length() - 1
        a = a + jnp.where(jnp.right_shift(row, sh) == jnp.right_shift(col, sh), part, 0.0)
        size *= 2

    while size <= n:
        half = size // 2
        qs, ks = [], []
        zero = jnp.zeros((half, HG_DIM), f32)
        for r0 in range(0, n, size):
            beta = b[r0 + half - 1:r0 + half]
            ks += [kk[r0:r0 + half] * jnp.exp(beta - b[r0:r0 + half]), zero]
            qs += [zero, qq[r0 + half:r0 + size] * jnp.exp(b[r0 + half:r0 + size] - beta)]
        part = _dot_nt(jnp.concatenate(qs, axis=0).astype(bf16), jnp.concatenate(ks, axis=0).astype(bf16))
        if size < n:
            sh = size.bit_length() - 1
            part = jnp.where(jnp.right_shift(row, sh) == jnp.right_shift(col, sh), part, 0.0)
        a = a + part
        size *= 2
    return a


def _hgrn_prompt(z, hg_lower, hg_norm, bsz, seq):
    rows = HG_STEP_CHUNKS * HG_CHUNK
    nch = seq // rows

    def zcol(base):
        return pl.BlockSpec((rows, HG_WIDTH), lambda b, c: (b * nch + c, base // HG_WIDTH))

    return pl.pallas_call(
        _hgrn_prompt_kernel,
        grid=(bsz, nch),
        in_specs=[zcol(COL_QB), zcol(COL_FB), zcol(COL_IB), zcol(COL_ZB),
                  pl.BlockSpec(hg_lower.shape, lambda b, c: (0, 0)),
                  pl.BlockSpec((1, HG_DIM), lambda b, c: (0, 0))],
        out_specs=[pl.BlockSpec((rows, HG_WIDTH), lambda b, c: (b * nch + c, 0)),
                   pl.BlockSpec((1, HG_HEADS, HG_DIM, HG_DIM), lambda b, c: (b, 0, 0, 0))],
        out_shape=[jax.ShapeDtypeStruct((bsz * seq, HG_WIDTH), f32),
                   jax.ShapeDtypeStruct((bsz, HG_HEADS, HG_DIM, HG_DIM), f32)],
        scratch_shapes=[pltpu.VMEM((HG_HEADS, HG_DIM, HG_DIM), f32)],
        compiler_params=_cparams(("parallel", "arbitrary")),
        name="hgrn_prompt",
    )(z, z, z, z, hg_lower, hg_norm)


def _hgrn_sample_kernel(q_ref, f_ref, i_ref, zb_ref, lo_ref, nw_ref, s_ref, y_ref, so_ref, o_scr):
    nb = q_ref.shape[0]
    lb = _lower_bound(lo_ref[...])
    qq, kk, logf = _hgrn_gates(q_ref[...], f_ref[...], lb)
    v = i_ref[...]
    pad = jnp.zeros((HG_DIM - nb, HG_DIM), f32)
    to_cols = lambda x: jnp.concatenate([x, pad], axis=0).T
    f_t, k_t, q_t = to_cols(jnp.exp(logf)), to_cols(kk), to_cols(qq)
    for b in range(nb):
        s_new = f_t[:, b:b + 1] * s_ref[0, b, 0] + k_t[:, b:b + 1] * v[b:b + 1, :]
        so_ref[0, b, 0] = s_new
        o_scr[b:b + 1, :] = jnp.sum(s_new * q_t[:, b:b + 1], axis=0, keepdims=True)
    y_ref[...] = _hgrn_out(o_scr[...], nw_ref[...], zb_ref[...])


def _hgrn_sample(z, state, hg_lower, hg_norm):
    nb = z.shape[0]

    def zcol(base):
        return pl.BlockSpec((nb, HG_DIM), lambda h: (0, base // HG_DIM + h))

    sblk = pl.BlockSpec((1, nb, 1, HG_DIM, HG_DIM), lambda h: (0, 0, h, 0, 0))
    return pl.pallas_call(
        _hgrn_sample_kernel,
        grid=(HG_HEADS,),
        in_specs=[zcol(COL_QB), zcol(COL_FB), zcol(COL_IB), zcol(COL_ZB),
                  pl.BlockSpec((hg_lower.shape[0], HG_DIM), lambda h: (0, h)),
                  pl.BlockSpec((1, HG_DIM), lambda h: (0, 0)),
                  sblk],
        out_specs=[pl.BlockSpec((nb, HG_DIM), lambda h: (0, h)), sblk],
        out_shape=[jax.ShapeDtypeStruct((nb, HG_WIDTH), f32),
                   jax.ShapeDtypeStruct(state.shape, f32)],
        scratch_shapes=[pltpu.VMEM((nb, HG_DIM), f32)],
        compiler_params=_cparams(("parallel",)),
        name="hgrn_sample",
    )(z, z, z, z, hg_lower, hg_norm, state)


N_PICK = N_SEL - 1


def _head_slopes():
    h = lax.broadcasted_iota(jnp.int32, (NSA_HEADS, 1), 0).astype(f32)
    return jnp.exp2(-(h + 1.0))


def _nsa_sample_cmp_kernel(q_ref, comp_ref, m_ref, oc_ref, ps_ref, *, t_pos):
    q = q_ref[0] * jnp.asarray(HEAD_DIM ** -0.5, bf16)
    slope = _head_slopes()
    head = lax.broadcasted_iota(jnp.int32, (NSA_HEADS, 1), 0)
    ncol = lax.broadcasted_iota(jnp.int32, (1, N_CMP_PAD), 1)
    d_c = t_pos - (ncol * CMP_STRIDE + (CMP_BLOCK - 1))
    mask = d_c >= 0
    o_c = jnp.zeros((NSA_HEADS, HEAD_DIM), f32)
    ps_ref[...] = jnp.zeros_like(ps_ref)
    for g in range(NSA_KV_HEADS):
        kc = comp_ref[0, g].astype(bf16)
        vc = comp_ref[0, NSA_KV_HEADS + g].astype(bf16)
        s = jnp.where(mask, _dot_nt(q, kc) - slope * d_c.astype(f32), NEG)
        m = jnp.max(s, axis=1, keepdims=True)
        e = jnp.where(mask, jnp.exp(s - m), 0.0)
        p = e * (1.0 / jnp.maximum(jnp.sum(e, axis=1, keepdims=True), 1e-30))
        mine = jnp.right_shift(head, NSA_GROUP.bit_length() - 1) == g
        o_c = jnp.where(mine, _dot(p.astype(bf16), vc), o_c)
        imp = jnp.sum(jnp.where(mine, p, 0.0), axis=0, keepdims=True)
        ps_ref[0, g:g + 1, :] = _importance_to_blocks(imp, m_ref[...])
    oc_ref[0] = o_c


def _nsa_sample_cmp(q8, comp, m_mat, t_pos):
    nb = q8.shape[0]
    return pl.pallas_call(
        functools.partial(_nsa_sample_cmp_kernel, t_pos=t_pos),
        grid=(nb,),
        in_specs=[pl.BlockSpec((1, NSA_HEADS, HEAD_DIM), lambda b: (b, 0, 0)),
                  pl.BlockSpec((1, N_COMBO, N_CMP_PAD, HEAD_DIM), lambda b: (b, 0, 0, 0)),
                  pl.BlockSpec((N_CMP_PAD, N_SELBLK), lambda b: (0, 0))],
        out_specs=[pl.BlockSpec((1, NSA_HEADS, HEAD_DIM), lambda b: (b, 0, 0)),
                   pl.BlockSpec((1, 8, N_SELBLK), lambda b: (b, 0, 0))],
        out_shape=[jax.ShapeDtypeStruct((nb, NSA_HEADS, HEAD_DIM), f32),
                   jax.ShapeDtypeStruct((nb, 8, N_SELBLK), f32)],
        compiler_params=_cparams(("parallel",)),
        name="nsa_sample_cmp",
    )(q8, comp, m_mat)


def _nsa_sample_topk_kernel(ps_ref, idx_ref, *, last_blk):
    ps = ps_ref[...]
    lane = lax.broadcasted_iota(jnp.int32, ps.shape, 1)
    score = ps + jnp.where((lane == 0) | (lane == last_blk), FORCE_BONUS, 0.0)
    out = jnp.zeros(ps.shape, jnp.int32)
    for r in range(N_PICK):
        best = jnp.max(score, axis=1, keepdims=True)
        idx = jnp.min(jnp.where(score == best, lane, N_SELBLK), axis=1, keepdims=True)
        out = jnp.where(lane == r, idx, out)
        score = jnp.where(lane == idx, REMOVED, score)
    idx_ref[...] = out


def _nsa_sample_topk(ps_rows, last_blk):
    return pl.pallas_call(
        functools.partial(_nsa_sample_topk_kernel, last_blk=last_blk),
        out_shape=jax.ShapeDtypeStruct(ps_rows.shape, jnp.int32),
        name="nsa_sample_topk",
    )(ps_rows)


def _nsa_sample_sel_kernel(pt_ref, idx_ref, slc_ref, q_ref, oc_ref, gate_ref, slc_new_ref, win_new_ref, wc_ref,
                           o_ref, wout_ref, kv_scr, sem_ref, *, t_pos):
    n_blk = NSA_KV_HEADS * N_PICK
    halves = PAGE_SIZE // SEL_BLOCK
    b = pl.program_id(0)
    nb = pl.num_programs(0)
    slot = lax.rem(b, 2)

    def picked(seq, k):
        return idx_ref[(seq * NSA_KV_HEADS + k // N_PICK) * N_SEL + k % N_PICK]

    def blk_copies(seq, s, k):
        page = pt_ref[seq, picked(seq, k) // halves]
        g = k // N_PICK
        return [pltpu.make_async_copy(slc_ref.at[page, kind * NSA_KV_HEADS + g], kv_scr.at[s, kind, k],
                                      sem_ref.at[s]) for kind in range(2)]

    def start_all(seq, s):
        for k in range(n_blk):
            for cp in blk_copies(seq, s, k):
                cp.start()

    @pl.when(b == 0)
    def _():
        start_all(0, 0)

    @pl.when(b + 1 < nb)
    def _():
        start_all(b + 1, 1 - slot)

    for k in range(n_blk):
        for cp in blk_copies(b, slot, k):
            cp.wait()

    q = q_ref[0] * jnp.asarray(HEAD_DIM ** -0.5, bf16)
    slope = _head_slopes()
    head = lax.broadcasted_iota(jnp.int32, (NSA_HEADS, 1), 0)
    lane = lax.broadcasted_iota(jnp.int32, (1, PAGE_SIZE), 1)
    new_slc = slc_new_ref[0]
    new_win = win_new_ref[0]
    wl = wc_ref.shape[3]
    wlane = lax.broadcasted_iota(jnp.int32, (1, wl), 1)
    d_w = (wl - wlane).astype(f32)
    qf = q.astype(f32)

    def attend(s, mask, v_t, new_row, g):
        k_new = new_row[:, g * HEAD_DIM:(g + 1) * HEAD_DIM].astype(bf16).astype(f32)
        v_new = new_row[:, KV_WIDTH + g * HEAD_DIM:KV_WIDTH + (g + 1) * HEAD_DIM].astype(bf16).astype(f32)
        s_new = jnp.sum(qf * k_new, axis=1, keepdims=True)
        m = jnp.maximum(jnp.max(s, axis=1, keepdims=True), s_new)
        e, e_new = jnp.exp(s - m), jnp.exp(s_new - m)
        if mask is not None:
            e = jnp.where(mask, e, 0.0)
        den = jnp.maximum(jnp.sum(e, axis=1, keepdims=True) + e_new, 1e-30)
        num = _dot_nt(e.astype(bf16), v_t.astype(bf16)) + e_new.astype(bf16).astype(f32) * v_new
        return num * (1.0 / den)

    o_s = jnp.zeros((NSA_HEADS, HEAD_DIM), f32)
    o_w = jnp.zeros((NSA_HEADS, HEAD_DIM), f32)
    for g in range(NSA_KV_HEADS):
        mine = jnp.right_shift(head, NSA_GROUP.bit_length() - 1) == g
        pos, member = [], []
        for n in range(N_PICK):
            j = picked(b, g * N_PICK + n)
            pos.append((j // halves) * PAGE_SIZE + lane)
            member.append(jnp.right_shift(lane, SEL_BLOCK.bit_length() - 1) == j % halves)
        pos = jnp.concatenate(pos, axis=1)
        member = jnp.concatenate(member, axis=1)
        k_t = jnp.concatenate([kv_scr[slot, 0, g * N_PICK + n] for n in range(N_PICK)], axis=1)
        v_t = jnp.concatenate([kv_scr[slot, 1, g * N_PICK + n] for n in range(N_PICK)], axis=1)
        s = jnp.where(member, _dot(q, k_t.astype(bf16)) - slope * (t_pos - pos).astype(f32), NEG)
        o_s = jnp.where(mine, attend(s, member, v_t, new_slc, g), o_s)
        s = _dot(q, wc_ref[0, g].astype(bf16)) - slope * d_w
        o_w = jnp.where(mine, attend(s, None, wc_ref[0, NSA_KV_HEADS + g], new_win, g), o_w)

    gate = jnp.broadcast_to(_sigmoid(gate_ref[0]), (NSA_HEADS, LANES))
    glane = lax.broadcasted_iota(jnp.int32, (NSA_HEADS, LANES), 1)
    gcol = [jnp.sum(jnp.where(glane == 3 * head + j, gate, 0.0), axis=1, keepdims=True) for j in range(3)]
    o_ref[0] = gcol[0] * oc_ref[0] + gcol[1] * o_s + gcol[2] * o_w
    eye = lax.broadcasted_iota(jnp.int32, (HEAD_DIM, HEAD_DIM), 0) == \
        lax.broadcasted_iota(jnp.int32, (HEAD_DIM, HEAD_DIM), 1)
    for c in range(N_COMBO):
        piece = jnp.broadcast_to(new_win[:, c * HEAD_DIM:(c + 1) * HEAD_DIM], (HEAD_DIM, HEAD_DIM))
        new_col = jnp.sum(jnp.where(eye, piece, 0.0), axis=1, keepdims=True)
        wout_ref[0, c] = jnp.where(wlane == wl - 1, new_col, pltpu.roll(wc_ref[0, c], wl - 1, axis=1))


def _nsa_sample_sel(page_table, idx_flat, slc_pages_t, q8, o_c, gates, slc_new, win_new, cache_win_t, t_pos):
    nb = q8.shape[0]
    wl = cache_win_t.shape[3]
    n_blk = NSA_KV_HEADS * N_PICK
    per_seq = lambda b, pt, idx: (b, 0, 0)
    win_blk = pl.BlockSpec((1, N_COMBO, HEAD_DIM, wl), lambda b, pt, idx: (b, 0, 0, 0))
    grid_spec = pltpu.PrefetchScalarGridSpec(
        num_scalar_prefetch=2,
        grid=(nb,),
        in_specs=[pl.BlockSpec(memory_space=pl.ANY),
                  pl.BlockSpec((1, NSA_HEADS, HEAD_DIM), per_seq),
                  pl.BlockSpec((1, NSA_HEADS, HEAD_DIM), per_seq),
                  pl.BlockSpec((1, 1, LANES), per_seq),
                  pl.BlockSpec((1, 1, ROW_WIDTH), per_seq),
                  pl.BlockSpec((1, 1, ROW_WIDTH), per_seq),
                  win_blk],
        out_specs=[pl.BlockSpec((1, NSA_HEADS, HEAD_DIM), per_seq), win_blk],
        scratch_shapes=[pltpu.VMEM((2, 2, n_blk, HEAD_DIM, PAGE_SIZE), f32),
                        pltpu.SemaphoreType.DMA((2,))],
    )
    return pl.pallas_call(
        functools.partial(_nsa_sample_sel_kernel, t_pos=t_pos),
        grid_spec=grid_spec,
        out_shape=[jax.ShapeDtypeStruct((nb, NSA_HEADS, HEAD_DIM), f32),
                   jax.ShapeDtypeStruct(cache_win_t.shape, f32)],
        compiler_params=_cparams(("arbitrary",)),
        name="nsa_sample_sel",
    )(page_table, idx_flat, slc_pages_t, q8, o_c, gates, slc_new, win_new, cache_win_t)


def kernel(x_prompt, x_sample, cache_cmp_kv, cache_slc_kv, cache_win_kv, state_hgrn, page_table, p_prompt, p_sample,
           w_in, g_pre, cmp_pe, cmp_w1, cmp_b1, cmp_w2, hg_lower, hg_norm, w_out, g_post, ple_proj, ple_gate):
    bsz, seq, _ = x_prompt.shape
    nb = x_sample.shape[0]
    n_pool = cache_cmp_kv.shape[1]
    n_pages = page_table.shape[1]
    past = n_pages * PAGE_SIZE
    kv_tail = (2, NSA_KV_HEADS, HEAD_DIM)

    o = _ORIG
    w = w_in[0]
    w_r = jnp.concatenate([w[:, o['q_a']:o['k_cmp']], w[:, o['z_a']:o['end']], w[:, o['k_cmp']:o['gate_a']],
                           w[:, o['gate_a']:o['z_a']],
                           jnp.zeros((D_MODEL, PROJ_WIDTH - o['end']), f32)], axis=1).astype(bf16)
    cw = _compress_weights(cmp_pe[0], cmp_w1[0], cmp_b1[0], cmp_w2[0])
    m_mat = _sel_map_matrix()
    w_out_b, ple_proj_b, ple_gate_b = w_out[0].astype(bf16), ple_proj[0].astype(bf16), ple_gate[0].astype(bf16)

    xp = x_prompt.reshape(bsz * seq, D_MODEL)
    zp, cmp_p, slc_p, win_p, gates_p = _proj(xp, g_pre, w_r, 1024)
    pages_p = cmp_p.reshape(bsz * seq // PAGE_SIZE, PAGE_SIZE, ROW_WIDTH)
    table_p = jnp.arange(bsz * seq // PAGE_SIZE, dtype=jnp.int32).reshape(bsz, seq // PAGE_SIZE)
    comp_p = _compress(pages_p, table_p, *cw, feature_major=False)
    q_t, k_slc, vt_slc, k_win, vt_win, cmp_tp, slc_tp, win_tp = _attention_layout(zp, cmp_p, slc_p, win_p, bsz, seq)
    o_a = _nsa_prompt_blocks(q_t, gates_p, comp_p, k_slc, vt_slc, k_win, vt_win, m_mat.T)
    yb_p, hg_p = _hgrn_prompt(zp, hg_lower, hg_norm, bsz, seq)
    h_p = _finish(xp, o_a, zp, yb_p, p_prompt[0].reshape(bsz * seq, PLE_DIM),
                  w_out_b, g_post, ple_proj_b, ple_gate_b, 512)

    wl_p = min(WINDOW, seq)

    def cache_rows(rows_t):
        return rows_t.reshape((1, bsz) + kv_tail + rows_t.shape[2:]).transpose(0, 1, 5, 2, 3, 4)

    out_prompt = (h_p.reshape(bsz, seq, D_MODEL), cache_rows(cmp_tp), cache_rows(slc_tp),
                  cache_rows(win_tp[:, :, seq - wl_p:]), hg_p[None])

    xs = x_sample.reshape(nb, D_MODEL)
    zs, cmp_s, slc_s, win_new, gates_s = _proj(xs, g_pre, w_r, nb)
    cmp_t = cache_cmp_kv.transpose(0, 1, 3, 4, 5, 2).reshape(n_pool, ROW_WIDTH, PAGE_SIZE)
    comp_s = _compress(cmp_t, page_table, *cw, feature_major=True)
    q8 = zs[:, COL_QA:COL_QA + NSA_WIDTH].astype(bf16).reshape(nb, NSA_HEADS, HEAD_DIM)
    oc_s, ps_s = _nsa_sample_cmp(q8, comp_s, m_mat, past)
    ps_rows = ps_s[:, :NSA_KV_HEADS, :].reshape(nb * NSA_KV_HEADS, N_SELBLK)
    idx = _nsa_sample_topk(ps_rows, past // SEL_BLOCK - 1)
    slc_t = cache_slc_kv.transpose(0, 1, 3, 4, 5, 2).reshape(n_pool, N_COMBO, HEAD_DIM, PAGE_SIZE)
    wl_s = cache_win_kv.shape[2]
    win_t = cache_win_kv.transpose(0, 1, 3, 4, 5, 2).reshape(nb, N_COMBO, HEAD_DIM, wl_s)
    oa_s, win_s = _nsa_sample_sel(page_table, idx[:, :N_SEL].reshape(-1), slc_t, q8, oc_s,
                                  gates_s.reshape(nb, 1, LANES), slc_s.reshape(nb, 1, ROW_WIDTH),
                                  win_new.reshape(nb, 1, ROW_WIDTH), win_t, past)
    win_s = win_s.reshape((1, nb) + kv_tail + (wl_s,)).transpose(0, 1, 5, 2, 3, 4)
    yb_s, hg_s = _hgrn_sample(zs, state_hgrn, hg_lower, hg_norm)
    h_s = _finish(xs, oa_s.reshape(nb, NSA_WIDTH), zs, yb_s, p_sample[0].reshape(nb, PLE_DIM),
                  w_out_b, g_post, ple_proj_b, ple_gate_b, nb)

    out_sample = (h_s.reshape(nb, 1, D_MODEL),
                  cmp_s.reshape((1, nb, 1) + kv_tail),
                  slc_s.reshape((1, nb, 1) + kv_tail),
                  win_s,
                  hg_s)
    return (out_prompt[0], out_sample[0]) + out_prompt[1:] + out_sample[1:]
```

```python
import functools

import numpy as np
import jax
import jax.numpy as jnp
from jax import lax
from jax.experimental import pallas as pl
from jax.experimental.pallas import tpu as pltpu

D_MODEL = 1024
PAGE_SIZE = 128
NSA_HEADS = 8
NSA_KV_HEADS = 2
NSA_GROUP = NSA_HEADS // NSA_KV_HEADS
HEAD_DIM = 64
NSA_WIDTH = NSA_HEADS * HEAD_DIM
KV_WIDTH = NSA_KV_HEADS * HEAD_DIM
CMP_BLOCK = 32
CMP_STRIDE = 16
CMP_RATIO = CMP_BLOCK // CMP_STRIDE
CMP_HIDDEN = 128
SEL_BLOCK = 64
SEL_RATIO = SEL_BLOCK // CMP_STRIDE
N_SEL = 16
WINDOW = 512
Q_BLOCK = 128
HG_HEADS = 4
HG_DIM = 128
HG_WIDTH = HG_HEADS * HG_DIM
PLE_DIM = 256
EPS = 1e-6
NEG = -1e30
FORCE_BONUS = 1e6
REMOVED = -3e38

COL_QA, COL_ZA, COL_QB, COL_FB, COL_IB, COL_ZB = 0, 512, 1024, 1536, 2048, 2560
COL_CMP, COL_SLC, COL_WIN, COL_GATE = 3072, 3328, 3584, 3840
PROJ_WIDTH = 4096
_ORIG = dict(q_a=0, k_cmp=512, gate_a=1280, z_a=1304, q_b=1816, f_b=2328, i_b=2840, z_b=3352, end=3864)

LANES = 128
VMEM_LIMIT = 56 * 1024 * 1024

ROW_WIDTH = 2 * KV_WIDTH
N_COMBO = 2 * NSA_KV_HEADS
SEL_CHUNK = 512
WIN_KEYS = WINDOW + Q_BLOCK
N_CMP_PAD = 512
N_SELBLK = 128

f32 = jnp.float32
bf16 = jnp.bfloat16


def _cparams(sem):
    return pltpu.CompilerParams(dimension_semantics=sem, vmem_limit_bytes=VMEM_LIMIT)


def _dot(a, b):
    return jnp.dot(a, b, preferred_element_type=f32)


def _dot_nt(a, b):
    return lax.dot_general(a, b, (((1,), (1,)), ((), ())), preferred_element_type=f32)


def _dot_tn(a, b):
    return lax.dot_general(a, b, (((0,), (0,)), ((), ())), preferred_element_type=f32)


def _sigmoid(x):
    return 1.0 / (1.0 + jnp.exp(-x))


def _silu(x):
    return x * _sigmoid(x)


PROJ_TILE = 1024
MAIN_TILES = COL_CMP // PROJ_TILE


def _proj_kernel(x_ref, g_ref, w_ref, z_ref, cmp_ref, slc_ref, win_ref, gate_ref, xn_ref):
    j = pl.program_id(1)

    @pl.when(j == 0)
    def _():
        x = x_ref[...]
        y = x * lax.rsqrt(jnp.mean(x * x, axis=-1, keepdims=True) + EPS)
        xn_ref[...] = (y * g_ref[...]).astype(bf16)

    res = _dot(xn_ref[...], w_ref[...])

    @pl.when(j < MAIN_TILES)
    def _():
        z_ref[...] = res

    @pl.when(j == MAIN_TILES)
    def _():
        base = COL_CMP
        cmp_ref[...] = res[:, COL_CMP - base:COL_SLC - base]
        slc_ref[...] = res[:, COL_SLC - base:COL_WIN - base]
        win_ref[...] = res[:, COL_WIN - base:COL_GATE - base]
        gate_ref[...] = res[:, COL_GATE - base:COL_GATE - base + LANES]


def _proj(x, g, w, tm):
    n = x.shape[0]
    row = lambda i, j: (i, 0)
    return pl.pallas_call(
        _proj_kernel,
        grid=(n // tm, PROJ_WIDTH // PROJ_TILE),
        in_specs=[pl.BlockSpec((tm, D_MODEL), row),
                  pl.BlockSpec((1, D_MODEL), lambda i, j: (0, 0)),
                  pl.BlockSpec((D_MODEL, PROJ_TILE), lambda i, j: (0, j))],
        out_specs=[pl.BlockSpec((tm, PROJ_TILE), lambda i, j: (i, jnp.minimum(j, MAIN_TILES - 1))),
                   pl.BlockSpec((tm, ROW_WIDTH), row), pl.BlockSpec((tm, ROW_WIDTH), row),
                   pl.BlockSpec((tm, ROW_WIDTH), row), pl.BlockSpec((tm, LANES), row)],
        out_shape=[jax.ShapeDtypeStruct((n, COL_CMP), f32),
                   jax.ShapeDtypeStruct((n, ROW_WIDTH), f32), jax.ShapeDtypeStruct((n, ROW_WIDTH), f32),
                   jax.ShapeDtypeStruct((n, ROW_WIDTH), f32), jax.ShapeDtypeStruct((n, LANES), f32)],
        scratch_shapes=[pltpu.VMEM((tm, D_MODEL), bf16)],
        compiler_params=_cparams(("parallel", "arbitrary")),
        name="proj",
    )(x, g, w)


def _finish_kernel(x_ref, oa_ref, za_ref, yb_ref, p_ref, wo_ref, gp_ref, pp_ref, pg_ref, h_ref):
    ya = (oa_ref[...] * _silu(za_ref[...])).astype(bf16)
    yb = yb_ref[...].astype(bf16)
    mix = _dot(ya, wo_ref[0:NSA_WIDTH, :]) + _dot(yb, wo_ref[NSA_WIDTH:, :])
    nrm = mix * lax.rsqrt(jnp.mean(mix * mix, axis=-1, keepdims=True) + EPS) * gp_ref[...]
    h = x_ref[...] + nrm
    gate = _sigmoid(_dot(h.astype(bf16), pg_ref[...]))
    h_ref[...] = h + _dot(p_ref[...].astype(bf16), pp_ref[...]) * gate


def _finish(x, oa, z, yb, p, w_out, g_post, ple_proj, ple_gate, tm):
    n = x.shape[0]
    za_blk = COL_ZA // NSA_WIDTH
    row = lambda i: (i, 0)
    const = lambda i: (0, 0)
    return pl.pallas_call(
        _finish_kernel,
        grid=(n // tm,),
        in_specs=[pl.BlockSpec((tm, D_MODEL), row),
                  pl.BlockSpec((tm, NSA_WIDTH), row),
                  pl.BlockSpec((tm, NSA_WIDTH), lambda i: (i, za_blk)),
                  pl.BlockSpec((tm, HG_WIDTH), row),
                  pl.BlockSpec((tm, PLE_DIM), row),
                  pl.BlockSpec((D_MODEL, D_MODEL), const),
                  pl.BlockSpec((1, D_MODEL), const),
                  pl.BlockSpec((PLE_DIM, D_MODEL), const),
                  pl.BlockSpec((D_MODEL, D_MODEL), const)],
        out_specs=pl.BlockSpec((tm, D_MODEL), row),
        out_shape=jax.ShapeDtypeStruct((n, D_MODEL), f32),
        compiler_params=_cparams(("parallel",)),
        name="finish",
    )(x, oa, z, yb, p, w_out, g_post, ple_proj, ple_gate)


SEGS_PER_PAGE = PAGE_SIZE // CMP_STRIDE


def _compress_kernel(pt_ref, pages_ref, wa_ref, wb_ref, pea_ref, peb_ref, b1_ref, w2_ref, out_ref,
                     stage_ref, rows_ref, sem_ref, *, n_pages, feature_major):
    b = pl.program_id(0)
    nb = pl.num_programs(0)
    lane_halves = ROW_WIDTH // LANES
    n_seg = n_pages * SEGS_PER_PAGE

    def page_copy(seq, j):
        s = lax.rem(seq, 2)
        return pltpu.make_async_copy(pages_ref.at[pt_ref[seq, j]], stage_ref.at[s, j], sem_ref.at[s])

    def start_all(seq):
        def body(j, c):
            page_copy(seq, j).start()
            return c
        lax.fori_loop(0, n_pages, body, 0)

    def wait_all(seq):
        def body(j, c):
            page_copy(seq, j).wait()
            return c
        lax.fori_loop(0, n_pages, body, 0)

    def to_rows(s):
        for j in range(n_pages):
            tile = stage_ref[s, j]
            rows = tile.T if feature_major else tile
            for h in range(lane_halves):
                rows_ref[s, h, j * PAGE_SIZE:(j + 1) * PAGE_SIZE, :] = rows[:, h * LANES:(h + 1) * LANES]

    def mlp(s):
        for kind in range(lane_halves):
            cols = slice(kind * LANES, (kind + 1) * LANES)
            seg_a, seg_b = [], []
            for j in range(CMP_STRIDE):
                r = rows_ref[s, kind, pl.ds(j, n_seg, stride=CMP_STRIDE), :]
                seg_a.append((r + pea_ref[j][:, cols]).astype(bf16))
                seg_b.append((r + peb_ref[j][:, cols]).astype(bf16))
            pa = _dot(jnp.concatenate(seg_a, axis=1), wa_ref[kind])
            pb = _dot(jnp.concatenate(seg_b, axis=1), wb_ref[kind])
            pb = pltpu.roll(pb, n_seg - 1, axis=0)
            hid = _silu(pa + pb + b1_ref[kind])
            for g in range(NSA_KV_HEADS):
                hg = hid[:, g * CMP_HIDDEN:(g + 1) * CMP_HIDDEN].astype(bf16)
                out_ref[0, kind * NSA_KV_HEADS + g] = _dot(hg, w2_ref[kind])

    @pl.when(b == 0)
    def _():
        start_all(0)

        @pl.when(nb > 1)
        def _():
            start_all(1)
        wait_all(0)
        to_rows(0)

    @pl.when(b + 2 < nb)
    def _():
        start_all(b + 2)

    @pl.when(b + 1 < nb)
    def _():
        wait_all(b + 1)

    for cur in range(2):
        @pl.when(lax.rem(b, 2) == cur)
        def _(cur=cur):
            to_rows(1 - cur)
            mlp(cur)


def _compress(pages, page_table, wa, wb, pea, peb, b1, w2, feature_major):
    nseq, n_pages = page_table.shape
    n_seg = n_pages * SEGS_PER_PAGE
    const3 = lambda b, pt: (0, 0, 0)
    once = pl.Buffered(1)
    grid_spec = pltpu.PrefetchScalarGridSpec(
        num_scalar_prefetch=1,
        grid=(nseq,),
        in_specs=[pl.BlockSpec(memory_space=pl.ANY),
                  pl.BlockSpec(wa.shape, const3, pipeline_mode=once),
                  pl.BlockSpec(wb.shape, const3, pipeline_mode=once),
                  pl.BlockSpec((CMP_STRIDE, 1, ROW_WIDTH), const3),
                  pl.BlockSpec((CMP_STRIDE, 1, ROW_WIDTH), const3),
                  pl.BlockSpec(b1.shape, const3),
                  pl.BlockSpec((2, CMP_HIDDEN, HEAD_DIM), const3)],
        out_specs=pl.BlockSpec((1, N_COMBO, n_seg, HEAD_DIM), lambda b, pt: (b, 0, 0, 0)),
        scratch_shapes=[pltpu.VMEM((2, n_pages) + pages.shape[1:], f32),
                        pltpu.VMEM((2, ROW_WIDTH // LANES, n_pages * PAGE_SIZE, LANES), f32),
                        pltpu.SemaphoreType.DMA((2,))],
    )
    return pl.pallas_call(
        functools.partial(_compress_kernel, n_pages=n_pages, feature_major=feature_major),
        grid_spec=grid_spec,
        out_shape=jax.ShapeDtypeStruct((nseq, N_COMBO, n_seg, HEAD_DIM), f32),
        compiler_params=_cparams(("arbitrary",)),
        name="compress",
    )(page_table, pages, wa, wb, pea, peb, b1, w2)


def _compress_weights(cmp_pe, cmp_w1, cmp_b1, cmp_w2):
    eye = jnp.eye(NSA_KV_HEADS, dtype=f32)

    def big(w_half):
        t = jnp.einsum('kjdh,gb->kjbdgh', w_half, eye)
        return t.reshape(2, CMP_STRIDE * KV_WIDTH, NSA_KV_HEADS * CMP_HIDDEN)

    w1 = cmp_w1.reshape(2, 2, CMP_STRIDE, HEAD_DIM, CMP_HIDDEN)
    wa = big(w1[:, 0]).astype(bf16)
    wb = big(w1[:, 1]).astype(bf16)
    pe = cmp_pe.reshape(2, 2, CMP_STRIDE, HEAD_DIM)

    def pe_row(p):
        t = jnp.broadcast_to(p.transpose(1, 0, 2)[:, :, None, :],
                             (CMP_STRIDE, 2, NSA_KV_HEADS, HEAD_DIM))
        return t.reshape(CMP_STRIDE, 1, ROW_WIDTH)

    pea, peb = pe_row(pe[:, 0]), pe_row(pe[:, 1])
    b1 = jnp.broadcast_to(cmp_b1[:, None, :], (2, NSA_KV_HEADS, CMP_HIDDEN)).reshape(2, 1, NSA_KV_HEADS * CMP_HIDDEN)
    return wa, wb, pea, peb, b1, cmp_w2.astype(bf16)


def _sel_map_matrix():
    wts = np.convolve(np.ones(SEL_RATIO), np.ones(CMP_RATIO))
    m = np.zeros((N_CMP_PAD, N_SELBLK), np.float32)
    for j in range(N_SELBLK):
        for k, wk in enumerate(wts):
            n = SEL_RATIO * j + k
            if n < N_CMP_PAD - 1:
                m[n, j] = wk
    return jnp.asarray(m, dtype=bf16)


def _importance_to_blocks(imp, m):
    hi = imp.astype(bf16)
    lo = (imp - hi.astype(f32)).astype(bf16)
    return _dot(hi, m) + _dot(lo, m)


LOG2E = 1.4426950408889634
MASK_BIG = 1e30
AUG_ROWS = 16
VT_ROWS = HEAD_DIM + 16
KEY_LOW = 256
N_FORCED = 3


NSA_STEP_BLOCKS = 2


def _nsa_prompt_blocks_kernel(qt_ref, gate_ref, comp_ref, caug_ref, ks_ref, vts_ref, kw_ref, vtw_ref, mt_ref, o_ref,
                              qa_ref, acc_ref, mg_ref, bias_ref, need_ref):
    cols = NSA_GROUP * Q_BLOCK
    n_blocks = qt_ref.shape[2]
    blocks = range(n_blocks)
    heads = range(NSA_KV_HEADS)
    insts = [(qb, g) for qb in blocks for g in heads]
    blk_per_chunk = SEL_CHUNK // SEL_BLOCK
    n_chunk_max = vts_ref.shape[2]

    lane = lax.broadcasted_iota(jnp.int32, (1, cols), 1)
    head = jnp.right_shift(lane, Q_BLOCK.bit_length() - 1)
    blk = lax.broadcasted_iota(jnp.int32, (N_SELBLK, Q_BLOCK), 0)
    qcol = lax.broadcasted_iota(jnp.int32, (N_SELBLK, Q_BLOCK), 1)
    c_end = lax.broadcasted_iota(jnp.int32, (N_CMP_PAD, 1), 0) * CMP_STRIDE + (CMP_BLOCK - 1)
    krow = lax.broadcasted_iota(jnp.int32, (SEL_CHUNK, 1), 0)
    wrow = lax.broadcasted_iota(jnp.int32, (WIN_KEYS, 1), 0)
    arow = lax.broadcasted_iota(jnp.int32, (AUG_ROWS // 2, cols), 0)
    gate_all = _sigmoid(gate_ref[...])

    q0 = [(pl.program_id(1) * n_blocks + qb) * Q_BLOCK for qb in blocks]
    tq = [q0[qb] + jnp.bitwise_and(lane, Q_BLOCK - 1) for qb in blocks]
    last = [q0[qb] // SEL_CHUNK for qb in blocks]
    w0 = [pl.multiple_of(jnp.maximum(q0[qb] - WINDOW, 0), Q_BLOCK) for qb in blocks]
    gate_t = [gate_all[qb * Q_BLOCK:(qb + 1) * Q_BLOCK].T for qb in blocks]

    slope, alibi_rows, pos_rows = [], [], []
    for g in heads:
        c = jnp.exp2(-(head.astype(f32) + 1.0 + NSA_GROUP * g)) * LOG2E
        c_hi = c.astype(bf16).astype(f32)
        c_mid = (c - c_hi).astype(bf16).astype(f32)
        c_lo = (c - c_hi - c_mid).astype(bf16).astype(f32)
        c3 = jnp.where((arow == 0) | (arow == 3), c_hi, jnp.where((arow == 1) | (arow == 4), c_mid, c_lo))
        slope.append(c)
        alibi_rows.append(jnp.where(arow >= 6, 0.0, c3))
        rows = jnp.where(arow >= 6, 0.0, jnp.where(arow >= 3, KEY_LOW * c3, c3))
        pos_rows.append(jnp.concatenate([rows, jnp.zeros_like(rows)], axis=0).astype(bf16))

    def chunk_scores(qb, g, ck, slot=0, live=None):
        k0 = pl.multiple_of(ck * SEL_CHUNK, SEL_CHUNK)
        b8 = bias_ref[qb, g, pl.ds(pl.multiple_of(ck * blk_per_chunk, blk_per_chunk), blk_per_chunk), :]
        if live is not None:
            b8 = jnp.where(live, b8, -MASK_BIG)
        qa_ref[qb, g, slot, HEAD_DIM:HEAD_DIM + AUG_ROWS, :] = jnp.concatenate(
            [jnp.concatenate([b8] * NSA_GROUP, axis=1), alibi_rows[g]], axis=0).astype(bf16)
        return _dot(ks_ref[0, g, pl.ds(k0, SEL_CHUNK), :], qa_ref[qb, g, slot])

    def sel_chunk(qb, g, ck, causal, s):
        k0 = pl.multiple_of(ck * SEL_CHUNK, SEL_CHUNK)
        if causal:
            s = jnp.where(k0 + krow <= tq[qb], s, NEG)
        shift = slope[g] * k0.astype(f32)
        m_old = mg_ref[qb, g] - shift
        m_new = jnp.maximum(m_old, jnp.max(s, axis=0, keepdims=True))
        p = jnp.exp2(s - m_new).astype(bf16)
        acc_ref[qb, g] = acc_ref[qb, g] * jnp.exp2(m_old - m_new) + _dot(vts_ref[0, g, ck], p)
        mg_ref[qb, g] = m_new + shift

    s_cmp, s_win = {}, {}
    for qb, g in insts:
        q_pos = jnp.concatenate([qt_ref[0, g, qb], pos_rows[g],
                                 jnp.zeros((LANES - HEAD_DIM - AUG_ROWS, cols), bf16)], axis=0)
        kc = jnp.concatenate([comp_ref[0, g].astype(bf16), caug_ref[...]], axis=1)
        s_cmp[qb, g] = _dot(kc, q_pos)
        kw = kw_ref[0, g, pl.ds(w0[qb], WIN_KEYS), :]
        s_win[qb, g] = _dot(kw, q_pos)

    oc_t, ps_t = {}, {}
    for qb in blocks:
        seen = c_end <= tq[qb]
        any_seen = tq[qb] >= CMP_BLOCK - 1
        for g in heads:
            s_c = jnp.where(seen, s_cmp[qb, g], NEG)
            e = jnp.exp2(s_c - jnp.max(s_c, axis=0, keepdims=True))
            rl = jnp.where(any_seen, 1.0 / jnp.maximum(jnp.sum(e, axis=0, keepdims=True), 1e-30), 0.0)
            p_c = e * rl
            oc_t[qb, g] = _dot_tn(comp_ref[0, NSA_KV_HEADS + g].astype(bf16), p_c.astype(bf16))
            imp_t = (p_c[:, 0:Q_BLOCK] + p_c[:, Q_BLOCK:2 * Q_BLOCK]
                     + p_c[:, 2 * Q_BLOCK:3 * Q_BLOCK] + p_c[:, 3 * Q_BLOCK:4 * Q_BLOCK])
            hi = imp_t.astype(bf16)
            lo = (imp_t - hi.astype(f32)).astype(bf16)
            ps_t[qb, g] = _dot(mt_ref[...], hi) + _dot(mt_ref[...], lo)

    ow_t = {}
    for qb in blocks:
        wkey = w0[qb] + wrow
        in_window = (wkey <= tq[qb]) & (wkey >= tq[qb] - WINDOW)
        t0 = w0[qb] // Q_BLOCK
        for g in heads:
            s_w = jnp.where(in_window, s_win[qb, g], NEG)
            e = jnp.exp2(s_w - jnp.max(s_w, axis=0, keepdims=True)).astype(bf16)
            acc_w = jnp.zeros((VT_ROWS, cols), f32)
            for t in range(WIN_KEYS // Q_BLOCK):
                acc_w = acc_w + _dot(vtw_ref[0, g, t0 + t], e[t * Q_BLOCK:(t + 1) * Q_BLOCK])
            ow_t[qb, g] = acc_w[0:HEAD_DIM] * (1.0 / jnp.maximum(acc_w[HEAD_DIM:HEAD_DIM + 1], 1e-30))

    valid, sel_t, score = {}, {}, {}
    for qb in blocks:
        jt = jnp.right_shift(q0[qb] + qcol, SEL_BLOCK.bit_length() - 1)
        valid[qb] = blk <= jt
        forced = (blk == 0) | (blk == jt) | (blk == jt - 1)
        for g in heads:
            sel_t[qb, g] = jnp.where(forced, 1.0, 0.0)
            score[qb, g] = jnp.where(valid[qb] & jnp.logical_not(forced), ps_t[qb, g], NEG)
    for _ in range(N_SEL - N_FORCED):
        for inst in insts:
            best = jnp.max(score[inst], axis=0, keepdims=True)
            idx = jnp.min(jnp.where(score[inst] == best, blk, N_SELBLK), axis=0, keepdims=True)
            hit = blk == idx
            sel_t[inst] = jnp.where(hit, 1.0, sel_t[inst])
            score[inst] = jnp.where(hit, REMOVED, score[inst])
    for qb, g in insts:
        sel_t[qb, g] = jnp.where(valid[qb], sel_t[qb, g], 0.0)

    for qb, g in insts:
        bias_ref[qb, g] = (sel_t[qb, g] - 1.0) * MASK_BIG
        for slot in range(qa_ref.shape[2]):
            qa_ref[qb, g, slot, 0:HEAD_DIM, :] = qt_ref[0, g, qb]
            qa_ref[qb, g, slot, HEAD_DIM + AUG_ROWS:, :] = jnp.zeros((LANES - HEAD_DIM - AUG_ROWS, cols), bf16)
        acc_ref[qb, g] = jnp.zeros(acc_ref.shape[2:], f32)
        mg_ref[qb, g] = jnp.full(mg_ref.shape[2:], NEG, f32)
    own = {(qb, g): chunk_scores(qb, g, last[qb]) for qb, g in insts}
    for qb, g in insts:
        sel_chunk(qb, g, last[qb], True, own[qb, g])

    n_need = []
    for qb in blocks:
        count = jnp.int32(0)
        for ck in range(n_chunk_max):
            rows = slice(ck * blk_per_chunk, (ck + 1) * blk_per_chunk)
            picked = jnp.maximum(*[jnp.max(sel_t[qb, g][rows, :]) for g in heads])
            need_ref[qb, count] = ck
            count = count + ((picked > 0.5) & (ck < last[qb])).astype(jnp.int32)
        n_need.append(count)

    for qb in blocks:
        def chunk_pair(i, carry, qb=qb):
            first = 2 * i
            live = first + 1 < n_need[qb]
            cks = [need_ref[qb, first], need_ref[qb, jnp.minimum(first + 1, n_need[qb] - 1)]]
            scores = [[chunk_scores(qb, g, cks[slot], slot, live if slot else None) for g in heads]
                      for slot in range(2)]
            for slot in range(2):
                for g in heads:
                    sel_chunk(qb, g, cks[slot], False, scores[slot][g])
            return carry

        lax.fori_loop(0, (n_need[qb] + 1) // 2, chunk_pair, 0)

    for qb, g in insts:
        acc = acc_ref[qb, g]
        os_t = acc[0:HEAD_DIM] * (1.0 / jnp.maximum(acc[HEAD_DIM:HEAD_DIM + 1], 1e-30))
        outs = []
        for r in range(NSA_GROUP):
            cs = slice(r * Q_BLOCK, (r + 1) * Q_BLOCK)
            row = (g * NSA_GROUP + r) * 3
            outs.append(gate_t[qb][row:row + 1, :] * oc_t[qb, g][:, cs] + gate_t[qb][row + 1:row + 2, :] * os_t[:, cs]
                        + gate_t[qb][row + 2:row + 3, :] * ow_t[qb, g][:, cs])
        for pair in range(NSA_GROUP // 2):
            both = jnp.concatenate([outs[2 * pair], outs[2 * pair + 1]], axis=0)
            lane0 = (g * NSA_GROUP // 2 + pair) * LANES
            o_ref[qb * Q_BLOCK:(qb + 1) * Q_BLOCK, lane0:lane0 + LANES] = both.T


def _nsa_prompt_blocks(q_t, gates, comp, k_slc, vt_slc, k_win, vt_win, m_t):
    bsz, _, nq = q_t.shape[:3]
    seq = k_slc.shape[2]
    cols = NSA_GROUP * Q_BLOCK
    nblk = NSA_STEP_BLOCKS
    steps = nq // nblk
    per_b4 = lambda b, i: (b, 0, 0, 0)
    per_b5 = lambda b, i: (b, 0, 0, 0, 0)
    inst = (nblk, NSA_KV_HEADS)
    return pl.pallas_call(
        _nsa_prompt_blocks_kernel,
        grid=(bsz, steps),
        in_specs=[pl.BlockSpec((1, NSA_KV_HEADS, nblk, HEAD_DIM, cols), lambda b, i: (b, 0, i, 0, 0)),
                  pl.BlockSpec((nblk * Q_BLOCK, LANES), lambda b, i: (b * steps + i, 0)),
                  pl.BlockSpec((1, N_COMBO, N_CMP_PAD, HEAD_DIM), per_b4),
                  pl.BlockSpec((N_CMP_PAD, HEAD_DIM), lambda b, i: (0, 0)),
                  pl.BlockSpec((1, NSA_KV_HEADS, seq, LANES), per_b4),
                  pl.BlockSpec((1,) + vt_slc.shape[1:], per_b5),
                  pl.BlockSpec((1, NSA_KV_HEADS, seq, LANES), per_b4),
                  pl.BlockSpec((1,) + vt_win.shape[1:], per_b5),
                  pl.BlockSpec((N_SELBLK, N_CMP_PAD), lambda b, i: (0, 0))],
        out_specs=pl.BlockSpec((nblk * Q_BLOCK, NSA_WIDTH), lambda b, i: (b * steps + i, 0)),
        out_shape=jax.ShapeDtypeStruct((bsz * seq, NSA_WIDTH), f32),
        scratch_shapes=[pltpu.VMEM(inst + (2, LANES, cols), bf16),
                        pltpu.VMEM(inst + (VT_ROWS, cols), f32),
                        pltpu.VMEM(inst + (1, cols), f32),
                        pltpu.VMEM(inst + (N_SELBLK, Q_BLOCK), f32),
                        pltpu.SMEM((nblk, vt_slc.shape[2]), jnp.int32)],
        compiler_params=_cparams(("parallel", "arbitrary")),
        name="nsa_prompt",
    )(q_t, gates, comp, _cmp_position_columns(), k_slc, vt_slc, k_win, vt_win, m_t)


def _position_columns(pos):
    low, high = (pos % KEY_LOW)[:, None], (pos // KEY_LOW)[:, None]
    return np.concatenate([low, low, low, high, high, high], axis=1).astype(np.float32)


def _cmp_position_columns():
    aug = np.zeros((N_CMP_PAD, HEAD_DIM), np.float32)
    aug[:, 0:6] = _position_columns(np.arange(N_CMP_PAD) * CMP_STRIDE + CMP_BLOCK - 1)
    return jnp.asarray(aug, dtype=bf16)


LAYOUT_ROWS = 1024


def _attention_layout_kernel(q_ref, cmp_ref, slc_ref, win_ref, xs_ref, xw_ref,
                             qt_ref, ks_ref, vts_ref, kw_ref, vtw_ref, cmp_t_ref, slc_t_ref, win_t_ref):
    cmp_t_ref[0] = cmp_ref[...].T
    tm = q_ref.shape[0]
    q = q_ref[...] * (HEAD_DIM ** -0.5 * LOG2E)
    for qb in range(tm // Q_BLOCK):
        rows = slice(qb * Q_BLOCK, (qb + 1) * Q_BLOCK)
        for pair in range(NSA_HEADS // 2):
            t = q[rows, pair * LANES:(pair + 1) * LANES].T
            for k in range(2):
                g, r = divmod(2 * pair + k, NSA_GROUP)
                qt_ref[0, g, qb, :, r * Q_BLOCK:(r + 1) * Q_BLOCK] = t[k * HEAD_DIM:(k + 1) * HEAD_DIM].astype(bf16)

    for src_ref, extra_ref, k_ref, vt_ref, t_ref in ((slc_ref, xs_ref, ks_ref, vts_ref, slc_t_ref),
                                                     (win_ref, xw_ref, kw_ref, vtw_ref, win_t_ref)):
        tile = vt_ref.shape[4]
        s = src_ref[...]
        s_t = s.T
        t_ref[0] = s_t
        v_t = s_t[KV_WIDTH:]
        tail = jnp.where(lax.broadcasted_iota(jnp.int32, (VT_ROWS - HEAD_DIM, tile), 0) == 0, 1.0, 0.0)
        for g in range(NSA_KV_HEADS):
            k_ref[0, g] = jnp.concatenate([s[:, g * HEAD_DIM:(g + 1) * HEAD_DIM].astype(bf16), extra_ref[...]], axis=1)
            for c in range(tm // tile):
                blk = v_t[g * HEAD_DIM:(g + 1) * HEAD_DIM, c * tile:(c + 1) * tile]
                vt_ref[0, g, c] = jnp.concatenate([blk, tail], axis=0).astype(bf16)


def _attention_layout(z, cmp, slc, win, bsz, seq):
    tm = LAYOUT_ROWS
    per_b = seq // tm
    pos = np.arange(seq)
    xs = np.zeros((seq, HEAD_DIM), np.float32)
    xs[pos, (pos // SEL_BLOCK) % (SEL_CHUNK // SEL_BLOCK)] = 1.0
    xs[:, 8:11] = (pos % KEY_LOW)[:, None]
    xs[:, 11:14] = (pos % SEL_CHUNK - pos % KEY_LOW)[:, None]
    xw = np.zeros((seq, HEAD_DIM), np.float32)
    xw[:, 0:6] = _position_columns(pos)
    row = lambda i: (i, 0)
    rep = lambda i: (i % per_b, 0)
    out4 = lambda i: (i // per_b, 0, i % per_b, 0)
    out5 = lambda i: (i // per_b, 0, i % per_b, 0, 0)
    k_shape = jax.ShapeDtypeStruct((bsz, NSA_KV_HEADS, seq, LANES), bf16)
    vt_shape = lambda tile: jax.ShapeDtypeStruct((bsz, NSA_KV_HEADS, seq // tile, VT_ROWS, tile), bf16)
    vt_spec = lambda tile: pl.BlockSpec((1, NSA_KV_HEADS, tm // tile, VT_ROWS, tile), out5)
    rows_t_shape = jax.ShapeDtypeStruct((bsz, ROW_WIDTH, seq), f32)
    rows_t_spec = pl.BlockSpec((1, ROW_WIDTH, tm), lambda i: (i // per_b, 0, i % per_b))
    return pl.pallas_call(
        _attention_layout_kernel,
        grid=(bsz * per_b,),
        in_specs=[pl.BlockSpec((tm, NSA_WIDTH), row), pl.BlockSpec((tm, ROW_WIDTH), row),
                  pl.BlockSpec((tm, ROW_WIDTH), row), pl.BlockSpec((tm, ROW_WIDTH), row),
                  pl.BlockSpec((tm, HEAD_DIM), rep), pl.BlockSpec((tm, HEAD_DIM), rep)],
        out_specs=[pl.BlockSpec((1, NSA_KV_HEADS, tm // Q_BLOCK, HEAD_DIM, NSA_GROUP * Q_BLOCK), out5),
                   pl.BlockSpec((1, NSA_KV_HEADS, tm, LANES), out4), vt_spec(SEL_CHUNK),
                   pl.BlockSpec((1, NSA_KV_HEADS, tm, LANES), out4), vt_spec(Q_BLOCK),
                   rows_t_spec, rows_t_spec, rows_t_spec],
        out_shape=[jax.ShapeDtypeStruct((bsz, NSA_KV_HEADS, seq // Q_BLOCK, HEAD_DIM, NSA_GROUP * Q_BLOCK), bf16),
                   k_shape, vt_shape(SEL_CHUNK), k_shape, vt_shape(Q_BLOCK),
                   rows_t_shape, rows_t_shape, rows_t_shape],
        compiler_params=_cparams(("parallel",)),
        name="attention_layout",
    )(z, cmp, slc, win, jnp.asarray(xs, dtype=bf16), jnp.asarray(xw, dtype=bf16))


HG_CHUNK = 128
HG_STEP_CHUNKS = 4
HG_BAND = 8


def _lower_bound(lo):
    m = jnp.max(lo, axis=0, keepdims=True)
    e = jnp.exp(lo - m)
    return e[0:1] / jnp.sum(e, axis=0, keepdims=True)


def _hgrn_gates(q, fl, lb):
    sg = _sigmoid(fl)
    f = lb + (1.0 - lb) * sg
    k = (1.0 - lb) * (1.0 - sg)
    return _silu(q), k, jnp.log(f)


def _hgrn_out(o, nw, zb):
    o = o * lax.rsqrt(jnp.mean(o * o, axis=-1, keepdims=True) + EPS) * nw
    return o * _silu(zb)


def _hgrn_prompt_kernel(q_ref, f_ref, i_ref, zb_ref, lo_ref, nw_ref, y_ref, s_ref, st_ref):
    c = pl.program_id(1)

    @pl.when(c == 0)
    def _():
        st_ref[...] = jnp.zeros_like(st_ref)

    n = HG_CHUNK
    row = lax.broadcasted_iota(jnp.int32, (n, n), 0)
    col = lax.broadcasted_iota(jnp.int32, (n, n), 1)
    heads = [slice(h * HG_DIM, (h + 1) * HG_DIM) for h in range(HG_HEADS)]

    qq, kk, logf = _hgrn_gates(q_ref[...], f_ref[...], _lower_bound(lo_ref[...]))
    v = i_ref[...]
    trow = lax.broadcasted_iota(jnp.int32, (n, 1), 0)
    chunks = [slice(i * n, (i + 1) * n) for i in range(q_ref.shape[0] // n)]
    bs = []
    for rs in chunks:
        b = logf[rs]
        step = 1
        while step < n:
            b = b + jnp.where(trow >= step, pltpu.roll(b, step, axis=0), 0.0)
            step *= 2
        bs.append(b)

    intra = [[_hgrn_intra(qq[rs, hs], kk[rs, hs], b[:, hs], row, col) for hs in heads] for rs, b in zip(chunks, bs)]
    for i, rs in enumerate(chunks):
        for h, hs in enumerate(heads):
            st = st_ref[h]
            bh, kh, vh = bs[i][:, hs], kk[rs, hs], v[rs, hs].astype(bf16)
            o = _dot(intra[i][h].astype(bf16), vh) + _dot_nt((qq[rs, hs] * jnp.exp(bh)).astype(bf16), st.astype(bf16))
            b_last = bh[n - 1:n]
            st_new = st * jnp.exp(b_last) + _dot_tn(vh, (kh * jnp.exp(b_last - bh)).astype(bf16))
            st_ref[h] = st_new
            y_ref[rs, hs] = _hgrn_out(o, nw_ref[...], zb_ref[rs, hs])

    @pl.when(c == pl.num_programs(1) - 1)
    def _():
        for h in range(HG_HEADS):
            s_ref[0, h] = st_ref[h].T


def _hgrn_intra(qq, kk, b, row, col):
    n = HG_CHUNK
    a = jnp.where(row == col, _dot_nt(qq.astype(bf16), kk.astype(bf16)), 0.0)

    sub = jnp.bitwise_and(row, HG_BAND - 1)
    b3 = b.reshape(n // HG_BAND, HG_BAND, HG_DIM)
    size = 2
    while size <= HG_BAND:
        half = size // 2
        beta = None
        for s0 in range(0, HG_BAND, size):
            piece = jnp.broadcast_to(b3[:, s0 + half - 1:s0 + half, :], b3.shape).reshape(n, HG_DIM)
            beta = piece if beta is None else jnp.where(sub >= s0, piece, beta)
        left = jnp.bitwise_and(sub, size - 1) < half
        decay = jnp.exp(jnp.where(left, beta - b, b - beta))
        part = _dot_nt(jnp.where(left, 0.0, qq * decay).astype(bf16), jnp.where(left, kk * decay, 0.0).astype(bf16))
        sh = size.bit_length() - 1
        a = a + jnp.where(jnp.right_shift(row, sh) == jnp.right_shift(col, sh), part, 0.0)
        size *= 2

    while size <= n:
        half = size // 2
        qs, ks = [], []
        zero = jnp.zeros((half, HG_DIM), f32)
        for r0 in range(0, n, size):
            beta = b[r0 + half - 1:r0 + half]
            ks += [kk[r0:r0 + half] * jnp.exp(beta - b[r0:r0 + half]), zero]
            qs += [zero, qq[r0 + half:r0 + size] * jnp.exp(b[r0 + half:r0 + size] - beta)]
        part = _dot_nt(jnp.concatenate(qs, axis=0).astype(bf16), jnp.concatenate(ks, axis=0).astype(bf16))
        if size < n:
            sh = size.bit_length() - 1
            part = jnp.where(jnp.right_shift(row, sh) == jnp.right_shift(col, sh), part, 0.0)
        a = a + part
        size *= 2
    return a


def _hgrn_prompt(z, hg_lower, hg_norm, bsz, seq):
    rows = HG_STEP_CHUNKS * HG_CHUNK
    nch = seq // rows

    def zcol(base):
        return pl.BlockSpec((rows, HG_WIDTH), lambda b, c: (b * nch + c, base // HG_WIDTH))

    return pl.pallas_call(
        _hgrn_prompt_kernel,
        grid=(bsz, nch),
        in_specs=[zcol(COL_QB), zcol(COL_FB), zcol(COL_IB), zcol(COL_ZB),
                  pl.BlockSpec(hg_lower.shape, lambda b, c: (0, 0)),
                  pl.BlockSpec((1, HG_DIM), lambda b, c: (0, 0))],
        out_specs=[pl.BlockSpec((rows, HG_WIDTH), lambda b, c: (b * nch + c, 0)),
                   pl.BlockSpec((1, HG_HEADS, HG_DIM, HG_DIM), lambda b, c: (b, 0, 0, 0))],
        out_shape=[jax.ShapeDtypeStruct((bsz * seq, HG_WIDTH), f32),
                   jax.ShapeDtypeStruct((bsz, HG_HEADS, HG_DIM, HG_DIM), f32)],
        scratch_shapes=[pltpu.VMEM((HG_HEADS, HG_DIM, HG_DIM), f32)],
        compiler_params=_cparams(("parallel", "arbitrary")),
        name="hgrn_prompt",
    )(z, z, z, z, hg_lower, hg_norm)


def _hgrn_sample_kernel(q_ref, f_ref, i_ref, zb_ref, lo_ref, nw_ref, s_ref, y_ref, so_ref, o_scr):
    nb = q_ref.shape[0]
    lb = _lower_bound(lo_ref[...])
    qq, kk, logf = _hgrn_gates(q_ref[...], f_ref[...], lb)
    v = i_ref[...]
    pad = jnp.zeros((HG_DIM - nb, HG_DIM), f32)
    to_cols = lambda x: jnp.concatenate([x, pad], axis=0).T
    f_t, k_t, q_t = to_cols(jnp.exp(logf)), to_cols(kk), to_cols(qq)
    for b in range(nb):
        s_new = f_t[:, b:b + 1] * s_ref[0, b, 0] + k_t[:, b:b + 1] * v[b:b + 1, :]
        so_ref[0, b, 0] = s_new
        o_scr[b:b + 1, :] = jnp.sum(s_new * q_t[:, b:b + 1], axis=0, keepdims=True)
    y_ref[...] = _hgrn_out(o_scr[...], nw_ref[...], zb_ref[...])


def _hgrn_sample(z, state, hg_lower, hg_norm):
    nb = z.shape[0]

    def zcol(base):
        return pl.BlockSpec((nb, HG_DIM), lambda h: (0, base // HG_DIM + h))

    sblk = pl.BlockSpec((1, nb, 1, HG_DIM, HG_DIM), lambda h: (0, 0, h, 0, 0))
    return pl.pallas_call(
        _hgrn_sample_kernel,
        grid=(HG_HEADS,),
        in_specs=[zcol(COL_QB), zcol(COL_FB), zcol(COL_IB), zcol(COL_ZB),
                  pl.BlockSpec((hg_lower.shape[0], HG_DIM), lambda h: (0, h)),
                  pl.BlockSpec((1, HG_DIM), lambda h: (0, 0)),
                  sblk],
        out_specs=[pl.BlockSpec((nb, HG_DIM), lambda h: (0, h)), sblk],
        out_shape=[jax.ShapeDtypeStruct((nb, HG_WIDTH), f32),
                   jax.ShapeDtypeStruct(state.shape, f32)],
        scratch_shapes=[pltpu.VMEM((nb, HG_DIM), f32)],
        compiler_params=_cparams(("parallel",)),
        name="hgrn_sample",
    )(z, z, z, z, hg_lower, hg_norm, state)


N_PICK = N_SEL - 1


def _head_slopes():
    h = lax.broadcasted_iota(jnp.int32, (NSA_HEADS, 1), 0).astype(f32)
    return jnp.exp2(-(h + 1.0))


def _nsa_sample_cmp_kernel(q_ref, comp_ref, m_ref, oc_ref, ps_ref, *, t_pos):
    q = q_ref[0] * jnp.asarray(HEAD_DIM ** -0.5, bf16)
    slope = _head_slopes()
    head = lax.broadcasted_iota(jnp.int32, (NSA_HEADS, 1), 0)
    ncol = lax.broadcasted_iota(jnp.int32, (1, N_CMP_PAD), 1)
    d_c = t_pos - (ncol * CMP_STRIDE + (CMP_BLOCK - 1))
    mask = d_c >= 0
    o_c = jnp.zeros((NSA_HEADS, HEAD_DIM), f32)
    ps_ref[...] = jnp.zeros_like(ps_ref)
    for g in range(NSA_KV_HEADS):
        kc = comp_ref[0, g].astype(bf16)
        vc = comp_ref[0, NSA_KV_HEADS + g].astype(bf16)
        s = jnp.where(mask, _dot_nt(q, kc) - slope * d_c.astype(f32), NEG)
        m = jnp.max(s, axis=1, keepdims=True)
        e = jnp.where(mask, jnp.exp(s - m), 0.0)
        p = e * (1.0 / jnp.maximum(jnp.sum(e, axis=1, keepdims=True), 1e-30))
        mine = jnp.right_shift(head, NSA_GROUP.bit_length() - 1) == g
        o_c = jnp.where(mine, _dot(p.astype(bf16), vc), o_c)
        imp = jnp.sum(jnp.where(mine, p, 0.0), axis=0, keepdims=True)
        ps_ref[0, g:g + 1, :] = _importance_to_blocks(imp, m_ref[...])
    oc_ref[0] = o_c


def _nsa_sample_cmp(q8, comp, m_mat, t_pos):
    nb = q8.shape[0]
    return pl.pallas_call(
        functools.partial(_nsa_sample_cmp_kernel, t_pos=t_pos),
        grid=(nb,),
        in_specs=[pl.BlockSpec((1, NSA_HEADS, HEAD_DIM), lambda b: (b, 0, 0)),
                  pl.BlockSpec((1, N_COMBO, N_CMP_PAD, HEAD_DIM), lambda b: (b, 0, 0, 0)),
                  pl.BlockSpec((N_CMP_PAD, N_SELBLK), lambda b: (0, 0))],
        out_specs=[pl.BlockSpec((1, NSA_HEADS, HEAD_DIM), lambda b: (b, 0, 0)),
                   pl.BlockSpec((1, 8, N_SELBLK), lambda b: (b, 0, 0))],
        out_shape=[jax.ShapeDtypeStruct((nb, NSA_HEADS, HEAD_DIM), f32),
                   jax.ShapeDtypeStruct((nb, 8, N_SELBLK), f32)],
        compiler_params=_cparams(("parallel",)),
        name="nsa_sample_cmp",
    )(q8, comp, m_mat)


def _nsa_sample_topk_kernel(ps_ref, idx_ref, *, last_blk):
    ps = ps_ref[...]
    lane = lax.broadcasted_iota(jnp.int32, ps.shape, 1)
    score = ps + jnp.where((lane == 0) | (lane == last_blk), FORCE_BONUS, 0.0)
    out = jnp.zeros(ps.shape, jnp.int32)
    for r in range(N_PICK):
        best = jnp.max(score, axis=1, keepdims=True)
        idx = jnp.min(jnp.where(score == best, lane, N_SELBLK), axis=1, keepdims=True)
        out = jnp.where(lane == r, idx, out)
        score = jnp.where(lane == idx, REMOVED, score)
    idx_ref[...] = out


def _nsa_sample_topk(ps_rows, last_blk):
    return pl.pallas_call(
        functools.partial(_nsa_sample_topk_kernel, last_blk=last_blk),
        out_shape=jax.ShapeDtypeStruct(ps_rows.shape, jnp.int32),
        name="nsa_sample_topk",
    )(ps_rows)


def _nsa_sample_sel_kernel(pt_ref, idx_ref, slc_ref, q_ref, oc_ref, gate_ref, slc_new_ref, win_new_ref, wc_ref,
                           o_ref, wout_ref, kv_scr, sem_ref, *, t_pos):
    n_blk = NSA_KV_HEADS * N_PICK
    halves = PAGE_SIZE // SEL_BLOCK
    b = pl.program_id(0)
    nb = pl.num_programs(0)
    slot = lax.rem(b, 2)

    def picked(seq, k):
        return idx_ref[(seq * NSA_KV_HEADS + k // N_PICK) * N_SEL + k % N_PICK]

    def blk_copies(seq, s, k):
        page = pt_ref[seq, picked(seq, k) // halves]
        g = k // N_PICK
        return [pltpu.make_async_copy(slc_ref.at[page, kind * NSA_KV_HEADS + g], kv_scr.at[s, kind, k],
                                      sem_ref.at[s]) for kind in range(2)]

    def start_all(seq, s):
        for k in range(n_blk):
            for cp in blk_copies(seq, s, k):
                cp.start()

    @pl.when(b == 0)
    def _():
        start_all(0, 0)

    @pl.when(b + 1 < nb)
    def _():
        start_all(b + 1, 1 - slot)

    for k in range(n_blk):
        for cp in blk_copies(b, slot, k):
            cp.wait()

    q = q_ref[0] * jnp.asarray(HEAD_DIM ** -0.5, bf16)
    slope = _head_slopes()
    head = lax.broadcasted_iota(jnp.int32, (NSA_HEADS, 1), 0)
    lane = lax.broadcasted_iota(jnp.int32, (1, PAGE_SIZE), 1)
    new_slc = slc_new_ref[0]
    new_win = win_new_ref[0]
    wl = wc_ref.shape[3]
    wlane = lax.broadcasted_iota(jnp.int32, (1, wl), 1)
    d_w = (wl - wlane).astype(f32)
    qf = q.astype(f32)

    def attend(s, mask, v_t, new_row, g):
        k_new = new_row[:, g * HEAD_DIM:(g + 1) * HEAD_DIM].astype(bf16).astype(f32)
        v_new = new_row[:, KV_WIDTH + g * HEAD_DIM:KV_WIDTH + (g + 1) * HEAD_DIM].astype(bf16).astype(f32)
        s_new = jnp.sum(qf * k_new, axis=1, keepdims=True)
        m = jnp.maximum(jnp.max(s, axis=1, keepdims=True), s_new)
        e, e_new = jnp.exp(s - m), jnp.exp(s_new - m)
        if mask is not None:
            e = jnp.where(mask, e, 0.0)
        den = jnp.maximum(jnp.sum(e, axis=1, keepdims=True) + e_new, 1e-30)
        num = _dot_nt(e.astype(bf16), v_t.astype(bf16)) + e_new.astype(bf16).astype(f32) * v_new
        return num * (1.0 / den)

    o_s = jnp.zeros((NSA_HEADS, HEAD_DIM), f32)
    o_w = jnp.zeros((NSA_HEADS, HEAD_DIM), f32)
    for g in range(NSA_KV_HEADS):
        mine = jnp.right_shift(head, NSA_GROUP.bit_length() - 1) == g
        pos, member = [], []
        for n in range(N_PICK):
            j = picked(b, g * N_PICK + n)
            pos.append((j // halves) * PAGE_SIZE + lane)
            member.append(jnp.right_shift(lane, SEL_BLOCK.bit_length() - 1) == j % halves)
        pos = jnp.concatenate(pos, axis=1)
        member = jnp.concatenate(member, axis=1)
        k_t = jnp.concatenate([kv_scr[slot, 0, g * N_PICK + n] for n in range(N_PICK)], axis=1)
        v_t = jnp.concatenate([kv_scr[slot, 1, g * N_PICK + n] for n in range(N_PICK)], axis=1)
        s = jnp.where(member, _dot(q, k_t.astype(bf16)) - slope * (t_pos - pos).astype(f32), NEG)
        o_s = jnp.where(mine, attend(s, member, v_t, new_slc, g), o_s)
        s = _dot(q, wc_ref[0, g].astype(bf16)) - slope * d_w
        o_w = jnp.where(mine, attend(s, None, wc_ref[0, NSA_KV_HEADS + g], new_win, g), o_w)

    gate = jnp.broadcast_to(_sigmoid(gate_ref[0]), (NSA_HEADS, LANES))
    glane = lax.broadcasted_iota(jnp.int32, (NSA_HEADS, LANES), 1)
    gcol = [jnp.sum(jnp.where(glane == 3 * head + j, gate, 0.0), axis=1, keepdims=True) for j in range(3)]
    o_ref[0] = gcol[0] * oc_ref[0] + gcol[1] * o_s + gcol[2] * o_w
    eye = lax.broadcasted_iota(jnp.int32, (HEAD_DIM, HEAD_DIM), 0) == \
        lax.broadcasted_iota(jnp.int32, (HEAD_DIM, HEAD_DIM), 1)
    for c in range(N_COMBO):
        piece = jnp.broadcast_to(new_win[:, c * HEAD_DIM:(c + 1) * HEAD_DIM], (HEAD_DIM, HEAD_DIM))
        new_col = jnp.sum(jnp.where(eye, piece, 0.0), axis=1, keepdims=True)
        wout_ref[0, c] = jnp.where(wlane == wl - 1, new_col, pltpu.roll(wc_ref[0, c], wl - 1, axis=1))


def _nsa_sample_sel(page_table, idx_flat, slc_pages_t, q8, o_c, gates, slc_new, win_new, cache_win_t, t_pos):
    nb = q8.shape[0]
    wl = cache_win_t.shape[3]
    n_blk = NSA_KV_HEADS * N_PICK
    per_seq = lambda b, pt, idx: (b, 0, 0)
    win_blk = pl.BlockSpec((1, N_COMBO, HEAD_DIM, wl), lambda b, pt, idx: (b, 0, 0, 0))
    grid_spec = pltpu.PrefetchScalarGridSpec(
        num_scalar_prefetch=2,
        grid=(nb,),
        in_specs=[pl.BlockSpec(memory_space=pl.ANY),
                  pl.BlockSpec((1, NSA_HEADS, HEAD_DIM), per_seq),
                  pl.BlockSpec((1, NSA_HEADS, HEAD_DIM), per_seq),
                  pl.BlockSpec((1, 1, LANES), per_seq),
                  pl.BlockSpec((1, 1, ROW_WIDTH), per_seq),
                  pl.BlockSpec((1, 1, ROW_WIDTH), per_seq),
                  win_blk],
        out_specs=[pl.BlockSpec((1, NSA_HEADS, HEAD_DIM), per_seq), win_blk],
        scratch_shapes=[pltpu.VMEM((2, 2, n_blk, HEAD_DIM, PAGE_SIZE), f32),
                        pltpu.SemaphoreType.DMA((2,))],
    )
    return pl.pallas_call(
        functools.partial(_nsa_sample_sel_kernel, t_pos=t_pos),
        grid_spec=grid_spec,
        out_shape=[jax.ShapeDtypeStruct((nb, NSA_HEADS, HEAD_DIM), f32),
                   jax.ShapeDtypeStruct(cache_win_t.shape, f32)],
        compiler_params=_cparams(("arbitrary",)),
        name="nsa_sample_sel",
    )(page_table, idx_flat, slc_pages_t, q8, o_c, gates, slc_new, win_new, cache_win_t)


def kernel(x_prompt, x_sample, cache_cmp_kv, cache_slc_kv, cache_win_kv, state_hgrn, page_table, p_prompt, p_sample,
           w_in, g_pre, cmp_pe, cmp_w1, cmp_b1, cmp_w2, hg_lower, hg_norm, w_out, g_post, ple_proj, ple_gate):
    bsz, seq, _ = x_prompt.shape
    nb = x_sample.shape[0]
    n_pool = cache_cmp_kv.shape[1]
    n_pages = page_table.shape[1]
    past = n_pages * PAGE_SIZE
    kv_tail = (2, NSA_KV_HEADS, HEAD_DIM)

    o = _ORIG
    w = w_in[0]
    w_r = jnp.concatenate([w[:, o['q_a']:o['k_cmp']], w[:, o['z_a']:o['end']], w[:, o['k_cmp']:o['gate_a']],
                           w[:, o['gate_a']:o['z_a']],
                           jnp.zeros((D_MODEL, PROJ_WIDTH - o['end']), f32)], axis=1).astype(bf16)
    cw = _compress_weights(cmp_pe[0], cmp_w1[0], cmp_b1[0], cmp_w2[0])
    m_mat = _sel_map_matrix()
    w_out_b, ple_proj_b, ple_gate_b = w_out[0].astype(bf16), ple_proj[0].astype(bf16), ple_gate[0].astype(bf16)

    xp = x_prompt.reshape(bsz * seq, D_MODEL)
    zp, cmp_p, slc_p, win_p, gates_p = _proj(xp, g_pre, w_r, 1024)
    pages_p = cmp_p.reshape(bsz * seq // PAGE_SIZE, PAGE_SIZE, ROW_WIDTH)
    table_p = jnp.arange(bsz * seq // PAGE_SIZE, dtype=jnp.int32).reshape(bsz, seq // PAGE_SIZE)
    comp_p = _compress(pages_p, table_p, *cw, feature_major=False)
    q_t, k_slc, vt_slc, k_win, vt_win, cmp_tp, slc_tp, win_tp = _attention_layout(zp, cmp_p, slc_p, win_p, bsz, seq)
    o_a = _nsa_prompt_blocks(q_t, gates_p, comp_p, k_slc, vt_slc, k_win, vt_win, m_mat.T)
    yb_p, hg_p = _hgrn_prompt(zp, hg_lower, hg_norm, bsz, seq)
    h_p = _finish(xp, o_a, zp, yb_p, p_prompt[0].reshape(bsz * seq, PLE_DIM),
                  w_out_b, g_post, ple_proj_b, ple_gate_b, 512)

    wl_p = min(WINDOW, seq)

    def cache_rows(rows_t):
        return rows_t.reshape((1, bsz) + kv_tail + rows_t.shape[2:]).transpose(0, 1, 5, 2, 3, 4)

    out_prompt = (h_p.reshape(bsz, seq, D_MODEL), cache_rows(cmp_tp), cache_rows(slc_tp),
                  cache_rows(win_tp[:, :, seq - wl_p:]), hg_p[None])

    xs = x_sample.reshape(nb, D_MODEL)
    zs, cmp_s, slc_s, win_new, gates_s = _proj(xs, g_pre, w_r, nb)
    cmp_t = cache_cmp_kv.transpose(0, 1, 3, 4, 5, 2).reshape(n_pool, ROW_WIDTH, PAGE_SIZE)
    comp_s = _compress(cmp_t, page_table, *cw, feature_major=True)
    q8 = zs[:, COL_QA:COL_QA + NSA_WIDTH].astype(bf16).reshape(nb, NSA_HEADS, HEAD_DIM)
    oc_s, ps_s = _nsa_sample_cmp(q8, comp_s, m_mat, past)
    ps_rows = ps_s[:, :NSA_KV_HEADS, :].reshape(nb * NSA_KV_HEADS, N_SELBLK)
    idx = _nsa_sample_topk(ps_rows, past // SEL_BLOCK - 1)
    slc_t = cache_slc_kv.transpose(0, 1, 3, 4, 5, 2).reshape(n_pool, N_COMBO, HEAD_DIM, PAGE_SIZE)
    wl_s = cache_win_kv.shape[2]
    win_t = cache_win_kv.transpose(0, 1, 3, 4, 5, 2).reshape(nb, N_COMBO, HEAD_DIM, wl_s)
    oa_s, win_s = _nsa_sample_sel(page_table, idx[:, :N_SEL].reshape(-1), slc_t, q8, oc_s,
                                  gates_s.reshape(nb, 1, LANES), slc_s.reshape(nb, 1, ROW_WIDTH),
                                  win_new.reshape(nb, 1, ROW_WIDTH), win_t, past)
    win_s = win_s.reshape((1, nb) + kv_tail + (wl_s,)).transpose(0, 1, 5, 2, 3, 4)
    yb_s, hg_s = _hgrn_sample(zs, state_hgrn, hg_lower, hg_norm)
    h_s = _finish(xs, oa_s.reshape(nb, NSA_WIDTH), zs, yb_s, p_sample[0].reshape(nb, PLE_DIM),
                  w_out_b, g_post, ple_proj_b, ple_gate_b, nb)

    out_sample = (h_s.reshape(nb, 1, D_MODEL),
                  cmp_s.reshape((1, nb, 1) + kv_tail),
                  slc_s.reshape((1, nb, 1) + kv_tail),
                  win_s,
                  hg_s)
    return (out_prompt[0], out_sample[0]) + out_prompt[1:] + out_sample[1:]
```

```python
import functools

import numpy as np
import jax
import jax.numpy as jnp
from jax import lax
from jax.experimental import pallas as pl
from jax.experimental.pallas import tpu as pltpu

D_MODEL = 1024
PAGE_SIZE = 128
NSA_HEADS = 8
NSA_KV_HEADS = 2
NSA_GROUP = NSA_HEADS // NSA_KV_HEADS
HEAD_DIM = 64
NSA_WIDTH = NSA_HEADS * HEAD_DIM
KV_WIDTH = NSA_KV_HEADS * HEAD_DIM
CMP_BLOCK = 32
CMP_STRIDE = 16
CMP_RATIO = CMP_BLOCK // CMP_STRIDE
CMP_HIDDEN = 128
SEL_BLOCK = 64
SEL_RATIO = SEL_BLOCK // CMP_STRIDE
N_SEL = 16
WINDOW = 512
Q_BLOCK = 128
HG_HEADS = 4
HG_DIM = 128
HG_WIDTH = HG_HEADS * HG_DIM
PLE_DIM = 256
EPS = 1e-6
NEG = -1e30
FORCE_BONUS = 1e6
REMOVED = -3e38

COL_QA, COL_ZA, COL_QB, COL_FB, COL_IB, COL_ZB = 0, 512, 1024, 1536, 2048, 2560
COL_CMP, COL_SLC, COL_WIN, COL_GATE = 3072, 3328, 3584, 3840
PROJ_WIDTH = 4096
_ORIG = dict(q_a=0, k_cmp=512, gate_a=1280, z_a=1304, q_b=1816, f_b=2328, i_b=2840, z_b=3352, end=3864)

LANES = 128
VMEM_LIMIT = 56 * 1024 * 1024

ROW_WIDTH = 2 * KV_WIDTH
N_COMBO = 2 * NSA_KV_HEADS
SEL_CHUNK = 512
WIN_KEYS = WINDOW + Q_BLOCK
N_CMP_PAD = 512
N_SELBLK = 128

f32 = jnp.float32
bf16 = jnp.bfloat16


def _cparams(sem):
    return pltpu.CompilerParams(dimension_semantics=sem, vmem_limit_bytes=VMEM_LIMIT)


def _dot(a, b):
    return jnp.dot(a, b, preferred_element_type=f32)


def _dot_nt(a, b):
    return lax.dot_general(a, b, (((1,), (1,)), ((), ())), preferred_element_type=f32)


def _dot_tn(a, b):
    return lax.dot_general(a, b, (((0,), (0,)), ((), ())), preferred_element_type=f32)


def _sigmoid(x):
    return 1.0 / (1.0 + jnp.exp(-x))


def _silu(x):
    return x * _sigmoid(x)


PROJ_TILE = 1024
MAIN_TILES = COL_CMP // PROJ_TILE


def _proj_kernel(x_ref, g_ref, w_ref, z_ref, cmp_ref, slc_ref, win_ref, gate_ref, xn_ref):
    j = pl.program_id(1)

    @pl.when(j == 0)
    def _():
        x = x_ref[...]
        y = x * lax.rsqrt(jnp.mean(x * x, axis=-1, keepdims=True) + EPS)
        xn_ref[...] = (y * g_ref[...]).astype(bf16)

    res = _dot(xn_ref[...], w_ref[...])

    @pl.when(j < MAIN_TILES)
    def _():
        z_ref[...] = res

    @pl.when(j == MAIN_TILES)
    def _():
        base = COL_CMP
        cmp_ref[...] = res[:, COL_CMP - base:COL_SLC - base]
        slc_ref[...] = res[:, COL_SLC - base:COL_WIN - base]
        win_ref[...] = res[:, COL_WIN - base:COL_GATE - base]
        gate_ref[...] = res[:, COL_GATE - base:COL_GATE - base + LANES]


def _proj(x, g, w, tm):
    n = x.shape[0]
    row = lambda i, j: (i, 0)
    return pl.pallas_call(
        _proj_kernel,
        grid=(n // tm, PROJ_WIDTH // PROJ_TILE),
        in_specs=[pl.BlockSpec((tm, D_MODEL), row),
                  pl.BlockSpec((1, D_MODEL), lambda i, j: (0, 0)),
                  pl.BlockSpec((D_MODEL, PROJ_TILE), lambda i, j: (0, j))],
        out_specs=[pl.BlockSpec((tm, PROJ_TILE), lambda i, j: (i, jnp.minimum(j, MAIN_TILES - 1))),
                   pl.BlockSpec((tm, ROW_WIDTH), row), pl.BlockSpec((tm, ROW_WIDTH), row),
                   pl.BlockSpec((tm, ROW_WIDTH), row), pl.BlockSpec((tm, LANES), row)],
        out_shape=[jax.ShapeDtypeStruct((n, COL_CMP), f32),
                   jax.ShapeDtypeStruct((n, ROW_WIDTH), f32), jax.ShapeDtypeStruct((n, ROW_WIDTH), f32),
                   jax.ShapeDtypeStruct((n, ROW_WIDTH), f32), jax.ShapeDtypeStruct((n, LANES), f32)],
        scratch_shapes=[pltpu.VMEM((tm, D_MODEL), bf16)],
        compiler_params=_cparams(("parallel", "arbitrary")),
        name="proj",
    )(x, g, w)


def _finish_kernel(x_ref, oa_ref, za_ref, yb_ref, p_ref, wo_ref, gp_ref, pp_ref, pg_ref, h_ref):
    ya = (oa_ref[...] * _silu(za_ref[...])).astype(bf16)
    yb = yb_ref[...].astype(bf16)
    mix = _dot(ya, wo_ref[0:NSA_WIDTH, :]) + _dot(yb, wo_ref[NSA_WIDTH:, :])
    nrm = mix * lax.rsqrt(jnp.mean(mix * mix, axis=-1, keepdims=True) + EPS) * gp_ref[...]
    h = x_ref[...] + nrm
    gate = _sigmoid(_dot(h.astype(bf16), pg_ref[...]))
    h_ref[...] = h + _dot(p_ref[...].astype(bf16), pp_ref[...]) * gate


def _finish(x, oa, z, yb, p, w_out, g_post, ple_proj, ple_gate, tm):
    n = x.shape[0]
    za_blk = COL_ZA // NSA_WIDTH
    row = lambda i: (i, 0)
    const = lambda i: (0, 0)
    return pl.pallas_call(
        _finish_kernel,
        grid=(n // tm,),
        in_specs=[pl.BlockSpec((tm, D_MODEL), row),
                  pl.BlockSpec((tm, NSA_WIDTH), row),
                  pl.BlockSpec((tm, NSA_WIDTH), lambda i: (i, za_blk)),
                  pl.BlockSpec((tm, HG_WIDTH), row),
                  pl.BlockSpec((tm, PLE_DIM), row),
                  pl.BlockSpec((D_MODEL, D_MODEL), const),
                  pl.BlockSpec((1, D_MODEL), const),
                  pl.BlockSpec((PLE_DIM, D_MODEL), const),
                  pl.BlockSpec((D_MODEL, D_MODEL), const)],
        out_specs=pl.BlockSpec((tm, D_MODEL), row),
        out_shape=jax.ShapeDtypeStruct((n, D_MODEL), f32),
        compiler_params=_cparams(("parallel",)),
        name="finish",
    )(x, oa, z, yb, p, w_out, g_post, ple_proj, ple_gate)


SEGS_PER_PAGE = PAGE_SIZE // CMP_STRIDE


def _compress_kernel(pt_ref, pages_ref, wa_ref, wb_ref, pea_ref, peb_ref, b1_ref, w2_ref, out_ref,
                     stage_ref, rows_ref, sem_ref, *, n_pages, feature_major):
    b = pl.program_id(0)
    nb = pl.num_programs(0)
    lane_halves = ROW_WIDTH // LANES
    n_seg = n_pages * SEGS_PER_PAGE

    def page_copy(seq, j):
        s = lax.rem(seq, 2)
        return pltpu.make_async_copy(pages_ref.at[pt_ref[seq, j]], stage_ref.at[s, j], sem_ref.at[s])

    def start_all(seq):
        def body(j, c):
            page_copy(seq, j).start()
            return c
        lax.fori_loop(0, n_pages, body, 0)

    def wait_all(seq):
        def body(j, c):
            page_copy(seq, j).wait()
            return c
        lax.fori_loop(0, n_pages, body, 0)

    def to_rows(s):
        for j in range(n_pages):
            tile = stage_ref[s, j]
            rows = tile.T if feature_major else tile
            for h in range(lane_halves):
                rows_ref[s, h, j * PAGE_SIZE:(j + 1) * PAGE_SIZE, :] = rows[:, h * LANES:(h + 1) * LANES]

    def mlp(s):
        for kind in range(lane_halves):
            cols = slice(kind * LANES, (kind + 1) * LANES)
            seg_a, seg_b = [], []
            for j in range(CMP_STRIDE):
                r = rows_ref[s, kind, pl.ds(j, n_seg, stride=CMP_STRIDE), :]
                seg_a.append((r + pea_ref[j][:, cols]).astype(bf16))
                seg_b.append((r + peb_ref[j][:, cols]).astype(bf16))
            pa = _dot(jnp.concatenate(seg_a, axis=1), wa_ref[kind])
            pb = _dot(jnp.concatenate(seg_b, axis=1), wb_ref[kind])
            pb = pltpu.roll(pb, n_seg - 1, axis=0)
            hid = _silu(pa + pb + b1_ref[kind])
            for g in range(NSA_KV_HEADS):
                hg = hid[:, g * CMP_HIDDEN:(g + 1) * CMP_HIDDEN].astype(bf16)
                out_ref[0, kind * NSA_KV_HEADS + g] = _dot(hg, w2_ref[kind])

    @pl.when(b == 0)
    def _():
        start_all(0)

        @pl.when(nb > 1)
        def _():
            start_all(1)
        wait_all(0)
        to_rows(0)

    @pl.when(b + 2 < nb)
    def _():
        start_all(b + 2)

    @pl.when(b + 1 < nb)
    def _():
        wait_all(b + 1)

    for cur in range(2):
        @pl.when(lax.rem(b, 2) == cur)
        def _(cur=cur):
            to_rows(1 - cur)
            mlp(cur)


def _compress(pages, page_table, wa, wb, pea, peb, b1, w2, feature_major):
    nseq, n_pages = page_table.shape
    n_seg = n_pages * SEGS_PER_PAGE
    const3 = lambda b, pt: (0, 0, 0)
    once = pl.Buffered(1)
    grid_spec = pltpu.PrefetchScalarGridSpec(
        num_scalar_prefetch=1,
        grid=(nseq,),
        in_specs=[pl.BlockSpec(memory_space=pl.ANY),
                  pl.BlockSpec(wa.shape, const3, pipeline_mode=once),
                  pl.BlockSpec(wb.shape, const3, pipeline_mode=once),
                  pl.BlockSpec((CMP_STRIDE, 1, ROW_WIDTH), const3),
                  pl.BlockSpec((CMP_STRIDE, 1, ROW_WIDTH), const3),
                  pl.BlockSpec(b1.shape, const3),
                  pl.BlockSpec((2, CMP_HIDDEN, HEAD_DIM), const3)],
        out_specs=pl.BlockSpec((1, N_COMBO, n_seg, HEAD_DIM), lambda b, pt: (b, 0, 0, 0)),
        scratch_shapes=[pltpu.VMEM((2, n_pages) + pages.shape[1:], f32),
                        pltpu.VMEM((2, ROW_WIDTH // LANES, n_pages * PAGE_SIZE, LANES), f32),
                        pltpu.SemaphoreType.DMA((2,))],
    )
    return pl.pallas_call(
        functools.partial(_compress_kernel, n_pages=n_pages, feature_major=feature_major),
        grid_spec=grid_spec,
        out_shape=jax.ShapeDtypeStruct((nseq, N_COMBO, n_seg, HEAD_DIM), f32),
        compiler_params=_cparams(("arbitrary",)),
        name="compress",
    )(page_table, pages, wa, wb, pea, peb, b1, w2)


def _compress_weights(cmp_pe, cmp_w1, cmp_b1, cmp_w2):
    eye = jnp.eye(NSA_KV_HEADS, dtype=f32)

    def big(w_half):
        t = jnp.einsum('kjdh,gb->kjbdgh', w_half, eye)
        return t.reshape(2, CMP_STRIDE * KV_WIDTH, NSA_KV_HEADS * CMP_HIDDEN)

    w1 = cmp_w1.reshape(2, 2, CMP_STRIDE, HEAD_DIM, CMP_HIDDEN)
    wa = big(w1[:, 0]).astype(bf16)
    wb = big(w1[:, 1]).astype(bf16)
    pe = cmp_pe.reshape(2, 2, CMP_STRIDE, HEAD_DIM)

    def pe_row(p):
        t = jnp.broadcast_to(p.transpose(1, 0, 2)[:, :, None, :],
                             (CMP_STRIDE, 2, NSA_KV_HEADS, HEAD_DIM))
        return t.reshape(CMP_STRIDE, 1, ROW_WIDTH)

    pea, peb = pe_row(pe[:, 0]), pe_row(pe[:, 1])
    b1 = jnp.broadcast_to(cmp_b1[:, None, :], (2, NSA_KV_HEADS, CMP_HIDDEN)).reshape(2, 1, NSA_KV_HEADS * CMP_HIDDEN)
    return wa, wb, pea, peb, b1, cmp_w2.astype(bf16)


def _sel_map_matrix():
    wts = np.convolve(np.ones(SEL_RATIO), np.ones(CMP_RATIO))
    m = np.zeros((N_CMP_PAD, N_SELBLK), np.float32)
    for j in range(N_SELBLK):
        for k, wk in enumerate(wts):
            n = SEL_RATIO * j + k
            if n < N_CMP_PAD - 1:
                m[n, j] = wk
    return jnp.asarray(m, dtype=bf16)


def _importance_to_blocks(imp, m):
    hi = imp.astype(bf16)
    lo = (imp - hi.astype(f32)).astype(bf16)
    return _dot(hi, m) + _dot(lo, m)


LOG2E = 1.4426950408889634
MASK_BIG = 1e30
AUG_ROWS = 16
VT_ROWS = HEAD_DIM + 16
KEY_LOW = 256
N_FORCED = 3


NSA_STEP_BLOCKS = 2


def _nsa_prompt_blocks_kernel(qt_ref, gate_ref, comp_ref, caug_ref, ks_ref, vts_ref, kw_ref, vtw_ref, mt_ref, o_ref,
                              qa_ref, acc_ref, mg_ref, bias_ref, need_ref):
    cols = NSA_GROUP * Q_BLOCK
    n_blocks = qt_ref.shape[2]
    blocks = range(n_blocks)
    heads = range(NSA_KV_HEADS)
    insts = [(qb, g) for qb in blocks for g in heads]
    blk_per_chunk = SEL_CHUNK // SEL_BLOCK
    n_chunk_max = vts_ref.shape[2]

    lane = lax.broadcasted_iota(jnp.int32, (1, cols), 1)
    head = jnp.right_shift(lane, Q_BLOCK.bit_length() - 1)
    blk = lax.broadcasted_iota(jnp.int32, (N_SELBLK, Q_BLOCK), 0)
    qcol = lax.broadcasted_iota(jnp.int32, (N_SELBLK, Q_BLOCK), 1)
    c_end = lax.broadcasted_iota(jnp.int32, (N_CMP_PAD, 1), 0) * CMP_STRIDE + (CMP_BLOCK - 1)
    krow = lax.broadcasted_iota(jnp.int32, (SEL_CHUNK, 1), 0)
    wrow = lax.broadcasted_iota(jnp.int32, (WIN_KEYS, 1), 0)
    arow = lax.broadcasted_iota(jnp.int32, (AUG_ROWS // 2, cols), 0)
    gate_all = _sigmoid(gate_ref[...])

    q0 = [(pl.program_id(1) * n_blocks + qb) * Q_BLOCK for qb in blocks]
    tq = [q0[qb] + jnp.bitwise_and(lane, Q_BLOCK - 1) for qb in blocks]
    last = [q0[qb] // SEL_CHUNK for qb in blocks]
    w0 = [pl.multiple_of(jnp.maximum(q0[qb] - WINDOW, 0), Q_BLOCK) for qb in blocks]
    gate_t = [gate_all[qb * Q_BLOCK:(qb + 1) * Q_BLOCK].T for qb in blocks]

    slope, alibi_rows, pos_rows = [], [], []
    for g in heads:
        c = jnp.exp2(-(head.astype(f32) + 1.0 + NSA_GROUP * g)) * LOG2E
        c_hi = c.astype(bf16).astype(f32)
        c_mid = (c - c_hi).astype(bf16).astype(f32)
        c_lo = (c - c_hi - c_mid).astype(bf16).astype(f32)
        c3 = jnp.where((arow == 0) | (arow == 3), c_hi, jnp.where((arow == 1) | (arow == 4), c_mid, c_lo))
        slope.append(c)
        alibi_rows.append(jnp.where(arow >= 6, 0.0, c3))
        rows = jnp.where(arow >= 6, 0.0, jnp.where(arow >= 3, KEY_LOW * c3, c3))
        pos_rows.append(jnp.concatenate([rows, jnp.zeros_like(rows)], axis=0).astype(bf16))

    def chunk_scores(qb, g, ck, slot=0, live=None):
        k0 = pl.multiple_of(ck * SEL_CHUNK, SEL_CHUNK)
        b8 = bias_ref[qb, g, pl.ds(pl.multiple_of(ck * blk_per_chunk, blk_per_chunk), blk_per_chunk), :]
        if live is not None:
            b8 = jnp.where(live, b8, -MASK_BIG)
        qa_ref[qb, g, slot, HEAD_DIM:HEAD_DIM + AUG_ROWS, :] = jnp.concatenate(
            [jnp.concatenate([b8] * NSA_GROUP, axis=1), alibi_rows[g]], axis=0).astype(bf16)
        return _dot(ks_ref[0, g, pl.ds(k0, SEL_CHUNK), :], qa_ref[qb, g, slot])

    def sel_chunk(qb, g, ck, causal, s):
        k0 = pl.multiple_of(ck * SEL_CHUNK, SEL_CHUNK)
        if causal:
            s = jnp.where(k0 + krow <= tq[qb], s, NEG)
        shift = slope[g] * k0.astype(f32)
        m_old = mg_ref[qb, g] - shift
        m_new = jnp.maximum(m_old, jnp.max(s, axis=0, keepdims=True))
        p = jnp.exp2(s - m_new).astype(bf16)
        acc_ref[qb, g] = acc_ref[qb, g] * jnp.exp2(m_old - m_new) + _dot(vts_ref[0, g, ck], p)
        mg_ref[qb, g] = m_new + shift

    s_cmp, s_win = {}, {}
    for qb, g in insts:
        q_pos = jnp.concatenate([qt_ref[0, g, qb], pos_rows[g],
                                 jnp.zeros((LANES - HEAD_DIM - AUG_ROWS, cols), bf16)], axis=0)
        kc = jnp.concatenate([comp_ref[0, g].astype(bf16), caug_ref[...]], axis=1)
        s_cmp[qb, g] = _dot(kc, q_pos)
        kw = kw_ref[0, g, pl.ds(w0[qb], WIN_KEYS), :]
        s_win[qb, g] = _dot(kw, q_pos)

    oc_t, ps_t = {}, {}
    for qb in blocks:
        seen = c_end <= tq[qb]
        any_seen = tq[qb] >= CMP_BLOCK - 1
        for g in heads:
            s_c = jnp.where(seen, s_cmp[qb, g], NEG)
            e = jnp.exp2(s_c - jnp.max(s_c, axis=0, keepdims=True))
            rl = jnp.where(any_seen, 1.0 / jnp.maximum(jnp.sum(e, axis=0, keepdims=True), 1e-30), 0.0)
            p_c = e * rl
            oc_t[qb, g] = _dot_tn(comp_ref[0, NSA_KV_HEADS + g].astype(bf16), p_c.astype(bf16))
            imp_t = (p_c[:, 0:Q_BLOCK] + p_c[:, Q_BLOCK:2 * Q_BLOCK]
                     + p_c[:, 2 * Q_BLOCK:3 * Q_BLOCK] + p_c[:, 3 * Q_BLOCK:4 * Q_BLOCK])
            hi = imp_t.astype(bf16)
            lo = (imp_t - hi.astype(f32)).astype(bf16)
            ps_t[qb, g] = _dot(mt_ref[...], hi) + _dot(mt_ref[...], lo)

    ow_t = {}
    for qb in blocks:
        wkey = w0[qb] + wrow
        in_window = (wkey <= tq[qb]) & (wkey >= tq[qb] - WINDOW)
        t0 = w0[qb] // Q_BLOCK
        for g in heads:
            s_w = jnp.where(in_window, s_win[qb, g], NEG)
            e = jnp.exp2(s_w - jnp.max(s_w, axis=0, keepdims=True)).astype(bf16)
            acc_w = jnp.zeros((VT_ROWS, cols), f32)
            for t in range(WIN_KEYS // Q_BLOCK):
                acc_w = acc_w + _dot(vtw_ref[0, g, t0 + t], e[t * Q_BLOCK:(t + 1) * Q_BLOCK])
            ow_t[qb, g] = acc_w[0:HEAD_DIM] * (1.0 / jnp.maximum(acc_w[HEAD_DIM:HEAD_DIM + 1], 1e-30))

    valid, sel_t, score = {}, {}, {}
    for qb in blocks:
        jt = jnp.right_shift(q0[qb] + qcol, SEL_BLOCK.bit_length() - 1)
        valid[qb] = blk <= jt
        forced = (blk == 0) | (blk == jt) | (blk == jt - 1)
        for g in heads:
            sel_t[qb, g] = jnp.where(forced, 1.0, 0.0)
            score[qb, g] = jnp.where(valid[qb] & jnp.logical_not(forced), ps_t[qb, g], NEG)
    for _ in range(N_SEL - N_FORCED):
        for inst in insts:
            best = jnp.max(score[inst], axis=0, keepdims=True)
            idx = jnp.min(jnp.where(score[inst] == best, blk, N_SELBLK), axis=0, keepdims=True)
            hit = blk == idx
            sel_t[inst] = jnp.where(hit, 1.0, sel_t[inst])
            score[inst] = jnp.where(hit, REMOVED, score[inst])
    for qb, g in insts:
        sel_t[qb, g] = jnp.where(valid[qb], sel_t[qb, g], 0.0)

    for qb, g in insts:
        bias_ref[qb, g] = (sel_t[qb, g] - 1.0) * MASK_BIG
        for slot in range(qa_ref.shape[2]):
            qa_ref[qb, g, slot, 0:HEAD_DIM, :] = qt_ref[0, g, qb]
            qa_ref[qb, g, slot, HEAD_DIM + AUG_ROWS:, :] = jnp.zeros((LANES - HEAD_DIM - AUG_ROWS, cols), bf16)
        acc_ref[qb, g] = jnp.zeros(acc_ref.shape[2:], f32)
        mg_ref[qb, g] = jnp.full(mg_ref.shape[2:], NEG, f32)
    own = {(qb, g): chunk_scores(qb, g, last[qb]) for qb, g in insts}
    for qb, g in insts:
        sel_chunk(qb, g, last[qb], True, own[qb, g])

    n_need = []
    for qb in blocks:
        count = jnp.int32(0)
        for ck in range(n_chunk_max):
            rows = slice(ck * blk_per_chunk, (ck + 1) * blk_per_chunk)
            picked = jnp.maximum(*[jnp.max(sel_t[qb, g][rows, :]) for g in heads])
            need_ref[qb, count] = ck
            count = count + ((picked > 0.5) & (ck < last[qb])).astype(jnp.int32)
        n_need.append(count)

    for qb in blocks:
        def chunk_pair(i, carry, qb=qb):
            first = 2 * i
            live = first + 1 < n_need[qb]
            cks = [need_ref[qb, first], need_ref[qb, jnp.minimum(first + 1, n_need[qb] - 1)]]
            scores = [[chunk_scores(qb, g, cks[slot], slot, live if slot else None) for g in heads]
                      for slot in range(2)]
            for slot in range(2):
                for g in heads:
                    sel_chunk(qb, g, cks[slot], False, scores[slot][g])
            return carry

        lax.fori_loop(0, (n_need[qb] + 1) // 2, chunk_pair, 0)

    for qb, g in insts:
        acc = acc_ref[qb, g]
        os_t = acc[0:HEAD_DIM] * (1.0 / jnp.maximum(acc[HEAD_DIM:HEAD_DIM + 1], 1e-30))
        outs = []
        for r in range(NSA_GROUP):
            cs = slice(r * Q_BLOCK, (r + 1) * Q_BLOCK)
            row = (g * NSA_GROUP + r) * 3
            outs.append(gate_t[qb][row:row + 1, :] * oc_t[qb, g][:, cs] + gate_t[qb][row + 1:row + 2, :] * os_t[:, cs]
                        + gate_t[qb][row + 2:row + 3, :] * ow_t[qb, g][:, cs])
        for pair in range(NSA_GROUP // 2):
            both = jnp.concatenate([outs[2 * pair], outs[2 * pair + 1]], axis=0)
            lane0 = (g * NSA_GROUP // 2 + pair) * LANES
            o_ref[qb * Q_BLOCK:(qb + 1) * Q_BLOCK, lane0:lane0 + LANES] = both.T


def _nsa_prompt_blocks(q_t, gates, comp, k_slc, vt_slc, k_win, vt_win, m_t):
    bsz, _, nq = q_t.shape[:3]
    seq = k_slc.shape[2]
    cols = NSA_GROUP * Q_BLOCK
    nblk = NSA_STEP_BLOCKS
    steps = nq // nblk
    per_b4 = lambda b, i: (b, 0, 0, 0)
    per_b5 = lambda b, i: (b, 0, 0, 0, 0)
    inst = (nblk, NSA_KV_HEADS)
    return pl.pallas_call(
        _nsa_prompt_blocks_kernel,
        grid=(bsz, steps),
        in_specs=[pl.BlockSpec((1, NSA_KV_HEADS, nblk, HEAD_DIM, cols), lambda b, i: (b, 0, i, 0, 0)),
                  pl.BlockSpec((nblk * Q_BLOCK, LANES), lambda b, i: (b * steps + i, 0)),
                  pl.BlockSpec((1, N_COMBO, N_CMP_PAD, HEAD_DIM), per_b4),
                  pl.BlockSpec((N_CMP_PAD, HEAD_DIM), lambda b, i: (0, 0)),
                  pl.BlockSpec((1, NSA_KV_HEADS, seq, LANES), per_b4),
                  pl.BlockSpec((1,) + vt_slc.shape[1:], per_b5),
                  pl.BlockSpec((1, NSA_KV_HEADS, seq, LANES), per_b4),
                  pl.BlockSpec((1,) + vt_win.shape[1:], per_b5),
                  pl.BlockSpec((N_SELBLK, N_CMP_PAD), lambda b, i: (0, 0))],
        out_specs=pl.BlockSpec((nblk * Q_BLOCK, NSA_WIDTH), lambda b, i: (b * steps + i, 0)),
        out_shape=jax.ShapeDtypeStruct((bsz * seq, NSA_WIDTH), f32),
        scratch_shapes=[pltpu.VMEM(inst + (2, LANES, cols), bf16),
                        pltpu.VMEM(inst + (VT_ROWS, cols), f32),
                        pltpu.VMEM(inst + (1, cols), f32),
                        pltpu.VMEM(inst + (N_SELBLK, Q_BLOCK), f32),
                        pltpu.SMEM((nblk, vt_slc.shape[2]), jnp.int32)],
        compiler_params=_cparams(("parallel", "arbitrary")),
        name="nsa_prompt",
    )(q_t, gates, comp, _cmp_position_columns(), k_slc, vt_slc, k_win, vt_win, m_t)


def _position_columns(pos):
    low, high = (pos % KEY_LOW)[:, None], (pos // KEY_LOW)[:, None]
    return np.concatenate([low, low, low, high, high, high], axis=1).astype(np.float32)


def _cmp_position_columns():
    aug = np.zeros((N_CMP_PAD, HEAD_DIM), np.float32)
    aug[:, 0:6] = _position_columns(np.arange(N_CMP_PAD) * CMP_STRIDE + CMP_BLOCK - 1)
    return jnp.asarray(aug, dtype=bf16)


LAYOUT_ROWS = 1024


def _attention_layout_kernel(q_ref, cmp_ref, slc_ref, win_ref, xs_ref, xw_ref,
                             qt_ref, ks_ref, vts_ref, kw_ref, vtw_ref, cmp_t_ref, slc_t_ref, win_t_ref):
    cmp_t_ref[0] = cmp_ref[...].T
    tm = q_ref.shape[0]
    q = q_ref[...] * (HEAD_DIM ** -0.5 * LOG2E)
    for qb in range(tm // Q_BLOCK):
        rows = slice(qb * Q_BLOCK, (qb + 1) * Q_BLOCK)
        for pair in range(NSA_HEADS // 2):
            t = q[rows, pair * LANES:(pair + 1) * LANES].T
            for k in range(2):
                g, r = divmod(2 * pair + k, NSA_GROUP)
                qt_ref[0, g, qb, :, r * Q_BLOCK:(r + 1) * Q_BLOCK] = t[k * HEAD_DIM:(k + 1) * HEAD_DIM].astype(bf16)

    for src_ref, extra_ref, k_ref, vt_ref, t_ref in ((slc_ref, xs_ref, ks_ref, vts_ref, slc_t_ref),
                                                     (win_ref, xw_ref, kw_ref, vtw_ref, win_t_ref)):
        tile = vt_ref.shape[4]
        s = src_ref[...]
        s_t = s.T
        t_ref[0] = s_t
        v_t = s_t[KV_WIDTH:]
        tail = jnp.where(lax.broadcasted_iota(jnp.int32, (VT_ROWS - HEAD_DIM, tile), 0) == 0, 1.0, 0.0)
        for g in range(NSA_KV_HEADS):
            k_ref[0, g] = jnp.concatenate([s[:, g * HEAD_DIM:(g + 1) * HEAD_DIM].astype(bf16), extra_ref[...]], axis=1)
            for c in range(tm // tile):
                blk = v_t[g * HEAD_DIM:(g + 1) * HEAD_DIM, c * tile:(c + 1) * tile]
                vt_ref[0, g, c] = jnp.concatenate([blk, tail], axis=0).astype(bf16)


def _attention_layout(z, cmp, slc, win, bsz, seq):
    tm = LAYOUT_ROWS
    per_b = seq // tm
    pos = np.arange(seq)
    xs = np.zeros((seq, HEAD_DIM), np.float32)
    xs[pos, (pos // SEL_BLOCK) % (SEL_CHUNK // SEL_BLOCK)] = 1.0
    xs[:, 8:11] = (pos % KEY_LOW)[:, None]
    xs[:, 11:14] = (pos % SEL_CHUNK - pos % KEY_LOW)[:, None]
    xw = np.zeros((seq, HEAD_DIM), np.float32)
    xw[:, 0:6] = _position_columns(pos)
    row = lambda i: (i, 0)
    rep = lambda i: (i % per_b, 0)
    out4 = lambda i: (i // per_b, 0, i % per_b, 0)
    out5 = lambda i: (i // per_b, 0, i % per_b, 0, 0)
    k_shape = jax.ShapeDtypeStruct((bsz, NSA_KV_HEADS, seq, LANES), bf16)
    vt_shape = lambda tile: jax.ShapeDtypeStruct((bsz, NSA_KV_HEADS, seq // tile, VT_ROWS, tile), bf16)
    vt_spec = lambda tile: pl.BlockSpec((1, NSA_KV_HEADS, tm // tile, VT_ROWS, tile), out5)
    rows_t_shape = jax.ShapeDtypeStruct((bsz, ROW_WIDTH, seq), f32)
    rows_t_spec = pl.BlockSpec((1, ROW_WIDTH, tm), lambda i: (i // per_b, 0, i % per_b))
    return pl.pallas_call(
        _attention_layout_kernel,
        grid=(bsz * per_b,),
        in_specs=[pl.BlockSpec((tm, NSA_WIDTH), row), pl.BlockSpec((tm, ROW_WIDTH), row),
                  pl.BlockSpec((tm, ROW_WIDTH), row), pl.BlockSpec((tm, ROW_WIDTH), row),
                  pl.BlockSpec((tm, HEAD_DIM), rep), pl.BlockSpec((tm, HEAD_DIM), rep)],
        out_specs=[pl.BlockSpec((1, NSA_KV_HEADS, tm // Q_BLOCK, HEAD_DIM, NSA_GROUP * Q_BLOCK), out5),
                   pl.BlockSpec((1, NSA_KV_HEADS, tm, LANES), out4), vt_spec(SEL_CHUNK),
                   pl.BlockSpec((1, NSA_KV_HEADS, tm, LANES), out4), vt_spec(Q_BLOCK),
                   rows_t_spec, rows_t_spec, rows_t_spec],
        out_shape=[jax.ShapeDtypeStruct((bsz, NSA_KV_HEADS, seq // Q_BLOCK, HEAD_DIM, NSA_GROUP * Q_BLOCK), bf16),
                   k_shape, vt_shape(SEL_CHUNK), k_shape, vt_shape(Q_BLOCK),
                   rows_t_shape, rows_t_shape, rows_t_shape],
        compiler_params=_cparams(("parallel",)),
        name="attention_layout",
    )(z, cmp, slc, win, jnp.asarray(xs, dtype=bf16), jnp.asarray(xw, dtype=bf16))


HG_CHUNK = 128
HG_STEP_CHUNKS = 4
HG_BAND = 8


def _lower_bound(lo):
    m = jnp.max(lo, axis=0, keepdims=True)
    e = jnp.exp(lo - m)
    return e[0:1] / jnp.sum(e, axis=0, keepdims=True)


def _hgrn_gates(q, fl, lb):
    sg = _sigmoid(fl)
    f = lb + (1.0 - lb) * sg
    k = (1.0 - lb) * (1.0 - sg)
    return _silu(q), k, jnp.log(f)


def _hgrn_out(o, nw, zb):
    o = o * lax.rsqrt(jnp.mean(o * o, axis=-1, keepdims=True) + EPS) * nw
    return o * _silu(zb)


def _hgrn_prompt_kernel(q_ref, f_ref, i_ref, zb_ref, lo_ref, nw_ref, y_ref, s_ref, st_ref):
    c = pl.program_id(1)

    @pl.when(c == 0)
    def _():
        st_ref[...] = jnp.zeros_like(st_ref)

    n = HG_CHUNK
    row = lax.broadcasted_iota(jnp.int32, (n, n), 0)
    col = lax.broadcasted_iota(jnp.int32, (n, n), 1)
    heads = [slice(h * HG_DIM, (h + 1) * HG_DIM) for h in range(HG_HEADS)]

    qq, kk, logf = _hgrn_gates(q_ref[...], f_ref[...], _lower_bound(lo_ref[...]))
    v = i_ref[...]
    chunks = [slice(i * n, (i + 1) * n) for i in range(q_ref.shape[0] // n)]
    tril = jnp.where(col <= row, 1.0, 0.0).astype(bf16)
    hi = logf.astype(bf16)
    r1 = logf - hi.astype(f32)
    mid = r1.astype(bf16)
    lo = (r1 - mid.astype(f32)).astype(bf16)
    bs = [_dot(tril, hi[rs]) + _dot(tril, mid[rs]) + _dot(tril, lo[rs]) for rs in chunks]

    intra = [[_hgrn_intra(qq[rs, hs], kk[rs, hs], b[:, hs], row, col) for hs in heads] for rs, b in zip(chunks, bs)]
    for i, rs in enumerate(chunks):
        for h, hs in enumerate(heads):
            st = st_ref[h]
            bh, kh, vh = bs[i][:, hs], kk[rs, hs], v[rs, hs].astype(bf16)
            o = _dot(intra[i][h].astype(bf16), vh) + _dot_nt((qq[rs, hs] * jnp.exp(bh)).astype(bf16), st.astype(bf16))
            b_last = bh[n - 1:n]
            st_new = st * jnp.exp(b_last) + _dot_tn(vh, (kh * jnp.exp(b_last - bh)).astype(bf16))
            st_ref[h] = st_new
            y_ref[rs, hs] = _hgrn_out(o, nw_ref[...], zb_ref[rs, hs])

    @pl.when(c == pl.num_programs(1) - 1)
    def _():
        for h in range(HG_HEADS):
            s_ref[0, h] = st_ref[h].T


def _hgrn_intra(qq, kk, b, row, col):
    n = HG_CHUNK
    a = jnp.where(row == col, _dot_nt(qq.astype(bf16), kk.astype(bf16)), 0.0)

    sub = jnp.bitwise_and(row, HG_BAND - 1)
    b3 = b.reshape(n // HG_BAND, HG_BAND, HG_DIM)
    size = 2
    while size <= HG_BAND:
        half = size // 2
        beta = None
        for s0 in range(0, HG_BAND, size):
            piece = jnp.broadcast_to(b3[:, s0 + half - 1:s0 + half, :], b3.shape).reshape(n, HG_DIM)
            beta = piece if beta is None else jnp.where(sub >= s0, piece, beta)
        left = jnp.bitwise_and(sub, size - 1) < half
        decay = jnp.exp(jnp.where(left, beta - b, b - beta))
        part = _dot_nt(jnp.where(left, 0.0, qq * decay).astype(bf16), jnp.where(left, kk * decay, 0.0).astype(bf16))
        sh = size.bit_length() - 1
        a = a + jnp.where(jnp.right_shift(row, sh) == jnp.right_shift(col, sh), part, 0.0)
        size *= 2

    while size <= n:
        half = size // 2
        qs, ks = [], []
        zero = jnp.zeros((half, HG_DIM), f32)
        for r0 in range(0, n, size):
            beta = b[r0 + half - 1:r0 + half]
            ks += [kk[r0:r0 + half] * jnp.exp(beta - b[r0:r0 + half]), zero]
            qs += [zero, qq[r0 + half:r0 + size] * jnp.exp(b[r0 + half:r0 + size] - beta)]
        part = _dot_nt(jnp.concatenate(qs, axis=0).astype(bf16), jnp.concatenate(ks, axis=0).astype(bf16))
        if size < n:
            sh = size.bit_length() - 1
            part = jnp.where(jnp.right_shift(row, sh) == jnp.right_shift(col, sh), part, 0.0)
        a = a + part
        size *= 2
    return a


def _hgrn_prompt(z, hg_lower, hg_norm, bsz, seq):
    rows = HG_STEP_CHUNKS * HG_CHUNK
    nch = seq // rows

    def zcol(base):
        return pl.BlockSpec((rows, HG_WIDTH), lambda b, c: (b * nch + c, base // HG_WIDTH))

    return pl.pallas_call(
        _hgrn_prompt_kernel,
        grid=(bsz, nch),
        in_specs=[zcol(COL_QB), zcol(COL_FB), zcol(COL_IB), zcol(COL_ZB),
                  pl.BlockSpec(hg_lower.shape, lambda b, c: (0, 0)),
                  pl.BlockSpec((1, HG_DIM), lambda b, c: (0, 0))],
        out_specs=[pl.BlockSpec((rows, HG_WIDTH), lambda b, c: (b * nch + c, 0)),
                   pl.BlockSpec((1, HG_HEADS, HG_DIM, HG_DIM), lambda b, c: (b, 0, 0, 0))],
        out_shape=[jax.ShapeDtypeStruct((bsz * seq, HG_WIDTH), f32),
                   jax.ShapeDtypeStruct((bsz, HG_HEADS, HG_DIM, HG_DIM), f32)],
        scratch_shapes=[pltpu.VMEM((HG_HEADS, HG_DIM, HG_DIM), f32)],
        compiler_params=_cparams(("parallel", "arbitrary")),
        name="hgrn_prompt",
    )(z, z, z, z, hg_lower, hg_norm)


def _hgrn_sample_kernel(q_ref, f_ref, i_ref, zb_ref, lo_ref, nw_ref, s_ref, y_ref, so_ref, o_scr):
    nb = q_ref.shape[0]
    lb = _lower_bound(lo_ref[...])
    qq, kk, logf = _hgrn_gates(q_ref[...], f_ref[...], lb)
    v = i_ref[...]
    pad = jnp.zeros((HG_DIM - nb, HG_DIM), f32)
    to_cols = lambda x: jnp.concatenate([x, pad], axis=0).T
    f_t, k_t, q_t = to_cols(jnp.exp(logf)), to_cols(kk), to_cols(qq)
    for b in range(nb):
        s_new = f_t[:, b:b + 1] * s_ref[0, b, 0] + k_t[:, b:b + 1] * v[b:b + 1, :]
        so_ref[0, b, 0] = s_new
        o_scr[b:b + 1, :] = jnp.sum(s_new * q_t[:, b:b + 1], axis=0, keepdims=True)
    y_ref[...] = _hgrn_out(o_scr[...], nw_ref[...], zb_ref[...])


def _hgrn_sample(z, state, hg_lower, hg_norm):
    nb = z.shape[0]

    def zcol(base):
        return pl.BlockSpec((nb, HG_DIM), lambda h: (0, base // HG_DIM + h))

    sblk = pl.BlockSpec((1, nb, 1, HG_DIM, HG_DIM), lambda h: (0, 0, h, 0, 0))
    return pl.pallas_call(
        _hgrn_sample_kernel,
        grid=(HG_HEADS,),
        in_specs=[zcol(COL_QB), zcol(COL_FB), zcol(COL_IB), zcol(COL_ZB),
                  pl.BlockSpec((hg_lower.shape[0], HG_DIM), lambda h: (0, h)),
                  pl.BlockSpec((1, HG_DIM), lambda h: (0, 0)),
                  sblk],
        out_specs=[pl.BlockSpec((nb, HG_DIM), lambda h: (0, h)), sblk],
        out_shape=[jax.ShapeDtypeStruct((nb, HG_WIDTH), f32),
                   jax.ShapeDtypeStruct(state.shape, f32)],
        scratch_shapes=[pltpu.VMEM((nb, HG_DIM), f32)],
        compiler_params=_cparams(("parallel",)),
        name="hgrn_sample",
    )(z, z, z, z, hg_lower, hg_norm, state)


N_PICK = N_SEL - 1


def _head_slopes():
    h = lax.broadcasted_iota(jnp.int32, (NSA_HEADS, 1), 0).astype(f32)
    return jnp.exp2(-(h + 1.0))


def _nsa_sample_cmp_kernel(q_ref, comp_ref, m_ref, oc_ref, ps_ref, *, t_pos):
    q = q_ref[0] * jnp.asarray(HEAD_DIM ** -0.5, bf16)
    slope = _head_slopes()
    head = lax.broadcasted_iota(jnp.int32, (NSA_HEADS, 1), 0)
    ncol = lax.broadcasted_iota(jnp.int32, (1, N_CMP_PAD), 1)
    d_c = t_pos - (ncol * CMP_STRIDE + (CMP_BLOCK - 1))
    mask = d_c >= 0
    o_c = jnp.zeros((NSA_HEADS, HEAD_DIM), f32)
    ps_ref[...] = jnp.zeros_like(ps_ref)
    for g in range(NSA_KV_HEADS):
        kc = comp_ref[0, g].astype(bf16)
        vc = comp_ref[0, NSA_KV_HEADS + g].astype(bf16)
        s = jnp.where(mask, _dot_nt(q, kc) - slope * d_c.astype(f32), NEG)
        m = jnp.max(s, axis=1, keepdims=True)
        e = jnp.where(mask, jnp.exp(s - m), 0.0)
        p = e * (1.0 / jnp.maximum(jnp.sum(e, axis=1, keepdims=True), 1e-30))
        mine = jnp.right_shift(head, NSA_GROUP.bit_length() - 1) == g
        o_c = jnp.where(mine, _dot(p.astype(bf16), vc), o_c)
        imp = jnp.sum(jnp.where(mine, p, 0.0), axis=0, keepdims=True)
        ps_ref[0, g:g + 1, :] = _importance_to_blocks(imp, m_ref[...])
    oc_ref[0] = o_c


def _nsa_sample_cmp(q8, comp, m_mat, t_pos):
    nb = q8.shape[0]
    return pl.pallas_call(
        functools.partial(_nsa_sample_cmp_kernel, t_pos=t_pos),
        grid=(nb,),
        in_specs=[pl.BlockSpec((1, NSA_HEADS, HEAD_DIM), lambda b: (b, 0, 0)),
                  pl.BlockSpec((1, N_COMBO, N_CMP_PAD, HEAD_DIM), lambda b: (b, 0, 0, 0)),
                  pl.BlockSpec((N_CMP_PAD, N_SELBLK), lambda b: (0, 0))],
        out_specs=[pl.BlockSpec((1, NSA_HEADS, HEAD_DIM), lambda b: (b, 0, 0)),
                   pl.BlockSpec((1, 8, N_SELBLK), lambda b: (b, 0, 0))],
        out_shape=[jax.ShapeDtypeStruct((nb, NSA_HEADS, HEAD_DIM), f32),
                   jax.ShapeDtypeStruct((nb, 8, N_SELBLK), f32)],
        compiler_params=_cparams(("parallel",)),
        name="nsa_sample_cmp",
    )(q8, comp, m_mat)


def _nsa_sample_topk_kernel(ps_ref, idx_ref, *, last_blk):
    ps = ps_ref[...]
    lane = lax.broadcasted_iota(jnp.int32, ps.shape, 1)
    score = ps + jnp.where((lane == 0) | (lane == last_blk), FORCE_BONUS, 0.0)
    out = jnp.zeros(ps.shape, jnp.int32)
    for r in range(N_PICK):
        best = jnp.max(score, axis=1, keepdims=True)
        idx = jnp.min(jnp.where(score == best, lane, N_SELBLK), axis=1, keepdims=True)
        out = jnp.where(lane == r, idx, out)
        score = jnp.where(lane == idx, REMOVED, score)
    idx_ref[...] = out


def _nsa_sample_topk(ps_rows, last_blk):
    return pl.pallas_call(
        functools.partial(_nsa_sample_topk_kernel, last_blk=last_blk),
        out_shape=jax.ShapeDtypeStruct(ps_rows.shape, jnp.int32),
        name="nsa_sample_topk",
    )(ps_rows)


def _nsa_sample_sel_kernel(pt_ref, idx_ref, slc_ref, q_ref, oc_ref, gate_ref, slc_new_ref, win_new_ref, wc_ref,
                           o_ref, wout_ref, kv_scr, sem_ref, *, t_pos):
    n_blk = NSA_KV_HEADS * N_PICK
    halves = PAGE_SIZE // SEL_BLOCK
    b = pl.program_id(0)
    nb = pl.num_programs(0)
    slot = lax.rem(b, 2)

    def picked(seq, k):
        return idx_ref[(seq * NSA_KV_HEADS + k // N_PICK) * N_SEL + k % N_PICK]

    def blk_copies(seq, s, k):
        page = pt_ref[seq, picked(seq, k) // halves]
        g = k // N_PICK
        return [pltpu.make_async_copy(slc_ref.at[page, kind * NSA_KV_HEADS + g], kv_scr.at[s, kind, k],
                                      sem_ref.at[s]) for kind in range(2)]

    def start_all(seq, s):
        for k in range(n_blk):
            for cp in blk_copies(seq, s, k):
                cp.start()

    @pl.when(b == 0)
    def _():
        start_all(0, 0)

    @pl.when(b + 1 < nb)
    def _():
        start_all(b + 1, 1 - slot)

    for k in range(n_blk):
        for cp in blk_copies(b, slot, k):
            cp.wait()

    q = q_ref[0] * jnp.asarray(HEAD_DIM ** -0.5, bf16)
    slope = _head_slopes()
    head = lax.broadcasted_iota(jnp.int32, (NSA_HEADS, 1), 0)
    lane = lax.broadcasted_iota(jnp.int32, (1, PAGE_SIZE), 1)
    new_slc = slc_new_ref[0]
    new_win = win_new_ref[0]
    wl = wc_ref.shape[3]
    wlane = lax.broadcasted_iota(jnp.int32, (1, wl), 1)
    d_w = (wl - wlane).astype(f32)
    qf = q.astype(f32)

    def attend(s, mask, v_t, new_row, g):
        k_new = new_row[:, g * HEAD_DIM:(g + 1) * HEAD_DIM].astype(bf16).astype(f32)
        v_new = new_row[:, KV_WIDTH + g * HEAD_DIM:KV_WIDTH + (g + 1) * HEAD_DIM].astype(bf16).astype(f32)
        s_new = jnp.sum(qf * k_new, axis=1, keepdims=True)
        m = jnp.maximum(jnp.max(s, axis=1, keepdims=True), s_new)
        e, e_new = jnp.exp(s - m), jnp.exp(s_new - m)
        if mask is not None:
            e = jnp.where(mask, e, 0.0)
        den = jnp.maximum(jnp.sum(e, axis=1, keepdims=True) + e_new, 1e-30)
        num = _dot_nt(e.astype(bf16), v_t.astype(bf16)) + e_new.astype(bf16).astype(f32) * v_new
        return num * (1.0 / den)

    o_s = jnp.zeros((NSA_HEADS, HEAD_DIM), f32)
    o_w = jnp.zeros((NSA_HEADS, HEAD_DIM), f32)
    for g in range(NSA_KV_HEADS):
        mine = jnp.right_shift(head, NSA_GROUP.bit_length() - 1) == g
        pos, member = [], []
        for n in range(N_PICK):
            j = picked(b, g * N_PICK + n)
            pos.append((j // halves) * PAGE_SIZE + lane)
            member.append(jnp.right_shift(lane, SEL_BLOCK.bit_length() - 1) == j % halves)
        pos = jnp.concatenate(pos, axis=1)
        member = jnp.concatenate(member, axis=1)
        k_t = jnp.concatenate([kv_scr[slot, 0, g * N_PICK + n] for n in range(N_PICK)], axis=1)
        v_t = jnp.concatenate([kv_scr[slot, 1, g * N_PICK + n] for n in range(N_PICK)], axis=1)
        s = jnp.where(member, _dot(q, k_t.astype(bf16)) - slope * (t_pos - pos).astype(f32), NEG)
        o_s = jnp.where(mine, attend(s, member, v_t, new_slc, g), o_s)
        s = _dot(q, wc_ref[0, g].astype(bf16)) - slope * d_w
        o_w = jnp.where(mine, attend(s, None, wc_ref[0, NSA_KV_HEADS + g], new_win, g), o_w)

    gate = jnp.broadcast_to(_sigmoid(gate_ref[0]), (NSA_HEADS, LANES))
    glane = lax.broadcasted_iota(jnp.int32, (NSA_HEADS, LANES), 1)
    gcol = [jnp.sum(jnp.where(glane == 3 * head + j, gate, 0.0), axis=1, keepdims=True) for j in range(3)]
    o_ref[0] = gcol[0] * oc_ref[0] + gcol[1] * o_s + gcol[2] * o_w
    eye = lax.broadcasted_iota(jnp.int32, (HEAD_DIM, HEAD_DIM), 0) == \
        lax.broadcasted_iota(jnp.int32, (HEAD_DIM, HEAD_DIM), 1)
    for c in range(N_COMBO):
        piece = jnp.broadcast_to(new_win[:, c * HEAD_DIM:(c + 1) * HEAD_DIM], (HEAD_DIM, HEAD_DIM))
        new_col = jnp.sum(jnp.where(eye, piece, 0.0), axis=1, keepdims=True)
        wout_ref[0, c] = jnp.where(wlane == wl - 1, new_col, pltpu.roll(wc_ref[0, c], wl - 1, axis=1))


def _nsa_sample_sel(page_table, idx_flat, slc_pages_t, q8, o_c, gates, slc_new, win_new, cache_win_t, t_pos):
    nb = q8.shape[0]
    wl = cache_win_t.shape[3]
    n_blk = NSA_KV_HEADS * N_PICK
    per_seq = lambda b, pt, idx: (b, 0, 0)
    win_blk = pl.BlockSpec((1, N_COMBO, HEAD_DIM, wl), lambda b, pt, idx: (b, 0, 0, 0))
    grid_spec = pltpu.PrefetchScalarGridSpec(
        num_scalar_prefetch=2,
        grid=(nb,),
        in_specs=[pl.BlockSpec(memory_space=pl.ANY),
                  pl.BlockSpec((1, NSA_HEADS, HEAD_DIM), per_seq),
                  pl.BlockSpec((1, NSA_HEADS, HEAD_DIM), per_seq),
                  pl.BlockSpec((1, 1, LANES), per_seq),
                  pl.BlockSpec((1, 1, ROW_WIDTH), per_seq),
                  pl.BlockSpec((1, 1, ROW_WIDTH), per_seq),
                  win_blk],
        out_specs=[pl.BlockSpec((1, NSA_HEADS, HEAD_DIM), per_seq), win_blk],
        scratch_shapes=[pltpu.VMEM((2, 2, n_blk, HEAD_DIM, PAGE_SIZE), f32),
                        pltpu.SemaphoreType.DMA((2,))],
    )
    return pl.pallas_call(
        functools.partial(_nsa_sample_sel_kernel, t_pos=t_pos),
        grid_spec=grid_spec,
        out_shape=[jax.ShapeDtypeStruct((nb, NSA_HEADS, HEAD_DIM), f32),
                   jax.ShapeDtypeStruct(cache_win_t.shape, f32)],
        compiler_params=_cparams(("arbitrary",)),
        name="nsa_sample_sel",
    )(page_table, idx_flat, slc_pages_t, q8, o_c, gates, slc_new, win_new, cache_win_t)


def kernel(x_prompt, x_sample, cache_cmp_kv, cache_slc_kv, cache_win_kv, state_hgrn, page_table, p_prompt, p_sample,
           w_in, g_pre, cmp_pe, cmp_w1, cmp_b1, cmp_w2, hg_lower, hg_norm, w_out, g_post, ple_proj, ple_gate):
    bsz, seq, _ = x_prompt.shape
    nb = x_sample.shape[0]
    n_pool = cache_cmp_kv.shape[1]
    n_pages = page_table.shape[1]
    past = n_pages * PAGE_SIZE
    kv_tail = (2, NSA_KV_HEADS, HEAD_DIM)

    o = _ORIG
    w = w_in[0]
    w_r = jnp.concatenate([w[:, o['q_a']:o['k_cmp']], w[:, o['z_a']:o['end']], w[:, o['k_cmp']:o['gate_a']],
                           w[:, o['gate_a']:o['z_a']],
                           jnp.zeros((D_MODEL, PROJ_WIDTH - o['end']), f32)], axis=1).astype(bf16)
    cw = _compress_weights(cmp_pe[0], cmp_w1[0], cmp_b1[0], cmp_w2[0])
    m_mat = _sel_map_matrix()
    w_out_b, ple_proj_b, ple_gate_b = w_out[0].astype(bf16), ple_proj[0].astype(bf16), ple_gate[0].astype(bf16)

    xp = x_prompt.reshape(bsz * seq, D_MODEL)
    zp, cmp_p, slc_p, win_p, gates_p = _proj(xp, g_pre, w_r, 1024)
    pages_p = cmp_p.reshape(bsz * seq // PAGE_SIZE, PAGE_SIZE, ROW_WIDTH)
    table_p = jnp.arange(bsz * seq // PAGE_SIZE, dtype=jnp.int32).reshape(bsz, seq // PAGE_SIZE)
    comp_p = _compress(pages_p, table_p, *cw, feature_major=False)
    q_t, k_slc, vt_slc, k_win, vt_win, cmp_tp, slc_tp, win_tp = _attention_layout(zp, cmp_p, slc_p, win_p, bsz, seq)
    o_a = _nsa_prompt_blocks(q_t, gates_p, comp_p, k_slc, vt_slc, k_win, vt_win, m_mat.T)
    yb_p, hg_p = _hgrn_prompt(zp, hg_lower, hg_norm, bsz, seq)
    h_p = _finish(xp, o_a, zp, yb_p, p_prompt[0].reshape(bsz * seq, PLE_DIM),
                  w_out_b, g_post, ple_proj_b, ple_gate_b, 1024)

    wl_p = min(WINDOW, seq)

    def cache_rows(rows_t):
        return rows_t.reshape((1, bsz) + kv_tail + rows_t.shape[2:]).transpose(0, 1, 5, 2, 3, 4)

    out_prompt = (h_p.reshape(bsz, seq, D_MODEL), cache_rows(cmp_tp), cache_rows(slc_tp),
                  cache_rows(win_tp[:, :, seq - wl_p:]), hg_p[None])

    xs = x_sample.reshape(nb, D_MODEL)
    zs, cmp_s, slc_s, win_new, gates_s = _proj(xs, g_pre, w_r, nb)
    cmp_t = cache_cmp_kv.transpose(0, 1, 3, 4, 5, 2).reshape(n_pool, ROW_WIDTH, PAGE_SIZE)
    comp_s = _compress(cmp_t, page_table, *cw, feature_major=True)
    q8 = zs[:, COL_QA:COL_QA + NSA_WIDTH].astype(bf16).reshape(nb, NSA_HEADS, HEAD_DIM)
    oc_s, ps_s = _nsa_sample_cmp(q8, comp_s, m_mat, past)
    ps_rows = ps_s[:, :NSA_KV_HEADS, :].reshape(nb * NSA_KV_HEADS, N_SELBLK)
    idx = _nsa_sample_topk(ps_rows, past // SEL_BLOCK - 1)
    slc_t = cache_slc_kv.transpose(0, 1, 3, 4, 5, 2).reshape(n_pool, N_COMBO, HEAD_DIM, PAGE_SIZE)
    wl_s = cache_win_kv.shape[2]
    win_t = cache_win_kv.transpose(0, 1, 3, 4, 5, 2).reshape(nb, N_COMBO, HEAD_DIM, wl_s)
    oa_s, win_s = _nsa_sample_sel(page_table, idx[:, :N_SEL].reshape(-1), slc_t, q8, oc_s,
                                  gates_s.reshape(nb, 1, LANES), slc_s.reshape(nb, 1, ROW_WIDTH),
                                  win_new.reshape(nb, 1, ROW_WIDTH), win_t, past)
    win_s = win_s.reshape((1, nb) + kv_tail + (wl_s,)).transpose(0, 1, 5, 2, 3, 4)
    yb_s, hg_s = _hgrn_sample(zs, state_hgrn, hg_lower, hg_norm)
    h_s = _finish(xs, oa_s.reshape(nb, NSA_WIDTH), zs, yb_s, p_sample[0].reshape(nb, PLE_DIM),
                  w_out_b, g_post, ple_proj_b, ple_gate_b, nb)

    out_sample = (h_s.reshape(nb, 1, D_MODEL),
                  cmp_s.reshape((1, nb, 1) + kv_tail),
                  slc_s.reshape((1, nb, 1) + kv_tail),
                  win_s,
                  hg_s)
    return (out_prompt[0], out_sample[0]) + out_prompt[1:] + out_sample[1:]
```

```python
import functools

import numpy as np
import jax
import jax.numpy as jnp
from jax import lax
from jax.experimental import pallas as pl
from jax.experimental.pallas import tpu as pltpu

D_MODEL = 1024
PAGE_SIZE = 128
NSA_HEADS = 8
NSA_KV_HEADS = 2
NSA_GROUP = NSA_HEADS // NSA_KV_HEADS
HEAD_DIM = 64
NSA_WIDTH = NSA_HEADS * HEAD_DIM
KV_WIDTH = NSA_KV_HEADS * HEAD_DIM
CMP_BLOCK = 32
CMP_STRIDE = 16
CMP_RATIO = CMP_BLOCK // CMP_STRIDE
CMP_HIDDEN = 128
SEL_BLOCK = 64
SEL_RATIO = SEL_BLOCK // CMP_STRIDE
N_SEL = 16
WINDOW = 512
Q_BLOCK = 128
HG_HEADS = 4
HG_DIM = 128
HG_WIDTH = HG_HEADS * HG_DIM
PLE_DIM = 256
EPS = 1e-6
NEG = -1e30
FORCE_BONUS = 1e6
REMOVED = -3e38

COL_QA, COL_ZA, COL_QB, COL_FB, COL_IB, COL_ZB = 0, 512, 1024, 1536, 2048, 2560
COL_CMP, COL_SLC, COL_WIN, COL_GATE = 3072, 3328, 3584, 3840
PROJ_WIDTH = 4096
_ORIG = dict(q_a=0, k_cmp=512, gate_a=1280, z_a=1304, q_b=1816, f_b=2328, i_b=2840, z_b=3352, end=3864)

LANES = 128
VMEM_LIMIT = 56 * 1024 * 1024

ROW_WIDTH = 2 * KV_WIDTH
N_COMBO = 2 * NSA_KV_HEADS
SEL_CHUNK = 512
WIN_KEYS = WINDOW + Q_BLOCK
N_CMP_PAD = 512
N_SELBLK = 128

f32 = jnp.float32
bf16 = jnp.bfloat16


def _cparams(sem):
    return pltpu.CompilerParams(dimension_semantics=sem, vmem_limit_bytes=VMEM_LIMIT)


def _dot(a, b):
    return jnp.dot(a, b, preferred_element_type=f32)


def _dot_nt(a, b):
    return lax.dot_general(a, b, (((1,), (1,)), ((), ())), preferred_element_type=f32)


def _dot_tn(a, b):
    return lax.dot_general(a, b, (((0,), (0,)), ((), ())), preferred_element_type=f32)


def _sigmoid(x):
    return 1.0 / (1.0 + jnp.exp(-x))


def _silu(x):
    return x * _sigmoid(x)


PROJ_TILE = 1024
MAIN_TILES = COL_CMP // PROJ_TILE


def _proj_kernel(x_ref, g_ref, w_ref, z_ref, cmp_ref, slc_ref, win_ref, gate_ref, xn_ref):
    j = pl.program_id(1)

    @pl.when(j == 0)
    def _():
        x = x_ref[...]
        y = x * lax.rsqrt(jnp.mean(x * x, axis=-1, keepdims=True) + EPS)
        xn_ref[...] = (y * g_ref[...]).astype(bf16)

    res = _dot(xn_ref[...], w_ref[...])

    @pl.when(j < MAIN_TILES)
    def _():
        z_ref[...] = res

    @pl.when(j == MAIN_TILES)
    def _():
        base = COL_CMP
        cmp_ref[...] = res[:, COL_CMP - base:COL_SLC - base]
        slc_ref[...] = res[:, COL_SLC - base:COL_WIN - base]
        win_ref[...] = res[:, COL_WIN - base:COL_GATE - base]
        gate_ref[...] = res[:, COL_GATE - base:COL_GATE - base + LANES]


def _proj(x, g, w, tm):
    n = x.shape[0]
    row = lambda i, j: (i, 0)
    return pl.pallas_call(
        _proj_kernel,
        grid=(n // tm, PROJ_WIDTH // PROJ_TILE),
        in_specs=[pl.BlockSpec((tm, D_MODEL), row),
                  pl.BlockSpec((1, D_MODEL), lambda i, j: (0, 0)),
                  pl.BlockSpec((D_MODEL, PROJ_TILE), lambda i, j: (0, j))],
        out_specs=[pl.BlockSpec((tm, PROJ_TILE), lambda i, j: (i, jnp.minimum(j, MAIN_TILES - 1))),
                   pl.BlockSpec((tm, ROW_WIDTH), row), pl.BlockSpec((tm, ROW_WIDTH), row),
                   pl.BlockSpec((tm, ROW_WIDTH), row), pl.BlockSpec((tm, LANES), row)],
        out_shape=[jax.ShapeDtypeStruct((n, COL_CMP), f32),
                   jax.ShapeDtypeStruct((n, ROW_WIDTH), f32), jax.ShapeDtypeStruct((n, ROW_WIDTH), f32),
                   jax.ShapeDtypeStruct((n, ROW_WIDTH), f32), jax.ShapeDtypeStruct((n, LANES), f32)],
        scratch_shapes=[pltpu.VMEM((tm, D_MODEL), bf16)],
        compiler_params=_cparams(("parallel", "arbitrary")),
        name="proj",
    )(x, g, w)


def _finish_kernel(x_ref, oa_ref, za_ref, yb_ref, p_ref, wo_ref, gp_ref, pp_ref, pg_ref, h_ref):
    ya = (oa_ref[...] * _silu(za_ref[...])).astype(bf16)
    yb = yb_ref[...].astype(bf16)
    mix = _dot(ya, wo_ref[0:NSA_WIDTH, :]) + _dot(yb, wo_ref[NSA_WIDTH:, :])
    nrm = mix * lax.rsqrt(jnp.mean(mix * mix, axis=-1, keepdims=True) + EPS) * gp_ref[...]
    h = x_ref[...] + nrm
    gate = _sigmoid(_dot(h.astype(bf16), pg_ref[...]))
    h_ref[...] = h + _dot(p_ref[...].astype(bf16), pp_ref[...]) * gate


def _finish(x, oa, z, yb, p, w_out, g_post, ple_proj, ple_gate, tm):
    n = x.shape[0]
    za_blk = COL_ZA // NSA_WIDTH
    row = lambda i: (i, 0)
    const = lambda i: (0, 0)
    return pl.pallas_call(
        _finish_kernel,
        grid=(n // tm,),
        in_specs=[pl.BlockSpec((tm, D_MODEL), row),
                  pl.BlockSpec((tm, NSA_WIDTH), row),
                  pl.BlockSpec((tm, NSA_WIDTH), lambda i: (i, za_blk)),
                  pl.BlockSpec((tm, HG_WIDTH), row),
                  pl.BlockSpec((tm, PLE_DIM), row),
                  pl.BlockSpec((D_MODEL, D_MODEL), const),
                  pl.BlockSpec((1, D_MODEL), const),
                  pl.BlockSpec((PLE_DIM, D_MODEL), const),
                  pl.BlockSpec((D_MODEL, D_MODEL), const)],
        out_specs=pl.BlockSpec((tm, D_MODEL), row),
        out_shape=jax.ShapeDtypeStruct((n, D_MODEL), f32),
        compiler_params=_cparams(("parallel",)),
        name="finish",
    )(x, oa, z, yb, p, w_out, g_post, ple_proj, ple_gate)


SEGS_PER_PAGE = PAGE_SIZE // CMP_STRIDE


def _compress_kernel(pt_ref, pages_ref, wa_ref, wb_ref, pea_ref, peb_ref, b1_ref, w2_ref, out_ref,
                     stage_ref, rows_ref, sem_ref, *, n_pages, feature_major):
    b = pl.program_id(0)
    nb = pl.num_programs(0)
    lane_halves = ROW_WIDTH // LANES
    n_seg = n_pages * SEGS_PER_PAGE

    def page_copy(seq, j):
        s = lax.rem(seq, 2)
        return pltpu.make_async_copy(pages_ref.at[pt_ref[seq, j]], stage_ref.at[s, j], sem_ref.at[s])

    def start_all(seq):
        def body(j, c):
            page_copy(seq, j).start()
            return c
        lax.fori_loop(0, n_pages, body, 0)

    def wait_all(seq):
        def body(j, c):
            page_copy(seq, j).wait()
            return c
        lax.fori_loop(0, n_pages, body, 0)

    def to_rows(s):
        for j in range(n_pages):
            tile = stage_ref[s, j]
            rows = tile.T if feature_major else tile
            for h in range(lane_halves):
                rows_ref[s, h, j * PAGE_SIZE:(j + 1) * PAGE_SIZE, :] = rows[:, h * LANES:(h + 1) * LANES]

    def mlp(s):
        for kind in range(lane_halves):
            cols = slice(kind * LANES, (kind + 1) * LANES)
            seg_a, seg_b = [], []
            for j in range(CMP_STRIDE):
                r = rows_ref[s, kind, pl.ds(j, n_seg, stride=CMP_STRIDE), :]
                seg_a.append((r + pea_ref[j][:, cols]).astype(bf16))
                seg_b.append((r + peb_ref[j][:, cols]).astype(bf16))
            pa = _dot(jnp.concatenate(seg_a, axis=1), wa_ref[kind])
            pb = _dot(jnp.concatenate(seg_b, axis=1), wb_ref[kind])
            pb = pltpu.roll(pb, n_seg - 1, axis=0)
            hid = _silu(pa + pb + b1_ref[kind])
            for g in range(NSA_KV_HEADS):
                hg = hid[:, g * CMP_HIDDEN:(g + 1) * CMP_HIDDEN].astype(bf16)
                out_ref[0, kind * NSA_KV_HEADS + g] = _dot(hg, w2_ref[kind])

    @pl.when(b == 0)
    def _():
        start_all(0)

        @pl.when(nb > 1)
        def _():
            start_all(1)
        wait_all(0)
        to_rows(0)

    @pl.when(b + 2 < nb)
    def _():
        start_all(b + 2)

    @pl.when(b + 1 < nb)
    def _():
        wait_all(b + 1)

    for cur in range(2):
        @pl.when(lax.rem(b, 2) == cur)
        def _(cur=cur):
            to_rows(1 - cur)
            mlp(cur)


def _compress(pages, page_table, wa, wb, pea, peb, b1, w2, feature_major):
    nseq, n_pages = page_table.shape
    n_seg = n_pages * SEGS_PER_PAGE
    const3 = lambda b, pt: (0, 0, 0)
    once = pl.Buffered(1)
    grid_spec = pltpu.PrefetchScalarGridSpec(
        num_scalar_prefetch=1,
        grid=(nseq,),
        in_specs=[pl.BlockSpec(memory_space=pl.ANY),
                  pl.BlockSpec(wa.shape, const3, pipeline_mode=once),
                  pl.BlockSpec(wb.shape, const3, pipeline_mode=once),
                  pl.BlockSpec((CMP_STRIDE, 1, ROW_WIDTH), const3),
                  pl.BlockSpec((CMP_STRIDE, 1, ROW_WIDTH), const3),
                  pl.BlockSpec(b1.shape, const3),
                  pl.BlockSpec((2, CMP_HIDDEN, HEAD_DIM), const3)],
        out_specs=pl.BlockSpec((1, N_COMBO, n_seg, HEAD_DIM), lambda b, pt: (b, 0, 0, 0)),
        scratch_shapes=[pltpu.VMEM((2, n_pages) + pages.shape[1:], f32),
                        pltpu.VMEM((2, ROW_WIDTH // LANES, n_pages * PAGE_SIZE, LANES), f32),
                        pltpu.SemaphoreType.DMA((2,))],
    )
    return pl.pallas_call(
        functools.partial(_compress_kernel, n_pages=n_pages, feature_major=feature_major),
        grid_spec=grid_spec,
        out_shape=jax.ShapeDtypeStruct((nseq, N_COMBO, n_seg, HEAD_DIM), f32),
        compiler_params=_cparams(("arbitrary",)),
        name="compress",
    )(page_table, pages, wa, wb, pea, peb, b1, w2)


def _compress_weights(cmp_pe, cmp_w1, cmp_b1, cmp_w2):
    eye = jnp.eye(NSA_KV_HEADS, dtype=f32)

    def big(w_half):
        t = jnp.einsum('kjdh,gb->kjbdgh', w_half, eye)
        return t.reshape(2, CMP_STRIDE * KV_WIDTH, NSA_KV_HEADS * CMP_HIDDEN)

    w1 = cmp_w1.reshape(2, 2, CMP_STRIDE, HEAD_DIM, CMP_HIDDEN)
    wa = big(w1[:, 0]).astype(bf16)
    wb = big(w1[:, 1]).astype(bf16)
    pe = cmp_pe.reshape(2, 2, CMP_STRIDE, HEAD_DIM)

    def pe_row(p):
        t = jnp.broadcast_to(p.transpose(1, 0, 2)[:, :, None, :],
                             (CMP_STRIDE, 2, NSA_KV_HEADS, HEAD_DIM))
        return t.reshape(CMP_STRIDE, 1, ROW_WIDTH)

    pea, peb = pe_row(pe[:, 0]), pe_row(pe[:, 1])
    b1 = jnp.broadcast_to(cmp_b1[:, None, :], (2, NSA_KV_HEADS, CMP_HIDDEN)).reshape(2, 1, NSA_KV_HEADS * CMP_HIDDEN)
    return wa, wb, pea, peb, b1, cmp_w2.astype(bf16)


def _sel_map_matrix():
    wts = np.convolve(np.ones(SEL_RATIO), np.ones(CMP_RATIO))
    m = np.zeros((N_CMP_PAD, N_SELBLK), np.float32)
    for j in range(N_SELBLK):
        for k, wk in enumerate(wts):
            n = SEL_RATIO * j + k
            if n < N_CMP_PAD - 1:
                m[n, j] = wk
    return jnp.asarray(m, dtype=bf16)


def _importance_to_blocks(imp, m):
    hi = imp.astype(bf16)
    lo = (imp - hi.astype(f32)).astype(bf16)
    return _dot(hi, m) + _dot(lo, m)


LOG2E = 1.4426950408889634
MASK_BIG = 1e30
AUG_ROWS = 16
VT_ROWS = HEAD_DIM + 16
KEY_LOW = 256
N_FORCED = 3


NSA_STEP_BLOCKS = 2


def _nsa_prompt_blocks_kernel(qt_ref, gate_ref, comp_ref, caug_ref, ks_ref, vts_ref, kw_ref, vtw_ref, mt_ref, o_ref,
                              qa_ref, acc_ref, mg_ref, bias_ref, need_ref):
    cols = NSA_GROUP * Q_BLOCK
    n_blocks = qt_ref.shape[2]
    blocks = range(n_blocks)
    heads = range(NSA_KV_HEADS)
    insts = [(qb, g) for qb in blocks for g in heads]
    blk_per_chunk = SEL_CHUNK // SEL_BLOCK
    n_chunk_max = vts_ref.shape[2]

    lane = lax.broadcasted_iota(jnp.int32, (1, cols), 1)
    head = jnp.right_shift(lane, Q_BLOCK.bit_length() - 1)
    blk = lax.broadcasted_iota(jnp.int32, (N_SELBLK, Q_BLOCK), 0)
    qcol = lax.broadcasted_iota(jnp.int32, (N_SELBLK, Q_BLOCK), 1)
    c_end = lax.broadcasted_iota(jnp.int32, (N_CMP_PAD, 1), 0) * CMP_STRIDE + (CMP_BLOCK - 1)
    krow = lax.broadcasted_iota(jnp.int32, (SEL_CHUNK, 1), 0)
    wrow = lax.broadcasted_iota(jnp.int32, (WIN_KEYS, 1), 0)
    arow = lax.broadcasted_iota(jnp.int32, (AUG_ROWS // 2, cols), 0)
    gate_all = _sigmoid(gate_ref[...])

    q0 = [(pl.program_id(1) * n_blocks + qb) * Q_BLOCK for qb in blocks]
    tq = [q0[qb] + jnp.bitwise_and(lane, Q_BLOCK - 1) for qb in blocks]
    last = [q0[qb] // SEL_CHUNK for qb in blocks]
    w0 = [pl.multiple_of(jnp.maximum(q0[qb] - WINDOW, 0), Q_BLOCK) for qb in blocks]
    gate_t = [gate_all[qb * Q_BLOCK:(qb + 1) * Q_BLOCK].T for qb in blocks]

    slope, alibi_rows, pos_rows = [], [], []
    for g in heads:
        c = jnp.exp2(-(head.astype(f32) + 1.0 + NSA_GROUP * g)) * LOG2E
        c_hi = c.astype(bf16).astype(f32)
        c_mid = (c - c_hi).astype(bf16).astype(f32)
        c_lo = (c - c_hi - c_mid).astype(bf16).astype(f32)
        c3 = jnp.where((arow == 0) | (arow == 3), c_hi, jnp.where((arow == 1) | (arow == 4), c_mid, c_lo))
        slope.append(c)
        alibi_rows.append(jnp.where(arow >= 6, 0.0, c3))
        rows = jnp.where(arow >= 6, 0.0, jnp.where(arow >= 3, KEY_LOW * c3, c3))
        pos_rows.append(jnp.concatenate([rows, jnp.zeros_like(rows)], axis=0).astype(bf16))

    def chunk_scores(qb, g, ck, slot=0, live=None):
        k0 = pl.multiple_of(ck * SEL_CHUNK, SEL_CHUNK)
        b8 = bias_ref[qb, g, pl.ds(pl.multiple_of(ck * blk_per_chunk, blk_per_chunk), blk_per_chunk), :]
        if live is not None:
            b8 = jnp.where(live, b8, -MASK_BIG)
        qa_ref[qb, g, slot, HEAD_DIM:HEAD_DIM + AUG_ROWS, :] = jnp.concatenate(
            [jnp.concatenate([b8] * NSA_GROUP, axis=1), alibi_rows[g]], axis=0).astype(bf16)
        return _dot(ks_ref[0, g, pl.ds(k0, SEL_CHUNK), :], qa_ref[qb, g, slot])

    def sel_chunk(qb, g, ck, causal, s):
        k0 = pl.multiple_of(ck * SEL_CHUNK, SEL_CHUNK)
        if causal:
            s = jnp.where(k0 + krow <= tq[qb], s, NEG)
        shift = slope[g] * k0.astype(f32)
        m_old = mg_ref[qb, g] - shift
        m_new = jnp.maximum(m_old, jnp.max(s, axis=0, keepdims=True))
        p = jnp.exp2(s - m_new).astype(bf16)
        acc_ref[qb, g] = acc_ref[qb, g] * jnp.exp2(m_old - m_new) + _dot(vts_ref[0, g, ck], p)
        mg_ref[qb, g] = m_new + shift

    s_cmp, s_win = {}, {}
    for qb, g in insts:
        q_pos = jnp.concatenate([qt_ref[0, g, qb], pos_rows[g],
                                 jnp.zeros((LANES - HEAD_DIM - AUG_ROWS, cols), bf16)], axis=0)
        kc = jnp.concatenate([comp_ref[0, g].astype(bf16), caug_ref[...]], axis=1)
        s_cmp[qb, g] = _dot(kc, q_pos)
        kw = kw_ref[0, g, pl.ds(w0[qb], WIN_KEYS), :]
        s_win[qb, g] = _dot(kw, q_pos)

    oc_t, ps_t = {}, {}
    for qb in blocks:
        seen = c_end <= tq[qb]
        any_seen = tq[qb] >= CMP_BLOCK - 1
        for g in heads:
            s_c = jnp.where(seen, s_cmp[qb, g], NEG)
            e = jnp.exp2(s_c - jnp.max(s_c, axis=0, keepdims=True))
            rl = jnp.where(any_seen, 1.0 / jnp.maximum(jnp.sum(e, axis=0, keepdims=True), 1e-30), 0.0)
            p_c = e * rl
            oc_t[qb, g] = _dot_tn(comp_ref[0, NSA_KV_HEADS + g].astype(bf16), p_c.astype(bf16))
            imp_t = (p_c[:, 0:Q_BLOCK] + p_c[:, Q_BLOCK:2 * Q_BLOCK]
                     + p_c[:, 2 * Q_BLOCK:3 * Q_BLOCK] + p_c[:, 3 * Q_BLOCK:4 * Q_BLOCK])
            hi = imp_t.astype(bf16)
            lo = (imp_t - hi.astype(f32)).astype(bf16)
            ps_t[qb, g] = _dot(mt_ref[...], hi) + _dot(mt_ref[...], lo)

    ow_t = {}
    for qb in blocks:
        wkey = w0[qb] + wrow
        in_window = (wkey <= tq[qb]) & (wkey >= tq[qb] - WINDOW)
        t0 = w0[qb] // Q_BLOCK
        for g in heads:
            s_w = jnp.where(in_window, s_win[qb, g], NEG)
            e = jnp.exp2(s_w - jnp.max(s_w, axis=0, keepdims=True)).astype(bf16)
            acc_w = jnp.zeros((VT_ROWS, cols), f32)
            for t in range(WIN_KEYS // Q_BLOCK):
                acc_w = acc_w + _dot(vtw_ref[0, g, t0 + t], e[t * Q_BLOCK:(t + 1) * Q_BLOCK])
            ow_t[qb, g] = acc_w[0:HEAD_DIM] * (1.0 / jnp.maximum(acc_w[HEAD_DIM:HEAD_DIM + 1], 1e-30))

    valid, sel_t, score = {}, {}, {}
    for qb in blocks:
        jt = jnp.right_shift(q0[qb] + qcol, SEL_BLOCK.bit_length() - 1)
        valid[qb] = blk <= jt
        forced = (blk == 0) | (blk == jt) | (blk == jt - 1)
        for g in heads:
            sel_t[qb, g] = jnp.where(forced, 1.0, 0.0)
            score[qb, g] = jnp.where(valid[qb] & jnp.logical_not(forced), ps_t[qb, g], NEG)
    for _ in range(N_SEL - N_FORCED):
        for inst in insts:
            best = jnp.max(score[inst], axis=0, keepdims=True)
            idx = jnp.min(jnp.where(score[inst] == best, blk, N_SELBLK), axis=0, keepdims=True)
            hit = blk == idx
            sel_t[inst] = jnp.where(hit, 1.0, sel_t[inst])
            score[inst] = jnp.where(hit, REMOVED, score[inst])
    for qb, g in insts:
        sel_t[qb, g] = jnp.where(valid[qb], sel_t[qb, g], 0.0)

    for qb, g in insts:
        bias_ref[qb, g] = (sel_t[qb, g] - 1.0) * MASK_BIG
        for slot in range(qa_ref.shape[2]):
            qa_ref[qb, g, slot, 0:HEAD_DIM, :] = qt_ref[0, g, qb]
            qa_ref[qb, g, slot, HEAD_DIM + AUG_ROWS:, :] = jnp.zeros((LANES - HEAD_DIM - AUG_ROWS, cols), bf16)
        acc_ref[qb, g] = jnp.zeros(acc_ref.shape[2:], f32)
        mg_ref[qb, g] = jnp.full(mg_ref.shape[2:], NEG, f32)
    own = {(qb, g): chunk_scores(qb, g, last[qb]) for qb, g in insts}
    for qb, g in insts:
        sel_chunk(qb, g, last[qb], True, own[qb, g])

    n_need = []
    for qb in blocks:
        count = jnp.int32(0)
        for ck in range(n_chunk_max):
            rows = slice(ck * blk_per_chunk, (ck + 1) * blk_per_chunk)
            picked = jnp.maximum(*[jnp.max(sel_t[qb, g][rows, :]) for g in heads])
            need_ref[qb, count] = ck
            count = count + ((picked > 0.5) & (ck < last[qb])).astype(jnp.int32)
        n_need.append(count)

    for qb in blocks:
        def chunk_pair(i, carry, qb=qb):
            first = 2 * i
            live = first + 1 < n_need[qb]
            cks = [need_ref[qb, first], need_ref[qb, jnp.minimum(first + 1, n_need[qb] - 1)]]
            scores = [[chunk_scores(qb, g, cks[slot], slot, live if slot else None) for g in heads]
                      for slot in range(2)]
            for slot in range(2):
                for g in heads:
                    sel_chunk(qb, g, cks[slot], False, scores[slot][g])
            return carry

        lax.fori_loop(0, (n_need[qb] + 1) // 2, chunk_pair, 0)

    for qb, g in insts:
        acc = acc_ref[qb, g]
        os_t = acc[0:HEAD_DIM] * (1.0 / jnp.maximum(acc[HEAD_DIM:HEAD_DIM + 1], 1e-30))
        outs = []
        for r in range(NSA_GROUP):
            cs = slice(r * Q_BLOCK, (r + 1) * Q_BLOCK)
            row = (g * NSA_GROUP + r) * 3
            outs.append(gate_t[qb][row:row + 1, :] * oc_t[qb, g][:, cs] + gate_t[qb][row + 1:row + 2, :] * os_t[:, cs]
                        + gate_t[qb][row + 2:row + 3, :] * ow_t[qb, g][:, cs])
        for pair in range(NSA_GROUP // 2):
            both = jnp.concatenate([outs[2 * pair], outs[2 * pair + 1]], axis=0)
            lane0 = (g * NSA_GROUP // 2 + pair) * LANES
            o_ref[qb * Q_BLOCK:(qb + 1) * Q_BLOCK, lane0:lane0 + LANES] = both.T


def _nsa_prompt_blocks(q_t, gates, comp, k_slc, vt_slc, k_win, vt_win, m_t):
    bsz, _, nq = q_t.shape[:3]
    seq = k_slc.shape[2]
    cols = NSA_GROUP * Q_BLOCK
    nblk = NSA_STEP_BLOCKS
    steps = nq // nblk
    per_b4 = lambda b, i: (b, 0, 0, 0)
    per_b5 = lambda b, i: (b, 0, 0, 0, 0)
    inst = (nblk, NSA_KV_HEADS)
    return pl.pallas_call(
        _nsa_prompt_blocks_kernel,
        grid=(bsz, steps),
        in_specs=[pl.BlockSpec((1, NSA_KV_HEADS, nblk, HEAD_DIM, cols), lambda b, i: (b, 0, i, 0, 0)),
                  pl.BlockSpec((nblk * Q_BLOCK, LANES), lambda b, i: (b * steps + i, 0)),
                  pl.BlockSpec((1, N_COMBO, N_CMP_PAD, HEAD_DIM), per_b4),
                  pl.BlockSpec((N_CMP_PAD, HEAD_DIM), lambda b, i: (0, 0)),
                  pl.BlockSpec((1, NSA_KV_HEADS, seq, LANES), per_b4),
                  pl.BlockSpec((1,) + vt_slc.shape[1:], per_b5),
                  pl.BlockSpec((1, NSA_KV_HEADS, seq, LANES), per_b4),
                  pl.BlockSpec((1,) + vt_win.shape[1:], per_b5),
                  pl.BlockSpec((N_SELBLK, N_CMP_PAD), lambda b, i: (0, 0))],
        out_specs=pl.BlockSpec((nblk * Q_BLOCK, NSA_WIDTH), lambda b, i: (b * steps + i, 0)),
        out_shape=jax.ShapeDtypeStruct((bsz * seq, NSA_WIDTH), f32),
        scratch_shapes=[pltpu.VMEM(inst + (2, LANES, cols), bf16),
                        pltpu.VMEM(inst + (VT_ROWS, cols), f32),
                        pltpu.VMEM(inst + (1, cols), f32),
                        pltpu.VMEM(inst + (N_SELBLK, Q_BLOCK), f32),
                        pltpu.SMEM((nblk, vt_slc.shape[2]), jnp.int32)],
        compiler_params=_cparams(("parallel", "arbitrary")),
        name="nsa_prompt",
    )(q_t, gates, comp, _cmp_position_columns(), k_slc, vt_slc, k_win, vt_win, m_t)


def _position_columns(pos):
    low, high = (pos % KEY_LOW)[:, None], (pos // KEY_LOW)[:, None]
    return np.concatenate([low, low, low, high, high, high], axis=1).astype(np.float32)


def _cmp_position_columns():
    aug = np.zeros((N_CMP_PAD, HEAD_DIM), np.float32)
    aug[:, 0:6] = _position_columns(np.arange(N_CMP_PAD) * CMP_STRIDE + CMP_BLOCK - 1)
    return jnp.asarray(aug, dtype=bf16)


LAYOUT_ROWS = 1024


def _attention_layout_kernel(q_ref, cmp_ref, slc_ref, win_ref, xs_ref, xw_ref,
                             qt_ref, ks_ref, vts_ref, kw_ref, vtw_ref, cmp_t_ref, slc_t_ref, win_t_ref):
    cmp_t_ref[0] = cmp_ref[...].T
    tm = q_ref.shape[0]
    q = q_ref[...] * (HEAD_DIM ** -0.5 * LOG2E)
    for qb in range(tm // Q_BLOCK):
        rows = slice(qb * Q_BLOCK, (qb + 1) * Q_BLOCK)
        for pair in range(NSA_HEADS // 2):
            t = q[rows, pair * LANES:(pair + 1) * LANES].T
            for k in range(2):
                g, r = divmod(2 * pair + k, NSA_GROUP)
                qt_ref[0, g, qb, :, r * Q_BLOCK:(r + 1) * Q_BLOCK] = t[k * HEAD_DIM:(k + 1) * HEAD_DIM].astype(bf16)

    for src_ref, extra_ref, k_ref, vt_ref, t_ref in ((slc_ref, xs_ref, ks_ref, vts_ref, slc_t_ref),
                                                     (win_ref, xw_ref, kw_ref, vtw_ref, win_t_ref)):
        tile = vt_ref.shape[4]
        s = src_ref[...]
        s_t = s.T
        t_ref[0] = s_t
        v_t = s_t[KV_WIDTH:]
        tail = jnp.where(lax.broadcasted_iota(jnp.int32, (VT_ROWS - HEAD_DIM, tile), 0) == 0, 1.0, 0.0)
        for g in range(NSA_KV_HEADS):
            k_ref[0, g] = jnp.concatenate([s[:, g * HEAD_DIM:(g + 1) * HEAD_DIM].astype(bf16), extra_ref[...]], axis=1)
            for c in range(tm // tile):
                blk = v_t[g * HEAD_DIM:(g + 1) * HEAD_DIM, c * tile:(c + 1) * tile]
                vt_ref[0, g, c] = jnp.concatenate([blk, tail], axis=0).astype(bf16)


def _attention_layout(z, cmp, slc, win, bsz, seq):
    tm = LAYOUT_ROWS
    per_b = seq // tm
    pos = np.arange(seq)
    xs = np.zeros((seq, HEAD_DIM), np.float32)
    xs[pos, (pos // SEL_BLOCK) % (SEL_CHUNK // SEL_BLOCK)] = 1.0
    xs[:, 8:11] = (pos % KEY_LOW)[:, None]
    xs[:, 11:14] = (pos % SEL_CHUNK - pos % KEY_LOW)[:, None]
    xw = np.zeros((seq, HEAD_DIM), np.float32)
    xw[:, 0:6] = _position_columns(pos)
    row = lambda i: (i, 0)
    rep = lambda i: (i % per_b, 0)
    out4 = lambda i: (i // per_b, 0, i % per_b, 0)
    out5 = lambda i: (i // per_b, 0, i % per_b, 0, 0)
    k_shape = jax.ShapeDtypeStruct((bsz, NSA_KV_HEADS, seq, LANES), bf16)
    vt_shape = lambda tile: jax.ShapeDtypeStruct((bsz, NSA_KV_HEADS, seq // tile, VT_ROWS, tile), bf16)
    vt_spec = lambda tile: pl.BlockSpec((1, NSA_KV_HEADS, tm // tile, VT_ROWS, tile), out5)
    rows_t_shape = jax.ShapeDtypeStruct((bsz, ROW_WIDTH, seq), f32)
    rows_t_spec = pl.BlockSpec((1, ROW_WIDTH, tm), lambda i: (i // per_b, 0, i % per_b))
    return pl.pallas_call(
        _attention_layout_kernel,
        grid=(bsz * per_b,),
        in_specs=[pl.BlockSpec((tm, NSA_WIDTH), row), pl.BlockSpec((tm, ROW_WIDTH), row),
                  pl.BlockSpec((tm, ROW_WIDTH), row), pl.BlockSpec((tm, ROW_WIDTH), row),
                  pl.BlockSpec((tm, HEAD_DIM), rep), pl.BlockSpec((tm, HEAD_DIM), rep)],
        out_specs=[pl.BlockSpec((1, NSA_KV_HEADS, tm // Q_BLOCK, HEAD_DIM, NSA_GROUP * Q_BLOCK), out5),
                   pl.BlockSpec((1, NSA_KV_HEADS, tm, LANES), out4), vt_spec(SEL_CHUNK),
                   pl.BlockSpec((1, NSA_KV_HEADS, tm, LANES), out4), vt_spec(Q_BLOCK),
                   rows_t_spec, rows_t_spec, rows_t_spec],
        out_shape=[jax.ShapeDtypeStruct((bsz, NSA_KV_HEADS, seq // Q_BLOCK, HEAD_DIM, NSA_GROUP * Q_BLOCK), bf16),
                   k_shape, vt_shape(SEL_CHUNK), k_shape, vt_shape(Q_BLOCK),
                   rows_t_shape, rows_t_shape, rows_t_shape],
        compiler_params=_cparams(("parallel",)),
        name="attention_layout",
    )(z, cmp, slc, win, jnp.asarray(xs, dtype=bf16), jnp.asarray(xw, dtype=bf16))


HG_CHUNK = 128
HG_STEP_CHUNKS = 4
HG_BAND = 8


def _lower_bound(lo):
    m = jnp.max(lo, axis=0, keepdims=True)
    e = jnp.exp(lo - m)
    return e[0:1] / jnp.sum(e, axis=0, keepdims=True)


def _hgrn_gates(q, fl, lb):
    sg = _sigmoid(fl)
    f = lb + (1.0 - lb) * sg
    k = (1.0 - lb) * (1.0 - sg)
    return _silu(q), k, jnp.log(f)


def _hgrn_out(o, nw, zb):
    o = o * lax.rsqrt(jnp.mean(o * o, axis=-1, keepdims=True) + EPS) * nw
    return o * _silu(zb)


def _hgrn_prompt_kernel(q_ref, f_ref, i_ref, zb_ref, lo_ref, nw_ref, y_ref, s_ref, st_ref):
    c = pl.program_id(1)

    @pl.when(c == 0)
    def _():
        st_ref[...] = jnp.zeros_like(st_ref)

    n = HG_CHUNK
    row = lax.broadcasted_iota(jnp.int32, (n, n), 0)
    col = lax.broadcasted_iota(jnp.int32, (n, n), 1)
    heads = [slice(h * HG_DIM, (h + 1) * HG_DIM) for h in range(HG_HEADS)]

    qq, kk, logf = _hgrn_gates(q_ref[...], f_ref[...], _lower_bound(lo_ref[...]))
    v = i_ref[...]
    chunks = [slice(i * n, (i + 1) * n) for i in range(q_ref.shape[0] // n)]
    tril = jnp.where(col <= row, 1.0, 0.0).astype(bf16)
    hi = logf.astype(bf16)
    r1 = logf - hi.astype(f32)
    mid = r1.astype(bf16)
    lo = (r1 - mid.astype(f32)).astype(bf16)
    bs = [_dot(tril, hi[rs]) + _dot(tril, mid[rs]) + _dot(tril, lo[rs]) for rs in chunks]

    intra = [[_hgrn_intra(qq[rs, hs], kk[rs, hs], b[:, hs], row, col) for hs in heads] for rs, b in zip(chunks, bs)]
    for i, rs in enumerate(chunks):
        for h, hs in enumerate(heads):
            st = st_ref[h]
            bh, kh, vh = bs[i][:, hs], kk[rs, hs], v[rs, hs].astype(bf16)
            o = _dot(intra[i][h].astype(bf16), vh) + _dot_nt((qq[rs, hs] * jnp.exp(bh)).astype(bf16), st.astype(bf16))
            b_last = bh[n - 1:n]
            st_new = st * jnp.exp(b_last) + _dot_tn(vh, (kh * jnp.exp(b_last - bh)).astype(bf16))
            st_ref[h] = st_new
            y_ref[rs, hs] = _hgrn_out(o, nw_ref[...], zb_ref[rs, hs])

    @pl.when(c == pl.num_programs(1) - 1)
    def _():
        for h in range(HG_HEADS):
            s_ref[0, h] = st_ref[h].T


def _hgrn_intra(qq, kk, b, row, col):
    n = HG_CHUNK
    a = jnp.where(row == col, _dot_nt(qq.astype(bf16), kk.astype(bf16)), 0.0)

    sub = jnp.bitwise_and(row, HG_BAND - 1)
    b3 = b.reshape(n // HG_BAND, HG_BAND, HG_DIM)
    size = 2
    while size <= HG_BAND:
        half = size // 2
        beta = None
        for s0 in range(0, HG_BAND, size):
            piece = jnp.broadcast_to(b3[:, s0 + half - 1:s0 + half, :], b3.shape).reshape(n, HG_DIM)
            beta = piece if beta is None else jnp.where(sub >= s0, piece, beta)
        left = jnp.bitwise_and(sub, size - 1) < half
        decay = jnp.exp(jnp.where(left, beta - b, b - beta))
        part = _dot_nt(jnp.where(left, 0.0, qq * decay).astype(bf16), jnp.where(left, kk * decay, 0.0).astype(bf16))
        sh = size.bit_length() - 1
        a = a + jnp.where(jnp.right_shift(row, sh) == jnp.right_shift(col, sh), part, 0.0)
        size *= 2

    while size <= n:
        half = size // 2
        qs, ks = [], []
        zero = jnp.zeros((half, HG_DIM), f32)
        for r0 in range(0, n, size):
            beta = b[r0 + half - 1:r0 + half]
            ks += [kk[r0:r0 + half] * jnp.exp(beta - b[r0:r0 + half]), zero]
            qs += [zero, qq[r0 + half:r0 + size] * jnp.exp(b[r0 + half:r0 + size] - beta)]
        part = _dot_nt(jnp.concatenate(qs, axis=0).astype(bf16), jnp.concatenate(ks, axis=0).astype(bf16))
        if size < n:
            sh = size.bit_length() - 1
            part = jnp.where(jnp.right_shift(row, sh) == jnp.right_shift(col, sh), part, 0.0)
        a = a + part
        size *= 2
    return a


def _hgrn_prompt(z, hg_lower, hg_norm, bsz, seq):
    rows = HG_STEP_CHUNKS * HG_CHUNK
    nch = seq // rows

    def zcol(base):
        return pl.BlockSpec((rows, HG_WIDTH), lambda b, c: (b * nch + c, base // HG_WIDTH))

    return pl.pallas_call(
        _hgrn_prompt_kernel,
        grid=(bsz, nch),
        in_specs=[zcol(COL_QB), zcol(COL_FB), zcol(COL_IB), zcol(COL_ZB),
                  pl.BlockSpec(hg_lower.shape, lambda b, c: (0, 0)),
                  pl.BlockSpec((1, HG_DIM), lambda b, c: (0, 0))],
        out_specs=[pl.BlockSpec((rows, HG_WIDTH), lambda b, c: (b * nch + c, 0)),
                   pl.BlockSpec((1, HG_HEADS, HG_DIM, HG_DIM), lambda b, c: (b, 0, 0, 0))],
        out_shape=[jax.ShapeDtypeStruct((bsz * seq, HG_WIDTH), f32),
                   jax.ShapeDtypeStruct((bsz, HG_HEADS, HG_DIM, HG_DIM), f32)],
        scratch_shapes=[pltpu.VMEM((HG_HEADS, HG_DIM, HG_DIM), f32)],
        compiler_params=_cparams(("parallel", "arbitrary")),
        name="hgrn_prompt",
    )(z, z, z, z, hg_lower, hg_norm)


def _hgrn_sample_kernel(q_ref, f_ref, i_ref, zb_ref, lo_ref, nw_ref, s_ref, y_ref, so_ref, o_scr):
    nb = q_ref.shape[0]
    lb = _lower_bound(lo_ref[...])
    qq, kk, logf = _hgrn_gates(q_ref[...], f_ref[...], lb)
    v = i_ref[...]
    pad = jnp.zeros((HG_DIM - nb, HG_DIM), f32)
    to_cols = lambda x: jnp.concatenate([x, pad], axis=0).T
    f_t, k_t, q_t = to_cols(jnp.exp(logf)), to_cols(kk), to_cols(qq)
    for b in range(nb):
        s_new = f_t[:, b:b + 1] * s_ref[0, b, 0] + k_t[:, b:b + 1] * v[b:b + 1, :]
        so_ref[0, b, 0] = s_new
        o_scr[b:b + 1, :] = jnp.sum(s_new * q_t[:, b:b + 1], axis=0, keepdims=True)
    y_ref[...] = _hgrn_out(o_scr[...], nw_ref[...], zb_ref[...])


def _hgrn_sample(z, state, hg_lower, hg_norm):
    nb = z.shape[0]

    def zcol(base):
        return pl.BlockSpec((nb, HG_DIM), lambda h: (0, base // HG_DIM + h))

    sblk = pl.BlockSpec((1, nb, 1, HG_DIM, HG_DIM), lambda h: (0, 0, h, 0, 0))
    return pl.pallas_call(
        _hgrn_sample_kernel,
        grid=(HG_HEADS,),
        in_specs=[zcol(COL_QB), zcol(COL_FB), zcol(COL_IB), zcol(COL_ZB),
                  pl.BlockSpec((hg_lower.shape[0], HG_DIM), lambda h: (0, h)),
                  pl.BlockSpec((1, HG_DIM), lambda h: (0, 0)),
                  sblk],
        out_specs=[pl.BlockSpec((nb, HG_DIM), lambda h: (0, h)), sblk],
        out_shape=[jax.ShapeDtypeStruct((nb, HG_WIDTH), f32),
                   jax.ShapeDtypeStruct(state.shape, f32)],
        scratch_shapes=[pltpu.VMEM((nb, HG_DIM), f32)],
        compiler_params=_cparams(("parallel",)),
        name="hgrn_sample",
    )(z, z, z, z, hg_lower, hg_norm, state)


N_PICK = N_SEL - 1


def _head_slopes():
    h = lax.broadcasted_iota(jnp.int32, (NSA_HEADS, 1), 0).astype(f32)
    return jnp.exp2(-(h + 1.0))


SAMPLE_CMP_SEQS = 8


def _nsa_sample_cmp_kernel(q_ref, comp_ref, m_ref, oc_ref, ps_ref, *, t_pos):
    slope = _head_slopes()
    head = lax.broadcasted_iota(jnp.int32, (NSA_HEADS, 1), 0)
    ncol = lax.broadcasted_iota(jnp.int32, (1, N_CMP_PAD), 1)
    d_c = t_pos - (ncol * CMP_STRIDE + (CMP_BLOCK - 1))
    mask = d_c >= 0
    ps_ref[...] = jnp.zeros_like(ps_ref)
    for b in range(q_ref.shape[0]):
        q = q_ref[b] * jnp.asarray(HEAD_DIM ** -0.5, bf16)
        o_c = jnp.zeros((NSA_HEADS, HEAD_DIM), f32)
        for g in range(NSA_KV_HEADS):
            kc = comp_ref[b, g].astype(bf16)
            vc = comp_ref[b, NSA_KV_HEADS + g].astype(bf16)
            s = jnp.where(mask, _dot_nt(q, kc) - slope * d_c.astype(f32), NEG)
            m = jnp.max(s, axis=1, keepdims=True)
            e = jnp.where(mask, jnp.exp(s - m), 0.0)
            p = e * (1.0 / jnp.maximum(jnp.sum(e, axis=1, keepdims=True), 1e-30))
            mine = jnp.right_shift(head, NSA_GROUP.bit_length() - 1) == g
            o_c = jnp.where(mine, _dot(p.astype(bf16), vc), o_c)
            imp = jnp.sum(jnp.where(mine, p, 0.0), axis=0, keepdims=True)
            ps_ref[b, g:g + 1, :] = _importance_to_blocks(imp, m_ref[...])
        oc_ref[b] = o_c


def _nsa_sample_cmp(q8, comp, m_mat, t_pos):
    nb = q8.shape[0]
    ns = SAMPLE_CMP_SEQS
    return pl.pallas_call(
        functools.partial(_nsa_sample_cmp_kernel, t_pos=t_pos),
        grid=(nb // ns,),
        in_specs=[pl.BlockSpec((ns, NSA_HEADS, HEAD_DIM), lambda b: (b, 0, 0)),
                  pl.BlockSpec((ns, N_COMBO, N_CMP_PAD, HEAD_DIM), lambda b: (b, 0, 0, 0)),
                  pl.BlockSpec((N_CMP_PAD, N_SELBLK), lambda b: (0, 0))],
        out_specs=[pl.BlockSpec((ns, NSA_HEADS, HEAD_DIM), lambda b: (b, 0, 0)),
                   pl.BlockSpec((ns, 8, N_SELBLK), lambda b: (b, 0, 0))],
        out_shape=[jax.ShapeDtypeStruct((nb, NSA_HEADS, HEAD_DIM), f32),
                   jax.ShapeDtypeStruct((nb, 8, N_SELBLK), f32)],
        compiler_params=_cparams(("parallel",)),
        name="nsa_sample_cmp",
    )(q8, comp, m_mat)


def _nsa_sample_topk_kernel(ps_ref, idx_ref, *, last_blk):
    ps = ps_ref[...]
    lane = lax.broadcasted_iota(jnp.int32, ps.shape, 1)
    score = ps + jnp.where((lane == 0) | (lane == last_blk), FORCE_BONUS, 0.0)
    out = jnp.zeros(ps.shape, jnp.int32)
    for r in range(N_PICK):
        best = jnp.max(score, axis=1, keepdims=True)
        idx = jnp.min(jnp.where(score == best, lane, N_SELBLK), axis=1, keepdims=True)
        out = jnp.where(lane == r, idx, out)
        score = jnp.where(lane == idx, REMOVED, score)
    idx_ref[...] = out


def _nsa_sample_topk(ps_rows, last_blk):
    return pl.pallas_call(
        functools.partial(_nsa_sample_topk_kernel, last_blk=last_blk),
        out_shape=jax.ShapeDtypeStruct(ps_rows.shape, jnp.int32),
        name="nsa_sample_topk",
    )(ps_rows)


def _nsa_sample_sel_kernel(pt_ref, idx_ref, slc_ref, q_ref, oc_ref, gate_ref, slc_new_ref, win_new_ref, wc_ref,
                           o_ref, wout_ref, kv_scr, sem_ref, *, t_pos):
    n_blk = NSA_KV_HEADS * N_PICK
    halves = PAGE_SIZE // SEL_BLOCK
    b = pl.program_id(0)
    nb = pl.num_programs(0)
    slot = lax.rem(b, 2)

    def picked(seq, k):
        return idx_ref[(seq * NSA_KV_HEADS + k // N_PICK) * N_SEL + k % N_PICK]

    def blk_copies(seq, s, k):
        page = pt_ref[seq, picked(seq, k) // halves]
        g = k // N_PICK
        return [pltpu.make_async_copy(slc_ref.at[page, kind * NSA_KV_HEADS + g], kv_scr.at[s, kind, k],
                                      sem_ref.at[s]) for kind in range(2)]

    def start_all(seq, s):
        for k in range(n_blk):
            for cp in blk_copies(seq, s, k):
                cp.start()

    @pl.when(b == 0)
    def _():
        start_all(0, 0)

    @pl.when(b + 1 < nb)
    def _():
        start_all(b + 1, 1 - slot)

    for k in range(n_blk):
        for cp in blk_copies(b, slot, k):
            cp.wait()

    q = q_ref[0] * jnp.asarray(HEAD_DIM ** -0.5, bf16)
    slope = _head_slopes()
    head = lax.broadcasted_iota(jnp.int32, (NSA_HEADS, 1), 0)
    lane = lax.broadcasted_iota(jnp.int32, (1, PAGE_SIZE), 1)
    new_slc = slc_new_ref[0]
    new_win = win_new_ref[0]
    wl = wc_ref.shape[3]
    wlane = lax.broadcasted_iota(jnp.int32, (1, wl), 1)
    d_w = (wl - wlane).astype(f32)
    qf = q.astype(f32)

    def attend(s, mask, v_t, new_row, g):
        k_new = new_row[:, g * HEAD_DIM:(g + 1) * HEAD_DIM].astype(bf16).astype(f32)
        v_new = new_row[:, KV_WIDTH + g * HEAD_DIM:KV_WIDTH + (g + 1) * HEAD_DIM].astype(bf16).astype(f32)
        s_new = jnp.sum(qf * k_new, axis=1, keepdims=True)
        m = jnp.maximum(jnp.max(s, axis=1, keepdims=True), s_new)
        e, e_new = jnp.exp(s - m), jnp.exp(s_new - m)
        if mask is not None:
            e = jnp.where(mask, e, 0.0)
        den = jnp.maximum(jnp.sum(e, axis=1, keepdims=True) + e_new, 1e-30)
        num = _dot_nt(e.astype(bf16), v_t.astype(bf16)) + e_new.astype(bf16).astype(f32) * v_new
        return num * (1.0 / den)

    o_s = jnp.zeros((NSA_HEADS, HEAD_DIM), f32)
    o_w = jnp.zeros((NSA_HEADS, HEAD_DIM), f32)
    for g in range(NSA_KV_HEADS):
        mine = jnp.right_shift(head, NSA_GROUP.bit_length() - 1) == g
        pos, member = [], []
        for n in range(N_PICK):
            j = picked(b, g * N_PICK + n)
            pos.append((j // halves) * PAGE_SIZE + lane)
            member.append(jnp.right_shift(lane, SEL_BLOCK.bit_length() - 1) == j % halves)
        pos = jnp.concatenate(pos, axis=1)
        member = jnp.concatenate(member, axis=1)
        k_t = jnp.concatenate([kv_scr[slot, 0, g * N_PICK + n] for n in range(N_PICK)], axis=1)
        v_t = jnp.concatenate([kv_scr[slot, 1, g * N_PICK + n] for n in range(N_PICK)], axis=1)
        s = jnp.where(member, _dot(q, k_t.astype(bf16)) - slope * (t_pos - pos).astype(f32), NEG)
        o_s = jnp.where(mine, attend(s, member, v_t, new_slc, g), o_s)
        s = _dot(q, wc_ref[0, g].astype(bf16)) - slope * d_w
        o_w = jnp.where(mine, attend(s, None, wc_ref[0, NSA_KV_HEADS + g], new_win, g), o_w)

    gate = jnp.broadcast_to(_sigmoid(gate_ref[0]), (NSA_HEADS, LANES))
    glane = lax.broadcasted_iota(jnp.int32, (NSA_HEADS, LANES), 1)
    gcol = [jnp.sum(jnp.where(glane == 3 * head + j, gate, 0.0), axis=1, keepdims=True) for j in range(3)]
    o_ref[0] = gcol[0] * oc_ref[0] + gcol[1] * o_s + gcol[2] * o_w
    eye = lax.broadcasted_iota(jnp.int32, (HEAD_DIM, HEAD_DIM), 0) == \
        lax.broadcasted_iota(jnp.int32, (HEAD_DIM, HEAD_DIM), 1)
    for c in range(N_COMBO):
        piece = jnp.broadcast_to(new_win[:, c * HEAD_DIM:(c + 1) * HEAD_DIM], (HEAD_DIM, HEAD_DIM))
        new_col = jnp.sum(jnp.where(eye, piece, 0.0), axis=1, keepdims=True)
        wout_ref[0, c] = jnp.where(wlane == wl - 1, new_col, pltpu.roll(wc_ref[0, c], wl - 1, axis=1))


def _nsa_sample_sel(page_table, idx_flat, slc_pages_t, q8, o_c, gates, slc_new, win_new, cache_win_t, t_pos):
    nb = q8.shape[0]
    wl = cache_win_t.shape[3]
    n_blk = NSA_KV_HEADS * N_PICK
    per_seq = lambda b, pt, idx: (b, 0, 0)
    win_blk = pl.BlockSpec((1, N_COMBO, HEAD_DIM, wl), lambda b, pt, idx: (b, 0, 0, 0))
    grid_spec = pltpu.PrefetchScalarGridSpec(
        num_scalar_prefetch=2,
        grid=(nb,),
        in_specs=[pl.BlockSpec(memory_space=pl.ANY),
                  pl.BlockSpec((1, NSA_HEADS, HEAD_DIM), per_seq),
                  pl.BlockSpec((1, NSA_HEADS, HEAD_DIM), per_seq),
                  pl.BlockSpec((1, 1, LANES), per_seq),
                  pl.BlockSpec((1, 1, ROW_WIDTH), per_seq),
                  pl.BlockSpec((1, 1, ROW_WIDTH), per_seq),
                  win_blk],
        out_specs=[pl.BlockSpec((1, NSA_HEADS, HEAD_DIM), per_seq), win_blk],
        scratch_shapes=[pltpu.VMEM((2, 2, n_blk, HEAD_DIM, PAGE_SIZE), f32),
                        pltpu.SemaphoreType.DMA((2,))],
    )
    return pl.pallas_call(
        functools.partial(_nsa_sample_sel_kernel, t_pos=t_pos),
        grid_spec=grid_spec,
        out_shape=[jax.ShapeDtypeStruct((nb, NSA_HEADS, HEAD_DIM), f32),
                   jax.ShapeDtypeStruct(cache_win_t.shape, f32)],
        compiler_params=_cparams(("arbitrary",)),
        name="nsa_sample_sel",
    )(page_table, idx_flat, slc_pages_t, q8, o_c, gates, slc_new, win_new, cache_win_t)


def kernel(x_prompt, x_sample, cache_cmp_kv, cache_slc_kv, cache_win_kv, state_hgrn, page_table, p_prompt, p_sample,
           w_in, g_pre, cmp_pe, cmp_w1, cmp_b1, cmp_w2, hg_lower, hg_norm, w_out, g_post, ple_proj, ple_gate):
    bsz, seq, _ = x_prompt.shape
    nb = x_sample.shape[0]
    n_pool = cache_cmp_kv.shape[1]
    n_pages = page_table.shape[1]
    past = n_pages * PAGE_SIZE
    kv_tail = (2, NSA_KV_HEADS, HEAD_DIM)

    o = _ORIG
    w = w_in[0]
    w_r = jnp.concatenate([w[:, o['q_a']:o['k_cmp']], w[:, o['z_a']:o['end']], w[:, o['k_cmp']:o['gate_a']],
                           w[:, o['gate_a']:o['z_a']],
                           jnp.zeros((D_MODEL, PROJ_WIDTH - o['end']), f32)], axis=1).astype(bf16)
    cw = _compress_weights(cmp_pe[0], cmp_w1[0], cmp_b1[0], cmp_w2[0])
    m_mat = _sel_map_matrix()
    w_out_b, ple_proj_b, ple_gate_b = w_out[0].astype(bf16), ple_proj[0].astype(bf16), ple_gate[0].astype(bf16)

    xp = x_prompt.reshape(bsz * seq, D_MODEL)
    zp, cmp_p, slc_p, win_p, gates_p = _proj(xp, g_pre, w_r, 1024)
    pages_p = cmp_p.reshape(bsz * seq // PAGE_SIZE, PAGE_SIZE, ROW_WIDTH)
    table_p = jnp.arange(bsz * seq // PAGE_SIZE, dtype=jnp.int32).reshape(bsz, seq // PAGE_SIZE)
    comp_p = _compress(pages_p, table_p, *cw, feature_major=False)
    q_t, k_slc, vt_slc, k_win, vt_win, cmp_tp, slc_tp, win_tp = _attention_layout(zp, cmp_p, slc_p, win_p, bsz, seq)
    o_a = _nsa_prompt_blocks(q_t, gates_p, comp_p, k_slc, vt_slc, k_win, vt_win, m_mat.T)
    yb_p, hg_p = _hgrn_prompt(zp, hg_lower, hg_norm, bsz, seq)
    h_p = _finish(xp, o_a, zp, yb_p, p_prompt[0].reshape(bsz * seq, PLE_DIM),
                  w_out_b, g_post, ple_proj_b, ple_gate_b, 1024)

    wl_p = min(WINDOW, seq)

    def cache_rows(rows_t):
        return rows_t.reshape((1, bsz) + kv_tail + rows_t.shape[2:]).transpose(0, 1, 5, 2, 3, 4)

    out_prompt = (h_p.reshape(bsz, seq, D_MODEL), cache_rows(cmp_tp), cache_rows(slc_tp),
                  cache_rows(win_tp[:, :, seq - wl_p:]), hg_p[None])

    xs = x_sample.reshape(nb, D_MODEL)
    zs, cmp_s, slc_s, win_new, gates_s = _proj(xs, g_pre, w_r, nb)
    cmp_t = cache_cmp_kv.transpose(0, 1, 3, 4, 5, 2).reshape(n_pool, ROW_WIDTH, PAGE_SIZE)
    comp_s = _compress(cmp_t, page_table, *cw, feature_major=True)
    q8 = zs[:, COL_QA:COL_QA + NSA_WIDTH].astype(bf16).reshape(nb, NSA_HEADS, HEAD_DIM)
    oc_s, ps_s = _nsa_sample_cmp(q8, comp_s, m_mat, past)
    ps_rows = ps_s[:, :NSA_KV_HEADS, :].reshape(nb * NSA_KV_HEADS, N_SELBLK)
    idx = _nsa_sample_topk(ps_rows, past // SEL_BLOCK - 1)
    slc_t = cache_slc_kv.transpose(0, 1, 3, 4, 5, 2).reshape(n_pool, N_COMBO, HEAD_DIM, PAGE_SIZE)
    wl_s = cache_win_kv.shape[2]
    win_t = cache_win_kv.transpose(0, 1, 3, 4, 5, 2).reshape(nb, N_COMBO, HEAD_DIM, wl_s)
    oa_s, win_s = _nsa_sample_sel(page_table, idx[:, :N_SEL].reshape(-1), slc_t, q8, oc_s,
                                  gates_s.reshape(nb, 1, LANES), slc_s.reshape(nb, 1, ROW_WIDTH),
                                  win_new.reshape(nb, 1, ROW_WIDTH), win_t, past)
    win_s = win_s.reshape((1, nb) + kv_tail + (wl_s,)).transpose(0, 1, 5, 2, 3, 4)
    yb_s, hg_s = _hgrn_sample(zs, state_hgrn, hg_lower, hg_norm)
    h_s = _finish(xs, oa_s.reshape(nb, NSA_WIDTH), zs, yb_s, p_sample[0].reshape(nb, PLE_DIM),
                  w_out_b, g_post, ple_proj_b, ple_gate_b, nb)

    out_sample = (h_s.reshape(nb, 1, D_MODEL),
                  cmp_s.reshape((1, nb, 1) + kv_tail),
                  slc_s.reshape((1, nb, 1) + kv_tail),
                  win_s,
                  hg_s)
    return (out_prompt[0], out_sample[0]) + out_prompt[1:] + out_sample[1:]
```

```python
import functools

import numpy as np
import jax
import jax.numpy as jnp
from jax import lax
from jax.experimental import pallas as pl
from jax.experimental.pallas import tpu as pltpu

D_MODEL = 1024
PAGE_SIZE = 128
NSA_HEADS = 8
NSA_KV_HEADS = 2
NSA_GROUP = NSA_HEADS // NSA_KV_HEADS
HEAD_DIM = 64
NSA_WIDTH = NSA_HEADS * HEAD_DIM
KV_WIDTH = NSA_KV_HEADS * HEAD_DIM
CMP_BLOCK = 32
CMP_STRIDE = 16
CMP_RATIO = CMP_BLOCK // CMP_STRIDE
CMP_HIDDEN = 128
SEL_BLOCK = 64
SEL_RATIO = SEL_BLOCK // CMP_STRIDE
N_SEL = 16
WINDOW = 512
Q_BLOCK = 128
HG_HEADS = 4
HG_DIM = 128
HG_WIDTH = HG_HEADS * HG_DIM
PLE_DIM = 256
EPS = 1e-6
NEG = -1e30
FORCE_BONUS = 1e6
REMOVED = -3e38

COL_QA, COL_ZA, COL_QB, COL_FB, COL_IB, COL_ZB = 0, 512, 1024, 1536, 2048, 2560
COL_CMP, COL_SLC, COL_WIN, COL_GATE = 3072, 3328, 3584, 3840
PROJ_WIDTH = 4096
_ORIG = dict(q_a=0, k_cmp=512, gate_a=1280, z_a=1304, q_b=1816, f_b=2328, i_b=2840, z_b=3352, end=3864)

LANES = 128
VMEM_LIMIT = 56 * 1024 * 1024

ROW_WIDTH = 2 * KV_WIDTH
N_COMBO = 2 * NSA_KV_HEADS
SEL_CHUNK = 512
WIN_KEYS = WINDOW + Q_BLOCK
N_CMP_PAD = 512
N_SELBLK = 128

f32 = jnp.float32
bf16 = jnp.bfloat16


def _cparams(sem):
    return pltpu.CompilerParams(dimension_semantics=sem, vmem_limit_bytes=VMEM_LIMIT)


def _dot(a, b):
    return jnp.dot(a, b, preferred_element_type=f32)


def _dot_nt(a, b):
    return lax.dot_general(a, b, (((1,), (1,)), ((), ())), preferred_element_type=f32)


def _dot_tn(a, b):
    return lax.dot_general(a, b, (((0,), (0,)), ((), ())), preferred_element_type=f32)


def _sigmoid(x):
    return 1.0 / (1.0 + jnp.exp(-x))


def _silu(x):
    return x * _sigmoid(x)


PROJ_TILE = 1024
MAIN_TILES = COL_CMP // PROJ_TILE


def _proj_kernel(x_ref, g_ref, w_ref, z_ref, cmp_ref, slc_ref, win_ref, gate_ref):
    x = x_ref[...]
    xn = (x * lax.rsqrt(jnp.mean(x * x, axis=-1, keepdims=True) + EPS) * g_ref[...]).astype(bf16)
    for j in range(MAIN_TILES):
        cols = slice(j * PROJ_TILE, (j + 1) * PROJ_TILE)
        z_ref[:, cols] = _dot(xn, w_ref[:, cols])
    base = COL_CMP
    res = _dot(xn, w_ref[:, base:])
    cmp_ref[...] = res[:, COL_CMP - base:COL_SLC - base]
    slc_ref[...] = res[:, COL_SLC - base:COL_WIN - base]
    win_ref[...] = res[:, COL_WIN - base:COL_GATE - base]
    gate_ref[...] = res[:, COL_GATE - base:COL_GATE - base + LANES]


def _proj(x, g, w, tm):
    n = x.shape[0]
    row = lambda i: (i, 0)
    const = lambda i: (0, 0)
    return pl.pallas_call(
        _proj_kernel,
        grid=(n // tm,),
        in_specs=[pl.BlockSpec((tm, D_MODEL), row),
                  pl.BlockSpec((1, D_MODEL), const),
                  pl.BlockSpec((D_MODEL, PROJ_WIDTH), const, pipeline_mode=pl.Buffered(1))],
        out_specs=[pl.BlockSpec((tm, COL_CMP), row),
                   pl.BlockSpec((tm, ROW_WIDTH), row), pl.BlockSpec((tm, ROW_WIDTH), row),
                   pl.BlockSpec((tm, ROW_WIDTH), row), pl.BlockSpec((tm, LANES), row)],
        out_shape=[jax.ShapeDtypeStruct((n, COL_CMP), f32),
                   jax.ShapeDtypeStruct((n, ROW_WIDTH), f32), jax.ShapeDtypeStruct((n, ROW_WIDTH), f32),
                   jax.ShapeDtypeStruct((n, ROW_WIDTH), f32), jax.ShapeDtypeStruct((n, LANES), f32)],
        compiler_params=_cparams(("parallel",)),
        name="proj",
    )(x, g, w)


def _finish_kernel(x_ref, oa_ref, za_ref, yb_ref, p_ref, wo_ref, gp_ref, pp_ref, pg_ref, h_ref):
    ya = (oa_ref[...] * _silu(za_ref[...])).astype(bf16)
    yb = yb_ref[...].astype(bf16)
    mix = _dot(ya, wo_ref[0:NSA_WIDTH, :]) + _dot(yb, wo_ref[NSA_WIDTH:, :])
    nrm = mix * lax.rsqrt(jnp.mean(mix * mix, axis=-1, keepdims=True) + EPS) * gp_ref[...]
    h = x_ref[...] + nrm
    gate = _sigmoid(_dot(h.astype(bf16), pg_ref[...]))
    h_ref[...] = h + _dot(p_ref[...].astype(bf16), pp_ref[...]) * gate


def _finish(x, oa, z, yb, p, w_out, g_post, ple_proj, ple_gate, tm):
    n = x.shape[0]
    za_blk = COL_ZA // NSA_WIDTH
    row = lambda i: (i, 0)
    const = lambda i: (0, 0)
    return pl.pallas_call(
        _finish_kernel,
        grid=(n // tm,),
        in_specs=[pl.BlockSpec((tm, D_MODEL), row),
                  pl.BlockSpec((tm, NSA_WIDTH), row),
                  pl.BlockSpec((tm, NSA_WIDTH), lambda i: (i, za_blk)),
                  pl.BlockSpec((tm, HG_WIDTH), row),
                  pl.BlockSpec((tm, PLE_DIM), row),
                  pl.BlockSpec((D_MODEL, D_MODEL), const),
                  pl.BlockSpec((1, D_MODEL), const),
                  pl.BlockSpec((PLE_DIM, D_MODEL), const),
                  pl.BlockSpec((D_MODEL, D_MODEL), const)],
        out_specs=pl.BlockSpec((tm, D_MODEL), row),
        out_shape=jax.ShapeDtypeStruct((n, D_MODEL), f32),
        compiler_params=_cparams(("parallel",)),
        name="finish",
    )(x, oa, z, yb, p, w_out, g_post, ple_proj, ple_gate)


SEGS_PER_PAGE = PAGE_SIZE // CMP_STRIDE


def _compress_kernel(pt_ref, pages_ref, wa_ref, wb_ref, pea_ref, peb_ref, b1_ref, w2_ref, out_ref,
                     stage_ref, rows_ref, sem_ref, *, n_pages, feature_major):
    b = pl.program_id(0)
    nb = pl.num_programs(0)
    lane_halves = ROW_WIDTH // LANES
    n_seg = n_pages * SEGS_PER_PAGE

    def page_copy(seq, j):
        s = lax.rem(seq, 2)
        return pltpu.make_async_copy(pages_ref.at[pt_ref[seq, j]], stage_ref.at[s, j], sem_ref.at[s])

    def start_all(seq):
        def body(j, c):
            page_copy(seq, j).start()
            return c
        lax.fori_loop(0, n_pages, body, 0)

    def wait_all(seq):
        def body(j, c):
            page_copy(seq, j).wait()
            return c
        lax.fori_loop(0, n_pages, body, 0)

    def to_rows(s):
        for j in range(n_pages):
            tile = stage_ref[s, j]
            rows = tile.T if feature_major else tile
            for h in range(lane_halves):
                rows_ref[s, h, j * PAGE_SIZE:(j + 1) * PAGE_SIZE, :] = rows[:, h * LANES:(h + 1) * LANES]

    def mlp(s):
        for kind in range(lane_halves):
            cols = slice(kind * LANES, (kind + 1) * LANES)
            seg_a, seg_b = [], []
            for j in range(CMP_STRIDE):
                r = rows_ref[s, kind, pl.ds(j, n_seg, stride=CMP_STRIDE), :]
                seg_a.append((r + pea_ref[j][:, cols]).astype(bf16))
                seg_b.append((r + peb_ref[j][:, cols]).astype(bf16))
            pa = _dot(jnp.concatenate(seg_a, axis=1), wa_ref[kind])
            pb = _dot(jnp.concatenate(seg_b, axis=1), wb_ref[kind])
            pb = pltpu.roll(pb, n_seg - 1, axis=0)
            hid = _silu(pa + pb + b1_ref[kind])
            for g in range(NSA_KV_HEADS):
                hg = hid[:, g * CMP_HIDDEN:(g + 1) * CMP_HIDDEN].astype(bf16)
                out_ref[0, kind * NSA_KV_HEADS + g] = _dot(hg, w2_ref[kind])

    @pl.when(b == 0)
    def _():
        start_all(0)

        @pl.when(nb > 1)
        def _():
            start_all(1)
        wait_all(0)
        to_rows(0)

    @pl.when(b + 2 < nb)
    def _():
        start_all(b + 2)

    @pl.when(b + 1 < nb)
    def _():
        wait_all(b + 1)

    for cur in range(2):
        @pl.when(lax.rem(b, 2) == cur)
        def _(cur=cur):
            to_rows(1 - cur)
            mlp(cur)


def _compress(pages, page_table, wa, wb, pea, peb, b1, w2, feature_major):
    nseq, n_pages = page_table.shape
    n_seg = n_pages * SEGS_PER_PAGE
    const3 = lambda b, pt: (0, 0, 0)
    once = pl.Buffered(1)
    grid_spec = pltpu.PrefetchScalarGridSpec(
        num_scalar_prefetch=1,
        grid=(nseq,),
        in_specs=[pl.BlockSpec(memory_space=pl.ANY),
                  pl.BlockSpec(wa.shape, const3, pipeline_mode=once),
                  pl.BlockSpec(wb.shape, const3, pipeline_mode=once),
                  pl.BlockSpec((CMP_STRIDE, 1, ROW_WIDTH), const3),
                  pl.BlockSpec((CMP_STRIDE, 1, ROW_WIDTH), const3),
                  pl.BlockSpec(b1.shape, const3),
                  pl.BlockSpec((2, CMP_HIDDEN, HEAD_DIM), const3)],
        out_specs=pl.BlockSpec((1, N_COMBO, n_seg, HEAD_DIM), lambda b, pt: (b, 0, 0, 0)),
        scratch_shapes=[pltpu.VMEM((2, n_pages) + pages.shape[1:], f32),
                        pltpu.VMEM((2, ROW_WIDTH // LANES, n_pages * PAGE_SIZE, LANES), f32),
                        pltpu.SemaphoreType.DMA((2,))],
    )
    return pl.pallas_call(
        functools.partial(_compress_kernel, n_pages=n_pages, feature_major=feature_major),
        grid_spec=grid_spec,
        out_shape=jax.ShapeDtypeStruct((nseq, N_COMBO, n_seg, HEAD_DIM), f32),
        compiler_params=_cparams(("arbitrary",)),
        name="compress",
    )(page_table, pages, wa, wb, pea, peb, b1, w2)


def _compress_weights(cmp_pe, cmp_w1, cmp_b1, cmp_w2):
    eye = jnp.eye(NSA_KV_HEADS, dtype=f32)

    def big(w_half):
        t = jnp.einsum('kjdh,gb->kjbdgh', w_half, eye)
        return t.reshape(2, CMP_STRIDE * KV_WIDTH, NSA_KV_HEADS * CMP_HIDDEN)

    w1 = cmp_w1.reshape(2, 2, CMP_STRIDE, HEAD_DIM, CMP_HIDDEN)
    wa = big(w1[:, 0]).astype(bf16)
    wb = big(w1[:, 1]).astype(bf16)
    pe = cmp_pe.reshape(2, 2, CMP_STRIDE, HEAD_DIM)

    def pe_row(p):
        t = jnp.broadcast_to(p.transpose(1, 0, 2)[:, :, None, :],
                             (CMP_STRIDE, 2, NSA_KV_HEADS, HEAD_DIM))
        return t.reshape(CMP_STRIDE, 1, ROW_WIDTH)

    pea, peb = pe_row(pe[:, 0]), pe_row(pe[:, 1])
    b1 = jnp.broadcast_to(cmp_b1[:, None, :], (2, NSA_KV_HEADS, CMP_HIDDEN)).reshape(2, 1, NSA_KV_HEADS * CMP_HIDDEN)
    return wa, wb, pea, peb, b1, cmp_w2.astype(bf16)


def _sel_map_matrix():
    wts = np.convolve(np.ones(SEL_RATIO), np.ones(CMP_RATIO))
    m = np.zeros((N_CMP_PAD, N_SELBLK), np.float32)
    for j in range(N_SELBLK):
        for k, wk in enumerate(wts):
            n = SEL_RATIO * j + k
            if n < N_CMP_PAD - 1:
                m[n, j] = wk
    return jnp.asarray(m, dtype=bf16)


def _importance_to_blocks(imp, m):
    hi = imp.astype(bf16)
    lo = (imp - hi.astype(f32)).astype(bf16)
    return _dot(hi, m) + _dot(lo, m)


LOG2E = 1.4426950408889634
MASK_BIG = 1e30
AUG_ROWS = 16
VT_ROWS = HEAD_DIM + 16
KEY_LOW = 256
N_FORCED = 3


NSA_STEP_BLOCKS = 2


def _nsa_prompt_blocks_kernel(qt_ref, gate_ref, comp_ref, caug_ref, ks_ref, vts_ref, kw_ref, vtw_ref, mt_ref, o_ref,
                              qa_ref, acc_ref, mg_ref, bias_ref, need_ref):
    cols = NSA_GROUP * Q_BLOCK
    n_blocks = qt_ref.shape[2]
    blocks = range(n_blocks)
    heads = range(NSA_KV_HEADS)
    insts = [(qb, g) for qb in blocks for g in heads]
    blk_per_chunk = SEL_CHUNK // SEL_BLOCK
    n_chunk_max = vts_ref.shape[2]

    lane = lax.broadcasted_iota(jnp.int32, (1, cols), 1)
    head = jnp.right_shift(lane, Q_BLOCK.bit_length() - 1)
    blk = lax.broadcasted_iota(jnp.int32, (N_SELBLK, Q_BLOCK), 0)
    qcol = lax.broadcasted_iota(jnp.int32, (N_SELBLK, Q_BLOCK), 1)
    c_end = lax.broadcasted_iota(jnp.int32, (N_CMP_PAD, 1), 0) * CMP_STRIDE + (CMP_BLOCK - 1)
    krow = lax.broadcasted_iota(jnp.int32, (SEL_CHUNK, 1), 0)
    wrow = lax.broadcasted_iota(jnp.int32, (WIN_KEYS, 1), 0)
    arow = lax.broadcasted_iota(jnp.int32, (AUG_ROWS // 2, cols), 0)
    gate_all = _sigmoid(gate_ref[...])

    q0 = [(pl.program_id(1) * n_blocks + qb) * Q_BLOCK for qb in blocks]
    tq = [q0[qb] + jnp.bitwise_and(lane, Q_BLOCK - 1) for qb in blocks]
    last = [q0[qb] // SEL_CHUNK for qb in blocks]
    w0 = [pl.multiple_of(jnp.maximum(q0[qb] - WINDOW, 0), Q_BLOCK) for qb in blocks]
    gate_t = [gate_all[qb * Q_BLOCK:(qb + 1) * Q_BLOCK].T for qb in blocks]

    slope, alibi_rows, pos_rows = [], [], []
    for g in heads:
        c = jnp.exp2(-(head.astype(f32) + 1.0 + NSA_GROUP * g)) * LOG2E
        c_hi = c.astype(bf16).astype(f32)
        c_mid = (c - c_hi).astype(bf16).astype(f32)
        c_lo = (c - c_hi - c_mid).astype(bf16).astype(f32)
        c3 = jnp.where((arow == 0) | (arow == 3), c_hi, jnp.where((arow == 1) | (arow == 4), c_mid, c_lo))
        slope.append(c)
        alibi_rows.append(jnp.where(arow >= 6, 0.0, c3))
        rows = jnp.where(arow >= 6, 0.0, jnp.where(arow >= 3, KEY_LOW * c3, c3))
        pos_rows.append(jnp.concatenate([rows, jnp.zeros_like(rows)], axis=0).astype(bf16))

    def chunk_scores(qb, g, ck, slot=0, live=None):
        k0 = pl.multiple_of(ck * SEL_CHUNK, SEL_CHUNK)
        b8 = bias_ref[qb, g, pl.ds(pl.multiple_of(ck * blk_per_chunk, blk_per_chunk), blk_per_chunk), :]
        if live is not None:
            b8 = jnp.where(live, b8, -MASK_BIG)
        qa_ref[qb, g, slot, HEAD_DIM:HEAD_DIM + AUG_ROWS, :] = jnp.concatenate(
            [jnp.concatenate([b8] * NSA_GROUP, axis=1), alibi_rows[g]], axis=0).astype(bf16)
        return _dot(ks_ref[0, g, pl.ds(k0, SEL_CHUNK), :], qa_ref[qb, g, slot])

    def sel_chunk(qb, g, ck, causal, s):
        k0 = pl.multiple_of(ck * SEL_CHUNK, SEL_CHUNK)
        if causal:
            s = jnp.where(k0 + krow <= tq[qb], s, NEG)
        shift = slope[g] * k0.astype(f32)
        m_old = mg_ref[qb, g] - shift
        m_new = jnp.maximum(m_old, jnp.max(s, axis=0, keepdims=True))
        p = jnp.exp2(s - m_new).astype(bf16)
        acc_ref[qb, g] = acc_ref[qb, g] * jnp.exp2(m_old - m_new) + _dot(vts_ref[0, g, ck], p)
        mg_ref[qb, g] = m_new + shift

    s_cmp, s_win = {}, {}
    for qb, g in insts:
        q_pos = jnp.concatenate([qt_ref[0, g, qb], pos_rows[g],
                                 jnp.zeros((LANES - HEAD_DIM - AUG_ROWS, cols), bf16)], axis=0)
        kc = jnp.concatenate([comp_ref[0, g].astype(bf16), caug_ref[...]], axis=1)
        s_cmp[qb, g] = _dot(kc, q_pos)
        kw = kw_ref[0, g, pl.ds(w0[qb], WIN_KEYS), :]
        s_win[qb, g] = _dot(kw, q_pos)

    oc_t, ps_t = {}, {}
    for qb in blocks:
        seen = c_end <= tq[qb]
        any_seen = tq[qb] >= CMP_BLOCK - 1
        for g in heads:
            s_c = jnp.where(seen, s_cmp[qb, g], NEG)
            e = jnp.exp2(s_c - jnp.max(s_c, axis=0, keepdims=True))
            rl = jnp.where(any_seen, 1.0 / jnp.maximum(jnp.sum(e, axis=0, keepdims=True), 1e-30), 0.0)
            p_c = e * rl
            oc_t[qb, g] = _dot_tn(comp_ref[0, NSA_KV_HEADS + g].astype(bf16), p_c.astype(bf16))
            imp_t = (p_c[:, 0:Q_BLOCK] + p_c[:, Q_BLOCK:2 * Q_BLOCK]
                     + p_c[:, 2 * Q_BLOCK:3 * Q_BLOCK] + p_c[:, 3 * Q_BLOCK:4 * Q_BLOCK])
            hi = imp_t.astype(bf16)
            lo = (imp_t - hi.astype(f32)).astype(bf16)
            ps_t[qb, g] = _dot(mt_ref[...], hi) + _dot(mt_ref[...], lo)

    ow_t = {}
    for qb in blocks:
        wkey = w0[qb] + wrow
        in_window = (wkey <= tq[qb]) & (wkey >= tq[qb] - WINDOW)
        t0 = w0[qb] // Q_BLOCK
        for g in heads:
            s_w = jnp.where(in_window, s_win[qb, g], NEG)
            e = jnp.exp2(s_w - jnp.max(s_w, axis=0, keepdims=True)).astype(bf16)
            acc_w = jnp.zeros((VT_ROWS, cols), f32)
            for t in range(WIN_KEYS // Q_BLOCK):
                acc_w = acc_w + _dot(vtw_ref[0, g, t0 + t], e[t * Q_BLOCK:(t + 1) * Q_BLOCK])
            ow_t[qb, g] = acc_w[0:HEAD_DIM] * (1.0 / jnp.maximum(acc_w[HEAD_DIM:HEAD_DIM + 1], 1e-30))

    valid, sel_t, score = {}, {}, {}
    for qb in blocks:
        jt = jnp.right_shift(q0[qb] + qcol, SEL_BLOCK.bit_length() - 1)
        valid[qb] = blk <= jt
        forced = (blk == 0) | (blk == jt) | (blk == jt - 1)
        for g in heads:
            sel_t[qb, g] = jnp.where(forced, 1.0, 0.0)
            score[qb, g] = jnp.where(valid[qb] & jnp.logical_not(forced), ps_t[qb, g], NEG)
    for _ in range(N_SEL - N_FORCED):
        for inst in insts:
            best = jnp.max(score[inst], axis=0, keepdims=True)
            idx = jnp.min(jnp.where(score[inst] == best, blk, N_SELBLK), axis=0, keepdims=True)
            hit = blk == idx
            sel_t[inst] = jnp.where(hit, 1.0, sel_t[inst])
            score[inst] = jnp.where(hit, REMOVED, score[inst])
    for qb, g in insts:
        sel_t[qb, g] = jnp.where(valid[qb], sel_t[qb, g], 0.0)

    for qb, g in insts:
        bias_ref[qb, g] = (sel_t[qb, g] - 1.0) * MASK_BIG
        for slot in range(qa_ref.shape[2]):
            qa_ref[qb, g, slot, 0:HEAD_DIM, :] = qt_ref[0, g, qb]
            qa_ref[qb, g, slot, HEAD_DIM + AUG_ROWS:, :] = jnp.zeros((LANES - HEAD_DIM - AUG_ROWS, cols), bf16)
        acc_ref[qb, g] = jnp.zeros(acc_ref.shape[2:], f32)
        mg_ref[qb, g] = jnp.full(mg_ref.shape[2:], NEG, f32)
    own = {(qb, g): chunk_scores(qb, g, last[qb]) for qb, g in insts}
    for qb, g in insts:
        sel_chunk(qb, g, last[qb], True, own[qb, g])

    n_need = []
    for qb in blocks:
        count = jnp.int32(0)
        for ck in range(n_chunk_max):
            rows = slice(ck * blk_per_chunk, (ck + 1) * blk_per_chunk)
            picked = jnp.maximum(*[jnp.max(sel_t[qb, g][rows, :]) for g in heads])
            need_ref[qb, count] = ck
            count = count + ((picked > 0.5) & (ck < last[qb])).astype(jnp.int32)
        n_need.append(count)

    for qb in blocks:
        def chunk_pair(i, carry, qb=qb):
            first = 2 * i
            live = first + 1 < n_need[qb]
            cks = [need_ref[qb, first], need_ref[qb, jnp.minimum(first + 1, n_need[qb] - 1)]]
            scores = [[chunk_scores(qb, g, cks[slot], slot, live if slot else None) for g in heads]
                      for slot in range(2)]
            for slot in range(2):
                for g in heads:
                    sel_chunk(qb, g, cks[slot], False, scores[slot][g])
            return carry

        lax.fori_loop(0, (n_need[qb] + 1) // 2, chunk_pair, 0)

    for qb, g in insts:
        acc = acc_ref[qb, g]
        os_t = acc[0:HEAD_DIM] * (1.0 / jnp.maximum(acc[HEAD_DIM:HEAD_DIM + 1], 1e-30))
        outs = []
        for r in range(NSA_GROUP):
            cs = slice(r * Q_BLOCK, (r + 1) * Q_BLOCK)
            row = (g * NSA_GROUP + r) * 3
            outs.append(gate_t[qb][row:row + 1, :] * oc_t[qb, g][:, cs] + gate_t[qb][row + 1:row + 2, :] * os_t[:, cs]
                        + gate_t[qb][row + 2:row + 3, :] * ow_t[qb, g][:, cs])
        for pair in range(NSA_GROUP // 2):
            both = jnp.concatenate([outs[2 * pair], outs[2 * pair + 1]], axis=0)
            lane0 = (g * NSA_GROUP // 2 + pair) * LANES
            o_ref[qb * Q_BLOCK:(qb + 1) * Q_BLOCK, lane0:lane0 + LANES] = both.T


def _nsa_prompt_blocks(q_t, gates, comp, k_slc, vt_slc, k_win, vt_win, m_t):
    bsz, _, nq = q_t.shape[:3]
    seq = k_slc.shape[2]
    cols = NSA_GROUP * Q_BLOCK
    nblk = NSA_STEP_BLOCKS
    steps = nq // nblk
    per_b4 = lambda b, i: (b, 0, 0, 0)
    per_b5 = lambda b, i: (b, 0, 0, 0, 0)
    inst = (nblk, NSA_KV_HEADS)
    return pl.pallas_call(
        _nsa_prompt_blocks_kernel,
        grid=(bsz, steps),
        in_specs=[pl.BlockSpec((1, NSA_KV_HEADS, nblk, HEAD_DIM, cols), lambda b, i: (b, 0, i, 0, 0)),
                  pl.BlockSpec((nblk * Q_BLOCK, LANES), lambda b, i: (b * steps + i, 0)),
                  pl.BlockSpec((1, N_COMBO, N_CMP_PAD, HEAD_DIM), per_b4),
                  pl.BlockSpec((N_CMP_PAD, HEAD_DIM), lambda b, i: (0, 0)),
                  pl.BlockSpec((1, NSA_KV_HEADS, seq, LANES), per_b4),
                  pl.BlockSpec((1,) + vt_slc.shape[1:], per_b5),
                  pl.BlockSpec((1, NSA_KV_HEADS, seq, LANES), per_b4),
                  pl.BlockSpec((1,) + vt_win.shape[1:], per_b5),
                  pl.BlockSpec((N_SELBLK, N_CMP_PAD), lambda b, i: (0, 0))],
        out_specs=pl.BlockSpec((nblk * Q_BLOCK, NSA_WIDTH), lambda b, i: (b * steps + i, 0)),
        out_shape=jax.ShapeDtypeStruct((bsz * seq, NSA_WIDTH), f32),
        scratch_shapes=[pltpu.VMEM(inst + (2, LANES, cols), bf16),
                        pltpu.VMEM(inst + (VT_ROWS, cols), f32),
                        pltpu.VMEM(inst + (1, cols), f32),
                        pltpu.VMEM(inst + (N_SELBLK, Q_BLOCK), f32),
                        pltpu.SMEM((nblk, vt_slc.shape[2]), jnp.int32)],
        compiler_params=_cparams(("parallel", "arbitrary")),
        name="nsa_prompt",
    )(q_t, gates, comp, _cmp_position_columns(), k_slc, vt_slc, k_win, vt_win, m_t)


def _position_columns(pos):
    low, high = (pos % KEY_LOW)[:, None], (pos // KEY_LOW)[:, None]
    return np.concatenate([low, low, low, high, high, high], axis=1).astype(np.float32)


def _cmp_position_columns():
    aug = np.zeros((N_CMP_PAD, HEAD_DIM), np.float32)
    aug[:, 0:6] = _position_columns(np.arange(N_CMP_PAD) * CMP_STRIDE + CMP_BLOCK - 1)
    return jnp.asarray(aug, dtype=bf16)


LAYOUT_ROWS = 1024


def _attention_layout_kernel(q_ref, cmp_ref, slc_ref, win_ref, xs_ref, xw_ref,
                             qt_ref, ks_ref, vts_ref, kw_ref, vtw_ref, cmp_t_ref, slc_t_ref, win_t_ref):
    cmp_t_ref[0] = cmp_ref[...].T
    tm = q_ref.shape[0]
    q = q_ref[...] * (HEAD_DIM ** -0.5 * LOG2E)
    for qb in range(tm // Q_BLOCK):
        rows = slice(qb * Q_BLOCK, (qb + 1) * Q_BLOCK)
        for pair in range(NSA_HEADS // 2):
            t = q[rows, pair * LANES:(pair + 1) * LANES].T
            for k in range(2):
                g, r = divmod(2 * pair + k, NSA_GROUP)
                qt_ref[0, g, qb, :, r * Q_BLOCK:(r + 1) * Q_BLOCK] = t[k * HEAD_DIM:(k + 1) * HEAD_DIM].astype(bf16)

    for src_ref, extra_ref, k_ref, vt_ref, t_ref in ((slc_ref, xs_ref, ks_ref, vts_ref, slc_t_ref),
                                                     (win_ref, xw_ref, kw_ref, vtw_ref, win_t_ref)):
        tile = vt_ref.shape[4]
        s = src_ref[...]
        s_t = s.T
        t_ref[0] = s_t
        v_t = s_t[KV_WIDTH:]
        tail = jnp.where(lax.broadcasted_iota(jnp.int32, (VT_ROWS - HEAD_DIM, tile), 0) == 0, 1.0, 0.0)
        for g in range(NSA_KV_HEADS):
            k_ref[0, g] = jnp.concatenate([s[:, g * HEAD_DIM:(g + 1) * HEAD_DIM].astype(bf16), extra_ref[...]], axis=1)
            for c in range(tm // tile):
                blk = v_t[g * HEAD_DIM:(g + 1) * HEAD_DIM, c * tile:(c + 1) * tile]
                vt_ref[0, g, c] = jnp.concatenate([blk, tail], axis=0).astype(bf16)


def _attention_layout(z, cmp, slc, win, bsz, seq):
    tm = LAYOUT_ROWS
    per_b = seq // tm
    pos = np.arange(seq)
    xs = np.zeros((seq, HEAD_DIM), np.float32)
    xs[pos, (pos // SEL_BLOCK) % (SEL_CHUNK // SEL_BLOCK)] = 1.0
    xs[:, 8:11] = (pos % KEY_LOW)[:, None]
    xs[:, 11:14] = (pos % SEL_CHUNK - pos % KEY_LOW)[:, None]
    xw = np.zeros((seq, HEAD_DIM), np.float32)
    xw[:, 0:6] = _position_columns(pos)
    row = lambda i: (i, 0)
    rep = lambda i: (i % per_b, 0)
    out4 = lambda i: (i // per_b, 0, i % per_b, 0)
    out5 = lambda i: (i // per_b, 0, i % per_b, 0, 0)
    k_shape = jax.ShapeDtypeStruct((bsz, NSA_KV_HEADS, seq, LANES), bf16)
    vt_shape = lambda tile: jax.ShapeDtypeStruct((bsz, NSA_KV_HEADS, seq // tile, VT_ROWS, tile), bf16)
    vt_spec = lambda tile: pl.BlockSpec((1, NSA_KV_HEADS, tm // tile, VT_ROWS, tile), out5)
    rows_t_shape = jax.ShapeDtypeStruct((bsz, ROW_WIDTH, seq), f32)
    rows_t_spec = pl.BlockSpec((1, ROW_WIDTH, tm), lambda i: (i // per_b, 0, i % per_b))
    return pl.pallas_call(
        _attention_layout_kernel,
        grid=(bsz * per_b,),
        in_specs=[pl.BlockSpec((tm, NSA_WIDTH), row), pl.BlockSpec((tm, ROW_WIDTH), row),
                  pl.BlockSpec((tm, ROW_WIDTH), row), pl.BlockSpec((tm, ROW_WIDTH), row),
                  pl.BlockSpec((tm, HEAD_DIM), rep), pl.BlockSpec((tm, HEAD_DIM), rep)],
        out_specs=[pl.BlockSpec((1, NSA_KV_HEADS, tm // Q_BLOCK, HEAD_DIM, NSA_GROUP * Q_BLOCK), out5),
                   pl.BlockSpec((1, NSA_KV_HEADS, tm, LANES), out4), vt_spec(SEL_CHUNK),
                   pl.BlockSpec((1, NSA_KV_HEADS, tm, LANES), out4), vt_spec(Q_BLOCK),
                   rows_t_spec, rows_t_spec, rows_t_spec],
        out_shape=[jax.ShapeDtypeStruct((bsz, NSA_KV_HEADS, seq // Q_BLOCK, HEAD_DIM, NSA_GROUP * Q_BLOCK), bf16),
                   k_shape, vt_shape(SEL_CHUNK), k_shape, vt_shape(Q_BLOCK),
                   rows_t_shape, rows_t_shape, rows_t_shape],
        compiler_params=_cparams(("parallel",)),
        name="attention_layout",
    )(z, cmp, slc, win, jnp.asarray(xs, dtype=bf16), jnp.asarray(xw, dtype=bf16))


HG_CHUNK = 128
HG_STEP_CHUNKS = 4
HG_BAND = 8


def _lower_bound(lo):
    m = jnp.max(lo, axis=0, keepdims=True)
    e = jnp.exp(lo - m)
    return e[0:1] / jnp.sum(e, axis=0, keepdims=True)


def _hgrn_gates(q, fl, lb):
    sg = _sigmoid(fl)
    f = lb + (1.0 - lb) * sg
    k = (1.0 - lb) * (1.0 - sg)
    return _silu(q), k, jnp.log(f)


def _hgrn_out(o, nw, zb):
    o = o * lax.rsqrt(jnp.mean(o * o, axis=-1, keepdims=True) + EPS) * nw
    return o * _silu(zb)


def _hgrn_prompt_kernel(q_ref, f_ref, i_ref, zb_ref, lo_ref, nw_ref, y_ref, s_ref, st_ref):
    c = pl.program_id(1)

    @pl.when(c == 0)
    def _():
        st_ref[...] = jnp.zeros_like(st_ref)

    n = HG_CHUNK
    row = lax.broadcasted_iota(jnp.int32, (n, n), 0)
    col = lax.broadcasted_iota(jnp.int32, (n, n), 1)
    heads = [slice(h * HG_DIM, (h + 1) * HG_DIM) for h in range(HG_HEADS)]

    qq, kk, logf = _hgrn_gates(q_ref[...], f_ref[...], _lower_bound(lo_ref[...]))
    v = i_ref[...]
    chunks = [slice(i * n, (i + 1) * n) for i in range(q_ref.shape[0] // n)]
    tril = jnp.where(col <= row, 1.0, 0.0).astype(bf16)
    hi = logf.astype(bf16)
    r1 = logf - hi.astype(f32)
    mid = r1.astype(bf16)
    lo = (r1 - mid.astype(f32)).astype(bf16)
    bs = [_dot(tril, hi[rs]) + _dot(tril, mid[rs]) + _dot(tril, lo[rs]) for rs in chunks]

    intra = [[_hgrn_intra(qq[rs, hs], kk[rs, hs], b[:, hs], row, col) for hs in heads] for rs, b in zip(chunks, bs)]
    for i, rs in enumerate(chunks):
        for h, hs in enumerate(heads):
            st = st_ref[h]
            bh, kh, vh = bs[i][:, hs], kk[rs, hs], v[rs, hs].astype(bf16)
            o = _dot(intra[i][h].astype(bf16), vh) + _dot_nt((qq[rs, hs] * jnp.exp(bh)).astype(bf16), st.astype(bf16))
            b_last = bh[n - 1:n]
            st_new = st * jnp.exp(b_last) + _dot_tn(vh, (kh * jnp.exp(b_last - bh)).astype(bf16))
            st_ref[h] = st_new
            y_ref[rs, hs] = _hgrn_out(o, nw_ref[...], zb_ref[rs, hs])

    @pl.when(c == pl.num_programs(1) - 1)
    def _():
        for h in range(HG_HEADS):
            s_ref[0, h] = st_ref[h].T


def _hgrn_intra(qq, kk, b, row, col):
    n = HG_CHUNK
    a = jnp.where(row == col, _dot_nt(qq.astype(bf16), kk.astype(bf16)), 0.0)

    sub = jnp.bitwise_and(row, HG_BAND - 1)
    b3 = b.reshape(n // HG_BAND, HG_BAND, HG_DIM)
    size = 2
    while size <= HG_BAND:
        half = size // 2
        beta = None
        for s0 in range(0, HG_BAND, size):
            piece = jnp.broadcast_to(b3[:, s0 + half - 1:s0 + half, :], b3.shape).reshape(n, HG_DIM)
            beta = piece if beta is None else jnp.where(sub >= s0, piece, beta)
        left = jnp.bitwise_and(sub, size - 1) < half
        decay = jnp.exp(jnp.where(left, beta - b, b - beta))
        part = _dot_nt(jnp.where(left, 0.0, qq * decay).astype(bf16), jnp.where(left, kk * decay, 0.0).astype(bf16))
        sh = size.bit_length() - 1
        a = a + jnp.where(jnp.right_shift(row, sh) == jnp.right_shift(col, sh), part, 0.0)
        size *= 2

    while size <= n:
        half = size // 2
        qs, ks = [], []
        zero = jnp.zeros((half, HG_DIM), f32)
        for r0 in range(0, n, size):
            beta = b[r0 + half - 1:r0 + half]
            ks += [kk[r0:r0 + half] * jnp.exp(beta - b[r0:r0 + half]), zero]
            qs += [zero, qq[r0 + half:r0 + size] * jnp.exp(b[r0 + half:r0 + size] - beta)]
        part = _dot_nt(jnp.concatenate(qs, axis=0).astype(bf16), jnp.concatenate(ks, axis=0).astype(bf16))
        if size < n:
            sh = size.bit_length() - 1
            part = jnp.where(jnp.right_shift(row, sh) == jnp.right_shift(col, sh), part, 0.0)
        a = a + part
        size *= 2
    return a


def _hgrn_prompt(z, hg_lower, hg_norm, bsz, seq):
    rows = HG_STEP_CHUNKS * HG_CHUNK
    nch = seq // rows

    def zcol(base):
        return pl.BlockSpec((rows, HG_WIDTH), lambda b, c: (b * nch + c, base // HG_WIDTH))

    return pl.pallas_call(
        _hgrn_prompt_kernel,
        grid=(bsz, nch),
        in_specs=[zcol(COL_QB), zcol(COL_FB), zcol(COL_IB), zcol(COL_ZB),
                  pl.BlockSpec(hg_lower.shape, lambda b, c: (0, 0)),
                  pl.BlockSpec((1, HG_DIM), lambda b, c: (0, 0))],
        out_specs=[pl.BlockSpec((rows, HG_WIDTH), lambda b, c: (b * nch + c, 0)),
                   pl.BlockSpec((1, HG_HEADS, HG_DIM, HG_DIM), lambda b, c: (b, 0, 0, 0))],
        out_shape=[jax.ShapeDtypeStruct((bsz * seq, HG_WIDTH), f32),
                   jax.ShapeDtypeStruct((bsz, HG_HEADS, HG_DIM, HG_DIM), f32)],
        scratch_shapes=[pltpu.VMEM((HG_HEADS, HG_DIM, HG_DIM), f32)],
        compiler_params=_cparams(("parallel", "arbitrary")),
        name="hgrn_prompt",
    )(z, z, z, z, hg_lower, hg_norm)


def _hgrn_sample_kernel(q_ref, f_ref, i_ref, zb_ref, lo_ref, nw_ref, s_ref, y_ref, so_ref, o_scr):
    nb = q_ref.shape[0]
    lb = _lower_bound(lo_ref[...])
    qq, kk, logf = _hgrn_gates(q_ref[...], f_ref[...], lb)
    v = i_ref[...]
    pad = jnp.zeros((HG_DIM - nb, HG_DIM), f32)
    to_cols = lambda x: jnp.concatenate([x, pad], axis=0).T
    f_t, k_t, q_t = to_cols(jnp.exp(logf)), to_cols(kk), to_cols(qq)
    for b in range(nb):
        s_new = f_t[:, b:b + 1] * s_ref[0, b, 0] + k_t[:, b:b + 1] * v[b:b + 1, :]
        so_ref[0, b, 0] = s_new
        o_scr[b:b + 1, :] = jnp.sum(s_new * q_t[:, b:b + 1], axis=0, keepdims=True)
    y_ref[...] = _hgrn_out(o_scr[...], nw_ref[...], zb_ref[...])


def _hgrn_sample(z, state, hg_lower, hg_norm):
    nb = z.shape[0]

    def zcol(base):
        return pl.BlockSpec((nb, HG_DIM), lambda h: (0, base // HG_DIM + h))

    sblk = pl.BlockSpec((1, nb, 1, HG_DIM, HG_DIM), lambda h: (0, 0, h, 0, 0))
    return pl.pallas_call(
        _hgrn_sample_kernel,
        grid=(HG_HEADS,),
        in_specs=[zcol(COL_QB), zcol(COL_FB), zcol(COL_IB), zcol(COL_ZB),
                  pl.BlockSpec((hg_lower.shape[0], HG_DIM), lambda h: (0, h)),
                  pl.BlockSpec((1, HG_DIM), lambda h: (0, 0)),
                  sblk],
        out_specs=[pl.BlockSpec((nb, HG_DIM), lambda h: (0, h)), sblk],
        out_shape=[jax.ShapeDtypeStruct((nb, HG_WIDTH), f32),
                   jax.ShapeDtypeStruct(state.shape, f32)],
        scratch_shapes=[pltpu.VMEM((nb, HG_DIM), f32)],
        compiler_params=_cparams(("parallel",)),
        name="hgrn_sample",
    )(z, z, z, z, hg_lower, hg_norm, state)


N_PICK = N_SEL - 1


def _head_slopes():
    h = lax.broadcasted_iota(jnp.int32, (NSA_HEADS, 1), 0).astype(f32)
    return jnp.exp2(-(h + 1.0))


SAMPLE_CMP_SEQS = 8


def _nsa_sample_cmp_kernel(q_ref, comp_ref, m_ref, oc_ref, ps_ref, *, t_pos):
    slope = _head_slopes()
    head = lax.broadcasted_iota(jnp.int32, (NSA_HEADS, 1), 0)
    ncol = lax.broadcasted_iota(jnp.int32, (1, N_CMP_PAD), 1)
    d_c = t_pos - (ncol * CMP_STRIDE + (CMP_BLOCK - 1))
    mask = d_c >= 0
    ps_ref[...] = jnp.zeros_like(ps_ref)
    for b in range(q_ref.shape[0]):
        q = q_ref[b] * jnp.asarray(HEAD_DIM ** -0.5, bf16)
        o_c = jnp.zeros((NSA_HEADS, HEAD_DIM), f32)
        for g in range(NSA_KV_HEADS):
            kc = comp_ref[b, g].astype(bf16)
            vc = comp_ref[b, NSA_KV_HEADS + g].astype(bf16)
            s = jnp.where(mask, _dot_nt(q, kc) - slope * d_c.astype(f32), NEG)
            m = jnp.max(s, axis=1, keepdims=True)
            e = jnp.where(mask, jnp.exp(s - m), 0.0)
            p = e * (1.0 / jnp.maximum(jnp.sum(e, axis=1, keepdims=True), 1e-30))
            mine = jnp.right_shift(head, NSA_GROUP.bit_length() - 1) == g
            o_c = jnp.where(mine, _dot(p.astype(bf16), vc), o_c)
            imp = jnp.sum(jnp.where(mine, p, 0.0), axis=0, keepdims=True)
            ps_ref[b, g:g + 1, :] = _importance_to_blocks(imp, m_ref[...])
        oc_ref[b] = o_c


def _nsa_sample_cmp(q8, comp, m_mat, t_pos):
    nb = q8.shape[0]
    ns = SAMPLE_CMP_SEQS
    return pl.pallas_call(
        functools.partial(_nsa_sample_cmp_kernel, t_pos=t_pos),
        grid=(nb // ns,),
        in_specs=[pl.BlockSpec((ns, NSA_HEADS, HEAD_DIM), lambda b: (b, 0, 0)),
                  pl.BlockSpec((ns, N_COMBO, N_CMP_PAD, HEAD_DIM), lambda b: (b, 0, 0, 0)),
                  pl.BlockSpec((N_CMP_PAD, N_SELBLK), lambda b: (0, 0))],
        out_specs=[pl.BlockSpec((ns, NSA_HEADS, HEAD_DIM), lambda b: (b, 0, 0)),
                   pl.BlockSpec((ns, 8, N_SELBLK), lambda b: (b, 0, 0))],
        out_shape=[jax.ShapeDtypeStruct((nb, NSA_HEADS, HEAD_DIM), f32),
                   jax.ShapeDtypeStruct((nb, 8, N_SELBLK), f32)],
        compiler_params=_cparams(("parallel",)),
        name="nsa_sample_cmp",
    )(q8, comp, m_mat)


def _nsa_sample_topk_kernel(ps_ref, idx_ref, *, last_blk):
    ps = ps_ref[...]
    lane = lax.broadcasted_iota(jnp.int32, ps.shape, 1)
    score = ps + jnp.where((lane == 0) | (lane == last_blk), FORCE_BONUS, 0.0)
    out = jnp.zeros(ps.shape, jnp.int32)
    for r in range(N_PICK):
        best = jnp.max(score, axis=1, keepdims=True)
        idx = jnp.min(jnp.where(score == best, lane, N_SELBLK), axis=1, keepdims=True)
        out = jnp.where(lane == r, idx, out)
        score = jnp.where(lane == idx, REMOVED, score)
    idx_ref[...] = out


def _nsa_sample_topk(ps_rows, last_blk):
    return pl.pallas_call(
        functools.partial(_nsa_sample_topk_kernel, last_blk=last_blk),
        out_shape=jax.ShapeDtypeStruct(ps_rows.shape, jnp.int32),
        name="nsa_sample_topk",
    )(ps_rows)


def _nsa_sample_sel_kernel(pt_ref, idx_ref, slc_ref, q_ref, oc_ref, gate_ref, slc_new_ref, win_new_ref, wc_ref,
                           o_ref, wout_ref, kv_scr, sem_ref, *, t_pos):
    n_blk = NSA_KV_HEADS * N_PICK
    halves = PAGE_SIZE // SEL_BLOCK
    b = pl.program_id(0)
    nb = pl.num_programs(0)
    slot = lax.rem(b, 2)

    def picked(seq, k):
        return idx_ref[(seq * NSA_KV_HEADS + k // N_PICK) * N_SEL + k % N_PICK]

    def blk_copies(seq, s, k):
        page = pt_ref[seq, picked(seq, k) // halves]
        g = k // N_PICK
        return [pltpu.make_async_copy(slc_ref.at[page, kind * NSA_KV_HEADS + g], kv_scr.at[s, kind, k],
                                      sem_ref.at[s]) for kind in range(2)]

    def start_all(seq, s):
        for k in range(n_blk):
            for cp in blk_copies(seq, s, k):
                cp.start()

    @pl.when(b == 0)
    def _():
        start_all(0, 0)

    @pl.when(b + 1 < nb)
    def _():
        start_all(b + 1, 1 - slot)

    for k in range(n_blk):
        for cp in blk_copies(b, slot, k):
            cp.wait()

    q = q_ref[0] * jnp.asarray(HEAD_DIM ** -0.5, bf16)
    slope = _head_slopes()
    head = lax.broadcasted_iota(jnp.int32, (NSA_HEADS, 1), 0)
    lane = lax.broadcasted_iota(jnp.int32, (1, PAGE_SIZE), 1)
    new_slc = slc_new_ref[0]
    new_win = win_new_ref[0]
    wl = wc_ref.shape[3]
    wlane = lax.broadcasted_iota(jnp.int32, (1, wl), 1)
    d_w = (wl - wlane).astype(f32)
    qf = q.astype(f32)

    def attend(s, mask, v_t, new_row, g):
        k_new = new_row[:, g * HEAD_DIM:(g + 1) * HEAD_DIM].astype(bf16).astype(f32)
        v_new = new_row[:, KV_WIDTH + g * HEAD_DIM:KV_WIDTH + (g + 1) * HEAD_DIM].astype(bf16).astype(f32)
        s_new = jnp.sum(qf * k_new, axis=1, keepdims=True)
        m = jnp.maximum(jnp.max(s, axis=1, keepdims=True), s_new)
        e, e_new = jnp.exp(s - m), jnp.exp(s_new - m)
        if mask is not None:
            e = jnp.where(mask, e, 0.0)
        den = jnp.maximum(jnp.sum(e, axis=1, keepdims=True) + e_new, 1e-30)
        num = _dot_nt(e.astype(bf16), v_t.astype(bf16)) + e_new.astype(bf16).astype(f32) * v_new
        return num * (1.0 / den)

    o_s = jnp.zeros((NSA_HEADS, HEAD_DIM), f32)
    o_w = jnp.zeros((NSA_HEADS, HEAD_DIM), f32)
    for g in range(NSA_KV_HEADS):
        mine = jnp.right_shift(head, NSA_GROUP.bit_length() - 1) == g
        pos, member = [], []
        for n in range(N_PICK):
            j = picked(b, g * N_PICK + n)
            pos.append((j // halves) * PAGE_SIZE + lane)
            member.append(jnp.right_shift(lane, SEL_BLOCK.bit_length() - 1) == j % halves)
        pos = jnp.concatenate(pos, axis=1)
        member = jnp.concatenate(member, axis=1)
        k_t = jnp.concatenate([kv_scr[slot, 0, g * N_PICK + n] for n in range(N_PICK)], axis=1)
        v_t = jnp.concatenate([kv_scr[slot, 1, g * N_PICK + n] for n in range(N_PICK)], axis=1)
        s = jnp.where(member, _dot(q, k_t.astype(bf16)) - slope * (t_pos - pos).astype(f32), NEG)
        o_s = jnp.where(mine, attend(s, member, v_t, new_slc, g), o_s)
        s = _dot(q, wc_ref[0, g].astype(bf16)) - slope * d_w
        o_w = jnp.where(mine, attend(s, None, wc_ref[0, NSA_KV_HEADS + g], new_win, g), o_w)

    gate = jnp.broadcast_to(_sigmoid(gate_ref[0]), (NSA_HEADS, LANES))
    glane = lax.broadcasted_iota(jnp.int32, (NSA_HEADS, LANES), 1)
    gcol = [jnp.sum(jnp.where(glane == 3 * head + j, gate, 0.0), axis=1, keepdims=True) for j in range(3)]
    o_ref[0] = gcol[0] * oc_ref[0] + gcol[1] * o_s + gcol[2] * o_w
    eye = lax.broadcasted_iota(jnp.int32, (HEAD_DIM, HEAD_DIM), 0) == \
        lax.broadcasted_iota(jnp.int32, (HEAD_DIM, HEAD_DIM), 1)
    for c in range(N_COMBO):
        piece = jnp.broadcast_to(new_win[:, c * HEAD_DIM:(c + 1) * HEAD_DIM], (HEAD_DIM, HEAD_DIM))
        new_col = jnp.sum(jnp.where(eye, piece, 0.0), axis=1, keepdims=True)
        wout_ref[0, c] = jnp.where(wlane == wl - 1, new_col, pltpu.roll(wc_ref[0, c], wl - 1, axis=1))


def _nsa_sample_sel(page_table, idx_flat, slc_pages_t, q8, o_c, gates, slc_new, win_new, cache_win_t, t_pos):
    nb = q8.shape[0]
    wl = cache_win_t.shape[3]
    n_blk = NSA_KV_HEADS * N_PICK
    per_seq = lambda b, pt, idx: (b, 0, 0)
    win_blk = pl.BlockSpec((1, N_COMBO, HEAD_DIM, wl), lambda b, pt, idx: (b, 0, 0, 0))
    grid_spec = pltpu.PrefetchScalarGridSpec(
        num_scalar_prefetch=2,
        grid=(nb,),
        in_specs=[pl.BlockSpec(memory_space=pl.ANY),
                  pl.BlockSpec((1, NSA_HEADS, HEAD_DIM), per_seq),
                  pl.BlockSpec((1, NSA_HEADS, HEAD_DIM), per_seq),
                  pl.BlockSpec((1, 1, LANES), per_seq),
                  pl.BlockSpec((1, 1, ROW_WIDTH), per_seq),
                  pl.BlockSpec((1, 1, ROW_WIDTH), per_seq),
                  win_blk],
        out_specs=[pl.BlockSpec((1, NSA_HEADS, HEAD_DIM), per_seq), win_blk],
        scratch_shapes=[pltpu.VMEM((2, 2, n_blk, HEAD_DIM, PAGE_SIZE), f32),
                        pltpu.SemaphoreType.DMA((2,))],
    )
    return pl.pallas_call(
        functools.partial(_nsa_sample_sel_kernel, t_pos=t_pos),
        grid_spec=grid_spec,
        out_shape=[jax.ShapeDtypeStruct((nb, NSA_HEADS, HEAD_DIM), f32),
                   jax.ShapeDtypeStruct(cache_win_t.shape, f32)],
        compiler_params=_cparams(("arbitrary",)),
        name="nsa_sample_sel",
    )(page_table, idx_flat, slc_pages_t, q8, o_c, gates, slc_new, win_new, cache_win_t)


def kernel(x_prompt, x_sample, cache_cmp_kv, cache_slc_kv, cache_win_kv, state_hgrn, page_table, p_prompt, p_sample,
           w_in, g_pre, cmp_pe, cmp_w1, cmp_b1, cmp_w2, hg_lower, hg_norm, w_out, g_post, ple_proj, ple_gate):
    bsz, seq, _ = x_prompt.shape
    nb = x_sample.shape[0]
    n_pool = cache_cmp_kv.shape[1]
    n_pages = page_table.shape[1]
    past = n_pages * PAGE_SIZE
    kv_tail = (2, NSA_KV_HEADS, HEAD_DIM)

    o = _ORIG
    w = w_in[0]
    w_r = jnp.concatenate([w[:, o['q_a']:o['k_cmp']], w[:, o['z_a']:o['end']], w[:, o['k_cmp']:o['gate_a']],
                           w[:, o['gate_a']:o['z_a']],
                           jnp.zeros((D_MODEL, PROJ_WIDTH - o['end']), f32)], axis=1).astype(bf16)
    cw = _compress_weights(cmp_pe[0], cmp_w1[0], cmp_b1[0], cmp_w2[0])
    m_mat = _sel_map_matrix()
    w_out_b, ple_proj_b, ple_gate_b = w_out[0].astype(bf16), ple_proj[0].astype(bf16), ple_gate[0].astype(bf16)

    xp = x_prompt.reshape(bsz * seq, D_MODEL)
    zp, cmp_p, slc_p, win_p, gates_p = _proj(xp, g_pre, w_r, 1024)
    pages_p = cmp_p.reshape(bsz * seq // PAGE_SIZE, PAGE_SIZE, ROW_WIDTH)
    table_p = jnp.arange(bsz * seq // PAGE_SIZE, dtype=jnp.int32).reshape(bsz, seq // PAGE_SIZE)
    comp_p = _compress(pages_p, table_p, *cw, feature_major=False)
    q_t, k_slc, vt_slc, k_win, vt_win, cmp_tp, slc_tp, win_tp = _attention_layout(zp, cmp_p, slc_p, win_p, bsz, seq)
    o_a = _nsa_prompt_blocks(q_t, gates_p, comp_p, k_slc, vt_slc, k_win, vt_win, m_mat.T)
    yb_p, hg_p = _hgrn_prompt(zp, hg_lower, hg_norm, bsz, seq)
    h_p = _finish(xp, o_a, zp, yb_p, p_prompt[0].reshape(bsz * seq, PLE_DIM),
                  w_out_b, g_post, ple_proj_b, ple_gate_b, 1024)

    wl_p = min(WINDOW, seq)

    def cache_rows(rows_t):
        return rows_t.reshape((1, bsz) + kv_tail + rows_t.shape[2:]).transpose(0, 1, 5, 2, 3, 4)

    out_prompt = (h_p.reshape(bsz, seq, D_MODEL), cache_rows(cmp_tp), cache_rows(slc_tp),
                  cache_rows(win_tp[:, :, seq - wl_p:]), hg_p[None])

    xs = x_sample.reshape(nb, D_MODEL)
    zs, cmp_s, slc_s, win_new, gates_s = _proj(xs, g_pre, w_r, nb)
    cmp_t = cache_cmp_kv.transpose(0, 1, 3, 4, 5, 2).reshape(n_pool, ROW_WIDTH, PAGE_SIZE)
    comp_s = _compress(cmp_t, page_table, *cw, feature_major=True)
    q8 = zs[:, COL_QA:COL_QA + NSA_WIDTH].astype(bf16).reshape(nb, NSA_HEADS, HEAD_DIM)
    oc_s, ps_s = _nsa_sample_cmp(q8, comp_s, m_mat, past)
    ps_rows = ps_s[:, :NSA_KV_HEADS, :].reshape(nb * NSA_KV_HEADS, N_SELBLK)
    idx = _nsa_sample_topk(ps_rows, past // SEL_BLOCK - 1)
    slc_t = cache_slc_kv.transpose(0, 1, 3, 4, 5, 2).reshape(n_pool, N_COMBO, HEAD_DIM, PAGE_SIZE)
    wl_s = cache_win_kv.shape[2]
    win_t = cache_win_kv.transpose(0, 1, 3, 4, 5, 2).reshape(nb, N_COMBO, HEAD_DIM, wl_s)
    oa_s, win_s = _nsa_sample_sel(page_table, idx[:, :N_SEL].reshape(-1), slc_t, q8, oc_s,
                                  gates_s.reshape(nb, 1, LANES), slc_s.reshape(nb, 1, ROW_WIDTH),
                                  win_new.reshape(nb, 1, ROW_WIDTH), win_t, past)
    win_s = win_s.reshape((1, nb) + kv_tail + (wl_s,)).transpose(0, 1, 5, 2, 3, 4)
    yb_s, hg_s = _hgrn_sample(zs, state_hgrn, hg_lower, hg_norm)
    h_s = _finish(xs, oa_s.reshape(nb, NSA_WIDTH), zs, yb_s, p_sample[0].reshape(nb, PLE_DIM),
                  w_out_b, g_post, ple_proj_b, ple_gate_b, nb)

    out_sample = (h_s.reshape(nb, 1, D_MODEL),
                  cmp_s.reshape((1, nb, 1) + kv_tail),
                  slc_s.reshape((1, nb, 1) + kv_tail),
                  win_s,
                  hg_s)
    return (out_prompt[0], out_sample[0]) + out_prompt[1:] + out_sample[1:]
```

```python
import functools

import numpy as np
import jax
import jax.numpy as jnp
from jax import lax
from jax.experimental import pallas as pl
from jax.experimental.pallas import tpu as pltpu

D_MODEL = 1024
PAGE_SIZE = 128
NSA_HEADS = 8
NSA_KV_HEADS = 2
NSA_GROUP = NSA_HEADS // NSA_KV_HEADS
HEAD_DIM = 64
NSA_WIDTH = NSA_HEADS * HEAD_DIM
KV_WIDTH = NSA_KV_HEADS * HEAD_DIM
CMP_BLOCK = 32
CMP_STRIDE = 16
CMP_RATIO = CMP_BLOCK // CMP_STRIDE
CMP_HIDDEN = 128
SEL_BLOCK = 64
SEL_RATIO = SEL_BLOCK // CMP_STRIDE
N_SEL = 16
WINDOW = 512
Q_BLOCK = 128
HG_HEADS = 4
HG_DIM = 128
HG_WIDTH = HG_HEADS * HG_DIM
PLE_DIM = 256
EPS = 1e-6
NEG = -1e30
FORCE_BONUS = 1e6
REMOVED = -3e38

COL_QA, COL_ZA, COL_QB, COL_FB, COL_IB, COL_ZB = 0, 512, 1024, 1536, 2048, 2560
COL_CMP, COL_SLC, COL_WIN, COL_GATE = 3072, 3328, 3584, 3840
PROJ_WIDTH = 4096
_ORIG = dict(q_a=0, k_cmp=512, gate_a=1280, z_a=1304, q_b=1816, f_b=2328, i_b=2840, z_b=3352, end=3864)

LANES = 128
VMEM_LIMIT = 56 * 1024 * 1024

ROW_WIDTH = 2 * KV_WIDTH
N_COMBO = 2 * NSA_KV_HEADS
SEL_CHUNK = 512
WIN_KEYS = WINDOW + Q_BLOCK
N_CMP_PAD = 512
N_SELBLK = 128

f32 = jnp.float32
bf16 = jnp.bfloat16


def _cparams(sem):
    return pltpu.CompilerParams(dimension_semantics=sem, vmem_limit_bytes=VMEM_LIMIT)


def _dot(a, b):
    return jnp.dot(a, b, preferred_element_type=f32)


def _dot_nt(a, b):
    return lax.dot_general(a, b, (((1,), (1,)), ((), ())), preferred_element_type=f32)


def _dot_tn(a, b):
    return lax.dot_general(a, b, (((0,), (0,)), ((), ())), preferred_element_type=f32)


def _sigmoid(x):
    return 1.0 / (1.0 + jnp.exp(-x))


def _silu(x):
    return x * _sigmoid(x)


PROJ_TILE = 1024
MAIN_TILES = COL_CMP // PROJ_TILE


def _proj_kernel(x_ref, g_ref, w_ref, z_ref, cmp_ref, slc_ref, win_ref, gate_ref):
    x = x_ref[...]
    xn = (x * lax.rsqrt(jnp.mean(x * x, axis=-1, keepdims=True) + EPS) * g_ref[...]).astype(bf16)
    for j in range(MAIN_TILES):
        cols = slice(j * PROJ_TILE, (j + 1) * PROJ_TILE)
        z_ref[:, cols] = _dot(xn, w_ref[:, cols])
    base = COL_CMP
    res = _dot(xn, w_ref[:, base:])
    cmp_ref[...] = res[:, COL_CMP - base:COL_SLC - base]
    slc_ref[...] = res[:, COL_SLC - base:COL_WIN - base]
    win_ref[...] = res[:, COL_WIN - base:COL_GATE - base]
    gate_ref[...] = res[:, COL_GATE - base:COL_GATE - base + LANES]


def _proj(x, g, w, tm):
    n = x.shape[0]
    row = lambda i: (i, 0)
    const = lambda i: (0, 0)
    return pl.pallas_call(
        _proj_kernel,
        grid=(n // tm,),
        in_specs=[pl.BlockSpec((tm, D_MODEL), row),
                  pl.BlockSpec((1, D_MODEL), const),
                  pl.BlockSpec((D_MODEL, PROJ_WIDTH), const, pipeline_mode=pl.Buffered(1))],
        out_specs=[pl.BlockSpec((tm, COL_CMP), row),
                   pl.BlockSpec((tm, ROW_WIDTH), row), pl.BlockSpec((tm, ROW_WIDTH), row),
                   pl.BlockSpec((tm, ROW_WIDTH), row), pl.BlockSpec((tm, LANES), row)],
        out_shape=[jax.ShapeDtypeStruct((n, COL_CMP), f32),
                   jax.ShapeDtypeStruct((n, ROW_WIDTH), f32), jax.ShapeDtypeStruct((n, ROW_WIDTH), f32),
                   jax.ShapeDtypeStruct((n, ROW_WIDTH), f32), jax.ShapeDtypeStruct((n, LANES), f32)],
        compiler_params=_cparams(("parallel",)),
        name="proj",
    )(x, g, w)


def _finish_kernel(x_ref, oa_ref, za_ref, yb_ref, p_ref, wo_ref, gp_ref, pp_ref, pg_ref, h_ref):
    ya = (oa_ref[...] * _silu(za_ref[...])).astype(bf16)
    yb = yb_ref[...].astype(bf16)
    mix = _dot(ya, wo_ref[0:NSA_WIDTH, :]) + _dot(yb, wo_ref[NSA_WIDTH:, :])
    nrm = mix * lax.rsqrt(jnp.mean(mix * mix, axis=-1, keepdims=True) + EPS) * gp_ref[...]
    h = x_ref[...] + nrm
    gate = _sigmoid(_dot(h.astype(bf16), pg_ref[...]))
    h_ref[...] = h + _dot(p_ref[...].astype(bf16), pp_ref[...]) * gate


def _finish(x, oa, z, yb, p, w_out, g_post, ple_proj, ple_gate, tm):
    n = x.shape[0]
    za_blk = COL_ZA // NSA_WIDTH
    row = lambda i: (i, 0)
    const = lambda i: (0, 0)
    return pl.pallas_call(
        _finish_kernel,
        grid=(n // tm,),
        in_specs=[pl.BlockSpec((tm, D_MODEL), row),
                  pl.BlockSpec((tm, NSA_WIDTH), row),
                  pl.BlockSpec((tm, NSA_WIDTH), lambda i: (i, za_blk)),
                  pl.BlockSpec((tm, HG_WIDTH), row),
                  pl.BlockSpec((tm, PLE_DIM), row),
                  pl.BlockSpec((D_MODEL, D_MODEL), const),
                  pl.BlockSpec((1, D_MODEL), const),
                  pl.BlockSpec((PLE_DIM, D_MODEL), const),
                  pl.BlockSpec((D_MODEL, D_MODEL), const)],
        out_specs=pl.BlockSpec((tm, D_MODEL), row),
        out_shape=jax.ShapeDtypeStruct((n, D_MODEL), f32),
        compiler_params=_cparams(("parallel",)),
        name="finish",
    )(x, oa, z, yb, p, w_out, g_post, ple_proj, ple_gate)


SEGS_PER_PAGE = PAGE_SIZE // CMP_STRIDE


def _compress_kernel(pt_ref, pages_ref, wa_ref, wb_ref, pea_ref, peb_ref, b1_ref, w2_ref, out_ref,
                     stage_ref, rows_ref, sem_ref, *, n_pages, feature_major):
    b = pl.program_id(0)
    nb = pl.num_programs(0)
    lane_halves = ROW_WIDTH // LANES
    n_seg = n_pages * SEGS_PER_PAGE

    def page_copy(seq, j):
        s = lax.rem(seq, 2)
        return pltpu.make_async_copy(pages_ref.at[pt_ref[seq, j]], stage_ref.at[s, j], sem_ref.at[s])

    def start_all(seq):
        def body(j, c):
            page_copy(seq, j).start()
            return c
        lax.fori_loop(0, n_pages, body, 0, unroll=8)

    def wait_all(seq):
        def body(j, c):
            page_copy(seq, j).wait()
            return c
        lax.fori_loop(0, n_pages, body, 0, unroll=8)

    def to_rows(s):
        for j in range(n_pages):
            tile = stage_ref[s, j]
            rows = tile.T if feature_major else tile
            for h in range(lane_halves):
                rows_ref[s, h, j * PAGE_SIZE:(j + 1) * PAGE_SIZE, :] = rows[:, h * LANES:(h + 1) * LANES]

    def mlp(s):
        for kind in range(lane_halves):
            cols = slice(kind * LANES, (kind + 1) * LANES)
            seg_a, seg_b = [], []
            for j in range(CMP_STRIDE):
                r = rows_ref[s, kind, pl.ds(j, n_seg, stride=CMP_STRIDE), :]
                seg_a.append((r + pea_ref[j][:, cols]).astype(bf16))
                seg_b.append((r + peb_ref[j][:, cols]).astype(bf16))
            pa = _dot(jnp.concatenate(seg_a, axis=1), wa_ref[kind])
            pb = _dot(jnp.concatenate(seg_b, axis=1), wb_ref[kind])
            pb = pltpu.roll(pb, n_seg - 1, axis=0)
            hid = _silu(pa + pb + b1_ref[kind])
            for g in range(NSA_KV_HEADS):
                hg = hid[:, g * CMP_HIDDEN:(g + 1) * CMP_HIDDEN].astype(bf16)
                out_ref[0, kind * NSA_KV_HEADS + g] = _dot(hg, w2_ref[kind])

    @pl.when(b == 0)
    def _():
        start_all(0)

        @pl.when(nb > 1)
        def _():
            start_all(1)
        wait_all(0)
        to_rows(0)

    @pl.when(b + 2 < nb)
    def _():
        start_all(b + 2)

    @pl.when(b + 1 < nb)
    def _():
        wait_all(b + 1)

    for cur in range(2):
        @pl.when(lax.rem(b, 2) == cur)
        def _(cur=cur):
            to_rows(1 - cur)
            mlp(cur)


def _compress(pages, page_table, wa, wb, pea, peb, b1, w2, feature_major):
    nseq, n_pages = page_table.shape
    n_seg = n_pages * SEGS_PER_PAGE
    const3 = lambda b, pt: (0, 0, 0)
    once = pl.Buffered(1)
    grid_spec = pltpu.PrefetchScalarGridSpec(
        num_scalar_prefetch=1,
        grid=(nseq,),
        in_specs=[pl.BlockSpec(memory_space=pl.ANY),
                  pl.BlockSpec(wa.shape, const3, pipeline_mode=once),
                  pl.BlockSpec(wb.shape, const3, pipeline_mode=once),
                  pl.BlockSpec((CMP_STRIDE, 1, ROW_WIDTH), const3),
                  pl.BlockSpec((CMP_STRIDE, 1, ROW_WIDTH), const3),
                  pl.BlockSpec(b1.shape, const3),
                  pl.BlockSpec((2, CMP_HIDDEN, HEAD_DIM), const3)],
        out_specs=pl.BlockSpec((1, N_COMBO, n_seg, HEAD_DIM), lambda b, pt: (b, 0, 0, 0)),
        scratch_shapes=[pltpu.VMEM((2, n_pages) + pages.shape[1:], f32),
                        pltpu.VMEM((2, ROW_WIDTH // LANES, n_pages * PAGE_SIZE, LANES), f32),
                        pltpu.SemaphoreType.DMA((2,))],
    )
    return pl.pallas_call(
        functools.partial(_compress_kernel, n_pages=n_pages, feature_major=feature_major),
        grid_spec=grid_spec,
        out_shape=jax.ShapeDtypeStruct((nseq, N_COMBO, n_seg, HEAD_DIM), f32),
        compiler_params=_cparams(("arbitrary",)),
        name="compress",
    )(page_table, pages, wa, wb, pea, peb, b1, w2)


def _compress_weights(cmp_pe, cmp_w1, cmp_b1, cmp_w2):
    eye = jnp.eye(NSA_KV_HEADS, dtype=f32)

    def big(w_half):
        t = jnp.einsum('kjdh,gb->kjbdgh', w_half, eye)
        return t.reshape(2, CMP_STRIDE * KV_WIDTH, NSA_KV_HEADS * CMP_HIDDEN)

    w1 = cmp_w1.reshape(2, 2, CMP_STRIDE, HEAD_DIM, CMP_HIDDEN)
    wa = big(w1[:, 0]).astype(bf16)
    wb = big(w1[:, 1]).astype(bf16)
    pe = cmp_pe.reshape(2, 2, CMP_STRIDE, HEAD_DIM)

    def pe_row(p):
        t = jnp.broadcast_to(p.transpose(1, 0, 2)[:, :, None, :],
                             (CMP_STRIDE, 2, NSA_KV_HEADS, HEAD_DIM))
        return t.reshape(CMP_STRIDE, 1, ROW_WIDTH)

    pea, peb = pe_row(pe[:, 0]), pe_row(pe[:, 1])
    b1 = jnp.broadcast_to(cmp_b1[:, None, :], (2, NSA_KV_HEADS, CMP_HIDDEN)).reshape(2, 1, NSA_KV_HEADS * CMP_HIDDEN)
    return wa, wb, pea, peb, b1, cmp_w2.astype(bf16)


def _sel_map_matrix():
    wts = np.convolve(np.ones(SEL_RATIO), np.ones(CMP_RATIO))
    m = np.zeros((N_CMP_PAD, N_SELBLK), np.float32)
    for j in range(N_SELBLK):
        for k, wk in enumerate(wts):
            n = SEL_RATIO * j + k
            if n < N_CMP_PAD - 1:
                m[n, j] = wk
    return jnp.asarray(m, dtype=bf16)


def _importance_to_blocks(imp, m):
    hi = imp.astype(bf16)
    lo = (imp - hi.astype(f32)).astype(bf16)
    return _dot(hi, m) + _dot(lo, m)


LOG2E = 1.4426950408889634
MASK_BIG = 1e30
AUG_ROWS = 16
VT_ROWS = HEAD_DIM + 16
KEY_LOW = 256
N_FORCED = 3


NSA_STEP_BLOCKS = 2


def _nsa_prompt_blocks_kernel(qt_ref, gate_ref, comp_ref, caug_ref, ks_ref, vts_ref, kw_ref, vtw_ref, mt_ref, o_ref,
                              qa_ref, acc_ref, mg_ref, bias_ref, need_ref):
    cols = NSA_GROUP * Q_BLOCK
    n_blocks = qt_ref.shape[2]
    blocks = range(n_blocks)
    heads = range(NSA_KV_HEADS)
    insts = [(qb, g) for qb in blocks for g in heads]
    blk_per_chunk = SEL_CHUNK // SEL_BLOCK
    n_chunk_max = vts_ref.shape[2]

    lane = lax.broadcasted_iota(jnp.int32, (1, cols), 1)
    head = jnp.right_shift(lane, Q_BLOCK.bit_length() - 1)
    blk = lax.broadcasted_iota(jnp.int32, (N_SELBLK, Q_BLOCK), 0)
    qcol = lax.broadcasted_iota(jnp.int32, (N_SELBLK, Q_BLOCK), 1)
    c_end = lax.broadcasted_iota(jnp.int32, (N_CMP_PAD, 1), 0) * CMP_STRIDE + (CMP_BLOCK - 1)
    krow = lax.broadcasted_iota(jnp.int32, (SEL_CHUNK, 1), 0)
    wrow = lax.broadcasted_iota(jnp.int32, (WIN_KEYS, 1), 0)
    arow = lax.broadcasted_iota(jnp.int32, (AUG_ROWS // 2, cols), 0)
    gate_all = _sigmoid(gate_ref[...])

    q0 = [(pl.program_id(1) * n_blocks + qb) * Q_BLOCK for qb in blocks]
    tq = [q0[qb] + jnp.bitwise_and(lane, Q_BLOCK - 1) for qb in blocks]
    last = [q0[qb] // SEL_CHUNK for qb in blocks]
    w0 = [pl.multiple_of(jnp.maximum(q0[qb] - WINDOW, 0), Q_BLOCK) for qb in blocks]
    gate_t = [gate_all[qb * Q_BLOCK:(qb + 1) * Q_BLOCK].T for qb in blocks]

    slope, alibi_rows, pos_rows = [], [], []
    for g in heads:
        c = jnp.exp2(-(head.astype(f32) + 1.0 + NSA_GROUP * g)) * LOG2E
        c_hi = c.astype(bf16).astype(f32)
        c_mid = (c - c_hi).astype(bf16).astype(f32)
        c_lo = (c - c_hi - c_mid).astype(bf16).astype(f32)
        c3 = jnp.where((arow == 0) | (arow == 3), c_hi, jnp.where((arow == 1) | (arow == 4), c_mid, c_lo))
        slope.append(c)
        alibi_rows.append(jnp.where(arow >= 6, 0.0, c3))
        rows = jnp.where(arow >= 6, 0.0, jnp.where(arow >= 3, KEY_LOW * c3, c3))
        pos_rows.append(jnp.concatenate([rows, jnp.zeros_like(rows)], axis=0).astype(bf16))

    def chunk_scores(qb, g, ck, slot=0, live=None):
        k0 = pl.multiple_of(ck * SEL_CHUNK, SEL_CHUNK)
        b8 = bias_ref[qb, g, pl.ds(pl.multiple_of(ck * blk_per_chunk, blk_per_chunk), blk_per_chunk), :]
        if live is not None:
            b8 = jnp.where(live, b8, -MASK_BIG)
        qa_ref[qb, g, slot, HEAD_DIM:HEAD_DIM + AUG_ROWS, :] = jnp.concatenate(
            [jnp.concatenate([b8] * NSA_GROUP, axis=1), alibi_rows[g]], axis=0).astype(bf16)
        return _dot(ks_ref[0, g, pl.ds(k0, SEL_CHUNK), :], qa_ref[qb, g, slot])

    def sel_chunk(qb, g, ck, causal, s):
        k0 = pl.multiple_of(ck * SEL_CHUNK, SEL_CHUNK)
        if causal:
            s = jnp.where(k0 + krow <= tq[qb], s, NEG)
        shift = slope[g] * k0.astype(f32)
        m_old = mg_ref[qb, g] - shift
        m_new = jnp.maximum(m_old, jnp.max(s, axis=0, keepdims=True))
        p = jnp.exp2(s - m_new).astype(bf16)
        acc_ref[qb, g] = acc_ref[qb, g] * jnp.exp2(m_old - m_new) + _dot(vts_ref[0, g, ck], p)
        mg_ref[qb, g] = m_new + shift

    s_cmp, s_win = {}, {}
    for qb, g in insts:
        q_pos = jnp.concatenate([qt_ref[0, g, qb], pos_rows[g],
                                 jnp.zeros((LANES - HEAD_DIM - AUG_ROWS, cols), bf16)], axis=0)
        kc = jnp.concatenate([comp_ref[0, g].astype(bf16), caug_ref[...]], axis=1)
        s_cmp[qb, g] = _dot(kc, q_pos)
        kw = kw_ref[0, g, pl.ds(w0[qb], WIN_KEYS), :]
        s_win[qb, g] = _dot(kw, q_pos)

    oc_t, ps_t = {}, {}
    for qb in blocks:
        seen = c_end <= tq[qb]
        any_seen = tq[qb] >= CMP_BLOCK - 1
        for g in heads:
            s_c = jnp.where(seen, s_cmp[qb, g], NEG)
            e = jnp.exp2(s_c - jnp.max(s_c, axis=0, keepdims=True))
            rl = jnp.where(any_seen, 1.0 / jnp.maximum(jnp.sum(e, axis=0, keepdims=True), 1e-30), 0.0)
            p_c = e * rl
            oc_t[qb, g] = _dot_tn(comp_ref[0, NSA_KV_HEADS + g].astype(bf16), p_c.astype(bf16))
            imp_t = (p_c[:, 0:Q_BLOCK] + p_c[:, Q_BLOCK:2 * Q_BLOCK]
                     + p_c[:, 2 * Q_BLOCK:3 * Q_BLOCK] + p_c[:, 3 * Q_BLOCK:4 * Q_BLOCK])
            hi = imp_t.astype(bf16)
            lo = (imp_t - hi.astype(f32)).astype(bf16)
            ps_t[qb, g] = _dot(mt_ref[...], hi) + _dot(mt_ref[...], lo)

    ow_t = {}
    for qb in blocks:
        wkey = w0[qb] + wrow
        in_window = (wkey <= tq[qb]) & (wkey >= tq[qb] - WINDOW)
        t0 = w0[qb] // Q_BLOCK
        for g in heads:
            s_w = jnp.where(in_window, s_win[qb, g], NEG)
            e = jnp.exp2(s_w - jnp.max(s_w, axis=0, keepdims=True)).astype(bf16)
            acc_w = jnp.zeros((VT_ROWS, cols), f32)
            for t in range(WIN_KEYS // Q_BLOCK):
                acc_w = acc_w + _dot(vtw_ref[0, g, t0 + t], e[t * Q_BLOCK:(t + 1) * Q_BLOCK])
            ow_t[qb, g] = acc_w[0:HEAD_DIM] * (1.0 / jnp.maximum(acc_w[HEAD_DIM:HEAD_DIM + 1], 1e-30))

    valid, sel_t, score = {}, {}, {}
    for qb in blocks:
        jt = jnp.right_shift(q0[qb] + qcol, SEL_BLOCK.bit_length() - 1)
        valid[qb] = blk <= jt
        forced = (blk == 0) | (blk == jt) | (blk == jt - 1)
        for g in heads:
            sel_t[qb, g] = jnp.where(forced, 1.0, 0.0)
            score[qb, g] = jnp.where(valid[qb] & jnp.logical_not(forced), ps_t[qb, g], NEG)
    for _ in range(N_SEL - N_FORCED):
        for inst in insts:
            best = jnp.max(score[inst], axis=0, keepdims=True)
            idx = jnp.min(jnp.where(score[inst] == best, blk, N_SELBLK), axis=0, keepdims=True)
            hit = blk == idx
            sel_t[inst] = jnp.where(hit, 1.0, sel_t[inst])
            score[inst] = jnp.where(hit, REMOVED, score[inst])
    for qb, g in insts:
        sel_t[qb, g] = jnp.where(valid[qb], sel_t[qb, g], 0.0)

    for qb, g in insts:
        bias_ref[qb, g] = (sel_t[qb, g] - 1.0) * MASK_BIG
        for slot in range(qa_ref.shape[2]):
            qa_ref[qb, g, slot, 0:HEAD_DIM, :] = qt_ref[0, g, qb]
            qa_ref[qb, g, slot, HEAD_DIM + AUG_ROWS:, :] = jnp.zeros((LANES - HEAD_DIM - AUG_ROWS, cols), bf16)
        acc_ref[qb, g] = jnp.zeros(acc_ref.shape[2:], f32)
        mg_ref[qb, g] = jnp.full(mg_ref.shape[2:], NEG, f32)
    own = {(qb, g): chunk_scores(qb, g, last[qb]) for qb, g in insts}
    for qb, g in insts:
        sel_chunk(qb, g, last[qb], True, own[qb, g])

    n_need = []
    for qb in blocks:
        count = jnp.int32(0)
        for ck in range(n_chunk_max):
            rows = slice(ck * blk_per_chunk, (ck + 1) * blk_per_chunk)
            picked = jnp.maximum(*[jnp.max(sel_t[qb, g][rows, :]) for g in heads])
            need_ref[qb, count] = ck
            count = count + ((picked > 0.5) & (ck < last[qb])).astype(jnp.int32)
        n_need.append(count)

    for qb in blocks:
        def chunk_pair(i, carry, qb=qb):
            first = 2 * i
            live = first + 1 < n_need[qb]
            cks = [need_ref[qb, first], need_ref[qb, jnp.minimum(first + 1, n_need[qb] - 1)]]
            scores = [[chunk_scores(qb, g, cks[slot], slot, live if slot else None) for g in heads]
                      for slot in range(2)]
            for slot in range(2):
                for g in heads:
                    sel_chunk(qb, g, cks[slot], False, scores[slot][g])
            return carry

        lax.fori_loop(0, (n_need[qb] + 1) // 2, chunk_pair, 0)

    for qb, g in insts:
        acc = acc_ref[qb, g]
        os_t = acc[0:HEAD_DIM] * (1.0 / jnp.maximum(acc[HEAD_DIM:HEAD_DIM + 1], 1e-30))
        outs = []
        for r in range(NSA_GROUP):
            cs = slice(r * Q_BLOCK, (r + 1) * Q_BLOCK)
            row = (g * NSA_GROUP + r) * 3
            outs.append(gate_t[qb][row:row + 1, :] * oc_t[qb, g][:, cs] + gate_t[qb][row + 1:row + 2, :] * os_t[:, cs]
                        + gate_t[qb][row + 2:row + 3, :] * ow_t[qb, g][:, cs])
        for pair in range(NSA_GROUP // 2):
            both = jnp.concatenate([outs[2 * pair], outs[2 * pair + 1]], axis=0)
            lane0 = (g * NSA_GROUP // 2 + pair) * LANES
            o_ref[qb * Q_BLOCK:(qb + 1) * Q_BLOCK, lane0:lane0 + LANES] = both.T


def _nsa_prompt_blocks(q_t, gates, comp, k_slc, vt_slc, k_win, vt_win, m_t):
    bsz, _, nq = q_t.shape[:3]
    seq = k_slc.shape[2]
    cols = NSA_GROUP * Q_BLOCK
    nblk = NSA_STEP_BLOCKS
    steps = nq // nblk
    per_b4 = lambda b, i: (b, 0, 0, 0)
    per_b5 = lambda b, i: (b, 0, 0, 0, 0)
    inst = (nblk, NSA_KV_HEADS)
    return pl.pallas_call(
        _nsa_prompt_blocks_kernel,
        grid=(bsz, steps),
        in_specs=[pl.BlockSpec((1, NSA_KV_HEADS, nblk, HEAD_DIM, cols), lambda b, i: (b, 0, i, 0, 0)),
                  pl.BlockSpec((nblk * Q_BLOCK, LANES), lambda b, i: (b * steps + i, 0)),
                  pl.BlockSpec((1, N_COMBO, N_CMP_PAD, HEAD_DIM), per_b4),
                  pl.BlockSpec((N_CMP_PAD, HEAD_DIM), lambda b, i: (0, 0)),
                  pl.BlockSpec((1, NSA_KV_HEADS, seq, LANES), per_b4),
                  pl.BlockSpec((1,) + vt_slc.shape[1:], per_b5),
                  pl.BlockSpec((1, NSA_KV_HEADS, seq, LANES), per_b4),
                  pl.BlockSpec((1,) + vt_win.shape[1:], per_b5),
                  pl.BlockSpec((N_SELBLK, N_CMP_PAD), lambda b, i: (0, 0))],
        out_specs=pl.BlockSpec((nblk * Q_BLOCK, NSA_WIDTH), lambda b, i: (b * steps + i, 0)),
        out_shape=jax.ShapeDtypeStruct((bsz * seq, NSA_WIDTH), f32),
        scratch_shapes=[pltpu.VMEM(inst + (2, LANES, cols), bf16),
                        pltpu.VMEM(inst + (VT_ROWS, cols), f32),
                        pltpu.VMEM(inst + (1, cols), f32),
                        pltpu.VMEM(inst + (N_SELBLK, Q_BLOCK), f32),
                        pltpu.SMEM((nblk, vt_slc.shape[2]), jnp.int32)],
        compiler_params=_cparams(("parallel", "arbitrary")),
        name="nsa_prompt",
    )(q_t, gates, comp, _cmp_position_columns(), k_slc, vt_slc, k_win, vt_win, m_t)


def _position_columns(pos):
    low, high = (pos % KEY_LOW)[:, None], (pos // KEY_LOW)[:, None]
    return np.concatenate([low, low, low, high, high, high], axis=1).astype(np.float32)


def _cmp_position_columns():
    aug = np.zeros((N_CMP_PAD, HEAD_DIM), np.float32)
    aug[:, 0:6] = _position_columns(np.arange(N_CMP_PAD) * CMP_STRIDE + CMP_BLOCK - 1)
    return jnp.asarray(aug, dtype=bf16)


LAYOUT_ROWS = 1024


def _attention_layout_kernel(q_ref, cmp_ref, slc_ref, win_ref, xs_ref, xw_ref,
                             qt_ref, ks_ref, vts_ref, kw_ref, vtw_ref, cmp_t_ref, slc_t_ref, win_t_ref):
    cmp_t_ref[0] = cmp_ref[...].T
    tm = q_ref.shape[0]
    q = q_ref[...] * (HEAD_DIM ** -0.5 * LOG2E)
    for qb in range(tm // Q_BLOCK):
        rows = slice(qb * Q_BLOCK, (qb + 1) * Q_BLOCK)
        for pair in range(NSA_HEADS // 2):
            t = q[rows, pair * LANES:(pair + 1) * LANES].T
            for k in range(2):
                g, r = divmod(2 * pair + k, NSA_GROUP)
                qt_ref[0, g, qb, :, r * Q_BLOCK:(r + 1) * Q_BLOCK] = t[k * HEAD_DIM:(k + 1) * HEAD_DIM].astype(bf16)

    for src_ref, extra_ref, k_ref, vt_ref, t_ref in ((slc_ref, xs_ref, ks_ref, vts_ref, slc_t_ref),
                                                     (win_ref, xw_ref, kw_ref, vtw_ref, win_t_ref)):
        tile = vt_ref.shape[4]
        s = src_ref[...]
        s_t = s.T
        t_ref[0] = s_t
        v_t = s_t[KV_WIDTH:]
        tail = jnp.where(lax.broadcasted_iota(jnp.int32, (VT_ROWS - HEAD_DIM, tile), 0) == 0, 1.0, 0.0)
        for g in range(NSA_KV_HEADS):
            k_ref[0, g] = jnp.concatenate([s[:, g * HEAD_DIM:(g + 1) * HEAD_DIM].astype(bf16), extra_ref[...]], axis=1)
            for c in range(tm // tile):
                blk = v_t[g * HEAD_DIM:(g + 1) * HEAD_DIM, c * tile:(c + 1) * tile]
                vt_ref[0, g, c] = jnp.concatenate([blk, tail], axis=0).astype(bf16)


def _attention_layout(z, cmp, slc, win, bsz, seq):
    tm = LAYOUT_ROWS
    per_b = seq // tm
    pos = np.arange(seq)
    xs = np.zeros((seq, HEAD_DIM), np.float32)
    xs[pos, (pos // SEL_BLOCK) % (SEL_CHUNK // SEL_BLOCK)] = 1.0
    xs[:, 8:11] = (pos % KEY_LOW)[:, None]
    xs[:, 11:14] = (pos % SEL_CHUNK - pos % KEY_LOW)[:, None]
    xw = np.zeros((seq, HEAD_DIM), np.float32)
    xw[:, 0:6] = _position_columns(pos)
    row = lambda i: (i, 0)
    rep = lambda i: (i % per_b, 0)
    out4 = lambda i: (i // per_b, 0, i % per_b, 0)
    out5 = lambda i: (i // per_b, 0, i % per_b, 0, 0)
    k_shape = jax.ShapeDtypeStruct((bsz, NSA_KV_HEADS, seq, LANES), bf16)
    vt_shape = lambda tile: jax.ShapeDtypeStruct((bsz, NSA_KV_HEADS, seq // tile, VT_ROWS, tile), bf16)
    vt_spec = lambda tile: pl.BlockSpec((1, NSA_KV_HEADS, tm // tile, VT_ROWS, tile), out5)
    rows_t_shape = jax.ShapeDtypeStruct((bsz, ROW_WIDTH, seq), f32)
    rows_t_spec = pl.BlockSpec((1, ROW_WIDTH, tm), lambda i: (i // per_b, 0, i % per_b))
    return pl.pallas_call(
        _attention_layout_kernel,
        grid=(bsz * per_b,),
        in_specs=[pl.BlockSpec((tm, NSA_WIDTH), row), pl.BlockSpec((tm, ROW_WIDTH), row),
                  pl.BlockSpec((tm, ROW_WIDTH), row), pl.BlockSpec((tm, ROW_WIDTH), row),
                  pl.BlockSpec((tm, HEAD_DIM), rep), pl.BlockSpec((tm, HEAD_DIM), rep)],
        out_specs=[pl.BlockSpec((1, NSA_KV_HEADS, tm // Q_BLOCK, HEAD_DIM, NSA_GROUP * Q_BLOCK), out5),
                   pl.BlockSpec((1, NSA_KV_HEADS, tm, LANES), out4), vt_spec(SEL_CHUNK),
                   pl.BlockSpec((1, NSA_KV_HEADS, tm, LANES), out4), vt_spec(Q_BLOCK),
                   rows_t_spec, rows_t_spec, rows_t_spec],
        out_shape=[jax.ShapeDtypeStruct((bsz, NSA_KV_HEADS, seq // Q_BLOCK, HEAD_DIM, NSA_GROUP * Q_BLOCK), bf16),
                   k_shape, vt_shape(SEL_CHUNK), k_shape, vt_shape(Q_BLOCK),
                   rows_t_shape, rows_t_shape, rows_t_shape],
        compiler_params=_cparams(("parallel",)),
        name="attention_layout",
    )(z, cmp, slc, win, jnp.asarray(xs, dtype=bf16), jnp.asarray(xw, dtype=bf16))


HG_CHUNK = 128
HG_STEP_CHUNKS = 4
HG_BAND = 8


def _lower_bound(lo):
    m = jnp.max(lo, axis=0, keepdims=True)
    e = jnp.exp(lo - m)
    return e[0:1] / jnp.sum(e, axis=0, keepdims=True)


def _hgrn_gates(q, fl, lb):
    sg = _sigmoid(fl)
    f = lb + (1.0 - lb) * sg
    k = (1.0 - lb) * (1.0 - sg)
    return _silu(q), k, jnp.log(f)


def _hgrn_out(o, nw, zb):
    o = o * lax.rsqrt(jnp.mean(o * o, axis=-1, keepdims=True) + EPS) * nw
    return o * _silu(zb)


def _hgrn_prompt_kernel(q_ref, f_ref, i_ref, zb_ref, lo_ref, nw_ref, y_ref, s_ref, st_ref):
    c = pl.program_id(1)

    @pl.when(c == 0)
    def _():
        st_ref[...] = jnp.zeros_like(st_ref)

    n = HG_CHUNK
    row = lax.broadcasted_iota(jnp.int32, (n, n), 0)
    col = lax.broadcasted_iota(jnp.int32, (n, n), 1)
    heads = [slice(h * HG_DIM, (h + 1) * HG_DIM) for h in range(HG_HEADS)]

    qq, kk, logf = _hgrn_gates(q_ref[...], f_ref[...], _lower_bound(lo_ref[...]))
    v = i_ref[...]
    chunks = [slice(i * n, (i + 1) * n) for i in range(q_ref.shape[0] // n)]
    tril = jnp.where(col <= row, 1.0, 0.0).astype(bf16)
    hi = logf.astype(bf16)
    r1 = logf - hi.astype(f32)
    mid = r1.astype(bf16)
    lo = (r1 - mid.astype(f32)).astype(bf16)
    bs = [_dot(tril, hi[rs]) + _dot(tril, mid[rs]) + _dot(tril, lo[rs]) for rs in chunks]

    intra = [[_hgrn_intra(qq[rs, hs], kk[rs, hs], b[:, hs], row, col) for hs in heads] for rs, b in zip(chunks, bs)]
    for i, rs in enumerate(chunks):
        for h, hs in enumerate(heads):
            st = st_ref[h]
            bh, kh, vh = bs[i][:, hs], kk[rs, hs], v[rs, hs].astype(bf16)
            o = _dot(intra[i][h].astype(bf16), vh) + _dot_nt((qq[rs, hs] * jnp.exp(bh)).astype(bf16), st.astype(bf16))
            b_last = bh[n - 1:n]
            st_new = st * jnp.exp(b_last) + _dot_tn(vh, (kh * jnp.exp(b_last - bh)).astype(bf16))
            st_ref[h] = st_new
            y_ref[rs, hs] = _hgrn_out(o, nw_ref[...], zb_ref[rs, hs])

    @pl.when(c == pl.num_programs(1) - 1)
    def _():
        for h in range(HG_HEADS):
            s_ref[0, h] = st_ref[h].T


def _hgrn_intra(qq, kk, b, row, col):
    n = HG_CHUNK
    a = jnp.where(row == col, _dot_nt(qq.astype(bf16), kk.astype(bf16)), 0.0)

    sub = jnp.bitwise_and(row, HG_BAND - 1)
    b3 = b.reshape(n // HG_BAND, HG_BAND, HG_DIM)
    size = 2
    while size <= HG_BAND:
        half = size // 2
        beta = None
        for s0 in range(0, HG_BAND, size):
            piece = jnp.broadcast_to(b3[:, s0 + half - 1:s0 + half, :], b3.shape).reshape(n, HG_DIM)
            beta = piece if beta is None else jnp.where(sub >= s0, piece, beta)
        left = jnp.bitwise_and(sub, size - 1) < half
        decay = jnp.exp(jnp.where(left, beta - b, b - beta))
        part = _dot_nt(jnp.where(left, 0.0, qq * decay).astype(bf16), jnp.where(left, kk * decay, 0.0).astype(bf16))
        sh = size.bit_length() - 1
        a = a + jnp.where(jnp.right_shift(row, sh) == jnp.right_shift(col, sh), part, 0.0)
        size *= 2

    while size <= n:
        half = size // 2
        qs, ks = [], []
        zero = jnp.zeros((half, HG_DIM), f32)
        for r0 in range(0, n, size):
            beta = b[r0 + half - 1:r0 + half]
            ks += [kk[r0:r0 + half] * jnp.exp(beta - b[r0:r0 + half]), zero]
            qs += [zero, qq[r0 + half:r0 + size] * jnp.exp(b[r0 + half:r0 + size] - beta)]
        part = _dot_nt(jnp.concatenate(qs, axis=0).astype(bf16), jnp.concatenate(ks, axis=0).astype(bf16))
        if size < n:
            sh = size.bit_length() - 1
            part = jnp.where(jnp.right_shift(row, sh) == jnp.right_shift(col, sh), part, 0.0)
        a = a + part
        size *= 2
    return a


def _hgrn_prompt(z, hg_lower, hg_norm, bsz, seq):
    rows = HG_STEP_CHUNKS * HG_CHUNK
    nch = seq // rows

    def zcol(base):
        return pl.BlockSpec((rows, HG_WIDTH), lambda b, c: (b * nch + c, base // HG_WIDTH))

    return pl.pallas_call(
        _hgrn_prompt_kernel,
        grid=(bsz, nch),
        in_specs=[zcol(COL_QB), zcol(COL_FB), zcol(COL_IB), zcol(COL_ZB),
                  pl.BlockSpec(hg_lower.shape, lambda b, c: (0, 0)),
                  pl.BlockSpec((1, HG_DIM), lambda b, c: (0, 0))],
        out_specs=[pl.BlockSpec((rows, HG_WIDTH), lambda b, c: (b * nch + c, 0)),
                   pl.BlockSpec((1, HG_HEADS, HG_DIM, HG_DIM), lambda b, c: (b, 0, 0, 0))],
        out_shape=[jax.ShapeDtypeStruct((bsz * seq, HG_WIDTH), f32),
                   jax.ShapeDtypeStruct((bsz, HG_HEADS, HG_DIM, HG_DIM), f32)],
        scratch_shapes=[pltpu.VMEM((HG_HEADS, HG_DIM, HG_DIM), f32)],
        compiler_params=_cparams(("parallel", "arbitrary")),
        name="hgrn_prompt",
    )(z, z, z, z, hg_lower, hg_norm)


def _hgrn_sample_kernel(q_ref, f_ref, i_ref, zb_ref, lo_ref, nw_ref, s_ref, y_ref, so_ref, o_scr):
    nb = q_ref.shape[0]
    lb = _lower_bound(lo_ref[...])
    qq, kk, logf = _hgrn_gates(q_ref[...], f_ref[...], lb)
    v = i_ref[...]
    pad = jnp.zeros((HG_DIM - nb, HG_DIM), f32)
    to_cols = lambda x: jnp.concatenate([x, pad], axis=0).T
    f_t, k_t, q_t = to_cols(jnp.exp(logf)), to_cols(kk), to_cols(qq)
    for b in range(nb):
        s_new = f_t[:, b:b + 1] * s_ref[0, b, 0] + k_t[:, b:b + 1] * v[b:b + 1, :]
        so_ref[0, b, 0] = s_new
        o_scr[b:b + 1, :] = jnp.sum(s_new * q_t[:, b:b + 1], axis=0, keepdims=True)
    y_ref[...] = _hgrn_out(o_scr[...], nw_ref[...], zb_ref[...])


def _hgrn_sample(z, state, hg_lower, hg_norm):
    nb = z.shape[0]

    def zcol(base):
        return pl.BlockSpec((nb, HG_DIM), lambda h: (0, base // HG_DIM + h))

    sblk = pl.BlockSpec((1, nb, 1, HG_DIM, HG_DIM), lambda h: (0, 0, h, 0, 0))
    return pl.pallas_call(
        _hgrn_sample_kernel,
        grid=(HG_HEADS,),
        in_specs=[zcol(COL_QB), zcol(COL_FB), zcol(COL_IB), zcol(COL_ZB),
                  pl.BlockSpec((hg_lower.shape[0], HG_DIM), lambda h: (0, h)),
                  pl.BlockSpec((1, HG_DIM), lambda h: (0, 0)),
                  sblk],
        out_specs=[pl.BlockSpec((nb, HG_DIM), lambda h: (0, h)), sblk],
        out_shape=[jax.ShapeDtypeStruct((nb, HG_WIDTH), f32),
                   jax.ShapeDtypeStruct(state.shape, f32)],
        scratch_shapes=[pltpu.VMEM((nb, HG_DIM), f32)],
        compiler_params=_cparams(("parallel",)),
        name="hgrn_sample",
    )(z, z, z, z, hg_lower, hg_norm, state)


N_PICK = N_SEL - 1


def _head_slopes():
    h = lax.broadcasted_iota(jnp.int32, (NSA_HEADS, 1), 0).astype(f32)
    return jnp.exp2(-(h + 1.0))


SAMPLE_CMP_SEQS = 8


def _nsa_sample_cmp_kernel(q_ref, comp_ref, m_ref, oc_ref, ps_ref, *, t_pos):
    slope = _head_slopes()
    head = lax.broadcasted_iota(jnp.int32, (NSA_HEADS, 1), 0)
    ncol = lax.broadcasted_iota(jnp.int32, (1, N_CMP_PAD), 1)
    d_c = t_pos - (ncol * CMP_STRIDE + (CMP_BLOCK - 1))
    mask = d_c >= 0
    ps_ref[...] = jnp.zeros_like(ps_ref)
    for b in range(q_ref.shape[0]):
        q = q_ref[b] * jnp.asarray(HEAD_DIM ** -0.5, bf16)
        o_c = jnp.zeros((NSA_HEADS, HEAD_DIM), f32)
        for g in range(NSA_KV_HEADS):
            kc = comp_ref[b, g].astype(bf16)
            vc = comp_ref[b, NSA_KV_HEADS + g].astype(bf16)
            s = jnp.where(mask, _dot_nt(q, kc) - slope * d_c.astype(f32), NEG)
            m = jnp.max(s, axis=1, keepdims=True)
            e = jnp.where(mask, jnp.exp(s - m), 0.0)
            p = e * (1.0 / jnp.maximum(jnp.sum(e, axis=1, keepdims=True), 1e-30))
            mine = jnp.right_shift(head, NSA_GROUP.bit_length() - 1) == g
            o_c = jnp.where(mine, _dot(p.astype(bf16), vc), o_c)
            imp = jnp.sum(jnp.where(mine, p, 0.0), axis=0, keepdims=True)
            ps_ref[b, g:g + 1, :] = _importance_to_blocks(imp, m_ref[...])
        oc_ref[b] = o_c


def _nsa_sample_cmp(q8, comp, m_mat, t_pos):
    nb = q8.shape[0]
    ns = SAMPLE_CMP_SEQS
    return pl.pallas_call(
        functools.partial(_nsa_sample_cmp_kernel, t_pos=t_pos),
        grid=(nb // ns,),
        in_specs=[pl.BlockSpec((ns, NSA_HEADS, HEAD_DIM), lambda b: (b, 0, 0)),
                  pl.BlockSpec((ns, N_COMBO, N_CMP_PAD, HEAD_DIM), lambda b: (b, 0, 0, 0)),
                  pl.BlockSpec((N_CMP_PAD, N_SELBLK), lambda b: (0, 0))],
        out_specs=[pl.BlockSpec((ns, NSA_HEADS, HEAD_DIM), lambda b: (b, 0, 0)),
                   pl.BlockSpec((ns, 8, N_SELBLK), lambda b: (b, 0, 0))],
        out_shape=[jax.ShapeDtypeStruct((nb, NSA_HEADS, HEAD_DIM), f32),
                   jax.ShapeDtypeStruct((nb, 8, N_SELBLK), f32)],
        compiler_params=_cparams(("parallel",)),
        name="nsa_sample_cmp",
    )(q8, comp, m_mat)


def _nsa_sample_topk_kernel(ps_ref, idx_ref, *, last_blk):
    ps = ps_ref[...]
    lane = lax.broadcasted_iota(jnp.int32, ps.shape, 1)
    score = ps + jnp.where((lane == 0) | (lane == last_blk), FORCE_BONUS, 0.0)
    out = jnp.zeros(ps.shape, jnp.int32)
    for r in range(N_PICK):
        best = jnp.max(score, axis=1, keepdims=True)
        idx = jnp.min(jnp.where(score == best, lane, N_SELBLK), axis=1, keepdims=True)
        out = jnp.where(lane == r, idx, out)
        score = jnp.where(lane == idx, REMOVED, score)
    idx_ref[...] = out


def _nsa_sample_topk(ps_rows, last_blk):
    return pl.pallas_call(
        functools.partial(_nsa_sample_topk_kernel, last_blk=last_blk),
        out_shape=jax.ShapeDtypeStruct(ps_rows.shape, jnp.int32),
        name="nsa_sample_topk",
    )(ps_rows)


def _nsa_sample_sel_kernel(pt_ref, idx_ref, slc_ref, q_ref, oc_ref, gate_ref, slc_new_ref, win_new_ref, wc_ref,
                           o_ref, wout_ref, kv_scr, sem_ref, *, t_pos):
    n_blk = NSA_KV_HEADS * N_PICK
    halves = PAGE_SIZE // SEL_BLOCK
    b = pl.program_id(0)
    nb = pl.num_programs(0)
    slot = lax.rem(b, 2)

    def picked(seq, k):
        return idx_ref[(seq * NSA_KV_HEADS + k // N_PICK) * N_SEL + k % N_PICK]

    def blk_copies(seq, s, k):
        page = pt_ref[seq, picked(seq, k) // halves]
        g = k // N_PICK
        return [pltpu.make_async_copy(slc_ref.at[page, kind * NSA_KV_HEADS + g], kv_scr.at[s, kind, k],
                                      sem_ref.at[s]) for kind in range(2)]

    def start_all(seq, s):
        for k in range(n_blk):
            for cp in blk_copies(seq, s, k):
                cp.start()

    @pl.when(b == 0)
    def _():
        start_all(0, 0)

    @pl.when(b + 1 < nb)
    def _():
        start_all(b + 1, 1 - slot)

    for k in range(n_blk):
        for cp in blk_copies(b, slot, k):
            cp.wait()

    q = q_ref[0] * jnp.asarray(HEAD_DIM ** -0.5, bf16)
    slope = _head_slopes()
    head = lax.broadcasted_iota(jnp.int32, (NSA_HEADS, 1), 0)
    lane = lax.broadcasted_iota(jnp.int32, (1, PAGE_SIZE), 1)
    new_slc = slc_new_ref[0]
    new_win = win_new_ref[0]
    wl = wc_ref.shape[3]
    wlane = lax.broadcasted_iota(jnp.int32, (1, wl), 1)
    d_w = (wl - wlane).astype(f32)
    qf = q.astype(f32)

    def attend(s, mask, v_t, new_row, g):
        k_new = new_row[:, g * HEAD_DIM:(g + 1) * HEAD_DIM].astype(bf16).astype(f32)
        v_new = new_row[:, KV_WIDTH + g * HEAD_DIM:KV_WIDTH + (g + 1) * HEAD_DIM].astype(bf16).astype(f32)
        s_new = jnp.sum(qf * k_new, axis=1, keepdims=True)
        m = jnp.maximum(jnp.max(s, axis=1, keepdims=True), s_new)
        e, e_new = jnp.exp(s - m), jnp.exp(s_new - m)
        if mask is not None:
            e = jnp.where(mask, e, 0.0)
        den = jnp.maximum(jnp.sum(e, axis=1, keepdims=True) + e_new, 1e-30)
        num = _dot_nt(e.astype(bf16), v_t.astype(bf16)) + e_new.astype(bf16).astype(f32) * v_new
        return num * (1.0 / den)

    o_s = jnp.zeros((NSA_HEADS, HEAD_DIM), f32)
    o_w = jnp.zeros((NSA_HEADS, HEAD_DIM), f32)
    for g in range(NSA_KV_HEADS):
        mine = jnp.right_shift(head, NSA_GROUP.bit_length() - 1) == g
        pos, member = [], []
        for n in range(N_PICK):
            j = picked(b, g * N_PICK + n)
            pos.append((j // halves) * PAGE_SIZE + lane)
            member.append(jnp.right_shift(lane, SEL_BLOCK.bit_length() - 1) == j % halves)
        pos = jnp.concatenate(pos, axis=1)
        member = jnp.concatenate(member, axis=1)
        k_t = jnp.concatenate([kv_scr[slot, 0, g * N_PICK + n] for n in range(N_PICK)], axis=1)
        v_t = jnp.concatenate([kv_scr[slot, 1, g * N_PICK + n] for n in range(N_PICK)], axis=1)
        s = jnp.where(member, _dot(q, k_t.astype(bf16)) - slope * (t_pos - pos).astype(f32), NEG)
        o_s = jnp.where(mine, attend(s, member, v_t, new_slc, g), o_s)
        s = _dot(q, wc_ref[0, g].astype(bf16)) - slope * d_w
        o_w = jnp.where(mine, attend(s, None, wc_ref[0, NSA_KV_HEADS + g], new_win, g), o_w)

    gate = jnp.broadcast_to(_sigmoid(gate_ref[0]), (NSA_HEADS, LANES))
    glane = lax.broadcasted_iota(jnp.int32, (NSA_HEADS, LANES), 1)
    gcol = [jnp.sum(jnp.where(glane == 3 * head + j, gate, 0.0), axis=1, keepdims=True) for j in range(3)]
    o_ref[0] = gcol[0] * oc_ref[0] + gcol[1] * o_s + gcol[2] * o_w
    eye = lax.broadcasted_iota(jnp.int32, (HEAD_DIM, HEAD_DIM), 0) == \
        lax.broadcasted_iota(jnp.int32, (HEAD_DIM, HEAD_DIM), 1)
    for c in range(N_COMBO):
        piece = jnp.broadcast_to(new_win[:, c * HEAD_DIM:(c + 1) * HEAD_DIM], (HEAD_DIM, HEAD_DIM))
        new_col = jnp.sum(jnp.where(eye, piece, 0.0), axis=1, keepdims=True)
        wout_ref[0, c] = jnp.where(wlane == wl - 1, new_col, pltpu.roll(wc_ref[0, c], wl - 1, axis=1))


def _nsa_sample_sel(page_table, idx_flat, slc_pages_t, q8, o_c, gates, slc_new, win_new, cache_win_t, t_pos):
    nb = q8.shape[0]
    wl = cache_win_t.shape[3]
    n_blk = NSA_KV_HEADS * N_PICK
    per_seq = lambda b, pt, idx: (b, 0, 0)
    win_blk = pl.BlockSpec((1, N_COMBO, HEAD_DIM, wl), lambda b, pt, idx: (b, 0, 0, 0))
    grid_spec = pltpu.PrefetchScalarGridSpec(
        num_scalar_prefetch=2,
        grid=(nb,),
        in_specs=[pl.BlockSpec(memory_space=pl.ANY),
                  pl.BlockSpec((1, NSA_HEADS, HEAD_DIM), per_seq),
                  pl.BlockSpec((1, NSA_HEADS, HEAD_DIM), per_seq),
                  pl.BlockSpec((1, 1, LANES), per_seq),
                  pl.BlockSpec((1, 1, ROW_WIDTH), per_seq),
                  pl.BlockSpec((1, 1, ROW_WIDTH), per_seq),
                  win_blk],
        out_specs=[pl.BlockSpec((1, NSA_HEADS, HEAD_DIM), per_seq), win_blk],
        scratch_shapes=[pltpu.VMEM((2, 2, n_blk, HEAD_DIM, PAGE_SIZE), f32),
                        pltpu.SemaphoreType.DMA((2,))],
    )
    return pl.pallas_call(
        functools.partial(_nsa_sample_sel_kernel, t_pos=t_pos),
        grid_spec=grid_spec,
        out_shape=[jax.ShapeDtypeStruct((nb, NSA_HEADS, HEAD_DIM), f32),
                   jax.ShapeDtypeStruct(cache_win_t.shape, f32)],
        compiler_params=_cparams(("arbitrary",)),
        name="nsa_sample_sel",
    )(page_table, idx_flat, slc_pages_t, q8, o_c, gates, slc_new, win_new, cache_win_t)


def kernel(x_prompt, x_sample, cache_cmp_kv, cache_slc_kv, cache_win_kv, state_hgrn, page_table, p_prompt, p_sample,
           w_in, g_pre, cmp_pe, cmp_w1, cmp_b1, cmp_w2, hg_lower, hg_norm, w_out, g_post, ple_proj, ple_gate):
    bsz, seq, _ = x_prompt.shape
    nb = x_sample.shape[0]
    n_pool = cache_cmp_kv.shape[1]
    n_pages = page_table.shape[1]
    past = n_pages * PAGE_SIZE
    kv_tail = (2, NSA_KV_HEADS, HEAD_DIM)

    o = _ORIG
    w = w_in[0]
    w_r = jnp.concatenate([w[:, o['q_a']:o['k_cmp']], w[:, o['z_a']:o['end']], w[:, o['k_cmp']:o['gate_a']],
                           w[:, o['gate_a']:o['z_a']],
                           jnp.zeros((D_MODEL, PROJ_WIDTH - o['end']), f32)], axis=1).astype(bf16)
    cw = _compress_weights(cmp_pe[0], cmp_w1[0], cmp_b1[0], cmp_w2[0])
    m_mat = _sel_map_matrix()
    w_out_b, ple_proj_b, ple_gate_b = w_out[0].astype(bf16), ple_proj[0].astype(bf16), ple_gate[0].astype(bf16)

    xp = x_prompt.reshape(bsz * seq, D_MODEL)
    zp, cmp_p, slc_p, win_p, gates_p = _proj(xp, g_pre, w_r, 1024)
    pages_p = cmp_p.reshape(bsz * seq // PAGE_SIZE, PAGE_SIZE, ROW_WIDTH)
    table_p = jnp.arange(bsz * seq // PAGE_SIZE, dtype=jnp.int32).reshape(bsz, seq // PAGE_SIZE)
    comp_p = _compress(pages_p, table_p, *cw, feature_major=False)
    q_t, k_slc, vt_slc, k_win, vt_win, cmp_tp, slc_tp, win_tp = _attention_layout(zp, cmp_p, slc_p, win_p, bsz, seq)
    o_a = _nsa_prompt_blocks(q_t, gates_p, comp_p, k_slc, vt_slc, k_win, vt_win, m_mat.T)
    yb_p, hg_p = _hgrn_prompt(zp, hg_lower, hg_norm, bsz, seq)
    h_p = _finish(xp, o_a, zp, yb_p, p_prompt[0].reshape(bsz * seq, PLE_DIM),
                  w_out_b, g_post, ple_proj_b, ple_gate_b, 1024)

    wl_p = min(WINDOW, seq)

    def cache_rows(rows_t):
        return rows_t.reshape((1, bsz) + kv_tail + rows_t.shape[2:]).transpose(0, 1, 5, 2, 3, 4)

    out_prompt = (h_p.reshape(bsz, seq, D_MODEL), cache_rows(cmp_tp), cache_rows(slc_tp),
                  cache_rows(win_tp[:, :, seq - wl_p:]), hg_p[None])

    xs = x_sample.reshape(nb, D_MODEL)
    zs, cmp_s, slc_s, win_new, gates_s = _proj(xs, g_pre, w_r, nb)
    cmp_t = cache_cmp_kv.transpose(0, 1, 3, 4, 5, 2).reshape(n_pool, ROW_WIDTH, PAGE_SIZE)
    comp_s = _compress(cmp_t, page_table, *cw, feature_major=True)
    q8 = zs[:, COL_QA:COL_QA + NSA_WIDTH].astype(bf16).reshape(nb, NSA_HEADS, HEAD_DIM)
    oc_s, ps_s = _nsa_sample_cmp(q8, comp_s, m_mat, past)
    ps_rows = ps_s[:, :NSA_KV_HEADS, :].reshape(nb * NSA_KV_HEADS, N_SELBLK)
    idx = _nsa_sample_topk(ps_rows, past // SEL_BLOCK - 1)
    slc_t = cache_slc_kv.transpose(0, 1, 3, 4, 5, 2).reshape(n_pool, N_COMBO, HEAD_DIM, PAGE_SIZE)
    wl_s = cache_win_kv.shape[2]
    win_t = cache_win_kv.transpose(0, 1, 3, 4, 5, 2).reshape(nb, N_COMBO, HEAD_DIM, wl_s)
    oa_s, win_s = _nsa_sample_sel(page_table, idx[:, :N_SEL].reshape(-1), slc_t, q8, oc_s,
                                  gates_s.reshape(nb, 1, LANES), slc_s.reshape(nb, 1, ROW_WIDTH),
                                  win_new.reshape(nb, 1, ROW_WIDTH), win_t, past)
    win_s = win_s.reshape((1, nb) + kv_tail + (wl_s,)).transpose(0, 1, 5, 2, 3, 4)
    yb_s, hg_s = _hgrn_sample(zs, state_hgrn, hg_lower, hg_norm)
    h_s = _finish(xs, oa_s.reshape(nb, NSA_WIDTH), zs, yb_s, p_sample[0].reshape(nb, PLE_DIM),
                  w_out_b, g_post, ple_proj_b, ple_gate_b, nb)

    out_sample = (h_s.reshape(nb, 1, D_MODEL),
                  cmp_s.reshape((1, nb, 1) + kv_tail),
                  slc_s.reshape((1, nb, 1) + kv_tail),
                  win_s,
                  hg_s)
    return (out_prompt[0], out_sample[0]) + out_prompt[1:] + out_sample[1:]
```

```python
import functools

import numpy as np
import jax
import jax.numpy as jnp
from jax import lax
from jax.experimental import pallas as pl
from jax.experimental.pallas import tpu as pltpu

D_MODEL = 1024
PAGE_SIZE = 128
NSA_HEADS = 8
NSA_KV_HEADS = 2
NSA_GROUP = NSA_HEADS // NSA_KV_HEADS
HEAD_DIM = 64
NSA_WIDTH = NSA_HEADS * HEAD_DIM
KV_WIDTH = NSA_KV_HEADS * HEAD_DIM
CMP_BLOCK = 32
CMP_STRIDE = 16
CMP_RATIO = CMP_BLOCK // CMP_STRIDE
CMP_HIDDEN = 128
SEL_BLOCK = 64
SEL_RATIO = SEL_BLOCK // CMP_STRIDE
N_SEL = 16
WINDOW = 512
Q_BLOCK = 128
HG_HEADS = 4
HG_DIM = 128
HG_WIDTH = HG_HEADS * HG_DIM
PLE_DIM = 256
EPS = 1e-6
NEG = -1e30
FORCE_BONUS = 1e6
REMOVED = -3e38

COL_QA, COL_ZA, COL_QB, COL_FB, COL_IB, COL_ZB = 0, 512, 1024, 1536, 2048, 2560
COL_CMP, COL_SLC, COL_WIN, COL_GATE = 3072, 3328, 3584, 3840
PROJ_WIDTH = 4096
_ORIG = dict(q_a=0, k_cmp=512, gate_a=1280, z_a=1304, q_b=1816, f_b=2328, i_b=2840, z_b=3352, end=3864)

LANES = 128
VMEM_LIMIT = 56 * 1024 * 1024

ROW_WIDTH = 2 * KV_WIDTH
N_COMBO = 2 * NSA_KV_HEADS
SEL_CHUNK = 512
WIN_KEYS = WINDOW + Q_BLOCK
N_CMP_PAD = 512
N_SELBLK = 128

f32 = jnp.float32
bf16 = jnp.bfloat16


def _cparams(sem):
    return pltpu.CompilerParams(dimension_semantics=sem, vmem_limit_bytes=VMEM_LIMIT)


def _dot(a, b):
    return jnp.dot(a, b, preferred_element_type=f32)


def _dot_nt(a, b):
    return lax.dot_general(a, b, (((1,), (1,)), ((), ())), preferred_element_type=f32)


def _dot_tn(a, b):
    return lax.dot_general(a, b, (((0,), (0,)), ((), ())), preferred_element_type=f32)


def _sigmoid(x):
    return 1.0 / (1.0 + jnp.exp(-x))


def _silu(x):
    return x * _sigmoid(x)


PROJ_TILE = 1024
MAIN_TILES = COL_CMP // PROJ_TILE


def _proj_kernel(x_ref, g_ref, w_ref, z_ref, cmp_ref, slc_ref, win_ref, gate_ref):
    x = x_ref[...]
    xn = (x * lax.rsqrt(jnp.mean(x * x, axis=-1, keepdims=True) + EPS) * g_ref[...]).astype(bf16)
    for j in range(MAIN_TILES):
        cols = slice(j * PROJ_TILE, (j + 1) * PROJ_TILE)
        z_ref[:, cols] = _dot(xn, w_ref[:, cols])
    base = COL_CMP
    res = _dot(xn, w_ref[:, base:])
    cmp_ref[...] = res[:, COL_CMP - base:COL_SLC - base]
    slc_ref[...] = res[:, COL_SLC - base:COL_WIN - base]
    win_ref[...] = res[:, COL_WIN - base:COL_GATE - base]
    gate_ref[...] = res[:, COL_GATE - base:COL_GATE - base + LANES]


def _proj(x, g, w, tm):
    n = x.shape[0]
    row = lambda i: (i, 0)
    const = lambda i: (0, 0)
    return pl.pallas_call(
        _proj_kernel,
        grid=(n // tm,),
        in_specs=[pl.BlockSpec((tm, D_MODEL), row),
                  pl.BlockSpec((1, D_MODEL), const),
                  pl.BlockSpec((D_MODEL, PROJ_WIDTH), const, pipeline_mode=pl.Buffered(1))],
        out_specs=[pl.BlockSpec((tm, COL_CMP), row),
                   pl.BlockSpec((tm, ROW_WIDTH), row), pl.BlockSpec((tm, ROW_WIDTH), row),
                   pl.BlockSpec((tm, ROW_WIDTH), row), pl.BlockSpec((tm, LANES), row)],
        out_shape=[jax.ShapeDtypeStruct((n, COL_CMP), f32),
                   jax.ShapeDtypeStruct((n, ROW_WIDTH), f32), jax.ShapeDtypeStruct((n, ROW_WIDTH), f32),
                   jax.ShapeDtypeStruct((n, ROW_WIDTH), f32), jax.ShapeDtypeStruct((n, LANES), f32)],
        compiler_params=_cparams(("parallel",)),
        name="proj",
    )(x, g, w)


def _finish_kernel(x_ref, oa_ref, za_ref, yb_ref, p_ref, wo_ref, gp_ref, pp_ref, pg_ref, h_ref):
    ya = (oa_ref[...] * _silu(za_ref[...])).astype(bf16)
    yb = yb_ref[...].astype(bf16)
    mix = _dot(ya, wo_ref[0:NSA_WIDTH, :]) + _dot(yb, wo_ref[NSA_WIDTH:, :])
    nrm = mix * lax.rsqrt(jnp.mean(mix * mix, axis=-1, keepdims=True) + EPS) * gp_ref[...]
    h = x_ref[...] + nrm
    gate = _sigmoid(_dot(h.astype(bf16), pg_ref[...]))
    h_ref[...] = h + _dot(p_ref[...].astype(bf16), pp_ref[...]) * gate


def _finish(x, oa, z, yb, p, w_out, g_post, ple_proj, ple_gate, tm):
    n = x.shape[0]
    za_blk = COL_ZA // NSA_WIDTH
    row = lambda i: (i, 0)
    const = lambda i: (0, 0)
    return pl.pallas_call(
        _finish_kernel,
        grid=(n // tm,),
        in_specs=[pl.BlockSpec((tm, D_MODEL), row),
                  pl.BlockSpec((tm, NSA_WIDTH), row),
                  pl.BlockSpec((tm, NSA_WIDTH), lambda i: (i, za_blk)),
                  pl.BlockSpec((tm, HG_WIDTH), row),
                  pl.BlockSpec((tm, PLE_DIM), row),
                  pl.BlockSpec((D_MODEL, D_MODEL), const),
                  pl.BlockSpec((1, D_MODEL), const),
                  pl.BlockSpec((PLE_DIM, D_MODEL), const),
                  pl.BlockSpec((D_MODEL, D_MODEL), const)],
        out_specs=pl.BlockSpec((tm, D_MODEL), row),
        out_shape=jax.ShapeDtypeStruct((n, D_MODEL), f32),
        compiler_params=_cparams(("parallel",)),
        name="finish",
    )(x, oa, z, yb, p, w_out, g_post, ple_proj, ple_gate)


SEGS_PER_PAGE = PAGE_SIZE // CMP_STRIDE


def _compress_kernel(pt_ref, pages_ref, wa_ref, wb_ref, pea_ref, peb_ref, b1_ref, w2_ref, out_ref,
                     stage_ref, rows_ref, sem_ref, *, n_pages, feature_major):
    b = pl.program_id(0)
    nb = pl.num_programs(0)
    lane_halves = ROW_WIDTH // LANES
    n_seg = n_pages * SEGS_PER_PAGE

    def page_copy(seq, j):
        s = lax.rem(seq, 2)
        return pltpu.make_async_copy(pages_ref.at[pt_ref[seq, j]], stage_ref.at[s, j], sem_ref.at[s])

    def start_all(seq):
        def body(j, c):
            page_copy(seq, j).start()
            return c
        lax.fori_loop(0, n_pages, body, 0, unroll=8)

    def wait_all(seq):
        def body(j, c):
            page_copy(seq, j).wait()
            return c
        lax.fori_loop(0, n_pages, body, 0, unroll=8)

    def to_rows(s):
        for j in range(n_pages):
            tile = stage_ref[s, j]
            rows = tile.T if feature_major else tile
            for h in range(lane_halves):
                rows_ref[s, h, j * PAGE_SIZE:(j + 1) * PAGE_SIZE, :] = rows[:, h * LANES:(h + 1) * LANES]

    def mlp(s):
        for kind in range(lane_halves):
            cols = slice(kind * LANES, (kind + 1) * LANES)
            seg_a, seg_b = [], []
            for j in range(CMP_STRIDE):
                r = rows_ref[s, kind, pl.ds(j, n_seg, stride=CMP_STRIDE), :]
                seg_a.append((r + pea_ref[j][:, cols]).astype(bf16))
                seg_b.append((r + peb_ref[j][:, cols]).astype(bf16))
            pa = _dot(jnp.concatenate(seg_a, axis=1), wa_ref[kind])
            pb = _dot(jnp.concatenate(seg_b, axis=1), wb_ref[kind])
            pb = pltpu.roll(pb, n_seg - 1, axis=0)
            hid = _silu(pa + pb + b1_ref[kind])
            for g in range(NSA_KV_HEADS):
                hg = hid[:, g * CMP_HIDDEN:(g + 1) * CMP_HIDDEN].astype(bf16)
                out_ref[0, kind * NSA_KV_HEADS + g] = _dot(hg, w2_ref[kind])

    @pl.when(b == 0)
    def _():
        start_all(0)

        @pl.when(nb > 1)
        def _():
            start_all(1)
        wait_all(0)
        to_rows(0)

    @pl.when(b + 2 < nb)
    def _():
        start_all(b + 2)

    @pl.when(b + 1 < nb)
    def _():
        wait_all(b + 1)

    for cur in range(2):
        @pl.when(lax.rem(b, 2) == cur)
        def _(cur=cur):
            to_rows(1 - cur)
            mlp(cur)


def _compress(pages, page_table, wa, wb, pea, peb, b1, w2, feature_major):
    nseq, n_pages = page_table.shape
    n_seg = n_pages * SEGS_PER_PAGE
    const3 = lambda b, pt: (0, 0, 0)
    once = pl.Buffered(1)
    grid_spec = pltpu.PrefetchScalarGridSpec(
        num_scalar_prefetch=1,
        grid=(nseq,),
        in_specs=[pl.BlockSpec(memory_space=pl.ANY),
                  pl.BlockSpec(wa.shape, const3, pipeline_mode=once),
                  pl.BlockSpec(wb.shape, const3, pipeline_mode=once),
                  pl.BlockSpec((CMP_STRIDE, 1, ROW_WIDTH), const3),
                  pl.BlockSpec((CMP_STRIDE, 1, ROW_WIDTH), const3),
                  pl.BlockSpec(b1.shape, const3),
                  pl.BlockSpec((2, CMP_HIDDEN, HEAD_DIM), const3)],
        out_specs=pl.BlockSpec((1, N_COMBO, n_seg, HEAD_DIM), lambda b, pt: (b, 0, 0, 0)),
        scratch_shapes=[pltpu.VMEM((2, n_pages) + pages.shape[1:], f32),
                        pltpu.VMEM((2, ROW_WIDTH // LANES, n_pages * PAGE_SIZE, LANES), f32),
                        pltpu.SemaphoreType.DMA((2,))],
    )
    return pl.pallas_call(
        functools.partial(_compress_kernel, n_pages=n_pages, feature_major=feature_major),
        grid_spec=grid_spec,
        out_shape=jax.ShapeDtypeStruct((nseq, N_COMBO, n_seg, HEAD_DIM), f32),
        compiler_params=_cparams(("arbitrary",)),
        name="compress",
    )(page_table, pages, wa, wb, pea, peb, b1, w2)


def _compress_weights(cmp_pe, cmp_w1, cmp_b1, cmp_w2):
    eye = jnp.eye(NSA_KV_HEADS, dtype=f32)

    def big(w_half):
        t = jnp.einsum('kjdh,gb->kjbdgh', w_half, eye)
        return t.reshape(2, CMP_STRIDE * KV_WIDTH, NSA_KV_HEADS * CMP_HIDDEN)

    w1 = cmp_w1.reshape(2, 2, CMP_STRIDE, HEAD_DIM, CMP_HIDDEN)
    wa = big(w1[:, 0]).astype(bf16)
    wb = big(w1[:, 1]).astype(bf16)
    pe = cmp_pe.reshape(2, 2, CMP_STRIDE, HEAD_DIM)

    def pe_row(p):
        t = jnp.broadcast_to(p.transpose(1, 0, 2)[:, :, None, :],
                             (CMP_STRIDE, 2, NSA_KV_HEADS, HEAD_DIM))
        return t.reshape(CMP_STRIDE, 1, ROW_WIDTH)

    pea, peb = pe_row(pe[:, 0]), pe_row(pe[:, 1])
    b1 = jnp.broadcast_to(cmp_b1[:, None, :], (2, NSA_KV_HEADS, CMP_HIDDEN)).reshape(2, 1, NSA_KV_HEADS * CMP_HIDDEN)
    return wa, wb, pea, peb, b1, cmp_w2.astype(bf16)


def _sel_map_matrix():
    wts = np.convolve(np.ones(SEL_RATIO), np.ones(CMP_RATIO))
    m = np.zeros((N_CMP_PAD, N_SELBLK), np.float32)
    for j in range(N_SELBLK):
        for k, wk in enumerate(wts):
            n = SEL_RATIO * j + k
            if n < N_CMP_PAD - 1:
                m[n, j] = wk
    return jnp.asarray(m, dtype=bf16)


def _importance_to_blocks(imp, m):
    hi = imp.astype(bf16)
    lo = (imp - hi.astype(f32)).astype(bf16)
    return _dot(hi, m) + _dot(lo, m)


LOG2E = 1.4426950408889634
MASK_BIG = 1e30
AUG_ROWS = 16
VT_ROWS = HEAD_DIM + 16
KEY_LOW = 256
N_FORCED = 3


NSA_STEP_BLOCKS = 2


def _nsa_prompt_blocks_kernel(qt_ref, gate_ref, comp_ref, caug_ref, ks_ref, vts_ref, kw_ref, vtw_ref, mt_ref, o_ref,
                              qa_ref, acc_ref, mg_ref, bias_ref, need_ref):
    cols = NSA_GROUP * Q_BLOCK
    n_blocks = qt_ref.shape[2]
    blocks = range(n_blocks)
    heads = range(NSA_KV_HEADS)
    insts = [(qb, g) for qb in blocks for g in heads]
    blk_per_chunk = SEL_CHUNK // SEL_BLOCK
    n_chunk_max = vts_ref.shape[2]

    lane = lax.broadcasted_iota(jnp.int32, (1, cols), 1)
    head = jnp.right_shift(lane, Q_BLOCK.bit_length() - 1)
    blk = lax.broadcasted_iota(jnp.int32, (N_SELBLK, Q_BLOCK), 0)
    qcol = lax.broadcasted_iota(jnp.int32, (N_SELBLK, Q_BLOCK), 1)
    c_end = lax.broadcasted_iota(jnp.int32, (N_CMP_PAD, 1), 0) * CMP_STRIDE + (CMP_BLOCK - 1)
    krow = lax.broadcasted_iota(jnp.int32, (SEL_CHUNK, 1), 0)
    wrow = lax.broadcasted_iota(jnp.int32, (WIN_KEYS, 1), 0)
    arow = lax.broadcasted_iota(jnp.int32, (AUG_ROWS // 2, cols), 0)
    gate_all = _sigmoid(gate_ref[...])

    q0 = [(pl.program_id(1) * n_blocks + qb) * Q_BLOCK for qb in blocks]
    tq = [q0[qb] + jnp.bitwise_and(lane, Q_BLOCK - 1) for qb in blocks]
    last = [q0[qb] // SEL_CHUNK for qb in blocks]
    w0 = [pl.multiple_of(jnp.maximum(q0[qb] - WINDOW, 0), Q_BLOCK) for qb in blocks]
    gate_t = [gate_all[qb * Q_BLOCK:(qb + 1) * Q_BLOCK].T for qb in blocks]

    slope, alibi_rows, pos_rows = [], [], []
    for g in heads:
        c = jnp.exp2(-(head.astype(f32) + 1.0 + NSA_GROUP * g)) * LOG2E
        c_hi = c.astype(bf16).astype(f32)
        c_mid = (c - c_hi).astype(bf16).astype(f32)
        c_lo = (c - c_hi - c_mid).astype(bf16).astype(f32)
        c3 = jnp.where((arow == 0) | (arow == 3), c_hi, jnp.where((arow == 1) | (arow == 4), c_mid, c_lo))
        slope.append(c)
        alibi_rows.append(jnp.where(arow >= 6, 0.0, c3))
        rows = jnp.where(arow >= 6, 0.0, jnp.where(arow >= 3, KEY_LOW * c3, c3))
        pos_rows.append(jnp.concatenate([rows, jnp.zeros_like(rows)], axis=0).astype(bf16))

    def chunk_scores(qb, g, ck, slot=0, live=None):
        k0 = pl.multiple_of(ck * SEL_CHUNK, SEL_CHUNK)
        b8 = bias_ref[qb, g, pl.ds(pl.multiple_of(ck * blk_per_chunk, blk_per_chunk), blk_per_chunk), :]
        if live is not None:
            b8 = jnp.where(live, b8, -MASK_BIG)
        qa_ref[qb, g, slot, HEAD_DIM:HEAD_DIM + AUG_ROWS, :] = jnp.concatenate(
            [jnp.concatenate([b8] * NSA_GROUP, axis=1), alibi_rows[g]], axis=0).astype(bf16)
        return _dot(ks_ref[0, g, pl.ds(k0, SEL_CHUNK), :], qa_ref[qb, g, slot])

    def sel_chunk(qb, g, ck, causal, s):
        k0 = pl.multiple_of(ck * SEL_CHUNK, SEL_CHUNK)
        if causal:
            s = jnp.where(k0 + krow <= tq[qb], s, NEG)
        shift = slope[g] * k0.astype(f32)
        m_old = mg_ref[qb, g] - shift
        m_new = jnp.maximum(m_old, jnp.max(s, axis=0, keepdims=True))
        p = jnp.exp2(s - m_new).astype(bf16)
        acc_ref[qb, g] = acc_ref[qb, g] * jnp.exp2(m_old - m_new) + _dot(vts_ref[0, g, ck], p)
        mg_ref[qb, g] = m_new + shift

    s_cmp, s_win = {}, {}
    for qb, g in insts:
        q_pos = jnp.concatenate([qt_ref[0, g, qb], pos_rows[g],
                                 jnp.zeros((LANES - HEAD_DIM - AUG_ROWS, cols), bf16)], axis=0)
        kc = jnp.concatenate([comp_ref[0, g].astype(bf16), caug_ref[...]], axis=1)
        s_cmp[qb, g] = _dot(kc, q_pos)
        kw = kw_ref[0, g, pl.ds(w0[qb], WIN_KEYS), :]
        s_win[qb, g] = _dot(kw, q_pos)

    oc_t, ps_t = {}, {}
    for qb in blocks:
        seen = c_end <= tq[qb]
        any_seen = tq[qb] >= CMP_BLOCK - 1
        for g in heads:
            s_c = jnp.where(seen, s_cmp[qb, g], NEG)
            e = jnp.exp2(s_c - jnp.max(s_c, axis=0, keepdims=True))
            rl = jnp.where(any_seen, 1.0 / jnp.maximum(jnp.sum(e, axis=0, keepdims=True), 1e-30), 0.0)
            p_c = e * rl
            oc_t[qb, g] = _dot_tn(comp_ref[0, NSA_KV_HEADS + g].astype(bf16), p_c.astype(bf16))
            imp_t = (p_c[:, 0:Q_BLOCK] + p_c[:, Q_BLOCK:2 * Q_BLOCK]
                     + p_c[:, 2 * Q_BLOCK:3 * Q_BLOCK] + p_c[:, 3 * Q_BLOCK:4 * Q_BLOCK])
            hi = imp_t.astype(bf16)
            lo = (imp_t - hi.astype(f32)).astype(bf16)
            ps_t[qb, g] = _dot(mt_ref[...], hi) + _dot(mt_ref[...], lo)

    ow_t = {}
    for qb in blocks:
        wkey = w0[qb] + wrow
        in_window = (wkey <= tq[qb]) & (wkey >= tq[qb] - WINDOW)
        t0 = w0[qb] // Q_BLOCK
        for g in heads:
            s_w = jnp.where(in_window, s_win[qb, g], NEG)
            e = jnp.exp2(s_w - jnp.max(s_w, axis=0, keepdims=True)).astype(bf16)
            acc_w = jnp.zeros((VT_ROWS, cols), f32)
            for t in range(WIN_KEYS // Q_BLOCK):
                acc_w = acc_w + _dot(vtw_ref[0, g, t0 + t], e[t * Q_BLOCK:(t + 1) * Q_BLOCK])
            ow_t[qb, g] = acc_w[0:HEAD_DIM] * (1.0 / jnp.maximum(acc_w[HEAD_DIM:HEAD_DIM + 1], 1e-30))

    valid, sel_t, score = {}, {}, {}
    for qb in blocks:
        jt = jnp.right_shift(q0[qb] + qcol, SEL_BLOCK.bit_length() - 1)
        valid[qb] = blk <= jt
        forced = (blk == 0) | (blk == jt) | (blk == jt - 1)
        for g in heads:
            sel_t[qb, g] = jnp.where(forced, 1.0, 0.0)
            score[qb, g] = jnp.where(valid[qb] & jnp.logical_not(forced), ps_t[qb, g], NEG)
    for _ in range(N_SEL - N_FORCED):
        for inst in insts:
            best = jnp.max(score[inst], axis=0, keepdims=True)
            idx = jnp.min(jnp.where(score[inst] == best, blk, N_SELBLK), axis=0, keepdims=True)
            hit = blk == idx
            sel_t[inst] = jnp.where(hit, 1.0, sel_t[inst])
            score[inst] = jnp.where(hit, REMOVED, score[inst])
    for qb, g in insts:
        sel_t[qb, g] = jnp.where(valid[qb], sel_t[qb, g], 0.0)

    for qb, g in insts:
        bias_ref[qb, g] = (sel_t[qb, g] - 1.0) * MASK_BIG
        for slot in range(qa_ref.shape[2]):
            qa_ref[qb, g, slot, 0:HEAD_DIM, :] = qt_ref[0, g, qb]
            qa_ref[qb, g, slot, HEAD_DIM + AUG_ROWS:, :] = jnp.zeros((LANES - HEAD_DIM - AUG_ROWS, cols), bf16)
        acc_ref[qb, g] = jnp.zeros(acc_ref.shape[2:], f32)
        mg_ref[qb, g] = jnp.full(mg_ref.shape[2:], NEG, f32)
    own = {(qb, g): chunk_scores(qb, g, last[qb]) for qb, g in insts}
    for qb, g in insts:
        sel_chunk(qb, g, last[qb], True, own[qb, g])

    n_need = []
    for qb in blocks:
        count = jnp.int32(0)
        for ck in range(n_chunk_max):
            rows = slice(ck * blk_per_chunk, (ck + 1) * blk_per_chunk)
            picked = jnp.maximum(*[jnp.max(sel_t[qb, g][rows, :]) for g in heads])
            need_ref[qb, count] = ck
            count = count + ((picked > 0.5) & (ck < last[qb])).astype(jnp.int32)
        n_need.append(count)

    for qb in blocks:
        def chunk_pair(i, carry, qb=qb):
            first = 2 * i
            live = first + 1 < n_need[qb]
            cks = [need_ref[qb, first], need_ref[qb, jnp.minimum(first + 1, n_need[qb] - 1)]]
            scores = [[chunk_scores(qb, g, cks[slot], slot, live if slot else None) for g in heads]
                      for slot in range(2)]
            for slot in range(2):
                for g in heads:
                    sel_chunk(qb, g, cks[slot], False, scores[slot][g])
            return carry

        lax.fori_loop(0, (n_need[qb] + 1) // 2, chunk_pair, 0)

    for qb, g in insts:
        acc = acc_ref[qb, g]
        os_t = acc[0:HEAD_DIM] * (1.0 / jnp.maximum(acc[HEAD_DIM:HEAD_DIM + 1], 1e-30))
        outs = []
        for r in range(NSA_GROUP):
            cs = slice(r * Q_BLOCK, (r + 1) * Q_BLOCK)
            row = (g * NSA_GROUP + r) * 3
            outs.append(gate_t[qb][row:row + 1, :] * oc_t[qb, g][:, cs] + gate_t[qb][row + 1:row + 2, :] * os_t[:, cs]
                        + gate_t[qb][row + 2:row + 3, :] * ow_t[qb, g][:, cs])
        for pair in range(NSA_GROUP // 2):
            both = jnp.concatenate([outs[2 * pair], outs[2 * pair + 1]], axis=0)
            lane0 = (g * NSA_GROUP // 2 + pair) * LANES
            o_ref[qb * Q_BLOCK:(qb + 1) * Q_BLOCK, lane0:lane0 + LANES] = both.T


def _nsa_prompt_blocks(q_t, gates, comp, k_slc, vt_slc, k_win, vt_win, m_t):
    bsz, _, nq = q_t.shape[:3]
    seq = k_slc.shape[2]
    cols = NSA_GROUP * Q_BLOCK
    nblk = NSA_STEP_BLOCKS
    steps = nq // nblk
    per_b4 = lambda b, i: (b, 0, 0, 0)
    per_b5 = lambda b, i: (b, 0, 0, 0, 0)
    inst = (nblk, NSA_KV_HEADS)
    return pl.pallas_call(
        _nsa_prompt_blocks_kernel,
        grid=(bsz, steps),
        in_specs=[pl.BlockSpec((1, NSA_KV_HEADS, nblk, HEAD_DIM, cols), lambda b, i: (b, 0, i, 0, 0)),
                  pl.BlockSpec((nblk * Q_BLOCK, LANES), lambda b, i: (b * steps + i, 0)),
                  pl.BlockSpec((1, N_COMBO, N_CMP_PAD, HEAD_DIM), per_b4),
                  pl.BlockSpec((N_CMP_PAD, HEAD_DIM), lambda b, i: (0, 0)),
                  pl.BlockSpec((1, NSA_KV_HEADS, seq, LANES), per_b4),
                  pl.BlockSpec((1,) + vt_slc.shape[1:], per_b5),
                  pl.BlockSpec((1, NSA_KV_HEADS, seq, LANES), per_b4),
                  pl.BlockSpec((1,) + vt_win.shape[1:], per_b5),
                  pl.BlockSpec((N_SELBLK, N_CMP_PAD), lambda b, i: (0, 0))],
        out_specs=pl.BlockSpec((nblk * Q_BLOCK, NSA_WIDTH), lambda b, i: (b * steps + i, 0)),
        out_shape=jax.ShapeDtypeStruct((bsz * seq, NSA_WIDTH), f32),
        scratch_shapes=[pltpu.VMEM(inst + (2, LANES, cols), bf16),
                        pltpu.VMEM(inst + (VT_ROWS, cols), f32),
                        pltpu.VMEM(inst + (1, cols), f32),
                        pltpu.VMEM(inst + (N_SELBLK, Q_BLOCK), f32),
                        pltpu.SMEM((nblk, vt_slc.shape[2]), jnp.int32)],
        compiler_params=_cparams(("parallel", "arbitrary")),
        name="nsa_prompt",
    )(q_t, gates, comp, _cmp_position_columns(), k_slc, vt_slc, k_win, vt_win, m_t)


def _position_columns(pos):
    low, high = (pos % KEY_LOW)[:, None], (pos // KEY_LOW)[:, None]
    return np.concatenate([low, low, low, high, high, high], axis=1).astype(np.float32)


def _cmp_position_columns():
    aug = np.zeros((N_CMP_PAD, HEAD_DIM), np.float32)
    aug[:, 0:6] = _position_columns(np.arange(N_CMP_PAD) * CMP_STRIDE + CMP_BLOCK - 1)
    return jnp.asarray(aug, dtype=bf16)


LAYOUT_ROWS = 1024


def _attention_layout_kernel(q_ref, cmp_ref, slc_ref, win_ref, xs_ref, xw_ref,
                             qt_ref, ks_ref, vts_ref, kw_ref, vtw_ref, cmp_t_ref, slc_t_ref, win_t_ref):
    cmp_t_ref[0] = cmp_ref[...].T
    tm = q_ref.shape[0]
    q = q_ref[...] * (HEAD_DIM ** -0.5 * LOG2E)
    for qb in range(tm // Q_BLOCK):
        rows = slice(qb * Q_BLOCK, (qb + 1) * Q_BLOCK)
        for pair in range(NSA_HEADS // 2):
            t = q[rows, pair * LANES:(pair + 1) * LANES].T
            for k in range(2):
                g, r = divmod(2 * pair + k, NSA_GROUP)
                qt_ref[0, g, qb, :, r * Q_BLOCK:(r + 1) * Q_BLOCK] = t[k * HEAD_DIM:(k + 1) * HEAD_DIM].astype(bf16)

    for src_ref, extra_ref, k_ref, vt_ref, t_ref in ((slc_ref, xs_ref, ks_ref, vts_ref, slc_t_ref),
                                                     (win_ref, xw_ref, kw_ref, vtw_ref, win_t_ref)):
        tile = vt_ref.shape[4]
        s = src_ref[...]
        s_t = s.T
        t_ref[0] = s_t
        v_t = s_t[KV_WIDTH:]
        tail = jnp.where(lax.broadcasted_iota(jnp.int32, (VT_ROWS - HEAD_DIM, tile), 0) == 0, 1.0, 0.0)
        for g in range(NSA_KV_HEADS):
            k_ref[0, g] = jnp.concatenate([s[:, g * HEAD_DIM:(g + 1) * HEAD_DIM].astype(bf16), extra_ref[...]], axis=1)
            for c in range(tm // tile):
                blk = v_t[g * HEAD_DIM:(g + 1) * HEAD_DIM, c * tile:(c + 1) * tile]
                vt_ref[0, g, c] = jnp.concatenate([blk, tail], axis=0).astype(bf16)


def _attention_layout(z, cmp, slc, win, bsz, seq):
    tm = LAYOUT_ROWS
    per_b = seq // tm
    pos = np.arange(seq)
    xs = np.zeros((seq, HEAD_DIM), np.float32)
    xs[pos, (pos // SEL_BLOCK) % (SEL_CHUNK // SEL_BLOCK)] = 1.0
    xs[:, 8:11] = (pos % KEY_LOW)[:, None]
    xs[:, 11:14] = (pos % SEL_CHUNK - pos % KEY_LOW)[:, None]
    xw = np.zeros((seq, HEAD_DIM), np.float32)
    xw[:, 0:6] = _position_columns(pos)
    row = lambda i: (i, 0)
    rep = lambda i: (i % per_b, 0)
    out4 = lambda i: (i // per_b, 0, i % per_b, 0)
    out5 = lambda i: (i // per_b, 0, i % per_b, 0, 0)
    k_shape = jax.ShapeDtypeStruct((bsz, NSA_KV_HEADS, seq, LANES), bf16)
    vt_shape = lambda tile: jax.ShapeDtypeStruct((bsz, NSA_KV_HEADS, seq // tile, VT_ROWS, tile), bf16)
    vt_spec = lambda tile: pl.BlockSpec((1, NSA_KV_HEADS, tm // tile, VT_ROWS, tile), out5)
    rows_t_shape = jax.ShapeDtypeStruct((bsz, ROW_WIDTH, seq), f32)
    rows_t_spec = pl.BlockSpec((1, ROW_WIDTH, tm), lambda i: (i // per_b, 0, i % per_b))
    return pl.pallas_call(
        _attention_layout_kernel,
        grid=(bsz * per_b,),
        in_specs=[pl.BlockSpec((tm, NSA_WIDTH), row), pl.BlockSpec((tm, ROW_WIDTH), row),
                  pl.BlockSpec((tm, ROW_WIDTH), row), pl.BlockSpec((tm, ROW_WIDTH), row),
                  pl.BlockSpec((tm, HEAD_DIM), rep), pl.BlockSpec((tm, HEAD_DIM), rep)],
        out_specs=[pl.BlockSpec((1, NSA_KV_HEADS, tm // Q_BLOCK, HEAD_DIM, NSA_GROUP * Q_BLOCK), out5),
                   pl.BlockSpec((1, NSA_KV_HEADS, tm, LANES), out4), vt_spec(SEL_CHUNK),
                   pl.BlockSpec((1, NSA_KV_HEADS, tm, LANES), out4), vt_spec(Q_BLOCK),
                   rows_t_spec, rows_t_spec, rows_t_spec],
        out_shape=[jax.ShapeDtypeStruct((bsz, NSA_KV_HEADS, seq // Q_BLOCK, HEAD_DIM, NSA_GROUP * Q_BLOCK), bf16),
                   k_shape, vt_shape(SEL_CHUNK), k_shape, vt_shape(Q_BLOCK),
                   rows_t_shape, rows_t_shape, rows_t_shape],
        compiler_params=_cparams(("parallel",)),
        name="attention_layout",
    )(z, cmp, slc, win, jnp.asarray(xs, dtype=bf16), jnp.asarray(xw, dtype=bf16))


HG_CHUNK = 128
HG_STEP_CHUNKS = 4
HG_BAND = 8


def _lower_bound(lo):
    m = jnp.max(lo, axis=0, keepdims=True)
    e = jnp.exp(lo - m)
    return e[0:1] / jnp.sum(e, axis=0, keepdims=True)


def _hgrn_gates(q, fl, lb):
    sg = _sigmoid(fl)
    f = lb + (1.0 - lb) * sg
    k = (1.0 - lb) * (1.0 - sg)
    return _silu(q), k, jnp.log(f)


def _hgrn_out(o, nw, zb):
    o = o * lax.rsqrt(jnp.mean(o * o, axis=-1, keepdims=True) + EPS) * nw
    return o * _silu(zb)


def _hgrn_prompt_kernel(q_ref, f_ref, i_ref, zb_ref, lo_ref, nw_ref, y_ref, s_ref, st_ref):
    c = pl.program_id(1)

    @pl.when(c == 0)
    def _():
        st_ref[...] = jnp.zeros_like(st_ref)

    n = HG_CHUNK
    row = lax.broadcasted_iota(jnp.int32, (n, n), 0)
    col = lax.broadcasted_iota(jnp.int32, (n, n), 1)
    heads = [slice(h * HG_DIM, (h + 1) * HG_DIM) for h in range(HG_HEADS)]

    qq, kk, logf = _hgrn_gates(q_ref[...], f_ref[...], _lower_bound(lo_ref[...]))
    v = i_ref[...]
    chunks = [slice(i * n, (i + 1) * n) for i in range(q_ref.shape[0] // n)]
    tril = jnp.where(col <= row, 1.0, 0.0).astype(bf16)
    hi = logf.astype(bf16)
    r1 = logf - hi.astype(f32)
    mid = r1.astype(bf16)
    lo = (r1 - mid.astype(f32)).astype(bf16)
    bs = [_dot(tril, hi[rs]) + _dot(tril, mid[rs]) + _dot(tril, lo[rs]) for rs in chunks]

    intra = [[_hgrn_intra(qq[rs, hs], kk[rs, hs], b[:, hs], row, col) for hs in heads] for rs, b in zip(chunks, bs)]
    for i, rs in enumerate(chunks):
        for h, hs in enumerate(heads):
            st = st_ref[h]
            bh, kh, vh = bs[i][:, hs], kk[rs, hs], v[rs, hs].astype(bf16)
            o = _dot(intra[i][h].astype(bf16), vh) + _dot_nt((qq[rs, hs] * jnp.exp(bh)).astype(bf16), st.astype(bf16))
            b_last = bh[n - 1:n]
            st_new = st * jnp.exp(b_last) + _dot_tn(vh, (kh * jnp.exp(b_last - bh)).astype(bf16))
            st_ref[h] = st_new
            y_ref[rs, hs] = _hgrn_out(o, nw_ref[...], zb_ref[rs, hs])

    @pl.when(c == pl.num_programs(1) - 1)
    def _():
        for h in range(HG_HEADS):
            s_ref[0, h] = st_ref[h].T


def _hgrn_intra(qq, kk, b, row, col):
    n = HG_CHUNK
    a = jnp.where(row == col, _dot_nt(qq.astype(bf16), kk.astype(bf16)), 0.0)

    sub = jnp.bitwise_and(row, HG_BAND - 1)
    b3 = b.reshape(n // HG_BAND, HG_BAND, HG_DIM)
    size = 2
    while size <= HG_BAND:
        half = size // 2
        beta = None
        for s0 in range(0, HG_BAND, size):
            piece = jnp.broadcast_to(b3[:, s0 + half - 1:s0 + half, :], b3.shape).reshape(n, HG_DIM)
            beta = piece if beta is None else jnp.where(sub >= s0, piece, beta)
        left = jnp.bitwise_and(sub, size - 1) < half
        decay = jnp.exp(jnp.where(left, beta - b, b - beta))
        part = _dot_nt(jnp.where(left, 0.0, qq * decay).astype(bf16), jnp.where(left, kk * decay, 0.0).astype(bf16))
        sh = size.bit_length() - 1
        a = a + jnp.where(jnp.right_shift(row, sh) == jnp.right_shift(col, sh), part, 0.0)
        size *= 2

    while size <= n:
        half = size // 2
        qs, ks = [], []
        zero = jnp.zeros((half, HG_DIM), f32)
        for r0 in range(0, n, size):
            beta = b[r0 + half - 1:r0 + half]
            ks += [kk[r0:r0 + half] * jnp.exp(beta - b[r0:r0 + half]), zero]
            qs += [zero, qq[r0 + half:r0 + size] * jnp.exp(b[r0 + half:r0 + size] - beta)]
        part = _dot_nt(jnp.concatenate(qs, axis=0).astype(bf16), jnp.concatenate(ks, axis=0).astype(bf16))
        if size < n:
            sh = size.bit_length() - 1
            part = jnp.where(jnp.right_shift(row, sh) == jnp.right_shift(col, sh), part, 0.0)
        a = a + part
        size *= 2
    return a


def _hgrn_prompt(z, hg_lower, hg_norm, bsz, seq):
    rows = HG_STEP_CHUNKS * HG_CHUNK
    nch = seq // rows

    def zcol(base):
        return pl.BlockSpec((rows, HG_WIDTH), lambda b, c: (b * nch + c, base // HG_WIDTH))

    return pl.pallas_call(
        _hgrn_prompt_kernel,
        grid=(bsz, nch),
        in_specs=[zcol(COL_QB), zcol(COL_FB), zcol(COL_IB), zcol(COL_ZB),
                  pl.BlockSpec(hg_lower.shape, lambda b, c: (0, 0)),
                  pl.BlockSpec((1, HG_DIM), lambda b, c: (0, 0))],
        out_specs=[pl.BlockSpec((rows, HG_WIDTH), lambda b, c: (b * nch + c, 0)),
                   pl.BlockSpec((1, HG_HEADS, HG_DIM, HG_DIM), lambda b, c: (b, 0, 0, 0))],
        out_shape=[jax.ShapeDtypeStruct((bsz * seq, HG_WIDTH), f32),
                   jax.ShapeDtypeStruct((bsz, HG_HEADS, HG_DIM, HG_DIM), f32)],
        scratch_shapes=[pltpu.VMEM((HG_HEADS, HG_DIM, HG_DIM), f32)],
        compiler_params=_cparams(("parallel", "arbitrary")),
        name="hgrn_prompt",
    )(z, z, z, z, hg_lower, hg_norm)


def _hgrn_sample_kernel(q_ref, f_ref, i_ref, zb_ref, lo_ref, nw_ref, s_ref, y_ref, so_ref, o_scr):
    nb = q_ref.shape[0]
    lb = _lower_bound(lo_ref[...])
    qq, kk, logf = _hgrn_gates(q_ref[...], f_ref[...], lb)
    v = i_ref[...]
    pad = jnp.zeros((HG_DIM - nb, HG_DIM), f32)
    to_cols = lambda x: jnp.concatenate([x, pad], axis=0).T
    f_t, k_t, q_t = to_cols(jnp.exp(logf)), to_cols(kk), to_cols(qq)
    for b in range(nb):
        s_new = f_t[:, b:b + 1] * s_ref[0, b, 0] + k_t[:, b:b + 1] * v[b:b + 1, :]
        so_ref[0, b, 0] = s_new
        o_scr[b:b + 1, :] = jnp.sum(s_new * q_t[:, b:b + 1], axis=0, keepdims=True)
    y_ref[...] = _hgrn_out(o_scr[...], nw_ref[...], zb_ref[...])


def _hgrn_sample(z, state, hg_lower, hg_norm):
    nb = z.shape[0]

    def zcol(base):
        return pl.BlockSpec((nb, HG_DIM), lambda h: (0, base // HG_DIM + h))

    sblk = pl.BlockSpec((1, nb, 1, HG_DIM, HG_DIM), lambda h: (0, 0, h, 0, 0))
    return pl.pallas_call(
        _hgrn_sample_kernel,
        grid=(HG_HEADS,),
        in_specs=[zcol(COL_QB), zcol(COL_FB), zcol(COL_IB), zcol(COL_ZB),
                  pl.BlockSpec((hg_lower.shape[0], HG_DIM), lambda h: (0, h)),
                  pl.BlockSpec((1, HG_DIM), lambda h: (0, 0)),
                  sblk],
        out_specs=[pl.BlockSpec((nb, HG_DIM), lambda h: (0, h)), sblk],
        out_shape=[jax.ShapeDtypeStruct((nb, HG_WIDTH), f32),
                   jax.ShapeDtypeStruct(state.shape, f32)],
        scratch_shapes=[pltpu.VMEM((nb, HG_DIM), f32)],
        compiler_params=_cparams(("parallel",)),
        name="hgrn_sample",
    )(z, z, z, z, hg_lower, hg_norm, state)


N_PICK = N_SEL - 1


def _head_slopes():
    h = lax.broadcasted_iota(jnp.int32, (NSA_HEADS, 1), 0).astype(f32)
    return jnp.exp2(-(h + 1.0))


SAMPLE_CMP_SEQS = 8


def _nsa_sample_cmp_kernel(q_ref, comp_ref, m_ref, oc_ref, ps_ref, *, t_pos):
    slope = _head_slopes()
    head = lax.broadcasted_iota(jnp.int32, (NSA_HEADS, 1), 0)
    ncol = lax.broadcasted_iota(jnp.int32, (1, N_CMP_PAD), 1)
    d_c = t_pos - (ncol * CMP_STRIDE + (CMP_BLOCK - 1))
    mask = d_c >= 0
    ps_ref[...] = jnp.zeros_like(ps_ref)
    for b in range(q_ref.shape[0]):
        q = q_ref[b] * jnp.asarray(HEAD_DIM ** -0.5, bf16)
        o_c = jnp.zeros((NSA_HEADS, HEAD_DIM), f32)
        for g in range(NSA_KV_HEADS):
            kc = comp_ref[b, g].astype(bf16)
            vc = comp_ref[b, NSA_KV_HEADS + g].astype(bf16)
            s = jnp.where(mask, _dot_nt(q, kc) - slope * d_c.astype(f32), NEG)
            m = jnp.max(s, axis=1, keepdims=True)
            e = jnp.where(mask, jnp.exp(s - m), 0.0)
            p = e * (1.0 / jnp.maximum(jnp.sum(e, axis=1, keepdims=True), 1e-30))
            mine = jnp.right_shift(head, NSA_GROUP.bit_length() - 1) == g
            o_c = jnp.where(mine, _dot(p.astype(bf16), vc), o_c)
            imp = jnp.sum(jnp.where(mine, p, 0.0), axis=0, keepdims=True)
            ps_ref[b, g:g + 1, :] = _importance_to_blocks(imp, m_ref[...])
        oc_ref[b] = o_c


def _nsa_sample_cmp(q8, comp, m_mat, t_pos):
    nb = q8.shape[0]
    ns = SAMPLE_CMP_SEQS
    return pl.pallas_call(
        functools.partial(_nsa_sample_cmp_kernel, t_pos=t_pos),
        grid=(nb // ns,),
        in_specs=[pl.BlockSpec((ns, NSA_HEADS, HEAD_DIM), lambda b: (b, 0, 0)),
                  pl.BlockSpec((ns, N_COMBO, N_CMP_PAD, HEAD_DIM), lambda b: (b, 0, 0, 0)),
                  pl.BlockSpec((N_CMP_PAD, N_SELBLK), lambda b: (0, 0))],
        out_specs=[pl.BlockSpec((ns, NSA_HEADS, HEAD_DIM), lambda b: (b, 0, 0)),
                   pl.BlockSpec((ns, 8, N_SELBLK), lambda b: (b, 0, 0))],
        out_shape=[jax.ShapeDtypeStruct((nb, NSA_HEADS, HEAD_DIM), f32),
                   jax.ShapeDtypeStruct((nb, 8, N_SELBLK), f32)],
        compiler_params=_cparams(("parallel",)),
        name="nsa_sample_cmp",
    )(q8, comp, m_mat)


def _nsa_sample_topk_kernel(ps_ref, idx_ref, *, last_blk):
    ps = ps_ref[...]
    lane = lax.broadcasted_iota(jnp.int32, ps.shape, 1)
    score = ps + jnp.where((lane == 0) | (lane == last_blk), FORCE_BONUS, 0.0)
    out = jnp.zeros(ps.shape, jnp.int32)
    for r in range(N_PICK):
        best = jnp.max(score, axis=1, keepdims=True)
        idx = jnp.min(jnp.where(score == best, lane, N_SELBLK), axis=1, keepdims=True)
        out = jnp.where(lane == r, idx, out)
        score = jnp.where(lane == idx, REMOVED, score)
    idx_ref[...] = out


def _nsa_sample_topk(ps_rows, last_blk):
    return pl.pallas_call(
        functools.partial(_nsa_sample_topk_kernel, last_blk=last_blk),
        out_shape=jax.ShapeDtypeStruct(ps_rows.shape, jnp.int32),
        name="nsa_sample_topk",
    )(ps_rows)


def _nsa_sample_sel_kernel(pt_ref, idx_ref, slc_ref, q_ref, oc_ref, gate_ref, slc_new_ref, win_new_ref, wc_ref,
                           o_ref, wout_ref, kv_scr, sem_ref, *, t_pos):
    n_blk = NSA_KV_HEADS * N_PICK
    halves = PAGE_SIZE // SEL_BLOCK
    b = pl.program_id(0)
    nb = pl.num_programs(0)
    slot = lax.rem(b, 2)

    def picked(seq, k):
        return idx_ref[(seq * NSA_KV_HEADS + k // N_PICK) * N_SEL + k % N_PICK]

    def blk_copies(seq, s, k):
        page = pt_ref[seq, picked(seq, k) // halves]
        g = k // N_PICK
        return [pltpu.make_async_copy(slc_ref.at[page, kind * NSA_KV_HEADS + g], kv_scr.at[s, kind, k],
                                      sem_ref.at[s]) for kind in range(2)]

    def start_all(seq, s):
        for k in range(n_blk):
            for kind, cp in enumerate(blk_copies(seq, s, k)):
                cp.start(priority=kind)

    @pl.when(b == 0)
    def _():
        start_all(0, 0)

    @pl.when(b + 1 < nb)
    def _():
        start_all(b + 1, 1 - slot)

    for k in range(n_blk):
        for cp in blk_copies(b, slot, k):
            cp.wait()

    q = q_ref[0] * jnp.asarray(HEAD_DIM ** -0.5, bf16)
    slope = _head_slopes()
    head = lax.broadcasted_iota(jnp.int32, (NSA_HEADS, 1), 0)
    lane = lax.broadcasted_iota(jnp.int32, (1, PAGE_SIZE), 1)
    new_slc = slc_new_ref[0]
    new_win = win_new_ref[0]
    wl = wc_ref.shape[3]
    wlane = lax.broadcasted_iota(jnp.int32, (1, wl), 1)
    d_w = (wl - wlane).astype(f32)
    qf = q.astype(f32)

    def attend(s, mask, v_t, new_row, g):
        k_new = new_row[:, g * HEAD_DIM:(g + 1) * HEAD_DIM].astype(bf16).astype(f32)
        v_new = new_row[:, KV_WIDTH + g * HEAD_DIM:KV_WIDTH + (g + 1) * HEAD_DIM].astype(bf16).astype(f32)
        s_new = jnp.sum(qf * k_new, axis=1, keepdims=True)
        m = jnp.maximum(jnp.max(s, axis=1, keepdims=True), s_new)
        e, e_new = jnp.exp(s - m), jnp.exp(s_new - m)
        if mask is not None:
            e = jnp.where(mask, e, 0.0)
        den = jnp.maximum(jnp.sum(e, axis=1, keepdims=True) + e_new, 1e-30)
        num = _dot_nt(e.astype(bf16), v_t.astype(bf16)) + e_new.astype(bf16).astype(f32) * v_new
        return num * (1.0 / den)

    o_s = jnp.zeros((NSA_HEADS, HEAD_DIM), f32)
    o_w = jnp.zeros((NSA_HEADS, HEAD_DIM), f32)
    for g in range(NSA_KV_HEADS):
        mine = jnp.right_shift(head, NSA_GROUP.bit_length() - 1) == g
        pos, member = [], []
        for n in range(N_PICK):
            j = picked(b, g * N_PICK + n)
            pos.append((j // halves) * PAGE_SIZE + lane)
            member.append(jnp.right_shift(lane, SEL_BLOCK.bit_length() - 1) == j % halves)
        pos = jnp.concatenate(pos, axis=1)
        member = jnp.concatenate(member, axis=1)
        k_t = jnp.concatenate([kv_scr[slot, 0, g * N_PICK + n] for n in range(N_PICK)], axis=1)
        v_t = jnp.concatenate([kv_scr[slot, 1, g * N_PICK + n] for n in range(N_PICK)], axis=1)
        s = jnp.where(member, _dot(q, k_t.astype(bf16)) - slope * (t_pos - pos).astype(f32), NEG)
        o_s = jnp.where(mine, attend(s, member, v_t, new_slc, g), o_s)
        s = _dot(q, wc_ref[0, g].astype(bf16)) - slope * d_w
        o_w = jnp.where(mine, attend(s, None, wc_ref[0, NSA_KV_HEADS + g], new_win, g), o_w)

    gate = jnp.broadcast_to(_sigmoid(gate_ref[0]), (NSA_HEADS, LANES))
    glane = lax.broadcasted_iota(jnp.int32, (NSA_HEADS, LANES), 1)
    gcol = [jnp.sum(jnp.where(glane == 3 * head + j, gate, 0.0), axis=1, keepdims=True) for j in range(3)]
    o_ref[0] = gcol[0] * oc_ref[0] + gcol[1] * o_s + gcol[2] * o_w
    eye = lax.broadcasted_iota(jnp.int32, (HEAD_DIM, HEAD_DIM), 0) == \
        lax.broadcasted_iota(jnp.int32, (HEAD_DIM, HEAD_DIM), 1)
    for c in range(N_COMBO):
        piece = jnp.broadcast_to(new_win[:, c * HEAD_DIM:(c + 1) * HEAD_DIM], (HEAD_DIM, HEAD_DIM))
        new_col = jnp.sum(jnp.where(eye, piece, 0.0), axis=1, keepdims=True)
        wout_ref[0, c] = jnp.where(wlane == wl - 1, new_col, pltpu.roll(wc_ref[0, c], wl - 1, axis=1))


def _nsa_sample_sel(page_table, idx_flat, slc_pages_t, q8, o_c, gates, slc_new, win_new, cache_win_t, t_pos):
    nb = q8.shape[0]
    wl = cache_win_t.shape[3]
    n_blk = NSA_KV_HEADS * N_PICK
    per_seq = lambda b, pt, idx: (b, 0, 0)
    win_blk = pl.BlockSpec((1, N_COMBO, HEAD_DIM, wl), lambda b, pt, idx: (b, 0, 0, 0))
    grid_spec = pltpu.PrefetchScalarGridSpec(
        num_scalar_prefetch=2,
        grid=(nb,),
        in_specs=[pl.BlockSpec(memory_space=pl.ANY),
                  pl.BlockSpec((1, NSA_HEADS, HEAD_DIM), per_seq),
                  pl.BlockSpec((1, NSA_HEADS, HEAD_DIM), per_seq),
                  pl.BlockSpec((1, 1, LANES), per_seq),
                  pl.BlockSpec((1, 1, ROW_WIDTH), per_seq),
                  pl.BlockSpec((1, 1, ROW_WIDTH), per_seq),
                  win_blk],
        out_specs=[pl.BlockSpec((1, NSA_HEADS, HEAD_DIM), per_seq), win_blk],
        scratch_shapes=[pltpu.VMEM((2, 2, n_blk, HEAD_DIM, PAGE_SIZE), f32),
                        pltpu.SemaphoreType.DMA((2,))],
    )
    return pl.pallas_call(
        functools.partial(_nsa_sample_sel_kernel, t_pos=t_pos),
        grid_spec=grid_spec,
        out_shape=[jax.ShapeDtypeStruct((nb, NSA_HEADS, HEAD_DIM), f32),
                   jax.ShapeDtypeStruct(cache_win_t.shape, f32)],
        compiler_params=_cparams(("arbitrary",)),
        name="nsa_sample_sel",
    )(page_table, idx_flat, slc_pages_t, q8, o_c, gates, slc_new, win_new, cache_win_t)


def kernel(x_prompt, x_sample, cache_cmp_kv, cache_slc_kv, cache_win_kv, state_hgrn, page_table, p_prompt, p_sample,
           w_in, g_pre, cmp_pe, cmp_w1, cmp_b1, cmp_w2, hg_lower, hg_norm, w_out, g_post, ple_proj, ple_gate):
    bsz, seq, _ = x_prompt.shape
    nb = x_sample.shape[0]
    n_pool = cache_cmp_kv.shape[1]
    n_pages = page_table.shape[1]
    past = n_pages * PAGE_SIZE
    kv_tail = (2, NSA_KV_HEADS, HEAD_DIM)

    o = _ORIG
    w = w_in[0]
    w_r = jnp.concatenate([w[:, o['q_a']:o['k_cmp']], w[:, o['z_a']:o['end']], w[:, o['k_cmp']:o['gate_a']],
                           w[:, o['gate_a']:o['z_a']],
                           jnp.zeros((D_MODEL, PROJ_WIDTH - o['end']), f32)], axis=1).astype(bf16)
    cw = _compress_weights(cmp_pe[0], cmp_w1[0], cmp_b1[0], cmp_w2[0])
    m_mat = _sel_map_matrix()
    w_out_b, ple_proj_b, ple_gate_b = w_out[0].astype(bf16), ple_proj[0].astype(bf16), ple_gate[0].astype(bf16)

    xp = x_prompt.reshape(bsz * seq, D_MODEL)
    zp, cmp_p, slc_p, win_p, gates_p = _proj(xp, g_pre, w_r, 1024)
    pages_p = cmp_p.reshape(bsz * seq // PAGE_SIZE, PAGE_SIZE, ROW_WIDTH)
    table_p = jnp.arange(bsz * seq // PAGE_SIZE, dtype=jnp.int32).reshape(bsz, seq // PAGE_SIZE)
    comp_p = _compress(pages_p, table_p, *cw, feature_major=False)
    q_t, k_slc, vt_slc, k_win, vt_win, cmp_tp, slc_tp, win_tp = _attention_layout(zp, cmp_p, slc_p, win_p, bsz, seq)
    o_a = _nsa_prompt_blocks(q_t, gates_p, comp_p, k_slc, vt_slc, k_win, vt_win, m_mat.T)
    yb_p, hg_p = _hgrn_prompt(zp, hg_lower, hg_norm, bsz, seq)
    h_p = _finish(xp, o_a, zp, yb_p, p_prompt[0].reshape(bsz * seq, PLE_DIM),
                  w_out_b, g_post, ple_proj_b, ple_gate_b, 1024)

    wl_p = min(WINDOW, seq)

    def cache_rows(rows_t):
        return rows_t.reshape((1, bsz) + kv_tail + rows_t.shape[2:]).transpose(0, 1, 5, 2, 3, 4)

    out_prompt = (h_p.reshape(bsz, seq, D_MODEL), cache_rows(cmp_tp), cache_rows(slc_tp),
                  cache_rows(win_tp[:, :, seq - wl_p:]), hg_p[None])

    xs = x_sample.reshape(nb, D_MODEL)
    zs, cmp_s, slc_s, win_new, gates_s = _proj(xs, g_pre, w_r, nb)
    cmp_t = cache_cmp_kv.transpose(0, 1, 3, 4, 5, 2).reshape(n_pool, ROW_WIDTH, PAGE_SIZE)
    comp_s = _compress(cmp_t, page_table, *cw, feature_major=True)
    q8 = zs[:, COL_QA:COL_QA + NSA_WIDTH].astype(bf16).reshape(nb, NSA_HEADS, HEAD_DIM)
    oc_s, ps_s = _nsa_sample_cmp(q8, comp_s, m_mat, past)
    ps_rows = ps_s[:, :NSA_KV_HEADS, :].reshape(nb * NSA_KV_HEADS, N_SELBLK)
    idx = _nsa_sample_topk(ps_rows, past // SEL_BLOCK - 1)
    slc_t = cache_slc_kv.transpose(0, 1, 3, 4, 5, 2).reshape(n_pool, N_COMBO, HEAD_DIM, PAGE_SIZE)
    wl_s = cache_win_kv.shape[2]
    win_t = cache_win_kv.transpose(0, 1, 3, 4, 5, 2).reshape(nb, N_COMBO, HEAD_DIM, wl_s)
    oa_s, win_s = _nsa_sample_sel(page_table, idx[:, :N_SEL].reshape(-1), slc_t, q8, oc_s,
                                  gates_s.reshape(nb, 1, LANES), slc_s.reshape(nb, 1, ROW_WIDTH),
                                  win_new.reshape(nb, 1, ROW_WIDTH), win_t, past)
    win_s = win_s.reshape((1, nb) + kv_tail + (wl_s,)).transpose(0, 1, 5, 2, 3, 4)
    yb_s, hg_s = _hgrn_sample(zs, state_hgrn, hg_lower, hg_norm)
    h_s = _finish(xs, oa_s.reshape(nb, NSA_WIDTH), zs, yb_s, p_sample[0].reshape(nb, PLE_DIM),
                  w_out_b, g_post, ple_proj_b, ple_gate_b, nb)

    out_sample = (h_s.reshape(nb, 1, D_MODEL),
                  cmp_s.reshape((1, nb, 1) + kv_tail),
                  slc_s.reshape((1, nb, 1) + kv_tail),
                  win_s,
                  hg_s)
    return (out_prompt[0], out_sample[0]) + out_prompt[1:] + out_sample[1:]
```
